```python
import math
import jax
import jax.numpy as jnp
from jax import lax
import numpy as np

D_MODEL = 1024
BATCH = 1
SEQ = 16384
DEPTH = 2

CTX_LEN = 256
GRID_W = 64
N_EVEN = (DEPTH + 1) // 2
N_ODD = DEPTH // 2
EPS = 1e-6
CONV_W = 3

SSD_HEADS = 8
SSD_HEAD_DIM = 64
SSD_INNER = SSD_HEADS * SSD_HEAD_DIM
SSD_GROUPS = 2
SSD_STATE = 64
SSD_CHUNK = 128
SSD_CONV_DIM = SSD_INNER + 2 * SSD_GROUPS * SSD_STATE

ATT_Q_HEADS = 8
ATT_KV_HEADS = 2
ATT_HEAD_DIM = 64
ATT_BLOCK = 128
WINDOW = 128
ROPE_THETA = 10000.0

ML_HEADS = 8
ML_QK_DIM = 64
ML_V_DIM = 128
ML_CHUNK = 128
ML_INNER = ML_HEADS * ML_V_DIM
ML_QKV = 2 * ML_HEADS * ML_QK_DIM + ML_INNER

N_EXPERTS = 16
N_GROUPS = 4
TOPK_GROUP = 1
TOP_K = 2
D_EXPERT = 512

EVEN_SIZES = (SSD_INNER, SSD_CONV_DIM, 2 * SSD_HEADS, ATT_Q_HEADS * ATT_HEAD_DIM, ATT_KV_HEADS * ATT_HEAD_DIM, ATT_KV_HEADS * ATT_HEAD_DIM)
EVEN_IN = sum(EVEN_SIZES)
EVEN_MIX = SSD_INNER + ATT_Q_HEADS * ATT_HEAD_DIM
ODD_SIZES = (ML_QKV, ML_INNER, 2 * ML_HEADS, 2 * ML_HEADS)
ODD_IN = sum(ODD_SIZES)

kernel_name = 'hybrid_ssd_swa_mlstm_moe_prefix_ctx'


def cuts(sizes):
    out, acc = [], 0
    for s in sizes[:-1]:
        acc += s
        out.append(acc)
    return out


def rmsnorm(x, g):
    xf = x.astype(jnp.float32)
    y = xf * lax.rsqrt(jnp.mean(xf * xf, axis=-1, keepdims=True) + EPS)
    return y.astype(x.dtype) * g


def adaln(cvec, w_mod, b_mod):
    m = (jax.nn.silu(cvec) @ w_mod + b_mod)[..., None, :]
    return jnp.split(m, 6, axis=-1)


def rev(t):
    return jnp.flip(t, axis=1)


def ident(t):
    return t


def dwconv_centred(x, w, b):
    L = x.shape[1]
    pad = CONV_W // 2
    xp = jnp.pad(x, ((0, 0), (pad, pad), (0, 0)))
    return sum(xp[:, j:j + L] * w[j] for j in range(CONV_W)) + b


def rope_1d(x, pos):
    n = x.shape[-1] // 2
    inv = ROPE_THETA ** (-jnp.arange(n, dtype=jnp.float32) / n)
    ang = pos.astype(jnp.float32)[:, None] * inv
    cos = jnp.cos(ang)[None, :, None, :]
    sin = jnp.sin(ang)[None, :, None, :]
    x1 = x[..., :n].astype(jnp.float32)
    x2 = x[..., n:].astype(jnp.float32)
    return jnp.concatenate([x1 * cos - x2 * sin, x1 * sin + x2 * cos], axis=-1)


def axial_rope(x, row, col):
    half = x.shape[-1] // 2
    return jnp.concatenate([rope_1d(x[..., :half], row), rope_1d(x[..., half:], col)], axis=-1).astype(x.dtype)


def segsum(a):
    T = a.shape[-1]
    strict = jnp.tril(jnp.ones((T, T), dtype=bool), -1)
    incl = jnp.tril(jnp.ones((T, T), dtype=bool))
    cs = jnp.cumsum(jnp.where(strict, a[..., :, None], 0.0), axis=-2)
    return jnp.where(incl, cs, -jnp.inf)


def ssd_scan(x, dt, A, Bm, Cm, h0):
    Bt, L, H, P = x.shape
    G, N = Bm.shape[2], Bm.shape[3]
    R = H // G
    T = SSD_CHUNK
    nc = L // T
    xc = x.reshape(Bt, nc, T, G, R, P)
    dtc = dt.reshape(Bt, nc, T, G, R)
    Bc = Bm.reshape(Bt, nc, T, G, N)
    Cc = Cm.reshape(Bt, nc, T, G, N)
    a = jnp.moveaxis(dtc * A.reshape(G, R), 2, -1)
    a_cs = jnp.cumsum(a, axis=-1)
    xdt = xc * dtc[..., None]
    Lm = jnp.exp(segsum(a))
    cb = jnp.einsum('bctgn,bcsgn->bcgts', Cc, Bc)
    y_diag = jnp.einsum('bcgts,bcgrts,bcsgrp->bctgrp', cb, Lm, xdt)
    decay_s = jnp.exp(a_cs[..., -1:] - a_cs)
    states = jnp.einsum('bcsgn,bcgrs,bcsgrp->bcgrpn', Bc, decay_s, xdt)
    states = jnp.concatenate([h0.reshape(Bt, 1, G, R, P, N).astype(states.dtype), states], axis=1)
    chunk_tot = jnp.pad(a_cs[..., -1], ((0, 0), (1, 0), (0, 0), (0, 0)))
    decay_chunk = jnp.exp(segsum(jnp.moveaxis(chunk_tot, 1, -1)))
    new_states = jnp.einsum('bgrzc,bcgrpn->bzgrpn', decay_chunk, states)
    states_in, final = new_states[:, :-1], new_states[:, -1]
    y_off = jnp.einsum('bctgn,bcgrpn,bcgrt->bctgrp', Cc, states_in, jnp.exp(a_cs))
    y = (y_diag + y_off).reshape(Bt, L, H, P)
    return y, final.reshape(Bt, H, P, N)


def mlstm_scan(q, k, v, ig, lf, state0):
    Bt, L, H, Dk = q.shape
    Dv = v.shape[-1]
    T = ML_CHUNK
    nc = L // T
    qc = q.reshape(Bt, nc, T, H, Dk)
    kc = k.reshape(Bt, nc, T, H, Dk)
    vc = v.reshape(Bt, nc, T, H, Dv)
    igc = jnp.moveaxis(ig.reshape(Bt, nc, T, H), 2, -1)
    b = jnp.cumsum(jnp.moveaxis(lf.reshape(Bt, nc, T, H), 2, -1), axis=-1)
    b_last = b[..., -1]
    w_end = b_last[..., None] - b + igc
    a_c = jnp.max(w_end, axis=-1)
    e_end = jnp.exp(w_end - a_c[..., None])
    S_c = jnp.einsum('bchs,bcshk,bcshv->bchkv', e_end, kc, vc)
    n_c = jnp.einsum('bchs,bcshk->bchk', e_end, kc)

    def step(carry, inp):
        Cs, ns, m = carry
        bl, a, Sc, nn_c = inp
        m_new = jnp.maximum(bl + m, a)
        sp = jnp.exp(bl + m - m_new)
        sc = jnp.exp(a - m_new)
        Cn = sp[..., None, None] * Cs + sc[..., None, None] * Sc
        nn = sp[..., None] * ns + sc[..., None] * nn_c
        return (Cn, nn, m_new), (Cs, ns, m)

    xs = (jnp.moveaxis(b_last, 1, 0), jnp.moveaxis(a_c, 1, 0), jnp.moveaxis(S_c, 1, 0), jnp.moveaxis(n_c, 1, 0))
    final, (C_in, n_in, m_in) = lax.scan(step, state0, xs)
    C_in = jnp.moveaxis(C_in, 0, 1)
    n_in = jnp.moveaxis(n_in, 0, 1)
    m_in = jnp.moveaxis(m_in, 0, 1)
    causal = jnp.tril(jnp.ones((T, T), dtype=bool))
    Dlog = jnp.where(causal, b[..., :, None] - b[..., None, :] + igc[..., None, :], -jnp.inf)
    g = b + m_in[..., None]
    m_star = jnp.maximum(g, jnp.max(Dlog, axis=-1))
    W = jnp.exp(Dlog - m_star[..., None]) * jnp.einsum('bcthk,bcshk->bchts', qc, kc)
    e_inter = jnp.exp(g - m_star)
    num = jnp.einsum('bchts,bcshv->bcthv', W, vc) + jnp.einsum('bcthk,bchkv,bcht->bcthv', qc, C_in, e_inter)
    den = W.sum(-1) + jnp.einsum('bcthk,bchk->bcht', qc, n_in) * e_inter
    denom = jnp.maximum(jnp.abs(den), jnp.exp(-m_star))
    h = num / jnp.swapaxes(denom, 2, 3)[..., None]
    return h.reshape(Bt, L, H, Dv), final


def window_attention(q, k, v, k_ctx, v_ctx, sink):
    Bt, S = q.shape[:2]
    nb = S // ATT_BLOCK
    G, R, Dh = ATT_KV_HEADS, ATT_Q_HEADS // ATT_KV_HEADS, ATT_HEAD_DIM
    scale = Dh ** -0.5
    qb = q.reshape(Bt, nb, ATT_BLOCK, G, R, Dh)

    def band(t):
        tp = jnp.pad(t, ((0, 0), (ATT_BLOCK, ATT_BLOCK), (0, 0), (0, 0))).reshape(Bt, nb + 2, ATT_BLOCK, G, Dh)
        return jnp.concatenate([tp[:, :-2], tp[:, 1:-1], tp[:, 2:]], axis=2)

    kb, vb = band(k), band(v)
    s_loc = jnp.einsum('bnqgrd,bnkgd->bngrqk', qb, kb, preferred_element_type=jnp.float32) * scale
    qpos = jnp.arange(nb)[:, None] * ATT_BLOCK + jnp.arange(ATT_BLOCK)[None]
    kpos = jnp.arange(nb)[:, None] * ATT_BLOCK - ATT_BLOCK + jnp.arange(3 * ATT_BLOCK)[None]
    valid = (jnp.abs(qpos[:, :, None] - kpos[:, None, :]) <= WINDOW) & (kpos[:, None, :] >= 0) & (kpos[:, None, :] < S)
    s_loc = jnp.where(valid[None, :, None, None], s_loc, -jnp.inf)
    s_ctx = jnp.einsum('bnqgrd,bcgd->bngrqc', qb, k_ctx, preferred_element_type=jnp.float32) * scale
    s_sink = sink.astype(jnp.float32).reshape(1, 1, G, R, 1, 1)
    m = jnp.maximum(jnp.maximum(s_loc.max(-1, keepdims=True), s_ctx.max(-1, keepdims=True)), s_sink)
    p_loc = jnp.exp(s_loc - m)
    p_ctx = jnp.exp(s_ctx - m)
    denom = p_loc.sum(-1, keepdims=True) + p_ctx.sum(-1, keepdims=True) + jnp.exp(s_sink - m)
    o = (jnp.einsum('bngrqk,bnkgd->bnqgrd', p_loc / denom, vb.astype(jnp.float32))
         + jnp.einsum('bngrqc,bcgd->bnqgrd', p_ctx / denom, v_ctx.astype(jnp.float32)))
    return o.reshape(Bt, S, ATT_Q_HEADS * Dh).astype(q.dtype)


def context_attention(q, k, v, sink):
    Bt, C = q.shape[:2]
    G, R = ATT_KV_HEADS, ATT_Q_HEADS // ATT_KV_HEADS
    qg = q.reshape(Bt, C, G, R, ATT_HEAD_DIM)
    s = jnp.einsum('bqgrd,bkgd->bgrqk', qg, k, preferred_element_type=jnp.float32) * ATT_HEAD_DIM ** -0.5
    s_sink = jnp.broadcast_to(sink.astype(jnp.float32).reshape(1, G, R, 1, 1), (Bt, G, R, C, 1))
    p = jax.nn.softmax(jnp.concatenate([s, s_sink], axis=-1), axis=-1)[..., :C]
    o = jnp.einsum('bgrqk,bkgd->bqgrd', p, v.astype(jnp.float32))
    return o.reshape(Bt, C, ATT_Q_HEADS * ATT_HEAD_DIM).astype(q.dtype)


def even_mixer(h_lat, h_ctx, row, col, w_in, conv_w, conv_b, dt_bias, a_log, d_skip, ssd_norm,
               q_norm, k_norm, sink, w_out, with_ctx):
    Bt = h_lat.shape[0]
    A = -jnp.exp(a_log.astype(jnp.float32))

    def project(h):
        L = h.shape[1]
        z, xbc, dt_raw, q, k, v = jnp.split(h @ w_in, cuts(EVEN_SIZES), axis=-1)
        xbc = jax.nn.silu(dwconv_centred(xbc, conv_w, conv_b))
        xs, bm, cm = jnp.split(xbc, [SSD_INNER, SSD_INNER + SSD_GROUPS * SSD_STATE], axis=-1)
        dt = jax.nn.softplus(dt_raw.reshape(Bt, L, 2, SSD_HEADS).astype(jnp.float32) + dt_bias)
        q = rmsnorm(q.reshape(Bt, L, ATT_Q_HEADS, ATT_HEAD_DIM), q_norm)
        k = rmsnorm(k.reshape(Bt, L, ATT_KV_HEADS, ATT_HEAD_DIM), k_norm)
        v = v.reshape(Bt, L, ATT_KV_HEADS, ATT_HEAD_DIM)
        return (z, xs.reshape(Bt, L, SSD_HEADS, SSD_HEAD_DIM), bm.reshape(Bt, L, SSD_GROUPS, SSD_STATE),
                cm.reshape(Bt, L, SSD_GROUPS, SSD_STATE), dt, q, k, v)

    zl, xl, bl, cl, dtl, ql, kl, vl = project(h_lat)
    zc, xc, bc, cc, dtc, qc, kc, vc = project(h_ctx)
    h0 = jnp.zeros((Bt, SSD_HEADS, SSD_HEAD_DIM, SSD_STATE), jnp.float32)
    y_lat = d_skip[:, None] * xl
    y_ctx = d_skip[:, None] * xc
    for d in range(2):
        f = rev if d else ident
        yc, hc = ssd_scan(f(xc), f(dtc[:, :, d]), A[d], f(bc), f(cc), h0)
        yl, _ = ssd_scan(f(xl), f(dtl[:, :, d]), A[d], f(bl), f(cl), hc)
        y_lat = y_lat + f(yl)
        if with_ctx:
            y_ctx = y_ctx + f(yc)

    def ssd_out(y, z):
        y = y.reshape(Bt, y.shape[1], SSD_INNER).astype(z.dtype)
        return rmsnorm(y * jax.nn.silu(z), ssd_norm)

    ql = axial_rope(ql, row, col)
    kl = axial_rope(kl, row, col)
    out_lat = jnp.concatenate([ssd_out(y_lat, zl), window_attention(ql, kl, vl, kc, vc, sink)], axis=-1) @ w_out
    out_ctx = None
    if with_ctx:
        out_ctx = (jnp.concatenate([ssd_out(y_ctx, zc), context_attention(qc, kc, vc, sink)], axis=-1) @ w_out).astype(h_ctx.dtype)
    return out_lat.astype(h_lat.dtype), out_ctx


def odd_mixer(h_lat, h_ctx, w_in, conv_w, conv_b, igate_b, fgate_b, head_norm, w_out, with_ctx):
    Bt = h_lat.shape[0]

    def project(h):
        L = h.shape[1]
        qkv, o, ig, fg = jnp.split(h @ w_in, cuts(ODD_SIZES), axis=-1)
        qkv = jax.nn.silu(dwconv_centred(qkv, conv_w, conv_b))
        q, k, v = jnp.split(qkv, [ML_HEADS * ML_QK_DIM, 2 * ML_HEADS * ML_QK_DIM], axis=-1)
        q = q.reshape(Bt, L, ML_HEADS, ML_QK_DIM)
        k = k.reshape(Bt, L, ML_HEADS, ML_QK_DIM) * (ML_QK_DIM ** -0.5)
        v = v.reshape(Bt, L, ML_HEADS, ML_V_DIM)
        ig = ig.reshape(Bt, L, 2, ML_HEADS).astype(jnp.float32) + igate_b
        lf = jax.nn.log_sigmoid(fg.reshape(Bt, L, 2, ML_HEADS).astype(jnp.float32) + fgate_b)
        return q, k, v, o, ig, lf

    ql, kl, vl, ol, igl, lfl = project(h_lat)
    qc, kc, vc, oc, igc, lfc = project(h_ctx)
    zero = (jnp.zeros((Bt, ML_HEADS, ML_QK_DIM, ML_V_DIM), jnp.float32),
            jnp.zeros((Bt, ML_HEADS, ML_QK_DIM), jnp.float32),
            jnp.zeros((Bt, ML_HEADS), jnp.float32))
    h_l = 0.0
    h_c = 0.0
    for d in range(2):
        f = rev if d else ident
        yc, sc = mlstm_scan(f(qc), f(kc), f(vc), f(igc[:, :, d]), f(lfc[:, :, d]), zero)
        yl, _ = mlstm_scan(f(ql), f(kl), f(vl), f(igl[:, :, d]), f(lfl[:, :, d]), sc)
        h_l = h_l + f(yl)
        if with_ctx:
            h_c = h_c + f(yc)

    def out(hh, o):
        hh = rmsnorm(hh, head_norm.reshape(ML_HEADS, ML_V_DIM)).reshape(Bt, hh.shape[1], ML_INNER).astype(o.dtype)
        return (hh * jax.nn.sigmoid(o)) @ w_out

    out_lat = out(h_l, ol).astype(h_lat.dtype)
    out_ctx = out(h_c, oc).astype(h_ctx.dtype) if with_ctx else None
    return out_lat, out_ctx


def moe(h, router_w, router_b, w1, w3, w2):
    scores = jax.nn.sigmoid((h @ router_w).astype(jnp.float32))
    biased = scores + router_b.astype(jnp.float32)
    per = N_EXPERTS // N_GROUPS
    grp = biased.reshape(biased.shape[:-1] + (N_GROUPS, per))
    grp_score = lax.top_k(grp, 2)[0].sum(-1)
    _, gidx = lax.top_k(grp_score, TOPK_GROUP)
    gsel = jax.nn.one_hot(gidx, N_GROUPS, dtype=jnp.float32).sum(-2) > 0
    esel = jnp.repeat(gsel, per, axis=-1)
    _, eidx = lax.top_k(jnp.where(esel, biased, -jnp.inf), TOP_K)
    w = jnp.take_along_axis(scores, eidx, axis=-1)
    w = w / w.sum(-1, keepdims=True)
    gates = (jax.nn.one_hot(eidx, N_EXPERTS, dtype=jnp.float32) * w[..., None]).sum(-2)
    out = jnp.zeros(h.shape, jnp.float32)
    for e in range(N_EXPERTS):
        a = jax.nn.silu(h @ w1[e]) * (h @ w3[e])
        out = out + gates[..., e:e + 1] * (a @ w2[e])
    return out.astype(h.dtype)


def setup_inputs(seed: int = 0) -> dict:
    key = jax.random.key(seed)
    keys = jax.random.split(key, 48)
    ki = iter(range(48))
    D = D_MODEL

    def nrm(shape, s):
        return jax.random.normal(keys[next(ki)], shape, jnp.float32) * s

    def unif(shape, lo, hi):
        return jax.random.uniform(keys[next(ki)], shape, jnp.float32, minval=lo, maxval=hi)

    dt = jnp.exp(unif((N_EVEN, 2, SSD_HEADS), math.log(1e-3), math.log(1e-1)))
    return {
        'x': nrm((BATCH, SEQ, D), 1.0),
        'c': nrm((BATCH, D), 1.0),
        'ctx': nrm((BATCH, CTX_LEN, D), 1.0),
        'c_ctx': nrm((D,), 1.0),
        'router_w': nrm((D, N_EXPERTS), D ** -0.5),
        'router_b': nrm((N_EXPERTS,), 0.01),
        'norm_mix': 1.0 + nrm((DEPTH, D), 0.02),
        'norm_ffn': 1.0 + nrm((DEPTH, D), 0.02),
        'w_mod': nrm((DEPTH, D, 6 * D), 0.5 * D ** -0.5),
        'b_mod': nrm((DEPTH, 6 * D), 0.02),
        'ev_w_in': nrm((N_EVEN, D, EVEN_IN), D ** -0.5),
        'ev_conv_w': nrm((N_EVEN, CONV_W, SSD_CONV_DIM), CONV_W ** -0.5),
        'ev_conv_b': nrm((N_EVEN, SSD_CONV_DIM), 0.02),
        'ev_dt_bias': dt + jnp.log(-jnp.expm1(-dt)),
        'ev_a_log': jnp.log(unif((N_EVEN, 2, SSD_HEADS), 1.0, 16.0)),
        'ev_d_skip': 1.0 + nrm((N_EVEN, SSD_HEADS), 0.1),
        'ev_ssd_norm': 1.0 + nrm((N_EVEN, SSD_INNER), 0.02),
        'ev_q_norm': 1.0 + nrm((N_EVEN, ATT_HEAD_DIM), 0.02),
        'ev_k_norm': 1.0 + nrm((N_EVEN, ATT_HEAD_DIM), 0.02),
        'ev_sink': nrm((N_EVEN, ATT_Q_HEADS), 1.0),
        'ev_w_out': nrm((N_EVEN, EVEN_MIX, D), EVEN_MIX ** -0.5),
        'od_w_in': nrm((N_ODD, D, ODD_IN), D ** -0.5),
        'od_conv_w': nrm((N_ODD, CONV_W, ML_QKV), CONV_W ** -0.5),
        'od_conv_b': nrm((N_ODD, ML_QKV), 0.02),
        'od_igate_b': nrm((N_ODD, 2, ML_HEADS), 0.1),
        'od_fgate_b': jnp.linspace(3.0, 6.0, ML_HEADS, dtype=jnp.float32)[None, None, :] + nrm((N_ODD, 2, ML_HEADS), 0.1),
        'od_head_norm': 1.0 + nrm((N_ODD, ML_INNER), 0.02),
        'od_w_out': nrm((N_ODD, ML_INNER, D), ML_INNER ** -0.5),
        'moe_w1': nrm((DEPTH, N_EXPERTS, D, D_EXPERT), D ** -0.5),
        'moe_w3': nrm((DEPTH, N_EXPERTS, D, D_EXPERT), D ** -0.5),
        'moe_w2': nrm((DEPTH, N_EXPERTS, D_EXPERT, D), D_EXPERT ** -0.5),
    }


def reference(x, c, ctx, c_ctx, router_w, router_b, norm_mix, norm_ffn, w_mod, b_mod,
              ev_w_in, ev_conv_w, ev_conv_b, ev_dt_bias, ev_a_log, ev_d_skip, ev_ssd_norm,
              ev_q_norm, ev_k_norm, ev_sink, ev_w_out,
              od_w_in, od_conv_w, od_conv_b, od_igate_b, od_fgate_b, od_head_norm, od_w_out,
              moe_w1, moe_w3, moe_w2):
    S = x.shape[1]
    ROWS = S // GRID_W
    row = jnp.repeat(jnp.arange(ROWS, dtype=jnp.int32), GRID_W)
    col = jnp.tile(jnp.arange(GRID_W, dtype=jnp.int32), ROWS)
    x_lat, x_ctx = x, ctx
    for l in range(DEPTH):
        with_ctx = l < DEPTH - 1
        sh_m, sc_m, g_m, sh_f, sc_f, g_f = adaln(c, w_mod[l], b_mod[l])
        csh_m, csc_m, cg_m, csh_f, csc_f, cg_f = adaln(c_ctx, w_mod[l], b_mod[l])
        h_lat = rmsnorm(x_lat, norm_mix[l]) * (1.0 + sc_m) + sh_m
        h_ctx = rmsnorm(x_ctx, norm_mix[l]) * (1.0 + csc_m) + csh_m
        i = l // 2
        if l % 2 == 0:
            m_lat, m_ctx = even_mixer(h_lat, h_ctx, row, col, ev_w_in[i], ev_conv_w[i], ev_conv_b[i], ev_dt_bias[i],
                                      ev_a_log[i], ev_d_skip[i], ev_ssd_norm[i], ev_q_norm[i], ev_k_norm[i],
                                      ev_sink[i], ev_w_out[i], with_ctx)
        else:
            m_lat, m_ctx = odd_mixer(h_lat, h_ctx, od_w_in[i], od_conv_w[i], od_conv_b[i], od_igate_b[i],
                                     od_fgate_b[i], od_head_norm[i], od_w_out[i], with_ctx)
        x_lat = x_lat + g_m * m_lat
        h_lat = rmsnorm(x_lat, norm_ffn[l]) * (1.0 + sc_f) + sh_f
        x_lat = x_lat + g_f * moe(h_lat, router_w, router_b, moe_w1[l], moe_w3[l], moe_w2[l])
        if with_ctx:
            x_ctx = x_ctx + cg_m * m_ctx
            h_ctx = rmsnorm(x_ctx, norm_ffn[l]) * (1.0 + csc_f) + csh_f
            x_ctx = x_ctx + cg_f * moe(h_ctx, router_w, router_b, moe_w1[l], moe_w3[l], moe_w2[l])
    return x_lat
```

```python
import functools
import math

import jax
import jax.numpy as jnp
from jax import lax
from jax.experimental import pallas as pl
from jax.experimental.pallas import tpu as pltpu

F32 = jnp.float32
BF16 = jnp.bfloat16
I32 = jnp.int32

EPS = 1e-6
D = 1024
T = 128
TM = 256
TMM = 256
GRID_W = 64
ROPE_THETA = 10000.0
N_EXPERTS = 16
N_BUCKETS = 24
NB_PAD = 32
D_EXPERT = 512
XROW = D + 128
NEG_INF = float("-inf")
VMEM_LIMIT = 56 * 1024 * 1024

_NN = (((1,), (0,)), ((), ()))
_NT = (((1,), (1,)), ((), ()))
_TN = (((0,), (0,)), ((), ()))

_PAIR_A = (0, 0, 0, 1, 1, 2)
_PAIR_B = (1, 2, 3, 2, 3, 3)


def _cparams(*sem):
    return pltpu.CompilerParams(dimension_semantics=sem, vmem_limit_bytes=VMEM_LIMIT)


def _dot(a, b, dims=_NN):
    return lax.dot_general(a, b, dims, preferred_element_type=F32)


def _split(a, n):
    out = []
    r = a
    for _ in range(n):
        t = r.astype(BF16)
        out.append(t)
        r = r - t.astype(F32)
    return out


def _mdot(as_, bs, dims=_NN, order=None):
    if order is None:
        order = len(as_) + len(bs) - 2
    acc = None
    for i, a in enumerate(as_):
        for j, b in enumerate(bs):
            if i + j <= order:
                p = _dot(a, b, dims)
                acc = p if acc is None else acc + p
    return acc


def _sigmoid(x):
    return 1.0 / (1.0 + jnp.exp(-x))


def _silu(x):
    return x * _sigmoid(x)


def _log1p_exp_neg_abs(x):
    e = jnp.exp(-jnp.abs(x))
    u = 1.0 + e
    um1 = u - 1.0
    return jnp.where(um1 == 0.0, e, jnp.log(u) * (e / jnp.where(um1 == 0.0, 1.0, um1)))


def _softplus(x):
    return jnp.maximum(x, 0.0) + _log1p_exp_neg_abs(x)


def _log_sigmoid(x):
    return jnp.minimum(x, 0.0) - _log1p_exp_neg_abs(x)


def _norm_mod(x, g, sc, sh):
    ms = jnp.mean(x * x, axis=-1, keepdims=True)
    return (x * lax.rsqrt(ms + EPS)) * g * (1.0 + sc) + sh


def _tri(rev):
    r = lax.broadcasted_iota(I32, (T, T), 0)
    c = lax.broadcasted_iota(I32, (T, T), 1)
    return (c >= r) if rev else (c <= r)


def _cumsums(rev, col, row):
    tri = _tri(rev)
    tri_b = tri.astype(F32).astype(BF16)
    trit_b = _tri(not rev).astype(F32).astype(BF16)
    ccol = _mdot([tri_b], _split(col, 3))
    crow = _mdot(_split(row, 3), [trit_b])
    return tri, ccol, crow


def _mod_kernel(c_ref, w_ref, b_ref, o_ref):
    a = _silu(c_ref[...])
    o_ref[0] = _mdot(_split(a, 2), _split(w_ref[0], 2), order=1) + b_ref[0]


def _modulation(c, c_ctx, w_mod, b_mod):
    depth = w_mod.shape[0]
    n = w_mod.shape[2]
    tn = 1536
    cc = jnp.zeros((8, D), F32).at[0].set(c[0]).at[1].set(c_ctx)
    out = pl.pallas_call(
        _mod_kernel,
        grid=(depth, n // tn),
        in_specs=[
            pl.BlockSpec((8, D), lambda l, j: (0, 0)),
            pl.BlockSpec((1, D, tn), lambda l, j: (l, 0, j)),
            pl.BlockSpec((1, 1, tn), lambda l, j: (l, 0, j)),
        ],
        out_specs=pl.BlockSpec((1, 8, tn), lambda l, j: (l, 0, j)),
        out_shape=jax.ShapeDtypeStruct((depth, 8, n), F32),
        compiler_params=_cparams("arbitrary", "arbitrary"),
        name="modulation",
    )(cc, w_mod, b_mod.reshape(depth, 1, n))
    return out[:, :2].reshape(depth, 2, 6, D)


def _halo_specs(nt):
    nb8 = nt // 8
    return [
        pl.BlockSpec((8, D), lambda i: (jnp.maximum(i * (TM // 8) - 1, 0), 0)),
        pl.BlockSpec((TM, D), lambda i: (i, 0)),
        pl.BlockSpec((8, D), lambda i: (jnp.minimum((i + 1) * (TM // 8), nb8 - 1), 0)),
    ]


def _mod_spec(nct):
    return pl.BlockSpec((1, 6, D), lambda i: (jnp.where(i < nct, 1, 0), 0, 0))


def _full(shape):
    nd = len(shape)
    return pl.BlockSpec(shape, lambda i: (0,) * nd)


def _seq_edges(i, nct, ntiles):
    prev_ok = jnp.logical_and(i != 0, i != nct).astype(F32)
    next_ok = jnp.logical_and(i != nct - 1, i != ntiles - 1).astype(F32)
    return prev_ok, next_ok


def _conv_silu(x, x_first_prev, x_last_next, cw, cb):
    n = x.shape[0]
    rows = lax.broadcasted_iota(I32, x.shape, 0)
    x_prev = jnp.where(rows == 0, x_first_prev, pltpu.roll(x, 1, 0))
    x_next = jnp.where(rows == n - 1, x_last_next, pltpu.roll(x, n - 1, 0))
    return _silu(x_prev * cw[0:1] + x * cw[1:2] + x_next * cw[2:3] + cb)


def _head_rms(xf, gamma):
    r = lax.broadcasted_iota(I32, (128, 128), 0) // 64
    c = lax.broadcasted_iota(I32, (128, 128), 1) // 64
    ones_bd = (r == c).astype(F32).astype(BF16)
    outs = []
    for j in range(xf.shape[1] // 128):
        blk = xf[:, 128 * j:128 * (j + 1)]
        ssum = _mdot(_split(blk * blk, 2), [ones_bd])
        outs.append(blk * lax.rsqrt(ssum * (1.0 / 64.0) + EPS))
    return jnp.concatenate(outs, axis=1) * gamma


def _rope(xf, cos, sin):
    lane = lax.broadcasted_iota(I32, (xf.shape[0], 128), 1)
    first = (lane % 32) < 16
    outs = []
    for j in range(xf.shape[1] // 128):
        blk = xf[:, 128 * j:128 * (j + 1)]
        partner = jnp.where(first, pltpu.roll(blk, 112, 1), pltpu.roll(blk, 16, 1))
        outs.append(blk * cos + partner * sin)
    return jnp.concatenate(outs, axis=1)


def _rope_tables(c_len, s_len):
    t = jnp.arange(s_len, dtype=I32)
    row = (t // GRID_W).astype(F32)
    col = (t % GRID_W).astype(F32)
    lane = jnp.arange(128, dtype=I32)
    inv = ROPE_THETA ** (-(lane % 16).astype(F32) / 16.0)
    pos = jnp.where(((lane % 64) < 32)[None, :], row[:, None], col[:, None])
    ang = pos * inv[None, :]
    sign = jnp.where((lane % 32) < 16, -1.0, 1.0).astype(F32)
    cos = jnp.concatenate([jnp.ones((c_len, 128), F32), jnp.cos(ang)], axis=0)
    sin = jnp.concatenate([jnp.zeros((c_len, 128), F32), jnp.sin(ang) * sign[None, :]], axis=0)
    return cos, sin


def _in0_kernel(nct, ntiles, xp_ref, x_ref, xn_ref, mod_ref, g_ref, wcat_ref, wdt_ref, wdtt_ref,
                cw_ref, cb_ref, dtbr_ref, dtbc_ref, qn_ref, kn_ref, cos_ref, sin_ref,
                z_ref, xs_ref, bc_ref, q_ref, kk_ref, vv_ref, dtc_ref, dtr_ref):
    i = pl.program_id(0)
    sh = mod_ref[0, 0:1, :]
    sc = mod_ref[0, 1:2, :]
    g = g_ref[...]
    h = _norm_mod(x_ref[...], g, sc, sh)
    hb = h.astype(BF16)
    main = _dot(hb, wcat_ref[...])
    prev_ok, next_ok = _seq_edges(i, nct, ntiles)
    wxbc = wcat_ref[:, 512:1280]
    hp = _norm_mod(xp_ref[...], g, sc, sh).astype(BF16)
    hn = _norm_mod(xn_ref[...], g, sc, sh).astype(BF16)
    xb_prev = _dot(hp, wxbc)[7:8, :] * prev_ok
    xb_next = _dot(hn, wxbc)[0:1, :] * next_ok
    act = _conv_silu(main[:, 512:1280], xb_prev, xb_next, cw_ref[...], cb_ref[...])
    z_ref[...] = main[:, 0:512].astype(BF16)
    xs_ref[...] = act[:, 0:512].astype(BF16)
    bc_ref[...] = act[:, 512:768].astype(BF16)
    cos = cos_ref[...]
    sin = sin_ref[...]
    q = _rope(_head_rms(main[:, 1280:1792], qn_ref[...]), cos, sin) * 0.125
    q_ref[...] = q.astype(BF16)
    k = _rope(_head_rms(main[:, 1792:1920], kn_ref[...]), cos, sin)
    kk_ref[...] = jnp.concatenate([k, pltpu.roll(k, 64, 1)], axis=1).astype(BF16)
    v = main[:, 1920:2048]
    vv_ref[...] = jnp.concatenate([v, pltpu.roll(v, 64, 1)], axis=1).astype(BF16)
    hs = _split(h, 3)
    dtc_ref[...] = _softplus(_mdot(hs, _split(wdt_ref[...], 3), order=2) + dtbr_ref[...])
    dtr_ref[...] = _softplus(_mdot(_split(wdtt_ref[...], 3), hs, _NT, order=2) + dtbc_ref[...])


def _in0(xcat, mod, p, nct):
    nt = xcat.shape[0]
    ntiles = nt // TM
    tile = lambda w: pl.BlockSpec((TM, w), lambda i: (i, 0))
    outs = [(512, BF16), (512, BF16), (256, BF16), (512, BF16), (256, BF16), (256, BF16), (16, F32)]
    return pl.pallas_call(
        functools.partial(_in0_kernel, nct, ntiles),
        grid=(ntiles,),
        in_specs=_halo_specs(nt) + [
            _mod_spec(nct), _full((1, D)), _full((D, 2048)), _full((D, 16)), _full((16, D)),
            _full((3, 768)), _full((1, 768)), _full((1, 16)), _full((16, 1)),
            _full((1, 512)), _full((1, 128)), tile(128), tile(128)],
        out_specs=[tile(w) for w, _ in outs] + [pl.BlockSpec((16, TM), lambda i: (0, i))],
        out_shape=[jax.ShapeDtypeStruct((nt, w), dt) for w, dt in outs]
        + [jax.ShapeDtypeStruct((16, nt), F32)],
        compiler_params=_cparams("arbitrary"),
        name="in_proj_even",
    )(xcat, xcat, xcat, mod, p["g_mix"], p["wcat"], p["wdt"], p["wdt_t"], p["conv_w"], p["conv_b"],
      p["dtb_row"], p["dtb_col"], p["q_norm"], p["k_norm"], p["cos"], p["sin"])


def _scan_chunk_map(rev, ncc, nlc):
    if not rev:
        return lambda j: j
    return lambda j: jnp.where(j < ncc, ncc - 1 - j, ncc + nlc - 1 - (j - ncc))


def _ssd_kernel(rev, *refs):
    if rev:
        (xs_ref, bc_ref, dtc_ref, dtr_ref, alr_ref, alc_ref, yf_ref, z_ref, dsk_ref, nrm_ref,
         o_ref, st_ref) = refs
    else:
        xs_ref, bc_ref, dtc_ref, dtr_ref, alr_ref, alc_ref, o_ref, st_ref = refs
    j = pl.program_id(0)

    @pl.when(j == 0)
    def _():
        st_ref[...] = jnp.zeros_like(st_ref)

    d = 8 if rev else 0
    a_coef_row = -jnp.exp(alr_ref[...])[:, d:d + 8]
    a_coef_col = -jnp.exp(alc_ref[...])[d:d + 8, :]
    dtc = dtc_ref[:, d:d + 8]
    dtr = dtr_ref[d:d + 8, :]
    tri, acs_col, acs_row = _cumsums(rev, dtc * a_coef_row, dtr * a_coef_col)
    end = 0 if rev else T - 1
    atot_row = acs_col[end:end + 1, :]
    atot_col = acs_row[:, end:end + 1]
    dec_col = jnp.exp(atot_row - acs_col) * dtc
    eacs_col = jnp.exp(acs_col)
    xs = xs_ref[...]
    bm = bc_ref[:, 0:128]
    cm = bc_ref[:, 128:256]
    lane = lax.broadcasted_iota(I32, (T, 128), 1)
    lo = lane < 64
    zero_b = jnp.zeros((T, 128), BF16)
    ys = [None] * 4
    for g in range(2):
        in_g = lo if g == 0 else jnp.logical_not(lo)
        cg = jnp.where(in_g, cm, zero_b)
        bg = jnp.where(in_g, bm, zero_b)
        cb = _dot(cg, bm, _NT)
        for hh in range(4):
            hd = 4 * g + hh
            pair, hi = hd // 2, hd % 2
            diff = acs_col[:, hd:hd + 1] - acs_row[hd:hd + 1, :]
            lmat = jnp.exp(jnp.where(tri, diff, NEG_INF))
            m = (cb * lmat * dtr[hd:hd + 1, :]).astype(BF16)
            xp = xs[:, 128 * pair:128 * (pair + 1)]
            xh = jnp.where(lo if hi == 0 else jnp.logical_not(lo), xp, zero_b)
            st = st_ref[hd]
            y = _dot(m, xh) + _dot(cg, st.astype(BF16)) * eacs_col[:, hd:hd + 1]
            ys[pair] = y if ys[pair] is None else ys[pair] + y
            xdec = (xh.astype(F32) * dec_col[:, hd:hd + 1]).astype(BF16)
            st_ref[hd] = jnp.exp(atot_col[hd:hd + 1, :]) * st + _dot(bg, xdec, _TN)
    y = jnp.concatenate(ys, axis=1)
    if not rev:
        o_ref[...] = y
    else:
        ytot = y + yf_ref[...] + dsk_ref[...] * xs.astype(F32)
        gated = ytot * _silu(z_ref[...].astype(F32))
        ms = jnp.mean(gated * gated, axis=-1, keepdims=True)
        o_ref[...] = (gated * lax.rsqrt(ms + EPS) * nrm_ref[...]).astype(BF16)


def _ssd(rev, xs, bc, dtc, dtr, p, ncc, nlc, yf=None, z=None):
    nt = xs.shape[0]
    cmap = _scan_chunk_map(rev, ncc, nlc)
    blk = lambda w: pl.BlockSpec((T, w), lambda j: (cmap(j), 0))
    in_specs = [blk(512), blk(256), blk(16), pl.BlockSpec((16, T), lambda j: (0, cmap(j))),
                _full((1, 16)), _full((16, 1))]
    args = [xs, bc, dtc, dtr, p["alog_row"], p["alog_col"]]
    if rev:
        in_specs += [blk(512), blk(512), _full((1, 512)), _full((1, 512))]
        args += [yf, z, p["d_skip"], p["ssd_norm"]]
    return pl.pallas_call(
        functools.partial(_ssd_kernel, rev),
        grid=(ncc + nlc,),
        in_specs=in_specs,
        out_specs=blk(512),
        out_shape=jax.ShapeDtypeStruct((nt, 512), BF16 if rev else F32),
        scratch_shapes=[pltpu.VMEM((8, 128, 128), F32)],
        compiler_params=_cparams("arbitrary"),
        name="ssd_bwd" if rev else "ssd_fwd",
    )(*args)


def _attn_kernel(ncc, nblk, q_ref, kp_ref, kc_ref, kn_ref, vp_ref, vc_ref, vn_ref, kx_ref, vx_ref,
                 sink_ref, o_ref):
    j = pl.program_id(0)
    c_len = kx_ref.shape[0]
    is_lat = j >= ncc
    prev_ok = jnp.logical_and(is_lat, j >= ncc + 1)
    next_ok = jnp.logical_and(is_lat, j <= nblk - 2)
    r = lax.broadcasted_iota(I32, (T, T), 0)
    c = lax.broadcasted_iota(I32, (T, T), 1)
    zero = jnp.zeros((T, T), F32)
    ninf = jnp.full((T, T), NEG_INF, F32)
    bias = jnp.concatenate([
        jnp.where(jnp.logical_and(prev_ok, c >= r), zero, ninf),
        jnp.where(is_lat, zero, ninf),
        jnp.where(jnp.logical_and(next_ok, c <= r), zero, ninf),
        jnp.zeros((T, c_len), F32)], axis=1)
    k_all = jnp.concatenate([kp_ref[...], kc_ref[...], kn_ref[...], kx_ref[...]], axis=0)
    v_all = jnp.concatenate([vp_ref[...], vc_ref[...], vn_ref[...], vx_ref[...]], axis=0)
    nk = k_all.shape[0]
    lane = lax.broadcasted_iota(I32, (nk, 128), 1)
    lo = lane < 64
    zero_b = jnp.zeros((nk, 128), BF16)
    q = q_ref[...]
    sink = sink_ref[...]
    for g in range(2):
        k_lo = jnp.where(lo, k_all[:, 128 * g:128 * (g + 1)], zero_b)
        k_hi = jnp.where(lo, zero_b, k_all[:, 128 * (1 - g):128 * (2 - g)])
        v_lo = jnp.where(lo, v_all[:, 128 * g:128 * (g + 1)], zero_b)
        v_hi = jnp.where(lo, zero_b, v_all[:, 128 * (1 - g):128 * (2 - g)])
        for pp in range(2):
            pair = 2 * g + pp
            qp = q[:, 128 * pair:128 * (pair + 1)]
            acc = None
            for hi in range(2):
                hd = 2 * pair + hi
                s = _dot(qp, k_hi if hi else k_lo, _NT) + bias
                sk = sink[:, hd:hd + 1]
                m = jnp.maximum(jnp.max(s, axis=-1, keepdims=True), sk)
                pr = jnp.exp(s - m)
                den = jnp.sum(pr, axis=-1, keepdims=True) + jnp.exp(sk - m)
                o = _dot(pr.astype(BF16), v_hi if hi else v_lo) / den
                acc = o if acc is None else acc + o
            o_ref[:, 128 * pair:128 * (pair + 1)] = acc.astype(BF16)


def _attention(q, kk, vv, sink, ncc, nlc):
    nt = q.shape[0]
    nblk = ncc + nlc
    c_len = ncc * T
    prev = lambda w: pl.BlockSpec((T, w), lambda j: (jnp.maximum(j - 1, 0), 0))
    cur = lambda w: pl.BlockSpec((T, w), lambda j: (j, 0))
    nxt = lambda w: pl.BlockSpec((T, w), lambda j: (jnp.minimum(j + 1, nblk - 1), 0))
    ctx = lambda w: pl.BlockSpec((c_len, w), lambda j: (0, 0))
    return pl.pallas_call(
        functools.partial(_attn_kernel, ncc, nblk),
        grid=(nblk,),
        in_specs=[cur(512), prev(256), cur(256), nxt(256), prev(256), cur(256), nxt(256),
                  ctx(256), ctx(256), _full((1, 8))],
        out_specs=cur(512),
        out_shape=jax.ShapeDtypeStruct((nt, 512), BF16),
        compiler_params=_cparams("arbitrary"),
        name="window_attention",
    )(q, kk, kk, kk, vv, vv, vv, kk, vv, sink)


def _in1_kernel(nct, ntiles, xp_ref, x_ref, xn_ref, mod_ref, g_ref, wcat_ref, wg_ref, wgt_ref,
                cw_ref, cb_ref, gbr_ref, gbc_ref,
                q_ref, kt_ref, v_ref, o_ref, gc_ref, gr_ref):
    i = pl.program_id(0)
    sh = mod_ref[0, 0:1, :]
    sc = mod_ref[0, 1:2, :]
    g = g_ref[...]
    h = _norm_mod(x_ref[...], g, sc, sh)
    hb = h.astype(BF16)
    main = _dot(hb, wcat_ref[...])
    prev_ok, next_ok = _seq_edges(i, nct, ntiles)
    wqkv = wcat_ref[:, 0:2048]
    hp = _norm_mod(xp_ref[...], g, sc, sh).astype(BF16)
    hn = _norm_mod(xn_ref[...], g, sc, sh).astype(BF16)
    x_prev = _dot(hp, wqkv)[7:8, :] * prev_ok
    x_next = _dot(hn, wqkv)[0:1, :] * next_ok
    act = _conv_silu(main[:, 0:2048], x_prev, x_next, cw_ref[...], cb_ref[...])
    q_ref[...] = act[:, 0:512].astype(BF16)
    kt_ref[...] = jnp.transpose(act[:, 512:1024] * 0.125).astype(BF16)
    v_ref[...] = act[:, 1024:2048].astype(BF16)
    o_ref[...] = main[:, 2048:3072].astype(BF16)
    hs = _split(h, 3)
    gc = _mdot(hs, _split(wg_ref[...], 3), order=2) + gbr_ref[...]
    gr = _mdot(_split(wgt_ref[...], 3), hs, _NT, order=2) + gbc_ref[...]
    gc_ref[...] = jnp.concatenate([gc[:, 0:16], _log_sigmoid(gc[:, 16:32])], axis=1)
    gr_ref[...] = jnp.concatenate([gr[0:16, :], _log_sigmoid(gr[16:32, :])], axis=0)


def _in1(xcat, mod, p, nct):
    nt = xcat.shape[0]
    ntiles = nt // TM
    tile = lambda w: pl.BlockSpec((TM, w), lambda i: (i, 0))
    return pl.pallas_call(
        functools.partial(_in1_kernel, nct, ntiles),
        grid=(ntiles,),
        in_specs=_halo_specs(nt) + [
            _mod_spec(nct), _full((1, D)), _full((D, 3072)), _full((D, 32)), _full((32, D)),
            _full((3, 2048)), _full((1, 2048)), _full((1, 32)), _full((32, 1))],
        out_specs=[tile(512), pl.BlockSpec((512, TM), lambda i: (0, i)), tile(1024), tile(1024),
                   tile(32), pl.BlockSpec((32, TM), lambda i: (0, i))],
        out_shape=[jax.ShapeDtypeStruct((nt, 512), BF16), jax.ShapeDtypeStruct((512, nt), BF16),
                   jax.ShapeDtypeStruct((nt, 1024), BF16), jax.ShapeDtypeStruct((nt, 1024), BF16),
                   jax.ShapeDtypeStruct((nt, 32), F32), jax.ShapeDtypeStruct((32, nt), F32)],
        compiler_params=_cparams("arbitrary"),
        name="in_proj_odd",
    )(xcat, xcat, xcat, mod, p["g_mix"], p["wcat"], p["wg"], p["wg_t"], p["conv_w"], p["conv_b"],
      p["gb_row"], p["gb_col"])


def _mlstm_kernel(rev, *refs):
    if rev:
        (q_ref, kt_ref, v_ref, gc_ref, gr_ref, hf_ref, og_ref, hn_ref, o_ref,
         c_ref, mc_ref, mr_ref) = refs
    else:
        q_ref, kt_ref, v_ref, gc_ref, gr_ref, o_ref, c_ref, mc_ref, mr_ref = refs
    j = pl.program_id(0)

    @pl.when(j == 0)
    def _():
        c_ref[...] = jnp.zeros_like(c_ref)
        mc_ref[...] = jnp.zeros_like(mc_ref)
        mr_ref[...] = jnp.zeros_like(mr_ref)

    d = 8 if rev else 0
    ig_col = gc_ref[:, d:d + 8]
    lf_col = gc_ref[:, 16 + d:24 + d]
    ig_row = gr_ref[d:d + 8, :]
    lf_row = gr_ref[16 + d:24 + d, :]
    tri, b_col, b_row = _cumsums(rev, lf_col, lf_row)
    end = 0 if rev else T - 1
    blast_row = b_col[end:end + 1, :]
    blast_col = b_row[:, end:end + 1]
    wend_row = blast_col - b_row + ig_row
    ac_col = jnp.max(wend_row, axis=1, keepdims=True)
    eend_row = jnp.exp(wend_row - ac_col)
    ac_row = jnp.max(blast_row - b_col + ig_col, axis=0, keepdims=True)
    m_col = mc_ref[:, 0:1]
    m_row = mr_ref[0:1, 0:8]
    mnew_col = jnp.maximum(blast_col + m_col, ac_col)
    sp_col = jnp.exp(blast_col + m_col - mnew_col)
    sc_col = jnp.exp(ac_col - mnew_col)
    mnew_row = jnp.maximum(blast_row + m_row, ac_row)
    g_col = b_col + m_row
    q = q_ref[...]
    ones_b = jnp.ones((T, 128), BF16)
    sub = lax.broadcasted_iota(I32, (128, T), 0)
    zero_k = jnp.zeros((128, T), BF16)
    for hd in range(8):
        pair, hi = hd // 2, hd % 2
        qp = q[:, 128 * pair:128 * (pair + 1)]
        ktp = kt_ref[128 * pair:128 * (pair + 1), :]
        kth = jnp.where((sub >= 64) if hi else (sub < 64), ktp, zero_k)
        vaug = jnp.concatenate([v_ref[:, 128 * hd:128 * (hd + 1)], ones_b], axis=1)
        dlog = jnp.where(tri, b_col[:, hd:hd + 1] - b_row[hd:hd + 1, :] + ig_row[hd:hd + 1, :], NEG_INF)
        gh = g_col[:, hd:hd + 1]
        mstar = jnp.maximum(gh, jnp.max(dlog, axis=-1, keepdims=True))
        w = (jnp.exp(dlog - mstar) * _dot(qp, kth)).astype(BF16)
        cst = c_ref[hd]
        nd = _dot(w, vaug) + jnp.exp(gh - mstar) * _dot(qp, cst.astype(BF16))
        den = jnp.maximum(jnp.abs(nd[:, 128:256]), jnp.exp(-mstar))
        hh = nd[:, 0:128] / den
        if rev:
            hh = hh + hf_ref[:, 128 * hd:128 * (hd + 1)]
            ms = jnp.mean(hh * hh, axis=-1, keepdims=True)
            hh = hh * lax.rsqrt(ms + EPS) * hn_ref[:, 128 * hd:128 * (hd + 1)]
            og = og_ref[:, 128 * hd:128 * (hd + 1)].astype(F32)
            o_ref[:, 128 * hd:128 * (hd + 1)] = (hh * _sigmoid(og)).astype(BF16)
        else:
            o_ref[:, 128 * hd:128 * (hd + 1)] = hh
        kte = (kth.astype(F32) * eend_row[hd:hd + 1, :]).astype(BF16)
        c_ref[hd] = sp_col[hd:hd + 1, :] * cst + sc_col[hd:hd + 1, :] * _dot(kte, vaug)
    mc_ref[...] = jnp.broadcast_to(mnew_col, mc_ref.shape)
    mr_ref[...] = jnp.broadcast_to(jnp.concatenate([mnew_row, jnp.zeros((1, 120), F32)], axis=1), mr_ref.shape)


def _mlstm(rev, q, kt, v, gc, gr, p, ncc, nlc, hf=None, og=None):
    nt = q.shape[0]
    cmap = _scan_chunk_map(rev, ncc, nlc)
    blk = lambda w: pl.BlockSpec((T, w), lambda j: (cmap(j), 0))
    blk_t = lambda h: pl.BlockSpec((h, T), lambda j: (0, cmap(j)))
    in_specs = [blk(512), blk_t(512), blk(1024), blk(32), blk_t(32)]
    args = [q, kt, v, gc, gr]
    if rev:
        in_specs += [blk(1024), blk(1024), _full((1, 1024))]
        args += [hf, og, p["head_norm"]]
    return pl.pallas_call(
        functools.partial(_mlstm_kernel, rev),
        grid=(ncc + nlc,),
        in_specs=in_specs,
        out_specs=blk(1024),
        out_shape=jax.ShapeDtypeStruct((nt, 1024), BF16 if rev else F32),
        scratch_shapes=[pltpu.VMEM((8, 128, 256), F32), pltpu.VMEM((8, 128), F32), pltpu.VMEM((8, 128), F32)],
        compiler_params=_cparams("arbitrary"),
        name="mlstm_bwd" if rev else "mlstm_fwd",
    )(*args)


def _route(logits_t, rb_col):
    scores = _sigmoid(logits_t)
    biased = scores + rb_col
    row = lambda a, e: a[e:e + 1, :]
    gscore = []
    for g in range(4):
        b0, b1, b2, b3 = (row(biased, 4 * g + e) for e in range(4))
        h1, l1 = jnp.maximum(b0, b1), jnp.minimum(b0, b1)
        h2, l2 = jnp.maximum(b2, b3), jnp.minimum(b2, b3)
        gscore.append(jnp.maximum(h1, h2) + jnp.maximum(jnp.minimum(h1, h2), jnp.maximum(l1, l2)))
    gidx = jnp.zeros_like(gscore[0], dtype=I32)
    best = gscore[0]
    for g in range(1, 4):
        better = gscore[g] > best
        gidx = jnp.where(better, g, gidx)
        best = jnp.where(better, gscore[g], best)

    def pick(a, e):
        out = row(a, e)
        for g in range(1, 4):
            out = jnp.where(gidx == g, row(a, 4 * g + e), out)
        return out

    sb = [pick(biased, e) for e in range(4)]
    ss = [pick(scores, e) for e in range(4)]
    i1 = jnp.zeros_like(gidx)
    v1, s1 = sb[0], ss[0]
    for e in range(1, 4):
        better = sb[e] > v1
        i1 = jnp.where(better, e, i1)
        v1 = jnp.where(better, sb[e], v1)
        s1 = jnp.where(better, ss[e], s1)
    i2 = jnp.zeros_like(gidx)
    v2 = jnp.full_like(v1, NEG_INF)
    s2 = jnp.zeros_like(s1)
    for e in range(4):
        better = jnp.logical_and(i1 != e, sb[e] > v2)
        i2 = jnp.where(better, e, i2)
        v2 = jnp.where(better, sb[e], v2)
        s2 = jnp.where(better, ss[e], s2)
    tot = s1 + s2
    w1 = s1 / tot
    w2 = s2 / tot
    first_low = i1 < i2
    a = jnp.minimum(i1, i2)
    b = jnp.maximum(i1, i2)
    off = jnp.where(a == 0, 0, jnp.where(a == 1, 3, 5))
    bucket = 6 * gidx + off + (b - a - 1)
    return bucket, jnp.where(first_low, w1, w2), jnp.where(first_low, w2, w1)


def _out_kernel(nmix, *refs):
    mix_refs = refs[:nmix]
    w_refs = refs[nmix:2 * nmix]
    (x_ref, mod_ref, g_ref, rwt_ref, rb_ref,
     xmid_ref, hrow_ref, bucket_ref, rank_ref, cnt_ref, cnt_scr) = refs[2 * nmix:]
    i = pl.program_id(0)

    @pl.when(i == 0)
    def _():
        cnt_scr[...] = jnp.zeros_like(cnt_scr)

    m = None
    for mr, wr in zip(mix_refs, w_refs):
        pdt = _dot(mr[...], wr[...])
        m = pdt if m is None else m + pdt
    x_mid = x_ref[...] + mod_ref[0, 2:3, :] * m
    xmid_ref[...] = x_mid
    h = _norm_mod(x_mid, g_ref[...], mod_ref[0, 4:5, :], mod_ref[0, 3:4, :])
    logits_t = _mdot(_split(rwt_ref[...], 3), _split(h, 3), _NT, order=2)
    bucket, w_lo, w_hi = _route(logits_t, rb_ref[...])
    wrows = jnp.concatenate([w_lo, w_hi, jnp.zeros((126, TM), F32)], axis=0)
    hrow_ref[...] = jnp.concatenate([h, jnp.transpose(wrows)], axis=1)
    brow = lax.broadcasted_iota(I32, (NB_PAD, TM), 0)
    onehot = (brow == bucket).astype(F32)
    r = lax.broadcasted_iota(I32, (TM, TM), 0)
    c = lax.broadcasted_iota(I32, (TM, TM), 1)
    before = (r < c).astype(F32).astype(BF16)
    cum = _dot(onehot.astype(BF16), before)
    base = cnt_scr[:, 0:1]
    rank = jnp.sum(onehot * (cum + base), axis=0, keepdims=True)
    bucket_ref[0] = bucket
    rank_ref[0] = rank.astype(I32)
    cnt = cnt_scr[...] + jnp.sum(onehot, axis=1, keepdims=True)
    cnt_scr[...] = cnt
    cnt_ref[...] = cnt.astype(I32)


def _out_proj(mixes, weights, x, mod, g_ffn, rwt, rb_col, t0, ntiles, nct):
    nmix = len(mixes)
    n = ntiles * TM
    tile_in = lambda w: pl.BlockSpec((TM, w), lambda i: (i + t0, 0))
    tile_out = lambda w: pl.BlockSpec((TM, w), lambda i: (i, 0))
    row_out = pl.BlockSpec((1, 1, TM), lambda i: (i, 0, 0))
    in_specs = [tile_in(mx.shape[1]) for mx in mixes] + [_full(w.shape) for w in weights] + [
        tile_in(D), pl.BlockSpec((1, 6, D), lambda i: (jnp.where(i + t0 < nct, 1, 0), 0, 0)),
        _full((1, D)), _full((16, D)), _full((16, 1))]
    return pl.pallas_call(
        functools.partial(_out_kernel, nmix),
        grid=(ntiles,),
        in_specs=in_specs,
        out_specs=[tile_out(D), tile_out(XROW), row_out, row_out, _full((NB_PAD, 128))],
        out_shape=[jax.ShapeDtypeStruct((n, D), F32), jax.ShapeDtypeStruct((n, XROW), F32),
                   jax.ShapeDtypeStruct((ntiles, 1, TM), I32), jax.ShapeDtypeStruct((ntiles, 1, TM), I32),
                   jax.ShapeDtypeStruct((NB_PAD, 128), I32)],
        scratch_shapes=[pltpu.VMEM((NB_PAD, 128), F32)],
        compiler_params=_cparams("arbitrary"),
        name="out_proj_router",
    )(*mixes, *weights, x, mod, g_ffn, rwt, rb_col)


def _dispatch_kernel(bucket_ref, rank_ref, off_ref, h_ref, dst_in_ref, dst_ref, sem):
    del dst_in_ref
    i = pl.program_id(0)

    def row_copy(r):
        t = i * TM + r
        dst = off_ref[bucket_ref[t]] + rank_ref[t]
        return pltpu.make_async_copy(h_ref.at[pl.ds(r, 1), :], dst_ref.at[pl.ds(dst, 1), :], sem)

    def start(r, carry):
        row_copy(r).start()
        return carry

    def wait(r, carry):
        row_copy(r).wait()
        return carry

    lax.fori_loop(0, TM, start, 0)
    lax.fori_loop(0, TM, wait, 0)


def _dispatch(hrow, bucket, rank, offsets, npad):
    n = hrow.shape[0]
    zeros = jnp.zeros((npad, XROW), F32)
    return pl.pallas_call(
        _dispatch_kernel,
        grid_spec=pltpu.PrefetchScalarGridSpec(
            num_scalar_prefetch=3,
            grid=(n // TM,),
            in_specs=[pl.BlockSpec((TM, XROW), lambda i, *_: (i, 0)), pl.BlockSpec(memory_space=pl.ANY)],
            out_specs=pl.BlockSpec(memory_space=pl.ANY),
            scratch_shapes=[pltpu.SemaphoreType.DMA(())],
        ),
        out_shape=jax.ShapeDtypeStruct((npad, XROW), F32),
        input_output_aliases={4: 0},
        compiler_params=_cparams("arbitrary"),
        name="moe_dispatch",
    )(bucket, rank, offsets, hrow, zeros)


def _moe_kernel(tea_ref, teb_ref, tblk_ref, tval_ref, x_ref, w1a_ref, w3a_ref, w2a_ref,
                w1b_ref, w3b_ref, w2b_ref, y_ref):
    del tea_ref, teb_ref, tblk_ref
    j = pl.program_id(0)

    @pl.when(tval_ref[j] != 0)
    def _():
        x = x_ref[...]
        hb = x[:, 0:D].astype(BF16)
        acts = []
        for w1_ref, w3_ref, lane in ((w1a_ref, w3a_ref, 0), (w1b_ref, w3b_ref, 1)):
            u = _dot(hb, w1_ref[0, 0].astype(BF16))
            v = _dot(hb, w3_ref[0, 0].astype(BF16))
            gate = x[:, D + lane:D + lane + 1]
            acts.append((_silu(u) * v * gate).astype(BF16))
        y_ref[...] = (_dot(acts[0], w2a_ref[0, 0].astype(BF16)) + _dot(acts[1], w2b_ref[0, 0].astype(BF16)))

    @pl.when(tval_ref[j] == 0)
    def _():
        y_ref[...] = jnp.zeros_like(y_ref)


def _moe(xs_sorted, w1, w3, w2, layer, tile_ea, tile_eb, tile_blk, tile_valid):
    npad = xs_sorted.shape[0]
    ntile = npad // TMM
    wspec = lambda shape, which: pl.BlockSpec(
        (1, 1) + shape, lambda j, ea, eb, blk, val: (layer, (ea, eb)[which][j], 0, 0))
    up, down = (D, D_EXPERT), (D_EXPERT, D)
    return pl.pallas_call(
        _moe_kernel,
        grid_spec=pltpu.PrefetchScalarGridSpec(
            num_scalar_prefetch=4,
            grid=(ntile,),
            in_specs=[pl.BlockSpec((TMM, XROW), lambda j, ea, eb, blk, val: (blk[j], 0)),
                      wspec(up, 0), wspec(up, 0), wspec(down, 0), wspec(up, 1), wspec(up, 1), wspec(down, 1)],
            out_specs=pl.BlockSpec((TMM, D), lambda j, ea, eb, blk, val: (j, 0)),
        ),
        out_shape=jax.ShapeDtypeStruct((npad, D), F32),
        compiler_params=_cparams("arbitrary"),
        name="moe_experts",
    )(tile_ea, tile_eb, tile_blk, tile_valid, xs_sorted, w1, w3, w2, w1, w3, w2)


def _combine_kernel(ntiles, bucket_ref, rank_ref, off_ref, x_ref, mod_ref, ys_ref, o_ref, ybuf, sem):
    i = pl.program_id(0)

    def row_copy(tile, slot, r):
        t = tile * TM + r
        src = off_ref[bucket_ref[t]] + rank_ref[t]
        return pltpu.make_async_copy(ys_ref.at[pl.ds(src, 1), :], ybuf.at[slot, pl.ds(r, 1), :], sem.at[slot])

    def start_tile(tile, slot):
        def body(r, carry):
            row_copy(tile, slot, r).start()
            return carry
        lax.fori_loop(0, TM, body, 0)

    @pl.when(i == 0)
    def _():
        start_tile(0, 0)

    @pl.when(i + 1 < ntiles)
    def _():
        start_tile(i + 1, (i + 1) % 2)

    slot = i % 2

    def wait_body(r, carry):
        row_copy(i, slot, r).wait()
        return carry

    lax.fori_loop(0, TM, wait_body, 0)
    o_ref[...] = x_ref[...] + mod_ref[0, 5:6, :] * ybuf[slot]


def _combine(x_mid, mod, ys, bucket, rank, offsets, t0, nct):
    n = x_mid.shape[0]
    ntiles = n // TM
    return pl.pallas_call(
        functools.partial(_combine_kernel, ntiles),
        grid_spec=pltpu.PrefetchScalarGridSpec(
            num_scalar_prefetch=3,
            grid=(ntiles,),
            in_specs=[pl.BlockSpec((TM, D), lambda i, *_: (i, 0)),
                      pl.BlockSpec((1, 6, D), lambda i, *_: (jnp.where(i + t0 < nct, 1, 0), 0, 0)),
                      pl.BlockSpec(memory_space=pl.ANY)],
            out_specs=pl.BlockSpec((TM, D), lambda i, *_: (i, 0)),
            scratch_shapes=[pltpu.VMEM((2, TM, D), F32), pltpu.SemaphoreType.DMA((2,))],
        ),
        out_shape=jax.ShapeDtypeStruct((n, D), F32),
        compiler_params=_cparams("arbitrary"),
        name="moe_combine",
    )(bucket, rank, offsets, x_mid, mod, ys)


def _moe_block(x_mid, hrow, bucket, rank, counts, mod, w1, w3, w2, layer, t0, nct):
    n = x_mid.shape[0]
    ntile = n // TMM + N_BUCKETS
    npad = ntile * TMM
    cnt = counts[:N_BUCKETS, 0]
    padded = ((cnt + TMM - 1) // TMM) * TMM
    ends = jnp.cumsum(padded)
    offsets = jnp.zeros((NB_PAD,), I32).at[:N_BUCKETS].set(ends - padded)
    total_tiles = ends[-1] // TMM
    tiles = jnp.arange(ntile, dtype=I32)
    tile_valid = (tiles < total_tiles).astype(I32)
    tile_blk = jnp.minimum(tiles, jnp.maximum(total_tiles - 1, 0))
    tile_bucket = jnp.minimum(jnp.searchsorted(ends, tile_blk * TMM, side="right"), N_BUCKETS - 1).astype(I32)
    pair = tile_bucket % 6
    grp = tile_bucket // 6
    tile_ea = (4 * grp + jnp.asarray(_PAIR_A, I32)[pair]).astype(I32)
    tile_eb = (4 * grp + jnp.asarray(_PAIR_B, I32)[pair]).astype(I32)
    bucket = bucket.reshape(-1)
    rank = rank.reshape(-1)
    xs_sorted = _dispatch(hrow, bucket, rank, offsets, npad)
    ys = _moe(xs_sorted, w1, w3, w2, layer, tile_ea, tile_eb, tile_blk, tile_valid)
    return _combine(x_mid, mod, ys, bucket, rank, offsets, t0, nct)


def kernel(x, c, ctx, c_ctx, router_w, router_b, norm_mix, norm_ffn, w_mod, b_mod, ev_w_in, ev_conv_w, ev_conv_b, ev_dt_bias, ev_a_log, ev_d_skip, ev_ssd_norm, ev_q_norm, ev_k_norm, ev_sink, ev_w_out, od_w_in, od_conv_w, od_conv_b, od_igate_b, od_fgate_b, od_head_norm, od_w_out, moe_w1, moe_w3, moe_w2):
    s_len = x.shape[1]
    c_len = ctx.shape[1]
    assert x.shape[0] == 1 and s_len % TM == 0 and c_len % TM == 0 and s_len % GRID_W == 0
    nct = c_len // TM
    ncc, nlc = c_len // T, s_len // T
    nt = c_len + s_len
    ntiles = nt // TM

    mod = _modulation(c, c_ctx, w_mod, b_mod)
    rwt = router_w.T
    rb_col = router_b.reshape(N_EXPERTS, 1)
    xcat = jnp.concatenate([ctx[0], x[0]], axis=0)

    w = ev_w_in[0]
    cos, sin = _rope_tables(c_len, s_len)
    p0 = dict(
        g_mix=norm_mix[0].reshape(1, D),
        wcat=jnp.concatenate([w[:, 0:1280], w[:, 1296:2064]], axis=1).astype(BF16),
        wdt=w[:, 1280:1296], wdt_t=w[:, 1280:1296].T,
        conv_w=ev_conv_w[0], conv_b=ev_conv_b[0].reshape(1, 768),
        dtb_row=ev_dt_bias[0].reshape(1, 16), dtb_col=ev_dt_bias[0].reshape(16, 1),
        q_norm=jnp.tile(ev_q_norm[0], 8).reshape(1, 512), k_norm=jnp.tile(ev_k_norm[0], 2).reshape(1, 128),
        cos=cos, sin=sin,
        alog_row=ev_a_log[0].reshape(1, 16), alog_col=ev_a_log[0].reshape(16, 1),
        d_skip=jnp.repeat(ev_d_skip[0], 64).reshape(1, 512), ssd_norm=ev_ssd_norm[0].reshape(1, 512))
    z, xs, bc, q, kk, vv, dtc, dtr = _in0(xcat, mod[0], p0, nct)
    yf = _ssd(False, xs, bc, dtc, dtr, p0, ncc, nlc)
    ymix = _ssd(True, xs, bc, dtc, dtr, p0, ncc, nlc, yf=yf, z=z)
    att = _attention(q, kk, vv, ev_sink[0].reshape(1, 8), ncc, nlc)
    wo = ev_w_out[0].astype(BF16)
    x_mid, hrow, bucket, rank, counts = _out_proj(
        [ymix, att], [wo[0:512], wo[512:1024]], xcat, mod[0], norm_ffn[0].reshape(1, D), rwt, rb_col, 0, ntiles, nct)
    xcat = _moe_block(x_mid, hrow, bucket, rank, counts, mod[0], moe_w1, moe_w3, moe_w2, 0, 0, nct)

    w = od_w_in[0]
    gb = jnp.concatenate([od_igate_b[0].reshape(16), od_fgate_b[0].reshape(16)])
    p1 = dict(
        g_mix=norm_mix[1].reshape(1, D),
        wcat=w[:, 0:3072].astype(BF16),
        wg=w[:, 3072:3104], wg_t=w[:, 3072:3104].T,
        conv_w=od_conv_w[0], conv_b=od_conv_b[0].reshape(1, 2048),
        gb_row=gb.reshape(1, 32), gb_col=gb.reshape(32, 1),
        head_norm=od_head_norm[0].reshape(1, 1024))
    q1, kt1, v1, og1, gc1, gr1 = _in1(xcat, mod[1], p1, nct)
    hf = _mlstm(False, q1, kt1, v1, gc1, gr1, p1, ncc, nlc)
    hmix = _mlstm(True, q1, kt1, v1, gc1, gr1, p1, ncc, nlc, hf=hf, og=og1)
    x_mid, hrow, bucket, rank, counts = _out_proj(
        [hmix], [od_w_out[0].astype(BF16)], xcat, mod[1], norm_ffn[1].reshape(1, D), rwt, rb_col,
        nct, ntiles - nct, nct)
    out = _moe_block(x_mid, hrow, bucket, rank, counts, mod[1], moe_w1, moe_w3, moe_w2, 1, nct, nct)
    return out[None]
```

```python
import functools
import math

import jax
import jax.numpy as jnp
from jax import lax
from jax.experimental import pallas as pl
from jax.experimental.pallas import tpu as pltpu

F32 = jnp.float32
BF16 = jnp.bfloat16
I32 = jnp.int32

EPS = 1e-6
D = 1024
T = 128
TM = 256
TMM = 256
GRID_W = 64
ROPE_THETA = 10000.0
N_EXPERTS = 16
N_BUCKETS = 24
NB_PAD = 32
D_EXPERT = 512
XROW = D + 128
NEG_INF = float("-inf")
VMEM_LIMIT = 56 * 1024 * 1024

_NN = (((1,), (0,)), ((), ()))
_NT = (((1,), (1,)), ((), ()))
_TN = (((0,), (0,)), ((), ()))

_PAIR_A = (0, 0, 0, 1, 1, 2)
_PAIR_B = (1, 2, 3, 2, 3, 3)


def _cparams(*sem):
    return pltpu.CompilerParams(dimension_semantics=sem, vmem_limit_bytes=VMEM_LIMIT)


def _dot(a, b, dims=_NN):
    return lax.dot_general(a, b, dims, preferred_element_type=F32)


def _split(a, n):
    out = []
    r = a
    for _ in range(n):
        t = r.astype(BF16)
        out.append(t)
        r = r - t.astype(F32)
    return out


def _mdot(as_, bs, dims=_NN, order=None):
    if order is None:
        order = len(as_) + len(bs) - 2
    acc = None
    for i, a in enumerate(as_):
        for j, b in enumerate(bs):
            if i + j <= order:
                p = _dot(a, b, dims)
                acc = p if acc is None else acc + p
    return acc


def _sigmoid(x):
    return 1.0 / (1.0 + jnp.exp(-x))


def _silu(x):
    return x * _sigmoid(x)


def _log1p_exp_neg_abs(x):
    e = jnp.exp(-jnp.abs(x))
    u = 1.0 + e
    um1 = u - 1.0
    return jnp.where(um1 == 0.0, e, jnp.log(u) * (e / jnp.where(um1 == 0.0, 1.0, um1)))


def _softplus(x):
    return jnp.maximum(x, 0.0) + _log1p_exp_neg_abs(x)


def _log_sigmoid(x):
    return jnp.minimum(x, 0.0) - _log1p_exp_neg_abs(x)


def _norm_mod(x, g, sc, sh):
    ms = jnp.mean(x * x, axis=-1, keepdims=True)
    return (x * lax.rsqrt(ms + EPS)) * g * (1.0 + sc) + sh


def _tri(rev):
    r = lax.broadcasted_iota(I32, (T, T), 0)
    c = lax.broadcasted_iota(I32, (T, T), 1)
    return (c >= r) if rev else (c <= r)


def _cumsums(rev, col, row):
    tri = _tri(rev)
    tri_b = tri.astype(F32).astype(BF16)
    trit_b = _tri(not rev).astype(F32).astype(BF16)
    ccol = _mdot([tri_b], _split(col, 3))
    crow = _mdot(_split(row, 3), [trit_b])
    return tri, ccol, crow


def _mod_kernel(c_ref, w_ref, b_ref, o_ref):
    a = _silu(c_ref[...])
    o_ref[0] = _mdot(_split(a, 2), _split(w_ref[0], 2), order=1) + b_ref[0]


def _modulation(c, c_ctx, w_mod, b_mod):
    depth = w_mod.shape[0]
    n = w_mod.shape[2]
    tn = 1536
    cc = jnp.zeros((8, D), F32).at[0].set(c[0]).at[1].set(c_ctx)
    out = pl.pallas_call(
        _mod_kernel,
        grid=(depth, n // tn),
        in_specs=[
            pl.BlockSpec((8, D), lambda l, j: (0, 0)),
            pl.BlockSpec((1, D, tn), lambda l, j: (l, 0, j)),
            pl.BlockSpec((1, 1, tn), lambda l, j: (l, 0, j)),
        ],
        out_specs=pl.BlockSpec((1, 8, tn), lambda l, j: (l, 0, j)),
        out_shape=jax.ShapeDtypeStruct((depth, 8, n), F32),
        compiler_params=_cparams("arbitrary", "arbitrary"),
        name="modulation",
    )(cc, w_mod, b_mod.reshape(depth, 1, n))
    return out[:, :2].reshape(depth, 2, 6, D)


def _halo_specs(nrows, tile_of):
    nb8 = nrows // 8
    return [
        pl.BlockSpec((8, D), lambda i: (jnp.maximum(tile_of(i) * (TM // 8) - 1, 0), 0)),
        pl.BlockSpec((TM, D), lambda i: (tile_of(i), 0)),
        pl.BlockSpec((8, D), lambda i: (jnp.minimum((tile_of(i) + 1) * (TM // 8), nb8 - 1), 0)),
    ]


def _ctx_tile(nct):
    return lambda i: jnp.minimum(i, nct - 1)


def _lat_tile(nct):
    return lambda i: jnp.maximum(i - nct, 0)


def _mod_spec(nct, t0=0):
    return pl.BlockSpec((1, 6, D), lambda i: (jnp.where(i + t0 < nct, 1, 0), 0, 0))


def _full(shape):
    nd = len(shape)
    return pl.BlockSpec(shape, lambda i: (0,) * nd)


def _seq_edges(i, nct, ntiles):
    prev_ok = jnp.logical_and(i != 0, i != nct).astype(F32)
    next_ok = jnp.logical_and(i != nct - 1, i != ntiles - 1).astype(F32)
    return prev_ok, next_ok


def _conv_silu(x, x_first_prev, x_last_next, cw, cb):
    n = x.shape[0]
    rows = lax.broadcasted_iota(I32, x.shape, 0)
    x_prev = jnp.where(rows == 0, x_first_prev, pltpu.roll(x, 1, 0))
    x_next = jnp.where(rows == n - 1, x_last_next, pltpu.roll(x, n - 1, 0))
    return _silu(x_prev * cw[0:1] + x * cw[1:2] + x_next * cw[2:3] + cb)


def _hi_lo_cols(blk, lo_pass, n):
    return blk + pltpu.roll(blk, 128 - n, 1) + lo_pass


def _hi_lo_weight(w):
    n = w.shape[1]
    hi = w.astype(BF16)
    lo = (w - hi.astype(F32)).astype(BF16)
    z = jnp.zeros((w.shape[0], 128 - 2 * n), BF16)
    return jnp.concatenate([hi, lo, z], axis=1), jnp.concatenate([hi, jnp.zeros_like(lo), z], axis=1)


def _head_rms(xf, gamma):
    r = lax.broadcasted_iota(I32, (128, 128), 0) // 64
    c = lax.broadcasted_iota(I32, (128, 128), 1) // 64
    ones_bd = (r == c).astype(F32).astype(BF16)
    outs = []
    for j in range(xf.shape[1] // 128):
        blk = xf[:, 128 * j:128 * (j + 1)]
        ssum = _mdot(_split(blk * blk, 2), [ones_bd])
        outs.append(blk * lax.rsqrt(ssum * (1.0 / 64.0) + EPS))
    return jnp.concatenate(outs, axis=1) * gamma


def _rope(xf, cos, sin):
    lane = lax.broadcasted_iota(I32, (xf.shape[0], 128), 1)
    first = (lane % 32) < 16
    outs = []
    for j in range(xf.shape[1] // 128):
        blk = xf[:, 128 * j:128 * (j + 1)]
        partner = jnp.where(first, pltpu.roll(blk, 112, 1), pltpu.roll(blk, 16, 1))
        outs.append(blk * cos + partner * sin)
    return jnp.concatenate(outs, axis=1)


def _rope_tables(s_len):
    rows = s_len // GRID_W
    inv = ROPE_THETA ** (-(jnp.arange(32, dtype=I32) % 16).astype(F32) / 16.0)
    sign = jnp.where(jnp.arange(32) < 16, -1.0, 1.0).astype(F32)
    ang_r = jnp.arange(rows, dtype=F32)[:, None] * inv[None, :]
    ang_c = jnp.arange(GRID_W, dtype=F32)[:, None] * inv[None, :]

    def table(fr, fc):
        r = jnp.broadcast_to(fr[:, None, :], (rows, GRID_W, 32))
        c = jnp.broadcast_to(fc[None, :, :], (rows, GRID_W, 32))
        t = jnp.concatenate([r, c, r, c], axis=-1)
        return t.reshape(s_len, 128)

    return table(jnp.cos(ang_r), jnp.cos(ang_c)), table(jnp.sin(ang_r) * sign, jnp.sin(ang_c) * sign)


def _in0_kernel(nct, ntiles, cp_ref, c_ref, cn_ref, xp_ref, x_ref, xn_ref, mod_ref, g_ref, wcat_ref, wdth_ref,
                cw_ref, cb_ref, dtb_ref, qn_ref, kn_ref, cos_ref, sin_ref,
                z_ref, xs_ref, bc_ref, q_ref, kk_ref, vv_ref, dtc_ref, dtr_ref):
    i = pl.program_id(0)
    is_ctx = i < nct
    sh = mod_ref[0, 0:1, :]
    sc = mod_ref[0, 1:2, :]
    g = g_ref[...]
    h = _norm_mod(jnp.where(is_ctx, c_ref[...], x_ref[...]), g, sc, sh)
    hb = h.astype(BF16)
    main = _dot(hb, wcat_ref[...])
    prev_ok, next_ok = _seq_edges(i, nct, ntiles)
    wxbc = wcat_ref[:, 512:1280]
    hp = _norm_mod(jnp.where(is_ctx, cp_ref[...], xp_ref[...]), g, sc, sh).astype(BF16)
    hn = _norm_mod(jnp.where(is_ctx, cn_ref[...], xn_ref[...]), g, sc, sh).astype(BF16)
    xb_prev = _dot(hp, wxbc)[7:8, :] * prev_ok
    xb_next = _dot(hn, wxbc)[0:1, :] * next_ok
    act = _conv_silu(main[:, 512:1280], xb_prev, xb_next, cw_ref[...], cb_ref[...])
    z_ref[...] = main[:, 0:512].astype(BF16)
    xs_ref[...] = act[:, 0:512].astype(BF16)
    bc_ref[...] = act[:, 512:768].astype(BF16)
    cos = jnp.where(is_ctx, 1.0, cos_ref[...])
    sin = jnp.where(is_ctx, 0.0, sin_ref[...])
    q = _rope(_head_rms(main[:, 1280:1792], qn_ref[...]), cos, sin) * 0.125
    q_ref[...] = q.astype(BF16)
    k = _rope(_head_rms(main[:, 1792:1920], kn_ref[...]), cos, sin)
    kk_ref[...] = jnp.concatenate([k, pltpu.roll(k, 64, 1)], axis=1).astype(BF16)
    v = main[:, 1920:2048]
    vv_ref[...] = jnp.concatenate([v, pltpu.roll(v, 64, 1)], axis=1).astype(BF16)
    h_lo = (h - hb.astype(F32)).astype(BF16)
    dt = _softplus(_hi_lo_cols(main[:, 2048:2176], _dot(h_lo, wdth_ref[...]), 16) + dtb_ref[...])
    dtc_ref[...] = dt[:, 0:16]
    dtr_ref[...] = jnp.transpose(dt)[0:16, :]


def _in0(ctx2, x2, mod, p, nct):
    c_len, s_len = ctx2.shape[0], x2.shape[0]
    nt = c_len + s_len
    ntiles = nt // TM
    tile = lambda w: pl.BlockSpec((TM, w), lambda i: (i, 0))
    lat = _lat_tile(nct)
    lat_tile = lambda w: pl.BlockSpec((TM, w), lambda i: (lat(i), 0))
    outs = [(512, BF16), (512, BF16), (256, BF16), (512, BF16), (256, BF16), (256, BF16), (16, F32)]
    return pl.pallas_call(
        functools.partial(_in0_kernel, nct, ntiles),
        grid=(ntiles,),
        in_specs=_halo_specs(c_len, _ctx_tile(nct)) + _halo_specs(s_len, lat) + [
            _mod_spec(nct), _full((1, D)), _full((D, 2176)), _full((D, 128)),
            _full((3, 768)), _full((1, 768)), _full((1, 128)),
            _full((1, 512)), _full((1, 128)), lat_tile(128), lat_tile(128)],
        out_specs=[tile(w) for w, _ in outs] + [pl.BlockSpec((16, TM), lambda i: (0, i))],
        out_shape=[jax.ShapeDtypeStruct((nt, w), dt) for w, dt in outs]
        + [jax.ShapeDtypeStruct((16, nt), F32)],
        compiler_params=_cparams("arbitrary"),
        name="in_proj_even",
    )(ctx2, ctx2, ctx2, x2, x2, x2, mod, p["g_mix"], p["wcat"], p["wdt_hi"], p["conv_w"], p["conv_b"],
      p["dt_bias"], p["q_norm"], p["k_norm"], p["cos"], p["sin"])


def _scan_chunk_map(rev, ncc, nlc):
    if not rev:
        return lambda j: j
    return lambda j: jnp.where(j < ncc, ncc - 1 - j, ncc + nlc - 1 - (j - ncc))


def _ssd_kernel(rev, *refs):
    if rev:
        (xs_ref, bc_ref, dtc_ref, dtr_ref, alr_ref, alc_ref, yf_ref, z_ref, dsk_ref, nrm_ref,
         o_ref, st_ref) = refs
    else:
        xs_ref, bc_ref, dtc_ref, dtr_ref, alr_ref, alc_ref, o_ref, st_ref = refs
    j = pl.program_id(0)

    @pl.when(j == 0)
    def _():
        st_ref[...] = jnp.zeros_like(st_ref)

    d = 8 if rev else 0
    a_coef_row = -jnp.exp(alr_ref[...])[:, d:d + 8]
    a_coef_col = -jnp.exp(alc_ref[...])[d:d + 8, :]
    dtc = dtc_ref[:, d:d + 8]
    dtr = dtr_ref[d:d + 8, :]
    tri, acs_col, acs_row = _cumsums(rev, dtc * a_coef_row, dtr * a_coef_col)
    end = 0 if rev else T - 1
    atot_row = acs_col[end:end + 1, :]
    atot_col = acs_row[:, end:end + 1]
    dec_col = jnp.exp(atot_row - acs_col) * dtc
    eacs_col = jnp.exp(acs_col)
    xs = xs_ref[...]
    bm = bc_ref[:, 0:128]
    cm = bc_ref[:, 128:256]
    lane = lax.broadcasted_iota(I32, (T, 128), 1)
    lo = lane < 64
    zero_b = jnp.zeros((T, 128), BF16)
    ys = [None] * 4
    for g in range(2):
        in_g = lo if g == 0 else jnp.logical_not(lo)
        cg = jnp.where(in_g, cm, zero_b)
        bg = jnp.where(in_g, bm, zero_b)
        cb = _dot(cg, bm, _NT)
        for hh in range(4):
            hd = 4 * g + hh
            pair, hi = hd // 2, hd % 2
            diff = acs_col[:, hd:hd + 1] - acs_row[hd:hd + 1, :]
            lmat = jnp.exp(jnp.where(tri, diff, NEG_INF))
            m = (cb * lmat * dtr[hd:hd + 1, :]).astype(BF16)
            xp = xs[:, 128 * pair:128 * (pair + 1)]
            xh = jnp.where(lo if hi == 0 else jnp.logical_not(lo), xp, zero_b)
            st = st_ref[hd]
            y = _dot(m, xh) + _dot(cg, st.astype(BF16)) * eacs_col[:, hd:hd + 1]
            ys[pair] = y if ys[pair] is None else ys[pair] + y
            xdec = (xh.astype(F32) * dec_col[:, hd:hd + 1]).astype(BF16)
            st_ref[hd] = jnp.exp(atot_col[hd:hd + 1, :]) * st + _dot(bg, xdec, _TN)
    y = jnp.concatenate(ys, axis=1)
    if not rev:
        o_ref[...] = y
    else:
        ytot = y + yf_ref[...] + dsk_ref[...] * xs.astype(F32)
        gated = ytot * _silu(z_ref[...].astype(F32))
        ms = jnp.mean(gated * gated, axis=-1, keepdims=True)
        o_ref[...] = (gated * lax.rsqrt(ms + EPS) * nrm_ref[...]).astype(BF16)


def _ssd(rev, xs, bc, dtc, dtr, p, ncc, nlc, yf=None, z=None):
    nt = xs.shape[0]
    cmap = _scan_chunk_map(rev, ncc, nlc)
    blk = lambda w: pl.BlockSpec((T, w), lambda j: (cmap(j), 0))
    in_specs = [blk(512), blk(256), blk(16), pl.BlockSpec((16, T), lambda j: (0, cmap(j))),
                _full((1, 16)), _full((16, 1))]
    args = [xs, bc, dtc, dtr, p["alog_row"], p["alog_col"]]
    if rev:
        in_specs += [blk(512), blk(512), _full((1, 512)), _full((1, 512))]
        args += [yf, z, p["d_skip"], p["ssd_norm"]]
    return pl.pallas_call(
        functools.partial(_ssd_kernel, rev),
        grid=(ncc + nlc,),
        in_specs=in_specs,
        out_specs=blk(512),
        out_shape=jax.ShapeDtypeStruct((nt, 512), BF16 if rev else F32),
        scratch_shapes=[pltpu.VMEM((8, 128, 128), F32)],
        compiler_params=_cparams("arbitrary"),
        name="ssd_bwd" if rev else "ssd_fwd",
    )(*args)


def _attn_kernel(ncc, nblk, q_ref, kp_ref, kc_ref, kn_ref, vp_ref, vc_ref, vn_ref, kx_ref, vx_ref,
                 sink_ref, o_ref):
    j = pl.program_id(0)
    c_len = kx_ref.shape[0]
    is_lat = j >= ncc
    prev_ok = jnp.logical_and(is_lat, j >= ncc + 1)
    next_ok = jnp.logical_and(is_lat, j <= nblk - 2)
    r = lax.broadcasted_iota(I32, (T, T), 0)
    c = lax.broadcasted_iota(I32, (T, T), 1)
    zero = jnp.zeros((T, T), F32)
    ninf = jnp.full((T, T), NEG_INF, F32)
    bias = jnp.concatenate([
        jnp.where(jnp.logical_and(prev_ok, c >= r), zero, ninf),
        jnp.where(is_lat, zero, ninf),
        jnp.where(jnp.logical_and(next_ok, c <= r), zero, ninf),
        jnp.zeros((T, c_len), F32)], axis=1)
    k_all = jnp.concatenate([kp_ref[...], kc_ref[...], kn_ref[...], kx_ref[...]], axis=0)
    v_all = jnp.concatenate([vp_ref[...], vc_ref[...], vn_ref[...], vx_ref[...]], axis=0)
    nk = k_all.shape[0]
    lane = lax.broadcasted_iota(I32, (nk, 128), 1)
    lo = lane < 64
    zero_b = jnp.zeros((nk, 128), BF16)
    q = q_ref[...]
    sink = sink_ref[...]
    for g in range(2):
        k_lo = jnp.where(lo, k_all[:, 128 * g:128 * (g + 1)], zero_b)
        k_hi = jnp.where(lo, zero_b, k_all[:, 128 * (1 - g):128 * (2 - g)])
        v_lo = jnp.where(lo, v_all[:, 128 * g:128 * (g + 1)], zero_b)
        v_hi = jnp.where(lo, zero_b, v_all[:, 128 * (1 - g):128 * (2 - g)])
        for pp in range(2):
            pair = 2 * g + pp
            qp = q[:, 128 * pair:128 * (pair + 1)]
            acc = None
            for hi in range(2):
                hd = 2 * pair + hi
                s = _dot(qp, k_hi if hi else k_lo, _NT) + bias
                sk = sink[:, hd:hd + 1]
                m = jnp.maximum(jnp.max(s, axis=-1, keepdims=True), sk)
                pr = jnp.exp(s - m)
                den = jnp.sum(pr, axis=-1, keepdims=True) + jnp.exp(sk - m)
                o = _dot(pr.astype(BF16), v_hi if hi else v_lo) / den
                acc = o if acc is None else acc + o
            o_ref[:, 128 * pair:128 * (pair + 1)] = acc.astype(BF16)


def _attention(q, kk, vv, sink, ncc, nlc):
    nt = q.shape[0]
    nblk = ncc + nlc
    c_len = ncc * T
    prev = lambda w: pl.BlockSpec((T, w), lambda j: (jnp.maximum(j - 1, 0), 0))
    cur = lambda w: pl.BlockSpec((T, w), lambda j: (j, 0))
    nxt = lambda w: pl.BlockSpec((T, w), lambda j: (jnp.minimum(j + 1, nblk - 1), 0))
    ctx = lambda w: pl.BlockSpec((c_len, w), lambda j: (0, 0))
    return pl.pallas_call(
        functools.partial(_attn_kernel, ncc, nblk),
        grid=(nblk,),
        in_specs=[cur(512), prev(256), cur(256), nxt(256), prev(256), cur(256), nxt(256),
                  ctx(256), ctx(256), _full((1, 8))],
        out_specs=cur(512),
        out_shape=jax.ShapeDtypeStruct((nt, 512), BF16),
        compiler_params=_cparams("arbitrary"),
        name="window_attention",
    )(q, kk, kk, kk, vv, vv, vv, kk, vv, sink)


def _in1_kernel(nct, ntiles, xp_ref, x_ref, xn_ref, yp_ref, y_ref, yn_ref, modp_ref, mod_ref, g_ref,
                wcat_ref, wgh_ref, cw_ref, cb_ref, gb_ref,
                x1_ref, q_ref, kt_ref, v_ref, o_ref, gc_ref, gr_ref):
    i = pl.program_id(0)
    gf = modp_ref[0, 5:6, :]
    sh = mod_ref[0, 0:1, :]
    sc = mod_ref[0, 1:2, :]
    g = g_ref[...]
    x1 = x_ref[...] + gf * y_ref[...]
    x1_ref[...] = x1
    h = _norm_mod(x1, g, sc, sh)
    hb = h.astype(BF16)
    main = _dot(hb, wcat_ref[...])
    prev_ok, next_ok = _seq_edges(i, nct, ntiles)
    wqkv = wcat_ref[:, 0:2048]
    hp = _norm_mod(xp_ref[...] + gf * yp_ref[...], g, sc, sh).astype(BF16)
    hn = _norm_mod(xn_ref[...] + gf * yn_ref[...], g, sc, sh).astype(BF16)
    x_prev = _dot(hp, wqkv)[7:8, :] * prev_ok
    x_next = _dot(hn, wqkv)[0:1, :] * next_ok
    act = _conv_silu(main[:, 0:2048], x_prev, x_next, cw_ref[...], cb_ref[...])
    q_ref[...] = act[:, 0:512].astype(BF16)
    kt_ref[...] = jnp.transpose(act[:, 512:1024] * 0.125).astype(BF16)
    v_ref[...] = act[:, 1024:2048].astype(BF16)
    o_ref[...] = main[:, 2048:3072].astype(BF16)
    h_lo = (h - hb.astype(F32)).astype(BF16)
    gates = _hi_lo_cols(main[:, 3072:3200], _dot(h_lo, wgh_ref[...]), 32) + gb_ref[...]
    lane = lax.broadcasted_iota(I32, gates.shape, 1)
    gates = jnp.where(lane < 16, gates, _log_sigmoid(gates))
    gc_ref[...] = gates[:, 0:32]
    gr_ref[...] = jnp.transpose(gates)[0:32, :]


def _in1(x_mid, y_nat, mod_prev, mod, p, nct):
    nt = x_mid.shape[0]
    ntiles = nt // TM
    tile = lambda w: pl.BlockSpec((TM, w), lambda i: (i, 0))
    ident = lambda i: i
    return pl.pallas_call(
        functools.partial(_in1_kernel, nct, ntiles),
        grid=(ntiles,),
        in_specs=_halo_specs(nt, ident) + _halo_specs(nt, ident) + [
            _mod_spec(nct), _mod_spec(nct), _full((1, D)), _full((D, 3200)), _full((D, 128)),
            _full((3, 2048)), _full((1, 2048)), _full((1, 128))],
        out_specs=[tile(D), tile(512), pl.BlockSpec((512, TM), lambda i: (0, i)), tile(1024), tile(1024),
                   tile(32), pl.BlockSpec((32, TM), lambda i: (0, i))],
        out_shape=[jax.ShapeDtypeStruct((nt, D), F32),
                   jax.ShapeDtypeStruct((nt, 512), BF16), jax.ShapeDtypeStruct((512, nt), BF16),
                   jax.ShapeDtypeStruct((nt, 1024), BF16), jax.ShapeDtypeStruct((nt, 1024), BF16),
                   jax.ShapeDtypeStruct((nt, 32), F32), jax.ShapeDtypeStruct((32, nt), F32)],
        compiler_params=_cparams("arbitrary"),
        name="in_proj_odd",
    )(x_mid, x_mid, x_mid, y_nat, y_nat, y_nat, mod_prev, mod, p["g_mix"], p["wcat"], p["wg_hi"],
      p["conv_w"], p["conv_b"], p["gate_bias"])


def _mlstm_kernel(rev, *refs):
    if rev:
        (q_ref, kt_ref, v_ref, gc_ref, gr_ref, hf_ref, og_ref, hn_ref, o_ref,
         c_ref, mc_ref, mr_ref) = refs
    else:
        q_ref, kt_ref, v_ref, gc_ref, gr_ref, o_ref, c_ref, mc_ref, mr_ref = refs
    j = pl.program_id(0)

    @pl.when(j == 0)
    def _():
        c_ref[...] = jnp.zeros_like(c_ref)
        mc_ref[...] = jnp.zeros_like(mc_ref)
        mr_ref[...] = jnp.zeros_like(mr_ref)

    d = 8 if rev else 0
    ig_col = gc_ref[:, d:d + 8]
    lf_col = gc_ref[:, 16 + d:24 + d]
    ig_row = gr_ref[d:d + 8, :]
    lf_row = gr_ref[16 + d:24 + d, :]
    tri, b_col, b_row = _cumsums(rev, lf_col, lf_row)
    end = 0 if rev else T - 1
    blast_row = b_col[end:end + 1, :]
    blast_col = b_row[:, end:end + 1]
    wend_row = blast_col - b_row + ig_row
    ac_col = jnp.max(wend_row, axis=1, keepdims=True)
    eend_row = jnp.exp(wend_row - ac_col)
    ac_row = jnp.max(blast_row - b_col + ig_col, axis=0, keepdims=True)
    m_col = mc_ref[:, 0:1]
    m_row = mr_ref[0:1, 0:8]
    mnew_col = jnp.maximum(blast_col + m_col, ac_col)
    sp_col = jnp.exp(blast_col + m_col - mnew_col)
    sc_col = jnp.exp(ac_col - mnew_col)
    mnew_row = jnp.maximum(blast_row + m_row, ac_row)
    g_col = b_col + m_row
    q = q_ref[...]
    ones_b = jnp.ones((T, 128), BF16)
    sub = lax.broadcasted_iota(I32, (128, T), 0)
    zero_k = jnp.zeros((128, T), BF16)
    for hd in range(8):
        pair, hi = hd // 2, hd % 2
        qp = q[:, 128 * pair:128 * (pair + 1)]
        ktp = kt_ref[128 * pair:128 * (pair + 1), :]
        kth = jnp.where((sub >= 64) if hi else (sub < 64), ktp, zero_k)
        vaug = jnp.concatenate([v_ref[:, 128 * hd:128 * (hd + 1)], ones_b], axis=1)
        dlog = jnp.where(tri, b_col[:, hd:hd + 1] - b_row[hd:hd + 1, :] + ig_row[hd:hd + 1, :], NEG_INF)
        gh = g_col[:, hd:hd + 1]
        mstar = jnp.maximum(gh, jnp.max(dlog, axis=-1, keepdims=True))
        w = (jnp.exp(dlog - mstar) * _dot(qp, kth)).astype(BF16)
        cst = c_ref[hd]
        nd = _dot(w, vaug) + jnp.exp(gh - mstar) * _dot(qp, cst.astype(BF16))
        den = jnp.maximum(jnp.abs(nd[:, 128:256]), jnp.exp(-mstar))
        hh = nd[:, 0:128] / den
        if rev:
            hh = hh + hf_ref[:, 128 * hd:128 * (hd + 1)]
            ms = jnp.mean(hh * hh, axis=-1, keepdims=True)
            hh = hh * lax.rsqrt(ms + EPS) * hn_ref[:, 128 * hd:128 * (hd + 1)]
            og = og_ref[:, 128 * hd:128 * (hd + 1)].astype(F32)
            o_ref[:, 128 * hd:128 * (hd + 1)] = (hh * _sigmoid(og)).astype(BF16)
        else:
            o_ref[:, 128 * hd:128 * (hd + 1)] = hh
        kte = (kth.astype(F32) * eend_row[hd:hd + 1, :]).astype(BF16)
        c_ref[hd] = sp_col[hd:hd + 1, :] * cst + sc_col[hd:hd + 1, :] * _dot(kte, vaug)
    mc_ref[...] = jnp.broadcast_to(mnew_col, mc_ref.shape)
    mr_ref[...] = jnp.broadcast_to(jnp.concatenate([mnew_row, jnp.zeros((1, 120), F32)], axis=1), mr_ref.shape)


def _mlstm(rev, q, kt, v, gc, gr, p, ncc, nlc, hf=None, og=None):
    nt = q.shape[0]
    cmap = _scan_chunk_map(rev, ncc, nlc)
    blk = lambda w: pl.BlockSpec((T, w), lambda j: (cmap(j), 0))
    blk_t = lambda h: pl.BlockSpec((h, T), lambda j: (0, cmap(j)))
    in_specs = [blk(512), blk_t(512), blk(1024), blk(32), blk_t(32)]
    args = [q, kt, v, gc, gr]
    if rev:
        in_specs += [blk(1024), blk(1024), _full((1, 1024))]
        args += [hf, og, p["head_norm"]]
    return pl.pallas_call(
        functools.partial(_mlstm_kernel, rev),
        grid=(ncc + nlc,),
        in_specs=in_specs,
        out_specs=blk(1024),
        out_shape=jax.ShapeDtypeStruct((nt, 1024), BF16 if rev else F32),
        scratch_shapes=[pltpu.VMEM((8, 128, 256), F32), pltpu.VMEM((8, 128), F32), pltpu.VMEM((8, 128), F32)],
        compiler_params=_cparams("arbitrary"),
        name="mlstm_bwd" if rev else "mlstm_fwd",
    )(*args)


def _route(logits_t, rb_col):
    scores = _sigmoid(logits_t)
    biased = scores + rb_col
    row = lambda a, e: a[e:e + 1, :]
    gscore = []
    for g in range(4):
        b0, b1, b2, b3 = (row(biased, 4 * g + e) for e in range(4))
        h1, l1 = jnp.maximum(b0, b1), jnp.minimum(b0, b1)
        h2, l2 = jnp.maximum(b2, b3), jnp.minimum(b2, b3)
        gscore.append(jnp.maximum(h1, h2) + jnp.maximum(jnp.minimum(h1, h2), jnp.maximum(l1, l2)))
    gidx = jnp.zeros_like(gscore[0], dtype=I32)
    best = gscore[0]
    for g in range(1, 4):
        better = gscore[g] > best
        gidx = jnp.where(better, g, gidx)
        best = jnp.where(better, gscore[g], best)

    def pick(a, e):
        out = row(a, e)
        for g in range(1, 4):
            out = jnp.where(gidx == g, row(a, 4 * g + e), out)
        return out

    sb = [pick(biased, e) for e in range(4)]
    ss = [pick(scores, e) for e in range(4)]
    i1 = jnp.zeros_like(gidx)
    v1, s1 = sb[0], ss[0]
    for e in range(1, 4):
        better = sb[e] > v1
        i1 = jnp.where(better, e, i1)
        v1 = jnp.where(better, sb[e], v1)
        s1 = jnp.where(better, ss[e], s1)
    i2 = jnp.zeros_like(gidx)
    v2 = jnp.full_like(v1, NEG_INF)
    s2 = jnp.zeros_like(s1)
    for e in range(4):
        better = jnp.logical_and(i1 != e, sb[e] > v2)
        i2 = jnp.where(better, e, i2)
        v2 = jnp.where(better, sb[e], v2)
        s2 = jnp.where(better, ss[e], s2)
    tot = s1 + s2
    w1 = s1 / tot
    w2 = s2 / tot
    first_low = i1 < i2
    a = jnp.minimum(i1, i2)
    b = jnp.maximum(i1, i2)
    off = jnp.where(a == 0, 0, jnp.where(a == 1, 3, 5))
    bucket = 6 * gidx + off + (b - a - 1)
    return bucket, jnp.where(first_low, w1, w2), jnp.where(first_low, w2, w1)


def _out_kernel(nmix, nct, two_src, *refs):
    mix_refs = refs[:nmix]
    refs = refs[nmix:]
    if two_src:
        c_ref, x_ref = refs[:2]
        refs = refs[2:]
    else:
        x_ref = refs[0]
        refs = refs[1:]
    (w_ref, mod_ref, g_ref, rw_ref, rwh_ref, rb_ref,
     xmid_ref, hrow_ref, bucket_ref, rank_ref, cnt_ref, cnt_scr) = refs
    i = pl.program_id(0)

    @pl.when(i == 0)
    def _():
        cnt_scr[...] = jnp.zeros_like(cnt_scr)

    mix = mix_refs[0][...] if nmix == 1 else jnp.concatenate([mr[...] for mr in mix_refs], axis=1)
    x = jnp.where(i < nct, c_ref[...], x_ref[...]) if two_src else x_ref[...]
    x_mid = x + mod_ref[0, 2:3, :] * _dot(mix, w_ref[...])
    xmid_ref[...] = x_mid
    h = _norm_mod(x_mid, g_ref[...], mod_ref[0, 4:5, :], mod_ref[0, 3:4, :])
    hb = h.astype(BF16)
    h_lo = (h - hb.astype(F32)).astype(BF16)
    logits = _hi_lo_cols(_dot(hb, rw_ref[...]), _dot(h_lo, rwh_ref[...]), 16)
    logits_t = jnp.transpose(logits)[0:16, :]
    bucket, w_lo, w_hi = _route(logits_t, rb_ref[...])
    wrows = jnp.concatenate([w_lo, w_hi, jnp.zeros((126, TM), F32)], axis=0)
    hrow_ref[...] = jnp.concatenate([h, jnp.transpose(wrows)], axis=1)
    brow = lax.broadcasted_iota(I32, (NB_PAD, TM), 0)
    onehot = (brow == bucket).astype(F32)
    r = lax.broadcasted_iota(I32, (TM, TM), 0)
    c = lax.broadcasted_iota(I32, (TM, TM), 1)
    before = (r < c).astype(F32).astype(BF16)
    cum = _dot(onehot.astype(BF16), before)
    base = cnt_scr[:, 0:1]
    rank = jnp.sum(onehot * (cum + base), axis=0, keepdims=True)
    bucket_ref[0] = bucket
    rank_ref[0] = rank.astype(I32)
    cnt = cnt_scr[...] + jnp.sum(onehot, axis=1, keepdims=True)
    cnt_scr[...] = cnt
    cnt_ref[...] = cnt.astype(I32)


def _out_proj(mixes, w_out, xs, mod, g_ffn, rw, rw_hi, rb_col, t0, ntiles, nct):
    nmix = len(mixes)
    two_src = len(xs) == 2
    n = ntiles * TM
    tile_in = lambda w: pl.BlockSpec((TM, w), lambda i: (i + t0, 0))
    tile_out = lambda w: pl.BlockSpec((TM, w), lambda i: (i, 0))
    row_out = pl.BlockSpec((1, 1, TM), lambda i: (i, 0, 0))
    if two_src:
        ctx_of, lat_of = _ctx_tile(nct), _lat_tile(nct)
        x_specs = [pl.BlockSpec((TM, D), lambda i: (ctx_of(i), 0)), pl.BlockSpec((TM, D), lambda i: (lat_of(i), 0))]
    else:
        x_specs = [tile_in(D)]
    in_specs = [tile_in(mx.shape[1]) for mx in mixes] + x_specs + [
        _full((D, D)), _mod_spec(nct, t0), _full((1, D)), _full((D, 128)), _full((D, 128)), _full((16, 1))]
    return pl.pallas_call(
        functools.partial(_out_kernel, nmix, nct, two_src),
        grid=(ntiles,),
        in_specs=in_specs,
        out_specs=[tile_out(D), tile_out(XROW), row_out, row_out, _full((NB_PAD, 128))],
        out_shape=[jax.ShapeDtypeStruct((n, D), F32), jax.ShapeDtypeStruct((n, XROW), F32),
                   jax.ShapeDtypeStruct((ntiles, 1, TM), I32), jax.ShapeDtypeStruct((ntiles, 1, TM), I32),
                   jax.ShapeDtypeStruct((NB_PAD, 128), I32)],
        scratch_shapes=[pltpu.VMEM((NB_PAD, 128), F32)],
        compiler_params=_cparams("arbitrary"),
        name="out_proj_router",
    )(*mixes, *xs, w_out, mod, g_ffn, rw, rw_hi, rb_col)


ROW_CHUNK = 256


def _rowmove_kernel(scatter, n, pos_ref, flo_ref, fhi_ref, src_ref, dst_ref, sem):
    def copy(s_row, d_row):
        return pltpu.make_async_copy(src_ref.at[pl.ds(s_row, 1), :], dst_ref.at[pl.ds(d_row, 1), :], sem)

    def wait_rows(lo, hi, unroll):
        def body(r, carry):
            copy(0, 0).wait()
            return carry
        lax.fori_loop(lo, hi, body, 0, unroll=unroll)

    ci = pl.program_id(0)
    last = ci == n // ROW_CHUNK - 1

    def start(r, c):
        t = ci * ROW_CHUNK + r
        p = pos_ref[t]
        (copy(t, p) if scatter else copy(p, t)).start()
        return c
    lax.fori_loop(0, ROW_CHUNK, start, 0, unroll=8)

    @pl.when(ci > 0)
    def _():
        wait_rows(0, ROW_CHUNK, 8)

    @pl.when(last)
    def _():
        wait_rows(0, ROW_CHUNK, 8)
        if scatter:
            _fill_padding(flo_ref, fhi_ref, src_ref, dst_ref, sem, copy, wait_rows)


def _fill_padding(flo_ref, fhi_ref, src_ref, dst_ref, sem, copy, wait_rows):
    for b in range(N_BUCKETS):
        lo, hi = flo_ref[b], fhi_ref[b]

        def fill(r, c):
            copy(0, r).start()
            return c
        lax.fori_loop(lo, hi, fill, 0)
        wait_rows(lo, hi, 1)

    def tile_copy(j):
        return pltpu.make_async_copy(src_ref.at[pl.ds(0, TMM), :], dst_ref.at[pl.ds(j * TMM, TMM), :], sem)

    def fill_tile(j, c):
        tile_copy(j).start()
        return c

    def wait_tile(j, c):
        tile_copy(j).wait()
        return c
    lax.fori_loop(flo_ref[N_BUCKETS], fhi_ref[N_BUCKETS], fill_tile, 0)
    lax.fori_loop(flo_ref[N_BUCKETS], fhi_ref[N_BUCKETS], wait_tile, 0)


def _rowmove(scatter, src, pos, fill_lo, fill_hi, n, nrows_out):
    width = src.shape[1]
    return pl.pallas_call(
        functools.partial(_rowmove_kernel, scatter, n),
        grid_spec=pltpu.PrefetchScalarGridSpec(
            num_scalar_prefetch=3,
            grid=(n // ROW_CHUNK,),
            in_specs=[pl.BlockSpec(memory_space=pl.ANY)],
            out_specs=pl.BlockSpec(memory_space=pl.ANY),
            scratch_shapes=[pltpu.SemaphoreType.DMA(())],
        ),
        out_shape=jax.ShapeDtypeStruct((nrows_out, width), F32),
        compiler_params=_cparams("arbitrary"),
        name="moe_scatter_rows" if scatter else "moe_gather_rows",
    )(pos, fill_lo, fill_hi, src)


def _moe_kernel(tea_ref, teb_ref, tblk_ref, tval_ref, x_ref, w1a_ref, w3a_ref, w2a_ref,
                w1b_ref, w3b_ref, w2b_ref, y_ref):
    del tea_ref, teb_ref, tblk_ref
    j = pl.program_id(0)

    @pl.when(tval_ref[j] != 0)
    def _():
        x = x_ref[...]
        hb = x[:, 0:D].astype(BF16)
        acts = []
        for w1_ref, w3_ref, lane in ((w1a_ref, w3a_ref, 0), (w1b_ref, w3b_ref, 1)):
            u = _dot(hb, w1_ref[0, 0].astype(BF16))
            v = _dot(hb, w3_ref[0, 0].astype(BF16))
            gate = x[:, D + lane:D + lane + 1]
            acts.append((_silu(u) * v * gate).astype(BF16))
        y_ref[...] = (_dot(acts[0], w2a_ref[0, 0].astype(BF16)) + _dot(acts[1], w2b_ref[0, 0].astype(BF16)))

    @pl.when(tval_ref[j] == 0)
    def _():
        y_ref[...] = jnp.zeros_like(y_ref)


def _moe(xs_sorted, w1, w3, w2, layer, tile_ea, tile_eb, tile_blk, tile_valid):
    npad = xs_sorted.shape[0]
    ntile = npad // TMM
    wspec = lambda shape, which: pl.BlockSpec(
        (1, 1) + shape, lambda j, ea, eb, blk, val: (layer, (ea, eb)[which][j], 0, 0))
    up, down = (D, D_EXPERT), (D_EXPERT, D)
    return pl.pallas_call(
        _moe_kernel,
        grid_spec=pltpu.PrefetchScalarGridSpec(
            num_scalar_prefetch=4,
            grid=(ntile,),
            in_specs=[pl.BlockSpec((TMM, XROW), lambda j, ea, eb, blk, val: (blk[j], 0)),
                      wspec(up, 0), wspec(up, 0), wspec(down, 0), wspec(up, 1), wspec(up, 1), wspec(down, 1)],
            out_specs=pl.BlockSpec((TMM, D), lambda j, ea, eb, blk, val: (j, 0)),
        ),
        out_shape=jax.ShapeDtypeStruct((npad, D), F32),
        compiler_params=_cparams("arbitrary"),
        name="moe_experts",
    )(tile_ea, tile_eb, tile_blk, tile_valid, xs_sorted, w1, w3, w2, w1, w3, w2)


def _final_add_kernel(x_ref, y_ref, mod_ref, o_ref):
    o_ref[...] = x_ref[...] + mod_ref[0, 5:6, :] * y_ref[...]


def _final_add(x_mid, y_nat, mod):
    n = x_mid.shape[0]
    tile = pl.BlockSpec((TM, D), lambda i: (i, 0))
    return pl.pallas_call(
        _final_add_kernel,
        grid=(n // TM,),
        in_specs=[tile, tile, pl.BlockSpec((1, 6, D), lambda i: (0, 0, 0))],
        out_specs=tile,
        out_shape=jax.ShapeDtypeStruct((n, D), F32),
        compiler_params=_cparams("arbitrary"),
        name="final_residual",
    )(x_mid, y_nat, mod)


def _moe_block(hrow, bucket, rank, counts, w1, w3, w2, layer):
    n = hrow.shape[0]
    ntile = n // TMM + N_BUCKETS
    npad = ntile * TMM
    cnt = counts[:N_BUCKETS, 0]
    padded = ((cnt + TMM - 1) // TMM) * TMM
    ends = jnp.cumsum(padded)
    starts = ends - padded
    total_tiles = ends[-1] // TMM
    tiles = jnp.arange(ntile, dtype=I32)
    tile_valid = (tiles < total_tiles).astype(I32)
    tile_blk = jnp.minimum(tiles, jnp.maximum(total_tiles - 1, 0))
    tile_bucket = jnp.minimum(jnp.sum((ends[None, :] <= (tile_blk * TMM)[:, None]).astype(I32), axis=1), N_BUCKETS - 1)
    pair = tile_bucket % 6
    grp = tile_bucket // 6
    pair_a = jnp.where(pair < 3, 0, jnp.where(pair < 5, 1, 2))
    pair_b = jnp.where(pair < 3, pair + 1, jnp.where(pair < 5, pair - 1, 3))
    tile_ea = (4 * grp + pair_a).astype(I32)
    tile_eb = (4 * grp + pair_b).astype(I32)
    bucket = bucket.reshape(-1)
    onehot = (bucket[:, None] == jnp.arange(N_BUCKETS, dtype=I32)[None, :]).astype(I32)
    pos = (rank.reshape(-1) + jnp.sum(onehot * starts[None, :], axis=1)).astype(I32)
    pad32 = lambda a, tail: jnp.zeros((NB_PAD,), I32).at[:N_BUCKETS].set(a.astype(I32)).at[N_BUCKETS].set(tail)
    fill_lo, fill_hi = pad32(starts + cnt, total_tiles), pad32(ends, ntile)
    xs_sorted = _rowmove(True, hrow, pos, fill_lo, fill_hi, n, npad)
    ys = _moe(xs_sorted, w1, w3, w2, layer, tile_ea, tile_eb, tile_blk, tile_valid)
    return _rowmove(False, ys, pos, fill_lo, fill_hi, n, n)


def kernel(x, c, ctx, c_ctx, router_w, router_b, norm_mix, norm_ffn, w_mod, b_mod, ev_w_in, ev_conv_w, ev_conv_b, ev_dt_bias, ev_a_log, ev_d_skip, ev_ssd_norm, ev_q_norm, ev_k_norm, ev_sink, ev_w_out, od_w_in, od_conv_w, od_conv_b, od_igate_b, od_fgate_b, od_head_norm, od_w_out, moe_w1, moe_w3, moe_w2):
    s_len = x.shape[1]
    c_len = ctx.shape[1]
    assert x.shape[0] == 1 and s_len % TM == 0 and c_len % TM == 0 and s_len % GRID_W == 0
    nct = c_len // TM
    ncc, nlc = c_len // T, s_len // T
    nt = c_len + s_len
    ntiles = nt // TM

    mod = _modulation(c, c_ctx, w_mod, b_mod)
    rw, rw_hi = _hi_lo_weight(router_w)
    rb_col = router_b.reshape(N_EXPERTS, 1)
    pad128 = lambda v: jnp.zeros((1, 128), F32).at[0, :v.shape[0]].set(v)

    w = ev_w_in[0]
    cos, sin = _rope_tables(s_len)
    wdt, wdt_hi = _hi_lo_weight(w[:, 1280:1296])
    p0 = dict(
        g_mix=norm_mix[0].reshape(1, D),
        wcat=jnp.concatenate([w[:, 0:1280].astype(BF16), w[:, 1296:2064].astype(BF16), wdt], axis=1),
        wdt_hi=wdt_hi,
        conv_w=ev_conv_w[0], conv_b=ev_conv_b[0].reshape(1, 768),
        dt_bias=pad128(ev_dt_bias[0].reshape(16)),
        q_norm=jnp.tile(ev_q_norm[0], 8).reshape(1, 512), k_norm=jnp.tile(ev_k_norm[0], 2).reshape(1, 128),
        cos=cos, sin=sin,
        alog_row=ev_a_log[0].reshape(1, 16), alog_col=ev_a_log[0].reshape(16, 1),
        d_skip=jnp.repeat(ev_d_skip[0], 64).reshape(1, 512), ssd_norm=ev_ssd_norm[0].reshape(1, 512))
    z, xs, bc, q, kk, vv, dtc, dtr = _in0(ctx[0], x[0], mod[0], p0, nct)
    yf = _ssd(False, xs, bc, dtc, dtr, p0, ncc, nlc)
    ymix = _ssd(True, xs, bc, dtc, dtr, p0, ncc, nlc, yf=yf, z=z)
    att = _attention(q, kk, vv, ev_sink[0].reshape(1, 8), ncc, nlc)
    x_mid0, hrow, bucket, rank, counts = _out_proj(
        [ymix, att], ev_w_out[0].astype(BF16), [ctx[0], x[0]], mod[0], norm_ffn[0].reshape(1, D),
        rw, rw_hi, rb_col, 0, ntiles, nct)
    y_nat0 = _moe_block(hrow, bucket, rank, counts, moe_w1, moe_w3, moe_w2, 0)

    w = od_w_in[0]
    wg, wg_hi = _hi_lo_weight(w[:, 3072:3104])
    p1 = dict(
        g_mix=norm_mix[1].reshape(1, D),
        wcat=jnp.concatenate([w[:, 0:3072].astype(BF16), wg], axis=1),
        wg_hi=wg_hi,
        conv_w=od_conv_w[0], conv_b=od_conv_b[0].reshape(1, 2048),
        gate_bias=pad128(jnp.concatenate([od_igate_b[0].reshape(16), od_fgate_b[0].reshape(16)])),
        head_norm=od_head_norm[0].reshape(1, 1024))
    x1, q1, kt1, v1, og1, gc1, gr1 = _in1(x_mid0, y_nat0, mod[0], mod[1], p1, nct)
    hf = _mlstm(False, q1, kt1, v1, gc1, gr1, p1, ncc, nlc)
    hmix = _mlstm(True, q1, kt1, v1, gc1, gr1, p1, ncc, nlc, hf=hf, og=og1)
    x_mid1, hrow, bucket, rank, counts = _out_proj(
        [hmix], od_w_out[0].astype(BF16), [x1], mod[1], norm_ffn[1].reshape(1, D),
        rw, rw_hi, rb_col, nct, ntiles - nct, nct)
    y_nat1 = _moe_block(hrow, bucket, rank, counts, moe_w1, moe_w3, moe_w2, 1)
    return _final_add(x_mid1, y_nat1, mod[1])[None]
```

```python
import functools
import math

import jax
import jax.numpy as jnp
from jax import lax
from jax.experimental import pallas as pl
from jax.experimental.pallas import tpu as pltpu

F32 = jnp.float32
BF16 = jnp.bfloat16
I32 = jnp.int32

EPS = 1e-6
D = 1024
T = 128
TM = 256
TMM = 256
GRID_W = 64
ROPE_THETA = 10000.0
N_EXPERTS = 16
N_BUCKETS = 24
NB_PAD = 32
D_EXPERT = 512
TOK = 8
U32 = jnp.uint32
NEG_INF = float("-inf")
VMEM_LIMIT = 56 * 1024 * 1024

_NN = (((1,), (0,)), ((), ()))
_NT = (((1,), (1,)), ((), ()))
_TN = (((0,), (0,)), ((), ()))

_PAIR_A = (0, 0, 0, 1, 1, 2)
_PAIR_B = (1, 2, 3, 2, 3, 3)


def _cparams(*sem):
    return pltpu.CompilerParams(dimension_semantics=sem, vmem_limit_bytes=VMEM_LIMIT)


def _dot(a, b, dims=_NN):
    return lax.dot_general(a, b, dims, preferred_element_type=F32)


def _split(a, n):
    out = []
    r = a
    for _ in range(n):
        t = r.astype(BF16)
        out.append(t)
        r = r - t.astype(F32)
    return out


def _mdot(as_, bs, dims=_NN, order=None):
    if order is None:
        order = len(as_) + len(bs) - 2
    acc = None
    for i, a in enumerate(as_):
        for j, b in enumerate(bs):
            if i + j <= order:
                p = _dot(a, b, dims)
                acc = p if acc is None else acc + p
    return acc


def _sigmoid(x):
    return 1.0 / (1.0 + jnp.exp(-x))


def _silu(x):
    return x * _sigmoid(x)


def _log1p_exp_neg_abs(x):
    e = jnp.exp(-jnp.abs(x))
    u = 1.0 + e
    um1 = u - 1.0
    return jnp.where(um1 == 0.0, e, jnp.log(u) * (e / jnp.where(um1 == 0.0, 1.0, um1)))


def _softplus(x):
    return jnp.maximum(x, 0.0) + _log1p_exp_neg_abs(x)


def _log_sigmoid(x):
    return jnp.minimum(x, 0.0) - _log1p_exp_neg_abs(x)


def _norm_mod(x, g, sc, sh):
    ms = jnp.mean(x * x, axis=-1, keepdims=True)
    return (x * lax.rsqrt(ms + EPS)) * g * (1.0 + sc) + sh


def _tok_load(ref, chunk, n):
    return ref[pl.ds(chunk, n, stride=TOK), :]


def _tok_store(ref, chunk, val):
    ref[pl.ds(chunk, val.shape[0], stride=TOK), :] = val


def _tok_rows(ref, n):
    return jnp.concatenate([_tok_load(ref, c, n) for c in range(TOK)], axis=1)


def _tri(rev):
    r = lax.broadcasted_iota(I32, (T, T), 0)
    c = lax.broadcasted_iota(I32, (T, T), 1)
    return (c >= r) if rev else (c <= r)


def _cumsums(rev, col, row):
    tri = _tri(rev)
    tri_b = tri.astype(F32).astype(BF16)
    trit_b = _tri(not rev).astype(F32).astype(BF16)
    ccol = _mdot([tri_b], _split(col, 3))
    crow = _mdot(_split(row, 3), [trit_b])
    return tri, ccol, crow


def _mod_kernel(c_ref, w_ref, b_ref, o_ref):
    a = _silu(c_ref[...])
    o_ref[0] = _mdot(_split(a, 2), _split(w_ref[0], 2), order=1) + b_ref[0]


def _modulation(c, c_ctx, w_mod, b_mod):
    depth = w_mod.shape[0]
    n = w_mod.shape[2]
    tn = 1536
    cc = jnp.zeros((8, D), F32).at[0].set(c[0]).at[1].set(c_ctx)
    out = pl.pallas_call(
        _mod_kernel,
        grid=(depth, n // tn),
        in_specs=[
            pl.BlockSpec((8, D), lambda l, j: (0, 0)),
            pl.BlockSpec((1, D, tn), lambda l, j: (l, 0, j)),
            pl.BlockSpec((1, 1, tn), lambda l, j: (l, 0, j)),
        ],
        out_specs=pl.BlockSpec((1, 8, tn), lambda l, j: (l, 0, j)),
        out_shape=jax.ShapeDtypeStruct((depth, 8, n), F32),
        compiler_params=_cparams("arbitrary", "arbitrary"),
        name="modulation",
    )(cc, w_mod, b_mod.reshape(depth, 1, n))
    return out[:, :2].reshape(depth, 2, 6, D)


def _halo_specs(nrows, tile_of, rows_per_token=1):
    nb8 = nrows // 8
    width = D if rows_per_token == 1 else 128
    return [
        pl.BlockSpec((8 * rows_per_token, width), lambda i: (jnp.maximum(tile_of(i) * (TM // 8) - 1, 0), 0)),
        pl.BlockSpec((TM * rows_per_token, width), lambda i: (tile_of(i), 0)),
        pl.BlockSpec((8 * rows_per_token, width), lambda i: (jnp.minimum((tile_of(i) + 1) * (TM // 8), nb8 - 1), 0)),
    ]


def _ctx_tile(nct):
    return lambda i: jnp.minimum(i, nct - 1)


def _lat_tile(nct):
    return lambda i: jnp.maximum(i - nct, 0)


def _mod_spec(nct, t0=0):
    return pl.BlockSpec((1, 6, D), lambda i: (jnp.where(i + t0 < nct, 1, 0), 0, 0))


def _full(shape):
    nd = len(shape)
    return pl.BlockSpec(shape, lambda i: (0,) * nd)


def _seq_edges(i, nct, ntiles):
    prev_ok = jnp.logical_and(i != 0, i != nct).astype(F32)
    next_ok = jnp.logical_and(i != nct - 1, i != ntiles - 1).astype(F32)
    return prev_ok, next_ok


def _conv_silu(x, x_first_prev, x_last_next, cw, cb):
    n = x.shape[0]
    rows = lax.broadcasted_iota(I32, x.shape, 0)
    x_prev = jnp.where(rows == 0, x_first_prev, pltpu.roll(x, 1, 0))
    x_next = jnp.where(rows == n - 1, x_last_next, pltpu.roll(x, n - 1, 0))
    return _silu(x_prev * cw[0:1] + x * cw[1:2] + x_next * cw[2:3] + cb)


def _hi_lo_cols(blk, lo_pass, n):
    return blk + pltpu.roll(blk, 128 - n, 1) + lo_pass


def _hi_lo_weight(w):
    n = w.shape[1]
    hi = w.astype(BF16)
    lo = (w - hi.astype(F32)).astype(BF16)
    z = jnp.zeros((w.shape[0], 128 - 2 * n), BF16)
    return jnp.concatenate([hi, lo, z], axis=1), jnp.concatenate([hi, jnp.zeros_like(lo), z], axis=1)


def _head_rms(xf, gamma):
    r = lax.broadcasted_iota(I32, (128, 128), 0) // 64
    c = lax.broadcasted_iota(I32, (128, 128), 1) // 64
    ones_bd = (r == c).astype(F32).astype(BF16)
    outs = []
    for j in range(xf.shape[1] // 128):
        blk = xf[:, 128 * j:128 * (j + 1)]
        ssum = _mdot(_split(blk * blk, 2), [ones_bd])
        outs.append(blk * lax.rsqrt(ssum * (1.0 / 64.0) + EPS))
    return jnp.concatenate(outs, axis=1) * gamma


def _rope(xf, cos, sin):
    lane = lax.broadcasted_iota(I32, (xf.shape[0], 128), 1)
    first = (lane % 32) < 16
    outs = []
    for j in range(xf.shape[1] // 128):
        blk = xf[:, 128 * j:128 * (j + 1)]
        partner = jnp.where(first, pltpu.roll(blk, 112, 1), pltpu.roll(blk, 16, 1))
        outs.append(blk * cos + partner * sin)
    return jnp.concatenate(outs, axis=1)


def _rope_tables(s_len):
    rows = s_len // GRID_W
    inv = ROPE_THETA ** (-(jnp.arange(32, dtype=I32) % 16).astype(F32) / 16.0)
    sign = jnp.where(jnp.arange(32) < 16, -1.0, 1.0).astype(F32)
    ang_r = jnp.arange(rows, dtype=F32)[:, None] * inv[None, :]
    ang_c = jnp.arange(GRID_W, dtype=F32)[:, None] * inv[None, :]

    def table(fr, fc):
        r = jnp.broadcast_to(fr[:, None, :], (rows, GRID_W, 32))
        c = jnp.broadcast_to(fc[None, :, :], (rows, GRID_W, 32))
        t = jnp.concatenate([r, c, r, c], axis=-1)
        return t.reshape(s_len, 128)

    return table(jnp.cos(ang_r), jnp.cos(ang_c)), table(jnp.sin(ang_r) * sign, jnp.sin(ang_c) * sign)


def _in0_kernel(nct, ntiles, cp_ref, c_ref, cn_ref, xp_ref, x_ref, xn_ref, mod_ref, g_ref, wcat_ref, wdth_ref,
                cw_ref, cb_ref, dtb_ref, qn_ref, kn_ref, cos_ref, sin_ref,
                z_ref, xs_ref, bc_ref, q_ref, kk_ref, vv_ref, dtc_ref, dtr_ref):
    i = pl.program_id(0)
    is_ctx = i < nct
    sh = mod_ref[0, 0:1, :]
    sc = mod_ref[0, 1:2, :]
    g = g_ref[...]
    h = _norm_mod(jnp.where(is_ctx, c_ref[...], x_ref[...]), g, sc, sh)
    hb = h.astype(BF16)
    main = _dot(hb, wcat_ref[...])
    prev_ok, next_ok = _seq_edges(i, nct, ntiles)
    wxbc = wcat_ref[:, 512:1280]
    hp = _norm_mod(jnp.where(is_ctx, cp_ref[...], xp_ref[...]), g, sc, sh).astype(BF16)
    hn = _norm_mod(jnp.where(is_ctx, cn_ref[...], xn_ref[...]), g, sc, sh).astype(BF16)
    xb_prev = _dot(hp, wxbc)[7:8, :] * prev_ok
    xb_next = _dot(hn, wxbc)[0:1, :] * next_ok
    act = _conv_silu(main[:, 512:1280], xb_prev, xb_next, cw_ref[...], cb_ref[...])
    z_ref[...] = main[:, 0:512].astype(BF16)
    xs_ref[...] = act[:, 0:512].astype(BF16)
    bc_ref[...] = act[:, 512:768].astype(BF16)
    cos = jnp.where(is_ctx, 1.0, cos_ref[...])
    sin = jnp.where(is_ctx, 0.0, sin_ref[...])
    q = _rope(_head_rms(main[:, 1280:1792], qn_ref[...]), cos, sin) * 0.125
    q_ref[...] = q.astype(BF16)
    k = _rope(_head_rms(main[:, 1792:1920], kn_ref[...]), cos, sin)
    kk_ref[...] = jnp.concatenate([k, pltpu.roll(k, 64, 1)], axis=1).astype(BF16)
    v = main[:, 1920:2048]
    vv_ref[...] = jnp.concatenate([v, pltpu.roll(v, 64, 1)], axis=1).astype(BF16)
    h_lo = (h - hb.astype(F32)).astype(BF16)
    dt = _softplus(_hi_lo_cols(main[:, 2048:2176], _dot(h_lo, wdth_ref[...]), 16) + dtb_ref[...])
    dtc_ref[...] = dt[:, 0:16]
    dtr_ref[...] = jnp.transpose(dt)[0:16, :]


def _in0(ctx2, x2, mod, p, nct):
    c_len, s_len = ctx2.shape[0], x2.shape[0]
    nt = c_len + s_len
    ntiles = nt // TM
    tile = lambda w: pl.BlockSpec((TM, w), lambda i: (i, 0))
    lat = _lat_tile(nct)
    lat_tile = lambda w: pl.BlockSpec((TM, w), lambda i: (lat(i), 0))
    outs = [(512, BF16), (512, BF16), (256, BF16), (512, BF16), (256, BF16), (256, BF16), (16, F32)]
    return pl.pallas_call(
        functools.partial(_in0_kernel, nct, ntiles),
        grid=(ntiles,),
        in_specs=_halo_specs(c_len, _ctx_tile(nct)) + _halo_specs(s_len, lat) + [
            _mod_spec(nct), _full((1, D)), _full((D, 2176)), _full((D, 128)),
            _full((3, 768)), _full((1, 768)), _full((1, 128)),
            _full((1, 512)), _full((1, 128)), lat_tile(128), lat_tile(128)],
        out_specs=[tile(w) for w, _ in outs] + [pl.BlockSpec((16, TM), lambda i: (0, i))],
        out_shape=[jax.ShapeDtypeStruct((nt, w), dt) for w, dt in outs]
        + [jax.ShapeDtypeStruct((16, nt), F32)],
        compiler_params=_cparams("arbitrary"),
        name="in_proj_even",
    )(ctx2, ctx2, ctx2, x2, x2, x2, mod, p["g_mix"], p["wcat"], p["wdt_hi"], p["conv_w"], p["conv_b"],
      p["dt_bias"], p["q_norm"], p["k_norm"], p["cos"], p["sin"])


def _scan_chunk_map(rev, ncc, nlc):
    if not rev:
        return lambda j: j
    return lambda j: jnp.where(j < ncc, ncc - 1 - j, ncc + nlc - 1 - (j - ncc))


def _ssd_kernel(rev, *refs):
    if rev:
        (xs_ref, bc_ref, dtc_ref, dtr_ref, alr_ref, alc_ref, yf_ref, z_ref, dsk_ref, nrm_ref,
         o_ref, st_ref) = refs
    else:
        xs_ref, bc_ref, dtc_ref, dtr_ref, alr_ref, alc_ref, o_ref, st_ref = refs
    j = pl.program_id(0)

    @pl.when(j == 0)
    def _():
        st_ref[...] = jnp.zeros_like(st_ref)

    d = 8 if rev else 0
    a_coef_row = -jnp.exp(alr_ref[...])[:, d:d + 8]
    a_coef_col = -jnp.exp(alc_ref[...])[d:d + 8, :]
    dtc = dtc_ref[:, d:d + 8]
    dtr = dtr_ref[d:d + 8, :]
    tri, acs_col, acs_row = _cumsums(rev, dtc * a_coef_row, dtr * a_coef_col)
    end = 0 if rev else T - 1
    atot_row = acs_col[end:end + 1, :]
    atot_col = acs_row[:, end:end + 1]
    dec_col = jnp.exp(atot_row - acs_col) * dtc
    eacs_col = jnp.exp(acs_col)
    xs = xs_ref[...]
    bm = bc_ref[:, 0:128]
    cm = bc_ref[:, 128:256]
    lane = lax.broadcasted_iota(I32, (T, 128), 1)
    lo = lane < 64
    zero_b = jnp.zeros((T, 128), BF16)
    ys = [None] * 4
    for g in range(2):
        in_g = lo if g == 0 else jnp.logical_not(lo)
        cg = jnp.where(in_g, cm, zero_b)
        bg = jnp.where(in_g, bm, zero_b)
        cb = _dot(cg, bm, _NT)
        for hh in range(4):
            hd = 4 * g + hh
            pair, hi = hd // 2, hd % 2
            diff = acs_col[:, hd:hd + 1] - acs_row[hd:hd + 1, :]
            lmat = jnp.exp(jnp.where(tri, diff, NEG_INF))
            m = (cb * lmat * dtr[hd:hd + 1, :]).astype(BF16)
            xp = xs[:, 128 * pair:128 * (pair + 1)]
            xh = jnp.where(lo if hi == 0 else jnp.logical_not(lo), xp, zero_b)
            st = st_ref[hd]
            y = _dot(m, xh) + _dot(cg, st.astype(BF16)) * eacs_col[:, hd:hd + 1]
            ys[pair] = y if ys[pair] is None else ys[pair] + y
            xdec = (xh.astype(F32) * dec_col[:, hd:hd + 1]).astype(BF16)
            st_ref[hd] = jnp.exp(atot_col[hd:hd + 1, :]) * st + _dot(bg, xdec, _TN)
    y = jnp.concatenate(ys, axis=1)
    if not rev:
        o_ref[...] = y
    else:
        ytot = y + yf_ref[...] + dsk_ref[...] * xs.astype(F32)
        gated = ytot * _silu(z_ref[...].astype(F32))
        ms = jnp.mean(gated * gated, axis=-1, keepdims=True)
        o_ref[...] = (gated * lax.rsqrt(ms + EPS) * nrm_ref[...]).astype(BF16)


def _ssd(rev, xs, bc, dtc, dtr, p, ncc, nlc, yf=None, z=None):
    nt = xs.shape[0]
    cmap = _scan_chunk_map(rev, ncc, nlc)
    blk = lambda w: pl.BlockSpec((T, w), lambda j: (cmap(j), 0))
    in_specs = [blk(512), blk(256), blk(16), pl.BlockSpec((16, T), lambda j: (0, cmap(j))),
                _full((1, 16)), _full((16, 1))]
    args = [xs, bc, dtc, dtr, p["alog_row"], p["alog_col"]]
    if rev:
        in_specs += [blk(512), blk(512), _full((1, 512)), _full((1, 512))]
        args += [yf, z, p["d_skip"], p["ssd_norm"]]
    return pl.pallas_call(
        functools.partial(_ssd_kernel, rev),
        grid=(ncc + nlc,),
        in_specs=in_specs,
        out_specs=blk(512),
        out_shape=jax.ShapeDtypeStruct((nt, 512), BF16 if rev else F32),
        scratch_shapes=[pltpu.VMEM((8, 128, 128), F32)],
        compiler_params=_cparams("arbitrary"),
        name="ssd_bwd" if rev else "ssd_fwd",
    )(*args)


def _attn_kernel(ncc, nblk, q_ref, kp_ref, kc_ref, kn_ref, vp_ref, vc_ref, vn_ref, kx_ref, vx_ref,
                 sink_ref, o_ref):
    j = pl.program_id(0)
    c_len = kx_ref.shape[0]
    is_lat = j >= ncc
    prev_ok = jnp.logical_and(is_lat, j >= ncc + 1)
    next_ok = jnp.logical_and(is_lat, j <= nblk - 2)
    r = lax.broadcasted_iota(I32, (T, T), 0)
    c = lax.broadcasted_iota(I32, (T, T), 1)
    zero = jnp.zeros((T, T), F32)
    ninf = jnp.full((T, T), NEG_INF, F32)
    bias = jnp.concatenate([
        jnp.where(jnp.logical_and(prev_ok, c >= r), zero, ninf),
        jnp.where(is_lat, zero, ninf),
        jnp.where(jnp.logical_and(next_ok, c <= r), zero, ninf),
        jnp.zeros((T, c_len), F32)], axis=1)
    k_all = jnp.concatenate([kp_ref[...], kc_ref[...], kn_ref[...], kx_ref[...]], axis=0)
    v_all = jnp.concatenate([vp_ref[...], vc_ref[...], vn_ref[...], vx_ref[...]], axis=0)
    nk = k_all.shape[0]
    lane = lax.broadcasted_iota(I32, (nk, 128), 1)
    lo = lane < 64
    zero_b = jnp.zeros((nk, 128), BF16)
    q = q_ref[...]
    sink = sink_ref[...]
    for g in range(2):
        k_lo = jnp.where(lo, k_all[:, 128 * g:128 * (g + 1)], zero_b)
        k_hi = jnp.where(lo, zero_b, k_all[:, 128 * (1 - g):128 * (2 - g)])
        v_lo = jnp.where(lo, v_all[:, 128 * g:128 * (g + 1)], zero_b)
        v_hi = jnp.where(lo, zero_b, v_all[:, 128 * (1 - g):128 * (2 - g)])
        for pp in range(2):
            pair = 2 * g + pp
            qp = q[:, 128 * pair:128 * (pair + 1)]
            acc = None
            for hi in range(2):
                hd = 2 * pair + hi
                s = _dot(qp, k_hi if hi else k_lo, _NT) + bias
                sk = sink[:, hd:hd + 1]
                m = jnp.maximum(jnp.max(s, axis=-1, keepdims=True), sk)
                pr = jnp.exp(s - m)
                den = jnp.sum(pr, axis=-1, keepdims=True) + jnp.exp(sk - m)
                o = _dot(pr.astype(BF16), v_hi if hi else v_lo) / den
                acc = o if acc is None else acc + o
            o_ref[:, 128 * pair:128 * (pair + 1)] = acc.astype(BF16)


def _attention(q, kk, vv, sink, ncc, nlc):
    nt = q.shape[0]
    nblk = ncc + nlc
    c_len = ncc * T
    prev = lambda w: pl.BlockSpec((T, w), lambda j: (jnp.maximum(j - 1, 0), 0))
    cur = lambda w: pl.BlockSpec((T, w), lambda j: (j, 0))
    nxt = lambda w: pl.BlockSpec((T, w), lambda j: (jnp.minimum(j + 1, nblk - 1), 0))
    ctx = lambda w: pl.BlockSpec((c_len, w), lambda j: (0, 0))
    return pl.pallas_call(
        functools.partial(_attn_kernel, ncc, nblk),
        grid=(nblk,),
        in_specs=[cur(512), prev(256), cur(256), nxt(256), prev(256), cur(256), nxt(256),
                  ctx(256), ctx(256), _full((1, 8))],
        out_specs=cur(512),
        out_shape=jax.ShapeDtypeStruct((nt, 512), BF16),
        compiler_params=_cparams("arbitrary"),
        name="window_attention",
    )(q, kk, kk, kk, vv, vv, vv, kk, vv, sink)


def _in1_kernel(nct, ntiles, xp_ref, x_ref, xn_ref, yp_ref, y_ref, yn_ref, modp_ref, mod_ref, g_ref,
                wcat_ref, wgh_ref, cw_ref, cb_ref, gb_ref,
                x1_ref, q_ref, kt_ref, v_ref, o_ref, gc_ref, gr_ref):
    i = pl.program_id(0)
    gf = modp_ref[0, 5:6, :]
    sh = mod_ref[0, 0:1, :]
    sc = mod_ref[0, 1:2, :]
    g = g_ref[...]
    x1 = x_ref[...] + gf * _tok_rows(y_ref, TM)
    x1_ref[...] = x1
    h = _norm_mod(x1, g, sc, sh)
    hb = h.astype(BF16)
    main = _dot(hb, wcat_ref[...])
    prev_ok, next_ok = _seq_edges(i, nct, ntiles)
    wqkv = wcat_ref[:, 0:2048]
    hp = _norm_mod(xp_ref[...] + gf * _tok_rows(yp_ref, 8), g, sc, sh).astype(BF16)
    hn = _norm_mod(xn_ref[...] + gf * _tok_rows(yn_ref, 8), g, sc, sh).astype(BF16)
    x_prev = _dot(hp, wqkv)[7:8, :] * prev_ok
    x_next = _dot(hn, wqkv)[0:1, :] * next_ok
    act = _conv_silu(main[:, 0:2048], x_prev, x_next, cw_ref[...], cb_ref[...])
    q_ref[...] = act[:, 0:512].astype(BF16)
    kt_ref[...] = jnp.transpose(act[:, 512:1024] * 0.125).astype(BF16)
    v_ref[...] = act[:, 1024:2048].astype(BF16)
    o_ref[...] = main[:, 2048:3072].astype(BF16)
    h_lo = (h - hb.astype(F32)).astype(BF16)
    gates = _hi_lo_cols(main[:, 3072:3200], _dot(h_lo, wgh_ref[...]), 32) + gb_ref[...]
    lane = lax.broadcasted_iota(I32, gates.shape, 1)
    gates = jnp.where(lane < 16, gates, _log_sigmoid(gates))
    gc_ref[...] = gates[:, 0:32]
    gr_ref[...] = jnp.transpose(gates)[0:32, :]


def _in1(x_mid, y_nat, mod_prev, mod, p, nct):
    nt = x_mid.shape[0]
    ntiles = nt // TM
    tile = lambda w: pl.BlockSpec((TM, w), lambda i: (i, 0))
    ident = lambda i: i
    return pl.pallas_call(
        functools.partial(_in1_kernel, nct, ntiles),
        grid=(ntiles,),
        in_specs=_halo_specs(nt, ident) + _halo_specs(nt, ident, rows_per_token=TOK) + [
            _mod_spec(nct), _mod_spec(nct), _full((1, D)), _full((D, 3200)), _full((D, 128)),
            _full((3, 2048)), _full((1, 2048)), _full((1, 128))],
        out_specs=[tile(D), tile(512), pl.BlockSpec((512, TM), lambda i: (0, i)), tile(1024), tile(1024),
                   tile(32), pl.BlockSpec((32, TM), lambda i: (0, i))],
        out_shape=[jax.ShapeDtypeStruct((nt, D), F32),
                   jax.ShapeDtypeStruct((nt, 512), BF16), jax.ShapeDtypeStruct((512, nt), BF16),
                   jax.ShapeDtypeStruct((nt, 1024), BF16), jax.ShapeDtypeStruct((nt, 1024), BF16),
                   jax.ShapeDtypeStruct((nt, 32), F32), jax.ShapeDtypeStruct((32, nt), F32)],
        compiler_params=_cparams("arbitrary"),
        name="in_proj_odd",
    )(x_mid, x_mid, x_mid, y_nat, y_nat, y_nat, mod_prev, mod, p["g_mix"], p["wcat"], p["wg_hi"],
      p["conv_w"], p["conv_b"], p["gate_bias"])


def _mlstm_kernel(rev, *refs):
    if rev:
        (q_ref, kt_ref, v_ref, gc_ref, gr_ref, hf_ref, og_ref, hn_ref, o_ref,
         c_ref, mc_ref, mr_ref) = refs
    else:
        q_ref, kt_ref, v_ref, gc_ref, gr_ref, o_ref, c_ref, mc_ref, mr_ref = refs
    j = pl.program_id(0)

    @pl.when(j == 0)
    def _():
        c_ref[...] = jnp.zeros_like(c_ref)
        mc_ref[...] = jnp.zeros_like(mc_ref)
        mr_ref[...] = jnp.zeros_like(mr_ref)

    d = 8 if rev else 0
    ig_col = gc_ref[:, d:d + 8]
    lf_col = gc_ref[:, 16 + d:24 + d]
    ig_row = gr_ref[d:d + 8, :]
    lf_row = gr_ref[16 + d:24 + d, :]
    tri, b_col, b_row = _cumsums(rev, lf_col, lf_row)
    end = 0 if rev else T - 1
    blast_row = b_col[end:end + 1, :]
    blast_col = b_row[:, end:end + 1]
    wend_row = blast_col - b_row + ig_row
    ac_col = jnp.max(wend_row, axis=1, keepdims=True)
    eend_row = jnp.exp(wend_row - ac_col)
    ac_row = jnp.max(blast_row - b_col + ig_col, axis=0, keepdims=True)
    m_col = mc_ref[:, 0:1]
    m_row = mr_ref[0:1, 0:8]
    mnew_col = jnp.maximum(blast_col + m_col, ac_col)
    sp_col = jnp.exp(blast_col + m_col - mnew_col)
    sc_col = jnp.exp(ac_col - mnew_col)
    mnew_row = jnp.maximum(blast_row + m_row, ac_row)
    g_col = b_col + m_row
    q = q_ref[...]
    ones_b = jnp.ones((T, 128), BF16)
    sub = lax.broadcasted_iota(I32, (128, T), 0)
    zero_k = jnp.zeros((128, T), BF16)
    for hd in range(8):
        pair, hi = hd // 2, hd % 2
        qp = q[:, 128 * pair:128 * (pair + 1)]
        ktp = kt_ref[128 * pair:128 * (pair + 1), :]
        kth = jnp.where((sub >= 64) if hi else (sub < 64), ktp, zero_k)
        vaug = jnp.concatenate([v_ref[:, 128 * hd:128 * (hd + 1)], ones_b], axis=1)
        dlog = jnp.where(tri, b_col[:, hd:hd + 1] - b_row[hd:hd + 1, :] + ig_row[hd:hd + 1, :], NEG_INF)
        gh = g_col[:, hd:hd + 1]
        mstar = jnp.maximum(gh, jnp.max(dlog, axis=-1, keepdims=True))
        w = (jnp.exp(dlog - mstar) * _dot(qp, kth)).astype(BF16)
        cst = c_ref[hd]
        nd = _dot(w, vaug) + jnp.exp(gh - mstar) * _dot(qp, cst.astype(BF16))
        den = jnp.maximum(jnp.abs(nd[:, 128:256]), jnp.exp(-mstar))
        hh = nd[:, 0:128] / den
        if rev:
            hh = hh + hf_ref[:, 128 * hd:128 * (hd + 1)]
            ms = jnp.mean(hh * hh, axis=-1, keepdims=True)
            hh = hh * lax.rsqrt(ms + EPS) * hn_ref[:, 128 * hd:128 * (hd + 1)]
            og = og_ref[:, 128 * hd:128 * (hd + 1)].astype(F32)
            o_ref[:, 128 * hd:128 * (hd + 1)] = (hh * _sigmoid(og)).astype(BF16)
        else:
            o_ref[:, 128 * hd:128 * (hd + 1)] = hh
        kte = (kth.astype(F32) * eend_row[hd:hd + 1, :]).astype(BF16)
        c_ref[hd] = sp_col[hd:hd + 1, :] * cst + sc_col[hd:hd + 1, :] * _dot(kte, vaug)
    mc_ref[...] = jnp.broadcast_to(mnew_col, mc_ref.shape)
    mr_ref[...] = jnp.broadcast_to(jnp.concatenate([mnew_row, jnp.zeros((1, 120), F32)], axis=1), mr_ref.shape)


def _mlstm(rev, q, kt, v, gc, gr, p, ncc, nlc, hf=None, og=None):
    nt = q.shape[0]
    cmap = _scan_chunk_map(rev, ncc, nlc)
    blk = lambda w: pl.BlockSpec((T, w), lambda j: (cmap(j), 0))
    blk_t = lambda h: pl.BlockSpec((h, T), lambda j: (0, cmap(j)))
    in_specs = [blk(512), blk_t(512), blk(1024), blk(32), blk_t(32)]
    args = [q, kt, v, gc, gr]
    if rev:
        in_specs += [blk(1024), blk(1024), _full((1, 1024))]
        args += [hf, og, p["head_norm"]]
    return pl.pallas_call(
        functools.partial(_mlstm_kernel, rev),
        grid=(ncc + nlc,),
        in_specs=in_specs,
        out_specs=blk(1024),
        out_shape=jax.ShapeDtypeStruct((nt, 1024), BF16 if rev else F32),
        scratch_shapes=[pltpu.VMEM((8, 128, 256), F32), pltpu.VMEM((8, 128), F32), pltpu.VMEM((8, 128), F32)],
        compiler_params=_cparams("arbitrary"),
        name="mlstm_bwd" if rev else "mlstm_fwd",
    )(*args)


def _route(logits_t, rb_col):
    scores = _sigmoid(logits_t)
    biased = scores + rb_col
    row = lambda a, e: a[e:e + 1, :]
    gscore = []
    for g in range(4):
        b0, b1, b2, b3 = (row(biased, 4 * g + e) for e in range(4))
        h1, l1 = jnp.maximum(b0, b1), jnp.minimum(b0, b1)
        h2, l2 = jnp.maximum(b2, b3), jnp.minimum(b2, b3)
        gscore.append(jnp.maximum(h1, h2) + jnp.maximum(jnp.minimum(h1, h2), jnp.maximum(l1, l2)))
    gidx = jnp.zeros_like(gscore[0], dtype=I32)
    best = gscore[0]
    for g in range(1, 4):
        better = gscore[g] > best
        gidx = jnp.where(better, g, gidx)
        best = jnp.where(better, gscore[g], best)

    def pick(a, e):
        out = row(a, e)
        for g in range(1, 4):
            out = jnp.where(gidx == g, row(a, 4 * g + e), out)
        return out

    sb = [pick(biased, e) for e in range(4)]
    ss = [pick(scores, e) for e in range(4)]
    i1 = jnp.zeros_like(gidx)
    v1, s1 = sb[0], ss[0]
    for e in range(1, 4):
        better = sb[e] > v1
        i1 = jnp.where(better, e, i1)
        v1 = jnp.where(better, sb[e], v1)
        s1 = jnp.where(better, ss[e], s1)
    i2 = jnp.zeros_like(gidx)
    v2 = jnp.full_like(v1, NEG_INF)
    s2 = jnp.zeros_like(s1)
    for e in range(4):
        better = jnp.logical_and(i1 != e, sb[e] > v2)
        i2 = jnp.where(better, e, i2)
        v2 = jnp.where(better, sb[e], v2)
        s2 = jnp.where(better, ss[e], s2)
    tot = s1 + s2
    w1 = s1 / tot
    w2 = s2 / tot
    first_low = i1 < i2
    a = jnp.minimum(i1, i2)
    b = jnp.maximum(i1, i2)
    off = jnp.where(a == 0, 0, jnp.where(a == 1, 3, 5))
    bucket = 6 * gidx + off + (b - a - 1)
    return bucket, jnp.where(first_low, w1, w2), jnp.where(first_low, w2, w1)


def _out_kernel(nmix, nct, two_src, *refs):
    mix_refs = refs[:nmix]
    refs = refs[nmix:]
    if two_src:
        c_ref, x_ref = refs[:2]
        refs = refs[2:]
    else:
        x_ref = refs[0]
        refs = refs[1:]
    (w_ref, mod_ref, g_ref, rw_ref, rwh_ref, rb_ref,
     xmid_ref, hrow_ref, bucket_ref, rank_ref, cnt_ref, cnt_scr) = refs
    i = pl.program_id(0)

    @pl.when(i == 0)
    def _():
        cnt_scr[...] = jnp.zeros_like(cnt_scr)

    mix = mix_refs[0][...] if nmix == 1 else jnp.concatenate([mr[...] for mr in mix_refs], axis=1)
    x = jnp.where(i < nct, c_ref[...], x_ref[...]) if two_src else x_ref[...]
    x_mid = x + mod_ref[0, 2:3, :] * _dot(mix, w_ref[...])
    xmid_ref[...] = x_mid
    h = _norm_mod(x_mid, g_ref[...], mod_ref[0, 4:5, :], mod_ref[0, 3:4, :])
    hb = h.astype(BF16)
    h_lo = (h - hb.astype(F32)).astype(BF16)
    logits = _hi_lo_cols(_dot(hb, rw_ref[...]), _dot(h_lo, rwh_ref[...]), 16)
    logits_t = jnp.transpose(logits)[0:16, :]
    bucket, w_lo, w_hi = _route(logits_t, rb_ref[...])
    bits = pltpu.bitcast(hb.astype(F32), U32)
    words = (bits[:, 512:1024] & jnp.uint32(0xFFFF0000)) | (bits[:, 0:512] >> 16)
    for cblk in range(4):
        _tok_store(hrow_ref, cblk, words[:, 128 * cblk:128 * (cblk + 1)])
    wrows = jnp.concatenate([w_lo, w_hi, jnp.zeros((126, TM), F32)], axis=0)
    _tok_store(hrow_ref, 4, pltpu.bitcast(jnp.transpose(wrows), U32))
    for cblk in range(5, TOK):
        _tok_store(hrow_ref, cblk, jnp.zeros((TM, 128), U32))
    brow = lax.broadcasted_iota(I32, (NB_PAD, TM), 0)
    onehot = (brow == bucket).astype(F32)
    r = lax.broadcasted_iota(I32, (TM, TM), 0)
    c = lax.broadcasted_iota(I32, (TM, TM), 1)
    before = (r < c).astype(F32).astype(BF16)
    cum = _dot(onehot.astype(BF16), before)
    base = cnt_scr[:, 0:1]
    rank = jnp.sum(onehot * (cum + base), axis=0, keepdims=True)
    bucket_ref[0] = bucket
    rank_ref[0] = rank.astype(I32)
    cnt = cnt_scr[...] + jnp.sum(onehot, axis=1, keepdims=True)
    cnt_scr[...] = cnt
    cnt_ref[...] = cnt.astype(I32)


def _out_proj(mixes, w_out, xs, mod, g_ffn, rw, rw_hi, rb_col, t0, ntiles, nct):
    nmix = len(mixes)
    two_src = len(xs) == 2
    n = ntiles * TM
    tile_in = lambda w: pl.BlockSpec((TM, w), lambda i: (i + t0, 0))
    tile_out = lambda w: pl.BlockSpec((TM, w), lambda i: (i, 0))
    row_out = pl.BlockSpec((1, 1, TM), lambda i: (i, 0, 0))
    if two_src:
        ctx_of, lat_of = _ctx_tile(nct), _lat_tile(nct)
        x_specs = [pl.BlockSpec((TM, D), lambda i: (ctx_of(i), 0)), pl.BlockSpec((TM, D), lambda i: (lat_of(i), 0))]
    else:
        x_specs = [tile_in(D)]
    in_specs = [tile_in(mx.shape[1]) for mx in mixes] + x_specs + [
        _full((D, D)), _mod_spec(nct, t0), _full((1, D)), _full((D, 128)), _full((D, 128)), _full((16, 1))]
    return pl.pallas_call(
        functools.partial(_out_kernel, nmix, nct, two_src),
        grid=(ntiles,),
        in_specs=in_specs,
        out_specs=[tile_out(D), pl.BlockSpec((TM * TOK, 128), lambda i: (i, 0)), row_out, row_out,
                   _full((NB_PAD, 128))],
        out_shape=[jax.ShapeDtypeStruct((n, D), F32), jax.ShapeDtypeStruct((n * TOK, 128), U32),
                   jax.ShapeDtypeStruct((ntiles, 1, TM), I32), jax.ShapeDtypeStruct((ntiles, 1, TM), I32),
                   jax.ShapeDtypeStruct((NB_PAD, 128), I32)],
        scratch_shapes=[pltpu.VMEM((NB_PAD, 128), F32)],
        compiler_params=_cparams("arbitrary"),
        name="out_proj_router",
    )(*mixes, *xs, w_out, mod, g_ffn, rw, rw_hi, rb_col)


ROW_CHUNK = 256


def _rowmove_kernel(scatter, n, pos_ref, flo_ref, fhi_ref, src_ref, dst_ref, sem):
    def copy(s_row, d_row):
        return pltpu.make_async_copy(src_ref.at[pl.ds(s_row, 1)], dst_ref.at[pl.ds(d_row, 1)], sem)

    def wait_rows(lo, hi, unroll):
        def body(r, carry):
            copy(0, 0).wait()
            return carry
        lax.fori_loop(lo, hi, body, 0, unroll=unroll)

    ci = pl.program_id(0)
    last = ci == n // ROW_CHUNK - 1

    def start(r, c):
        t = ci * ROW_CHUNK + r
        p = pos_ref[t]
        (copy(t, p) if scatter else copy(p, t)).start()
        return c
    lax.fori_loop(0, ROW_CHUNK, start, 0, unroll=8)

    @pl.when(ci > 0)
    def _():
        wait_rows(0, ROW_CHUNK, 8)

    @pl.when(last)
    def _():
        wait_rows(0, ROW_CHUNK, 8)
        if scatter:
            _fill_padding(flo_ref, fhi_ref, src_ref, dst_ref, sem, copy, wait_rows)


def _fill_padding(flo_ref, fhi_ref, src_ref, dst_ref, sem, copy, wait_rows):
    for b in range(N_BUCKETS):
        lo, hi = flo_ref[b], fhi_ref[b]

        def fill(r, c):
            copy(0, r).start()
            return c
        lax.fori_loop(lo, hi, fill, 0)
        wait_rows(lo, hi, 1)

    def tile_copy(j):
        return pltpu.make_async_copy(src_ref.at[pl.ds(0, TMM)], dst_ref.at[pl.ds(j * TMM, TMM)], sem)

    def fill_tile(j, c):
        tile_copy(j).start()
        return c

    def wait_tile(j, c):
        tile_copy(j).wait()
        return c
    lax.fori_loop(flo_ref[N_BUCKETS], fhi_ref[N_BUCKETS], fill_tile, 0)
    lax.fori_loop(flo_ref[N_BUCKETS], fhi_ref[N_BUCKETS], wait_tile, 0)


def _rowmove(scatter, src, pos, fill_lo, fill_hi, n, nrows_out):
    out = pl.pallas_call(
        functools.partial(_rowmove_kernel, scatter, n),
        grid_spec=pltpu.PrefetchScalarGridSpec(
            num_scalar_prefetch=3,
            grid=(n // ROW_CHUNK,),
            in_specs=[pl.BlockSpec(memory_space=pl.ANY)],
            out_specs=pl.BlockSpec(memory_space=pl.ANY),
            scratch_shapes=[pltpu.SemaphoreType.DMA(())],
        ),
        out_shape=jax.ShapeDtypeStruct((nrows_out, TOK, 128), src.dtype),
        compiler_params=_cparams("arbitrary"),
        name="moe_scatter_rows" if scatter else "moe_gather_rows",
    )(pos, fill_lo, fill_hi, src.reshape(-1, TOK, 128))
    return out.reshape(nrows_out * TOK, 128)


def _moe_kernel(tea_ref, teb_ref, tblk_ref, tval_ref, x_ref, w1a_ref, w3a_ref, w2a_ref,
                w1b_ref, w3b_ref, w2b_ref, y_ref):
    del tea_ref, teb_ref, tblk_ref
    j = pl.program_id(0)

    @pl.when(tval_ref[j] != 0)
    def _():
        words = [_tok_load(x_ref, cblk, TMM) for cblk in range(4)]
        low = [pltpu.bitcast(wd << 16, F32).astype(BF16) for wd in words]
        high = [pltpu.bitcast(wd & jnp.uint32(0xFFFF0000), F32).astype(BF16) for wd in words]
        hb = jnp.concatenate(low + high, axis=1)
        gates = pltpu.bitcast(_tok_load(x_ref, 4, TMM), F32)
        acts = []
        for w1_ref, w3_ref, lane in ((w1a_ref, w3a_ref, 0), (w1b_ref, w3b_ref, 1)):
            u = _dot(hb, w1_ref[0, 0].astype(BF16))
            v = _dot(hb, w3_ref[0, 0].astype(BF16))
            acts.append((_silu(u) * v * gates[:, lane:lane + 1]).astype(BF16))
        y = _dot(acts[0], w2a_ref[0, 0].astype(BF16)) + _dot(acts[1], w2b_ref[0, 0].astype(BF16))
        for cblk in range(TOK):
            _tok_store(y_ref, cblk, y[:, 128 * cblk:128 * (cblk + 1)])

    @pl.when(tval_ref[j] == 0)
    def _():
        y_ref[...] = jnp.zeros_like(y_ref)


def _moe(xs_sorted, w1, w3, w2, layer, tile_ea, tile_eb, tile_blk, tile_valid):
    npad = xs_sorted.shape[0] // TOK
    ntile = npad // TMM
    wspec = lambda shape, which: pl.BlockSpec(
        (1, 1) + shape, lambda j, ea, eb, blk, val: (layer, (ea, eb)[which][j], 0, 0))
    up, down = (D, D_EXPERT), (D_EXPERT, D)
    return pl.pallas_call(
        _moe_kernel,
        grid_spec=pltpu.PrefetchScalarGridSpec(
            num_scalar_prefetch=4,
            grid=(ntile,),
            in_specs=[pl.BlockSpec((TMM * TOK, 128), lambda j, ea, eb, blk, val: (blk[j], 0)),
                      wspec(up, 0), wspec(up, 0), wspec(down, 0), wspec(up, 1), wspec(up, 1), wspec(down, 1)],
            out_specs=pl.BlockSpec((TMM * TOK, 128), lambda j, ea, eb, blk, val: (j, 0)),
        ),
        out_shape=jax.ShapeDtypeStruct((npad * TOK, 128), F32),
        compiler_params=_cparams("arbitrary"),
        name="moe_experts",
    )(tile_ea, tile_eb, tile_blk, tile_valid, xs_sorted, w1, w3, w2, w1, w3, w2)


def _final_add_kernel(x_ref, y_ref, mod_ref, o_ref):
    o_ref[...] = x_ref[...] + mod_ref[0, 5:6, :] * _tok_rows(y_ref, TM)


def _final_add(x_mid, y_nat, mod):
    n = x_mid.shape[0]
    tile = pl.BlockSpec((TM, D), lambda i: (i, 0))
    return pl.pallas_call(
        _final_add_kernel,
        grid=(n // TM,),
        in_specs=[tile, pl.BlockSpec((TM * TOK, 128), lambda i: (i, 0)), pl.BlockSpec((1, 6, D), lambda i: (0, 0, 0))],
        out_specs=tile,
        out_shape=jax.ShapeDtypeStruct((n, D), F32),
        compiler_params=_cparams("arbitrary"),
        name="final_residual",
    )(x_mid, y_nat, mod)


def _moe_block(hrow, bucket, rank, counts, w1, w3, w2, layer):
    n = hrow.shape[0] // TOK
    ntile = n // TMM + N_BUCKETS
    npad = ntile * TMM
    cnt = counts[:N_BUCKETS, 0]
    padded = ((cnt + TMM - 1) // TMM) * TMM
    ends = jnp.cumsum(padded)
    starts = ends - padded
    total_tiles = ends[-1] // TMM
    tiles = jnp.arange(ntile, dtype=I32)
    tile_valid = (tiles < total_tiles).astype(I32)
    tile_blk = jnp.minimum(tiles, jnp.maximum(total_tiles - 1, 0))
    tile_bucket = jnp.minimum(jnp.sum((ends[None, :] <= (tile_blk * TMM)[:, None]).astype(I32), axis=1), N_BUCKETS - 1)
    pair = tile_bucket % 6
    grp = tile_bucket // 6
    pair_a = jnp.where(pair < 3, 0, jnp.where(pair < 5, 1, 2))
    pair_b = jnp.where(pair < 3, pair + 1, jnp.where(pair < 5, pair - 1, 3))
    tile_ea = (4 * grp + pair_a).astype(I32)
    tile_eb = (4 * grp + pair_b).astype(I32)
    bucket = bucket.reshape(-1)
    onehot = (bucket[:, None] == jnp.arange(N_BUCKETS, dtype=I32)[None, :]).astype(I32)
    pos = (rank.reshape(-1) + jnp.sum(onehot * starts[None, :], axis=1)).astype(I32)
    pad32 = lambda a, tail: jnp.zeros((NB_PAD,), I32).at[:N_BUCKETS].set(a.astype(I32)).at[N_BUCKETS].set(tail)
    fill_lo, fill_hi = pad32(starts + cnt, total_tiles), pad32(ends, ntile)
    xs_sorted = _rowmove(True, hrow, pos, fill_lo, fill_hi, n, npad)
    ys = _moe(xs_sorted, w1, w3, w2, layer, tile_ea, tile_eb, tile_blk, tile_valid)
    return _rowmove(False, ys, pos, fill_lo, fill_hi, n, n)


def kernel(x, c, ctx, c_ctx, router_w, router_b, norm_mix, norm_ffn, w_mod, b_mod, ev_w_in, ev_conv_w, ev_conv_b, ev_dt_bias, ev_a_log, ev_d_skip, ev_ssd_norm, ev_q_norm, ev_k_norm, ev_sink, ev_w_out, od_w_in, od_conv_w, od_conv_b, od_igate_b, od_fgate_b, od_head_norm, od_w_out, moe_w1, moe_w3, moe_w2):
    s_len = x.shape[1]
    c_len = ctx.shape[1]
    assert x.shape[0] == 1 and s_len % TM == 0 and c_len % TM == 0 and s_len % GRID_W == 0
    nct = c_len // TM
    ncc, nlc = c_len // T, s_len // T
    nt = c_len + s_len
    ntiles = nt // TM

    mod = _modulation(c, c_ctx, w_mod, b_mod)
    rw, rw_hi = _hi_lo_weight(router_w)
    rb_col = router_b.reshape(N_EXPERTS, 1)
    pad128 = lambda v: jnp.zeros((1, 128), F32).at[0, :v.shape[0]].set(v)

    w = ev_w_in[0]
    cos, sin = _rope_tables(s_len)
    wdt, wdt_hi = _hi_lo_weight(w[:, 1280:1296])
    p0 = dict(
        g_mix=norm_mix[0].reshape(1, D),
        wcat=jnp.concatenate([w[:, 0:1280].astype(BF16), w[:, 1296:2064].astype(BF16), wdt], axis=1),
        wdt_hi=wdt_hi,
        conv_w=ev_conv_w[0], conv_b=ev_conv_b[0].reshape(1, 768),
        dt_bias=pad128(ev_dt_bias[0].reshape(16)),
        q_norm=jnp.tile(ev_q_norm[0], 8).reshape(1, 512), k_norm=jnp.tile(ev_k_norm[0], 2).reshape(1, 128),
        cos=cos, sin=sin,
        alog_row=ev_a_log[0].reshape(1, 16), alog_col=ev_a_log[0].reshape(16, 1),
        d_skip=jnp.repeat(ev_d_skip[0], 64).reshape(1, 512), ssd_norm=ev_ssd_norm[0].reshape(1, 512))
    z, xs, bc, q, kk, vv, dtc, dtr = _in0(ctx[0], x[0], mod[0], p0, nct)
    yf = _ssd(False, xs, bc, dtc, dtr, p0, ncc, nlc)
    ymix = _ssd(True, xs, bc, dtc, dtr, p0, ncc, nlc, yf=yf, z=z)
    att = _attention(q, kk, vv, ev_sink[0].reshape(1, 8), ncc, nlc)
    x_mid0, hrow, bucket, rank, counts = _out_proj(
        [ymix, att], ev_w_out[0].astype(BF16), [ctx[0], x[0]], mod[0], norm_ffn[0].reshape(1, D),
        rw, rw_hi, rb_col, 0, ntiles, nct)
    y_nat0 = _moe_block(hrow, bucket, rank, counts, moe_w1, moe_w3, moe_w2, 0)

    w = od_w_in[0]
    wg, wg_hi = _hi_lo_weight(w[:, 3072:3104])
    p1 = dict(
        g_mix=norm_mix[1].reshape(1, D),
        wcat=jnp.concatenate([w[:, 0:3072].astype(BF16), wg], axis=1),
        wg_hi=wg_hi,
        conv_w=od_conv_w[0], conv_b=od_conv_b[0].reshape(1, 2048),
        gate_bias=pad128(jnp.concatenate([od_igate_b[0].reshape(16), od_fgate_b[0].reshape(16)])),
        head_norm=od_head_norm[0].reshape(1, 1024))
    x1, q1, kt1, v1, og1, gc1, gr1 = _in1(x_mid0, y_nat0, mod[0], mod[1], p1, nct)
    hf = _mlstm(False, q1, kt1, v1, gc1, gr1, p1, ncc, nlc)
    hmix = _mlstm(True, q1, kt1, v1, gc1, gr1, p1, ncc, nlc, hf=hf, og=og1)
    x_mid1, hrow, bucket, rank, counts = _out_proj(
        [hmix], od_w_out[0].astype(BF16), [x1], mod[1], norm_ffn[1].reshape(1, D),
        rw, rw_hi, rb_col, nct, ntiles - nct, nct)
    y_nat1 = _moe_block(hrow, bucket, rank, counts, moe_w1, moe_w3, moe_w2, 1)
    return _final_add(x_mid1, y_nat1, mod[1])[None]
```

```python
import functools
import math

import jax
import jax.numpy as jnp
from jax import lax
from jax.experimental import pallas as pl
from jax.experimental.pallas import tpu as pltpu

F32 = jnp.float32
BF16 = jnp.bfloat16
I32 = jnp.int32

EPS = 1e-6
D = 1024
T = 128
TM = 256
TMM = 256
GRID_W = 64
ROPE_THETA = 10000.0
N_EXPERTS = 16
N_BUCKETS = 24
NB_PAD = 32
D_EXPERT = 512
TOK = 8
U32 = jnp.uint32
NEG_INF = float("-inf")
VMEM_LIMIT = 56 * 1024 * 1024

_NN = (((1,), (0,)), ((), ()))
_NT = (((1,), (1,)), ((), ()))
_TN = (((0,), (0,)), ((), ()))

_PAIR_A = (0, 0, 0, 1, 1, 2)
_PAIR_B = (1, 2, 3, 2, 3, 3)


def _cparams(*sem):
    return pltpu.CompilerParams(dimension_semantics=sem, vmem_limit_bytes=VMEM_LIMIT)


def _dot(a, b, dims=_NN):
    return lax.dot_general(a, b, dims, preferred_element_type=F32)


def _split(a, n):
    out = []
    r = a
    for _ in range(n):
        t = r.astype(BF16)
        out.append(t)
        r = r - t.astype(F32)
    return out


def _mdot(as_, bs, dims=_NN, order=None):
    if order is None:
        order = len(as_) + len(bs) - 2
    acc = None
    for i, a in enumerate(as_):
        for j, b in enumerate(bs):
            if i + j <= order:
                p = _dot(a, b, dims)
                acc = p if acc is None else acc + p
    return acc


def _sigmoid(x):
    return 1.0 / (1.0 + jnp.exp(-x))


def _silu(x):
    return x * _sigmoid(x)


def _log1p_exp_neg_abs(x):
    e = jnp.exp(-jnp.abs(x))
    u = 1.0 + e
    um1 = u - 1.0
    return jnp.where(um1 == 0.0, e, jnp.log(u) * (e / jnp.where(um1 == 0.0, 1.0, um1)))


def _softplus(x):
    return jnp.maximum(x, 0.0) + _log1p_exp_neg_abs(x)


def _log_sigmoid(x):
    return jnp.minimum(x, 0.0) - _log1p_exp_neg_abs(x)


def _norm_mod(x, g, sc, sh):
    ms = jnp.mean(x * x, axis=-1, keepdims=True)
    return (x * lax.rsqrt(ms + EPS)) * g * (1.0 + sc) + sh


def _tok_load(ref, chunk, n):
    return ref[pl.ds(chunk, n, stride=TOK), :]


def _tok_store(ref, chunk, val):
    ref[pl.ds(chunk, val.shape[0], stride=TOK), :] = val


def _tok_rows(ref, n):
    return jnp.concatenate([_tok_load(ref, c, n) for c in range(TOK)], axis=1)


def _tri(rev):
    r = lax.broadcasted_iota(I32, (T, T), 0)
    c = lax.broadcasted_iota(I32, (T, T), 1)
    return (c >= r) if rev else (c <= r)


def _cumsums(rev, col, row):
    tri = _tri(rev)
    tri_b = tri.astype(F32).astype(BF16)
    trit_b = _tri(not rev).astype(F32).astype(BF16)
    ccol = _mdot([tri_b], _split(col, 3))
    crow = _mdot(_split(row, 3), [trit_b])
    return tri, ccol, crow


def _mod_kernel(c_ref, w_ref, b_ref, o_ref):
    a = _silu(c_ref[...])
    o_ref[0] = _mdot(_split(a, 2), _split(w_ref[0], 2), order=1) + b_ref[0]


def _modulation(c, c_ctx, w_mod, b_mod):
    depth = w_mod.shape[0]
    n = w_mod.shape[2]
    tn = 1536
    cc = jnp.zeros((8, D), F32).at[0].set(c[0]).at[1].set(c_ctx)
    out = pl.pallas_call(
        _mod_kernel,
        grid=(depth, n // tn),
        in_specs=[
            pl.BlockSpec((8, D), lambda l, j: (0, 0)),
            pl.BlockSpec((1, D, tn), lambda l, j: (l, 0, j)),
            pl.BlockSpec((1, 1, tn), lambda l, j: (l, 0, j)),
        ],
        out_specs=pl.BlockSpec((1, 8, tn), lambda l, j: (l, 0, j)),
        out_shape=jax.ShapeDtypeStruct((depth, 8, n), F32),
        compiler_params=_cparams("arbitrary", "arbitrary"),
        name="modulation",
    )(cc, w_mod, b_mod.reshape(depth, 1, n))
    return out[:, :2].reshape(depth, 2, 6, D)


def _halo_specs(nrows, tile_of):
    nb8 = nrows // 8
    return [
        pl.BlockSpec((8, D), lambda i: (jnp.maximum(tile_of(i) * (TM // 8) - 1, 0), 0)),
        pl.BlockSpec((TM, D), lambda i: (tile_of(i), 0)),
        pl.BlockSpec((8, D), lambda i: (jnp.minimum((tile_of(i) + 1) * (TM // 8), nb8 - 1), 0)),
    ]


def _ctx_tile(nct):
    return lambda i: jnp.minimum(i, nct - 1)


def _lat_tile(nct):
    return lambda i: jnp.maximum(i - nct, 0)


def _mod_spec(nct, t0=0):
    return pl.BlockSpec((1, 6, D), lambda i: (jnp.where(i + t0 < nct, 1, 0), 0, 0))


def _full(shape):
    nd = len(shape)
    return pl.BlockSpec(shape, lambda i: (0,) * nd)


def _seq_edges(i, nct, ntiles):
    prev_ok = jnp.logical_and(i != 0, i != nct).astype(F32)
    next_ok = jnp.logical_and(i != nct - 1, i != ntiles - 1).astype(F32)
    return prev_ok, next_ok


def _conv_silu(x, x_first_prev, x_last_next, cw, cb):
    n = x.shape[0]
    rows = lax.broadcasted_iota(I32, x.shape, 0)
    x_prev = jnp.where(rows == 0, x_first_prev, pltpu.roll(x, 1, 0))
    x_next = jnp.where(rows == n - 1, x_last_next, pltpu.roll(x, n - 1, 0))
    return _silu(x_prev * cw[0:1] + x * cw[1:2] + x_next * cw[2:3] + cb)


def _hi_lo_cols(blk, lo_pass, n):
    return blk + pltpu.roll(blk, 128 - n, 1) + lo_pass


def _hi_lo_weight(w):
    n = w.shape[1]
    hi = w.astype(BF16)
    lo = (w - hi.astype(F32)).astype(BF16)
    z = jnp.zeros((w.shape[0], 128 - 2 * n), BF16)
    return jnp.concatenate([hi, lo, z], axis=1), jnp.concatenate([hi, jnp.zeros_like(lo), z], axis=1)


def _head_rms(xf, gamma):
    r = lax.broadcasted_iota(I32, (128, 128), 0) // 64
    c = lax.broadcasted_iota(I32, (128, 128), 1) // 64
    ones_bd = (r == c).astype(F32).astype(BF16)
    outs = []
    for j in range(xf.shape[1] // 128):
        blk = xf[:, 128 * j:128 * (j + 1)]
        ssum = _mdot(_split(blk * blk, 2), [ones_bd])
        outs.append(blk * lax.rsqrt(ssum * (1.0 / 64.0) + EPS))
    return jnp.concatenate(outs, axis=1) * gamma


def _rope(xf, cos, sin):
    lane = lax.broadcasted_iota(I32, (xf.shape[0], 128), 1)
    first = (lane % 32) < 16
    outs = []
    for j in range(xf.shape[1] // 128):
        blk = xf[:, 128 * j:128 * (j + 1)]
        partner = jnp.where(first, pltpu.roll(blk, 112, 1), pltpu.roll(blk, 16, 1))
        outs.append(blk * cos + partner * sin)
    return jnp.concatenate(outs, axis=1)


def _rope_tables(s_len):
    rows = s_len // GRID_W
    inv = ROPE_THETA ** (-(jnp.arange(32, dtype=I32) % 16).astype(F32) / 16.0)
    sign = jnp.where(jnp.arange(32) < 16, -1.0, 1.0).astype(F32)
    ang_r = jnp.arange(rows, dtype=F32)[:, None] * inv[None, :]
    ang_c = jnp.arange(GRID_W, dtype=F32)[:, None] * inv[None, :]

    def table(fr, fc):
        r = jnp.broadcast_to(fr[:, None, :], (rows, GRID_W, 32))
        c = jnp.broadcast_to(fc[None, :, :], (rows, GRID_W, 32))
        t = jnp.concatenate([r, c, r, c], axis=-1)
        return t.reshape(s_len, 128)

    return table(jnp.cos(ang_r), jnp.cos(ang_c)), table(jnp.sin(ang_r) * sign, jnp.sin(ang_c) * sign)


def _in0_kernel(nct, ntiles, cp_ref, c_ref, cn_ref, xp_ref, x_ref, xn_ref, mod_ref, g_ref, wcat_ref, wdth_ref,
                cw_ref, cb_ref, dtb_ref, qn_ref, kn_ref, cos_ref, sin_ref,
                z_ref, xs_ref, bc_ref, q_ref, kk_ref, vv_ref, dtc_ref, dtr_ref):
    i = pl.program_id(0)
    is_ctx = i < nct
    sh = mod_ref[0, 0:1, :]
    sc = mod_ref[0, 1:2, :]
    g = g_ref[...]
    h = _norm_mod(jnp.where(is_ctx, c_ref[...], x_ref[...]), g, sc, sh)
    hb = h.astype(BF16)
    main = _dot(hb, wcat_ref[...])
    prev_ok, next_ok = _seq_edges(i, nct, ntiles)
    wxbc = wcat_ref[:, 512:1280]
    hp = _norm_mod(jnp.where(is_ctx, cp_ref[...], xp_ref[...]), g, sc, sh).astype(BF16)
    hn = _norm_mod(jnp.where(is_ctx, cn_ref[...], xn_ref[...]), g, sc, sh).astype(BF16)
    xb_prev = _dot(hp, wxbc)[7:8, :] * prev_ok
    xb_next = _dot(hn, wxbc)[0:1, :] * next_ok
    act = _conv_silu(main[:, 512:1280], xb_prev, xb_next, cw_ref[...], cb_ref[...])
    z_ref[...] = main[:, 0:512].astype(BF16)
    xs_ref[...] = act[:, 0:512].astype(BF16)
    bc_ref[...] = act[:, 512:768].astype(BF16)
    cos = jnp.where(is_ctx, 1.0, cos_ref[...])
    sin = jnp.where(is_ctx, 0.0, sin_ref[...])
    q = _rope(_head_rms(main[:, 1280:1792], qn_ref[...]), cos, sin) * 0.125
    q_ref[...] = q.astype(BF16)
    k = _rope(_head_rms(main[:, 1792:1920], kn_ref[...]), cos, sin)
    kk_ref[...] = jnp.concatenate([k, pltpu.roll(k, 64, 1)], axis=1).astype(BF16)
    v = main[:, 1920:2048]
    vv_ref[...] = jnp.concatenate([v, pltpu.roll(v, 64, 1)], axis=1).astype(BF16)
    h_lo = (h - hb.astype(F32)).astype(BF16)
    dt = _softplus(_hi_lo_cols(main[:, 2048:2176], _dot(h_lo, wdth_ref[...]), 16) + dtb_ref[...])
    dtc_ref[...] = dt[:, 0:16]
    dtr_ref[...] = jnp.transpose(dt)[0:16, :]


def _in0(ctx2, x2, mod, p, nct):
    c_len, s_len = ctx2.shape[0], x2.shape[0]
    nt = c_len + s_len
    ntiles = nt // TM
    tile = lambda w: pl.BlockSpec((TM, w), lambda i: (i, 0))
    lat = _lat_tile(nct)
    lat_tile = lambda w: pl.BlockSpec((TM, w), lambda i: (lat(i), 0))
    outs = [(512, BF16), (512, BF16), (256, BF16), (512, BF16), (256, BF16), (256, BF16), (16, F32)]
    return pl.pallas_call(
        functools.partial(_in0_kernel, nct, ntiles),
        grid=(ntiles,),
        in_specs=_halo_specs(c_len, _ctx_tile(nct)) + _halo_specs(s_len, lat) + [
            _mod_spec(nct), _full((1, D)), _full((D, 2176)), _full((D, 128)),
            _full((3, 768)), _full((1, 768)), _full((1, 128)),
            _full((1, 512)), _full((1, 128)), lat_tile(128), lat_tile(128)],
        out_specs=[tile(w) for w, _ in outs] + [pl.BlockSpec((16, TM), lambda i: (0, i))],
        out_shape=[jax.ShapeDtypeStruct((nt, w), dt) for w, dt in outs]
        + [jax.ShapeDtypeStruct((16, nt), F32)],
        compiler_params=_cparams("arbitrary"),
        name="in_proj_even",
    )(ctx2, ctx2, ctx2, x2, x2, x2, mod, p["g_mix"], p["wcat"], p["wdt_hi"], p["conv_w"], p["conv_b"],
      p["dt_bias"], p["q_norm"], p["k_norm"], p["cos"], p["sin"])


def _scan_chunk_map(rev, ncc, nlc):
    if not rev:
        return lambda j: j
    return lambda j: jnp.where(j < ncc, ncc - 1 - j, ncc + nlc - 1 - (j - ncc))


def _ssd_kernel(rev, *refs):
    if rev:
        (xs_ref, bc_ref, dtc_ref, dtr_ref, alr_ref, alc_ref, yf_ref, z_ref, dsk_ref, nrm_ref,
         o_ref, st_ref) = refs
    else:
        xs_ref, bc_ref, dtc_ref, dtr_ref, alr_ref, alc_ref, o_ref, st_ref = refs
    j = pl.program_id(0)

    @pl.when(j == 0)
    def _():
        st_ref[...] = jnp.zeros_like(st_ref)

    d = 8 if rev else 0
    a_coef_row = -jnp.exp(alr_ref[...])[:, d:d + 8]
    a_coef_col = -jnp.exp(alc_ref[...])[d:d + 8, :]
    dtc = dtc_ref[:, d:d + 8]
    dtr = dtr_ref[d:d + 8, :]
    tri, acs_col, acs_row = _cumsums(rev, dtc * a_coef_row, dtr * a_coef_col)
    end = 0 if rev else T - 1
    atot_row = acs_col[end:end + 1, :]
    atot_col = acs_row[:, end:end + 1]
    dec_col = jnp.exp(atot_row - acs_col) * dtc
    eacs_col = jnp.exp(acs_col)
    xs = xs_ref[...]
    bm = bc_ref[:, 0:128]
    cm = bc_ref[:, 128:256]
    lane = lax.broadcasted_iota(I32, (T, 128), 1)
    lo = lane < 64
    zero_b = jnp.zeros((T, 128), BF16)
    ys = [None] * 4
    for g in range(2):
        in_g = lo if g == 0 else jnp.logical_not(lo)
        cg = jnp.where(in_g, cm, zero_b)
        bg = jnp.where(in_g, bm, zero_b)
        cb = _dot(cg, bm, _NT)
        for hh in range(4):
            hd = 4 * g + hh
            pair, hi = hd // 2, hd % 2
            diff = acs_col[:, hd:hd + 1] - acs_row[hd:hd + 1, :]
            lmat = jnp.exp(jnp.where(tri, diff, NEG_INF))
            m = (cb * lmat * dtr[hd:hd + 1, :]).astype(BF16)
            xp = xs[:, 128 * pair:128 * (pair + 1)]
            xh = jnp.where(lo if hi == 0 else jnp.logical_not(lo), xp, zero_b)
            st = st_ref[hd]
            y = _dot(m, xh) + _dot(cg, st.astype(BF16)) * eacs_col[:, hd:hd + 1]
            ys[pair] = y if ys[pair] is None else ys[pair] + y
            xdec = (xh.astype(F32) * dec_col[:, hd:hd + 1]).astype(BF16)
            st_ref[hd] = jnp.exp(atot_col[hd:hd + 1, :]) * st + _dot(bg, xdec, _TN)
    y = jnp.concatenate(ys, axis=1)
    if not rev:
        o_ref[...] = y
    else:
        ytot = y + yf_ref[...] + dsk_ref[...] * xs.astype(F32)
        gated = ytot * _silu(z_ref[...].astype(F32))
        ms = jnp.mean(gated * gated, axis=-1, keepdims=True)
        o_ref[...] = (gated * lax.rsqrt(ms + EPS) * nrm_ref[...]).astype(BF16)


def _ssd(rev, xs, bc, dtc, dtr, p, ncc, nlc, yf=None, z=None):
    nt = xs.shape[0]
    cmap = _scan_chunk_map(rev, ncc, nlc)
    blk = lambda w: pl.BlockSpec((T, w), lambda j: (cmap(j), 0))
    in_specs = [blk(512), blk(256), blk(16), pl.BlockSpec((16, T), lambda j: (0, cmap(j))),
                _full((1, 16)), _full((16, 1))]
    args = [xs, bc, dtc, dtr, p["alog_row"], p["alog_col"]]
    if rev:
        in_specs += [blk(512), blk(512), _full((1, 512)), _full((1, 512))]
        args += [yf, z, p["d_skip"], p["ssd_norm"]]
    return pl.pallas_call(
        functools.partial(_ssd_kernel, rev),
        grid=(ncc + nlc,),
        in_specs=in_specs,
        out_specs=blk(512),
        out_shape=jax.ShapeDtypeStruct((nt, 512), BF16 if rev else F32),
        scratch_shapes=[pltpu.VMEM((8, 128, 128), F32)],
        compiler_params=_cparams("arbitrary"),
        name="ssd_bwd" if rev else "ssd_fwd",
    )(*args)


def _attn_kernel(ncc, nblk, q_ref, kp_ref, kc_ref, kn_ref, vp_ref, vc_ref, vn_ref, kx_ref, vx_ref,
                 sink_ref, o_ref):
    j = pl.program_id(0)
    c_len = kx_ref.shape[0]
    is_lat = j >= ncc
    prev_ok = jnp.logical_and(is_lat, j >= ncc + 1)
    next_ok = jnp.logical_and(is_lat, j <= nblk - 2)
    r = lax.broadcasted_iota(I32, (T, T), 0)
    c = lax.broadcasted_iota(I32, (T, T), 1)
    zero = jnp.zeros((T, T), F32)
    ninf = jnp.full((T, T), NEG_INF, F32)
    bias = jnp.concatenate([
        jnp.where(jnp.logical_and(prev_ok, c >= r), zero, ninf),
        jnp.where(is_lat, zero, ninf),
        jnp.where(jnp.logical_and(next_ok, c <= r), zero, ninf),
        jnp.zeros((T, c_len), F32)], axis=1)
    k_all = jnp.concatenate([kp_ref[...], kc_ref[...], kn_ref[...], kx_ref[...]], axis=0)
    v_all = jnp.concatenate([vp_ref[...], vc_ref[...], vn_ref[...], vx_ref[...]], axis=0)
    nk = k_all.shape[0]
    lane = lax.broadcasted_iota(I32, (nk, 128), 1)
    lo = lane < 64
    zero_b = jnp.zeros((nk, 128), BF16)
    q = q_ref[...]
    sink = sink_ref[...]
    for g in range(2):
        k_lo = jnp.where(lo, k_all[:, 128 * g:128 * (g + 1)], zero_b)
        k_hi = jnp.where(lo, zero_b, k_all[:, 128 * (1 - g):128 * (2 - g)])
        v_lo = jnp.where(lo, v_all[:, 128 * g:128 * (g + 1)], zero_b)
        v_hi = jnp.where(lo, zero_b, v_all[:, 128 * (1 - g):128 * (2 - g)])
        for pp in range(2):
            pair = 2 * g + pp
            qp = q[:, 128 * pair:128 * (pair + 1)]
            acc = None
            for hi in range(2):
                hd = 2 * pair + hi
                s = _dot(qp, k_hi if hi else k_lo, _NT) + bias
                sk = sink[:, hd:hd + 1]
                m = jnp.maximum(jnp.max(s, axis=-1, keepdims=True), sk)
                pr = jnp.exp(s - m)
                den = jnp.sum(pr, axis=-1, keepdims=True) + jnp.exp(sk - m)
                o = _dot(pr.astype(BF16), v_hi if hi else v_lo) / den
                acc = o if acc is None else acc + o
            o_ref[:, 128 * pair:128 * (pair + 1)] = acc.astype(BF16)


def _attention(q, kk, vv, sink, ncc, nlc):
    nt = q.shape[0]
    nblk = ncc + nlc
    c_len = ncc * T
    prev = lambda w: pl.BlockSpec((T, w), lambda j: (jnp.maximum(j - 1, 0), 0))
    cur = lambda w: pl.BlockSpec((T, w), lambda j: (j, 0))
    nxt = lambda w: pl.BlockSpec((T, w), lambda j: (jnp.minimum(j + 1, nblk - 1), 0))
    ctx = lambda w: pl.BlockSpec((c_len, w), lambda j: (0, 0))
    return pl.pallas_call(
        functools.partial(_attn_kernel, ncc, nblk),
        grid=(nblk,),
        in_specs=[cur(512), prev(256), cur(256), nxt(256), prev(256), cur(256), nxt(256),
                  ctx(256), ctx(256), _full((1, 8))],
        out_specs=cur(512),
        out_shape=jax.ShapeDtypeStruct((nt, 512), BF16),
        compiler_params=_cparams("arbitrary"),
        name="window_attention",
    )(q, kk, kk, kk, vv, vv, vv, kk, vv, sink)


def _in1_kernel(nct, ntiles, xp_ref, x_ref, xn_ref, mod_ref, g_ref,
                wcat_ref, wgh_ref, cw_ref, cb_ref, gb_ref,
                q_ref, kt_ref, v_ref, o_ref, gc_ref, gr_ref):
    i = pl.program_id(0)
    sh = mod_ref[0, 0:1, :]
    sc = mod_ref[0, 1:2, :]
    g = g_ref[...]
    h = _norm_mod(x_ref[...], g, sc, sh)
    hb = h.astype(BF16)
    main = _dot(hb, wcat_ref[...])
    prev_ok, next_ok = _seq_edges(i, nct, ntiles)
    wqkv = wcat_ref[:, 0:2048]
    hp = _norm_mod(xp_ref[...], g, sc, sh).astype(BF16)
    hn = _norm_mod(xn_ref[...], g, sc, sh).astype(BF16)
    x_prev = _dot(hp, wqkv)[7:8, :] * prev_ok
    x_next = _dot(hn, wqkv)[0:1, :] * next_ok
    act = _conv_silu(main[:, 0:2048], x_prev, x_next, cw_ref[...], cb_ref[...])
    q_ref[...] = act[:, 0:512].astype(BF16)
    kt_ref[...] = jnp.transpose(act[:, 512:1024] * 0.125).astype(BF16)
    v_ref[...] = act[:, 1024:2048].astype(BF16)
    o_ref[...] = main[:, 2048:3072].astype(BF16)
    h_lo = (h - hb.astype(F32)).astype(BF16)
    gates = _hi_lo_cols(main[:, 3072:3200], _dot(h_lo, wgh_ref[...]), 32) + gb_ref[...]
    lane = lax.broadcasted_iota(I32, gates.shape, 1)
    gates = jnp.where(lane < 16, gates, _log_sigmoid(gates))
    gc_ref[...] = gates[:, 0:32]
    gr_ref[...] = jnp.transpose(gates)[0:32, :]


def _in1(x1, mod, p, nct):
    nt = x1.shape[0]
    ntiles = nt // TM
    tile = lambda w: pl.BlockSpec((TM, w), lambda i: (i, 0))
    return pl.pallas_call(
        functools.partial(_in1_kernel, nct, ntiles),
        grid=(ntiles,),
        in_specs=_halo_specs(nt, lambda i: i) + [
            _mod_spec(nct), _full((1, D)), _full((D, 3200)), _full((D, 128)),
            _full((3, 2048)), _full((1, 2048)), _full((1, 128))],
        out_specs=[tile(512), pl.BlockSpec((512, TM), lambda i: (0, i)), tile(1024), tile(1024),
                   tile(32), pl.BlockSpec((32, TM), lambda i: (0, i))],
        out_shape=[jax.ShapeDtypeStruct((nt, 512), BF16), jax.ShapeDtypeStruct((512, nt), BF16),
                   jax.ShapeDtypeStruct((nt, 1024), BF16), jax.ShapeDtypeStruct((nt, 1024), BF16),
                   jax.ShapeDtypeStruct((nt, 32), F32), jax.ShapeDtypeStruct((32, nt), F32)],
        compiler_params=_cparams("arbitrary"),
        name="in_proj_odd",
    )(x1, x1, x1, mod, p["g_mix"], p["wcat"], p["wg_hi"], p["conv_w"], p["conv_b"], p["gate_bias"])


def _mlstm_kernel(rev, *refs):
    if rev:
        (q_ref, kt_ref, v_ref, gc_ref, gr_ref, hf_ref, og_ref, hn_ref, o_ref,
         c_ref, mc_ref, mr_ref) = refs
    else:
        q_ref, kt_ref, v_ref, gc_ref, gr_ref, o_ref, c_ref, mc_ref, mr_ref = refs
    j = pl.program_id(0)

    @pl.when(j == 0)
    def _():
        c_ref[...] = jnp.zeros_like(c_ref)
        mc_ref[...] = jnp.zeros_like(mc_ref)
        mr_ref[...] = jnp.zeros_like(mr_ref)

    d = 8 if rev else 0
    ig_col = gc_ref[:, d:d + 8]
    lf_col = gc_ref[:, 16 + d:24 + d]
    ig_row = gr_ref[d:d + 8, :]
    lf_row = gr_ref[16 + d:24 + d, :]
    tri, b_col, b_row = _cumsums(rev, lf_col, lf_row)
    end = 0 if rev else T - 1
    blast_row = b_col[end:end + 1, :]
    blast_col = b_row[:, end:end + 1]
    wend_row = blast_col - b_row + ig_row
    ac_col = jnp.max(wend_row, axis=1, keepdims=True)
    eend_row = jnp.exp(wend_row - ac_col)
    ac_row = jnp.max(blast_row - b_col + ig_col, axis=0, keepdims=True)
    m_col = mc_ref[:, 0:1]
    m_row = mr_ref[0:1, 0:8]
    mnew_col = jnp.maximum(blast_col + m_col, ac_col)
    sp_col = jnp.exp(blast_col + m_col - mnew_col)
    sc_col = jnp.exp(ac_col - mnew_col)
    mnew_row = jnp.maximum(blast_row + m_row, ac_row)
    g_col = b_col + m_row
    q = q_ref[...]
    ones_b = jnp.ones((T, 128), BF16)
    sub = lax.broadcasted_iota(I32, (128, T), 0)
    zero_k = jnp.zeros((128, T), BF16)
    for hd in range(8):
        pair, hi = hd // 2, hd % 2
        qp = q[:, 128 * pair:128 * (pair + 1)]
        ktp = kt_ref[128 * pair:128 * (pair + 1), :]
        kth = jnp.where((sub >= 64) if hi else (sub < 64), ktp, zero_k)
        vaug = jnp.concatenate([v_ref[:, 128 * hd:128 * (hd + 1)], ones_b], axis=1)
        dlog = jnp.where(tri, b_col[:, hd:hd + 1] - b_row[hd:hd + 1, :] + ig_row[hd:hd + 1, :], NEG_INF)
        gh = g_col[:, hd:hd + 1]
        mstar = jnp.maximum(gh, jnp.max(dlog, axis=-1, keepdims=True))
        w = (jnp.exp(dlog - mstar) * _dot(qp, kth)).astype(BF16)
        cst = c_ref[hd]
        nd = _dot(w, vaug) + jnp.exp(gh - mstar) * _dot(qp, cst.astype(BF16))
        den = jnp.maximum(jnp.abs(nd[:, 128:256]), jnp.exp(-mstar))
        hh = nd[:, 0:128] / den
        if rev:
            hh = hh + hf_ref[:, 128 * hd:128 * (hd + 1)]
            ms = jnp.mean(hh * hh, axis=-1, keepdims=True)
            hh = hh * lax.rsqrt(ms + EPS) * hn_ref[:, 128 * hd:128 * (hd + 1)]
            og = og_ref[:, 128 * hd:128 * (hd + 1)].astype(F32)
            o_ref[:, 128 * hd:128 * (hd + 1)] = (hh * _sigmoid(og)).astype(BF16)
        else:
            o_ref[:, 128 * hd:128 * (hd + 1)] = hh
        kte = (kth.astype(F32) * eend_row[hd:hd + 1, :]).astype(BF16)
        c_ref[hd] = sp_col[hd:hd + 1, :] * cst + sc_col[hd:hd + 1, :] * _dot(kte, vaug)
    mc_ref[...] = jnp.broadcast_to(mnew_col, mc_ref.shape)
    mr_ref[...] = jnp.broadcast_to(jnp.concatenate([mnew_row, jnp.zeros((1, 120), F32)], axis=1), mr_ref.shape)


def _mlstm(rev, q, kt, v, gc, gr, p, ncc, nlc, hf=None, og=None):
    nt = q.shape[0]
    cmap = _scan_chunk_map(rev, ncc, nlc)
    blk = lambda w: pl.BlockSpec((T, w), lambda j: (cmap(j), 0))
    blk_t = lambda h: pl.BlockSpec((h, T), lambda j: (0, cmap(j)))
    in_specs = [blk(512), blk_t(512), blk(1024), blk(32), blk_t(32)]
    args = [q, kt, v, gc, gr]
    if rev:
        in_specs += [blk(1024), blk(1024), _full((1, 1024))]
        args += [hf, og, p["head_norm"]]
    return pl.pallas_call(
        functools.partial(_mlstm_kernel, rev),
        grid=(ncc + nlc,),
        in_specs=in_specs,
        out_specs=blk(1024),
        out_shape=jax.ShapeDtypeStruct((nt, 1024), BF16 if rev else F32),
        scratch_shapes=[pltpu.VMEM((8, 128, 256), F32), pltpu.VMEM((8, 128), F32), pltpu.VMEM((8, 128), F32)],
        compiler_params=_cparams("arbitrary"),
        name="mlstm_bwd" if rev else "mlstm_fwd",
    )(*args)


def _route(logits_t, rb_col):
    scores = _sigmoid(logits_t)
    biased = scores + rb_col
    row = lambda a, e: a[e:e + 1, :]
    gscore = []
    for g in range(4):
        b0, b1, b2, b3 = (row(biased, 4 * g + e) for e in range(4))
        h1, l1 = jnp.maximum(b0, b1), jnp.minimum(b0, b1)
        h2, l2 = jnp.maximum(b2, b3), jnp.minimum(b2, b3)
        gscore.append(jnp.maximum(h1, h2) + jnp.maximum(jnp.minimum(h1, h2), jnp.maximum(l1, l2)))
    gidx = jnp.zeros_like(gscore[0], dtype=I32)
    best = gscore[0]
    for g in range(1, 4):
        better = gscore[g] > best
        gidx = jnp.where(better, g, gidx)
        best = jnp.where(better, gscore[g], best)

    def pick(a, e):
        out = row(a, e)
        for g in range(1, 4):
            out = jnp.where(gidx == g, row(a, 4 * g + e), out)
        return out

    sb = [pick(biased, e) for e in range(4)]
    ss = [pick(scores, e) for e in range(4)]
    i1 = jnp.zeros_like(gidx)
    v1, s1 = sb[0], ss[0]
    for e in range(1, 4):
        better = sb[e] > v1
        i1 = jnp.where(better, e, i1)
        v1 = jnp.where(better, sb[e], v1)
        s1 = jnp.where(better, ss[e], s1)
    i2 = jnp.zeros_like(gidx)
    v2 = jnp.full_like(v1, NEG_INF)
    s2 = jnp.zeros_like(s1)
    for e in range(4):
        better = jnp.logical_and(i1 != e, sb[e] > v2)
        i2 = jnp.where(better, e, i2)
        v2 = jnp.where(better, sb[e], v2)
        s2 = jnp.where(better, ss[e], s2)
    tot = s1 + s2
    w1 = s1 / tot
    w2 = s2 / tot
    first_low = i1 < i2
    a = jnp.minimum(i1, i2)
    b = jnp.maximum(i1, i2)
    off = jnp.where(a == 0, 0, jnp.where(a == 1, 3, 5))
    bucket = 6 * gidx + off + (b - a - 1)
    return bucket, jnp.where(first_low, w1, w2), jnp.where(first_low, w2, w1)


def _out_kernel(nmix, nct, two_src, *refs):
    mix_refs = refs[:nmix]
    refs = refs[nmix:]
    if two_src:
        c_ref, x_ref = refs[:2]
        refs = refs[2:]
    else:
        x_ref = refs[0]
        refs = refs[1:]
    (w_ref, mod_ref, g_ref, rw_ref, rwh_ref, rb_ref,
     xmid_ref, hrow_ref, bucket_ref, rank_ref, cnt_ref, cnt_scr) = refs
    i = pl.program_id(0)

    @pl.when(i == 0)
    def _():
        cnt_scr[...] = jnp.zeros_like(cnt_scr)

    mix = mix_refs[0][...] if nmix == 1 else jnp.concatenate([mr[...] for mr in mix_refs], axis=1)
    x = jnp.where(i < nct, c_ref[...], x_ref[...]) if two_src else x_ref[...]
    x_mid = x + mod_ref[0, 2:3, :] * _dot(mix, w_ref[...])
    xmid_ref[...] = x_mid
    h = _norm_mod(x_mid, g_ref[...], mod_ref[0, 4:5, :], mod_ref[0, 3:4, :])
    hb = h.astype(BF16)
    h_lo = (h - hb.astype(F32)).astype(BF16)
    logits = _hi_lo_cols(_dot(hb, rw_ref[...]), _dot(h_lo, rwh_ref[...]), 16)
    logits_t = jnp.transpose(logits)[0:16, :]
    bucket, w_lo, w_hi = _route(logits_t, rb_ref[...])
    bits = pltpu.bitcast(hb.astype(F32), U32)
    words = (bits[:, 512:1024] & jnp.uint32(0xFFFF0000)) | (bits[:, 0:512] >> 16)
    for cblk in range(4):
        _tok_store(hrow_ref, cblk, words[:, 128 * cblk:128 * (cblk + 1)])
    wrows = jnp.concatenate([w_lo, w_hi, jnp.zeros((126, TM), F32)], axis=0)
    _tok_store(hrow_ref, 4, pltpu.bitcast(jnp.transpose(wrows), U32))
    for cblk in range(5, TOK):
        _tok_store(hrow_ref, cblk, jnp.zeros((TM, 128), U32))
    brow = lax.broadcasted_iota(I32, (NB_PAD, TM), 0)
    onehot = (brow == bucket).astype(F32)
    r = lax.broadcasted_iota(I32, (TM, TM), 0)
    c = lax.broadcasted_iota(I32, (TM, TM), 1)
    before = (r < c).astype(F32).astype(BF16)
    cum = _dot(onehot.astype(BF16), before)
    base = cnt_scr[:, 0:1]
    rank = jnp.sum(onehot * (cum + base), axis=0, keepdims=True)
    bucket_ref[0] = bucket
    rank_ref[0] = rank.astype(I32)
    cnt = cnt_scr[...] + jnp.sum(onehot, axis=1, keepdims=True)
    cnt_scr[...] = cnt
    cnt_ref[...] = cnt.astype(I32)


def _out_proj(mixes, w_out, xs, mod, g_ffn, rw, rw_hi, rb_col, t0, ntiles, nct):
    nmix = len(mixes)
    two_src = len(xs) == 2
    n = ntiles * TM
    tile_in = lambda w: pl.BlockSpec((TM, w), lambda i: (i + t0, 0))
    tile_out = lambda w: pl.BlockSpec((TM, w), lambda i: (i, 0))
    row_out = pl.BlockSpec((1, 1, TM), lambda i: (i, 0, 0))
    if two_src:
        ctx_of, lat_of = _ctx_tile(nct), _lat_tile(nct)
        x_specs = [pl.BlockSpec((TM, D), lambda i: (ctx_of(i), 0)), pl.BlockSpec((TM, D), lambda i: (lat_of(i), 0))]
    else:
        x_specs = [tile_in(D)]
    in_specs = [tile_in(mx.shape[1]) for mx in mixes] + x_specs + [
        _full((D, D)), _mod_spec(nct, t0), _full((1, D)), _full((D, 128)), _full((D, 128)), _full((16, 1))]
    return pl.pallas_call(
        functools.partial(_out_kernel, nmix, nct, two_src),
        grid=(ntiles,),
        in_specs=in_specs,
        out_specs=[tile_out(D), pl.BlockSpec((TM * TOK, 128), lambda i: (i, 0)), row_out, row_out,
                   _full((NB_PAD, 128))],
        out_shape=[jax.ShapeDtypeStruct((n, D), F32), jax.ShapeDtypeStruct((n * TOK, 128), U32),
                   jax.ShapeDtypeStruct((ntiles, 1, TM), I32), jax.ShapeDtypeStruct((ntiles, 1, TM), I32),
                   jax.ShapeDtypeStruct((NB_PAD, 128), I32)],
        scratch_shapes=[pltpu.VMEM((NB_PAD, 128), F32)],
        compiler_params=_cparams("arbitrary"),
        name="out_proj_router",
    )(*mixes, *xs, w_out, mod, g_ffn, rw, rw_hi, rb_col)


def _scatter_kernel(ntiles, pos_ref, flo_ref, fhi_ref, h_ref, dst_ref, sem):
    i = pl.program_id(0)

    def tok(ref, t):
        return ref.at[pl.ds(pl.multiple_of(t * TOK, TOK), TOK), :]

    def copy(r, d_row):
        return pltpu.make_async_copy(tok(h_ref, r), tok(dst_ref, d_row), sem)

    def wait_rows(lo, hi, unroll):
        def body(r, carry):
            copy(0, 0).wait()
            return carry
        lax.fori_loop(lo, hi, body, 0, unroll=unroll)

    def start(r, c):
        copy(r, pos_ref[i * TM + r]).start()
        return c
    lax.fori_loop(0, TM, start, 0, unroll=8)
    wait_rows(0, TM, 8)

    @pl.when(i == ntiles - 1)
    def _():
        for b in range(N_BUCKETS):
            lo, hi = flo_ref[b], fhi_ref[b]

            def fill(r, c):
                copy(0, r).start()
                return c
            lax.fori_loop(lo, hi, fill, 0)
            wait_rows(lo, hi, 1)

        def tile_copy(j):
            rows = TMM * TOK
            return pltpu.make_async_copy(h_ref, dst_ref.at[pl.ds(pl.multiple_of(j * rows, rows), rows), :], sem)

        def fill_tile(j, c):
            tile_copy(j).start()
            return c

        def wait_tile(j, c):
            tile_copy(j).wait()
            return c
        lax.fori_loop(flo_ref[N_BUCKETS], fhi_ref[N_BUCKETS], fill_tile, 0)
        lax.fori_loop(flo_ref[N_BUCKETS], fhi_ref[N_BUCKETS], wait_tile, 0)


def _scatter_rows(hrow, pos, fill_lo, fill_hi, n, npad):
    assert TM == TMM
    return pl.pallas_call(
        functools.partial(_scatter_kernel, n // TM),
        grid_spec=pltpu.PrefetchScalarGridSpec(
            num_scalar_prefetch=3,
            grid=(n // TM,),
            in_specs=[pl.BlockSpec((TM * TOK, 128), lambda i, *_: (i, 0))],
            out_specs=pl.BlockSpec(memory_space=pl.ANY),
            scratch_shapes=[pltpu.SemaphoreType.DMA(())],
        ),
        out_shape=jax.ShapeDtypeStruct((npad * TOK, 128), hrow.dtype),
        compiler_params=_cparams("arbitrary"),
        name="moe_scatter_rows",
    )(pos, fill_lo, fill_hi, hrow)


def _combine_kernel(ntiles, pos_ref, x_ref, mod_ref, ys_ref, o_ref, ybuf, sem):
    i = pl.program_id(0)

    def copy(tile, slot, r):
        src = pl.multiple_of(pos_ref[tile * TM + r] * TOK, TOK)
        dst = pl.multiple_of(r * TOK, TOK)
        return pltpu.make_async_copy(ys_ref.at[pl.ds(src, TOK), :], ybuf.at[slot, pl.ds(dst, TOK), :], sem.at[slot])

    def start_tile(tile, slot):
        def body(r, carry):
            copy(tile, slot, r).start()
            return carry
        lax.fori_loop(0, TM, body, 0, unroll=8)

    @pl.when(i == 0)
    def _():
        start_tile(0, 0)

    @pl.when(i + 1 < ntiles)
    def _():
        start_tile(i + 1, (i + 1) % 2)

    slot = i % 2

    def wait_body(r, carry):
        copy(i, slot, 0).wait()
        return carry
    lax.fori_loop(0, TM, wait_body, 0, unroll=8)
    o_ref[...] = x_ref[...] + mod_ref[0, 5:6, :] * _tok_rows(ybuf.at[slot], TM)


def _combine(x_mid, mod, ys, pos, t0, nct):
    n = x_mid.shape[0]
    ntiles = n // TM
    return pl.pallas_call(
        functools.partial(_combine_kernel, ntiles),
        grid_spec=pltpu.PrefetchScalarGridSpec(
            num_scalar_prefetch=1,
            grid=(ntiles,),
            in_specs=[pl.BlockSpec((TM, D), lambda i, *_: (i, 0)),
                      pl.BlockSpec((1, 6, D), lambda i, *_: (jnp.where(i + t0 < nct, 1, 0), 0, 0)),
                      pl.BlockSpec(memory_space=pl.ANY)],
            out_specs=pl.BlockSpec((TM, D), lambda i, *_: (i, 0)),
            scratch_shapes=[pltpu.VMEM((2, TM * TOK, 128), F32), pltpu.SemaphoreType.DMA((2,))],
        ),
        out_shape=jax.ShapeDtypeStruct((n, D), F32),
        compiler_params=_cparams("arbitrary"),
        name="moe_combine",
    )(pos, x_mid, mod, ys)


def _moe_kernel(tea_ref, teb_ref, tblk_ref, tval_ref, x_ref, w1a_ref, w3a_ref, w2a_ref,
                w1b_ref, w3b_ref, w2b_ref, y_ref):
    del tea_ref, teb_ref, tblk_ref
    j = pl.program_id(0)

    @pl.when(tval_ref[j] != 0)
    def _():
        words = [_tok_load(x_ref, cblk, TMM) for cblk in range(4)]
        low = [pltpu.bitcast(wd << 16, F32).astype(BF16) for wd in words]
        high = [pltpu.bitcast(wd & jnp.uint32(0xFFFF0000), F32).astype(BF16) for wd in words]
        hb = jnp.concatenate(low + high, axis=1)
        gates = pltpu.bitcast(_tok_load(x_ref, 4, TMM), F32)
        acts = []
        for w1_ref, w3_ref, lane in ((w1a_ref, w3a_ref, 0), (w1b_ref, w3b_ref, 1)):
            u = _dot(hb, w1_ref[0, 0].astype(BF16))
            v = _dot(hb, w3_ref[0, 0].astype(BF16))
            acts.append((_silu(u) * v * gates[:, lane:lane + 1]).astype(BF16))
        y = _dot(acts[0], w2a_ref[0, 0].astype(BF16)) + _dot(acts[1], w2b_ref[0, 0].astype(BF16))
        for cblk in range(TOK):
            _tok_store(y_ref, cblk, y[:, 128 * cblk:128 * (cblk + 1)])

    @pl.when(tval_ref[j] == 0)
    def _():
        y_ref[...] = jnp.zeros_like(y_ref)


def _moe(xs_sorted, w1, w3, w2, layer, tile_ea, tile_eb, tile_blk, tile_valid):
    npad = xs_sorted.shape[0] // TOK
    ntile = npad // TMM
    wspec = lambda shape, which: pl.BlockSpec(
        (1, 1) + shape, lambda j, ea, eb, blk, val: (layer, (ea, eb)[which][j], 0, 0))
    up, down = (D, D_EXPERT), (D_EXPERT, D)
    return pl.pallas_call(
        _moe_kernel,
        grid_spec=pltpu.PrefetchScalarGridSpec(
            num_scalar_prefetch=4,
            grid=(ntile,),
            in_specs=[pl.BlockSpec((TMM * TOK, 128), lambda j, ea, eb, blk, val: (blk[j], 0)),
                      wspec(up, 0), wspec(up, 0), wspec(down, 0), wspec(up, 1), wspec(up, 1), wspec(down, 1)],
            out_specs=pl.BlockSpec((TMM * TOK, 128), lambda j, ea, eb, blk, val: (j, 0)),
        ),
        out_shape=jax.ShapeDtypeStruct((npad * TOK, 128), F32),
        compiler_params=_cparams("arbitrary"),
        name="moe_experts",
    )(tile_ea, tile_eb, tile_blk, tile_valid, xs_sorted, w1, w3, w2, w1, w3, w2)


def _moe_block(x_mid, hrow, bucket, rank, counts, mod, w1, w3, w2, layer, t0, nct):
    n = hrow.shape[0] // TOK
    ntile = n // TMM + N_BUCKETS
    npad = ntile * TMM
    cnt = counts[:N_BUCKETS, 0]
    padded = ((cnt + TMM - 1) // TMM) * TMM
    ends = jnp.cumsum(padded)
    starts = ends - padded
    total_tiles = ends[-1] // TMM
    tiles = jnp.arange(ntile, dtype=I32)
    tile_valid = (tiles < total_tiles).astype(I32)
    tile_blk = jnp.minimum(tiles, jnp.maximum(total_tiles - 1, 0))
    tile_bucket = jnp.minimum(jnp.sum((ends[None, :] <= (tile_blk * TMM)[:, None]).astype(I32), axis=1), N_BUCKETS - 1)
    pair = tile_bucket % 6
    grp = tile_bucket // 6
    pair_a = jnp.where(pair < 3, 0, jnp.where(pair < 5, 1, 2))
    pair_b = jnp.where(pair < 3, pair + 1, jnp.where(pair < 5, pair - 1, 3))
    tile_ea = (4 * grp + pair_a).astype(I32)
    tile_eb = (4 * grp + pair_b).astype(I32)
    bucket = bucket.reshape(-1)
    onehot = (bucket[:, None] == jnp.arange(N_BUCKETS, dtype=I32)[None, :]).astype(I32)
    pos = (rank.reshape(-1) + jnp.sum(onehot * starts[None, :], axis=1)).astype(I32)
    pad32 = lambda a, tail: jnp.zeros((NB_PAD,), I32).at[:N_BUCKETS].set(a.astype(I32)).at[N_BUCKETS].set(tail)
    fill_lo, fill_hi = pad32(starts + cnt, total_tiles), pad32(ends, ntile)
    xs_sorted = _scatter_rows(hrow, pos, fill_lo, fill_hi, n, npad)
    ys = _moe(xs_sorted, w1, w3, w2, layer, tile_ea, tile_eb, tile_blk, tile_valid)
    return _combine(x_mid, mod, ys, pos, t0, nct)


def kernel(x, c, ctx, c_ctx, router_w, router_b, norm_mix, norm_ffn, w_mod, b_mod, ev_w_in, ev_conv_w, ev_conv_b, ev_dt_bias, ev_a_log, ev_d_skip, ev_ssd_norm, ev_q_norm, ev_k_norm, ev_sink, ev_w_out, od_w_in, od_conv_w, od_conv_b, od_igate_b, od_fgate_b, od_head_norm, od_w_out, moe_w1, moe_w3, moe_w2):
    s_len = x.shape[1]
    c_len = ctx.shape[1]
    assert x.shape[0] == 1 and s_len % TM == 0 and c_len % TM == 0 and s_len % GRID_W == 0
    nct = c_len // TM
    ncc, nlc = c_len // T, s_len // T
    nt = c_len + s_len
    ntiles = nt // TM

    mod = _modulation(c, c_ctx, w_mod, b_mod)
    rw, rw_hi = _hi_lo_weight(router_w)
    rb_col = router_b.reshape(N_EXPERTS, 1)
    pad128 = lambda v: jnp.zeros((1, 128), F32).at[0, :v.shape[0]].set(v)

    w = ev_w_in[0]
    cos, sin = _rope_tables(s_len)
    wdt, wdt_hi = _hi_lo_weight(w[:, 1280:1296])
    p0 = dict(
        g_mix=norm_mix[0].reshape(1, D),
        wcat=jnp.concatenate([w[:, 0:1280].astype(BF16), w[:, 1296:2064].astype(BF16), wdt], axis=1),
        wdt_hi=wdt_hi,
        conv_w=ev_conv_w[0], conv_b=ev_conv_b[0].reshape(1, 768),
        dt_bias=pad128(ev_dt_bias[0].reshape(16)),
        q_norm=jnp.tile(ev_q_norm[0], 8).reshape(1, 512), k_norm=jnp.tile(ev_k_norm[0], 2).reshape(1, 128),
        cos=cos, sin=sin,
        alog_row=ev_a_log[0].reshape(1, 16), alog_col=ev_a_log[0].reshape(16, 1),
        d_skip=jnp.repeat(ev_d_skip[0], 64).reshape(1, 512), ssd_norm=ev_ssd_norm[0].reshape(1, 512))
    z, xs, bc, q, kk, vv, dtc, dtr = _in0(ctx[0], x[0], mod[0], p0, nct)
    yf = _ssd(False, xs, bc, dtc, dtr, p0, ncc, nlc)
    ymix = _ssd(True, xs, bc, dtc, dtr, p0, ncc, nlc, yf=yf, z=z)
    att = _attention(q, kk, vv, ev_sink[0].reshape(1, 8), ncc, nlc)
    x_mid0, hrow, bucket, rank, counts = _out_proj(
        [ymix, att], ev_w_out[0].astype(BF16), [ctx[0], x[0]], mod[0], norm_ffn[0].reshape(1, D),
        rw, rw_hi, rb_col, 0, ntiles, nct)
    x1 = _moe_block(x_mid0, hrow, bucket, rank, counts, mod[0], moe_w1, moe_w3, moe_w2, 0, 0, nct)

    w = od_w_in[0]
    wg, wg_hi = _hi_lo_weight(w[:, 3072:3104])
    p1 = dict(
        g_mix=norm_mix[1].reshape(1, D),
        wcat=jnp.concatenate([w[:, 0:3072].astype(BF16), wg], axis=1),
        wg_hi=wg_hi,
        conv_w=od_conv_w[0], conv_b=od_conv_b[0].reshape(1, 2048),
        gate_bias=pad128(jnp.concatenate([od_igate_b[0].reshape(16), od_fgate_b[0].reshape(16)])),
        head_norm=od_head_norm[0].reshape(1, 1024))
    q1, kt1, v1, og1, gc1, gr1 = _in1(x1, mod[1], p1, nct)
    hf = _mlstm(False, q1, kt1, v1, gc1, gr1, p1, ncc, nlc)
    hmix = _mlstm(True, q1, kt1, v1, gc1, gr1, p1, ncc, nlc, hf=hf, og=og1)
    x_mid1, hrow, bucket, rank, counts = _out_proj(
        [hmix], od_w_out[0].astype(BF16), [x1], mod[1], norm_ffn[1].reshape(1, D),
        rw, rw_hi, rb_col, nct, ntiles - nct, nct)
    return _moe_block(x_mid1, hrow, bucket, rank, counts, mod[1], moe_w1, moe_w3, moe_w2, 1, nct, nct)[None]
```

```python
import functools
import math

import jax
import jax.numpy as jnp
from jax import lax
from jax.experimental import pallas as pl
from jax.experimental.pallas import tpu as pltpu

F32 = jnp.float32
BF16 = jnp.bfloat16
I32 = jnp.int32

EPS = 1e-6
D = 1024
T = 128
TM = 256
TMM = 256
GRID_W = 64
ROPE_THETA = 10000.0
N_EXPERTS = 16
N_BUCKETS = 24
NB_PAD = 32
D_EXPERT = 512
TOK = 8
U32 = jnp.uint32
NEG_INF = float("-inf")
VMEM_LIMIT = 56 * 1024 * 1024

_NN = (((1,), (0,)), ((), ()))
_NT = (((1,), (1,)), ((), ()))
_TN = (((0,), (0,)), ((), ()))

_PAIR_A = (0, 0, 0, 1, 1, 2)
_PAIR_B = (1, 2, 3, 2, 3, 3)


def _cparams(*sem):
    return pltpu.CompilerParams(dimension_semantics=sem, vmem_limit_bytes=VMEM_LIMIT)


def _dot(a, b, dims=_NN):
    return lax.dot_general(a, b, dims, preferred_element_type=F32)


def _split(a, n):
    out = []
    r = a
    for _ in range(n):
        t = r.astype(BF16)
        out.append(t)
        r = r - t.astype(F32)
    return out


def _mdot(as_, bs, dims=_NN, order=None):
    if order is None:
        order = len(as_) + len(bs) - 2
    acc = None
    for i, a in enumerate(as_):
        for j, b in enumerate(bs):
            if i + j <= order:
                p = _dot(a, b, dims)
                acc = p if acc is None else acc + p
    return acc


def _sigmoid(x):
    return 1.0 / (1.0 + jnp.exp(-x))


def _silu(x):
    return x * _sigmoid(x)


def _log1p_exp_neg_abs(x):
    e = jnp.exp(-jnp.abs(x))
    u = 1.0 + e
    um1 = u - 1.0
    return jnp.where(um1 == 0.0, e, jnp.log(u) * (e / jnp.where(um1 == 0.0, 1.0, um1)))


def _softplus(x):
    return jnp.maximum(x, 0.0) + _log1p_exp_neg_abs(x)


def _log_sigmoid(x):
    return jnp.minimum(x, 0.0) - _log1p_exp_neg_abs(x)


def _norm_mod(x, g, sc, sh):
    ms = jnp.mean(x * x, axis=-1, keepdims=True)
    return (x * lax.rsqrt(ms + EPS)) * g * (1.0 + sc) + sh


def _tok_load(ref, chunk, n):
    return ref[pl.ds(chunk, n, stride=TOK), :]


def _tok_store(ref, chunk, val):
    ref[pl.ds(chunk, val.shape[0], stride=TOK), :] = val


def _tok_rows(ref, n):
    return jnp.concatenate([_tok_load(ref, c, n) for c in range(TOK)], axis=1)


def _tri(rev):
    r = lax.broadcasted_iota(I32, (T, T), 0)
    c = lax.broadcasted_iota(I32, (T, T), 1)
    return (c >= r) if rev else (c <= r)


def _cumsums(rev, col, row):
    tri = _tri(rev)
    tri_b = tri.astype(F32).astype(BF16)
    trit_b = _tri(not rev).astype(F32).astype(BF16)
    ccol = _mdot([tri_b], _split(col, 3))
    crow = _mdot(_split(row, 3), [trit_b])
    return tri, ccol, crow


def _lane_bcast(col):
    n = col.shape[1]
    col = jnp.concatenate([col, jnp.zeros((col.shape[0], 128 - n), F32)], axis=1)
    r = lax.broadcasted_iota(I32, (128, n * 128), 0)
    c = lax.broadcasted_iota(I32, (128, n * 128), 1) // 128
    return _mdot(_split(col, 3), [(r == c).astype(F32).astype(BF16)])


def _mod_kernel(c_ref, w_ref, b_ref, o_ref):
    a = _silu(c_ref[...])
    o_ref[0] = _mdot(_split(a, 2), _split(w_ref[0], 2), order=1) + b_ref[0]


def _modulation(c, c_ctx, w_mod, b_mod):
    depth = w_mod.shape[0]
    n = w_mod.shape[2]
    tn = 1536
    cc = jnp.zeros((8, D), F32).at[0].set(c[0]).at[1].set(c_ctx)
    out = pl.pallas_call(
        _mod_kernel,
        grid=(depth, n // tn),
        in_specs=[
            pl.BlockSpec((8, D), lambda l, j: (0, 0)),
            pl.BlockSpec((1, D, tn), lambda l, j: (l, 0, j)),
            pl.BlockSpec((1, 1, tn), lambda l, j: (l, 0, j)),
        ],
        out_specs=pl.BlockSpec((1, 8, tn), lambda l, j: (l, 0, j)),
        out_shape=jax.ShapeDtypeStruct((depth, 8, n), F32),
        compiler_params=_cparams("arbitrary", "arbitrary"),
        name="modulation",
    )(cc, w_mod, b_mod.reshape(depth, 1, n))
    return out[:, :2].reshape(depth, 2, 6, D)


def _halo_specs(nrows, tile_of):
    nb8 = nrows // 8
    return [
        pl.BlockSpec((8, D), lambda i: (jnp.maximum(tile_of(i) * (TM // 8) - 1, 0), 0)),
        pl.BlockSpec((TM, D), lambda i: (tile_of(i), 0)),
        pl.BlockSpec((8, D), lambda i: (jnp.minimum((tile_of(i) + 1) * (TM // 8), nb8 - 1), 0)),
    ]


def _ctx_tile(nct):
    return lambda i: jnp.minimum(i, nct - 1)


def _lat_tile(nct):
    return lambda i: jnp.maximum(i - nct, 0)


def _mod_spec(nct, t0=0):
    return pl.BlockSpec((1, 6, D), lambda i: (jnp.where(i + t0 < nct, 1, 0), 0, 0))


def _full(shape):
    nd = len(shape)
    return pl.BlockSpec(shape, lambda i: (0,) * nd)


def _seq_edges(i, nct, ntiles):
    prev_ok = jnp.logical_and(i != 0, i != nct).astype(F32)
    next_ok = jnp.logical_and(i != nct - 1, i != ntiles - 1).astype(F32)
    return prev_ok, next_ok


def _conv_silu(x, x_first_prev, x_last_next, cw, cb):
    n = x.shape[0]
    rows = lax.broadcasted_iota(I32, x.shape, 0)
    x_prev = jnp.where(rows == 0, x_first_prev, pltpu.roll(x, 1, 0))
    x_next = jnp.where(rows == n - 1, x_last_next, pltpu.roll(x, n - 1, 0))
    return _silu(x_prev * cw[0:1] + x * cw[1:2] + x_next * cw[2:3] + cb)


def _hi_lo_cols(blk, lo_pass, n):
    return blk + pltpu.roll(blk, 128 - n, 1) + lo_pass


def _hi_lo_weight(w):
    n = w.shape[1]
    hi = w.astype(BF16)
    lo = (w - hi.astype(F32)).astype(BF16)
    z = jnp.zeros((w.shape[0], 128 - 2 * n), BF16)
    return jnp.concatenate([hi, lo, z], axis=1), jnp.concatenate([hi, jnp.zeros_like(lo), z], axis=1)


def _head_rms(xf, gamma):
    r = lax.broadcasted_iota(I32, (128, 128), 0) // 64
    c = lax.broadcasted_iota(I32, (128, 128), 1) // 64
    ones_bd = (r == c).astype(F32).astype(BF16)
    outs = []
    for j in range(xf.shape[1] // 128):
        blk = xf[:, 128 * j:128 * (j + 1)]
        ssum = _dot((blk * blk).astype(BF16), ones_bd)
        outs.append(blk * lax.rsqrt(ssum * (1.0 / 64.0) + EPS))
    return jnp.concatenate(outs, axis=1) * gamma


def _rope(xf, cos, sin):
    lane = lax.broadcasted_iota(I32, (xf.shape[0], 128), 1)
    first = (lane % 32) < 16
    outs = []
    for j in range(xf.shape[1] // 128):
        blk = xf[:, 128 * j:128 * (j + 1)]
        partner = jnp.where(first, pltpu.roll(blk, 112, 1), pltpu.roll(blk, 16, 1))
        outs.append(blk * cos + partner * sin)
    return jnp.concatenate(outs, axis=1)


def _rope_tables(s_len):
    rows = s_len // GRID_W
    inv = ROPE_THETA ** (-(jnp.arange(32, dtype=I32) % 16).astype(F32) / 16.0)
    sign = jnp.where(jnp.arange(32) < 16, -1.0, 1.0).astype(F32)
    ang_r = jnp.arange(rows, dtype=F32)[:, None] * inv[None, :]
    ang_c = jnp.arange(GRID_W, dtype=F32)[:, None] * inv[None, :]

    def table(fr, fc):
        r = jnp.broadcast_to(fr[:, None, :], (rows, GRID_W, 32))
        c = jnp.broadcast_to(fc[None, :, :], (rows, GRID_W, 32))
        t = jnp.concatenate([r, c, r, c], axis=-1)
        return t.reshape(s_len, 128)

    return table(jnp.cos(ang_r), jnp.cos(ang_c)), table(jnp.sin(ang_r) * sign, jnp.sin(ang_c) * sign)


def _in0_kernel(nct, ntiles, cp_ref, c_ref, cn_ref, xp_ref, x_ref, xn_ref, mod_ref, g_ref, wcat_ref, wdth_ref,
                cw_ref, cb_ref, dtb_ref, qn_ref, kn_ref, cos_ref, sin_ref,
                z_ref, xs_ref, bc_ref, q_ref, kk_ref, vv_ref, dtc_ref, dtr_ref):
    i = pl.program_id(0)
    is_ctx = i < nct
    sh = mod_ref[0, 0:1, :]
    sc = mod_ref[0, 1:2, :]
    g = g_ref[...]
    x_all = jnp.concatenate([jnp.where(is_ctx, cp_ref[...], xp_ref[...]),
                             jnp.where(is_ctx, c_ref[...], x_ref[...]),
                             jnp.where(is_ctx, cn_ref[...], xn_ref[...])], axis=0)
    h_all = _norm_mod(x_all, g, sc, sh)
    hb_all = h_all.astype(BF16)
    main_all = _dot(hb_all, wcat_ref[...])
    h, hb, main = h_all[8:8 + TM], hb_all[8:8 + TM], main_all[8:8 + TM]
    prev_ok, next_ok = _seq_edges(i, nct, ntiles)
    xb_prev = main_all[7:8, 512:1280] * prev_ok
    xb_next = main_all[8 + TM:9 + TM, 512:1280] * next_ok
    act = _conv_silu(main[:, 512:1280], xb_prev, xb_next, cw_ref[...], cb_ref[...])
    z_ref[...] = main[:, 0:512].astype(BF16)
    xs_ref[...] = act[:, 0:512].astype(BF16)
    bc_ref[...] = act[:, 512:768].astype(BF16)
    cos = jnp.where(is_ctx, 1.0, cos_ref[...])
    sin = jnp.where(is_ctx, 0.0, sin_ref[...])
    q = _rope(_head_rms(main[:, 1280:1792], qn_ref[...]), cos, sin) * 0.125
    q_ref[...] = q.astype(BF16)
    k = _rope(_head_rms(main[:, 1792:1920], kn_ref[...]), cos, sin)
    kk_ref[...] = jnp.concatenate([k, pltpu.roll(k, 64, 1)], axis=1).astype(BF16)
    v = main[:, 1920:2048]
    vv_ref[...] = jnp.concatenate([v, pltpu.roll(v, 64, 1)], axis=1).astype(BF16)
    h_lo = (h - hb.astype(F32)).astype(BF16)
    dt = _softplus(_hi_lo_cols(main[:, 2048:2176], _dot(h_lo, wdth_ref[...]), 16) + dtb_ref[...])
    dtc_ref[...] = dt[:, 0:16]
    dtr_ref[...] = jnp.transpose(dt)[0:16, :]


def _in0(ctx2, x2, mod, p, nct):
    c_len, s_len = ctx2.shape[0], x2.shape[0]
    nt = c_len + s_len
    ntiles = nt // TM
    tile = lambda w: pl.BlockSpec((TM, w), lambda i: (i, 0))
    lat = _lat_tile(nct)
    lat_tile = lambda w: pl.BlockSpec((TM, w), lambda i: (lat(i), 0))
    outs = [(512, BF16), (512, BF16), (256, BF16), (512, BF16), (256, BF16), (256, BF16), (16, F32)]
    return pl.pallas_call(
        functools.partial(_in0_kernel, nct, ntiles),
        grid=(ntiles,),
        in_specs=_halo_specs(c_len, _ctx_tile(nct)) + _halo_specs(s_len, lat) + [
            _mod_spec(nct), _full((1, D)), _full((D, 2176)), _full((D, 128)),
            _full((3, 768)), _full((1, 768)), _full((1, 128)),
            _full((1, 512)), _full((1, 128)), lat_tile(128), lat_tile(128)],
        out_specs=[tile(w) for w, _ in outs] + [pl.BlockSpec((16, TM), lambda i: (0, i))],
        out_shape=[jax.ShapeDtypeStruct((nt, w), dt) for w, dt in outs]
        + [jax.ShapeDtypeStruct((16, nt), F32)],
        compiler_params=_cparams("arbitrary"),
        name="in_proj_even",
    )(ctx2, ctx2, ctx2, x2, x2, x2, mod, p["g_mix"], p["wcat"], p["wdt_hi"], p["conv_w"], p["conv_b"],
      p["dt_bias"], p["q_norm"], p["k_norm"], p["cos"], p["sin"])


def _scan_chunk_map(rev, ncc, nlc):
    if not rev:
        return lambda j: j
    return lambda j: jnp.where(j < ncc, ncc - 1 - j, ncc + nlc - 1 - (j - ncc))


def _ssd_kernel(rev, *refs):
    if rev:
        (xs_ref, bc_ref, dtc_ref, dtr_ref, alr_ref, alc_ref, yf_ref, z_ref, dsk_ref, nrm_ref,
         o_ref, st_ref) = refs
    else:
        xs_ref, bc_ref, dtc_ref, dtr_ref, alr_ref, alc_ref, o_ref, st_ref = refs
    j = pl.program_id(0)

    @pl.when(j == 0)
    def _():
        st_ref[...] = jnp.zeros_like(st_ref)

    d = 8 if rev else 0
    a_coef_row = -jnp.exp(alr_ref[...])[:, d:d + 8]
    a_coef_col = -jnp.exp(alc_ref[...])[d:d + 8, :]
    dtc = dtc_ref[:, d:d + 8]
    dtr = dtr_ref[d:d + 8, :]
    tri, acs_col, acs_row = _cumsums(rev, dtc * a_coef_row, dtr * a_coef_col)
    end = 0 if rev else T - 1
    atot_col = acs_row[:, end:end + 1]
    acs_bc = _lane_bcast(acs_col)
    dec_row = jnp.exp(atot_col - acs_row) * dtr
    xs = xs_ref[...]
    bm = bc_ref[:, 0:128]
    cm = bc_ref[:, 128:256]
    lane = lax.broadcasted_iota(I32, (T, 128), 1)
    lo = lane < 64
    zero_b = jnp.zeros((T, 128), BF16)
    hi_half = jnp.logical_not(lo)
    cgs = [jnp.where(lo, cm, zero_b), jnp.where(hi_half, cm, zero_b)]
    cbs = [_dot(cgs[g], bm, _NT) for g in range(2)]
    sub = lax.broadcasted_iota(I32, (128, T), 0)
    eye = (sub == lax.broadcasted_iota(I32, (128, T), 1)).astype(F32).astype(BF16)
    bmt = _dot(eye, bm, _NT)
    bgts = [jnp.where(sub < 64, bmt, 0.0), jnp.where(sub >= 64, bmt, 0.0)]
    xhs = []
    for hd in range(8):
        xp = xs[:, 128 * (hd // 2):128 * (hd // 2 + 1)]
        xhs.append(jnp.where(hi_half if hd % 2 else lo, xp, zero_b))
    y_off = []
    for hd in range(8):
        g = hd // 4
        st = st_ref[hd]
        y_off.append(_dot(cgs[g], st.astype(BF16)) * jnp.exp(acs_bc[:, 128 * hd:128 * (hd + 1)]))
        bdec = (bgts[g] * dec_row[hd:hd + 1, :]).astype(BF16)
        st_ref[hd] = jnp.exp(atot_col[hd:hd + 1, :]) * st + _dot(bdec, xhs[hd])
    ys = [None] * 4
    for hd in range(8):
        diff = acs_bc[:, 128 * hd:128 * (hd + 1)] - acs_row[hd:hd + 1, :]
        lmat = jnp.exp(jnp.where(tri, diff, NEG_INF))
        m = (cbs[hd // 4] * lmat * dtr[hd:hd + 1, :]).astype(BF16)
        y = _dot(m, xhs[hd]) + y_off[hd]
        ys[hd // 2] = y if hd % 2 == 0 else ys[hd // 2] + y
    y = jnp.concatenate(ys, axis=1)
    if not rev:
        o_ref[...] = y
    else:
        ytot = y + yf_ref[...] + dsk_ref[...] * xs.astype(F32)
        gated = ytot * _silu(z_ref[...].astype(F32))
        ms = jnp.mean(gated * gated, axis=-1, keepdims=True)
        o_ref[...] = (gated * lax.rsqrt(ms + EPS) * nrm_ref[...]).astype(BF16)


def _ssd(rev, xs, bc, dtc, dtr, p, ncc, nlc, yf=None, z=None):
    nt = xs.shape[0]
    cmap = _scan_chunk_map(rev, ncc, nlc)
    blk = lambda w: pl.BlockSpec((T, w), lambda j: (cmap(j), 0))
    in_specs = [blk(512), blk(256), blk(16), pl.BlockSpec((16, T), lambda j: (0, cmap(j))),
                _full((1, 16)), _full((16, 1))]
    args = [xs, bc, dtc, dtr, p["alog_row"], p["alog_col"]]
    if rev:
        in_specs += [blk(512), blk(512), _full((1, 512)), _full((1, 512))]
        args += [yf, z, p["d_skip"], p["ssd_norm"]]
    return pl.pallas_call(
        functools.partial(_ssd_kernel, rev),
        grid=(ncc + nlc,),
        in_specs=in_specs,
        out_specs=blk(512),
        out_shape=jax.ShapeDtypeStruct((nt, 512), BF16 if rev else F32),
        scratch_shapes=[pltpu.VMEM((8, 128, 128), F32)],
        compiler_params=_cparams("arbitrary"),
        name="ssd_bwd" if rev else "ssd_fwd",
    )(*args)


def _attn_kernel(ncc, nblk, q_ref, kp_ref, kc_ref, kn_ref, vp_ref, vc_ref, vn_ref, kx_ref, vx_ref,
                 sink_ref, o_ref):
    j = pl.program_id(0)
    c_len = kx_ref.shape[0]
    is_lat = j >= ncc
    prev_ok = jnp.logical_and(is_lat, j >= ncc + 1)
    next_ok = jnp.logical_and(is_lat, j <= nblk - 2)
    r = lax.broadcasted_iota(I32, (T, T), 0)
    c = lax.broadcasted_iota(I32, (T, T), 1)
    zero = jnp.zeros((T, T), F32)
    ninf = jnp.full((T, T), NEG_INF, F32)
    bias = jnp.concatenate([
        jnp.where(jnp.logical_and(prev_ok, c >= r), zero, ninf),
        jnp.where(is_lat, zero, ninf),
        jnp.where(jnp.logical_and(next_ok, c <= r), zero, ninf),
        jnp.zeros((T, c_len), F32)], axis=1)
    k_all = jnp.concatenate([kp_ref[...], kc_ref[...], kn_ref[...], kx_ref[...]], axis=0)
    v_all = jnp.concatenate([vp_ref[...], vc_ref[...], vn_ref[...], vx_ref[...]], axis=0)
    nk = k_all.shape[0]
    lane = lax.broadcasted_iota(I32, (nk, 128), 1)
    lo = lane < 64
    zero_b = jnp.zeros((nk, 128), BF16)
    q = q_ref[...]
    sink = sink_ref[...]
    kmask, vmask = [], []
    for g in range(2):
        kmask += [jnp.where(lo, k_all[:, 128 * g:128 * (g + 1)], zero_b),
                  jnp.where(lo, zero_b, k_all[:, 128 * (1 - g):128 * (2 - g)])]
        vmask += [jnp.where(lo, v_all[:, 128 * g:128 * (g + 1)], zero_b),
                  jnp.where(lo, zero_b, v_all[:, 128 * (1 - g):128 * (2 - g)])]

    def scores(hd):
        pair = hd // 2
        return _dot(q[:, 128 * pair:128 * (pair + 1)], kmask[2 * (hd // 4) + hd % 2], _NT) + bias

    s_next = scores(0)
    acc = None
    for hd in range(8):
        s = s_next
        if hd + 1 < 8:
            s_next = scores(hd + 1)
        sk = sink[:, hd:hd + 1]
        m = jnp.maximum(jnp.max(s, axis=-1, keepdims=True), sk)
        pr = jnp.exp(s - m)
        den = jnp.sum(pr, axis=-1, keepdims=True) + jnp.exp(sk - m)
        o = _dot(pr.astype(BF16), vmask[2 * (hd // 4) + hd % 2]) / den
        if hd % 2 == 0:
            acc = o
        else:
            pair = hd // 2
            o_ref[:, 128 * pair:128 * (pair + 1)] = (acc + o).astype(BF16)


def _attention(q, kk, vv, sink, ncc, nlc):
    nt = q.shape[0]
    nblk = ncc + nlc
    c_len = ncc * T
    prev = lambda w: pl.BlockSpec((T, w), lambda j: (jnp.maximum(j - 1, 0), 0))
    cur = lambda w: pl.BlockSpec((T, w), lambda j: (j, 0))
    nxt = lambda w: pl.BlockSpec((T, w), lambda j: (jnp.minimum(j + 1, nblk - 1), 0))
    ctx = lambda w: pl.BlockSpec((c_len, w), lambda j: (0, 0))
    return pl.pallas_call(
        functools.partial(_attn_kernel, ncc, nblk),
        grid=(nblk,),
        in_specs=[cur(512), prev(256), cur(256), nxt(256), prev(256), cur(256), nxt(256),
                  ctx(256), ctx(256), _full((1, 8))],
        out_specs=cur(512),
        out_shape=jax.ShapeDtypeStruct((nt, 512), BF16),
        compiler_params=_cparams("arbitrary"),
        name="window_attention",
    )(q, kk, kk, kk, vv, vv, vv, kk, vv, sink)


def _in1_kernel(nct, ntiles, xp_ref, x_ref, xn_ref, mod_ref, g_ref,
                wcat_ref, wgh_ref, cw_ref, cb_ref, gb_ref,
                q_ref, kt_ref, v_ref, o_ref, gc_ref, gr_ref):
    i = pl.program_id(0)
    sh = mod_ref[0, 0:1, :]
    sc = mod_ref[0, 1:2, :]
    g = g_ref[...]
    x_all = jnp.concatenate([xp_ref[...], x_ref[...], xn_ref[...]], axis=0)
    h_all = _norm_mod(x_all, g, sc, sh)
    hb_all = h_all.astype(BF16)
    main_all = _dot(hb_all, wcat_ref[...])
    h, hb, main = h_all[8:8 + TM], hb_all[8:8 + TM], main_all[8:8 + TM]
    prev_ok, next_ok = _seq_edges(i, nct, ntiles)
    x_prev = main_all[7:8, 0:2048] * prev_ok
    x_next = main_all[8 + TM:9 + TM, 0:2048] * next_ok
    act = _conv_silu(main[:, 0:2048], x_prev, x_next, cw_ref[...], cb_ref[...])
    q_ref[...] = act[:, 0:512].astype(BF16)
    kt_ref[...] = jnp.transpose(act[:, 512:1024] * 0.125).astype(BF16)
    v_ref[...] = act[:, 1024:2048].astype(BF16)
    o_ref[...] = main[:, 2048:3072].astype(BF16)
    h_lo = (h - hb.astype(F32)).astype(BF16)
    gates = _hi_lo_cols(main[:, 3072:3200], _dot(h_lo, wgh_ref[...]), 32) + gb_ref[...]
    lane = lax.broadcasted_iota(I32, gates.shape, 1)
    gates = jnp.where(lane < 16, gates, _log_sigmoid(gates))
    gc_ref[...] = gates[:, 0:32]
    gr_ref[...] = jnp.transpose(gates)[0:32, :]


def _in1(x1, mod, p, nct):
    nt = x1.shape[0]
    ntiles = nt // TM
    tile = lambda w: pl.BlockSpec((TM, w), lambda i: (i, 0))
    return pl.pallas_call(
        functools.partial(_in1_kernel, nct, ntiles),
        grid=(ntiles,),
        in_specs=_halo_specs(nt, lambda i: i) + [
            _mod_spec(nct), _full((1, D)), _full((D, 3200)), _full((D, 128)),
            _full((3, 2048)), _full((1, 2048)), _full((1, 128))],
        out_specs=[tile(512), pl.BlockSpec((512, TM), lambda i: (0, i)), tile(1024), tile(1024),
                   tile(32), pl.BlockSpec((32, TM), lambda i: (0, i))],
        out_shape=[jax.ShapeDtypeStruct((nt, 512), BF16), jax.ShapeDtypeStruct((512, nt), BF16),
                   jax.ShapeDtypeStruct((nt, 1024), BF16), jax.ShapeDtypeStruct((nt, 1024), BF16),
                   jax.ShapeDtypeStruct((nt, 32), F32), jax.ShapeDtypeStruct((32, nt), F32)],
        compiler_params=_cparams("arbitrary"),
        name="in_proj_odd",
    )(x1, x1, x1, mod, p["g_mix"], p["wcat"], p["wg_hi"], p["conv_w"], p["conv_b"], p["gate_bias"])


def _mlstm_kernel(rev, *refs):
    if rev:
        (q_ref, kt_ref, v_ref, gc_ref, gr_ref, hf_ref, og_ref, hn_ref, o_ref,
         c_ref, mc_ref, mr_ref) = refs
    else:
        q_ref, kt_ref, v_ref, gc_ref, gr_ref, o_ref, c_ref, mc_ref, mr_ref = refs
    j = pl.program_id(0)

    @pl.when(j == 0)
    def _():
        c_ref[...] = jnp.zeros_like(c_ref)
        mc_ref[...] = jnp.zeros_like(mc_ref)
        mr_ref[...] = jnp.zeros_like(mr_ref)

    d = 8 if rev else 0
    ig_col = gc_ref[:, d:d + 8]
    lf_col = gc_ref[:, 16 + d:24 + d]
    ig_row = gr_ref[d:d + 8, :]
    lf_row = gr_ref[16 + d:24 + d, :]
    tri, b_col, b_row = _cumsums(rev, lf_col, lf_row)
    end = 0 if rev else T - 1
    blast_row = b_col[end:end + 1, :]
    blast_col = b_row[:, end:end + 1]
    wend_row = blast_col - b_row + ig_row
    ac_col = jnp.max(wend_row, axis=1, keepdims=True)
    eend_row = jnp.exp(wend_row - ac_col)
    ac_row = jnp.max(blast_row - b_col + ig_col, axis=0, keepdims=True)
    m_col = mc_ref[:, 0:1]
    m_row = mr_ref[0:1, 0:8]
    mnew_col = jnp.maximum(blast_col + m_col, ac_col)
    sp_col = jnp.exp(blast_col + m_col - mnew_col)
    sc_col = jnp.exp(ac_col - mnew_col)
    mnew_row = jnp.maximum(blast_row + m_row, ac_row)
    b_bc = _lane_bcast(b_col)
    q = q_ref[...]
    ones_b = jnp.ones((T, 128), BF16)
    sub = lax.broadcasted_iota(I32, (128, T), 0)
    zero_k = jnp.zeros((128, T), BF16)
    def head_matmuls(hd):
        pair, hi = hd // 2, hd % 2
        qp = q[:, 128 * pair:128 * (pair + 1)]
        ktp = kt_ref[128 * pair:128 * (pair + 1), :]
        kth = jnp.where((sub >= 64) if hi else (sub < 64), ktp, zero_k)
        vaug = jnp.concatenate([v_ref[:, 128 * hd:128 * (hd + 1)], ones_b], axis=1)
        cst = c_ref[hd]
        sqk = _dot(qp, kth)
        inter = _dot(qp, cst.astype(BF16))
        kte = (kth.astype(F32) * eend_row[hd:hd + 1, :]).astype(BF16)
        c_ref[hd] = sp_col[hd:hd + 1, :] * cst + sc_col[hd:hd + 1, :] * _dot(kte, vaug)
        return sqk, inter, vaug

    nxt = head_matmuls(0)
    for hd in range(8):
        sqk, inter, vaug = nxt
        if hd + 1 < 8:
            nxt = head_matmuls(hd + 1)
        bh = b_bc[:, 128 * hd:128 * (hd + 1)]
        dlog = jnp.where(tri, bh - b_row[hd:hd + 1, :] + ig_row[hd:hd + 1, :], NEG_INF)
        gh = bh + mr_ref[0:1, hd:hd + 1]
        mstar = jnp.maximum(gh, jnp.max(dlog, axis=-1, keepdims=True))
        w = (jnp.exp(dlog - mstar) * sqk).astype(BF16)
        intra = _dot(w, vaug)
        e_int = jnp.exp(gh - mstar)
        den = jnp.maximum(jnp.abs(intra[:, 128:256] + e_int * inter[:, 128:256]), jnp.exp(-mstar))
        hh = (intra[:, 0:128] + e_int * inter[:, 0:128]) / den
        if rev:
            hh = hh + hf_ref[:, 128 * hd:128 * (hd + 1)]
            ms = jnp.mean(hh * hh, axis=-1, keepdims=True)
            hh = hh * lax.rsqrt(ms + EPS) * hn_ref[:, 128 * hd:128 * (hd + 1)]
            og = og_ref[:, 128 * hd:128 * (hd + 1)].astype(F32)
            o_ref[:, 128 * hd:128 * (hd + 1)] = (hh * _sigmoid(og)).astype(BF16)
        else:
            o_ref[:, 128 * hd:128 * (hd + 1)] = hh
    mc_ref[...] = jnp.broadcast_to(mnew_col, mc_ref.shape)
    mr_ref[...] = jnp.broadcast_to(jnp.concatenate([mnew_row, jnp.zeros((1, 120), F32)], axis=1), mr_ref.shape)


def _mlstm(rev, q, kt, v, gc, gr, p, ncc, nlc, hf=None, og=None):
    nt = q.shape[0]
    cmap = _scan_chunk_map(rev, ncc, nlc)
    blk = lambda w: pl.BlockSpec((T, w), lambda j: (cmap(j), 0))
    blk_t = lambda h: pl.BlockSpec((h, T), lambda j: (0, cmap(j)))
    in_specs = [blk(512), blk_t(512), blk(1024), blk(32), blk_t(32)]
    args = [q, kt, v, gc, gr]
    if rev:
        in_specs += [blk(1024), blk(1024), _full((1, 1024))]
        args += [hf, og, p["head_norm"]]
    return pl.pallas_call(
        functools.partial(_mlstm_kernel, rev),
        grid=(ncc + nlc,),
        in_specs=in_specs,
        out_specs=blk(1024),
        out_shape=jax.ShapeDtypeStruct((nt, 1024), BF16 if rev else F32),
        scratch_shapes=[pltpu.VMEM((8, 128, 256), F32), pltpu.VMEM((8, 128), F32), pltpu.VMEM((8, 128), F32)],
        compiler_params=_cparams("arbitrary"),
        name="mlstm_bwd" if rev else "mlstm_fwd",
    )(*args)


def _route(logits_t, rb_col):
    scores = _sigmoid(logits_t)
    biased = scores + rb_col
    row = lambda a, e: a[e:e + 1, :]
    gscore = []
    for g in range(4):
        b0, b1, b2, b3 = (row(biased, 4 * g + e) for e in range(4))
        h1, l1 = jnp.maximum(b0, b1), jnp.minimum(b0, b1)
        h2, l2 = jnp.maximum(b2, b3), jnp.minimum(b2, b3)
        gscore.append(jnp.maximum(h1, h2) + jnp.maximum(jnp.minimum(h1, h2), jnp.maximum(l1, l2)))
    gidx = jnp.zeros_like(gscore[0], dtype=I32)
    best = gscore[0]
    for g in range(1, 4):
        better = gscore[g] > best
        gidx = jnp.where(better, g, gidx)
        best = jnp.where(better, gscore[g], best)

    def pick(a, e):
        out = row(a, e)
        for g in range(1, 4):
            out = jnp.where(gidx == g, row(a, 4 * g + e), out)
        return out

    sb = [pick(biased, e) for e in range(4)]
    ss = [pick(scores, e) for e in range(4)]
    i1 = jnp.zeros_like(gidx)
    v1, s1 = sb[0], ss[0]
    for e in range(1, 4):
        better = sb[e] > v1
        i1 = jnp.where(better, e, i1)
        v1 = jnp.where(better, sb[e], v1)
        s1 = jnp.where(better, ss[e], s1)
    i2 = jnp.zeros_like(gidx)
    v2 = jnp.full_like(v1, NEG_INF)
    s2 = jnp.zeros_like(s1)
    for e in range(4):
        better = jnp.logical_and(i1 != e, sb[e] > v2)
        i2 = jnp.where(better, e, i2)
        v2 = jnp.where(better, sb[e], v2)
        s2 = jnp.where(better, ss[e], s2)
    tot = s1 + s2
    w1 = s1 / tot
    w2 = s2 / tot
    first_low = i1 < i2
    a = jnp.minimum(i1, i2)
    b = jnp.maximum(i1, i2)
    off = jnp.where(a == 0, 0, jnp.where(a == 1, 3, 5))
    bucket = 6 * gidx + off + (b - a - 1)
    return bucket, jnp.where(first_low, w1, w2), jnp.where(first_low, w2, w1)


def _out_kernel(nmix, nct, two_src, *refs):
    mix_refs = refs[:nmix]
    refs = refs[nmix:]
    if two_src:
        c_ref, x_ref = refs[:2]
        refs = refs[2:]
    else:
        x_ref = refs[0]
        refs = refs[1:]
    (w_ref, mod_ref, g_ref, rw_ref, rwh_ref, rb_ref,
     xmid_ref, hrow_ref, bucket_ref, rank_ref, cnt_ref, cnt_scr) = refs
    i = pl.program_id(0)

    @pl.when(i == 0)
    def _():
        cnt_scr[...] = jnp.zeros_like(cnt_scr)

    mix = mix_refs[0][...] if nmix == 1 else jnp.concatenate([mr[...] for mr in mix_refs], axis=1)
    x = jnp.where(i < nct, c_ref[...], x_ref[...]) if two_src else x_ref[...]
    x_mid = x + mod_ref[0, 2:3, :] * _dot(mix, w_ref[...])
    xmid_ref[...] = x_mid
    h = _norm_mod(x_mid, g_ref[...], mod_ref[0, 4:5, :], mod_ref[0, 3:4, :])
    hb = h.astype(BF16)
    h_lo = (h - hb.astype(F32)).astype(BF16)
    logits = _hi_lo_cols(_dot(hb, rw_ref[...]), _dot(h_lo, rwh_ref[...]), 16)
    logits_t = jnp.transpose(logits)[0:16, :]
    bucket, w_lo, w_hi = _route(logits_t, rb_ref[...])
    bits = pltpu.bitcast(hb.astype(F32), U32)
    words = (bits[:, 512:1024] & jnp.uint32(0xFFFF0000)) | (bits[:, 0:512] >> 16)
    for cblk in range(4):
        _tok_store(hrow_ref, cblk, words[:, 128 * cblk:128 * (cblk + 1)])
    wrows = jnp.concatenate([w_lo, w_hi, jnp.zeros((126, TM), F32)], axis=0)
    _tok_store(hrow_ref, 4, pltpu.bitcast(jnp.transpose(wrows), U32))
    for cblk in range(5, TOK):
        _tok_store(hrow_ref, cblk, jnp.zeros((TM, 128), U32))
    brow = lax.broadcasted_iota(I32, (NB_PAD, TM), 0)
    onehot = (brow == bucket).astype(F32)
    r = lax.broadcasted_iota(I32, (TM, TM), 0)
    c = lax.broadcasted_iota(I32, (TM, TM), 1)
    before = (r < c).astype(F32).astype(BF16)
    cum = _dot(onehot.astype(BF16), before)
    base = cnt_scr[:, 0:1]
    rank = jnp.sum(onehot * (cum + base), axis=0, keepdims=True)
    bucket_ref[0] = bucket
    rank_ref[0] = rank.astype(I32)
    cnt = cnt_scr[...] + jnp.sum(onehot, axis=1, keepdims=True)
    cnt_scr[...] = cnt
    cnt_ref[...] = cnt.astype(I32)


def _out_proj(mixes, w_out, xs, mod, g_ffn, rw, rw_hi, rb_col, t0, ntiles, nct):
    nmix = len(mixes)
    two_src = len(xs) == 2
    n = ntiles * TM
    tile_in = lambda w: pl.BlockSpec((TM, w), lambda i: (i + t0, 0))
    tile_out = lambda w: pl.BlockSpec((TM, w), lambda i: (i, 0))
    row_out = pl.BlockSpec((1, 1, TM), lambda i: (i, 0, 0))
    if two_src:
        ctx_of, lat_of = _ctx_tile(nct), _lat_tile(nct)
        x_specs = [pl.BlockSpec((TM, D), lambda i: (ctx_of(i), 0)), pl.BlockSpec((TM, D), lambda i: (lat_of(i), 0))]
    else:
        x_specs = [tile_in(D)]
    in_specs = [tile_in(mx.shape[1]) for mx in mixes] + x_specs + [
        _full((D, D)), _mod_spec(nct, t0), _full((1, D)), _full((D, 128)), _full((D, 128)), _full((16, 1))]
    return pl.pallas_call(
        functools.partial(_out_kernel, nmix, nct, two_src),
        grid=(ntiles,),
        in_specs=in_specs,
        out_specs=[tile_out(D), pl.BlockSpec((TM * TOK, 128), lambda i: (i, 0)), row_out, row_out,
                   _full((NB_PAD, 128))],
        out_shape=[jax.ShapeDtypeStruct((n, D), F32), jax.ShapeDtypeStruct((n * TOK, 128), U32),
                   jax.ShapeDtypeStruct((ntiles, 1, TM), I32), jax.ShapeDtypeStruct((ntiles, 1, TM), I32),
                   jax.ShapeDtypeStruct((NB_PAD, 128), I32)],
        scratch_shapes=[pltpu.VMEM((NB_PAD, 128), F32)],
        compiler_params=_cparams("arbitrary"),
        name="out_proj_router",
    )(*mixes, *xs, w_out, mod, g_ffn, rw, rw_hi, rb_col)


def _scatter_kernel(ntiles, pos_ref, flo_ref, fhi_ref, h_ref, dst_ref, sem):
    i = pl.program_id(0)

    def tok(ref, t):
        return ref.at[pl.ds(pl.multiple_of(t * TOK, TOK), TOK), :]

    def copy(r, d_row):
        return pltpu.make_async_copy(tok(h_ref, r), tok(dst_ref, d_row), sem)

    def wait_rows(lo, hi, unroll):
        def body(r, carry):
            copy(0, 0).wait()
            return carry
        lax.fori_loop(lo, hi, body, 0, unroll=unroll)

    def start(r, c):
        copy(r, pos_ref[i * TM + r]).start()
        return c
    lax.fori_loop(0, TM, start, 0, unroll=8)
    wait_rows(0, TM, 8)

    @pl.when(i == ntiles - 1)
    def _():
        for b in range(N_BUCKETS):
            lo, hi = flo_ref[b], fhi_ref[b]

            def fill(r, c):
                copy(0, r).start()
                return c
            lax.fori_loop(lo, hi, fill, 0)
            wait_rows(lo, hi, 1)

        def tile_copy(j):
            rows = TMM * TOK
            return pltpu.make_async_copy(h_ref, dst_ref.at[pl.ds(pl.multiple_of(j * rows, rows), rows), :], sem)

        def fill_tile(j, c):
            tile_copy(j).start()
            return c

        def wait_tile(j, c):
            tile_copy(j).wait()
            return c
        lax.fori_loop(flo_ref[N_BUCKETS], fhi_ref[N_BUCKETS], fill_tile, 0)
        lax.fori_loop(flo_ref[N_BUCKETS], fhi_ref[N_BUCKETS], wait_tile, 0)


def _scatter_rows(hrow, pos, fill_lo, fill_hi, n, npad):
    assert TM == TMM
    return pl.pallas_call(
        functools.partial(_scatter_kernel, n // TM),
        grid_spec=pltpu.PrefetchScalarGridSpec(
            num_scalar_prefetch=3,
            grid=(n // TM,),
            in_specs=[pl.BlockSpec((TM * TOK, 128), lambda i, *_: (i, 0))],
            out_specs=pl.BlockSpec(memory_space=pl.ANY),
            scratch_shapes=[pltpu.SemaphoreType.DMA(())],
        ),
        out_shape=jax.ShapeDtypeStruct((npad * TOK, 128), hrow.dtype),
        compiler_params=_cparams("arbitrary"),
        name="moe_scatter_rows",
    )(pos, fill_lo, fill_hi, hrow)


def _combine_kernel(ntiles, pos_ref, x_ref, mod_ref, ys_ref, o_ref, ybuf, sem):
    i = pl.program_id(0)

    def copy(tile, slot, r):
        src = pl.multiple_of(pos_ref[tile * TM + r] * TOK, TOK)
        dst = pl.multiple_of(r * TOK, TOK)
        return pltpu.make_async_copy(ys_ref.at[pl.ds(src, TOK), :], ybuf.at[slot, pl.ds(dst, TOK), :], sem.at[slot])

    def start_tile(tile, slot):
        def body(r, carry):
            copy(tile, slot, r).start()
            return carry
        lax.fori_loop(0, TM, body, 0, unroll=8)

    @pl.when(i == 0)
    def _():
        start_tile(0, 0)

    @pl.when(i + 1 < ntiles)
    def _():
        start_tile(i + 1, (i + 1) % 2)

    slot = i % 2

    def wait_body(r, carry):
        copy(i, slot, 0).wait()
        return carry
    lax.fori_loop(0, TM, wait_body, 0, unroll=8)
    o_ref[...] = x_ref[...] + mod_ref[0, 5:6, :] * _tok_rows(ybuf.at[slot], TM)


def _combine(x_mid, mod, ys, pos, t0, nct):
    n = x_mid.shape[0]
    ntiles = n // TM
    return pl.pallas_call(
        functools.partial(_combine_kernel, ntiles),
        grid_spec=pltpu.PrefetchScalarGridSpec(
            num_scalar_prefetch=1,
            grid=(ntiles,),
            in_specs=[pl.BlockSpec((TM, D), lambda i, *_: (i, 0)),
                      pl.BlockSpec((1, 6, D), lambda i, *_: (jnp.where(i + t0 < nct, 1, 0), 0, 0)),
                      pl.BlockSpec(memory_space=pl.ANY)],
            out_specs=pl.BlockSpec((TM, D), lambda i, *_: (i, 0)),
            scratch_shapes=[pltpu.VMEM((2, TM * TOK, 128), F32), pltpu.SemaphoreType.DMA((2,))],
        ),
        out_shape=jax.ShapeDtypeStruct((n, D), F32),
        compiler_params=_cparams("arbitrary"),
        name="moe_combine",
    )(pos, x_mid, mod, ys)


def _moe_kernel(tea_ref, teb_ref, tblk_ref, tval_ref, x_ref, w1a_ref, w3a_ref, w2a_ref,
                w1b_ref, w3b_ref, w2b_ref, y_ref):
    del tea_ref, teb_ref, tblk_ref
    j = pl.program_id(0)

    @pl.when(tval_ref[j] != 0)
    def _():
        words = [_tok_load(x_ref, cblk, TMM) for cblk in range(4)]
        low = [pltpu.bitcast(wd << 16, F32).astype(BF16) for wd in words]
        high = [pltpu.bitcast(wd & jnp.uint32(0xFFFF0000), F32).astype(BF16) for wd in words]
        hb = jnp.concatenate(low + high, axis=1)
        gates = pltpu.bitcast(_tok_load(x_ref, 4, TMM), F32)
        acts = []
        for w1_ref, w3_ref, lane in ((w1a_ref, w3a_ref, 0), (w1b_ref, w3b_ref, 1)):
            u = _dot(hb, w1_ref[0, 0].astype(BF16))
            v = _dot(hb, w3_ref[0, 0].astype(BF16))
            acts.append((_silu(u) * v * gates[:, lane:lane + 1]).astype(BF16))
        y = _dot(acts[0], w2a_ref[0, 0].astype(BF16)) + _dot(acts[1], w2b_ref[0, 0].astype(BF16))
        for cblk in range(TOK):
            _tok_store(y_ref, cblk, y[:, 128 * cblk:128 * (cblk + 1)])

    @pl.when(tval_ref[j] == 0)
    def _():
        y_ref[...] = jnp.zeros_like(y_ref)


def _moe(xs_sorted, w1, w3, w2, layer, tile_ea, tile_eb, tile_blk, tile_valid):
    npad = xs_sorted.shape[0] // TOK
    ntile = npad // TMM
    wspec = lambda shape, which: pl.BlockSpec(
        (1, 1) + shape, lambda j, ea, eb, blk, val: (layer, (ea, eb)[which][j], 0, 0))
    up, down = (D, D_EXPERT), (D_EXPERT, D)
    return pl.pallas_call(
        _moe_kernel,
        grid_spec=pltpu.PrefetchScalarGridSpec(
            num_scalar_prefetch=4,
            grid=(ntile,),
            in_specs=[pl.BlockSpec((TMM * TOK, 128), lambda j, ea, eb, blk, val: (blk[j], 0)),
                      wspec(up, 0), wspec(up, 0), wspec(down, 0), wspec(up, 1), wspec(up, 1), wspec(down, 1)],
            out_specs=pl.BlockSpec((TMM * TOK, 128), lambda j, ea, eb, blk, val: (j, 0)),
        ),
        out_shape=jax.ShapeDtypeStruct((npad * TOK, 128), F32),
        compiler_params=_cparams("arbitrary"),
        name="moe_experts",
    )(tile_ea, tile_eb, tile_blk, tile_valid, xs_sorted, w1, w3, w2, w1, w3, w2)


def _moe_block(x_mid, hrow, bucket, rank, counts, mod, w1, w3, w2, layer, t0, nct):
    n = hrow.shape[0] // TOK
    ntile = n // TMM + N_BUCKETS
    npad = ntile * TMM
    cnt = counts[:N_BUCKETS, 0]
    padded = ((cnt + TMM - 1) // TMM) * TMM
    ends = jnp.cumsum(padded)
    starts = ends - padded
    total_tiles = ends[-1] // TMM
    tiles = jnp.arange(ntile, dtype=I32)
    tile_valid = (tiles < total_tiles).astype(I32)
    tile_blk = jnp.minimum(tiles, jnp.maximum(total_tiles - 1, 0))
    tile_bucket = jnp.minimum(jnp.sum((ends[None, :] <= (tile_blk * TMM)[:, None]).astype(I32), axis=1), N_BUCKETS - 1)
    pair = tile_bucket % 6
    grp = tile_bucket // 6
    pair_a = jnp.where(pair < 3, 0, jnp.where(pair < 5, 1, 2))
    pair_b = jnp.where(pair < 3, pair + 1, jnp.where(pair < 5, pair - 1, 3))
    tile_ea = (4 * grp + pair_a).astype(I32)
    tile_eb = (4 * grp + pair_b).astype(I32)
    bucket = bucket.reshape(-1)
    onehot = (bucket[:, None] == jnp.arange(N_BUCKETS, dtype=I32)[None, :]).astype(I32)
    pos = (rank.reshape(-1) + jnp.sum(onehot * starts[None, :], axis=1)).astype(I32)
    pad32 = lambda a, tail: jnp.zeros((NB_PAD,), I32).at[:N_BUCKETS].set(a.astype(I32)).at[N_BUCKETS].set(tail)
    fill_lo, fill_hi = pad32(starts + cnt, total_tiles), pad32(ends, ntile)
    xs_sorted = _scatter_rows(hrow, pos, fill_lo, fill_hi, n, npad)
    ys = _moe(xs_sorted, w1, w3, w2, layer, tile_ea, tile_eb, tile_blk, tile_valid)
    return _combine(x_mid, mod, ys, pos, t0, nct)


def kernel(x, c, ctx, c_ctx, router_w, router_b, norm_mix, norm_ffn, w_mod, b_mod, ev_w_in, ev_conv_w, ev_conv_b, ev_dt_bias, ev_a_log, ev_d_skip, ev_ssd_norm, ev_q_norm, ev_k_norm, ev_sink, ev_w_out, od_w_in, od_conv_w, od_conv_b, od_igate_b, od_fgate_b, od_head_norm, od_w_out, moe_w1, moe_w3, moe_w2):
    s_len = x.shape[1]
    c_len = ctx.shape[1]
    assert x.shape[0] == 1 and s_len % TM == 0 and c_len % TM == 0 and s_len % GRID_W == 0
    nct = c_len // TM
    ncc, nlc = c_len // T, s_len // T
    nt = c_len + s_len
    ntiles = nt // TM

    mod = _modulation(c, c_ctx, w_mod, b_mod)
    rw, rw_hi = _hi_lo_weight(router_w)
    rb_col = router_b.reshape(N_EXPERTS, 1)
    pad128 = lambda v: jnp.zeros((1, 128), F32).at[0, :v.shape[0]].set(v)

    w = ev_w_in[0]
    cos, sin = _rope_tables(s_len)
    wdt, wdt_hi = _hi_lo_weight(w[:, 1280:1296])
    p0 = dict(
        g_mix=norm_mix[0].reshape(1, D),
        wcat=jnp.concatenate([w[:, 0:1280].astype(BF16), w[:, 1296:2064].astype(BF16), wdt], axis=1),
        wdt_hi=wdt_hi,
        conv_w=ev_conv_w[0], conv_b=ev_conv_b[0].reshape(1, 768),
        dt_bias=pad128(ev_dt_bias[0].reshape(16)),
        q_norm=jnp.tile(ev_q_norm[0], 8).reshape(1, 512), k_norm=jnp.tile(ev_k_norm[0], 2).reshape(1, 128),
        cos=cos, sin=sin,
        alog_row=ev_a_log[0].reshape(1, 16), alog_col=ev_a_log[0].reshape(16, 1),
        d_skip=jnp.repeat(ev_d_skip[0], 64).reshape(1, 512), ssd_norm=ev_ssd_norm[0].reshape(1, 512))
    z, xs, bc, q, kk, vv, dtc, dtr = _in0(ctx[0], x[0], mod[0], p0, nct)
    yf = _ssd(False, xs, bc, dtc, dtr, p0, ncc, nlc)
    ymix = _ssd(True, xs, bc, dtc, dtr, p0, ncc, nlc, yf=yf, z=z)
    att = _attention(q, kk, vv, ev_sink[0].reshape(1, 8), ncc, nlc)
    x_mid0, hrow, bucket, rank, counts = _out_proj(
        [ymix, att], ev_w_out[0].astype(BF16), [ctx[0], x[0]], mod[0], norm_ffn[0].reshape(1, D),
        rw, rw_hi, rb_col, 0, ntiles, nct)
    x1 = _moe_block(x_mid0, hrow, bucket, rank, counts, mod[0], moe_w1, moe_w3, moe_w2, 0, 0, nct)

    w = od_w_in[0]
    wg, wg_hi = _hi_lo_weight(w[:, 3072:3104])
    p1 = dict(
        g_mix=norm_mix[1].reshape(1, D),
        wcat=jnp.concatenate([w[:, 0:3072].astype(BF16), wg], axis=1),
        wg_hi=wg_hi,
        conv_w=od_conv_w[0], conv_b=od_conv_b[0].reshape(1, 2048),
        gate_bias=pad128(jnp.concatenate([od_igate_b[0].reshape(16), od_fgate_b[0].reshape(16)])),
        head_norm=od_head_norm[0].reshape(1, 1024))
    q1, kt1, v1, og1, gc1, gr1 = _in1(x1, mod[1], p1, nct)
    hf = _mlstm(False, q1, kt1, v1, gc1, gr1, p1, ncc, nlc)
    hmix = _mlstm(True, q1, kt1, v1, gc1, gr1, p1, ncc, nlc, hf=hf, og=og1)
    x_mid1, hrow, bucket, rank, counts = _out_proj(
        [hmix], od_w_out[0].astype(BF16), [x1], mod[1], norm_ffn[1].reshape(1, D),
        rw, rw_hi, rb_col, nct, ntiles - nct, nct)
    return _moe_block(x_mid1, hrow, bucket, rank, counts, mod[1], moe_w1, moe_w3, moe_w2, 1, nct, nct)[None]
```

```python
import functools
import math

import jax
import jax.numpy as jnp
from jax import lax
from jax.experimental import pallas as pl
from jax.experimental.pallas import tpu as pltpu

F32 = jnp.float32
BF16 = jnp.bfloat16
I32 = jnp.int32

EPS = 1e-6
D = 1024
T = 128
TM = 256
TMM = 256
GRID_W = 64
ROPE_THETA = 10000.0
N_EXPERTS = 16
N_BUCKETS = 24
NB_PAD = 32
D_EXPERT = 512
TOK = 8
U32 = jnp.uint32
NEG_INF = float("-inf")
VMEM_LIMIT = 56 * 1024 * 1024

_NN = (((1,), (0,)), ((), ()))
_NT = (((1,), (1,)), ((), ()))
_TN = (((0,), (0,)), ((), ()))

_PAIR_A = (0, 0, 0, 1, 1, 2)
_PAIR_B = (1, 2, 3, 2, 3, 3)


def _cparams(*sem):
    return pltpu.CompilerParams(dimension_semantics=sem, vmem_limit_bytes=VMEM_LIMIT)


def _dot(a, b, dims=_NN):
    return lax.dot_general(a, b, dims, preferred_element_type=F32)


def _split(a, n):
    out = []
    r = a
    for _ in range(n):
        t = r.astype(BF16)
        out.append(t)
        r = r - t.astype(F32)
    return out


def _mdot(as_, bs, dims=_NN, order=None):
    if order is None:
        order = len(as_) + len(bs) - 2
    acc = None
    for i, a in enumerate(as_):
        for j, b in enumerate(bs):
            if i + j <= order:
                p = _dot(a, b, dims)
                acc = p if acc is None else acc + p
    return acc


def _sigmoid(x):
    return 1.0 / (1.0 + jnp.exp(-x))


def _silu(x):
    return x * _sigmoid(x)


def _log1p_exp_neg_abs(x):
    e = jnp.exp(-jnp.abs(x))
    u = 1.0 + e
    um1 = u - 1.0
    return jnp.where(um1 == 0.0, e, jnp.log(u) * (e / jnp.where(um1 == 0.0, 1.0, um1)))


def _softplus(x):
    return jnp.maximum(x, 0.0) + _log1p_exp_neg_abs(x)


def _log_sigmoid(x):
    return jnp.minimum(x, 0.0) - _log1p_exp_neg_abs(x)


def _norm_mod(x, g, sc, sh):
    ms = jnp.mean(x * x, axis=-1, keepdims=True)
    return (x * lax.rsqrt(ms + EPS)) * g * (1.0 + sc) + sh


def _tok_load(ref, chunk, n):
    return ref[pl.ds(chunk, n, stride=TOK), :]


def _tok_store(ref, chunk, val):
    ref[pl.ds(chunk, val.shape[0], stride=TOK), :] = val


def _tok_rows(ref, n):
    return jnp.concatenate([_tok_load(ref, c, n) for c in range(TOK)], axis=1)


def _tri(rev):
    r = lax.broadcasted_iota(I32, (T, T), 0)
    c = lax.broadcasted_iota(I32, (T, T), 1)
    return (c >= r) if rev else (c <= r)


def _cumsums(rev, col, row):
    tri = _tri(rev)
    tri_b = tri.astype(F32).astype(BF16)
    trit_b = _tri(not rev).astype(F32).astype(BF16)
    ccol = _mdot([tri_b], _split(col, 3))
    crow = _mdot(_split(row, 3), [trit_b])
    return tri, ccol, crow


def _lane_bcast(col):
    n = col.shape[1]
    col = jnp.concatenate([col, jnp.zeros((col.shape[0], 128 - n), F32)], axis=1)
    r = lax.broadcasted_iota(I32, (128, n * 128), 0)
    c = lax.broadcasted_iota(I32, (128, n * 128), 1) // 128
    return _mdot(_split(col, 3), [(r == c).astype(F32).astype(BF16)])


def _mod_kernel(c_ref, w_ref, b_ref, o_ref):
    a = _silu(c_ref[...])
    o_ref[0] = _mdot(_split(a, 2), _split(w_ref[0], 2), order=1) + b_ref[0]


def _modulation(c, c_ctx, w_mod, b_mod):
    depth = w_mod.shape[0]
    n = w_mod.shape[2]
    tn = 1536
    cc = jnp.zeros((8, D), F32).at[0].set(c[0]).at[1].set(c_ctx)
    out = pl.pallas_call(
        _mod_kernel,
        grid=(depth, n // tn),
        in_specs=[
            pl.BlockSpec((8, D), lambda l, j: (0, 0)),
            pl.BlockSpec((1, D, tn), lambda l, j: (l, 0, j)),
            pl.BlockSpec((1, 1, tn), lambda l, j: (l, 0, j)),
        ],
        out_specs=pl.BlockSpec((1, 8, tn), lambda l, j: (l, 0, j)),
        out_shape=jax.ShapeDtypeStruct((depth, 8, n), F32),
        compiler_params=_cparams("arbitrary", "arbitrary"),
        name="modulation",
    )(cc, w_mod, b_mod.reshape(depth, 1, n))
    return out[:, :2].reshape(depth, 2, 6, D)


def _halo_specs(nrows, tile_of):
    nb8 = nrows // 8
    return [
        pl.BlockSpec((8, D), lambda i: (jnp.maximum(tile_of(i) * (TM // 8) - 1, 0), 0)),
        pl.BlockSpec((TM, D), lambda i: (tile_of(i), 0)),
        pl.BlockSpec((8, D), lambda i: (jnp.minimum((tile_of(i) + 1) * (TM // 8), nb8 - 1), 0)),
    ]


def _ctx_tile(nct):
    return lambda i: jnp.minimum(i, nct - 1)


def _lat_tile(nct):
    return lambda i: jnp.maximum(i - nct, 0)


def _mod_spec(nct, t0=0):
    return pl.BlockSpec((1, 6, D), lambda i: (jnp.where(i + t0 < nct, 1, 0), 0, 0))


def _full(shape):
    nd = len(shape)
    return pl.BlockSpec(shape, lambda i: (0,) * nd)


def _seq_edges(i, nct, ntiles):
    prev_ok = jnp.logical_and(i != 0, i != nct).astype(F32)
    next_ok = jnp.logical_and(i != nct - 1, i != ntiles - 1).astype(F32)
    return prev_ok, next_ok


def _conv_silu(x, x_first_prev, x_last_next, cw, cb):
    n = x.shape[0]
    rows = lax.broadcasted_iota(I32, x.shape, 0)
    x_prev = jnp.where(rows == 0, x_first_prev, pltpu.roll(x, 1, 0))
    x_next = jnp.where(rows == n - 1, x_last_next, pltpu.roll(x, n - 1, 0))
    return _silu(x_prev * cw[0:1] + x * cw[1:2] + x_next * cw[2:3] + cb)


def _hi_lo_cols(blk, lo_pass, n):
    return blk + pltpu.roll(blk, 128 - n, 1) + lo_pass


def _hi_lo_weight(w):
    n = w.shape[1]
    hi = w.astype(BF16)
    lo = (w - hi.astype(F32)).astype(BF16)
    z = jnp.zeros((w.shape[0], 128 - 2 * n), BF16)
    return jnp.concatenate([hi, lo, z], axis=1), jnp.concatenate([hi, jnp.zeros_like(lo), z], axis=1)


def _head_rms(xf, gamma):
    r = lax.broadcasted_iota(I32, (128, 128), 0) // 64
    c = lax.broadcasted_iota(I32, (128, 128), 1) // 64
    ones_bd = (r == c).astype(F32).astype(BF16)
    outs = []
    for j in range(xf.shape[1] // 128):
        blk = xf[:, 128 * j:128 * (j + 1)]
        ssum = _dot((blk * blk).astype(BF16), ones_bd)
        outs.append(blk * lax.rsqrt(ssum * (1.0 / 64.0) + EPS))
    return jnp.concatenate(outs, axis=1) * gamma


def _rope(xf, cos, sin):
    lane = lax.broadcasted_iota(I32, (xf.shape[0], 128), 1)
    first = (lane % 32) < 16
    outs = []
    for j in range(xf.shape[1] // 128):
        blk = xf[:, 128 * j:128 * (j + 1)]
        partner = jnp.where(first, pltpu.roll(blk, 112, 1), pltpu.roll(blk, 16, 1))
        outs.append(blk * cos + partner * sin)
    return jnp.concatenate(outs, axis=1)


def _rope_tables(s_len):
    rows = s_len // GRID_W
    inv = ROPE_THETA ** (-(jnp.arange(32, dtype=I32) % 16).astype(F32) / 16.0)
    sign = jnp.where(jnp.arange(32) < 16, -1.0, 1.0).astype(F32)
    ang_r = jnp.arange(rows, dtype=F32)[:, None] * inv[None, :]
    ang_c = jnp.arange(GRID_W, dtype=F32)[:, None] * inv[None, :]

    def table(fr, fc):
        r = jnp.broadcast_to(fr[:, None, :], (rows, GRID_W, 32))
        c = jnp.broadcast_to(fc[None, :, :], (rows, GRID_W, 32))
        t = jnp.concatenate([r, c, r, c], axis=-1)
        return t.reshape(s_len, 128)

    return table(jnp.cos(ang_r), jnp.cos(ang_c)), table(jnp.sin(ang_r) * sign, jnp.sin(ang_c) * sign)


def _in0_kernel(nct, ntiles, cp_ref, c_ref, cn_ref, xp_ref, x_ref, xn_ref, mod_ref, g_ref, wcat_ref, wdth_ref,
                cw_ref, cb_ref, dtb_ref, qn_ref, kn_ref, cos_ref, sin_ref,
                z_ref, xs_ref, bc_ref, q_ref, kk_ref, vv_ref, dtc_ref, dtr_ref):
    i = pl.program_id(0)
    is_ctx = i < nct
    sh = mod_ref[0, 0:1, :]
    sc = mod_ref[0, 1:2, :]
    g = g_ref[...]
    x_all = jnp.concatenate([jnp.where(is_ctx, cp_ref[...], xp_ref[...]),
                             jnp.where(is_ctx, c_ref[...], x_ref[...]),
                             jnp.where(is_ctx, cn_ref[...], xn_ref[...])], axis=0)
    h_all = _norm_mod(x_all, g, sc, sh)
    hb_all = h_all.astype(BF16)
    main_all = _dot(hb_all, wcat_ref[...])
    h, hb, main = h_all[8:8 + TM], hb_all[8:8 + TM], main_all[8:8 + TM]
    prev_ok, next_ok = _seq_edges(i, nct, ntiles)
    xb_prev = main_all[7:8, 512:1280] * prev_ok
    xb_next = main_all[8 + TM:9 + TM, 512:1280] * next_ok
    act = _conv_silu(main[:, 512:1280], xb_prev, xb_next, cw_ref[...], cb_ref[...])
    z_ref[...] = main[:, 0:512].astype(BF16)
    xs_ref[...] = act[:, 0:512].astype(BF16)
    bc_ref[...] = act[:, 512:768].astype(BF16)
    cos = jnp.where(is_ctx, 1.0, cos_ref[...])
    sin = jnp.where(is_ctx, 0.0, sin_ref[...])
    q = _rope(_head_rms(main[:, 1280:1792], qn_ref[...]), cos, sin) * 0.125
    q_ref[...] = q.astype(BF16)
    k = _rope(_head_rms(main[:, 1792:1920], kn_ref[...]), cos, sin)
    kk_ref[...] = jnp.concatenate([k, pltpu.roll(k, 64, 1)], axis=1).astype(BF16)
    v = main[:, 1920:2048]
    vv_ref[...] = jnp.concatenate([v, pltpu.roll(v, 64, 1)], axis=1).astype(BF16)
    h_lo = (h - hb.astype(F32)).astype(BF16)
    dt = _softplus(_hi_lo_cols(main[:, 2048:2176], _dot(h_lo, wdth_ref[...]), 16) + dtb_ref[...])
    dtc_ref[...] = dt[:, 0:16]
    dtr_ref[...] = jnp.transpose(dt)[0:16, :]


def _in0(ctx2, x2, mod, p, nct):
    c_len, s_len = ctx2.shape[0], x2.shape[0]
    nt = c_len + s_len
    ntiles = nt // TM
    tile = lambda w: pl.BlockSpec((TM, w), lambda i: (i, 0))
    lat = _lat_tile(nct)
    lat_tile = lambda w: pl.BlockSpec((TM, w), lambda i: (lat(i), 0))
    outs = [(512, BF16), (512, BF16), (256, BF16), (512, BF16), (256, BF16), (256, BF16), (16, F32)]
    return pl.pallas_call(
        functools.partial(_in0_kernel, nct, ntiles),
        grid=(ntiles,),
        in_specs=_halo_specs(c_len, _ctx_tile(nct)) + _halo_specs(s_len, lat) + [
            _mod_spec(nct), _full((1, D)), _full((D, 2176)), _full((D, 128)),
            _full((3, 768)), _full((1, 768)), _full((1, 128)),
            _full((1, 512)), _full((1, 128)), lat_tile(128), lat_tile(128)],
        out_specs=[tile(w) for w, _ in outs] + [pl.BlockSpec((16, TM), lambda i: (0, i))],
        out_shape=[jax.ShapeDtypeStruct((nt, w), dt) for w, dt in outs]
        + [jax.ShapeDtypeStruct((16, nt), F32)],
        compiler_params=_cparams("arbitrary"),
        name="in_proj_even",
    )(ctx2, ctx2, ctx2, x2, x2, x2, mod, p["g_mix"], p["wcat"], p["wdt_hi"], p["conv_w"], p["conv_b"],
      p["dt_bias"], p["q_norm"], p["k_norm"], p["cos"], p["sin"])


def _scan_chunk_map(rev, ncc, nlc):
    if not rev:
        return lambda j: j
    return lambda j: jnp.where(j < ncc, ncc - 1 - j, ncc + nlc - 1 - (j - ncc))


def _ssd_kernel(rev, *refs):
    if rev:
        (xs_ref, bc_ref, dtc_ref, dtr_ref, alr_ref, alc_ref, yf_ref, z_ref, dsk_ref, nrm_ref,
         o_ref, st_ref) = refs
    else:
        xs_ref, bc_ref, dtc_ref, dtr_ref, alr_ref, alc_ref, o_ref, st_ref = refs
    j = pl.program_id(0)

    @pl.when(j == 0)
    def _():
        st_ref[...] = jnp.zeros_like(st_ref)

    d = 8 if rev else 0
    a_coef_row = -jnp.exp(alr_ref[...])[:, d:d + 8]
    a_coef_col = -jnp.exp(alc_ref[...])[d:d + 8, :]
    dtc = dtc_ref[:, d:d + 8]
    dtr = dtr_ref[d:d + 8, :]
    tri, acs_col, acs_row = _cumsums(rev, dtc * a_coef_row, dtr * a_coef_col)
    end = 0 if rev else T - 1
    atot_col = acs_row[:, end:end + 1]
    acs_bc = _lane_bcast(acs_col)
    dec_row = jnp.exp(atot_col - acs_row) * dtr
    xs = xs_ref[...]
    bm = bc_ref[:, 0:128]
    cm = bc_ref[:, 128:256]
    lane = lax.broadcasted_iota(I32, (T, 128), 1)
    lo = lane < 64
    zero_b = jnp.zeros((T, 128), BF16)
    hi_half = jnp.logical_not(lo)
    cgs = [jnp.where(lo, cm, zero_b), jnp.where(hi_half, cm, zero_b)]
    cbs = [_dot(cgs[g], bm, _NT) for g in range(2)]
    sub = lax.broadcasted_iota(I32, (128, T), 0)
    eye = (sub == lax.broadcasted_iota(I32, (128, T), 1)).astype(F32).astype(BF16)
    bmt = _dot(eye, bm, _NT)
    bgts = [jnp.where(sub < 64, bmt, 0.0), jnp.where(sub >= 64, bmt, 0.0)]
    xpairs = []
    for pr in range(4):
        xp = xs[:, 128 * pr:128 * (pr + 1)]
        xpairs.append(jnp.concatenate([jnp.where(lo, xp, zero_b), jnp.where(hi_half, xp, zero_b)], axis=0))
    y_off = []
    for pr in range(4):
        g, h0, h1 = pr // 2, 2 * pr, 2 * pr + 1
        st = st_ref[pr]
        eacs = jnp.exp(jnp.where(lo, acs_bc[:, 128 * h0:128 * (h0 + 1)], acs_bc[:, 128 * h1:128 * (h1 + 1)]))
        y_off.append(_dot(cgs[g], st.astype(BF16)) * eacs)
        bdec = jnp.concatenate([(bgts[g] * dec_row[h0:h0 + 1, :]).astype(BF16),
                                (bgts[g] * dec_row[h1:h1 + 1, :]).astype(BF16)], axis=1)
        carry = jnp.where(lo[0:1, :], jnp.exp(atot_col[h0:h0 + 1, :]), jnp.exp(atot_col[h1:h1 + 1, :]))
        st_ref[pr] = carry * st + _dot(bdec, xpairs[pr])
    ys = []
    for pr in range(4):
        ms = []
        for hd in (2 * pr, 2 * pr + 1):
            diff = acs_bc[:, 128 * hd:128 * (hd + 1)] - acs_row[hd:hd + 1, :]
            lmat = jnp.exp(jnp.where(tri, diff, NEG_INF))
            ms.append((cbs[hd // 4] * lmat * dtr[hd:hd + 1, :]).astype(BF16))
        ys.append(_dot(jnp.concatenate(ms, axis=1), xpairs[pr]) + y_off[pr])
    y = jnp.concatenate(ys, axis=1)
    if not rev:
        o_ref[...] = y
    else:
        ytot = y + yf_ref[...] + dsk_ref[...] * xs.astype(F32)
        gated = ytot * _silu(z_ref[...].astype(F32))
        ms = jnp.mean(gated * gated, axis=-1, keepdims=True)
        o_ref[...] = (gated * lax.rsqrt(ms + EPS) * nrm_ref[...]).astype(BF16)


def _ssd(rev, xs, bc, dtc, dtr, p, ncc, nlc, yf=None, z=None):
    nt = xs.shape[0]
    cmap = _scan_chunk_map(rev, ncc, nlc)
    blk = lambda w: pl.BlockSpec((T, w), lambda j: (cmap(j), 0))
    in_specs = [blk(512), blk(256), blk(16), pl.BlockSpec((16, T), lambda j: (0, cmap(j))),
                _full((1, 16)), _full((16, 1))]
    args = [xs, bc, dtc, dtr, p["alog_row"], p["alog_col"]]
    if rev:
        in_specs += [blk(512), blk(512), _full((1, 512)), _full((1, 512))]
        args += [yf, z, p["d_skip"], p["ssd_norm"]]
    return pl.pallas_call(
        functools.partial(_ssd_kernel, rev),
        grid=(ncc + nlc,),
        in_specs=in_specs,
        out_specs=blk(512),
        out_shape=jax.ShapeDtypeStruct((nt, 512), BF16 if rev else F32),
        scratch_shapes=[pltpu.VMEM((4, 128, 128), F32)],
        compiler_params=_cparams("arbitrary"),
        name="ssd_bwd" if rev else "ssd_fwd",
    )(*args)


def _attn_kernel(ncc, nblk, q_ref, kp_ref, kc_ref, kn_ref, vp_ref, vc_ref, vn_ref, kx_ref, vx_ref,
                 sink_ref, o_ref):
    j = pl.program_id(0)
    c_len = kx_ref.shape[0]
    is_lat = j >= ncc
    prev_ok = jnp.logical_and(is_lat, j >= ncc + 1)
    next_ok = jnp.logical_and(is_lat, j <= nblk - 2)
    r = lax.broadcasted_iota(I32, (T, T), 0)
    c = lax.broadcasted_iota(I32, (T, T), 1)
    zero = jnp.zeros((T, T), F32)
    ninf = jnp.full((T, T), NEG_INF, F32)
    bias = jnp.concatenate([
        jnp.where(jnp.logical_and(prev_ok, c >= r), zero, ninf),
        jnp.where(is_lat, zero, ninf),
        jnp.where(jnp.logical_and(next_ok, c <= r), zero, ninf),
        jnp.zeros((T, c_len), F32)], axis=1)
    k_all = jnp.concatenate([kp_ref[...], kc_ref[...], kn_ref[...], kx_ref[...]], axis=0)
    v_all = jnp.concatenate([vp_ref[...], vc_ref[...], vn_ref[...], vx_ref[...]], axis=0)
    lo = lax.broadcasted_iota(I32, (T, 128), 1) < 64
    zero_b = jnp.zeros((T, 128), BF16)
    q = q_ref[...]
    sink = sink_ref[...]
    bias4 = jnp.concatenate([bias] * 4, axis=0)
    outs = {}
    stacks = [[hd for hd in range(8) if (hd // 4 + hd % 2) % 2 == b] for b in range(2)]
    s_all = []
    for b in range(2):
        qs = []
        for hd in stacks[b]:
            qp = q[:, 128 * (hd // 2):128 * (hd // 2 + 1)]
            qs.append(jnp.where(lo, zero_b, qp) if hd % 2 else jnp.where(lo, qp, zero_b))
        s_all.append(_dot(jnp.concatenate(qs, axis=0), k_all[:, 128 * b:128 * (b + 1)], _NT) + bias4)
    for b in range(2):
        s = s_all[b]
        sk = jnp.concatenate([jnp.broadcast_to(sink[:, hd:hd + 1], (T, 1)) for hd in stacks[b]], axis=0)
        m = jnp.maximum(jnp.max(s, axis=-1, keepdims=True), sk)
        pr = jnp.exp(s - m)
        den = jnp.sum(pr, axis=-1, keepdims=True) + jnp.exp(sk - m)
        o = _dot(pr.astype(BF16), v_all[:, 128 * b:128 * (b + 1)]) / den
        for n, hd in enumerate(stacks[b]):
            outs[hd] = o[T * n:T * (n + 1)]
    for pair in range(4):
        o_ref[:, 128 * pair:128 * (pair + 1)] = jnp.where(lo, outs[2 * pair], outs[2 * pair + 1]).astype(BF16)


def _attention(q, kk, vv, sink, ncc, nlc):
    nt = q.shape[0]
    nblk = ncc + nlc
    c_len = ncc * T
    prev = lambda w: pl.BlockSpec((T, w), lambda j: (jnp.maximum(j - 1, 0), 0))
    cur = lambda w: pl.BlockSpec((T, w), lambda j: (j, 0))
    nxt = lambda w: pl.BlockSpec((T, w), lambda j: (jnp.minimum(j + 1, nblk - 1), 0))
    ctx = lambda w: pl.BlockSpec((c_len, w), lambda j: (0, 0))
    return pl.pallas_call(
        functools.partial(_attn_kernel, ncc, nblk),
        grid=(nblk,),
        in_specs=[cur(512), prev(256), cur(256), nxt(256), prev(256), cur(256), nxt(256),
                  ctx(256), ctx(256), _full((1, 8))],
        out_specs=cur(512),
        out_shape=jax.ShapeDtypeStruct((nt, 512), BF16),
        compiler_params=_cparams("arbitrary"),
        name="window_attention",
    )(q, kk, kk, kk, vv, vv, vv, kk, vv, sink)


def _in1_kernel(nct, ntiles, xp_ref, x_ref, xn_ref, mod_ref, g_ref,
                wcat_ref, wgh_ref, cw_ref, cb_ref, gb_ref,
                q_ref, kt_ref, v_ref, o_ref, gc_ref, gr_ref):
    i = pl.program_id(0)
    sh = mod_ref[0, 0:1, :]
    sc = mod_ref[0, 1:2, :]
    g = g_ref[...]
    x_all = jnp.concatenate([xp_ref[...], x_ref[...], xn_ref[...]], axis=0)
    h_all = _norm_mod(x_all, g, sc, sh)
    hb_all = h_all.astype(BF16)
    main_all = _dot(hb_all, wcat_ref[...])
    h, hb, main = h_all[8:8 + TM], hb_all[8:8 + TM], main_all[8:8 + TM]
    prev_ok, next_ok = _seq_edges(i, nct, ntiles)
    x_prev = main_all[7:8, 0:2048] * prev_ok
    x_next = main_all[8 + TM:9 + TM, 0:2048] * next_ok
    act = _conv_silu(main[:, 0:2048], x_prev, x_next, cw_ref[...], cb_ref[...])
    q_ref[...] = act[:, 0:512].astype(BF16)
    kt_ref[...] = jnp.transpose(act[:, 512:1024] * 0.125).astype(BF16)
    v_ref[...] = act[:, 1024:2048].astype(BF16)
    o_ref[...] = main[:, 2048:3072].astype(BF16)
    h_lo = (h - hb.astype(F32)).astype(BF16)
    gates = _hi_lo_cols(main[:, 3072:3200], _dot(h_lo, wgh_ref[...]), 32) + gb_ref[...]
    lane = lax.broadcasted_iota(I32, gates.shape, 1)
    gates = jnp.where(lane < 16, gates, _log_sigmoid(gates))
    gc_ref[...] = gates[:, 0:32]
    gr_ref[...] = jnp.transpose(gates)[0:32, :]


def _in1(x1, mod, p, nct):
    nt = x1.shape[0]
    ntiles = nt // TM
    tile = lambda w: pl.BlockSpec((TM, w), lambda i: (i, 0))
    return pl.pallas_call(
        functools.partial(_in1_kernel, nct, ntiles),
        grid=(ntiles,),
        in_specs=_halo_specs(nt, lambda i: i) + [
            _mod_spec(nct), _full((1, D)), _full((D, 3200)), _full((D, 128)),
            _full((3, 2048)), _full((1, 2048)), _full((1, 128))],
        out_specs=[tile(512), pl.BlockSpec((512, TM), lambda i: (0, i)), tile(1024), tile(1024),
                   tile(32), pl.BlockSpec((32, TM), lambda i: (0, i))],
        out_shape=[jax.ShapeDtypeStruct((nt, 512), BF16), jax.ShapeDtypeStruct((512, nt), BF16),
                   jax.ShapeDtypeStruct((nt, 1024), BF16), jax.ShapeDtypeStruct((nt, 1024), BF16),
                   jax.ShapeDtypeStruct((nt, 32), F32), jax.ShapeDtypeStruct((32, nt), F32)],
        compiler_params=_cparams("arbitrary"),
        name="in_proj_odd",
    )(x1, x1, x1, mod, p["g_mix"], p["wcat"], p["wg_hi"], p["conv_w"], p["conv_b"], p["gate_bias"])


def _mlstm_kernel(rev, *refs):
    if rev:
        (q_ref, kt_ref, v_ref, gc_ref, gr_ref, hf_ref, og_ref, hn_ref, o_ref,
         c_ref, mc_ref, mr_ref) = refs
    else:
        q_ref, kt_ref, v_ref, gc_ref, gr_ref, o_ref, c_ref, mc_ref, mr_ref = refs
    j = pl.program_id(0)

    @pl.when(j == 0)
    def _():
        c_ref[...] = jnp.zeros_like(c_ref)
        mc_ref[...] = jnp.zeros_like(mc_ref)
        mr_ref[...] = jnp.zeros_like(mr_ref)

    d = 8 if rev else 0
    ig_col = gc_ref[:, d:d + 8]
    lf_col = gc_ref[:, 16 + d:24 + d]
    ig_row = gr_ref[d:d + 8, :]
    lf_row = gr_ref[16 + d:24 + d, :]
    tri, b_col, b_row = _cumsums(rev, lf_col, lf_row)
    end = 0 if rev else T - 1
    blast_row = b_col[end:end + 1, :]
    blast_col = b_row[:, end:end + 1]
    wend_row = blast_col - b_row + ig_row
    ac_col = jnp.max(wend_row, axis=1, keepdims=True)
    eend_row = jnp.exp(wend_row - ac_col)
    ac_row = jnp.max(blast_row - b_col + ig_col, axis=0, keepdims=True)
    m_col = mc_ref[:, 0:1]
    m_row = mr_ref[0:1, 0:8]
    mnew_col = jnp.maximum(blast_col + m_col, ac_col)
    sp_col = jnp.exp(blast_col + m_col - mnew_col)
    sc_col = jnp.exp(ac_col - mnew_col)
    mnew_row = jnp.maximum(blast_row + m_row, ac_row)
    b_bc = _lane_bcast(b_col)
    q = q_ref[...]
    ones_b = jnp.ones((T, 128), BF16)
    sub = lax.broadcasted_iota(I32, (128, T), 0)
    zero_k = jnp.zeros((128, T), BF16)
    def head_matmuls(hd):
        pair, hi = hd // 2, hd % 2
        qp = q[:, 128 * pair:128 * (pair + 1)]
        ktp = kt_ref[128 * pair:128 * (pair + 1), :]
        kth = jnp.where((sub >= 64) if hi else (sub < 64), ktp, zero_k)
        vaug = jnp.concatenate([v_ref[:, 128 * hd:128 * (hd + 1)], ones_b], axis=1)
        cst = c_ref[hd]
        sqk = _dot(qp, kth)
        inter = _dot(qp, cst.astype(BF16))
        kte = (kth.astype(F32) * eend_row[hd:hd + 1, :]).astype(BF16)
        c_ref[hd] = sp_col[hd:hd + 1, :] * cst + sc_col[hd:hd + 1, :] * _dot(kte, vaug)
        return sqk, inter, vaug

    nxt = head_matmuls(0)
    for hd in range(8):
        sqk, inter, vaug = nxt
        if hd + 1 < 8:
            nxt = head_matmuls(hd + 1)
        bh = b_bc[:, 128 * hd:128 * (hd + 1)]
        dlog = jnp.where(tri, bh - b_row[hd:hd + 1, :] + ig_row[hd:hd + 1, :], NEG_INF)
        gh = bh + mr_ref[0:1, hd:hd + 1]
        mstar = jnp.maximum(gh, jnp.max(dlog, axis=-1, keepdims=True))
        w = (jnp.exp(dlog - mstar) * sqk).astype(BF16)
        intra = _dot(w, vaug)
        e_int = jnp.exp(gh - mstar)
        den = jnp.maximum(jnp.abs(intra[:, 128:256] + e_int * inter[:, 128:256]), jnp.exp(-mstar))
        hh = (intra[:, 0:128] + e_int * inter[:, 0:128]) / den
        if rev:
            hh = hh + hf_ref[:, 128 * hd:128 * (hd + 1)]
            ms = jnp.mean(hh * hh, axis=-1, keepdims=True)
            hh = hh * lax.rsqrt(ms + EPS) * hn_ref[:, 128 * hd:128 * (hd + 1)]
            og = og_ref[:, 128 * hd:128 * (hd + 1)].astype(F32)
            o_ref[:, 128 * hd:128 * (hd + 1)] = (hh * _sigmoid(og)).astype(BF16)
        else:
            o_ref[:, 128 * hd:128 * (hd + 1)] = hh
    mc_ref[...] = jnp.broadcast_to(mnew_col, mc_ref.shape)
    mr_ref[...] = jnp.broadcast_to(jnp.concatenate([mnew_row, jnp.zeros((1, 120), F32)], axis=1), mr_ref.shape)


def _mlstm(rev, q, kt, v, gc, gr, p, ncc, nlc, hf=None, og=None):
    nt = q.shape[0]
    cmap = _scan_chunk_map(rev, ncc, nlc)
    blk = lambda w: pl.BlockSpec((T, w), lambda j: (cmap(j), 0))
    blk_t = lambda h: pl.BlockSpec((h, T), lambda j: (0, cmap(j)))
    in_specs = [blk(512), blk_t(512), blk(1024), blk(32), blk_t(32)]
    args = [q, kt, v, gc, gr]
    if rev:
        in_specs += [blk(1024), blk(1024), _full((1, 1024))]
        args += [hf, og, p["head_norm"]]
    return pl.pallas_call(
        functools.partial(_mlstm_kernel, rev),
        grid=(ncc + nlc,),
        in_specs=in_specs,
        out_specs=blk(1024),
        out_shape=jax.ShapeDtypeStruct((nt, 1024), BF16 if rev else F32),
        scratch_shapes=[pltpu.VMEM((8, 128, 256), F32), pltpu.VMEM((8, 128), F32), pltpu.VMEM((8, 128), F32)],
        compiler_params=_cparams("arbitrary"),
        name="mlstm_bwd" if rev else "mlstm_fwd",
    )(*args)


def _route(logits_t, rb_col):
    scores = _sigmoid(logits_t)
    biased = scores + rb_col
    row = lambda a, e: a[e:e + 1, :]
    gscore = []
    for g in range(4):
        b0, b1, b2, b3 = (row(biased, 4 * g + e) for e in range(4))
        h1, l1 = jnp.maximum(b0, b1), jnp.minimum(b0, b1)
        h2, l2 = jnp.maximum(b2, b3), jnp.minimum(b2, b3)
        gscore.append(jnp.maximum(h1, h2) + jnp.maximum(jnp.minimum(h1, h2), jnp.maximum(l1, l2)))
    gidx = jnp.zeros_like(gscore[0], dtype=I32)
    best = gscore[0]
    for g in range(1, 4):
        better = gscore[g] > best
        gidx = jnp.where(better, g, gidx)
        best = jnp.where(better, gscore[g], best)

    def pick(a, e):
        out = row(a, e)
        for g in range(1, 4):
            out = jnp.where(gidx == g, row(a, 4 * g + e), out)
        return out

    sb = [pick(biased, e) for e in range(4)]
    ss = [pick(scores, e) for e in range(4)]
    i1 = jnp.zeros_like(gidx)
    v1, s1 = sb[0], ss[0]
    for e in range(1, 4):
        better = sb[e] > v1
        i1 = jnp.where(better, e, i1)
        v1 = jnp.where(better, sb[e], v1)
        s1 = jnp.where(better, ss[e], s1)
    i2 = jnp.zeros_like(gidx)
    v2 = jnp.full_like(v1, NEG_INF)
    s2 = jnp.zeros_like(s1)
    for e in range(4):
        better = jnp.logical_and(i1 != e, sb[e] > v2)
        i2 = jnp.where(better, e, i2)
        v2 = jnp.where(better, sb[e], v2)
        s2 = jnp.where(better, ss[e], s2)
    tot = s1 + s2
    w1 = s1 / tot
    w2 = s2 / tot
    first_low = i1 < i2
    a = jnp.minimum(i1, i2)
    b = jnp.maximum(i1, i2)
    off = jnp.where(a == 0, 0, jnp.where(a == 1, 3, 5))
    bucket = 6 * gidx + off + (b - a - 1)
    return bucket, jnp.where(first_low, w1, w2), jnp.where(first_low, w2, w1)


def _out_kernel(nmix, nct, two_src, *refs):
    mix_refs = refs[:nmix]
    refs = refs[nmix:]
    if two_src:
        c_ref, x_ref = refs[:2]
        refs = refs[2:]
    else:
        x_ref = refs[0]
        refs = refs[1:]
    (w_ref, mod_ref, g_ref, rw_ref, rwh_ref, rb_ref,
     xmid_ref, hrow_ref, bucket_ref, rank_ref, cnt_ref, cnt_scr) = refs
    i = pl.program_id(0)

    @pl.when(i == 0)
    def _():
        cnt_scr[...] = jnp.zeros_like(cnt_scr)

    mix = mix_refs[0][...] if nmix == 1 else jnp.concatenate([mr[...] for mr in mix_refs], axis=1)
    x = jnp.where(i < nct, c_ref[...], x_ref[...]) if two_src else x_ref[...]
    x_mid = x + mod_ref[0, 2:3, :] * _dot(mix, w_ref[...])
    xmid_ref[...] = x_mid
    h = _norm_mod(x_mid, g_ref[...], mod_ref[0, 4:5, :], mod_ref[0, 3:4, :])
    hb = h.astype(BF16)
    h_lo = (h - hb.astype(F32)).astype(BF16)
    logits = _hi_lo_cols(_dot(hb, rw_ref[...]), _dot(h_lo, rwh_ref[...]), 16)
    logits_t = jnp.transpose(logits)[0:16, :]
    bucket, w_lo, w_hi = _route(logits_t, rb_ref[...])
    bits = pltpu.bitcast(hb.astype(F32), U32)
    words = (bits[:, 512:1024] & jnp.uint32(0xFFFF0000)) | (bits[:, 0:512] >> 16)
    for cblk in range(4):
        _tok_store(hrow_ref, cblk, words[:, 128 * cblk:128 * (cblk + 1)])
    wrows = jnp.concatenate([w_lo, w_hi, jnp.zeros((126, TM), F32)], axis=0)
    _tok_store(hrow_ref, 4, pltpu.bitcast(jnp.transpose(wrows), U32))
    for cblk in range(5, TOK):
        _tok_store(hrow_ref, cblk, jnp.zeros((TM, 128), U32))
    brow = lax.broadcasted_iota(I32, (NB_PAD, TM), 0)
    onehot = (brow == bucket).astype(F32)
    r = lax.broadcasted_iota(I32, (TM, TM), 0)
    c = lax.broadcasted_iota(I32, (TM, TM), 1)
    before = (r < c).astype(F32).astype(BF16)
    cum = _dot(onehot.astype(BF16), before)
    base = cnt_scr[:, 0:1]
    rank = jnp.sum(onehot * (cum + base), axis=0, keepdims=True)
    bucket_ref[0] = bucket
    rank_ref[0] = rank.astype(I32)
    cnt = cnt_scr[...] + jnp.sum(onehot, axis=1, keepdims=True)
    cnt_scr[...] = cnt
    cnt_ref[...] = cnt.astype(I32)


def _out_proj(mixes, w_out, xs, mod, g_ffn, rw, rw_hi, rb_col, t0, ntiles, nct):
    nmix = len(mixes)
    two_src = len(xs) == 2
    n = ntiles * TM
    tile_in = lambda w: pl.BlockSpec((TM, w), lambda i: (i + t0, 0))
    tile_out = lambda w: pl.BlockSpec((TM, w), lambda i: (i, 0))
    row_out = pl.BlockSpec((1, 1, TM), lambda i: (i, 0, 0))
    if two_src:
        ctx_of, lat_of = _ctx_tile(nct), _lat_tile(nct)
        x_specs = [pl.BlockSpec((TM, D), lambda i: (ctx_of(i), 0)), pl.BlockSpec((TM, D), lambda i: (lat_of(i), 0))]
    else:
        x_specs = [tile_in(D)]
    in_specs = [tile_in(mx.shape[1]) for mx in mixes] + x_specs + [
        _full((D, D)), _mod_spec(nct, t0), _full((1, D)), _full((D, 128)), _full((D, 128)), _full((16, 1))]
    return pl.pallas_call(
        functools.partial(_out_kernel, nmix, nct, two_src),
        grid=(ntiles,),
        in_specs=in_specs,
        out_specs=[tile_out(D), pl.BlockSpec((TM * TOK, 128), lambda i: (i, 0)), row_out, row_out,
                   _full((NB_PAD, 128))],
        out_shape=[jax.ShapeDtypeStruct((n, D), F32), jax.ShapeDtypeStruct((n * TOK, 128), U32),
                   jax.ShapeDtypeStruct((ntiles, 1, TM), I32), jax.ShapeDtypeStruct((ntiles, 1, TM), I32),
                   jax.ShapeDtypeStruct((NB_PAD, 128), I32)],
        scratch_shapes=[pltpu.VMEM((NB_PAD, 128), F32)],
        compiler_params=_cparams("arbitrary"),
        name="out_proj_router",
    )(*mixes, *xs, w_out, mod, g_ffn, rw, rw_hi, rb_col)


def _scatter_kernel(ntiles, pos_ref, flo_ref, fhi_ref, h_ref, dst_ref, sem):
    i = pl.program_id(0)

    def tok(ref, t):
        return ref.at[pl.ds(pl.multiple_of(t * TOK, TOK), TOK), :]

    def copy(r, d_row):
        return pltpu.make_async_copy(tok(h_ref, r), tok(dst_ref, d_row), sem)

    def wait_rows(lo, hi, unroll):
        def body(r, carry):
            copy(0, 0).wait()
            return carry
        lax.fori_loop(lo, hi, body, 0, unroll=unroll)

    def start(r8, c):
        for k in range(8):
            r = r8 * 8 + k
            copy(r, pos_ref[i * TM + r]).start(priority=k % 2)
        return c
    lax.fori_loop(0, TM // 8, start, 0)
    wait_rows(0, TM, 8)

    @pl.when(i == ntiles - 1)
    def _():
        for b in range(N_BUCKETS):
            lo, hi = flo_ref[b], fhi_ref[b]

            def fill(r, c):
                copy(0, r).start()
                return c
            lax.fori_loop(lo, hi, fill, 0)
            wait_rows(lo, hi, 1)

        def tile_copy(j):
            rows = TMM * TOK
            return pltpu.make_async_copy(h_ref, dst_ref.at[pl.ds(pl.multiple_of(j * rows, rows), rows), :], sem)

        def fill_tile(j, c):
            tile_copy(j).start()
            return c

        def wait_tile(j, c):
            tile_copy(j).wait()
            return c
        lax.fori_loop(flo_ref[N_BUCKETS], fhi_ref[N_BUCKETS], fill_tile, 0)
        lax.fori_loop(flo_ref[N_BUCKETS], fhi_ref[N_BUCKETS], wait_tile, 0)


def _scatter_rows(hrow, pos, fill_lo, fill_hi, n, npad):
    assert TM == TMM
    return pl.pallas_call(
        functools.partial(_scatter_kernel, n // TM),
        grid_spec=pltpu.PrefetchScalarGridSpec(
            num_scalar_prefetch=3,
            grid=(n // TM,),
            in_specs=[pl.BlockSpec((TM * TOK, 128), lambda i, *_: (i, 0))],
            out_specs=pl.BlockSpec(memory_space=pl.ANY),
            scratch_shapes=[pltpu.SemaphoreType.DMA(())],
        ),
        out_shape=jax.ShapeDtypeStruct((npad * TOK, 128), hrow.dtype),
        compiler_params=_cparams("arbitrary"),
        name="moe_scatter_rows",
    )(pos, fill_lo, fill_hi, hrow)


def _combine_kernel(ntiles, pos_ref, x_ref, mod_ref, ys_ref, o_ref, ybuf, sem):
    i = pl.program_id(0)

    def copy(tile, slot, r):
        src = pl.multiple_of(pos_ref[tile * TM + r] * TOK, TOK)
        dst = pl.multiple_of(r * TOK, TOK)
        return pltpu.make_async_copy(ys_ref.at[pl.ds(src, TOK), :], ybuf.at[slot, pl.ds(dst, TOK), :], sem.at[slot])

    def start_tile(tile, slot):
        def body(r8, carry):
            for k in range(8):
                copy(tile, slot, r8 * 8 + k).start(priority=k % 2)
            return carry
        lax.fori_loop(0, TM // 8, body, 0)

    @pl.when(i == 0)
    def _():
        start_tile(0, 0)

    @pl.when(i + 1 < ntiles)
    def _():
        start_tile(i + 1, (i + 1) % 2)

    slot = i % 2

    def wait_body(r, carry):
        copy(i, slot, 0).wait()
        return carry
    lax.fori_loop(0, TM, wait_body, 0, unroll=8)
    o_ref[...] = x_ref[...] + mod_ref[0, 5:6, :] * _tok_rows(ybuf.at[slot], TM)


def _combine(x_mid, mod, ys, pos, t0, nct):
    n = x_mid.shape[0]
    ntiles = n // TM
    return pl.pallas_call(
        functools.partial(_combine_kernel, ntiles),
        grid_spec=pltpu.PrefetchScalarGridSpec(
            num_scalar_prefetch=1,
            grid=(ntiles,),
            in_specs=[pl.BlockSpec((TM, D), lambda i, *_: (i, 0)),
                      pl.BlockSpec((1, 6, D), lambda i, *_: (jnp.where(i + t0 < nct, 1, 0), 0, 0)),
                      pl.BlockSpec(memory_space=pl.ANY)],
            out_specs=pl.BlockSpec((TM, D), lambda i, *_: (i, 0)),
            scratch_shapes=[pltpu.VMEM((2, TM * TOK, 128), F32), pltpu.SemaphoreType.DMA((2,))],
        ),
        out_shape=jax.ShapeDtypeStruct((n, D), F32),
        compiler_params=_cparams("arbitrary"),
        name="moe_combine",
    )(pos, x_mid, mod, ys)


def _moe_kernel(tea_ref, teb_ref, tblk_ref, tval_ref, x_ref, w1a_ref, w3a_ref, w2a_ref,
                w1b_ref, w3b_ref, w2b_ref, y_ref):
    del tea_ref, teb_ref, tblk_ref
    j = pl.program_id(0)

    @pl.when(tval_ref[j] != 0)
    def _():
        words = [_tok_load(x_ref, cblk, TMM) for cblk in range(4)]
        low = [pltpu.bitcast(wd << 16, F32).astype(BF16) for wd in words]
        high = [pltpu.bitcast(wd & jnp.uint32(0xFFFF0000), F32).astype(BF16) for wd in words]
        hb = jnp.concatenate(low + high, axis=1)
        gates = pltpu.bitcast(_tok_load(x_ref, 4, TMM), F32)
        acts = []
        for w1_ref, w3_ref, lane in ((w1a_ref, w3a_ref, 0), (w1b_ref, w3b_ref, 1)):
            u = _dot(hb, w1_ref[0, 0].astype(BF16))
            v = _dot(hb, w3_ref[0, 0].astype(BF16))
            acts.append((_silu(u) * v * gates[:, lane:lane + 1]).astype(BF16))
        y = _dot(acts[0], w2a_ref[0, 0].astype(BF16)) + _dot(acts[1], w2b_ref[0, 0].astype(BF16))
        for cblk in range(TOK):
            _tok_store(y_ref, cblk, y[:, 128 * cblk:128 * (cblk + 1)])

    @pl.when(tval_ref[j] == 0)
    def _():
        y_ref[...] = jnp.zeros_like(y_ref)


def _moe(xs_sorted, w1, w3, w2, layer, tile_ea, tile_eb, tile_blk, tile_valid):
    npad = xs_sorted.shape[0] // TOK
    ntile = npad // TMM
    wspec = lambda shape, which: pl.BlockSpec(
        (1, 1) + shape, lambda j, ea, eb, blk, val: (layer, (ea, eb)[which][j], 0, 0))
    up, down = (D, D_EXPERT), (D_EXPERT, D)
    return pl.pallas_call(
        _moe_kernel,
        grid_spec=pltpu.PrefetchScalarGridSpec(
            num_scalar_prefetch=4,
            grid=(ntile,),
            in_specs=[pl.BlockSpec((TMM * TOK, 128), lambda j, ea, eb, blk, val: (blk[j], 0)),
                      wspec(up, 0), wspec(up, 0), wspec(down, 0), wspec(up, 1), wspec(up, 1), wspec(down, 1)],
            out_specs=pl.BlockSpec((TMM * TOK, 128), lambda j, ea, eb, blk, val: (j, 0)),
        ),
        out_shape=jax.ShapeDtypeStruct((npad * TOK, 128), F32),
        compiler_params=_cparams("arbitrary"),
        name="moe_experts",
    )(tile_ea, tile_eb, tile_blk, tile_valid, xs_sorted, w1, w3, w2, w1, w3, w2)


def _moe_block(x_mid, hrow, bucket, rank, counts, mod, w1, w3, w2, layer, t0, nct):
    n = hrow.shape[0] // TOK
    ntile = n // TMM + N_BUCKETS
    npad = ntile * TMM
    cnt = counts[:N_BUCKETS, 0]
    padded = ((cnt + TMM - 1) // TMM) * TMM
    ends = jnp.cumsum(padded)
    starts = ends - padded
    total_tiles = ends[-1] // TMM
    tiles = jnp.arange(ntile, dtype=I32)
    tile_valid = (tiles < total_tiles).astype(I32)
    tile_blk = jnp.minimum(tiles, jnp.maximum(total_tiles - 1, 0))
    tile_bucket = jnp.minimum(jnp.sum((ends[None, :] <= (tile_blk * TMM)[:, None]).astype(I32), axis=1), N_BUCKETS - 1)
    pair = tile_bucket % 6
    grp = tile_bucket // 6
    pair_a = jnp.where(pair < 3, 0, jnp.where(pair < 5, 1, 2))
    pair_b = jnp.where(pair < 3, pair + 1, jnp.where(pair < 5, pair - 1, 3))
    tile_ea = (4 * grp + pair_a).astype(I32)
    tile_eb = (4 * grp + pair_b).astype(I32)
    bucket = bucket.reshape(-1)
    onehot = (bucket[:, None] == jnp.arange(N_BUCKETS, dtype=I32)[None, :]).astype(I32)
    pos = (rank.reshape(-1) + jnp.sum(onehot * starts[None, :], axis=1)).astype(I32)
    pad32 = lambda a, tail: jnp.zeros((NB_PAD,), I32).at[:N_BUCKETS].set(a.astype(I32)).at[N_BUCKETS].set(tail)
    fill_lo, fill_hi = pad32(starts + cnt, total_tiles), pad32(ends, ntile)
    xs_sorted = _scatter_rows(hrow, pos, fill_lo, fill_hi, n, npad)
    ys = _moe(xs_sorted, w1, w3, w2, layer, tile_ea, tile_eb, tile_blk, tile_valid)
    return _combine(x_mid, mod, ys, pos, t0, nct)


def kernel(x, c, ctx, c_ctx, router_w, router_b, norm_mix, norm_ffn, w_mod, b_mod, ev_w_in, ev_conv_w, ev_conv_b, ev_dt_bias, ev_a_log, ev_d_skip, ev_ssd_norm, ev_q_norm, ev_k_norm, ev_sink, ev_w_out, od_w_in, od_conv_w, od_conv_b, od_igate_b, od_fgate_b, od_head_norm, od_w_out, moe_w1, moe_w3, moe_w2):
    s_len = x.shape[1]
    c_len = ctx.shape[1]
    assert x.shape[0] == 1 and s_len % TM == 0 and c_len % TM == 0 and s_len % GRID_W == 0
    nct = c_len // TM
    ncc, nlc = c_len // T, s_len // T
    nt = c_len + s_len
    ntiles = nt // TM

    mod = _modulation(c, c_ctx, w_mod, b_mod)
    rw, rw_hi = _hi_lo_weight(router_w)
    rb_col = router_b.reshape(N_EXPERTS, 1)
    pad128 = lambda v: jnp.zeros((1, 128), F32).at[0, :v.shape[0]].set(v)

    w = ev_w_in[0]
    cos, sin = _rope_tables(s_len)
    wdt, wdt_hi = _hi_lo_weight(w[:, 1280:1296])
    p0 = dict(
        g_mix=norm_mix[0].reshape(1, D),
        wcat=jnp.concatenate([w[:, 0:1280].astype(BF16), w[:, 1296:2064].astype(BF16), wdt], axis=1),
        wdt_hi=wdt_hi,
        conv_w=ev_conv_w[0], conv_b=ev_conv_b[0].reshape(1, 768),
        dt_bias=pad128(ev_dt_bias[0].reshape(16)),
        q_norm=jnp.tile(ev_q_norm[0], 8).reshape(1, 512), k_norm=jnp.tile(ev_k_norm[0], 2).reshape(1, 128),
        cos=cos, sin=sin,
        alog_row=ev_a_log[0].reshape(1, 16), alog_col=ev_a_log[0].reshape(16, 1),
        d_skip=jnp.repeat(ev_d_skip[0], 64).reshape(1, 512), ssd_norm=ev_ssd_norm[0].reshape(1, 512))
    z, xs, bc, q, kk, vv, dtc, dtr = _in0(ctx[0], x[0], mod[0], p0, nct)
    yf = _ssd(False, xs, bc, dtc, dtr, p0, ncc, nlc)
    ymix = _ssd(True, xs, bc, dtc, dtr, p0, ncc, nlc, yf=yf, z=z)
    att = _attention(q, kk, vv, ev_sink[0].reshape(1, 8), ncc, nlc)
    x_mid0, hrow, bucket, rank, counts = _out_proj(
        [ymix, att], ev_w_out[0].astype(BF16), [ctx[0], x[0]], mod[0], norm_ffn[0].reshape(1, D),
        rw, rw_hi, rb_col, 0, ntiles, nct)
    x1 = _moe_block(x_mid0, hrow, bucket, rank, counts, mod[0], moe_w1, moe_w3, moe_w2, 0, 0, nct)

    w = od_w_in[0]
    wg, wg_hi = _hi_lo_weight(w[:, 3072:3104])
    p1 = dict(
        g_mix=norm_mix[1].reshape(1, D),
        wcat=jnp.concatenate([w[:, 0:3072].astype(BF16), wg], axis=1),
        wg_hi=wg_hi,
        conv_w=od_conv_w[0], conv_b=od_conv_b[0].reshape(1, 2048),
        gate_bias=pad128(jnp.concatenate([od_igate_b[0].reshape(16), od_fgate_b[0].reshape(16)])),
        head_norm=od_head_norm[0].reshape(1, 1024))
    q1, kt1, v1, og1, gc1, gr1 = _in1(x1, mod[1], p1, nct)
    hf = _mlstm(False, q1, kt1, v1, gc1, gr1, p1, ncc, nlc)
    hmix = _mlstm(True, q1, kt1, v1, gc1, gr1, p1, ncc, nlc, hf=hf, og=og1)
    x_mid1, hrow, bucket, rank, counts = _out_proj(
        [hmix], od_w_out[0].astype(BF16), [x1], mod[1], norm_ffn[1].reshape(1, D),
        rw, rw_hi, rb_col, nct, ntiles - nct, nct)
    return _moe_block(x_mid1, hrow, bucket, rank, counts, mod[1], moe_w1, moe_w3, moe_w2, 1, nct, nct)[None]
```

```python
import functools
import math

import jax
import jax.numpy as jnp
from jax import lax
from jax.experimental import pallas as pl
from jax.experimental.pallas import tpu as pltpu

F32 = jnp.float32
BF16 = jnp.bfloat16
I32 = jnp.int32

EPS = 1e-6
D = 1024
T = 128
TM = 256
TMM = 256
GRID_W = 64
ROPE_THETA = 10000.0
N_EXPERTS = 16
N_BUCKETS = 24
NB_PAD = 32
D_EXPERT = 512
TOK = 8
U32 = jnp.uint32
NEG_INF = float("-inf")
VMEM_LIMIT = 56 * 1024 * 1024

_NN = (((1,), (0,)), ((), ()))
_NT = (((1,), (1,)), ((), ()))
_TN = (((0,), (0,)), ((), ()))

_PAIR_A = (0, 0, 0, 1, 1, 2)
_PAIR_B = (1, 2, 3, 2, 3, 3)


def _cparams(*sem):
    return pltpu.CompilerParams(dimension_semantics=sem, vmem_limit_bytes=VMEM_LIMIT)


def _dot(a, b, dims=_NN):
    return lax.dot_general(a, b, dims, preferred_element_type=F32)


def _split(a, n):
    out = []
    r = a
    for _ in range(n):
        t = r.astype(BF16)
        out.append(t)
        r = r - t.astype(F32)
    return out


def _mdot(as_, bs, dims=_NN, order=None):
    if order is None:
        order = len(as_) + len(bs) - 2
    acc = None
    for i, a in enumerate(as_):
        for j, b in enumerate(bs):
            if i + j <= order:
                p = _dot(a, b, dims)
                acc = p if acc is None else acc + p
    return acc


def _sigmoid(x):
    return 1.0 / (1.0 + jnp.exp(-x))


def _silu(x):
    return x * _sigmoid(x)


def _log1p_exp_neg_abs(x):
    e = jnp.exp(-jnp.abs(x))
    u = 1.0 + e
    um1 = u - 1.0
    return jnp.where(um1 == 0.0, e, jnp.log(u) * (e / jnp.where(um1 == 0.0, 1.0, um1)))


def _softplus(x):
    return jnp.maximum(x, 0.0) + _log1p_exp_neg_abs(x)


def _log_sigmoid(x):
    return jnp.minimum(x, 0.0) - _log1p_exp_neg_abs(x)


def _norm_mod(x, g, sc, sh):
    ms = jnp.mean(x * x, axis=-1, keepdims=True)
    return (x * lax.rsqrt(ms + EPS)) * g * (1.0 + sc) + sh


def _tok_load(ref, chunk, n):
    return ref[pl.ds(chunk, n, stride=TOK), :]


def _tok_store(ref, chunk, val):
    ref[pl.ds(chunk, val.shape[0], stride=TOK), :] = val


def _tok_rows(ref, n):
    return jnp.concatenate([_tok_load(ref, c, n) for c in range(TOK)], axis=1)


def _tri(rev):
    r = lax.broadcasted_iota(I32, (T, T), 0)
    c = lax.broadcasted_iota(I32, (T, T), 1)
    return (c >= r) if rev else (c <= r)


def _cumsums(rev, col, row):
    tri = _tri(rev)
    tri_b = tri.astype(F32).astype(BF16)
    trit_b = _tri(not rev).astype(F32).astype(BF16)
    ccol = _mdot([tri_b], _split(col, 3))
    crow = _mdot(_split(row, 3), [trit_b])
    return tri, ccol, crow


def _lane_bcast(col):
    n = col.shape[1]
    col = jnp.concatenate([col, jnp.zeros((col.shape[0], 128 - n), F32)], axis=1)
    r = lax.broadcasted_iota(I32, (128, n * 128), 0)
    c = lax.broadcasted_iota(I32, (128, n * 128), 1) // 128
    return _mdot(_split(col, 3), [(r == c).astype(F32).astype(BF16)])


def _mod_kernel(c_ref, w_ref, b_ref, o_ref):
    a = _silu(c_ref[...])
    o_ref[0] = _mdot(_split(a, 2), _split(w_ref[0], 2), order=1) + b_ref[0]


def _modulation(c, c_ctx, w_mod, b_mod):
    depth = w_mod.shape[0]
    n = w_mod.shape[2]
    tn = 1536
    cc = jnp.zeros((8, D), F32).at[0].set(c[0]).at[1].set(c_ctx)
    out = pl.pallas_call(
        _mod_kernel,
        grid=(depth, n // tn),
        in_specs=[
            pl.BlockSpec((8, D), lambda l, j: (0, 0)),
            pl.BlockSpec((1, D, tn), lambda l, j: (l, 0, j)),
            pl.BlockSpec((1, 1, tn), lambda l, j: (l, 0, j)),
        ],
        out_specs=pl.BlockSpec((1, 8, tn), lambda l, j: (l, 0, j)),
        out_shape=jax.ShapeDtypeStruct((depth, 8, n), F32),
        compiler_params=_cparams("arbitrary", "arbitrary"),
        name="modulation",
    )(cc, w_mod, b_mod.reshape(depth, 1, n))
    return out[:, :2].reshape(depth, 2, 6, D)


def _halo_specs(nrows, tile_of):
    nb8 = nrows // 8
    return [
        pl.BlockSpec((8, D), lambda i: (jnp.maximum(tile_of(i) * (TM // 8) - 1, 0), 0)),
        pl.BlockSpec((TM, D), lambda i: (tile_of(i), 0)),
        pl.BlockSpec((8, D), lambda i: (jnp.minimum((tile_of(i) + 1) * (TM // 8), nb8 - 1), 0)),
    ]


def _ctx_tile(nct):
    return lambda i: jnp.minimum(i, nct - 1)


def _lat_tile(nct):
    return lambda i: jnp.maximum(i - nct, 0)


def _mod_spec(nct, t0=0):
    return pl.BlockSpec((1, 6, D), lambda i: (jnp.where(i + t0 < nct, 1, 0), 0, 0))


def _full(shape):
    nd = len(shape)
    return pl.BlockSpec(shape, lambda i: (0,) * nd)


def _seq_edges(i, nct, ntiles):
    prev_ok = jnp.logical_and(i != 0, i != nct).astype(F32)
    next_ok = jnp.logical_and(i != nct - 1, i != ntiles - 1).astype(F32)
    return prev_ok, next_ok


def _conv_silu(x, x_first_prev, x_last_next, cw, cb):
    n = x.shape[0]
    rows = lax.broadcasted_iota(I32, x.shape, 0)
    x_prev = jnp.where(rows == 0, x_first_prev, pltpu.roll(x, 1, 0))
    x_next = jnp.where(rows == n - 1, x_last_next, pltpu.roll(x, n - 1, 0))
    return _silu(x_prev * cw[0:1] + x * cw[1:2] + x_next * cw[2:3] + cb)


def _hi_lo_cols(blk, lo_pass, n):
    return blk + pltpu.roll(blk, 128 - n, 1) + lo_pass


def _hi_lo_weight(w):
    n = w.shape[1]
    hi = w.astype(BF16)
    lo = (w - hi.astype(F32)).astype(BF16)
    z = jnp.zeros((w.shape[0], 128 - 2 * n), BF16)
    return jnp.concatenate([hi, lo, z], axis=1), jnp.concatenate([hi, jnp.zeros_like(lo), z], axis=1)


def _head_rms(xf, gamma):
    r = lax.broadcasted_iota(I32, (128, 128), 0) // 64
    c = lax.broadcasted_iota(I32, (128, 128), 1) // 64
    ones_bd = (r == c).astype(F32).astype(BF16)
    outs = []
    for j in range(xf.shape[1] // 128):
        blk = xf[:, 128 * j:128 * (j + 1)]
        ssum = _dot((blk * blk).astype(BF16), ones_bd)
        outs.append(blk * lax.rsqrt(ssum * (1.0 / 64.0) + EPS))
    return jnp.concatenate(outs, axis=1) * gamma


def _rope(xf, cos, sin):
    lane = lax.broadcasted_iota(I32, (xf.shape[0], 128), 1)
    first = (lane % 32) < 16
    outs = []
    for j in range(xf.shape[1] // 128):
        blk = xf[:, 128 * j:128 * (j + 1)]
        partner = jnp.where(first, pltpu.roll(blk, 112, 1), pltpu.roll(blk, 16, 1))
        outs.append(blk * cos + partner * sin)
    return jnp.concatenate(outs, axis=1)


def _rope_tables(s_len):
    rows = s_len // GRID_W
    inv = ROPE_THETA ** (-(jnp.arange(32, dtype=I32) % 16).astype(F32) / 16.0)
    sign = jnp.where(jnp.arange(32) < 16, -1.0, 1.0).astype(F32)
    ang_r = jnp.arange(rows, dtype=F32)[:, None] * inv[None, :]
    ang_c = jnp.arange(GRID_W, dtype=F32)[:, None] * inv[None, :]

    def table(fr, fc):
        r = jnp.broadcast_to(fr[:, None, :], (rows, GRID_W, 32))
        c = jnp.broadcast_to(fc[None, :, :], (rows, GRID_W, 32))
        t = jnp.concatenate([r, c, r, c], axis=-1)
        return t.reshape(s_len, 128)

    return table(jnp.cos(ang_r), jnp.cos(ang_c)), table(jnp.sin(ang_r) * sign, jnp.sin(ang_c) * sign)


def _in0_kernel(nct, ntiles, cp_ref, c_ref, cn_ref, xp_ref, x_ref, xn_ref, mod_ref, g_ref, wcat_ref, wdth_ref,
                cw_ref, cb_ref, dtb_ref, qn_ref, kn_ref, cos_ref, sin_ref,
                z_ref, xs_ref, bc_ref, q_ref, kk_ref, vv_ref, dtc_ref, dtr_ref):
    i = pl.program_id(0)
    is_ctx = i < nct
    sh = mod_ref[0, 0:1, :]
    sc = mod_ref[0, 1:2, :]
    g = g_ref[...]
    x_all = jnp.concatenate([jnp.where(is_ctx, cp_ref[...], xp_ref[...]),
                             jnp.where(is_ctx, c_ref[...], x_ref[...]),
                             jnp.where(is_ctx, cn_ref[...], xn_ref[...])], axis=0)
    h_all = _norm_mod(x_all, g, sc, sh)
    hb_all = h_all.astype(BF16)
    main_all = _dot(hb_all, wcat_ref[...])
    h, hb, main = h_all[8:8 + TM], hb_all[8:8 + TM], main_all[8:8 + TM]
    prev_ok, next_ok = _seq_edges(i, nct, ntiles)
    xb_prev = main_all[7:8, 512:1280] * prev_ok
    xb_next = main_all[8 + TM:9 + TM, 512:1280] * next_ok
    act = _conv_silu(main[:, 512:1280], xb_prev, xb_next, cw_ref[...], cb_ref[...])
    z_ref[...] = main[:, 0:512].astype(BF16)
    xs_ref[...] = act[:, 0:512].astype(BF16)
    bc_ref[...] = act[:, 512:768].astype(BF16)
    cos = jnp.where(is_ctx, 1.0, cos_ref[...])
    sin = jnp.where(is_ctx, 0.0, sin_ref[...])
    q = _rope(_head_rms(main[:, 1280:1792], qn_ref[...]), cos, sin) * 0.125
    q_ref[...] = q.astype(BF16)
    k = _rope(_head_rms(main[:, 1792:1920], kn_ref[...]), cos, sin)
    kk_ref[...] = jnp.concatenate([k, pltpu.roll(k, 64, 1)], axis=1).astype(BF16)
    v = main[:, 1920:2048]
    vv_ref[...] = jnp.concatenate([v, pltpu.roll(v, 64, 1)], axis=1).astype(BF16)
    h_lo = (h - hb.astype(F32)).astype(BF16)
    dt = _softplus(_hi_lo_cols(main[:, 2048:2176], _dot(h_lo, wdth_ref[...]), 16) + dtb_ref[...])
    dtc_ref[...] = dt[:, 0:16]
    dtr_ref[...] = jnp.transpose(dt)[0:16, :]


def _in0(ctx2, x2, mod, p, nct):
    c_len, s_len = ctx2.shape[0], x2.shape[0]
    nt = c_len + s_len
    ntiles = nt // TM
    tile = lambda w: pl.BlockSpec((TM, w), lambda i: (i, 0))
    lat = _lat_tile(nct)
    lat_tile = lambda w: pl.BlockSpec((TM, w), lambda i: (lat(i), 0))
    outs = [(512, BF16), (512, BF16), (256, BF16), (512, BF16), (256, BF16), (256, BF16), (16, F32)]
    return pl.pallas_call(
        functools.partial(_in0_kernel, nct, ntiles),
        grid=(ntiles,),
        in_specs=_halo_specs(c_len, _ctx_tile(nct)) + _halo_specs(s_len, lat) + [
            _mod_spec(nct), _full((1, D)), _full((D, 2176)), _full((D, 128)),
            _full((3, 768)), _full((1, 768)), _full((1, 128)),
            _full((1, 512)), _full((1, 128)), lat_tile(128), lat_tile(128)],
        out_specs=[tile(w) for w, _ in outs] + [pl.BlockSpec((16, TM), lambda i: (0, i))],
        out_shape=[jax.ShapeDtypeStruct((nt, w), dt) for w, dt in outs]
        + [jax.ShapeDtypeStruct((16, nt), F32)],
        compiler_params=_cparams("arbitrary"),
        name="in_proj_even",
    )(ctx2, ctx2, ctx2, x2, x2, x2, mod, p["g_mix"], p["wcat"], p["wdt_hi"], p["conv_w"], p["conv_b"],
      p["dt_bias"], p["q_norm"], p["k_norm"], p["cos"], p["sin"])


def _scan_chunk_map(rev, ncc, nlc):
    if not rev:
        return lambda j: j
    return lambda j: jnp.where(j < ncc, ncc - 1 - j, ncc + nlc - 1 - (j - ncc))


def _ssd_kernel(rev, *refs):
    if rev:
        (xs_ref, bc_ref, dtc_ref, dtr_ref, alr_ref, alc_ref, yf_ref, z_ref, dsk_ref, nrm_ref,
         o_ref, st_ref) = refs
    else:
        xs_ref, bc_ref, dtc_ref, dtr_ref, alr_ref, alc_ref, o_ref, st_ref = refs
    j = pl.program_id(0)

    @pl.when(j == 0)
    def _():
        st_ref[...] = jnp.zeros_like(st_ref)

    d = 8 if rev else 0
    a_coef_row = -jnp.exp(alr_ref[...])[:, d:d + 8]
    a_coef_col = -jnp.exp(alc_ref[...])[d:d + 8, :]
    dtc = dtc_ref[:, d:d + 8]
    dtr = dtr_ref[d:d + 8, :]
    tri, acs_col, acs_row = _cumsums(rev, dtc * a_coef_row, dtr * a_coef_col)
    end = 0 if rev else T - 1
    atot_col = acs_row[:, end:end + 1]
    acs_bc = _lane_bcast(acs_col)
    dec_row = jnp.exp(atot_col - acs_row) * dtr
    xs = xs_ref[...]
    bm = bc_ref[:, 0:128]
    cm = bc_ref[:, 128:256]
    lane = lax.broadcasted_iota(I32, (T, 128), 1)
    lo = lane < 64
    zero_b = jnp.zeros((T, 128), BF16)
    hi_half = jnp.logical_not(lo)
    cgs = [jnp.where(lo, cm, zero_b), jnp.where(hi_half, cm, zero_b)]
    cbs = [_dot(cgs[g], bm, _NT) for g in range(2)]
    sub = lax.broadcasted_iota(I32, (128, T), 0)
    eye = (sub == lax.broadcasted_iota(I32, (128, T), 1)).astype(F32).astype(BF16)
    bmt = _dot(eye, bm, _NT)
    bgts = [jnp.where(sub < 64, bmt, 0.0), jnp.where(sub >= 64, bmt, 0.0)]
    xpairs = []
    for pr in range(4):
        xp = xs[:, 128 * pr:128 * (pr + 1)]
        xpairs.append(jnp.concatenate([jnp.where(lo, xp, zero_b), jnp.where(hi_half, xp, zero_b)], axis=0))
    y_off = []
    for pr in range(4):
        g, h0, h1 = pr // 2, 2 * pr, 2 * pr + 1
        st = st_ref[pr]
        eacs = jnp.exp(jnp.where(lo, acs_bc[:, 128 * h0:128 * (h0 + 1)], acs_bc[:, 128 * h1:128 * (h1 + 1)]))
        y_off.append(_dot(cgs[g], st.astype(BF16)) * eacs)
        bdec = jnp.concatenate([(bgts[g] * dec_row[h0:h0 + 1, :]).astype(BF16),
                                (bgts[g] * dec_row[h1:h1 + 1, :]).astype(BF16)], axis=1)
        carry = jnp.where(lo[0:1, :], jnp.exp(atot_col[h0:h0 + 1, :]), jnp.exp(atot_col[h1:h1 + 1, :]))
        st_ref[pr] = carry * st + _dot(bdec, xpairs[pr])
    ys = []
    for pr in range(4):
        ms = []
        for hd in (2 * pr, 2 * pr + 1):
            diff = acs_bc[:, 128 * hd:128 * (hd + 1)] - acs_row[hd:hd + 1, :]
            lmat = jnp.exp(jnp.where(tri, diff, NEG_INF))
            ms.append((cbs[hd // 4] * lmat * dtr[hd:hd + 1, :]).astype(BF16))
        ys.append(_dot(jnp.concatenate(ms, axis=1), xpairs[pr]) + y_off[pr])
    y = jnp.concatenate(ys, axis=1)
    if not rev:
        o_ref[...] = y
    else:
        ytot = y + yf_ref[...] + dsk_ref[...] * xs.astype(F32)
        gated = ytot * _silu(z_ref[...].astype(F32))
        ms = jnp.mean(gated * gated, axis=-1, keepdims=True)
        o_ref[...] = (gated * lax.rsqrt(ms + EPS) * nrm_ref[...]).astype(BF16)


def _ssd(rev, xs, bc, dtc, dtr, p, ncc, nlc, yf=None, z=None):
    nt = xs.shape[0]
    cmap = _scan_chunk_map(rev, ncc, nlc)
    blk = lambda w: pl.BlockSpec((T, w), lambda j: (cmap(j), 0))
    in_specs = [blk(512), blk(256), blk(16), pl.BlockSpec((16, T), lambda j: (0, cmap(j))),
                _full((1, 16)), _full((16, 1))]
    args = [xs, bc, dtc, dtr, p["alog_row"], p["alog_col"]]
    if rev:
        in_specs += [blk(512), blk(512), _full((1, 512)), _full((1, 512))]
        args += [yf, z, p["d_skip"], p["ssd_norm"]]
    return pl.pallas_call(
        functools.partial(_ssd_kernel, rev),
        grid=(ncc + nlc,),
        in_specs=in_specs,
        out_specs=blk(512),
        out_shape=jax.ShapeDtypeStruct((nt, 512), BF16 if rev else F32),
        scratch_shapes=[pltpu.VMEM((4, 128, 128), F32)],
        compiler_params=_cparams("arbitrary"),
        name="ssd_bwd" if rev else "ssd_fwd",
    )(*args)


def _attn_kernel(ncc, nblk, q_ref, kp_ref, kc_ref, kn_ref, vp_ref, vc_ref, vn_ref, kx_ref, vx_ref,
                 sink_ref, o_ref):
    j = pl.program_id(0)
    c_len = kx_ref.shape[0]
    is_lat = j >= ncc
    prev_ok = jnp.logical_and(is_lat, j >= ncc + 1)
    next_ok = jnp.logical_and(is_lat, j <= nblk - 2)
    r = lax.broadcasted_iota(I32, (T, T), 0)
    c = lax.broadcasted_iota(I32, (T, T), 1)
    zero = jnp.zeros((T, T), F32)
    ninf = jnp.full((T, T), NEG_INF, F32)
    bias = jnp.concatenate([
        jnp.where(jnp.logical_and(prev_ok, c >= r), zero, ninf),
        jnp.where(is_lat, zero, ninf),
        jnp.where(jnp.logical_and(next_ok, c <= r), zero, ninf),
        jnp.zeros((T, c_len), F32)], axis=1)
    k_all = jnp.concatenate([kp_ref[...], kc_ref[...], kn_ref[...], kx_ref[...]], axis=0)
    v_all = jnp.concatenate([vp_ref[...], vc_ref[...], vn_ref[...], vx_ref[...]], axis=0)
    lo = lax.broadcasted_iota(I32, (T, 128), 1) < 64
    zero_b = jnp.zeros((T, 128), BF16)
    q = q_ref[...]
    sink = sink_ref[...]
    bias4 = jnp.concatenate([bias] * 4, axis=0)
    outs = {}
    stacks = [[hd for hd in range(8) if (hd // 4 + hd % 2) % 2 == b] for b in range(2)]
    s_all = []
    for b in range(2):
        qs = []
        for hd in stacks[b]:
            qp = q[:, 128 * (hd // 2):128 * (hd // 2 + 1)]
            qs.append(jnp.where(lo, zero_b, qp) if hd % 2 else jnp.where(lo, qp, zero_b))
        s_all.append(_dot(jnp.concatenate(qs, axis=0), k_all[:, 128 * b:128 * (b + 1)], _NT) + bias4)
    for b in range(2):
        s = s_all[b]
        sk = jnp.concatenate([jnp.broadcast_to(sink[:, hd:hd + 1], (T, 1)) for hd in stacks[b]], axis=0)
        m = jnp.maximum(jnp.max(s, axis=-1, keepdims=True), sk)
        pr = jnp.exp(s - m)
        den = jnp.sum(pr, axis=-1, keepdims=True) + jnp.exp(sk - m)
        o = _dot(pr.astype(BF16), v_all[:, 128 * b:128 * (b + 1)]) / den
        for n, hd in enumerate(stacks[b]):
            outs[hd] = o[T * n:T * (n + 1)]
    for pair in range(4):
        o_ref[:, 128 * pair:128 * (pair + 1)] = jnp.where(lo, outs[2 * pair], outs[2 * pair + 1]).astype(BF16)


def _attention(q, kk, vv, sink, ncc, nlc):
    nt = q.shape[0]
    nblk = ncc + nlc
    c_len = ncc * T
    prev = lambda w: pl.BlockSpec((T, w), lambda j: (jnp.maximum(j - 1, 0), 0))
    cur = lambda w: pl.BlockSpec((T, w), lambda j: (j, 0))
    nxt = lambda w: pl.BlockSpec((T, w), lambda j: (jnp.minimum(j + 1, nblk - 1), 0))
    ctx = lambda w: pl.BlockSpec((c_len, w), lambda j: (0, 0))
    return pl.pallas_call(
        functools.partial(_attn_kernel, ncc, nblk),
        grid=(nblk,),
        in_specs=[cur(512), prev(256), cur(256), nxt(256), prev(256), cur(256), nxt(256),
                  ctx(256), ctx(256), _full((1, 8))],
        out_specs=cur(512),
        out_shape=jax.ShapeDtypeStruct((nt, 512), BF16),
        compiler_params=_cparams("arbitrary"),
        name="window_attention",
    )(q, kk, kk, kk, vv, vv, vv, kk, vv, sink)


def _in1_kernel(nct, ntiles, xp_ref, x_ref, xn_ref, mod_ref, g_ref,
                wcat_ref, wgh_ref, cw_ref, cb_ref, gb_ref,
                q_ref, kt_ref, v_ref, o_ref, gc_ref, gr_ref):
    i = pl.program_id(0)
    sh = mod_ref[0, 0:1, :]
    sc = mod_ref[0, 1:2, :]
    g = g_ref[...]
    x_all = jnp.concatenate([xp_ref[...], x_ref[...], xn_ref[...]], axis=0)
    h_all = _norm_mod(x_all, g, sc, sh)
    hb_all = h_all.astype(BF16)
    main_all = _dot(hb_all, wcat_ref[...])
    h, hb, main = h_all[8:8 + TM], hb_all[8:8 + TM], main_all[8:8 + TM]
    prev_ok, next_ok = _seq_edges(i, nct, ntiles)
    x_prev = main_all[7:8, 0:2048] * prev_ok
    x_next = main_all[8 + TM:9 + TM, 0:2048] * next_ok
    act = _conv_silu(main[:, 0:2048], x_prev, x_next, cw_ref[...], cb_ref[...])
    q_ref[...] = act[:, 0:512].astype(BF16)
    kt_ref[...] = jnp.transpose(act[:, 512:1024] * 0.125).astype(BF16)
    v_ref[...] = act[:, 1024:2048].astype(BF16)
    o_ref[...] = main[:, 2048:3072].astype(BF16)
    h_lo = (h - hb.astype(F32)).astype(BF16)
    gates = _hi_lo_cols(main[:, 3072:3200], _dot(h_lo, wgh_ref[...]), 32) + gb_ref[...]
    lane = lax.broadcasted_iota(I32, gates.shape, 1)
    gates = jnp.where(lane < 16, gates, _log_sigmoid(gates))
    gc_ref[...] = gates[:, 0:32]
    gr_ref[...] = jnp.transpose(gates)[0:32, :]


def _in1(x1, mod, p, nct):
    nt = x1.shape[0]
    ntiles = nt // TM
    tile = lambda w: pl.BlockSpec((TM, w), lambda i: (i, 0))
    return pl.pallas_call(
        functools.partial(_in1_kernel, nct, ntiles),
        grid=(ntiles,),
        in_specs=_halo_specs(nt, lambda i: i) + [
            _mod_spec(nct), _full((1, D)), _full((D, 3200)), _full((D, 128)),
            _full((3, 2048)), _full((1, 2048)), _full((1, 128))],
        out_specs=[tile(512), pl.BlockSpec((512, TM), lambda i: (0, i)), tile(1024), tile(1024),
                   tile(32), pl.BlockSpec((32, TM), lambda i: (0, i))],
        out_shape=[jax.ShapeDtypeStruct((nt, 512), BF16), jax.ShapeDtypeStruct((512, nt), BF16),
                   jax.ShapeDtypeStruct((nt, 1024), BF16), jax.ShapeDtypeStruct((nt, 1024), BF16),
                   jax.ShapeDtypeStruct((nt, 32), F32), jax.ShapeDtypeStruct((32, nt), F32)],
        compiler_params=_cparams("arbitrary"),
        name="in_proj_odd",
    )(x1, x1, x1, mod, p["g_mix"], p["wcat"], p["wg_hi"], p["conv_w"], p["conv_b"], p["gate_bias"])


def _mlstm_kernel(rev, *refs):
    if rev:
        (q_ref, kt_ref, v_ref, gc_ref, gr_ref, hf_ref, og_ref, hn_ref, o_ref,
         c_ref, mc_ref, mr_ref) = refs
    else:
        q_ref, kt_ref, v_ref, gc_ref, gr_ref, o_ref, c_ref, mc_ref, mr_ref = refs
    j = pl.program_id(0)

    @pl.when(j == 0)
    def _():
        c_ref[...] = jnp.zeros_like(c_ref)
        mc_ref[...] = jnp.zeros_like(mc_ref)
        mr_ref[...] = jnp.zeros_like(mr_ref)

    d = 8 if rev else 0
    ig_col = gc_ref[:, d:d + 8]
    lf_col = gc_ref[:, 16 + d:24 + d]
    ig_row = gr_ref[d:d + 8, :]
    lf_row = gr_ref[16 + d:24 + d, :]
    tri, b_col, b_row = _cumsums(rev, lf_col, lf_row)
    end = 0 if rev else T - 1
    blast_row = b_col[end:end + 1, :]
    blast_col = b_row[:, end:end + 1]
    wend_row = blast_col - b_row + ig_row
    ac_col = jnp.max(wend_row, axis=1, keepdims=True)
    eend_row = jnp.exp(wend_row - ac_col)
    ac_row = jnp.max(blast_row - b_col + ig_col, axis=0, keepdims=True)
    m_col = mc_ref[:, 0:1]
    m_row = mr_ref[0:1, 0:8]
    mnew_col = jnp.maximum(blast_col + m_col, ac_col)
    sp_col = jnp.exp(blast_col + m_col - mnew_col)
    sc_col = jnp.exp(ac_col - mnew_col)
    mnew_row = jnp.maximum(blast_row + m_row, ac_row)
    b_bc = _lane_bcast(b_col)
    q = q_ref[...]
    ones_b = jnp.ones((T, 128), BF16)
    sub = lax.broadcasted_iota(I32, (128, T), 0)
    zero_k = jnp.zeros((128, T), BF16)
    def head_matmuls(hd):
        pair, hi = hd // 2, hd % 2
        qp = q[:, 128 * pair:128 * (pair + 1)]
        ktp = kt_ref[128 * pair:128 * (pair + 1), :]
        kth = jnp.where((sub >= 64) if hi else (sub < 64), ktp, zero_k)
        vaug = jnp.concatenate([v_ref[:, 128 * hd:128 * (hd + 1)], ones_b], axis=1)
        cst = c_ref[hd]
        sqk = _dot(qp, kth)
        inter = _dot(qp, cst.astype(BF16))
        kte = (kth.astype(F32) * eend_row[hd:hd + 1, :]).astype(BF16)
        c_ref[hd] = sp_col[hd:hd + 1, :] * cst + sc_col[hd:hd + 1, :] * _dot(kte, vaug)
        return sqk, inter, vaug

    nxt = head_matmuls(0)
    for hd in range(8):
        sqk, inter, vaug = nxt
        if hd + 1 < 8:
            nxt = head_matmuls(hd + 1)
        bh = b_bc[:, 128 * hd:128 * (hd + 1)]
        dlog = jnp.where(tri, bh - b_row[hd:hd + 1, :] + ig_row[hd:hd + 1, :], NEG_INF)
        gh = bh + mr_ref[0:1, hd:hd + 1]
        mstar = jnp.maximum(gh, jnp.max(dlog, axis=-1, keepdims=True))
        w = (jnp.exp(dlog - mstar) * sqk).astype(BF16)
        intra = _dot(w, vaug)
        e_int = jnp.exp(gh - mstar)
        den = jnp.maximum(jnp.abs(intra[:, 128:256] + e_int * inter[:, 128:256]), jnp.exp(-mstar))
        hh = (intra[:, 0:128] + e_int * inter[:, 0:128]) / den
        if rev:
            hh = hh + hf_ref[:, 128 * hd:128 * (hd + 1)]
            ms = jnp.mean(hh * hh, axis=-1, keepdims=True)
            hh = hh * lax.rsqrt(ms + EPS) * hn_ref[:, 128 * hd:128 * (hd + 1)]
            og = og_ref[:, 128 * hd:128 * (hd + 1)].astype(F32)
            o_ref[:, 128 * hd:128 * (hd + 1)] = (hh * _sigmoid(og)).astype(BF16)
        else:
            o_ref[:, 128 * hd:128 * (hd + 1)] = hh
    mc_ref[...] = jnp.broadcast_to(mnew_col, mc_ref.shape)
    mr_ref[...] = jnp.broadcast_to(jnp.concatenate([mnew_row, jnp.zeros((1, 120), F32)], axis=1), mr_ref.shape)


def _mlstm(rev, q, kt, v, gc, gr, p, ncc, nlc, hf=None, og=None):
    nt = q.shape[0]
    cmap = _scan_chunk_map(rev, ncc, nlc)
    blk = lambda w: pl.BlockSpec((T, w), lambda j: (cmap(j), 0))
    blk_t = lambda h: pl.BlockSpec((h, T), lambda j: (0, cmap(j)))
    in_specs = [blk(512), blk_t(512), blk(1024), blk(32), blk_t(32)]
    args = [q, kt, v, gc, gr]
    if rev:
        in_specs += [blk(1024), blk(1024), _full((1, 1024))]
        args += [hf, og, p["head_norm"]]
    return pl.pallas_call(
        functools.partial(_mlstm_kernel, rev),
        grid=(ncc + nlc,),
        in_specs=in_specs,
        out_specs=blk(1024),
        out_shape=jax.ShapeDtypeStruct((nt, 1024), BF16 if rev else F32),
        scratch_shapes=[pltpu.VMEM((8, 128, 256), F32), pltpu.VMEM((8, 128), F32), pltpu.VMEM((8, 128), F32)],
        compiler_params=_cparams("arbitrary"),
        name="mlstm_bwd" if rev else "mlstm_fwd",
    )(*args)


def _route(logits_t, rb_col):
    scores = _sigmoid(logits_t)
    biased = scores + rb_col
    row = lambda a, e: a[e:e + 1, :]
    gscore = []
    for g in range(4):
        b0, b1, b2, b3 = (row(biased, 4 * g + e) for e in range(4))
        h1, l1 = jnp.maximum(b0, b1), jnp.minimum(b0, b1)
        h2, l2 = jnp.maximum(b2, b3), jnp.minimum(b2, b3)
        gscore.append(jnp.maximum(h1, h2) + jnp.maximum(jnp.minimum(h1, h2), jnp.maximum(l1, l2)))
    gidx = jnp.zeros_like(gscore[0], dtype=I32)
    best = gscore[0]
    for g in range(1, 4):
        better = gscore[g] > best
        gidx = jnp.where(better, g, gidx)
        best = jnp.where(better, gscore[g], best)

    def pick(a, e):
        out = row(a, e)
        for g in range(1, 4):
            out = jnp.where(gidx == g, row(a, 4 * g + e), out)
        return out

    sb = [pick(biased, e) for e in range(4)]
    ss = [pick(scores, e) for e in range(4)]
    i1 = jnp.zeros_like(gidx)
    v1, s1 = sb[0], ss[0]
    for e in range(1, 4):
        better = sb[e] > v1
        i1 = jnp.where(better, e, i1)
        v1 = jnp.where(better, sb[e], v1)
        s1 = jnp.where(better, ss[e], s1)
    i2 = jnp.zeros_like(gidx)
    v2 = jnp.full_like(v1, NEG_INF)
    s2 = jnp.zeros_like(s1)
    for e in range(4):
        better = jnp.logical_and(i1 != e, sb[e] > v2)
        i2 = jnp.where(better, e, i2)
        v2 = jnp.where(better, sb[e], v2)
        s2 = jnp.where(better, ss[e], s2)
    tot = s1 + s2
    w1 = s1 / tot
    w2 = s2 / tot
    first_low = i1 < i2
    a = jnp.minimum(i1, i2)
    b = jnp.maximum(i1, i2)
    off = jnp.where(a == 0, 0, jnp.where(a == 1, 3, 5))
    bucket = 6 * gidx + off + (b - a - 1)
    return bucket, jnp.where(first_low, w1, w2), jnp.where(first_low, w2, w1)


def _out_kernel(nmix, nct, two_src, sub, t0, *refs):
    per = nmix + (2 if two_src else 1)
    tile_refs = [refs[per * u:per * (u + 1)] for u in range(sub)]
    (w_ref, mod_ref, g_ref, rw_ref, rwh_ref, rb_ref,
     xmid_ref, hrow_ref, bucket_ref, rank_ref, cnt_ref, cnt_scr) = refs[per * sub:]
    i = pl.program_id(0)

    @pl.when(i == 0)
    def _():
        cnt_scr[...] = jnp.zeros_like(cnt_scr)

    brow = lax.broadcasted_iota(I32, (NB_PAD, TM), 0)
    r = lax.broadcasted_iota(I32, (TM, TM), 0)
    c = lax.broadcasted_iota(I32, (TM, TM), 1)
    before = (r < c).astype(F32).astype(BF16)
    onehots = []
    for u in range(sub):
        mix_refs = tile_refs[u][:nmix]
        is_ctx = sub * i + u + t0 < nct
        mod = jnp.where(is_ctx, mod_ref[1], mod_ref[0])
        mix = mix_refs[0][...] if nmix == 1 else jnp.concatenate([mr[...] for mr in mix_refs], axis=1)
        if two_src:
            x = jnp.where(is_ctx, tile_refs[u][nmix][...], tile_refs[u][nmix + 1][...])
        else:
            x = tile_refs[u][nmix][...]
        x_mid = x + mod[2:3, :] * _dot(mix, w_ref[...])
        xmid_ref[TM * u:TM * (u + 1), :] = x_mid
        h = _norm_mod(x_mid, g_ref[...], mod[4:5, :], mod[3:4, :])
        hb = h.astype(BF16)
        h_lo = (h - hb.astype(F32)).astype(BF16)
        logits = _hi_lo_cols(_dot(hb, rw_ref[...]), _dot(h_lo, rwh_ref[...]), 16)
        logits_t = jnp.transpose(logits)[0:16, :]
        bucket, w_lo, w_hi = _route(logits_t, rb_ref[...])
        hrow_u = hrow_ref.at[pl.ds(TM * TOK * u, TM * TOK), :]
        bits = pltpu.bitcast(hb.astype(F32), U32)
        words = (bits[:, 512:1024] & jnp.uint32(0xFFFF0000)) | (bits[:, 0:512] >> 16)
        for cblk in range(4):
            _tok_store(hrow_u, cblk, words[:, 128 * cblk:128 * (cblk + 1)])
        wrows = jnp.concatenate([w_lo, w_hi, jnp.zeros((126, TM), F32)], axis=0)
        _tok_store(hrow_u, 4, pltpu.bitcast(jnp.transpose(wrows), U32))
        for cblk in range(5, TOK):
            _tok_store(hrow_u, cblk, jnp.zeros((TM, 128), U32))
        bucket_ref[u] = bucket
        onehots.append((brow == bucket).astype(F32))
    cnt = cnt_scr[...]
    for u in range(sub):
        onehot = onehots[u]
        cum = _dot(onehot.astype(BF16), before)
        rank_ref[u] = jnp.sum(onehot * (cum + cnt[:, 0:1]), axis=0, keepdims=True).astype(I32)
        cnt = cnt + jnp.sum(onehot, axis=1, keepdims=True)
    cnt_scr[...] = cnt
    cnt_ref[...] = cnt.astype(I32)


def _out_proj(mixes, w_out, xs, mod, g_ffn, rw, rw_hi, rb_col, t0, ntiles, nct):
    nmix = len(mixes)
    two_src = len(xs) == 2
    n = ntiles * TM
    sub = max(s for s in (5, 4, 3, 2, 1) if ntiles % s == 0)
    in_specs, args = [], []
    for u in range(sub):
        tile_of = lambda i, u=u: sub * i + u + t0
        for mx in mixes:
            in_specs.append(pl.BlockSpec((TM, mx.shape[1]), lambda i, f=tile_of: (f(i), 0)))
            args.append(mx)
        if two_src:
            in_specs += [pl.BlockSpec((TM, D), lambda i, f=tile_of: (jnp.minimum(f(i), nct - 1), 0)),
                         pl.BlockSpec((TM, D), lambda i, f=tile_of: (jnp.maximum(f(i) - nct, 0), 0))]
        else:
            in_specs.append(pl.BlockSpec((TM, D), lambda i, f=tile_of: (f(i), 0)))
        args += list(xs)
    in_specs += [_full((D, D)), _full((2, 6, D)), _full((1, D)), _full((D, 128)), _full((D, 128)), _full((16, 1))]
    rows_out = pl.BlockSpec((sub, 1, TM), lambda i: (i, 0, 0))
    return pl.pallas_call(
        functools.partial(_out_kernel, nmix, nct, two_src, sub, t0),
        grid=(ntiles // sub,),
        in_specs=in_specs,
        out_specs=[pl.BlockSpec((sub * TM, D), lambda i: (i, 0)),
                   pl.BlockSpec((sub * TM * TOK, 128), lambda i: (i, 0)), rows_out, rows_out,
                   _full((NB_PAD, 128))],
        out_shape=[jax.ShapeDtypeStruct((n, D), F32), jax.ShapeDtypeStruct((n * TOK, 128), U32),
                   jax.ShapeDtypeStruct((ntiles, 1, TM), I32), jax.ShapeDtypeStruct((ntiles, 1, TM), I32),
                   jax.ShapeDtypeStruct((NB_PAD, 128), I32)],
        scratch_shapes=[pltpu.VMEM((NB_PAD, 128), F32)],
        compiler_params=_cparams("arbitrary"),
        name="out_proj_router",
    )(*args, w_out, mod, g_ffn, rw, rw_hi, rb_col)


def _scatter_kernel(ntiles, pos_ref, flo_ref, fhi_ref, src_ref, dst_ref, hbuf, in_sem, out_sem):
    i = pl.program_id(0)
    slot = i % 3
    h_ref = hbuf.at[slot]
    sem = out_sem.at[slot]
    rows = TM * TOK

    def load(tile, s):
        return pltpu.make_async_copy(src_ref.at[pl.ds(pl.multiple_of(tile * rows, rows), rows), :],
                                     hbuf.at[s], in_sem.at[s])

    def tok(ref, t):
        return ref.at[pl.ds(pl.multiple_of(t * TOK, TOK), TOK), :]

    def copy(r, d_row, src=h_ref, sm=sem):
        return pltpu.make_async_copy(tok(src, r), tok(dst_ref, d_row), sm)

    def wait_rows(lo, hi, unroll, s=slot):
        def body(r, carry):
            copy(0, 0, hbuf.at[s], out_sem.at[s]).wait()
            return carry
        lax.fori_loop(lo, hi, body, 0, unroll=unroll)

    @pl.when(i == 0)
    def _():
        load(0, 0).start()
        if ntiles > 1:
            load(1, 1).start()

    load(i, slot).wait()

    def start(r8, c):
        for k in range(8):
            r = r8 * 8 + k
            copy(r, pos_ref[i * TM + r]).start(priority=k % 2)
        return c
    lax.fori_loop(0, TM // 8, start, 0)

    @pl.when(i > 0)
    def _():
        wait_rows(0, TM, 8, (i + 2) % 3)

    @pl.when(i + 2 < ntiles)
    def _():
        load(i + 2, (i + 2) % 3).start()

    @pl.when(i == ntiles - 1)
    def _():
        wait_rows(0, TM, 8)
        for b in range(N_BUCKETS):
            lo, hi = flo_ref[b], fhi_ref[b]

            def fill(r, c):
                copy(0, r).start()
                return c
            lax.fori_loop(lo, hi, fill, 0)
            wait_rows(lo, hi, 1)

        def tile_copy(j):
            rows = TMM * TOK
            return pltpu.make_async_copy(h_ref, dst_ref.at[pl.ds(pl.multiple_of(j * rows, rows), rows), :], sem)

        def fill_tile(j, c):
            tile_copy(j).start()
            return c

        def wait_tile(j, c):
            tile_copy(j).wait()
            return c
        lax.fori_loop(flo_ref[N_BUCKETS], fhi_ref[N_BUCKETS], fill_tile, 0)
        lax.fori_loop(flo_ref[N_BUCKETS], fhi_ref[N_BUCKETS], wait_tile, 0)


def _scatter_rows(hrow, pos, fill_lo, fill_hi, n, npad):
    assert TM == TMM
    return pl.pallas_call(
        functools.partial(_scatter_kernel, n // TM),
        grid_spec=pltpu.PrefetchScalarGridSpec(
            num_scalar_prefetch=3,
            grid=(n // TM,),
            in_specs=[pl.BlockSpec(memory_space=pl.ANY)],
            out_specs=pl.BlockSpec(memory_space=pl.ANY),
            scratch_shapes=[pltpu.VMEM((3, TM * TOK, 128), hrow.dtype), pltpu.SemaphoreType.DMA((3,)),
                            pltpu.SemaphoreType.DMA((3,))],
        ),
        out_shape=jax.ShapeDtypeStruct((npad * TOK, 128), hrow.dtype),
        compiler_params=_cparams("arbitrary"),
        name="moe_scatter_rows",
    )(pos, fill_lo, fill_hi, hrow)


def _combine_kernel(ntiles, pos_ref, x_ref, mod_ref, ys_ref, o_ref, ybuf, sem):
    i = pl.program_id(0)

    def copy(tile, slot, r):
        src = pl.multiple_of(pos_ref[tile * TM + r] * TOK, TOK)
        dst = pl.multiple_of(r * TOK, TOK)
        return pltpu.make_async_copy(ys_ref.at[pl.ds(src, TOK), :], ybuf.at[slot, pl.ds(dst, TOK), :], sem.at[slot])

    def start_tile(tile, slot):
        def body(r8, carry):
            for k in range(8):
                copy(tile, slot, r8 * 8 + k).start(priority=k % 2)
            return carry
        lax.fori_loop(0, TM // 8, body, 0)

    @pl.when(i == 0)
    def _():
        start_tile(0, 0)

    @pl.when(i + 1 < ntiles)
    def _():
        start_tile(i + 1, (i + 1) % 2)

    slot = i % 2

    def wait_body(r, carry):
        copy(i, slot, 0).wait()
        return carry
    lax.fori_loop(0, TM, wait_body, 0, unroll=8)
    o_ref[...] = x_ref[...] + mod_ref[0, 5:6, :] * _tok_rows(ybuf.at[slot], TM)


def _combine(x_mid, mod, ys, pos, t0, nct):
    n = x_mid.shape[0]
    ntiles = n // TM
    return pl.pallas_call(
        functools.partial(_combine_kernel, ntiles),
        grid_spec=pltpu.PrefetchScalarGridSpec(
            num_scalar_prefetch=1,
            grid=(ntiles,),
            in_specs=[pl.BlockSpec((TM, D), lambda i, *_: (i, 0)),
                      pl.BlockSpec((1, 6, D), lambda i, *_: (jnp.where(i + t0 < nct, 1, 0), 0, 0)),
                      pl.BlockSpec(memory_space=pl.ANY)],
            out_specs=pl.BlockSpec((TM, D), lambda i, *_: (i, 0)),
            scratch_shapes=[pltpu.VMEM((2, TM * TOK, 128), F32), pltpu.SemaphoreType.DMA((2,))],
        ),
        out_shape=jax.ShapeDtypeStruct((n, D), F32),
        compiler_params=_cparams("arbitrary"),
        name="moe_combine",
    )(pos, x_mid, mod, ys)


def _moe_kernel(tea_ref, teb_ref, tblk_ref, tval_ref, x_ref, w1a_ref, w3a_ref, w2a_ref,
                w1b_ref, w3b_ref, w2b_ref, y_ref):
    del tea_ref, teb_ref, tblk_ref
    j = pl.program_id(0)

    @pl.when(tval_ref[j] != 0)
    def _():
        words = [_tok_load(x_ref, cblk, TMM) for cblk in range(4)]
        low = [pltpu.bitcast(wd << 16, F32).astype(BF16) for wd in words]
        high = [pltpu.bitcast(wd & jnp.uint32(0xFFFF0000), F32).astype(BF16) for wd in words]
        hb = jnp.concatenate(low + high, axis=1)
        gates = pltpu.bitcast(_tok_load(x_ref, 4, TMM), F32)
        acts = []
        for w1_ref, w3_ref, lane in ((w1a_ref, w3a_ref, 0), (w1b_ref, w3b_ref, 1)):
            u = _dot(hb, w1_ref[0, 0].astype(BF16))
            v = _dot(hb, w3_ref[0, 0].astype(BF16))
            acts.append((_silu(u) * v * gates[:, lane:lane + 1]).astype(BF16))
        y = _dot(acts[0], w2a_ref[0, 0].astype(BF16)) + _dot(acts[1], w2b_ref[0, 0].astype(BF16))
        for cblk in range(TOK):
            _tok_store(y_ref, cblk, y[:, 128 * cblk:128 * (cblk + 1)])

    @pl.when(tval_ref[j] == 0)
    def _():
        y_ref[...] = jnp.zeros_like(y_ref)


def _moe(xs_sorted, w1, w3, w2, layer, tile_ea, tile_eb, tile_blk, tile_valid):
    npad = xs_sorted.shape[0] // TOK
    ntile = npad // TMM
    wspec = lambda shape, which: pl.BlockSpec(
        (1, 1) + shape, lambda j, ea, eb, blk, val: (layer, (ea, eb)[which][j], 0, 0))
    up, down = (D, D_EXPERT), (D_EXPERT, D)
    return pl.pallas_call(
        _moe_kernel,
        grid_spec=pltpu.PrefetchScalarGridSpec(
            num_scalar_prefetch=4,
            grid=(ntile,),
            in_specs=[pl.BlockSpec((TMM * TOK, 128), lambda j, ea, eb, blk, val: (blk[j], 0)),
                      wspec(up, 0), wspec(up, 0), wspec(down, 0), wspec(up, 1), wspec(up, 1), wspec(down, 1)],
            out_specs=pl.BlockSpec((TMM * TOK, 128), lambda j, ea, eb, blk, val: (j, 0)),
        ),
        out_shape=jax.ShapeDtypeStruct((npad * TOK, 128), F32),
        compiler_params=_cparams("arbitrary"),
        name="moe_experts",
    )(tile_ea, tile_eb, tile_blk, tile_valid, xs_sorted, w1, w3, w2, w1, w3, w2)


def _moe_block(x_mid, hrow, bucket, rank, counts, mod, w1, w3, w2, layer, t0, nct):
    n = hrow.shape[0] // TOK
    ntile = n // TMM + N_BUCKETS
    npad = ntile * TMM
    cnt = counts[:N_BUCKETS, 0]
    padded = ((cnt + TMM - 1) // TMM) * TMM
    ends = jnp.cumsum(padded)
    starts = ends - padded
    total_tiles = ends[-1] // TMM
    tiles = jnp.arange(ntile, dtype=I32)
    tile_valid = (tiles < total_tiles).astype(I32)
    tile_blk = jnp.minimum(tiles, jnp.maximum(total_tiles - 1, 0))
    tile_bucket = jnp.minimum(jnp.sum((ends[None, :] <= (tile_blk * TMM)[:, None]).astype(I32), axis=1), N_BUCKETS - 1)
    pair = tile_bucket % 6
    grp = tile_bucket // 6
    pair_a = jnp.where(pair < 3, 0, jnp.where(pair < 5, 1, 2))
    pair_b = jnp.where(pair < 3, pair + 1, jnp.where(pair < 5, pair - 1, 3))
    tile_ea = (4 * grp + pair_a).astype(I32)
    tile_eb = (4 * grp + pair_b).astype(I32)
    bucket = bucket.reshape(-1)
    onehot = (bucket[:, None] == jnp.arange(N_BUCKETS, dtype=I32)[None, :]).astype(I32)
    pos = (rank.reshape(-1) + jnp.sum(onehot * starts[None, :], axis=1)).astype(I32)
    pad32 = lambda a, tail: jnp.zeros((NB_PAD,), I32).at[:N_BUCKETS].set(a.astype(I32)).at[N_BUCKETS].set(tail)
    fill_lo, fill_hi = pad32(starts + cnt, total_tiles), pad32(ends, ntile)
    xs_sorted = _scatter_rows(hrow, pos, fill_lo, fill_hi, n, npad)
    ys = _moe(xs_sorted, w1, w3, w2, layer, tile_ea, tile_eb, tile_blk, tile_valid)
    return _combine(x_mid, mod, ys, pos, t0, nct)


def kernel(x, c, ctx, c_ctx, router_w, router_b, norm_mix, norm_ffn, w_mod, b_mod, ev_w_in, ev_conv_w, ev_conv_b, ev_dt_bias, ev_a_log, ev_d_skip, ev_ssd_norm, ev_q_norm, ev_k_norm, ev_sink, ev_w_out, od_w_in, od_conv_w, od_conv_b, od_igate_b, od_fgate_b, od_head_norm, od_w_out, moe_w1, moe_w3, moe_w2):
    s_len = x.shape[1]
    c_len = ctx.shape[1]
    assert x.shape[0] == 1 and s_len % TM == 0 and c_len % TM == 0 and s_len % GRID_W == 0
    nct = c_len // TM
    ncc, nlc = c_len // T, s_len // T
    nt = c_len + s_len
    ntiles = nt // TM

    mod = _modulation(c, c_ctx, w_mod, b_mod)
    rw, rw_hi = _hi_lo_weight(router_w)
    rb_col = router_b.reshape(N_EXPERTS, 1)
    pad128 = lambda v: jnp.zeros((1, 128), F32).at[0, :v.shape[0]].set(v)

    w = ev_w_in[0]
    cos, sin = _rope_tables(s_len)
    wdt, wdt_hi = _hi_lo_weight(w[:, 1280:1296])
    p0 = dict(
        g_mix=norm_mix[0].reshape(1, D),
        wcat=jnp.concatenate([w[:, 0:1280].astype(BF16), w[:, 1296:2064].astype(BF16), wdt], axis=1),
        wdt_hi=wdt_hi,
        conv_w=ev_conv_w[0], conv_b=ev_conv_b[0].reshape(1, 768),
        dt_bias=pad128(ev_dt_bias[0].reshape(16)),
        q_norm=jnp.tile(ev_q_norm[0], 8).reshape(1, 512), k_norm=jnp.tile(ev_k_norm[0], 2).reshape(1, 128),
        cos=cos, sin=sin,
        alog_row=ev_a_log[0].reshape(1, 16), alog_col=ev_a_log[0].reshape(16, 1),
        d_skip=jnp.repeat(ev_d_skip[0], 64).reshape(1, 512), ssd_norm=ev_ssd_norm[0].reshape(1, 512))
    z, xs, bc, q, kk, vv, dtc, dtr = _in0(ctx[0], x[0], mod[0], p0, nct)
    yf = _ssd(False, xs, bc, dtc, dtr, p0, ncc, nlc)
    ymix = _ssd(True, xs, bc, dtc, dtr, p0, ncc, nlc, yf=yf, z=z)
    att = _attention(q, kk, vv, ev_sink[0].reshape(1, 8), ncc, nlc)
    x_mid0, hrow, bucket, rank, counts = _out_proj(
        [ymix, att], ev_w_out[0].astype(BF16), [ctx[0], x[0]], mod[0], norm_ffn[0].reshape(1, D),
        rw, rw_hi, rb_col, 0, ntiles, nct)
    x1 = _moe_block(x_mid0, hrow, bucket, rank, counts, mod[0], moe_w1, moe_w3, moe_w2, 0, 0, nct)

    w = od_w_in[0]
    wg, wg_hi = _hi_lo_weight(w[:, 3072:3104])
    p1 = dict(
        g_mix=norm_mix[1].reshape(1, D),
        wcat=jnp.concatenate([w[:, 0:3072].astype(BF16), wg], axis=1),
        wg_hi=wg_hi,
        conv_w=od_conv_w[0], conv_b=od_conv_b[0].reshape(1, 2048),
        gate_bias=pad128(jnp.concatenate([od_igate_b[0].reshape(16), od_fgate_b[0].reshape(16)])),
        head_norm=od_head_norm[0].reshape(1, 1024))
    q1, kt1, v1, og1, gc1, gr1 = _in1(x1, mod[1], p1, nct)
    hf = _mlstm(False, q1, kt1, v1, gc1, gr1, p1, ncc, nlc)
    hmix = _mlstm(True, q1, kt1, v1, gc1, gr1, p1, ncc, nlc, hf=hf, og=og1)
    x_mid1, hrow, bucket, rank, counts = _out_proj(
        [hmix], od_w_out[0].astype(BF16), [x1], mod[1], norm_ffn[1].reshape(1, D),
        rw, rw_hi, rb_col, nct, ntiles - nct, nct)
    return _moe_block(x_mid1, hrow, bucket, rank, counts, mod[1], moe_w1, moe_w3, moe_w2, 1, nct, nct)[None]
```

```python
import functools
import math

import jax
import jax.numpy as jnp
from jax import lax
from jax.experimental import pallas as pl
from jax.experimental.pallas import tpu as pltpu

F32 = jnp.float32
BF16 = jnp.bfloat16
I32 = jnp.int32

EPS = 1e-6
D = 1024
T = 128
SCAN_CHUNKS = 2
TM = 256
TMM = 256
GRID_W = 64
ROPE_THETA = 10000.0
N_EXPERTS = 16
N_BUCKETS = 24
NB_PAD = 32
D_EXPERT = 512
TOK = 8
U32 = jnp.uint32
NEG_INF = float("-inf")
VMEM_LIMIT = 56 * 1024 * 1024

_NN = (((1,), (0,)), ((), ()))
_NT = (((1,), (1,)), ((), ()))
_TN = (((0,), (0,)), ((), ()))

_PAIR_A = (0, 0, 0, 1, 1, 2)
_PAIR_B = (1, 2, 3, 2, 3, 3)


def _cparams(*sem):
    return pltpu.CompilerParams(dimension_semantics=sem, vmem_limit_bytes=VMEM_LIMIT)


def _dot(a, b, dims=_NN):
    return lax.dot_general(a, b, dims, preferred_element_type=F32)


def _split(a, n):
    out = []
    r = a
    for _ in range(n):
        t = r.astype(BF16)
        out.append(t)
        r = r - t.astype(F32)
    return out


def _mdot(as_, bs, dims=_NN, order=None):
    if order is None:
        order = len(as_) + len(bs) - 2
    acc = None
    for i, a in enumerate(as_):
        for j, b in enumerate(bs):
            if i + j <= order:
                p = _dot(a, b, dims)
                acc = p if acc is None else acc + p
    return acc


def _sigmoid(x):
    return 1.0 / (1.0 + jnp.exp(-x))


def _silu(x):
    return x * _sigmoid(x)


def _log1p_exp_neg_abs(x):
    e = jnp.exp(-jnp.abs(x))
    u = 1.0 + e
    um1 = u - 1.0
    return jnp.where(um1 == 0.0, e, jnp.log(u) * (e / jnp.where(um1 == 0.0, 1.0, um1)))


def _softplus(x):
    return jnp.maximum(x, 0.0) + _log1p_exp_neg_abs(x)


def _log_sigmoid(x):
    return jnp.minimum(x, 0.0) - _log1p_exp_neg_abs(x)


def _norm_mod(x, g, sc, sh):
    ms = jnp.mean(x * x, axis=-1, keepdims=True)
    return (x * lax.rsqrt(ms + EPS)) * g * (1.0 + sc) + sh


def _tok_load(ref, chunk, n):
    return ref[pl.ds(chunk, n, stride=TOK), :]


def _tok_store(ref, chunk, val):
    ref[pl.ds(chunk, val.shape[0], stride=TOK), :] = val


def _tok_rows(ref, n):
    return jnp.concatenate([_tok_load(ref, c, n) for c in range(TOK)], axis=1)


def _tri(rev):
    r = lax.broadcasted_iota(I32, (T, T), 0)
    c = lax.broadcasted_iota(I32, (T, T), 1)
    return (c >= r) if rev else (c <= r)


def _cumsums(rev, col, row):
    tri = _tri(rev)
    tri_b = tri.astype(F32).astype(BF16)
    trit_b = _tri(not rev).astype(F32).astype(BF16)
    ccol = _mdot([tri_b], _split(col, 3))
    crow = _mdot(_split(row, 3), [trit_b])
    return tri, ccol, crow


def _lane_bcast(col):
    n = col.shape[1]
    col = jnp.concatenate([col, jnp.zeros((col.shape[0], 128 - n), F32)], axis=1)
    r = lax.broadcasted_iota(I32, (128, n * 128), 0)
    c = lax.broadcasted_iota(I32, (128, n * 128), 1) // 128
    return _mdot(_split(col, 3), [(r == c).astype(F32).astype(BF16)])


def _mod_kernel(c_ref, w_ref, b_ref, o_ref):
    a = _silu(c_ref[...])
    o_ref[0] = _mdot(_split(a, 2), _split(w_ref[0], 2), order=1) + b_ref[0]


def _modulation(c, c_ctx, w_mod, b_mod):
    depth = w_mod.shape[0]
    n = w_mod.shape[2]
    tn = 1536
    cc = jnp.zeros((8, D), F32).at[0].set(c[0]).at[1].set(c_ctx)
    out = pl.pallas_call(
        _mod_kernel,
        grid=(depth, n // tn),
        in_specs=[
            pl.BlockSpec((8, D), lambda l, j: (0, 0)),
            pl.BlockSpec((1, D, tn), lambda l, j: (l, 0, j)),
            pl.BlockSpec((1, 1, tn), lambda l, j: (l, 0, j)),
        ],
        out_specs=pl.BlockSpec((1, 8, tn), lambda l, j: (l, 0, j)),
        out_shape=jax.ShapeDtypeStruct((depth, 8, n), F32),
        compiler_params=_cparams("arbitrary", "arbitrary"),
        name="modulation",
    )(cc, w_mod, b_mod.reshape(depth, 1, n))
    return out[:, :2].reshape(depth, 2, 6, D)


def _halo_specs(nrows, tile_of):
    nb8 = nrows // 8
    return [
        pl.BlockSpec((8, D), lambda i: (jnp.maximum(tile_of(i) * (TM // 8) - 1, 0), 0)),
        pl.BlockSpec((TM, D), lambda i: (tile_of(i), 0)),
        pl.BlockSpec((8, D), lambda i: (jnp.minimum((tile_of(i) + 1) * (TM // 8), nb8 - 1), 0)),
    ]


def _ctx_tile(nct):
    return lambda i: jnp.minimum(i, nct - 1)


def _lat_tile(nct):
    return lambda i: jnp.maximum(i - nct, 0)


def _mod_spec(nct, t0=0):
    return pl.BlockSpec((1, 6, D), lambda i: (jnp.where(i + t0 < nct, 1, 0), 0, 0))


def _full(shape):
    nd = len(shape)
    return pl.BlockSpec(shape, lambda i: (0,) * nd)


def _seq_edges(i, nct, ntiles):
    prev_ok = jnp.logical_and(i != 0, i != nct).astype(F32)
    next_ok = jnp.logical_and(i != nct - 1, i != ntiles - 1).astype(F32)
    return prev_ok, next_ok


def _conv_silu(x, x_first_prev, x_last_next, cw, cb):
    n = x.shape[0]
    rows = lax.broadcasted_iota(I32, x.shape, 0)
    x_prev = jnp.where(rows == 0, x_first_prev, pltpu.roll(x, 1, 0))
    x_next = jnp.where(rows == n - 1, x_last_next, pltpu.roll(x, n - 1, 0))
    return _silu(x_prev * cw[0:1] + x * cw[1:2] + x_next * cw[2:3] + cb)


def _hi_lo_cols(blk, lo_pass, n):
    return blk + pltpu.roll(blk, 128 - n, 1) + lo_pass


def _hi_lo_weight(w):
    n = w.shape[1]
    hi = w.astype(BF16)
    lo = (w - hi.astype(F32)).astype(BF16)
    z = jnp.zeros((w.shape[0], 128 - 2 * n), BF16)
    return jnp.concatenate([hi, lo, z], axis=1), jnp.concatenate([hi, jnp.zeros_like(lo), z], axis=1)


def _head_rms(xf, gamma):
    r = lax.broadcasted_iota(I32, (128, 128), 0) // 64
    c = lax.broadcasted_iota(I32, (128, 128), 1) // 64
    ones_bd = (r == c).astype(F32).astype(BF16)
    outs = []
    for j in range(xf.shape[1] // 128):
        blk = xf[:, 128 * j:128 * (j + 1)]
        ssum = _dot((blk * blk).astype(BF16), ones_bd)
        outs.append(blk * lax.rsqrt(ssum * (1.0 / 64.0) + EPS))
    return jnp.concatenate(outs, axis=1) * gamma


def _rope(xf, cos, sin):
    lane = lax.broadcasted_iota(I32, (xf.shape[0], 128), 1)
    first = (lane % 32) < 16
    outs = []
    for j in range(xf.shape[1] // 128):
        blk = xf[:, 128 * j:128 * (j + 1)]
        partner = jnp.where(first, pltpu.roll(blk, 112, 1), pltpu.roll(blk, 16, 1))
        outs.append(blk * cos + partner * sin)
    return jnp.concatenate(outs, axis=1)


def _rope_tables(s_len):
    rows = s_len // GRID_W
    inv = ROPE_THETA ** (-(jnp.arange(32, dtype=I32) % 16).astype(F32) / 16.0)
    sign = jnp.where(jnp.arange(32) < 16, -1.0, 1.0).astype(F32)
    ang_r = jnp.arange(rows, dtype=F32)[:, None] * inv[None, :]
    ang_c = jnp.arange(GRID_W, dtype=F32)[:, None] * inv[None, :]

    def table(fr, fc):
        r = jnp.broadcast_to(fr[:, None, :], (rows, GRID_W, 32))
        c = jnp.broadcast_to(fc[None, :, :], (rows, GRID_W, 32))
        t = jnp.concatenate([r, c, r, c], axis=-1)
        return t.reshape(s_len, 128)

    return table(jnp.cos(ang_r), jnp.cos(ang_c)), table(jnp.sin(ang_r) * sign, jnp.sin(ang_c) * sign)


def _in0_kernel(nct, ntiles, cp_ref, c_ref, cn_ref, xp_ref, x_ref, xn_ref, mod_ref, g_ref, wcat_ref, wdth_ref,
                cw_ref, cb_ref, dtb_ref, qn_ref, kn_ref, cos_ref, sin_ref,
                z_ref, xs_ref, bc_ref, q_ref, kk_ref, vv_ref, dtc_ref, dtr_ref):
    i = pl.program_id(0)
    is_ctx = i < nct
    sh = mod_ref[0, 0:1, :]
    sc = mod_ref[0, 1:2, :]
    g = g_ref[...]
    x_all = jnp.concatenate([jnp.where(is_ctx, cp_ref[...], xp_ref[...]),
                             jnp.where(is_ctx, c_ref[...], x_ref[...]),
                             jnp.where(is_ctx, cn_ref[...], xn_ref[...])], axis=0)
    h_all = _norm_mod(x_all, g, sc, sh)
    hb_all = h_all.astype(BF16)
    main_all = _dot(hb_all, wcat_ref[...])
    h, hb, main = h_all[8:8 + TM], hb_all[8:8 + TM], main_all[8:8 + TM]
    prev_ok, next_ok = _seq_edges(i, nct, ntiles)
    xb_prev = main_all[7:8, 512:1280] * prev_ok
    xb_next = main_all[8 + TM:9 + TM, 512:1280] * next_ok
    act = _conv_silu(main[:, 512:1280], xb_prev, xb_next, cw_ref[...], cb_ref[...])
    z_ref[...] = main[:, 0:512].astype(BF16)
    xs_ref[...] = act[:, 0:512].astype(BF16)
    bc_ref[...] = act[:, 512:768].astype(BF16)
    cos = jnp.where(is_ctx, 1.0, cos_ref[...])
    sin = jnp.where(is_ctx, 0.0, sin_ref[...])
    q = _rope(_head_rms(main[:, 1280:1792], qn_ref[...]), cos, sin) * 0.125
    q_ref[...] = q.astype(BF16)
    k = _rope(_head_rms(main[:, 1792:1920], kn_ref[...]), cos, sin)
    kk_ref[...] = jnp.concatenate([k, pltpu.roll(k, 64, 1)], axis=1).astype(BF16)
    v = main[:, 1920:2048]
    vv_ref[...] = jnp.concatenate([v, pltpu.roll(v, 64, 1)], axis=1).astype(BF16)
    h_lo = (h - hb.astype(F32)).astype(BF16)
    dt = _softplus(_hi_lo_cols(main[:, 2048:2176], _dot(h_lo, wdth_ref[...]), 16) + dtb_ref[...])
    dtc_ref[...] = dt[:, 0:16]
    dtr_ref[...] = jnp.transpose(dt)[0:16, :]


def _in0(ctx2, x2, mod, p, nct):
    c_len, s_len = ctx2.shape[0], x2.shape[0]
    nt = c_len + s_len
    ntiles = nt // TM
    tile = lambda w: pl.BlockSpec((TM, w), lambda i: (i, 0))
    lat = _lat_tile(nct)
    lat_tile = lambda w: pl.BlockSpec((TM, w), lambda i: (lat(i), 0))
    outs = [(512, BF16), (512, BF16), (256, BF16), (512, BF16), (256, BF16), (256, BF16), (16, F32)]
    return pl.pallas_call(
        functools.partial(_in0_kernel, nct, ntiles),
        grid=(ntiles,),
        in_specs=_halo_specs(c_len, _ctx_tile(nct)) + _halo_specs(s_len, lat) + [
            _mod_spec(nct), _full((1, D)), _full((D, 2176)), _full((D, 128)),
            _full((3, 768)), _full((1, 768)), _full((1, 128)),
            _full((1, 512)), _full((1, 128)), lat_tile(128), lat_tile(128)],
        out_specs=[tile(w) for w, _ in outs] + [pl.BlockSpec((16, TM), lambda i: (0, i))],
        out_shape=[jax.ShapeDtypeStruct((nt, w), dt) for w, dt in outs]
        + [jax.ShapeDtypeStruct((16, nt), F32)],
        compiler_params=_cparams("arbitrary"),
        name="in_proj_even",
    )(ctx2, ctx2, ctx2, x2, x2, x2, mod, p["g_mix"], p["wcat"], p["wdt_hi"], p["conv_w"], p["conv_b"],
      p["dt_bias"], p["q_norm"], p["k_norm"], p["cos"], p["sin"])


def _scan_chunk_map(rev, ncc, nlc):
    if not rev:
        return lambda j: j
    return lambda j: jnp.where(j < ncc, ncc - 1 - j, ncc + nlc - 1 - (j - ncc))


def _ssd_kernel(rev, *refs):
    if rev:
        (xs_ref, bc_ref, dtc_ref, dtr_ref, alr_ref, alc_ref, yf_ref, z_ref, dsk_ref, nrm_ref,
         o_ref, st_ref) = refs
    else:
        xs_ref, bc_ref, dtc_ref, dtr_ref, alr_ref, alc_ref, o_ref, st_ref = refs
    j = pl.program_id(0)

    @pl.when(j == 0)
    def _():
        st_ref[...] = jnp.zeros_like(st_ref)

    d = 8 if rev else 0
    a_coef_row = -jnp.exp(alr_ref[...])[:, d:d + 8]
    a_coef_col = -jnp.exp(alc_ref[...])[d:d + 8, :]
    lane = lax.broadcasted_iota(I32, (T, 128), 1)
    lo = lane < 64
    zero_b = jnp.zeros((T, 128), BF16)
    hi_half = jnp.logical_not(lo)
    sub = lax.broadcasted_iota(I32, (128, T), 0)
    eye = (sub == lax.broadcasted_iota(I32, (128, T), 1)).astype(F32).astype(BF16)
    end = 0 if rev else T - 1
    order = tuple(reversed(range(SCAN_CHUNKS))) if rev else tuple(range(SCAN_CHUNKS))

    def prologue(c):
        rows = slice(c * T, (c + 1) * T)
        dtc = dtc_ref[rows, d:d + 8]
        dtr = dtr_ref[d:d + 8, rows]
        tri, acs_col, acs_row = _cumsums(rev, dtc * a_coef_row, dtr * a_coef_col)
        atot_col = acs_row[:, end:end + 1]
        acs_bc = _lane_bcast(acs_col)
        dec_row = jnp.exp(atot_col - acs_row) * dtr
        xs = xs_ref[rows, :]
        bm = bc_ref[rows, 0:128]
        cm = bc_ref[rows, 128:256]
        cgs = [jnp.where(lo, cm, zero_b), jnp.where(hi_half, cm, zero_b)]
        cbs = [_dot(cgs[g], bm, _NT) for g in range(2)]
        bmt = _dot(eye, bm, _NT)
        bgts = [jnp.where(sub < 64, bmt, 0.0), jnp.where(sub >= 64, bmt, 0.0)]
        xpairs = []
        for pr in range(4):
            xp = xs[:, 128 * pr:128 * (pr + 1)]
            xpairs.append(jnp.concatenate([jnp.where(lo, xp, zero_b), jnp.where(hi_half, xp, zero_b)], axis=0))
        return dict(tri=tri, acs_row=acs_row, acs_bc=acs_bc, atot_col=atot_col, dec_row=dec_row, dtr=dtr,
                    xs=xs, cgs=cgs, cbs=cbs, bgts=bgts, xpairs=xpairs)

    pro = {c: prologue(c) for c in order}
    y_off = {}
    for c in order:
        p = pro[c]
        for pr in range(4):
            g, h0, h1 = pr // 2, 2 * pr, 2 * pr + 1
            st = st_ref[pr]
            eacs = jnp.exp(jnp.where(lo, p["acs_bc"][:, 128 * h0:128 * (h0 + 1)],
                                     p["acs_bc"][:, 128 * h1:128 * (h1 + 1)]))
            y_off[c, pr] = _dot(p["cgs"][g], st.astype(BF16)) * eacs
            bdec = jnp.concatenate([(p["bgts"][g] * p["dec_row"][h0:h0 + 1, :]).astype(BF16),
                                    (p["bgts"][g] * p["dec_row"][h1:h1 + 1, :]).astype(BF16)], axis=1)
            carry = jnp.where(lo[0:1, :], jnp.exp(p["atot_col"][h0:h0 + 1, :]), jnp.exp(p["atot_col"][h1:h1 + 1, :]))
            st_ref[pr] = carry * st + _dot(bdec, p["xpairs"][pr])
    for c in order:
        p = pro[c]
        rows = slice(c * T, (c + 1) * T)
        ys = []
        for pr in range(4):
            ms = []
            for hd in (2 * pr, 2 * pr + 1):
                diff = p["acs_bc"][:, 128 * hd:128 * (hd + 1)] - p["acs_row"][hd:hd + 1, :]
                lmat = jnp.exp(jnp.where(p["tri"], diff, NEG_INF))
                ms.append((p["cbs"][hd // 4] * lmat * p["dtr"][hd:hd + 1, :]).astype(BF16))
            ys.append(_dot(jnp.concatenate(ms, axis=1), p["xpairs"][pr]) + y_off[c, pr])
        y = jnp.concatenate(ys, axis=1)
        if not rev:
            o_ref[rows, :] = y
        else:
            ytot = y + yf_ref[rows, :] + dsk_ref[...] * p["xs"].astype(F32)
            gated = ytot * _silu(z_ref[rows, :].astype(F32))
            ms = jnp.mean(gated * gated, axis=-1, keepdims=True)
            o_ref[rows, :] = (gated * lax.rsqrt(ms + EPS) * nrm_ref[...]).astype(BF16)


def _ssd(rev, xs, bc, dtc, dtr, p, ncc, nlc, yf=None, z=None):
    nt = xs.shape[0]
    assert ncc % SCAN_CHUNKS == 0 and nlc % SCAN_CHUNKS == 0
    ncc, nlc = ncc // SCAN_CHUNKS, nlc // SCAN_CHUNKS
    rows = SCAN_CHUNKS * T
    cmap = _scan_chunk_map(rev, ncc, nlc)
    blk = lambda w: pl.BlockSpec((rows, w), lambda j: (cmap(j), 0))
    in_specs = [blk(512), blk(256), blk(16), pl.BlockSpec((16, rows), lambda j: (0, cmap(j))),
                _full((1, 16)), _full((16, 1))]
    args = [xs, bc, dtc, dtr, p["alog_row"], p["alog_col"]]
    if rev:
        in_specs += [blk(512), blk(512), _full((1, 512)), _full((1, 512))]
        args += [yf, z, p["d_skip"], p["ssd_norm"]]
    return pl.pallas_call(
        functools.partial(_ssd_kernel, rev),
        grid=(ncc + nlc,),
        in_specs=in_specs,
        out_specs=blk(512),
        out_shape=jax.ShapeDtypeStruct((nt, 512), BF16 if rev else F32),
        scratch_shapes=[pltpu.VMEM((4, 128, 128), F32)],
        compiler_params=_cparams("arbitrary"),
        name="ssd_bwd" if rev else "ssd_fwd",
    )(*args)


def _attn_kernel(ncc, nblk, q_ref, kp_ref, kc_ref, kn_ref, vp_ref, vc_ref, vn_ref, kx_ref, vx_ref,
                 sink_ref, o_ref):
    j = pl.program_id(0)
    c_len = kx_ref.shape[0]
    is_lat = j >= ncc
    prev_ok = jnp.logical_and(is_lat, j >= ncc + 1)
    next_ok = jnp.logical_and(is_lat, j <= nblk - 2)
    r = lax.broadcasted_iota(I32, (T, T), 0)
    c = lax.broadcasted_iota(I32, (T, T), 1)
    zero = jnp.zeros((T, T), F32)
    ninf = jnp.full((T, T), NEG_INF, F32)
    bias = jnp.concatenate([
        jnp.where(jnp.logical_and(prev_ok, c >= r), zero, ninf),
        jnp.where(is_lat, zero, ninf),
        jnp.where(jnp.logical_and(next_ok, c <= r), zero, ninf),
        jnp.zeros((T, c_len), F32)], axis=1)
    k_all = jnp.concatenate([kp_ref[...], kc_ref[...], kn_ref[...], kx_ref[...]], axis=0)
    v_all = jnp.concatenate([vp_ref[...], vc_ref[...], vn_ref[...], vx_ref[...]], axis=0)
    lo = lax.broadcasted_iota(I32, (T, 128), 1) < 64
    zero_b = jnp.zeros((T, 128), BF16)
    q = q_ref[...]
    sink = sink_ref[...]
    bias4 = jnp.concatenate([bias] * 4, axis=0)
    outs = {}
    stacks = [[hd for hd in range(8) if (hd // 4 + hd % 2) % 2 == b] for b in range(2)]
    s_all = []
    for b in range(2):
        qs = []
        for hd in stacks[b]:
            qp = q[:, 128 * (hd // 2):128 * (hd // 2 + 1)]
            qs.append(jnp.where(lo, zero_b, qp) if hd % 2 else jnp.where(lo, qp, zero_b))
        s_all.append(_dot(jnp.concatenate(qs, axis=0), k_all[:, 128 * b:128 * (b + 1)], _NT) + bias4)
    for b in range(2):
        s = s_all[b]
        sk = jnp.concatenate([jnp.broadcast_to(sink[:, hd:hd + 1], (T, 1)) for hd in stacks[b]], axis=0)
        m = jnp.maximum(jnp.max(s, axis=-1, keepdims=True), sk)
        pr = jnp.exp(s - m)
        den = jnp.sum(pr, axis=-1, keepdims=True) + jnp.exp(sk - m)
        o = _dot(pr.astype(BF16), v_all[:, 128 * b:128 * (b + 1)]) / den
        for n, hd in enumerate(stacks[b]):
            outs[hd] = o[T * n:T * (n + 1)]
    for pair in range(4):
        o_ref[:, 128 * pair:128 * (pair + 1)] = jnp.where(lo, outs[2 * pair], outs[2 * pair + 1]).astype(BF16)


def _attention(q, kk, vv, sink, ncc, nlc):
    nt = q.shape[0]
    nblk = ncc + nlc
    c_len = ncc * T
    prev = lambda w: pl.BlockSpec((T, w), lambda j: (jnp.maximum(j - 1, 0), 0))
    cur = lambda w: pl.BlockSpec((T, w), lambda j: (j, 0))
    nxt = lambda w: pl.BlockSpec((T, w), lambda j: (jnp.minimum(j + 1, nblk - 1), 0))
    ctx = lambda w: pl.BlockSpec((c_len, w), lambda j: (0, 0))
    return pl.pallas_call(
        functools.partial(_attn_kernel, ncc, nblk),
        grid=(nblk,),
        in_specs=[cur(512), prev(256), cur(256), nxt(256), prev(256), cur(256), nxt(256),
                  ctx(256), ctx(256), _full((1, 8))],
        out_specs=cur(512),
        out_shape=jax.ShapeDtypeStruct((nt, 512), BF16),
        compiler_params=_cparams("arbitrary"),
        name="window_attention",
    )(q, kk, kk, kk, vv, vv, vv, kk, vv, sink)


def _in1_kernel(nct, ntiles, xp_ref, x_ref, xn_ref, mod_ref, g_ref,
                wcat_ref, wgh_ref, cw_ref, cb_ref, gb_ref,
                q_ref, kt_ref, v_ref, o_ref, gc_ref, gr_ref):
    i = pl.program_id(0)
    sh = mod_ref[0, 0:1, :]
    sc = mod_ref[0, 1:2, :]
    g = g_ref[...]
    x_all = jnp.concatenate([xp_ref[...], x_ref[...], xn_ref[...]], axis=0)
    h_all = _norm_mod(x_all, g, sc, sh)
    hb_all = h_all.astype(BF16)
    main_all = _dot(hb_all, wcat_ref[...])
    h, hb, main = h_all[8:8 + TM], hb_all[8:8 + TM], main_all[8:8 + TM]
    prev_ok, next_ok = _seq_edges(i, nct, ntiles)
    x_prev = main_all[7:8, 0:2048] * prev_ok
    x_next = main_all[8 + TM:9 + TM, 0:2048] * next_ok
    act = _conv_silu(main[:, 0:2048], x_prev, x_next, cw_ref[...], cb_ref[...])
    q_ref[...] = act[:, 0:512].astype(BF16)
    kt_ref[...] = jnp.transpose(act[:, 512:1024] * 0.125).astype(BF16)
    v_ref[...] = act[:, 1024:2048].astype(BF16)
    o_ref[...] = main[:, 2048:3072].astype(BF16)
    h_lo = (h - hb.astype(F32)).astype(BF16)
    gates = _hi_lo_cols(main[:, 3072:3200], _dot(h_lo, wgh_ref[...]), 32) + gb_ref[...]
    lane = lax.broadcasted_iota(I32, gates.shape, 1)
    gates = jnp.where(lane < 16, gates, _log_sigmoid(gates))
    gc_ref[...] = gates[:, 0:32]
    gr_ref[...] = jnp.transpose(gates)[0:32, :]


def _in1(x1, mod, p, nct):
    nt = x1.shape[0]
    ntiles = nt // TM
    tile = lambda w: pl.BlockSpec((TM, w), lambda i: (i, 0))
    return pl.pallas_call(
        functools.partial(_in1_kernel, nct, ntiles),
        grid=(ntiles,),
        in_specs=_halo_specs(nt, lambda i: i) + [
            _mod_spec(nct), _full((1, D)), _full((D, 3200)), _full((D, 128)),
            _full((3, 2048)), _full((1, 2048)), _full((1, 128))],
        out_specs=[tile(512), pl.BlockSpec((512, TM), lambda i: (0, i)), tile(1024), tile(1024),
                   tile(32), pl.BlockSpec((32, TM), lambda i: (0, i))],
        out_shape=[jax.ShapeDtypeStruct((nt, 512), BF16), jax.ShapeDtypeStruct((512, nt), BF16),
                   jax.ShapeDtypeStruct((nt, 1024), BF16), jax.ShapeDtypeStruct((nt, 1024), BF16),
                   jax.ShapeDtypeStruct((nt, 32), F32), jax.ShapeDtypeStruct((32, nt), F32)],
        compiler_params=_cparams("arbitrary"),
        name="in_proj_odd",
    )(x1, x1, x1, mod, p["g_mix"], p["wcat"], p["wg_hi"], p["conv_w"], p["conv_b"], p["gate_bias"])


def _mlstm_kernel(rev, *refs):
    if rev:
        (q_ref, kt_ref, v_ref, gc_ref, gr_ref, hf_ref, og_ref, hn_ref, o_ref,
         c_ref, mc_ref, mr_ref) = refs
    else:
        q_ref, kt_ref, v_ref, gc_ref, gr_ref, o_ref, c_ref, mc_ref, mr_ref = refs
    j = pl.program_id(0)

    @pl.when(j == 0)
    def _():
        c_ref[...] = jnp.zeros_like(c_ref)
        mc_ref[...] = jnp.zeros_like(mc_ref)
        mr_ref[...] = jnp.zeros_like(mr_ref)

    d = 8 if rev else 0
    end = 0 if rev else T - 1
    order = tuple(reversed(range(SCAN_CHUNKS))) if rev else tuple(range(SCAN_CHUNKS))
    ones_b = jnp.ones((T, 128), BF16)
    sub = lax.broadcasted_iota(I32, (128, T), 0)
    zero_k = jnp.zeros((128, T), BF16)

    def prologue(c):
        rows = slice(c * T, (c + 1) * T)
        ig_col = gc_ref[rows, d:d + 8]
        lf_col = gc_ref[rows, 16 + d:24 + d]
        ig_row = gr_ref[d:d + 8, rows]
        lf_row = gr_ref[16 + d:24 + d, rows]
        tri, b_col, b_row = _cumsums(rev, lf_col, lf_row)
        blast_row = b_col[end:end + 1, :]
        blast_col = b_row[:, end:end + 1]
        wend_row = blast_col - b_row + ig_row
        ac_col = jnp.max(wend_row, axis=1, keepdims=True)
        return dict(tri=tri, b_row=b_row, ig_row=ig_row, blast_row=blast_row, blast_col=blast_col,
                    ac_col=ac_col, eend_row=jnp.exp(wend_row - ac_col),
                    ac_row=jnp.max(blast_row - b_col + ig_col, axis=0, keepdims=True),
                    b_bc=_lane_bcast(b_col), q=q_ref[rows, :])

    pro = {c: prologue(c) for c in order}
    m_col = mc_ref[:, 0:1]
    m_row = mr_ref[0:1, 0:8]
    for c in order:
        p = pro[c]
        mnew_col = jnp.maximum(p["blast_col"] + m_col, p["ac_col"])
        p["sp_col"] = jnp.exp(p["blast_col"] + m_col - mnew_col)
        p["sc_col"] = jnp.exp(p["ac_col"] - mnew_col)
        p["m_row"] = m_row
        m_col, m_row = mnew_col, jnp.maximum(p["blast_row"] + m_row, p["ac_row"])

    def head_matmuls(c, hd):
        p = pro[c]
        rows = slice(c * T, (c + 1) * T)
        pair, hi = hd // 2, hd % 2
        qp = p["q"][:, 128 * pair:128 * (pair + 1)]
        ktp = kt_ref[128 * pair:128 * (pair + 1), rows]
        kth = jnp.where((sub >= 64) if hi else (sub < 64), ktp, zero_k)
        vaug = jnp.concatenate([v_ref[rows, 128 * hd:128 * (hd + 1)], ones_b], axis=1)
        cst = c_ref[hd]
        sqk = _dot(qp, kth)
        inter = _dot(qp, cst.astype(BF16))
        kte = (kth.astype(F32) * p["eend_row"][hd:hd + 1, :]).astype(BF16)
        c_ref[hd] = p["sp_col"][hd:hd + 1, :] * cst + p["sc_col"][hd:hd + 1, :] * _dot(kte, vaug)
        return sqk, inter, vaug

    items = [(c, hd) for c in order for hd in range(8)]
    nxt = head_matmuls(*items[0])
    for n, (c, hd) in enumerate(items):
        p = pro[c]
        rows = slice(c * T, (c + 1) * T)
        cols = slice(128 * hd, 128 * (hd + 1))
        sqk, inter, vaug = nxt
        if n + 1 < len(items):
            nxt = head_matmuls(*items[n + 1])
        bh = p["b_bc"][:, cols]
        dlog = jnp.where(p["tri"], bh - p["b_row"][hd:hd + 1, :] + p["ig_row"][hd:hd + 1, :], NEG_INF)
        gh = bh + p["m_row"][:, hd:hd + 1]
        mstar = jnp.maximum(gh, jnp.max(dlog, axis=-1, keepdims=True))
        w = (jnp.exp(dlog - mstar) * sqk).astype(BF16)
        intra = _dot(w, vaug)
        e_int = jnp.exp(gh - mstar)
        den = jnp.maximum(jnp.abs(intra[:, 128:256] + e_int * inter[:, 128:256]), jnp.exp(-mstar))
        hh = (intra[:, 0:128] + e_int * inter[:, 0:128]) / den
        if rev:
            hh = hh + hf_ref[rows, cols]
            ms = jnp.mean(hh * hh, axis=-1, keepdims=True)
            hh = hh * lax.rsqrt(ms + EPS) * hn_ref[:, cols]
            og = og_ref[rows, cols].astype(F32)
            o_ref[rows, cols] = (hh * _sigmoid(og)).astype(BF16)
        else:
            o_ref[rows, cols] = hh
    mc_ref[...] = jnp.broadcast_to(m_col, mc_ref.shape)
    mr_ref[...] = jnp.broadcast_to(jnp.concatenate([m_row, jnp.zeros((1, 120), F32)], axis=1), mr_ref.shape)


def _mlstm(rev, q, kt, v, gc, gr, p, ncc, nlc, hf=None, og=None):
    nt = q.shape[0]
    assert ncc % SCAN_CHUNKS == 0 and nlc % SCAN_CHUNKS == 0
    ncc, nlc = ncc // SCAN_CHUNKS, nlc // SCAN_CHUNKS
    rows = SCAN_CHUNKS * T
    cmap = _scan_chunk_map(rev, ncc, nlc)
    blk = lambda w: pl.BlockSpec((rows, w), lambda j: (cmap(j), 0))
    blk_t = lambda h: pl.BlockSpec((h, rows), lambda j: (0, cmap(j)))
    in_specs = [blk(512), blk_t(512), blk(1024), blk(32), blk_t(32)]
    args = [q, kt, v, gc, gr]
    if rev:
        in_specs += [blk(1024), blk(1024), _full((1, 1024))]
        args += [hf, og, p["head_norm"]]
    return pl.pallas_call(
        functools.partial(_mlstm_kernel, rev),
        grid=(ncc + nlc,),
        in_specs=in_specs,
        out_specs=blk(1024),
        out_shape=jax.ShapeDtypeStruct((nt, 1024), BF16 if rev else F32),
        scratch_shapes=[pltpu.VMEM((8, 128, 256), F32), pltpu.VMEM((8, 128), F32), pltpu.VMEM((8, 128), F32)],
        compiler_params=_cparams("arbitrary"),
        name="mlstm_bwd" if rev else "mlstm_fwd",
    )(*args)


def _route(logits_t, rb_col):
    scores = _sigmoid(logits_t)
    biased = scores + rb_col
    row = lambda a, e: a[e:e + 1, :]
    gscore = []
    for g in range(4):
        b0, b1, b2, b3 = (row(biased, 4 * g + e) for e in range(4))
        h1, l1 = jnp.maximum(b0, b1), jnp.minimum(b0, b1)
        h2, l2 = jnp.maximum(b2, b3), jnp.minimum(b2, b3)
        gscore.append(jnp.maximum(h1, h2) + jnp.maximum(jnp.minimum(h1, h2), jnp.maximum(l1, l2)))
    gidx = jnp.zeros_like(gscore[0], dtype=I32)
    best = gscore[0]
    for g in range(1, 4):
        better = gscore[g] > best
        gidx = jnp.where(better, g, gidx)
        best = jnp.where(better, gscore[g], best)

    def pick(a, e):
        out = row(a, e)
        for g in range(1, 4):
            out = jnp.where(gidx == g, row(a, 4 * g + e), out)
        return out

    sb = [pick(biased, e) for e in range(4)]
    ss = [pick(scores, e) for e in range(4)]
    i1 = jnp.zeros_like(gidx)
    v1, s1 = sb[0], ss[0]
    for e in range(1, 4):
        better = sb[e] > v1
        i1 = jnp.where(better, e, i1)
        v1 = jnp.where(better, sb[e], v1)
        s1 = jnp.where(better, ss[e], s1)
    i2 = jnp.zeros_like(gidx)
    v2 = jnp.full_like(v1, NEG_INF)
    s2 = jnp.zeros_like(s1)
    for e in range(4):
        better = jnp.logical_and(i1 != e, sb[e] > v2)
        i2 = jnp.where(better, e, i2)
        v2 = jnp.where(better, sb[e], v2)
        s2 = jnp.where(better, ss[e], s2)
    tot = s1 + s2
    w1 = s1 / tot
    w2 = s2 / tot
    first_low = i1 < i2
    a = jnp.minimum(i1, i2)
    b = jnp.maximum(i1, i2)
    off = jnp.where(a == 0, 0, jnp.where(a == 1, 3, 5))
    bucket = 6 * gidx + off + (b - a - 1)
    return bucket, jnp.where(first_low, w1, w2), jnp.where(first_low, w2, w1)


def _out_kernel(nmix, nct, two_src, sub, t0, *refs):
    per = nmix + (2 if two_src else 1)
    tile_refs = [refs[per * u:per * (u + 1)] for u in range(sub)]
    (w_ref, mod_ref, g_ref, rw_ref, rwh_ref, rb_ref,
     xmid_ref, hrow_ref, bucket_ref, rank_ref, cnt_ref, cnt_scr) = refs[per * sub:]
    i = pl.program_id(0)

    @pl.when(i == 0)
    def _():
        cnt_scr[...] = jnp.zeros_like(cnt_scr)

    brow = lax.broadcasted_iota(I32, (NB_PAD, TM), 0)
    r = lax.broadcasted_iota(I32, (TM, TM), 0)
    c = lax.broadcasted_iota(I32, (TM, TM), 1)
    before = (r < c).astype(F32).astype(BF16)
    onehots = []
    for u in range(sub):
        mix_refs = tile_refs[u][:nmix]
        is_ctx = sub * i + u + t0 < nct
        mod = jnp.where(is_ctx, mod_ref[1], mod_ref[0])
        mix = mix_refs[0][...] if nmix == 1 else jnp.concatenate([mr[...] for mr in mix_refs], axis=1)
        if two_src:
            x = jnp.where(is_ctx, tile_refs[u][nmix][...], tile_refs[u][nmix + 1][...])
        else:
            x = tile_refs[u][nmix][...]
        x_mid = x + mod[2:3, :] * _dot(mix, w_ref[...])
        xmid_ref[TM * u:TM * (u + 1), :] = x_mid
        h = _norm_mod(x_mid, g_ref[...], mod[4:5, :], mod[3:4, :])
        hb = h.astype(BF16)
        h_lo = (h - hb.astype(F32)).astype(BF16)
        logits = _hi_lo_cols(_dot(hb, rw_ref[...]), _dot(h_lo, rwh_ref[...]), 16)
        logits_t = jnp.transpose(logits)[0:16, :]
        bucket = _route(logits_t, rb_ref[...])[0]
        hrow_u = hrow_ref.at[pl.ds(TM * TOK * u, TM * TOK), :]
        for cblk in range(TOK):
            _tok_store(hrow_u, cblk, h[:, 128 * cblk:128 * (cblk + 1)])
        bucket_ref[u] = bucket
        onehots.append((brow == bucket).astype(F32))
    cnt = cnt_scr[...]
    for u in range(sub):
        onehot = onehots[u]
        cum = _dot(onehot.astype(BF16), before)
        rank_ref[u] = jnp.sum(onehot * (cum + cnt[:, 0:1]), axis=0, keepdims=True).astype(I32)
        cnt = cnt + jnp.sum(onehot, axis=1, keepdims=True)
    cnt_scr[...] = cnt
    cnt_ref[...] = cnt.astype(I32)


def _out_proj(mixes, w_out, xs, mod, g_ffn, rw, rw_hi, rb_col, t0, ntiles, nct):
    nmix = len(mixes)
    two_src = len(xs) == 2
    n = ntiles * TM
    sub = max(s for s in (5, 4, 3, 2, 1) if ntiles % s == 0)
    in_specs, args = [], []
    for u in range(sub):
        tile_of = lambda i, u=u: sub * i + u + t0
        for mx in mixes:
            in_specs.append(pl.BlockSpec((TM, mx.shape[1]), lambda i, f=tile_of: (f(i), 0)))
            args.append(mx)
        if two_src:
            in_specs += [pl.BlockSpec((TM, D), lambda i, f=tile_of: (jnp.minimum(f(i), nct - 1), 0)),
                         pl.BlockSpec((TM, D), lambda i, f=tile_of: (jnp.maximum(f(i) - nct, 0), 0))]
        else:
            in_specs.append(pl.BlockSpec((TM, D), lambda i, f=tile_of: (f(i), 0)))
        args += list(xs)
    in_specs += [_full((D, D)), _full((2, 6, D)), _full((1, D)), _full((D, 128)), _full((D, 128)), _full((16, 1))]
    rows_out = pl.BlockSpec((sub, 1, TM), lambda i: (i, 0, 0))
    return pl.pallas_call(
        functools.partial(_out_kernel, nmix, nct, two_src, sub, t0),
        grid=(ntiles // sub,),
        in_specs=in_specs,
        out_specs=[pl.BlockSpec((sub * TM, D), lambda i: (i, 0)),
                   pl.BlockSpec((sub * TM * TOK, 128), lambda i: (i, 0)), rows_out, rows_out,
                   _full((NB_PAD, 128))],
        out_shape=[jax.ShapeDtypeStruct((n, D), F32), jax.ShapeDtypeStruct((n * TOK, 128), F32),
                   jax.ShapeDtypeStruct((ntiles, 1, TM), I32), jax.ShapeDtypeStruct((ntiles, 1, TM), I32),
                   jax.ShapeDtypeStruct((NB_PAD, 128), I32)],
        scratch_shapes=[pltpu.VMEM((NB_PAD, 128), F32)],
        compiler_params=_cparams("arbitrary"),
        name="out_proj_router",
    )(*args, w_out, mod, g_ffn, rw, rw_hi, rb_col)


def _scatter_kernel(ntiles, pos_ref, flo_ref, fhi_ref, src_ref, dst_ref, hbuf, in_sem, out_sem):
    i = pl.program_id(0)
    slot = i % 3
    h_ref = hbuf.at[slot]
    sem = out_sem.at[slot]
    rows = TM * TOK

    def load(tile, s):
        return pltpu.make_async_copy(src_ref.at[pl.ds(pl.multiple_of(tile * rows, rows), rows), :],
                                     hbuf.at[s], in_sem.at[s])

    def tok(ref, t):
        return ref.at[pl.ds(pl.multiple_of(t * TOK, TOK), TOK), :]

    def copy(r, d_row, src=h_ref, sm=sem):
        return pltpu.make_async_copy(tok(src, r), tok(dst_ref, d_row), sm)

    def wait_rows(lo, hi, unroll, s=slot):
        def body(r, carry):
            copy(0, 0, hbuf.at[s], out_sem.at[s]).wait()
            return carry
        lax.fori_loop(lo, hi, body, 0, unroll=unroll)

    @pl.when(i == 0)
    def _():
        load(0, 0).start()
        if ntiles > 1:
            load(1, 1).start()

    load(i, slot).wait()

    def start(r8, c):
        for k in range(8):
            r = r8 * 8 + k
            copy(r, pos_ref[i * TM + r]).start(priority=k % 2)
        return c
    lax.fori_loop(0, TM // 8, start, 0)

    @pl.when(i > 0)
    def _():
        wait_rows(0, TM, 8, (i + 2) % 3)

    @pl.when(i + 2 < ntiles)
    def _():
        load(i + 2, (i + 2) % 3).start()

    @pl.when(i == ntiles - 1)
    def _():
        wait_rows(0, TM, 8)
        for b in range(N_BUCKETS):
            lo, hi = flo_ref[b], fhi_ref[b]

            def fill(r, c):
                copy(0, r).start()
                return c
            lax.fori_loop(lo, hi, fill, 0)
            wait_rows(lo, hi, 1)

        def tile_copy(j):
            rows = TMM * TOK
            return pltpu.make_async_copy(h_ref, dst_ref.at[pl.ds(pl.multiple_of(j * rows, rows), rows), :], sem)

        def fill_tile(j, c):
            tile_copy(j).start()
            return c

        def wait_tile(j, c):
            tile_copy(j).wait()
            return c
        lax.fori_loop(flo_ref[N_BUCKETS], fhi_ref[N_BUCKETS], fill_tile, 0)
        lax.fori_loop(flo_ref[N_BUCKETS], fhi_ref[N_BUCKETS], wait_tile, 0)


def _scatter_rows(hrow, pos, fill_lo, fill_hi, n, npad):
    assert TM == TMM
    return pl.pallas_call(
        functools.partial(_scatter_kernel, n // TM),
        grid_spec=pltpu.PrefetchScalarGridSpec(
            num_scalar_prefetch=3,
            grid=(n // TM,),
            in_specs=[pl.BlockSpec(memory_space=pl.ANY)],
            out_specs=pl.BlockSpec(memory_space=pl.ANY),
            scratch_shapes=[pltpu.VMEM((3, TM * TOK, 128), hrow.dtype), pltpu.SemaphoreType.DMA((3,)),
                            pltpu.SemaphoreType.DMA((3,))],
        ),
        out_shape=jax.ShapeDtypeStruct((npad * TOK, 128), hrow.dtype),
        compiler_params=_cparams("arbitrary"),
        name="moe_scatter_rows",
    )(pos, fill_lo, fill_hi, hrow)


def _combine_kernel(ntiles, pos_ref, x_ref, mod_ref, ys_ref, o_ref, ybuf, sem):
    i = pl.program_id(0)

    def copy(tile, slot, r):
        src = pl.multiple_of(pos_ref[tile * TM + r] * TOK, TOK)
        dst = pl.multiple_of(r * TOK, TOK)
        return pltpu.make_async_copy(ys_ref.at[pl.ds(src, TOK), :], ybuf.at[slot, pl.ds(dst, TOK), :], sem.at[slot])

    def start_tile(tile, slot):
        def body(r8, carry):
            for k in range(8):
                copy(tile, slot, r8 * 8 + k).start(priority=k % 2)
            return carry
        lax.fori_loop(0, TM // 8, body, 0)

    @pl.when(i == 0)
    def _():
        start_tile(0, 0)

    @pl.when(i + 1 < ntiles)
    def _():
        start_tile(i + 1, (i + 1) % 2)

    slot = i % 2

    def wait_body(r, carry):
        copy(i, slot, 0).wait()
        return carry
    lax.fori_loop(0, TM, wait_body, 0, unroll=8)
    o_ref[...] = x_ref[...] + mod_ref[0, 5:6, :] * _tok_rows(ybuf.at[slot], TM)


def _combine(x_mid, mod, ys, pos, t0, nct):
    n = x_mid.shape[0]
    ntiles = n // TM
    return pl.pallas_call(
        functools.partial(_combine_kernel, ntiles),
        grid_spec=pltpu.PrefetchScalarGridSpec(
            num_scalar_prefetch=1,
            grid=(ntiles,),
            in_specs=[pl.BlockSpec((TM, D), lambda i, *_: (i, 0)),
                      pl.BlockSpec((1, 6, D), lambda i, *_: (jnp.where(i + t0 < nct, 1, 0), 0, 0)),
                      pl.BlockSpec(memory_space=pl.ANY)],
            out_specs=pl.BlockSpec((TM, D), lambda i, *_: (i, 0)),
            scratch_shapes=[pltpu.VMEM((2, TM * TOK, 128), F32), pltpu.SemaphoreType.DMA((2,))],
        ),
        out_shape=jax.ShapeDtypeStruct((n, D), F32),
        compiler_params=_cparams("arbitrary"),
        name="moe_combine",
    )(pos, x_mid, mod, ys)


def _moe_kernel(tea_ref, teb_ref, tblk_ref, tval_ref, x_ref, rw_ref, rwh_ref, w1a_ref, w3a_ref, w2a_ref,
                w1b_ref, w3b_ref, w2b_ref, y_ref):
    del tblk_ref
    j = pl.program_id(0)

    @pl.when(tval_ref[j] != 0)
    def _():
        h = _tok_rows(x_ref, TMM)
        hb = h.astype(BF16)
        h_lo = (h - hb.astype(F32)).astype(BF16)
        logits = _hi_lo_cols(_dot(hb, rw_ref[...]), _dot(h_lo, rwh_ref[...]), 16)
        scores = _sigmoid(logits)
        lane = lax.broadcasted_iota(I32, scores.shape, 1)
        s_a = jnp.sum(jnp.where(lane == tea_ref[j], scores, 0.0), axis=1, keepdims=True)
        s_b = jnp.sum(jnp.where(lane == teb_ref[j], scores, 0.0), axis=1, keepdims=True)
        gates = (s_a / (s_a + s_b), s_b / (s_a + s_b))
        acts = []
        for w1_ref, w3_ref, gate in ((w1a_ref, w3a_ref, gates[0]), (w1b_ref, w3b_ref, gates[1])):
            u = _dot(hb, w1_ref[0, 0].astype(BF16))
            v = _dot(hb, w3_ref[0, 0].astype(BF16))
            acts.append((_silu(u) * v * gate).astype(BF16))
        y = _dot(acts[0], w2a_ref[0, 0].astype(BF16)) + _dot(acts[1], w2b_ref[0, 0].astype(BF16))
        for cblk in range(TOK):
            _tok_store(y_ref, cblk, y[:, 128 * cblk:128 * (cblk + 1)])

    @pl.when(tval_ref[j] == 0)
    def _():
        y_ref[...] = jnp.zeros_like(y_ref)


def _moe(xs_sorted, rw, rw_hi, w1, w3, w2, layer, tile_ea, tile_eb, tile_blk, tile_valid):
    npad = xs_sorted.shape[0] // TOK
    ntile = npad // TMM
    wspec = lambda shape, which: pl.BlockSpec(
        (1, 1) + shape, lambda j, ea, eb, blk, val: (layer, (ea, eb)[which][j], 0, 0))
    up, down = (D, D_EXPERT), (D_EXPERT, D)
    return pl.pallas_call(
        _moe_kernel,
        grid_spec=pltpu.PrefetchScalarGridSpec(
            num_scalar_prefetch=4,
            grid=(ntile,),
            in_specs=[pl.BlockSpec((TMM * TOK, 128), lambda j, ea, eb, blk, val: (blk[j], 0)),
                      pl.BlockSpec((D, 128), lambda j, *_: (0, 0)), pl.BlockSpec((D, 128), lambda j, *_: (0, 0)),
                      wspec(up, 0), wspec(up, 0), wspec(down, 0), wspec(up, 1), wspec(up, 1), wspec(down, 1)],
            out_specs=pl.BlockSpec((TMM * TOK, 128), lambda j, ea, eb, blk, val: (j, 0)),
        ),
        out_shape=jax.ShapeDtypeStruct((npad * TOK, 128), F32),
        compiler_params=_cparams("arbitrary"),
        name="moe_experts",
    )(tile_ea, tile_eb, tile_blk, tile_valid, xs_sorted, rw, rw_hi, w1, w3, w2, w1, w3, w2)


def _moe_block(x_mid, hrow, bucket, rank, counts, mod, rw, rw_hi, w1, w3, w2, layer, t0, nct):
    n = hrow.shape[0] // TOK
    ntile = n // TMM + N_BUCKETS
    npad = ntile * TMM
    cnt = counts[:N_BUCKETS, 0]
    padded = ((cnt + TMM - 1) // TMM) * TMM
    ends = jnp.cumsum(padded)
    starts = ends - padded
    total_tiles = ends[-1] // TMM
    tiles = jnp.arange(ntile, dtype=I32)
    tile_valid = (tiles < total_tiles).astype(I32)
    tile_blk = jnp.minimum(tiles, jnp.maximum(total_tiles - 1, 0))
    tile_bucket = jnp.minimum(jnp.sum((ends[None, :] <= (tile_blk * TMM)[:, None]).astype(I32), axis=1), N_BUCKETS - 1)
    pair = tile_bucket % 6
    grp = tile_bucket // 6
    pair_a = jnp.where(pair < 3, 0, jnp.where(pair < 5, 1, 2))
    pair_b = jnp.where(pair < 3, pair + 1, jnp.where(pair < 5, pair - 1, 3))
    tile_ea = (4 * grp + pair_a).astype(I32)
    tile_eb = (4 * grp + pair_b).astype(I32)
    bucket = bucket.reshape(-1)
    onehot = (bucket[:, None] == jnp.arange(N_BUCKETS, dtype=I32)[None, :]).astype(I32)
    pos = (rank.reshape(-1) + jnp.sum(onehot * starts[None, :], axis=1)).astype(I32)
    pad32 = lambda a, tail: jnp.zeros((NB_PAD,), I32).at[:N_BUCKETS].set(a.astype(I32)).at[N_BUCKETS].set(tail)
    fill_lo, fill_hi = pad32(starts + cnt, total_tiles), pad32(ends, ntile)
    xs_sorted = _scatter_rows(hrow, pos, fill_lo, fill_hi, n, npad)
    ys = _moe(xs_sorted, rw, rw_hi, w1, w3, w2, layer, tile_ea, tile_eb, tile_blk, tile_valid)
    return _combine(x_mid, mod, ys, pos, t0, nct)


def kernel(x, c, ctx, c_ctx, router_w, router_b, norm_mix, norm_ffn, w_mod, b_mod, ev_w_in, ev_conv_w, ev_conv_b, ev_dt_bias, ev_a_log, ev_d_skip, ev_ssd_norm, ev_q_norm, ev_k_norm, ev_sink, ev_w_out, od_w_in, od_conv_w, od_conv_b, od_igate_b, od_fgate_b, od_head_norm, od_w_out, moe_w1, moe_w3, moe_w2):
    s_len = x.shape[1]
    c_len = ctx.shape[1]
    assert x.shape[0] == 1 and s_len % TM == 0 and c_len % TM == 0 and s_len % GRID_W == 0
    nct = c_len // TM
    ncc, nlc = c_len // T, s_len // T
    nt = c_len + s_len
    ntiles = nt // TM

    mod = _modulation(c, c_ctx, w_mod, b_mod)
    rw, rw_hi = _hi_lo_weight(router_w)
    rb_col = router_b.reshape(N_EXPERTS, 1)
    pad128 = lambda v: jnp.zeros((1, 128), F32).at[0, :v.shape[0]].set(v)

    w = ev_w_in[0]
    cos, sin = _rope_tables(s_len)
    wdt, wdt_hi = _hi_lo_weight(w[:, 1280:1296])
    p0 = dict(
        g_mix=norm_mix[0].reshape(1, D),
        wcat=jnp.concatenate([w[:, 0:1280].astype(BF16), w[:, 1296:2064].astype(BF16), wdt], axis=1),
        wdt_hi=wdt_hi,
        conv_w=ev_conv_w[0], conv_b=ev_conv_b[0].reshape(1, 768),
        dt_bias=pad128(ev_dt_bias[0].reshape(16)),
        q_norm=jnp.tile(ev_q_norm[0], 8).reshape(1, 512), k_norm=jnp.tile(ev_k_norm[0], 2).reshape(1, 128),
        cos=cos, sin=sin,
        alog_row=ev_a_log[0].reshape(1, 16), alog_col=ev_a_log[0].reshape(16, 1),
        d_skip=jnp.repeat(ev_d_skip[0], 64).reshape(1, 512), ssd_norm=ev_ssd_norm[0].reshape(1, 512))
    z, xs, bc, q, kk, vv, dtc, dtr = _in0(ctx[0], x[0], mod[0], p0, nct)
    yf = _ssd(False, xs, bc, dtc, dtr, p0, ncc, nlc)
    ymix = _ssd(True, xs, bc, dtc, dtr, p0, ncc, nlc, yf=yf, z=z)
    att = _attention(q, kk, vv, ev_sink[0].reshape(1, 8), ncc, nlc)
    x_mid0, hrow, bucket, rank, counts = _out_proj(
        [ymix, att], ev_w_out[0].astype(BF16), [ctx[0], x[0]], mod[0], norm_ffn[0].reshape(1, D),
        rw, rw_hi, rb_col, 0, ntiles, nct)
    x1 = _moe_block(x_mid0, hrow, bucket, rank, counts, mod[0], rw, rw_hi, moe_w1, moe_w3, moe_w2, 0, 0, nct)

    w = od_w_in[0]
    wg, wg_hi = _hi_lo_weight(w[:, 3072:3104])
    p1 = dict(
        g_mix=norm_mix[1].reshape(1, D),
        wcat=jnp.concatenate([w[:, 0:3072].astype(BF16), wg], axis=1),
        wg_hi=wg_hi,
        conv_w=od_conv_w[0], conv_b=od_conv_b[0].reshape(1, 2048),
        gate_bias=pad128(jnp.concatenate([od_igate_b[0].reshape(16), od_fgate_b[0].reshape(16)])),
        head_norm=od_head_norm[0].reshape(1, 1024))
    q1, kt1, v1, og1, gc1, gr1 = _in1(x1, mod[1], p1, nct)
    hf = _mlstm(False, q1, kt1, v1, gc1, gr1, p1, ncc, nlc)
    hmix = _mlstm(True, q1, kt1, v1, gc1, gr1, p1, ncc, nlc, hf=hf, og=og1)
    x_mid1, hrow, bucket, rank, counts = _out_proj(
        [hmix], od_w_out[0].astype(BF16), [x1], mod[1], norm_ffn[1].reshape(1, D),
        rw, rw_hi, rb_col, nct, ntiles - nct, nct)
    return _moe_block(x_mid1, hrow, bucket, rank, counts, mod[1], rw, rw_hi, moe_w1, moe_w3, moe_w2, 1, nct,
                      nct)[None]
```

```python
import functools
import math

import jax
import jax.numpy as jnp
from jax import lax
from jax.experimental import pallas as pl
from jax.experimental.pallas import tpu as pltpu

F32 = jnp.float32
BF16 = jnp.bfloat16
I32 = jnp.int32

EPS = 1e-6
D = 1024
T = 128
SCAN_CHUNKS = 2
TM = 256
TMM = 256
GRID_W = 64
ROPE_THETA = 10000.0
N_EXPERTS = 16
N_BUCKETS = 24
NB_PAD = 32
D_EXPERT = 512
TOK = 8
U32 = jnp.uint32
NEG_INF = float("-inf")
VMEM_LIMIT = 56 * 1024 * 1024

_NN = (((1,), (0,)), ((), ()))
_NT = (((1,), (1,)), ((), ()))
_TN = (((0,), (0,)), ((), ()))

_PAIR_A = (0, 0, 0, 1, 1, 2)
_PAIR_B = (1, 2, 3, 2, 3, 3)


def _cparams(*sem):
    return pltpu.CompilerParams(dimension_semantics=sem, vmem_limit_bytes=VMEM_LIMIT)


def _dot(a, b, dims=_NN):
    return lax.dot_general(a, b, dims, preferred_element_type=F32)


def _split(a, n):
    out = []
    r = a
    for _ in range(n):
        t = r.astype(BF16)
        out.append(t)
        r = r - t.astype(F32)
    return out


def _mdot(as_, bs, dims=_NN, order=None):
    if order is None:
        order = len(as_) + len(bs) - 2
    acc = None
    for i, a in enumerate(as_):
        for j, b in enumerate(bs):
            if i + j <= order:
                p = _dot(a, b, dims)
                acc = p if acc is None else acc + p
    return acc


def _sigmoid(x):
    return 1.0 / (1.0 + jnp.exp(-x))


def _silu(x):
    return x * _sigmoid(x)


def _log1p_exp_neg_abs(x):
    e = jnp.exp(-jnp.abs(x))
    u = 1.0 + e
    um1 = u - 1.0
    return jnp.where(um1 == 0.0, e, jnp.log(u) * (e / jnp.where(um1 == 0.0, 1.0, um1)))


def _softplus(x):
    return jnp.maximum(x, 0.0) + _log1p_exp_neg_abs(x)


def _log_sigmoid(x):
    return jnp.minimum(x, 0.0) - _log1p_exp_neg_abs(x)


def _norm_mod(x, g, sc, sh):
    ms = jnp.mean(x * x, axis=-1, keepdims=True)
    return (x * lax.rsqrt(ms + EPS)) * g * (1.0 + sc) + sh


def _tok_load(ref, chunk, n):
    return ref[pl.ds(chunk, n, stride=TOK), :]


def _tok_store(ref, chunk, val):
    ref[pl.ds(chunk, val.shape[0], stride=TOK), :] = val


def _tok_rows(ref, n):
    return jnp.concatenate([_tok_load(ref, c, n) for c in range(TOK)], axis=1)


def _tri(rev):
    r = lax.broadcasted_iota(I32, (T, T), 0)
    c = lax.broadcasted_iota(I32, (T, T), 1)
    return (c >= r) if rev else (c <= r)


def _cumsums(rev, col, row):
    tri = _tri(rev)
    tri_b = tri.astype(F32).astype(BF16)
    trit_b = _tri(not rev).astype(F32).astype(BF16)
    ccol = _mdot([tri_b], _split(col, 3))
    crow = _mdot(_split(row, 3), [trit_b])
    return tri, ccol, crow


def _lane_bcast(col):
    n = col.shape[1]
    col = jnp.concatenate([col, jnp.zeros((col.shape[0], 128 - n), F32)], axis=1)
    r = lax.broadcasted_iota(I32, (128, n * 128), 0)
    c = lax.broadcasted_iota(I32, (128, n * 128), 1) // 128
    return _mdot(_split(col, 3), [(r == c).astype(F32).astype(BF16)])


def _mod_kernel(c_ref, w_ref, b_ref, o_ref):
    a = _silu(c_ref[...])
    o_ref[0] = _mdot(_split(a, 2), _split(w_ref[0], 2), order=1) + b_ref[0]


def _modulation(c, c_ctx, w_mod, b_mod):
    depth = w_mod.shape[0]
    n = w_mod.shape[2]
    tn = 1536
    cc = jnp.zeros((8, D), F32).at[0].set(c[0]).at[1].set(c_ctx)
    out = pl.pallas_call(
        _mod_kernel,
        grid=(depth, n // tn),
        in_specs=[
            pl.BlockSpec((8, D), lambda l, j: (0, 0)),
            pl.BlockSpec((1, D, tn), lambda l, j: (l, 0, j)),
            pl.BlockSpec((1, 1, tn), lambda l, j: (l, 0, j)),
        ],
        out_specs=pl.BlockSpec((1, 8, tn), lambda l, j: (l, 0, j)),
        out_shape=jax.ShapeDtypeStruct((depth, 8, n), F32),
        compiler_params=_cparams("arbitrary", "arbitrary"),
        name="modulation",
    )(cc, w_mod, b_mod.reshape(depth, 1, n))
    return out[:, :2].reshape(depth, 2, 6, D)


def _halo_specs(nrows, tile_of):
    nb8 = nrows // 8
    return [
        pl.BlockSpec((8, D), lambda i: (jnp.maximum(tile_of(i) * (TM // 8) - 1, 0), 0)),
        pl.BlockSpec((TM, D), lambda i: (tile_of(i), 0)),
        pl.BlockSpec((8, D), lambda i: (jnp.minimum((tile_of(i) + 1) * (TM // 8), nb8 - 1), 0)),
    ]


def _ctx_tile(nct):
    return lambda i: jnp.minimum(i, nct - 1)


def _lat_tile(nct):
    return lambda i: jnp.maximum(i - nct, 0)


def _mod_spec(nct, t0=0):
    return pl.BlockSpec((1, 6, D), lambda i: (jnp.where(i + t0 < nct, 1, 0), 0, 0))


def _full(shape):
    nd = len(shape)
    return pl.BlockSpec(shape, lambda i: (0,) * nd)


def _seq_edges(i, nct, ntiles):
    prev_ok = jnp.logical_and(i != 0, i != nct).astype(F32)
    next_ok = jnp.logical_and(i != nct - 1, i != ntiles - 1).astype(F32)
    return prev_ok, next_ok


def _conv_silu(x, x_first_prev, x_last_next, cw, cb):
    n = x.shape[0]
    rows = lax.broadcasted_iota(I32, x.shape, 0)
    x_prev = jnp.where(rows == 0, x_first_prev, pltpu.roll(x, 1, 0))
    x_next = jnp.where(rows == n - 1, x_last_next, pltpu.roll(x, n - 1, 0))
    return _silu(x_prev * cw[0:1] + x * cw[1:2] + x_next * cw[2:3] + cb)


def _hi_lo_cols(blk, lo_pass, n):
    return blk + pltpu.roll(blk, 128 - n, 1) + lo_pass


def _hi_lo_weight(w):
    n = w.shape[1]
    hi = w.astype(BF16)
    lo = (w - hi.astype(F32)).astype(BF16)
    z = jnp.zeros((w.shape[0], 128 - 2 * n), BF16)
    return jnp.concatenate([hi, lo, z], axis=1), jnp.concatenate([hi, jnp.zeros_like(lo), z], axis=1)


def _head_rms(xf, gamma):
    r = lax.broadcasted_iota(I32, (128, 128), 0) // 64
    c = lax.broadcasted_iota(I32, (128, 128), 1) // 64
    ones_bd = (r == c).astype(F32).astype(BF16)
    outs = []
    for j in range(xf.shape[1] // 128):
        blk = xf[:, 128 * j:128 * (j + 1)]
        ssum = _dot((blk * blk).astype(BF16), ones_bd)
        outs.append(blk * lax.rsqrt(ssum * (1.0 / 64.0) + EPS))
    return jnp.concatenate(outs, axis=1) * gamma


def _rope(xf, cos, sin):
    lane = lax.broadcasted_iota(I32, (xf.shape[0], 128), 1)
    first = (lane % 32) < 16
    outs = []
    for j in range(xf.shape[1] // 128):
        blk = xf[:, 128 * j:128 * (j + 1)]
        partner = jnp.where(first, pltpu.roll(blk, 112, 1), pltpu.roll(blk, 16, 1))
        outs.append(blk * cos + partner * sin)
    return jnp.concatenate(outs, axis=1)


def _rope_tables(s_len):
    lane = jnp.arange(128, dtype=I32)
    inv = ROPE_THETA ** (-(lane % 16).astype(F32) / 16.0)
    sign = jnp.where((lane % 32) < 16, -1.0, 1.0).astype(F32)
    ang_r = jnp.arange(s_len // GRID_W, dtype=F32)[:, None] * inv[None, :]
    ang_c = jnp.tile(jnp.arange(GRID_W, dtype=F32), TM // GRID_W)[:, None] * inv[None, :]
    both = lambda ang: jnp.stack([jnp.cos(ang), jnp.sin(ang) * sign[None, :]])
    return both(ang_r), both(ang_c)


def _in0_kernel(nct, ntiles, cp_ref, c_ref, cn_ref, xp_ref, x_ref, xn_ref, mod_ref, g_ref, wa_ref, wb_ref,
                wdt_ref, wdth_ref, cw_ref, cb_ref, dtb_ref, qn_ref, kn_ref, rowcs_ref, colcs_ref,
                z_ref, xs_ref, bc_ref, q_ref, kk_ref, vv_ref, dtc_ref, dtr_ref):
    i = pl.program_id(0)
    is_ctx = i < nct
    sh = mod_ref[0, 0:1, :]
    sc = mod_ref[0, 1:2, :]
    g = g_ref[...]
    x_all = jnp.concatenate([jnp.where(is_ctx, cp_ref[...], xp_ref[...]),
                             jnp.where(is_ctx, c_ref[...], x_ref[...]),
                             jnp.where(is_ctx, cn_ref[...], xn_ref[...])], axis=0)
    h_all = _norm_mod(x_all, g, sc, sh)
    hb_all = h_all.astype(BF16)
    zx_all = _dot(hb_all, wa_ref[...])
    h, hb, zx = h_all[8:8 + TM], hb_all[8:8 + TM], zx_all[8:8 + TM]
    qkv = _dot(hb, wb_ref[...])
    prev_ok, next_ok = _seq_edges(i, nct, ntiles)
    xb_prev = zx_all[7:8, 512:1280] * prev_ok
    xb_next = zx_all[8 + TM:9 + TM, 512:1280] * next_ok
    act = _conv_silu(zx[:, 512:1280], xb_prev, xb_next, cw_ref[...], cb_ref[...])
    z_ref[...] = zx[:, 0:512].astype(BF16)
    xs_ref[...] = act[:, 0:512].astype(BF16)
    bc_ref[...] = act[:, 512:768].astype(BF16)
    row0 = jnp.maximum(i - nct, 0) * (TM // GRID_W)
    rowcs = [jnp.concatenate([jnp.broadcast_to(rowcs_ref[t, pl.ds(row0 + kq, 1), :], (GRID_W, 128))
                              for kq in range(TM // GRID_W)], axis=0) for t in range(2)]
    row_lanes = (lax.broadcasted_iota(I32, (TM, 128), 1) % 64) < 32
    cos = jnp.where(is_ctx, 1.0, jnp.where(row_lanes, rowcs[0], colcs_ref[0]))
    sin = jnp.where(is_ctx, 0.0, jnp.where(row_lanes, rowcs[1], colcs_ref[1]))
    q = _rope(_head_rms(qkv[:, 0:512], qn_ref[...]), cos, sin) * 0.125
    q_ref[...] = q.astype(BF16)
    k = _rope(_head_rms(qkv[:, 512:640], kn_ref[...]), cos, sin)
    kk_ref[...] = jnp.concatenate([k, pltpu.roll(k, 64, 1)], axis=1).astype(BF16)
    v = qkv[:, 640:768]
    vv_ref[...] = jnp.concatenate([v, pltpu.roll(v, 64, 1)], axis=1).astype(BF16)
    h_lo = (h - hb.astype(F32)).astype(BF16)
    dt = _softplus(_hi_lo_cols(_dot(hb, wdt_ref[...]), _dot(h_lo, wdth_ref[...]), 16) + dtb_ref[...])
    dtc_ref[...] = dt[:, 0:16]
    dtr_ref[...] = jnp.transpose(dt)[0:16, :]


def _in0(ctx2, x2, mod, p, nct):
    c_len, s_len = ctx2.shape[0], x2.shape[0]
    nt = c_len + s_len
    ntiles = nt // TM
    tile = lambda w: pl.BlockSpec((TM, w), lambda i: (i, 0))
    lat = _lat_tile(nct)
    lat_tile = lambda w: pl.BlockSpec((TM, w), lambda i: (lat(i), 0))
    outs = [(512, BF16), (512, BF16), (256, BF16), (512, BF16), (256, BF16), (256, BF16), (16, F32)]
    return pl.pallas_call(
        functools.partial(_in0_kernel, nct, ntiles),
        grid=(ntiles,),
        in_specs=_halo_specs(c_len, _ctx_tile(nct)) + _halo_specs(s_len, lat) + [
            _mod_spec(nct), _full((1, D)), _full((D, 1280)), _full((D, 768)), _full((D, 128)), _full((D, 128)),
            _full((3, 768)), _full((1, 768)), _full((1, 128)),
            _full((1, 512)), _full((1, 128)), _full((2, s_len // GRID_W, 128)), _full((2, TM, 128))],
        out_specs=[tile(w) for w, _ in outs] + [pl.BlockSpec((16, TM), lambda i: (0, i))],
        out_shape=[jax.ShapeDtypeStruct((nt, w), dt) for w, dt in outs]
        + [jax.ShapeDtypeStruct((16, nt), F32)],
        compiler_params=_cparams("arbitrary"),
        name="in_proj_even",
    )(ctx2, ctx2, ctx2, x2, x2, x2, mod, p["g_mix"], p["w_zx"], p["w_qkv"], p["wdt"], p["wdt_hi"],
      p["conv_w"], p["conv_b"], p["dt_bias"], p["q_norm"], p["k_norm"], p["rope_rows"], p["rope_cols"])


def _scan_chunk_map(rev, ncc, nlc):
    if not rev:
        return lambda j: j
    return lambda j: jnp.where(j < ncc, ncc - 1 - j, ncc + nlc - 1 - (j - ncc))


def _ssd_kernel(rev, *refs):
    if rev:
        (xs_ref, bc_ref, dtc_ref, dtr_ref, alr_ref, alc_ref, yf_ref, z_ref, dsk_ref, nrm_ref,
         o_ref, st_ref) = refs
    else:
        xs_ref, bc_ref, dtc_ref, dtr_ref, alr_ref, alc_ref, o_ref, st_ref = refs
    j = pl.program_id(0)

    @pl.when(j == 0)
    def _():
        st_ref[...] = jnp.zeros_like(st_ref)

    d = 8 if rev else 0
    a_coef_row = -jnp.exp(alr_ref[...])[:, d:d + 8]
    a_coef_col = -jnp.exp(alc_ref[...])[d:d + 8, :]
    lane = lax.broadcasted_iota(I32, (T, 128), 1)
    lo = lane < 64
    zero_b = jnp.zeros((T, 128), BF16)
    hi_half = jnp.logical_not(lo)
    sub = lax.broadcasted_iota(I32, (128, T), 0)
    eye = (sub == lax.broadcasted_iota(I32, (128, T), 1)).astype(F32).astype(BF16)
    end = 0 if rev else T - 1
    order = tuple(reversed(range(SCAN_CHUNKS))) if rev else tuple(range(SCAN_CHUNKS))

    def prologue(c):
        rows = slice(c * T, (c + 1) * T)
        dtc = dtc_ref[rows, d:d + 8]
        dtr = dtr_ref[d:d + 8, rows]
        tri, acs_col, acs_row = _cumsums(rev, dtc * a_coef_row, dtr * a_coef_col)
        atot_col = acs_row[:, end:end + 1]
        acs_bc = _lane_bcast(acs_col)
        dec_row = jnp.exp(atot_col - acs_row) * dtr
        xs = xs_ref[rows, :]
        bm = bc_ref[rows, 0:128]
        cm = bc_ref[rows, 128:256]
        cgs = [jnp.where(lo, cm, zero_b), jnp.where(hi_half, cm, zero_b)]
        cbs = [_dot(cgs[g], bm, _NT) for g in range(2)]
        bmt = _dot(eye, bm, _NT)
        bgts = [jnp.where(sub < 64, bmt, 0.0), jnp.where(sub >= 64, bmt, 0.0)]
        xpairs = []
        for pr in range(4):
            xp = xs[:, 128 * pr:128 * (pr + 1)]
            xpairs.append(jnp.concatenate([jnp.where(lo, xp, zero_b), jnp.where(hi_half, xp, zero_b)], axis=0))
        return dict(tri=tri, acs_row=acs_row, acs_bc=acs_bc, atot_col=atot_col, dec_row=dec_row, dtr=dtr,
                    xs=xs, cgs=cgs, cbs=cbs, bgts=bgts, xpairs=xpairs)

    pro = {c: prologue(c) for c in order}
    y_off = {}
    for c in order:
        p = pro[c]
        for pr in range(4):
            g, h0, h1 = pr // 2, 2 * pr, 2 * pr + 1
            st = st_ref[pr]
            eacs = jnp.exp(jnp.where(lo, p["acs_bc"][:, 128 * h0:128 * (h0 + 1)],
                                     p["acs_bc"][:, 128 * h1:128 * (h1 + 1)]))
            y_off[c, pr] = _dot(p["cgs"][g], st.astype(BF16)) * eacs
            bdec = jnp.concatenate([(p["bgts"][g] * p["dec_row"][h0:h0 + 1, :]).astype(BF16),
                                    (p["bgts"][g] * p["dec_row"][h1:h1 + 1, :]).astype(BF16)], axis=1)
            carry = jnp.where(lo[0:1, :], jnp.exp(p["atot_col"][h0:h0 + 1, :]), jnp.exp(p["atot_col"][h1:h1 + 1, :]))
            st_ref[pr] = carry * st + _dot(bdec, p["xpairs"][pr])
    for c in order:
        p = pro[c]
        rows = slice(c * T, (c + 1) * T)
        ys = []
        for pr in range(4):
            ms = []
            for hd in (2 * pr, 2 * pr + 1):
                diff = p["acs_bc"][:, 128 * hd:128 * (hd + 1)] - p["acs_row"][hd:hd + 1, :]
                lmat = jnp.exp(jnp.where(p["tri"], diff, NEG_INF))
                ms.append((p["cbs"][hd // 4] * lmat * p["dtr"][hd:hd + 1, :]).astype(BF16))
            ys.append(_dot(jnp.concatenate(ms, axis=1), p["xpairs"][pr]) + y_off[c, pr])
        y = jnp.concatenate(ys, axis=1)
        if not rev:
            o_ref[rows, :] = y
        else:
            ytot = y + yf_ref[rows, :] + dsk_ref[...] * p["xs"].astype(F32)
            gated = ytot * _silu(z_ref[rows, :].astype(F32))
            ms = jnp.mean(gated * gated, axis=-1, keepdims=True)
            o_ref[rows, :] = (gated * lax.rsqrt(ms + EPS) * nrm_ref[...]).astype(BF16)


def _ssd(rev, xs, bc, dtc, dtr, p, ncc, nlc, yf=None, z=None):
    nt = xs.shape[0]
    assert ncc % SCAN_CHUNKS == 0 and nlc % SCAN_CHUNKS == 0
    ncc, nlc = ncc // SCAN_CHUNKS, nlc // SCAN_CHUNKS
    rows = SCAN_CHUNKS * T
    cmap = _scan_chunk_map(rev, ncc, nlc)
    blk = lambda w: pl.BlockSpec((rows, w), lambda j: (cmap(j), 0))
    in_specs = [blk(512), blk(256), blk(16), pl.BlockSpec((16, rows), lambda j: (0, cmap(j))),
                _full((1, 16)), _full((16, 1))]
    args = [xs, bc, dtc, dtr, p["alog_row"], p["alog_col"]]
    if rev:
        in_specs += [blk(512), blk(512), _full((1, 512)), _full((1, 512))]
        args += [yf, z, p["d_skip"], p["ssd_norm"]]
    return pl.pallas_call(
        functools.partial(_ssd_kernel, rev),
        grid=(ncc + nlc,),
        in_specs=in_specs,
        out_specs=blk(512),
        out_shape=jax.ShapeDtypeStruct((nt, 512), BF16 if rev else F32),
        scratch_shapes=[pltpu.VMEM((4, 128, 128), F32)],
        compiler_params=_cparams("arbitrary"),
        name="ssd_bwd" if rev else "ssd_fwd",
    )(*args)


def _attn_kernel(ncc, nblk, q_ref, kp_ref, kc_ref, kn_ref, vp_ref, vc_ref, vn_ref, kx_ref, vx_ref,
                 sink_ref, o_ref):
    j = pl.program_id(0)
    c_len = kx_ref.shape[0]
    r = lax.broadcasted_iota(I32, (T, T), 0)
    c = lax.broadcasted_iota(I32, (T, T), 1)
    zero = jnp.zeros((T, T), F32)
    ninf = jnp.full((T, T), NEG_INF, F32)
    lo = lax.broadcasted_iota(I32, (T, 128), 1) < 64
    zero_b = jnp.zeros((T, 128), BF16)
    sink = sink_ref[...]
    kblk = [kp_ref[...], kc_ref[0:T, :], kc_ref[T:2 * T, :], kn_ref[...]]
    vblk = [vp_ref[...], vc_ref[0:T, :], vc_ref[T:2 * T, :], vn_ref[...]]
    stacks = [[hd for hd in range(8) if (hd // 4 + hd % 2) % 2 == b] for b in range(2)]
    s_all, v_all = {}, {}
    for qb in range(2):
        jb = 2 * j + qb
        is_lat = jb >= ncc
        prev_ok = jnp.logical_and(is_lat, jb >= ncc + 1)
        next_ok = jnp.logical_and(is_lat, jb <= nblk - 2)
        bias = jnp.concatenate([
            jnp.where(jnp.logical_and(prev_ok, c >= r), zero, ninf),
            jnp.where(is_lat, zero, ninf),
            jnp.where(jnp.logical_and(next_ok, c <= r), zero, ninf),
            jnp.zeros((T, c_len), F32)], axis=1)
        bias4 = jnp.concatenate([bias] * 4, axis=0)
        k_all = jnp.concatenate(kblk[qb:qb + 3] + [kx_ref[...]], axis=0)
        v_all[qb] = jnp.concatenate(vblk[qb:qb + 3] + [vx_ref[...]], axis=0)
        q = q_ref[qb * T:(qb + 1) * T, :]
        for b in range(2):
            qs = []
            for hd in stacks[b]:
                qp = q[:, 128 * (hd // 2):128 * (hd // 2 + 1)]
                qs.append(jnp.where(lo, zero_b, qp) if hd % 2 else jnp.where(lo, qp, zero_b))
            s_all[qb, b] = _dot(jnp.concatenate(qs, axis=0), k_all[:, 128 * b:128 * (b + 1)], _NT) + bias4
    for qb in range(2):
        outs = {}
        for b in range(2):
            s = s_all[qb, b]
            sk = jnp.concatenate([jnp.broadcast_to(sink[:, hd:hd + 1], (T, 1)) for hd in stacks[b]], axis=0)
            m = jnp.maximum(jnp.max(s, axis=-1, keepdims=True), sk)
            pr = jnp.exp(s - m)
            den = jnp.sum(pr, axis=-1, keepdims=True) + jnp.exp(sk - m)
            o = _dot(pr.astype(BF16), v_all[qb][:, 128 * b:128 * (b + 1)]) / den
            for n, hd in enumerate(stacks[b]):
                outs[hd] = o[T * n:T * (n + 1)]
        for pair in range(4):
            o_ref[qb * T:(qb + 1) * T, 128 * pair:128 * (pair + 1)] = jnp.where(
                lo, outs[2 * pair], outs[2 * pair + 1]).astype(BF16)


def _attention(q, kk, vv, sink, ncc, nlc):
    nt = q.shape[0]
    nblk = ncc + nlc
    assert nblk % 2 == 0 and ncc % 2 == 0
    c_len = ncc * T
    prev = lambda w: pl.BlockSpec((T, w), lambda j: (jnp.maximum(2 * j - 1, 0), 0))
    cur = lambda w: pl.BlockSpec((2 * T, w), lambda j: (j, 0))
    nxt = lambda w: pl.BlockSpec((T, w), lambda j: (jnp.minimum(2 * j + 2, nblk - 1), 0))
    ctx = lambda w: pl.BlockSpec((c_len, w), lambda j: (0, 0))
    return pl.pallas_call(
        functools.partial(_attn_kernel, ncc, nblk),
        grid=(nblk // 2,),
        in_specs=[cur(512), prev(256), cur(256), nxt(256), prev(256), cur(256), nxt(256),
                  ctx(256), ctx(256), _full((1, 8))],
        out_specs=cur(512),
        out_shape=jax.ShapeDtypeStruct((nt, 512), BF16),
        compiler_params=_cparams("arbitrary"),
        name="window_attention",
    )(q, kk, kk, kk, vv, vv, vv, kk, vv, sink)


def _in1_kernel(nct, ntiles, xp_ref, x_ref, xn_ref, mod_ref, g_ref,
                wcat_ref, wg_ref, wgh_ref, cw_ref, cb_ref, gb_ref,
                q_ref, kt_ref, v_ref, o_ref, gc_ref, gr_ref):
    i = pl.program_id(0)
    sh = mod_ref[0, 0:1, :]
    sc = mod_ref[0, 1:2, :]
    g = g_ref[...]
    x_all = jnp.concatenate([xp_ref[...], x_ref[...], xn_ref[...]], axis=0)
    h_all = _norm_mod(x_all, g, sc, sh)
    hb_all = h_all.astype(BF16)
    main_all = _dot(hb_all, wcat_ref[...])
    h, hb, main = h_all[8:8 + TM], hb_all[8:8 + TM], main_all[8:8 + TM]
    prev_ok, next_ok = _seq_edges(i, nct, ntiles)
    x_prev = main_all[7:8, 0:2048] * prev_ok
    x_next = main_all[8 + TM:9 + TM, 0:2048] * next_ok
    act = _conv_silu(main[:, 0:2048], x_prev, x_next, cw_ref[...], cb_ref[...])
    q_ref[...] = act[:, 0:512].astype(BF16)
    kt_ref[...] = jnp.transpose(act[:, 512:1024] * 0.125).astype(BF16)
    v_ref[...] = act[:, 1024:2048].astype(BF16)
    o_ref[...] = main[:, 2048:3072].astype(BF16)
    h_lo = (h - hb.astype(F32)).astype(BF16)
    gates = _hi_lo_cols(_dot(hb, wg_ref[...]), _dot(h_lo, wgh_ref[...]), 32) + gb_ref[...]
    lane = lax.broadcasted_iota(I32, gates.shape, 1)
    gates = jnp.where(lane < 16, gates, _log_sigmoid(gates))
    gc_ref[...] = gates[:, 0:32]
    gr_ref[...] = jnp.transpose(gates)[0:32, :]


def _in1(x1, mod, p, nct):
    nt = x1.shape[0]
    ntiles = nt // TM
    tile = lambda w: pl.BlockSpec((TM, w), lambda i: (i, 0))
    return pl.pallas_call(
        functools.partial(_in1_kernel, nct, ntiles),
        grid=(ntiles,),
        in_specs=_halo_specs(nt, lambda i: i) + [
            _mod_spec(nct), _full((1, D)), _full((D, 3072)), _full((D, 128)), _full((D, 128)),
            _full((3, 2048)), _full((1, 2048)), _full((1, 128))],
        out_specs=[tile(512), pl.BlockSpec((512, TM), lambda i: (0, i)), tile(1024), tile(1024),
                   tile(32), pl.BlockSpec((32, TM), lambda i: (0, i))],
        out_shape=[jax.ShapeDtypeStruct((nt, 512), BF16), jax.ShapeDtypeStruct((512, nt), BF16),
                   jax.ShapeDtypeStruct((nt, 1024), BF16), jax.ShapeDtypeStruct((nt, 1024), BF16),
                   jax.ShapeDtypeStruct((nt, 32), F32), jax.ShapeDtypeStruct((32, nt), F32)],
        compiler_params=_cparams("arbitrary"),
        name="in_proj_odd",
    )(x1, x1, x1, mod, p["g_mix"], p["wcat"], p["wg"], p["wg_hi"], p["conv_w"], p["conv_b"], p["gate_bias"])


def _mlstm_kernel(rev, *refs):
    if rev:
        (q_ref, kt_ref, v_ref, gc_ref, gr_ref, hf_ref, og_ref, hn_ref, o_ref,
         c_ref, mc_ref, mr_ref) = refs
    else:
        q_ref, kt_ref, v_ref, gc_ref, gr_ref, o_ref, c_ref, mc_ref, mr_ref = refs
    j = pl.program_id(0)

    @pl.when(j == 0)
    def _():
        c_ref[...] = jnp.zeros_like(c_ref)
        mc_ref[...] = jnp.zeros_like(mc_ref)
        mr_ref[...] = jnp.zeros_like(mr_ref)

    d = 8 if rev else 0
    end = 0 if rev else T - 1
    order = tuple(reversed(range(SCAN_CHUNKS))) if rev else tuple(range(SCAN_CHUNKS))
    ones_b = jnp.ones((T, 128), BF16)
    sub = lax.broadcasted_iota(I32, (128, T), 0)
    zero_k = jnp.zeros((128, T), BF16)

    def prologue(c):
        rows = slice(c * T, (c + 1) * T)
        ig_col = gc_ref[rows, d:d + 8]
        lf_col = gc_ref[rows, 16 + d:24 + d]
        ig_row = gr_ref[d:d + 8, rows]
        lf_row = gr_ref[16 + d:24 + d, rows]
        tri, b_col, b_row = _cumsums(rev, lf_col, lf_row)
        blast_row = b_col[end:end + 1, :]
        blast_col = b_row[:, end:end + 1]
        wend_row = blast_col - b_row + ig_row
        ac_col = jnp.max(wend_row, axis=1, keepdims=True)
        return dict(tri=tri, b_row=b_row, ig_row=ig_row, blast_row=blast_row, blast_col=blast_col,
                    ac_col=ac_col, eend_row=jnp.exp(wend_row - ac_col),
                    ac_row=jnp.max(blast_row - b_col + ig_col, axis=0, keepdims=True),
                    b_bc=_lane_bcast(b_col), q=q_ref[rows, :])

    pro = {c: prologue(c) for c in order}
    m_col = mc_ref[:, 0:1]
    m_row = mr_ref[0:1, 0:8]
    for c in order:
        p = pro[c]
        mnew_col = jnp.maximum(p["blast_col"] + m_col, p["ac_col"])
        p["sp_col"] = jnp.exp(p["blast_col"] + m_col - mnew_col)
        p["sc_col"] = jnp.exp(p["ac_col"] - mnew_col)
        p["m_row"] = m_row
        m_col, m_row = mnew_col, jnp.maximum(p["blast_row"] + m_row, p["ac_row"])

    def head_matmuls(c, hd):
        p = pro[c]
        rows = slice(c * T, (c + 1) * T)
        pair, hi = hd // 2, hd % 2
        qp = p["q"][:, 128 * pair:128 * (pair + 1)]
        ktp = kt_ref[128 * pair:128 * (pair + 1), rows]
        kth = jnp.where((sub >= 64) if hi else (sub < 64), ktp, zero_k)
        vaug = jnp.concatenate([v_ref[rows, 128 * hd:128 * (hd + 1)], ones_b], axis=1)
        cst = c_ref[hd]
        sqk = _dot(qp, kth)
        inter = _dot(qp, cst.astype(BF16))
        kte = (kth.astype(F32) * p["eend_row"][hd:hd + 1, :]).astype(BF16)
        c_ref[hd] = p["sp_col"][hd:hd + 1, :] * cst + p["sc_col"][hd:hd + 1, :] * _dot(kte, vaug)
        return sqk, inter, vaug

    items = [(c, hd) for c in order for hd in range(8)]
    nxt = head_matmuls(*items[0])
    for n, (c, hd) in enumerate(items):
        p = pro[c]
        rows = slice(c * T, (c + 1) * T)
        cols = slice(128 * hd, 128 * (hd + 1))
        sqk, inter, vaug = nxt
        if n + 1 < len(items):
            nxt = head_matmuls(*items[n + 1])
        bh = p["b_bc"][:, cols]
        dlog = jnp.where(p["tri"], bh - p["b_row"][hd:hd + 1, :] + p["ig_row"][hd:hd + 1, :], NEG_INF)
        gh = bh + p["m_row"][:, hd:hd + 1]
        mstar = jnp.maximum(gh, jnp.max(dlog, axis=-1, keepdims=True))
        w = (jnp.exp(dlog - mstar) * sqk).astype(BF16)
        intra = _dot(w, vaug)
        e_int = jnp.exp(gh - mstar)
        den = jnp.maximum(jnp.abs(intra[:, 128:256] + e_int * inter[:, 128:256]), jnp.exp(-mstar))
        hh = (intra[:, 0:128] + e_int * inter[:, 0:128]) / den
        if rev:
            hh = hh + hf_ref[rows, cols]
            ms = jnp.mean(hh * hh, axis=-1, keepdims=True)
            hh = hh * lax.rsqrt(ms + EPS) * hn_ref[:, cols]
            og = og_ref[rows, cols].astype(F32)
            o_ref[rows, cols] = (hh * _sigmoid(og)).astype(BF16)
        else:
            o_ref[rows, cols] = hh
    mc_ref[...] = jnp.broadcast_to(m_col, mc_ref.shape)
    mr_ref[...] = jnp.broadcast_to(jnp.concatenate([m_row, jnp.zeros((1, 120), F32)], axis=1), mr_ref.shape)


def _mlstm(rev, q, kt, v, gc, gr, p, ncc, nlc, hf=None, og=None):
    nt = q.shape[0]
    assert ncc % SCAN_CHUNKS == 0 and nlc % SCAN_CHUNKS == 0
    ncc, nlc = ncc // SCAN_CHUNKS, nlc // SCAN_CHUNKS
    rows = SCAN_CHUNKS * T
    cmap = _scan_chunk_map(rev, ncc, nlc)
    blk = lambda w: pl.BlockSpec((rows, w), lambda j: (cmap(j), 0))
    blk_t = lambda h: pl.BlockSpec((h, rows), lambda j: (0, cmap(j)))
    in_specs = [blk(512), blk_t(512), blk(1024), blk(32), blk_t(32)]
    args = [q, kt, v, gc, gr]
    if rev:
        in_specs += [blk(1024), blk(1024), _full((1, 1024))]
        args += [hf, og, p["head_norm"]]
    return pl.pallas_call(
        functools.partial(_mlstm_kernel, rev),
        grid=(ncc + nlc,),
        in_specs=in_specs,
        out_specs=blk(1024),
        out_shape=jax.ShapeDtypeStruct((nt, 1024), BF16 if rev else F32),
        scratch_shapes=[pltpu.VMEM((8, 128, 256), F32), pltpu.VMEM((8, 128), F32), pltpu.VMEM((8, 128), F32)],
        compiler_params=_cparams("arbitrary"),
        name="mlstm_bwd" if rev else "mlstm_fwd",
    )(*args)


def _route(logits_t, rb_col):
    scores = _sigmoid(logits_t)
    biased = scores + rb_col
    row = lambda a, e: a[e:e + 1, :]
    gscore = []
    for g in range(4):
        b0, b1, b2, b3 = (row(biased, 4 * g + e) for e in range(4))
        h1, l1 = jnp.maximum(b0, b1), jnp.minimum(b0, b1)
        h2, l2 = jnp.maximum(b2, b3), jnp.minimum(b2, b3)
        gscore.append(jnp.maximum(h1, h2) + jnp.maximum(jnp.minimum(h1, h2), jnp.maximum(l1, l2)))
    gidx = jnp.zeros_like(gscore[0], dtype=I32)
    best = gscore[0]
    for g in range(1, 4):
        better = gscore[g] > best
        gidx = jnp.where(better, g, gidx)
        best = jnp.where(better, gscore[g], best)

    def pick(a, e):
        out = row(a, e)
        for g in range(1, 4):
            out = jnp.where(gidx == g, row(a, 4 * g + e), out)
        return out

    sb = [pick(biased, e) for e in range(4)]
    ss = [pick(scores, e) for e in range(4)]
    i1 = jnp.zeros_like(gidx)
    v1, s1 = sb[0], ss[0]
    for e in range(1, 4):
        better = sb[e] > v1
        i1 = jnp.where(better, e, i1)
        v1 = jnp.where(better, sb[e], v1)
        s1 = jnp.where(better, ss[e], s1)
    i2 = jnp.zeros_like(gidx)
    v2 = jnp.full_like(v1, NEG_INF)
    s2 = jnp.zeros_like(s1)
    for e in range(4):
        better = jnp.logical_and(i1 != e, sb[e] > v2)
        i2 = jnp.where(better, e, i2)
        v2 = jnp.where(better, sb[e], v2)
        s2 = jnp.where(better, ss[e], s2)
    tot = s1 + s2
    w1 = s1 / tot
    w2 = s2 / tot
    first_low = i1 < i2
    a = jnp.minimum(i1, i2)
    b = jnp.maximum(i1, i2)
    off = jnp.where(a == 0, 0, jnp.where(a == 1, 3, 5))
    bucket = 6 * gidx + off + (b - a - 1)
    return bucket, jnp.where(first_low, w1, w2), jnp.where(first_low, w2, w1)


def _out_kernel(nmix, nct, two_src, sub, t0, *refs):
    per = nmix + (2 if two_src else 1)
    tile_refs = [refs[per * u:per * (u + 1)] for u in range(sub)]
    (w_ref, mod_ref, g_ref, rw_ref, rwh_ref, rb_ref,
     xmid_ref, hrow_ref, bucket_ref, rank_ref, cnt_ref, cnt_scr) = refs[per * sub:]
    i = pl.program_id(0)

    @pl.when(i == 0)
    def _():
        cnt_scr[...] = jnp.zeros_like(cnt_scr)

    brow = lax.broadcasted_iota(I32, (NB_PAD, TM), 0)
    r = lax.broadcasted_iota(I32, (TM, TM), 0)
    c = lax.broadcasted_iota(I32, (TM, TM), 1)
    before = (r < c).astype(F32).astype(BF16)
    onehots = []
    for u in range(sub):
        mix_refs = tile_refs[u][:nmix]
        is_ctx = sub * i + u + t0 < nct
        mod = jnp.where(is_ctx, mod_ref[1], mod_ref[0])
        mix = mix_refs[0][...] if nmix == 1 else jnp.concatenate([mr[...] for mr in mix_refs], axis=1)
        if two_src:
            x = jnp.where(is_ctx, tile_refs[u][nmix][...], tile_refs[u][nmix + 1][...])
        else:
            x = tile_refs[u][nmix][...]
        x_mid = x + mod[2:3, :] * _dot(mix, w_ref[...])
        xmid_ref[TM * u:TM * (u + 1), :] = x_mid
        h = _norm_mod(x_mid, g_ref[...], mod[4:5, :], mod[3:4, :])
        hb = h.astype(BF16)
        h_lo = (h - hb.astype(F32)).astype(BF16)
        logits = _hi_lo_cols(_dot(hb, rw_ref[...]), _dot(h_lo, rwh_ref[...]), 16)
        logits_t = jnp.transpose(logits)[0:16, :]
        bucket = _route(logits_t, rb_ref[...])[0]
        hrow_u = hrow_ref.at[pl.ds(TM * TOK * u, TM * TOK), :]
        for cblk in range(TOK):
            _tok_store(hrow_u, cblk, h[:, 128 * cblk:128 * (cblk + 1)])
        bucket_ref[u] = bucket
        onehots.append((brow == bucket).astype(F32))
    cnt = cnt_scr[...]
    for u in range(sub):
        onehot = onehots[u]
        cum = _dot(onehot.astype(BF16), before)
        rank_ref[u] = jnp.sum(onehot * (cum + cnt[:, 0:1]), axis=0, keepdims=True).astype(I32)
        cnt = cnt + jnp.sum(onehot, axis=1, keepdims=True)
    cnt_scr[...] = cnt
    cnt_ref[...] = cnt.astype(I32)


def _out_proj(mixes, w_out, xs, mod, g_ffn, rw, rw_hi, rb_col, t0, ntiles, nct):
    nmix = len(mixes)
    two_src = len(xs) == 2
    n = ntiles * TM
    sub = max(s for s in (5, 4, 3, 2, 1) if ntiles % s == 0)
    in_specs, args = [], []
    for u in range(sub):
        tile_of = lambda i, u=u: sub * i + u + t0
        for mx in mixes:
            in_specs.append(pl.BlockSpec((TM, mx.shape[1]), lambda i, f=tile_of: (f(i), 0)))
            args.append(mx)
        if two_src:
            in_specs += [pl.BlockSpec((TM, D), lambda i, f=tile_of: (jnp.minimum(f(i), nct - 1), 0)),
                         pl.BlockSpec((TM, D), lambda i, f=tile_of: (jnp.maximum(f(i) - nct, 0), 0))]
        else:
            in_specs.append(pl.BlockSpec((TM, D), lambda i, f=tile_of: (f(i), 0)))
        args += list(xs)
    in_specs += [_full((D, D)), _full((2, 6, D)), _full((1, D)), _full((D, 128)), _full((D, 128)), _full((16, 1))]
    rows_out = pl.BlockSpec((sub, 1, TM), lambda i: (i, 0, 0))
    return pl.pallas_call(
        functools.partial(_out_kernel, nmix, nct, two_src, sub, t0),
        grid=(ntiles // sub,),
        in_specs=in_specs,
        out_specs=[pl.BlockSpec((sub * TM, D), lambda i: (i, 0)),
                   pl.BlockSpec((sub * TM * TOK, 128), lambda i: (i, 0)), rows_out, rows_out,
                   _full((NB_PAD, 128))],
        out_shape=[jax.ShapeDtypeStruct((n, D), F32), jax.ShapeDtypeStruct((n * TOK, 128), F32),
                   jax.ShapeDtypeStruct((ntiles, 1, TM), I32), jax.ShapeDtypeStruct((ntiles, 1, TM), I32),
                   jax.ShapeDtypeStruct((NB_PAD, 128), I32)],
        scratch_shapes=[pltpu.VMEM((NB_PAD, 128), F32)],
        compiler_params=_cparams("arbitrary"),
        name="out_proj_router",
    )(*args, w_out, mod, g_ffn, rw, rw_hi, rb_col)


def _scatter_kernel(ntiles, pos_ref, flo_ref, fhi_ref, src_ref, dst_ref, hbuf, in_sem, out_sem):
    i = pl.program_id(0)
    slot = i % 3
    h_ref = hbuf.at[slot]
    sem = out_sem.at[slot]
    rows = TM * TOK

    def load(tile, s):
        return pltpu.make_async_copy(src_ref.at[pl.ds(pl.multiple_of(tile * rows, rows), rows), :],
                                     hbuf.at[s], in_sem.at[s])

    def tok(ref, t):
        return ref.at[pl.ds(pl.multiple_of(t * TOK, TOK), TOK), :]

    def copy(r, d_row, src=h_ref, sm=sem):
        return pltpu.make_async_copy(tok(src, r), tok(dst_ref, d_row), sm)

    def wait_rows(lo, hi, unroll, s=slot):
        def body(r, carry):
            copy(0, 0, hbuf.at[s], out_sem.at[s]).wait()
            return carry
        lax.fori_loop(lo, hi, body, 0, unroll=unroll)

    @pl.when(i == 0)
    def _():
        load(0, 0).start()
        if ntiles > 1:
            load(1, 1).start()

    load(i, slot).wait()

    def start(r8, c):
        for k in range(8):
            r = r8 * 8 + k
            copy(r, pos_ref[i * TM + r]).start(priority=k % 2)
        return c
    lax.fori_loop(0, TM // 8, start, 0)

    @pl.when(i > 0)
    def _():
        wait_rows(0, TM, 8, (i + 2) % 3)

    @pl.when(i + 2 < ntiles)
    def _():
        load(i + 2, (i + 2) % 3).start()

    @pl.when(i == ntiles - 1)
    def _():
        wait_rows(0, TM, 8)
        for b in range(N_BUCKETS):
            lo, hi = flo_ref[b], fhi_ref[b]

            def fill(r, c):
                copy(0, r).start()
                return c
            lax.fori_loop(lo, hi, fill, 0)
            wait_rows(lo, hi, 1)

        def tile_copy(j):
            rows = TMM * TOK
            return pltpu.make_async_copy(h_ref, dst_ref.at[pl.ds(pl.multiple_of(j * rows, rows), rows), :], sem)

        def fill_tile(j, c):
            tile_copy(j).start()
            return c

        def wait_tile(j, c):
            tile_copy(j).wait()
            return c
        lax.fori_loop(flo_ref[N_BUCKETS], fhi_ref[N_BUCKETS], fill_tile, 0)
        lax.fori_loop(flo_ref[N_BUCKETS], fhi_ref[N_BUCKETS], wait_tile, 0)


def _scatter_rows(hrow, pos, fill_lo, fill_hi, n, npad):
    assert TM == TMM
    return pl.pallas_call(
        functools.partial(_scatter_kernel, n // TM),
        grid_spec=pltpu.PrefetchScalarGridSpec(
            num_scalar_prefetch=3,
            grid=(n // TM,),
            in_specs=[pl.BlockSpec(memory_space=pl.ANY)],
            out_specs=pl.BlockSpec(memory_space=pl.ANY),
            scratch_shapes=[pltpu.VMEM((3, TM * TOK, 128), hrow.dtype), pltpu.SemaphoreType.DMA((3,)),
                            pltpu.SemaphoreType.DMA((3,))],
        ),
        out_shape=jax.ShapeDtypeStruct((npad * TOK, 128), hrow.dtype),
        compiler_params=_cparams("arbitrary"),
        name="moe_scatter_rows",
    )(pos, fill_lo, fill_hi, hrow)


def _combine_kernel(ntiles, pos_ref, x_ref, mod_ref, ys_ref, o_ref, ybuf, sem):
    i = pl.program_id(0)

    def copy(tile, slot, r):
        src = pl.multiple_of(pos_ref[tile * TM + r] * TOK, TOK)
        dst = pl.multiple_of(r * TOK, TOK)
        return pltpu.make_async_copy(ys_ref.at[pl.ds(src, TOK), :], ybuf.at[slot, pl.ds(dst, TOK), :], sem.at[slot])

    def start_tile(tile, slot):
        def body(r8, carry):
            for k in range(8):
                copy(tile, slot, r8 * 8 + k).start(priority=k % 2)
            return carry
        lax.fori_loop(0, TM // 8, body, 0)

    @pl.when(i == 0)
    def _():
        start_tile(0, 0)

    @pl.when(i + 1 < ntiles)
    def _():
        start_tile(i + 1, (i + 1) % 2)

    slot = i % 2

    def wait_body(r, carry):
        copy(i, slot, 0).wait()
        return carry
    lax.fori_loop(0, TM, wait_body, 0, unroll=8)
    o_ref[...] = x_ref[...] + mod_ref[0, 5:6, :] * _tok_rows(ybuf.at[slot], TM)


def _combine(x_mid, mod, ys, pos, t0, nct):
    n = x_mid.shape[0]
    ntiles = n // TM
    return pl.pallas_call(
        functools.partial(_combine_kernel, ntiles),
        grid_spec=pltpu.PrefetchScalarGridSpec(
            num_scalar_prefetch=1,
            grid=(ntiles,),
            in_specs=[pl.BlockSpec((TM, D), lambda i, *_: (i, 0)),
                      pl.BlockSpec((1, 6, D), lambda i, *_: (jnp.where(i + t0 < nct, 1, 0), 0, 0)),
                      pl.BlockSpec(memory_space=pl.ANY)],
            out_specs=pl.BlockSpec((TM, D), lambda i, *_: (i, 0)),
            scratch_shapes=[pltpu.VMEM((2, TM * TOK, 128), F32), pltpu.SemaphoreType.DMA((2,))],
        ),
        out_shape=jax.ShapeDtypeStruct((n, D), F32),
        compiler_params=_cparams("arbitrary"),
        name="moe_combine",
    )(pos, x_mid, mod, ys)


def _moe_kernel(tea_ref, teb_ref, tblk_ref, tval_ref, x_ref, rw_ref, rwh_ref, w1a_ref, w3a_ref, w2a_ref,
                w1b_ref, w3b_ref, w2b_ref, y_ref):
    del tblk_ref
    j = pl.program_id(0)

    @pl.when(tval_ref[j] != 0)
    def _():
        h = _tok_rows(x_ref, TMM)
        hb = h.astype(BF16)
        h_lo = (h - hb.astype(F32)).astype(BF16)
        logits = _hi_lo_cols(_dot(hb, rw_ref[...]), _dot(h_lo, rwh_ref[...]), 16)
        scores = _sigmoid(logits)
        lane = lax.broadcasted_iota(I32, scores.shape, 1)
        s_a = jnp.sum(jnp.where(lane == tea_ref[j], scores, 0.0), axis=1, keepdims=True)
        s_b = jnp.sum(jnp.where(lane == teb_ref[j], scores, 0.0), axis=1, keepdims=True)
        gates = (s_a / (s_a + s_b), s_b / (s_a + s_b))
        acts = []
        for w1_ref, w3_ref, gate in ((w1a_ref, w3a_ref, gates[0]), (w1b_ref, w3b_ref, gates[1])):
            u = _dot(hb, w1_ref[0, 0].astype(BF16))
            v = _dot(hb, w3_ref[0, 0].astype(BF16))
            acts.append((_silu(u) * v * gate).astype(BF16))
        y = _dot(acts[0], w2a_ref[0, 0].astype(BF16)) + _dot(acts[1], w2b_ref[0, 0].astype(BF16))
        for cblk in range(TOK):
            _tok_store(y_ref, cblk, y[:, 128 * cblk:128 * (cblk + 1)])

    @pl.when(tval_ref[j] == 0)
    def _():
        y_ref[...] = jnp.zeros_like(y_ref)


def _moe(xs_sorted, rw, rw_hi, w1, w3, w2, layer, tile_ea, tile_eb, tile_blk, tile_valid):
    npad = xs_sorted.shape[0] // TOK
    ntile = npad // TMM
    wspec = lambda shape, which: pl.BlockSpec(
        (1, 1) + shape, lambda j, ea, eb, blk, val: (layer, (ea, eb)[which][j], 0, 0))
    up, down = (D, D_EXPERT), (D_EXPERT, D)
    return pl.pallas_call(
        _moe_kernel,
        grid_spec=pltpu.PrefetchScalarGridSpec(
            num_scalar_prefetch=4,
            grid=(ntile,),
            in_specs=[pl.BlockSpec((TMM * TOK, 128), lambda j, ea, eb, blk, val: (blk[j], 0)),
                      pl.BlockSpec((D, 128), lambda j, *_: (0, 0)), pl.BlockSpec((D, 128), lambda j, *_: (0, 0)),
                      wspec(up, 0), wspec(up, 0), wspec(down, 0), wspec(up, 1), wspec(up, 1), wspec(down, 1)],
            out_specs=pl.BlockSpec((TMM * TOK, 128), lambda j, ea, eb, blk, val: (j, 0)),
        ),
        out_shape=jax.ShapeDtypeStruct((npad * TOK, 128), F32),
        compiler_params=_cparams("arbitrary"),
        name="moe_experts",
    )(tile_ea, tile_eb, tile_blk, tile_valid, xs_sorted, rw, rw_hi, w1, w3, w2, w1, w3, w2)


def _moe_block(x_mid, hrow, bucket, rank, counts, mod, rw, rw_hi, w1, w3, w2, layer, t0, nct):
    n = hrow.shape[0] // TOK
    ntile = n // TMM + N_BUCKETS
    npad = ntile * TMM
    cnt = counts[:N_BUCKETS, 0]
    padded = ((cnt + TMM - 1) // TMM) * TMM
    ends = jnp.cumsum(padded)
    starts = ends - padded
    total_tiles = ends[-1] // TMM
    tiles = jnp.arange(ntile, dtype=I32)
    tile_valid = (tiles < total_tiles).astype(I32)
    tile_blk = jnp.minimum(tiles, jnp.maximum(total_tiles - 1, 0))
    tile_bucket = jnp.minimum(jnp.sum((ends[None, :] <= (tile_blk * TMM)[:, None]).astype(I32), axis=1), N_BUCKETS - 1)
    pair = tile_bucket % 6
    grp = tile_bucket // 6
    pair_a = jnp.where(pair < 3, 0, jnp.where(pair < 5, 1, 2))
    pair_b = jnp.where(pair < 3, pair + 1, jnp.where(pair < 5, pair - 1, 3))
    tile_ea = (4 * grp + pair_a).astype(I32)
    tile_eb = (4 * grp + pair_b).astype(I32)
    bucket = bucket.reshape(-1)
    onehot = (bucket[:, None] == jnp.arange(N_BUCKETS, dtype=I32)[None, :]).astype(I32)
    pos = (rank.reshape(-1) + jnp.sum(onehot * starts[None, :], axis=1)).astype(I32)
    pad32 = lambda a, tail: jnp.zeros((NB_PAD,), I32).at[:N_BUCKETS].set(a.astype(I32)).at[N_BUCKETS].set(tail)
    fill_lo, fill_hi = pad32(starts + cnt, total_tiles), pad32(ends, ntile)
    xs_sorted = _scatter_rows(hrow, pos, fill_lo, fill_hi, n, npad)
    ys = _moe(xs_sorted, rw, rw_hi, w1, w3, w2, layer, tile_ea, tile_eb, tile_blk, tile_valid)
    return _combine(x_mid, mod, ys, pos, t0, nct)


def kernel(x, c, ctx, c_ctx, router_w, router_b, norm_mix, norm_ffn, w_mod, b_mod, ev_w_in, ev_conv_w, ev_conv_b, ev_dt_bias, ev_a_log, ev_d_skip, ev_ssd_norm, ev_q_norm, ev_k_norm, ev_sink, ev_w_out, od_w_in, od_conv_w, od_conv_b, od_igate_b, od_fgate_b, od_head_norm, od_w_out, moe_w1, moe_w3, moe_w2):
    s_len = x.shape[1]
    c_len = ctx.shape[1]
    assert x.shape[0] == 1 and s_len % TM == 0 and c_len % TM == 0 and s_len % GRID_W == 0
    nct = c_len // TM
    ncc, nlc = c_len // T, s_len // T
    nt = c_len + s_len
    ntiles = nt // TM

    mod = _modulation(c, c_ctx, w_mod, b_mod)
    rw, rw_hi = _hi_lo_weight(router_w)
    rb_col = router_b.reshape(N_EXPERTS, 1)
    pad128 = lambda v: jnp.zeros((1, 128), F32).at[0, :v.shape[0]].set(v)

    w = ev_w_in[0]
    rope_rows, rope_cols = _rope_tables(s_len)
    wdt, wdt_hi = _hi_lo_weight(w[:, 1280:1296])
    p0 = dict(
        g_mix=norm_mix[0].reshape(1, D),
        w_zx=w[:, 0:1280].astype(BF16), w_qkv=w[:, 1296:2064].astype(BF16), wdt=wdt, wdt_hi=wdt_hi,
        conv_w=ev_conv_w[0], conv_b=ev_conv_b[0].reshape(1, 768),
        dt_bias=pad128(ev_dt_bias[0].reshape(16)),
        q_norm=jnp.tile(ev_q_norm[0], 8).reshape(1, 512), k_norm=jnp.tile(ev_k_norm[0], 2).reshape(1, 128),
        rope_rows=rope_rows, rope_cols=rope_cols,
        alog_row=ev_a_log[0].reshape(1, 16), alog_col=ev_a_log[0].reshape(16, 1),
        d_skip=jnp.repeat(ev_d_skip[0], 64).reshape(1, 512), ssd_norm=ev_ssd_norm[0].reshape(1, 512))
    z, xs, bc, q, kk, vv, dtc, dtr = _in0(ctx[0], x[0], mod[0], p0, nct)
    yf = _ssd(False, xs, bc, dtc, dtr, p0, ncc, nlc)
    ymix = _ssd(True, xs, bc, dtc, dtr, p0, ncc, nlc, yf=yf, z=z)
    att = _attention(q, kk, vv, ev_sink[0].reshape(1, 8), ncc, nlc)
    x_mid0, hrow, bucket, rank, counts = _out_proj(
        [ymix, att], ev_w_out[0].astype(BF16), [ctx[0], x[0]], mod[0], norm_ffn[0].reshape(1, D),
        rw, rw_hi, rb_col, 0, ntiles, nct)
    x1 = _moe_block(x_mid0, hrow, bucket, rank, counts, mod[0], rw, rw_hi, moe_w1, moe_w3, moe_w2, 0, 0, nct)

    w = od_w_in[0]
    wg, wg_hi = _hi_lo_weight(w[:, 3072:3104])
    p1 = dict(
        g_mix=norm_mix[1].reshape(1, D),
        wcat=w[:, 0:3072].astype(BF16), wg=wg, wg_hi=wg_hi,
        conv_w=od_conv_w[0], conv_b=od_conv_b[0].reshape(1, 2048),
        gate_bias=pad128(jnp.concatenate([od_igate_b[0].reshape(16), od_fgate_b[0].reshape(16)])),
        head_norm=od_head_norm[0].reshape(1, 1024))
    q1, kt1, v1, og1, gc1, gr1 = _in1(x1, mod[1], p1, nct)
    hf = _mlstm(False, q1, kt1, v1, gc1, gr1, p1, ncc, nlc)
    hmix = _mlstm(True, q1, kt1, v1, gc1, gr1, p1, ncc, nlc, hf=hf, og=og1)
    x_mid1, hrow, bucket, rank, counts = _out_proj(
        [hmix], od_w_out[0].astype(BF16), [x1], mod[1], norm_ffn[1].reshape(1, D),
        rw, rw_hi, rb_col, nct, ntiles - nct, nct)
    return _moe_block(x_mid1, hrow, bucket, rank, counts, mod[1], rw, rw_hi, moe_w1, moe_w3, moe_w2, 1, nct,
                      nct)[None]
```

```python
import functools
import math

import jax
import jax.numpy as jnp
from jax import lax
from jax.experimental import pallas as pl
from jax.experimental.pallas import tpu as pltpu

F32 = jnp.float32
BF16 = jnp.bfloat16
I32 = jnp.int32

EPS = 1e-6
D = 1024
T = 128
SCAN_CHUNKS = 2
TM = 256
TMM = 256
GRID_W = 64
ROPE_THETA = 10000.0
N_EXPERTS = 16
N_BUCKETS = 24
NB_PAD = 32
D_EXPERT = 512
TOK = 8
U32 = jnp.uint32
NEG_INF = float("-inf")
VMEM_LIMIT = 56 * 1024 * 1024

_NN = (((1,), (0,)), ((), ()))
_NT = (((1,), (1,)), ((), ()))
_TN = (((0,), (0,)), ((), ()))

_PAIR_SLOTS = ((0, 1), (2, 1), (2, 0), (3, 0), (3, 1), (3, 2))


def _cparams(*sem):
    return pltpu.CompilerParams(dimension_semantics=sem, vmem_limit_bytes=VMEM_LIMIT)


def _dot(a, b, dims=_NN):
    return lax.dot_general(a, b, dims, preferred_element_type=F32)


def _split(a, n):
    out = []
    r = a
    for _ in range(n):
        t = r.astype(BF16)
        out.append(t)
        r = r - t.astype(F32)
    return out


def _mdot(as_, bs, dims=_NN, order=None):
    if order is None:
        order = len(as_) + len(bs) - 2
    acc = None
    for i, a in enumerate(as_):
        for j, b in enumerate(bs):
            if i + j <= order:
                p = _dot(a, b, dims)
                acc = p if acc is None else acc + p
    return acc


def _sigmoid(x):
    return 1.0 / (1.0 + jnp.exp(-x))


def _silu(x):
    return x * _sigmoid(x)


def _log1p_exp_neg_abs(x):
    e = jnp.exp(-jnp.abs(x))
    u = 1.0 + e
    um1 = u - 1.0
    return jnp.where(um1 == 0.0, e, jnp.log(u) * (e / jnp.where(um1 == 0.0, 1.0, um1)))


def _softplus(x):
    return jnp.maximum(x, 0.0) + _log1p_exp_neg_abs(x)


def _log_sigmoid(x):
    return jnp.minimum(x, 0.0) - _log1p_exp_neg_abs(x)


def _norm_mod(x, g, sc, sh):
    ms = jnp.mean(x * x, axis=-1, keepdims=True)
    return (x * lax.rsqrt(ms + EPS)) * g * (1.0 + sc) + sh


def _tok_load(ref, chunk, n):
    return ref[pl.ds(chunk, n, stride=TOK), :]


def _tok_store(ref, chunk, val):
    ref[pl.ds(chunk, val.shape[0], stride=TOK), :] = val


def _tok_rows(ref, n):
    return jnp.concatenate([_tok_load(ref, c, n) for c in range(TOK)], axis=1)


def _tri(rev):
    r = lax.broadcasted_iota(I32, (T, T), 0)
    c = lax.broadcasted_iota(I32, (T, T), 1)
    return (c >= r) if rev else (c <= r)


def _cumsums(rev, col, row):
    tri = _tri(rev)
    tri_b = tri.astype(F32).astype(BF16)
    trit_b = _tri(not rev).astype(F32).astype(BF16)
    ccol = _mdot([tri_b], _split(col, 3))
    crow = _mdot(_split(row, 3), [trit_b])
    return tri, ccol, crow


def _lane_bcast(col):
    n = col.shape[1]
    col = jnp.concatenate([col, jnp.zeros((col.shape[0], 128 - n), F32)], axis=1)
    r = lax.broadcasted_iota(I32, (128, n * 128), 0)
    c = lax.broadcasted_iota(I32, (128, n * 128), 1) // 128
    return _mdot(_split(col, 3), [(r == c).astype(F32).astype(BF16)])


def _mod_kernel(c_ref, w_ref, b_ref, o_ref):
    a = _silu(c_ref[...])
    o_ref[0] = _mdot(_split(a, 2), _split(w_ref[0], 2), order=1) + b_ref[0]


def _modulation(c, c_ctx, w_mod, b_mod):
    depth = w_mod.shape[0]
    n = w_mod.shape[2]
    tn = 1536
    cc = jnp.zeros((8, D), F32).at[0].set(c[0]).at[1].set(c_ctx)
    out = pl.pallas_call(
        _mod_kernel,
        grid=(depth, n // tn),
        in_specs=[
            pl.BlockSpec((8, D), lambda l, j: (0, 0)),
            pl.BlockSpec((1, D, tn), lambda l, j: (l, 0, j)),
            pl.BlockSpec((1, 1, tn), lambda l, j: (l, 0, j)),
        ],
        out_specs=pl.BlockSpec((1, 8, tn), lambda l, j: (l, 0, j)),
        out_shape=jax.ShapeDtypeStruct((depth, 8, n), F32),
        compiler_params=_cparams("arbitrary", "arbitrary"),
        name="modulation",
    )(cc, w_mod, b_mod.reshape(depth, 1, n))
    return out[:, :2].reshape(depth, 2, 6, D)


def _halo_specs(nrows, tile_of):
    nb8 = nrows // 8
    return [
        pl.BlockSpec((8, D), lambda i: (jnp.maximum(tile_of(i) * (TM // 8) - 1, 0), 0)),
        pl.BlockSpec((TM, D), lambda i: (tile_of(i), 0)),
        pl.BlockSpec((8, D), lambda i: (jnp.minimum((tile_of(i) + 1) * (TM // 8), nb8 - 1), 0)),
    ]


def _ctx_tile(nct):
    return lambda i: jnp.minimum(i, nct - 1)


def _lat_tile(nct):
    return lambda i: jnp.maximum(i - nct, 0)


def _mod_spec(nct, t0=0):
    return pl.BlockSpec((1, 6, D), lambda i: (jnp.where(i + t0 < nct, 1, 0), 0, 0))


def _full(shape):
    nd = len(shape)
    return pl.BlockSpec(shape, lambda i: (0,) * nd)


def _seq_edges(i, nct, ntiles):
    prev_ok = jnp.logical_and(i != 0, i != nct).astype(F32)
    next_ok = jnp.logical_and(i != nct - 1, i != ntiles - 1).astype(F32)
    return prev_ok, next_ok


def _conv_silu(x, x_first_prev, x_last_next, cw, cb):
    n = x.shape[0]
    rows = lax.broadcasted_iota(I32, x.shape, 0)
    x_prev = jnp.where(rows == 0, x_first_prev, pltpu.roll(x, 1, 0))
    x_next = jnp.where(rows == n - 1, x_last_next, pltpu.roll(x, n - 1, 0))
    return _silu(x_prev * cw[0:1] + x * cw[1:2] + x_next * cw[2:3] + cb)


def _hi_lo_cols(blk, lo_pass, n):
    return blk + pltpu.roll(blk, 128 - n, 1) + lo_pass


def _hi_lo_weight(w):
    n = w.shape[1]
    hi = w.astype(BF16)
    lo = (w - hi.astype(F32)).astype(BF16)
    z = jnp.zeros((w.shape[0], 128 - 2 * n), BF16)
    return jnp.concatenate([hi, lo, z], axis=1), jnp.concatenate([hi, jnp.zeros_like(lo), z], axis=1)


def _head_rms(xf, gamma):
    r = lax.broadcasted_iota(I32, (128, 128), 0) // 64
    c = lax.broadcasted_iota(I32, (128, 128), 1) // 64
    ones_bd = (r == c).astype(F32).astype(BF16)
    outs = []
    for j in range(xf.shape[1] // 128):
        blk = xf[:, 128 * j:128 * (j + 1)]
        ssum = _dot((blk * blk).astype(BF16), ones_bd)
        outs.append(blk * lax.rsqrt(ssum * (1.0 / 64.0) + EPS))
    return jnp.concatenate(outs, axis=1) * gamma


def _rope(xf, cos, sin):
    lane = lax.broadcasted_iota(I32, (xf.shape[0], 128), 1)
    first = (lane % 32) < 16
    outs = []
    for j in range(xf.shape[1] // 128):
        blk = xf[:, 128 * j:128 * (j + 1)]
        partner = jnp.where(first, pltpu.roll(blk, 112, 1), pltpu.roll(blk, 16, 1))
        outs.append(blk * cos + partner * sin)
    return jnp.concatenate(outs, axis=1)


def _rope_tables(s_len):
    lane = jnp.arange(128, dtype=I32)
    inv = ROPE_THETA ** (-(lane % 16).astype(F32) / 16.0)
    sign = jnp.where((lane % 32) < 16, -1.0, 1.0).astype(F32)
    ang_r = jnp.arange(s_len // GRID_W, dtype=F32)[:, None] * inv[None, :]
    ang_c = jnp.tile(jnp.arange(GRID_W, dtype=F32), TM // GRID_W)[:, None] * inv[None, :]
    both = lambda ang: jnp.stack([jnp.cos(ang), jnp.sin(ang) * sign[None, :]])
    return both(ang_r), both(ang_c)


def _in0_kernel(nct, ntiles, cp_ref, c_ref, cn_ref, xp_ref, x_ref, xn_ref, mod_ref, g_ref, wa_ref, wb_ref,
                wdt_ref, wdth_ref, cw_ref, cb_ref, dtb_ref, qn_ref, kn_ref, rowcs_ref, colcs_ref,
                z_ref, xs_ref, bc_ref, q_ref, kk_ref, vv_ref, dtc_ref, dtr_ref):
    i = pl.program_id(0)
    is_ctx = i < nct
    sh = mod_ref[0, 0:1, :]
    sc = mod_ref[0, 1:2, :]
    g = g_ref[...]
    x_all = jnp.concatenate([jnp.where(is_ctx, cp_ref[...], xp_ref[...]),
                             jnp.where(is_ctx, c_ref[...], x_ref[...]),
                             jnp.where(is_ctx, cn_ref[...], xn_ref[...])], axis=0)
    h_all = _norm_mod(x_all, g, sc, sh)
    hb_all = h_all.astype(BF16)
    zx_all = _dot(hb_all, wa_ref[...])
    h, hb, zx = h_all[8:8 + TM], hb_all[8:8 + TM], zx_all[8:8 + TM]
    qkv = _dot(hb, wb_ref[...])
    prev_ok, next_ok = _seq_edges(i, nct, ntiles)
    xb_prev = zx_all[7:8, 512:1280] * prev_ok
    xb_next = zx_all[8 + TM:9 + TM, 512:1280] * next_ok
    act = _conv_silu(zx[:, 512:1280], xb_prev, xb_next, cw_ref[...], cb_ref[...])
    z_ref[...] = zx[:, 0:512].astype(BF16)
    xs_ref[...] = act[:, 0:512].astype(BF16)
    bc_ref[...] = act[:, 512:768].astype(BF16)
    row0 = jnp.maximum(i - nct, 0) * (TM // GRID_W)
    rowcs = [jnp.concatenate([jnp.broadcast_to(rowcs_ref[t, pl.ds(row0 + kq, 1), :], (GRID_W, 128))
                              for kq in range(TM // GRID_W)], axis=0) for t in range(2)]
    row_lanes = (lax.broadcasted_iota(I32, (TM, 128), 1) % 64) < 32
    cos = jnp.where(is_ctx, 1.0, jnp.where(row_lanes, rowcs[0], colcs_ref[0]))
    sin = jnp.where(is_ctx, 0.0, jnp.where(row_lanes, rowcs[1], colcs_ref[1]))
    q = _rope(_head_rms(qkv[:, 0:512], qn_ref[...]), cos, sin) * 0.125
    q_ref[...] = q.astype(BF16)
    k = _rope(_head_rms(qkv[:, 512:640], kn_ref[...]), cos, sin)
    kk_ref[...] = jnp.concatenate([k, pltpu.roll(k, 64, 1)], axis=1).astype(BF16)
    v = qkv[:, 640:768]
    vv_ref[...] = jnp.concatenate([v, pltpu.roll(v, 64, 1)], axis=1).astype(BF16)
    h_lo = (h - hb.astype(F32)).astype(BF16)
    dt = _softplus(_hi_lo_cols(_dot(hb, wdt_ref[...]), _dot(h_lo, wdth_ref[...]), 16) + dtb_ref[...])
    dtc_ref[...] = dt[:, 0:16]
    dtr_ref[...] = jnp.transpose(dt)[0:16, :]


def _in0(ctx2, x2, mod, p, nct):
    c_len, s_len = ctx2.shape[0], x2.shape[0]
    nt = c_len + s_len
    ntiles = nt // TM
    tile = lambda w: pl.BlockSpec((TM, w), lambda i: (i, 0))
    lat = _lat_tile(nct)
    lat_tile = lambda w: pl.BlockSpec((TM, w), lambda i: (lat(i), 0))
    outs = [(512, BF16), (512, BF16), (256, BF16), (512, BF16), (256, BF16), (256, BF16), (16, F32)]
    return pl.pallas_call(
        functools.partial(_in0_kernel, nct, ntiles),
        grid=(ntiles,),
        in_specs=_halo_specs(c_len, _ctx_tile(nct)) + _halo_specs(s_len, lat) + [
            _mod_spec(nct), _full((1, D)), _full((D, 1280)), _full((D, 768)), _full((D, 128)), _full((D, 128)),
            _full((3, 768)), _full((1, 768)), _full((1, 128)),
            _full((1, 512)), _full((1, 128)), _full((2, s_len // GRID_W, 128)), _full((2, TM, 128))],
        out_specs=[tile(w) for w, _ in outs] + [pl.BlockSpec((16, TM), lambda i: (0, i))],
        out_shape=[jax.ShapeDtypeStruct((nt, w), dt) for w, dt in outs]
        + [jax.ShapeDtypeStruct((16, nt), F32)],
        compiler_params=_cparams("arbitrary"),
        name="in_proj_even",
    )(ctx2, ctx2, ctx2, x2, x2, x2, mod, p["g_mix"], p["w_zx"], p["w_qkv"], p["wdt"], p["wdt_hi"],
      p["conv_w"], p["conv_b"], p["dt_bias"], p["q_norm"], p["k_norm"], p["rope_rows"], p["rope_cols"])


def _scan_chunk_map(rev, ncc, nlc):
    if not rev:
        return lambda j: j
    return lambda j: jnp.where(j < ncc, ncc - 1 - j, ncc + nlc - 1 - (j - ncc))


def _ssd_kernel(rev, *refs):
    if rev:
        (xs_ref, bc_ref, dtc_ref, dtr_ref, alr_ref, alc_ref, yf_ref, z_ref, dsk_ref, nrm_ref,
         o_ref, st_ref) = refs
    else:
        xs_ref, bc_ref, dtc_ref, dtr_ref, alr_ref, alc_ref, o_ref, st_ref = refs
    j = pl.program_id(0)

    @pl.when(j == 0)
    def _():
        st_ref[...] = jnp.zeros_like(st_ref)

    d = 8 if rev else 0
    a_coef_row = -jnp.exp(alr_ref[...])[:, d:d + 8]
    a_coef_col = -jnp.exp(alc_ref[...])[d:d + 8, :]
    lane = lax.broadcasted_iota(I32, (T, 128), 1)
    lo = lane < 64
    zero_b = jnp.zeros((T, 128), BF16)
    hi_half = jnp.logical_not(lo)
    sub = lax.broadcasted_iota(I32, (128, T), 0)
    eye = (sub == lax.broadcasted_iota(I32, (128, T), 1)).astype(F32).astype(BF16)
    end = 0 if rev else T - 1
    order = tuple(reversed(range(SCAN_CHUNKS))) if rev else tuple(range(SCAN_CHUNKS))

    def prologue(c):
        rows = slice(c * T, (c + 1) * T)
        dtc = dtc_ref[rows, d:d + 8]
        dtr = dtr_ref[d:d + 8, rows]
        tri, acs_col, acs_row = _cumsums(rev, dtc * a_coef_row, dtr * a_coef_col)
        atot_col = acs_row[:, end:end + 1]
        acs_bc = _lane_bcast(acs_col)
        dec_row = jnp.exp(atot_col - acs_row) * dtr
        xs = xs_ref[rows, :]
        bm = bc_ref[rows, 0:128]
        cm = bc_ref[rows, 128:256]
        cgs = [jnp.where(lo, cm, zero_b), jnp.where(hi_half, cm, zero_b)]
        cbs = [_dot(cgs[g], bm, _NT) for g in range(2)]
        bmt = _dot(eye, bm, _NT)
        bgts = [jnp.where(sub < 64, bmt, 0.0), jnp.where(sub >= 64, bmt, 0.0)]
        xpairs = []
        for pr in range(4):
            xp = xs[:, 128 * pr:128 * (pr + 1)]
            xpairs.append(jnp.concatenate([jnp.where(lo, xp, zero_b), jnp.where(hi_half, xp, zero_b)], axis=0))
        return dict(tri=tri, acs_row=acs_row, acs_bc=acs_bc, atot_col=atot_col, dec_row=dec_row, dtr=dtr,
                    xs=xs, cgs=cgs, cbs=cbs, bgts=bgts, xpairs=xpairs)

    pro = {c: prologue(c) for c in order}
    y_off = {}
    for c in order:
        p = pro[c]
        for pr in range(4):
            g, h0, h1 = pr // 2, 2 * pr, 2 * pr + 1
            st = st_ref[pr]
            eacs = jnp.exp(jnp.where(lo, p["acs_bc"][:, 128 * h0:128 * (h0 + 1)],
                                     p["acs_bc"][:, 128 * h1:128 * (h1 + 1)]))
            y_off[c, pr] = _dot(p["cgs"][g], st.astype(BF16)) * eacs
            bdec = jnp.concatenate([(p["bgts"][g] * p["dec_row"][h0:h0 + 1, :]).astype(BF16),
                                    (p["bgts"][g] * p["dec_row"][h1:h1 + 1, :]).astype(BF16)], axis=1)
            carry = jnp.where(lo[0:1, :], jnp.exp(p["atot_col"][h0:h0 + 1, :]), jnp.exp(p["atot_col"][h1:h1 + 1, :]))
            st_ref[pr] = carry * st + _dot(bdec, p["xpairs"][pr])
    for c in order:
        p = pro[c]
        rows = slice(c * T, (c + 1) * T)
        ys = []
        for pr in range(4):
            ms = []
            for hd in (2 * pr, 2 * pr + 1):
                diff = p["acs_bc"][:, 128 * hd:128 * (hd + 1)] - p["acs_row"][hd:hd + 1, :]
                lmat = jnp.exp(jnp.where(p["tri"], diff, NEG_INF))
                ms.append((p["cbs"][hd // 4] * lmat * p["dtr"][hd:hd + 1, :]).astype(BF16))
            ys.append(_dot(jnp.concatenate(ms, axis=1), p["xpairs"][pr]) + y_off[c, pr])
        y = jnp.concatenate(ys, axis=1)
        if not rev:
            o_ref[rows, :] = y
        else:
            ytot = y + yf_ref[rows, :] + dsk_ref[...] * p["xs"].astype(F32)
            gated = ytot * _silu(z_ref[rows, :].astype(F32))
            ms = jnp.mean(gated * gated, axis=-1, keepdims=True)
            o_ref[rows, :] = (gated * lax.rsqrt(ms + EPS) * nrm_ref[...]).astype(BF16)


def _ssd(rev, xs, bc, dtc, dtr, p, ncc, nlc, yf=None, z=None):
    nt = xs.shape[0]
    assert ncc % SCAN_CHUNKS == 0 and nlc % SCAN_CHUNKS == 0
    ncc, nlc = ncc // SCAN_CHUNKS, nlc // SCAN_CHUNKS
    rows = SCAN_CHUNKS * T
    cmap = _scan_chunk_map(rev, ncc, nlc)
    blk = lambda w: pl.BlockSpec((rows, w), lambda j: (cmap(j), 0))
    in_specs = [blk(512), blk(256), blk(16), pl.BlockSpec((16, rows), lambda j: (0, cmap(j))),
                _full((1, 16)), _full((16, 1))]
    args = [xs, bc, dtc, dtr, p["alog_row"], p["alog_col"]]
    if rev:
        in_specs += [blk(512), blk(512), _full((1, 512)), _full((1, 512))]
        args += [yf, z, p["d_skip"], p["ssd_norm"]]
    return pl.pallas_call(
        functools.partial(_ssd_kernel, rev),
        grid=(ncc + nlc,),
        in_specs=in_specs,
        out_specs=blk(512),
        out_shape=jax.ShapeDtypeStruct((nt, 512), BF16 if rev else F32),
        scratch_shapes=[pltpu.VMEM((4, 128, 128), F32)],
        compiler_params=_cparams("arbitrary"),
        name="ssd_bwd" if rev else "ssd_fwd",
    )(*args)


def _attn_kernel(ncc, nblk, q_ref, kp_ref, kc_ref, kn_ref, vp_ref, vc_ref, vn_ref, kx_ref, vx_ref,
                 sink_ref, o_ref):
    j = pl.program_id(0)
    c_len = kx_ref.shape[0]
    r = lax.broadcasted_iota(I32, (T, T), 0)
    c = lax.broadcasted_iota(I32, (T, T), 1)
    zero = jnp.zeros((T, T), F32)
    ninf = jnp.full((T, T), NEG_INF, F32)
    lo = lax.broadcasted_iota(I32, (T, 128), 1) < 64
    zero_b = jnp.zeros((T, 128), BF16)
    sink = sink_ref[...]
    kblk = [kp_ref[...], kc_ref[0:T, :], kc_ref[T:2 * T, :], kn_ref[...]]
    vblk = [vp_ref[...], vc_ref[0:T, :], vc_ref[T:2 * T, :], vn_ref[...]]
    stacks = [[hd for hd in range(8) if (hd // 4 + hd % 2) % 2 == b] for b in range(2)]
    s_all, v_all = {}, {}
    for qb in range(2):
        jb = 2 * j + qb
        is_lat = jb >= ncc
        prev_ok = jnp.logical_and(is_lat, jb >= ncc + 1)
        next_ok = jnp.logical_and(is_lat, jb <= nblk - 2)
        bias = jnp.concatenate([
            jnp.where(jnp.logical_and(prev_ok, c >= r), zero, ninf),
            jnp.where(is_lat, zero, ninf),
            jnp.where(jnp.logical_and(next_ok, c <= r), zero, ninf),
            jnp.zeros((T, c_len), F32)], axis=1)
        bias4 = jnp.concatenate([bias] * 4, axis=0)
        k_all = jnp.concatenate(kblk[qb:qb + 3] + [kx_ref[...]], axis=0)
        v_all[qb] = jnp.concatenate(vblk[qb:qb + 3] + [vx_ref[...]], axis=0)
        q = q_ref[qb * T:(qb + 1) * T, :]
        for b in range(2):
            qs = []
            for hd in stacks[b]:
                qp = q[:, 128 * (hd // 2):128 * (hd // 2 + 1)]
                qs.append(jnp.where(lo, zero_b, qp) if hd % 2 else jnp.where(lo, qp, zero_b))
            s_all[qb, b] = _dot(jnp.concatenate(qs, axis=0), k_all[:, 128 * b:128 * (b + 1)], _NT) + bias4
    for qb in range(2):
        outs = {}
        for b in range(2):
            s = s_all[qb, b]
            sk = jnp.concatenate([jnp.broadcast_to(sink[:, hd:hd + 1], (T, 1)) for hd in stacks[b]], axis=0)
            m = jnp.maximum(jnp.max(s, axis=-1, keepdims=True), sk)
            pr = jnp.exp(s - m)
            den = jnp.sum(pr, axis=-1, keepdims=True) + jnp.exp(sk - m)
            o = _dot(pr.astype(BF16), v_all[qb][:, 128 * b:128 * (b + 1)]) / den
            for n, hd in enumerate(stacks[b]):
                outs[hd] = o[T * n:T * (n + 1)]
        for pair in range(4):
            o_ref[qb * T:(qb + 1) * T, 128 * pair:128 * (pair + 1)] = jnp.where(
                lo, outs[2 * pair], outs[2 * pair + 1]).astype(BF16)


def _attention(q, kk, vv, sink, ncc, nlc):
    nt = q.shape[0]
    nblk = ncc + nlc
    assert nblk % 2 == 0 and ncc % 2 == 0
    c_len = ncc * T
    prev = lambda w: pl.BlockSpec((T, w), lambda j: (jnp.maximum(2 * j - 1, 0), 0))
    cur = lambda w: pl.BlockSpec((2 * T, w), lambda j: (j, 0))
    nxt = lambda w: pl.BlockSpec((T, w), lambda j: (jnp.minimum(2 * j + 2, nblk - 1), 0))
    ctx = lambda w: pl.BlockSpec((c_len, w), lambda j: (0, 0))
    return pl.pallas_call(
        functools.partial(_attn_kernel, ncc, nblk),
        grid=(nblk // 2,),
        in_specs=[cur(512), prev(256), cur(256), nxt(256), prev(256), cur(256), nxt(256),
                  ctx(256), ctx(256), _full((1, 8))],
        out_specs=cur(512),
        out_shape=jax.ShapeDtypeStruct((nt, 512), BF16),
        compiler_params=_cparams("arbitrary"),
        name="window_attention",
    )(q, kk, kk, kk, vv, vv, vv, kk, vv, sink)


def _in1_kernel(nct, ntiles, xp_ref, x_ref, xn_ref, mod_ref, g_ref,
                wcat_ref, wg_ref, wgh_ref, cw_ref, cb_ref, gb_ref,
                q_ref, kt_ref, v_ref, o_ref, gc_ref, gr_ref):
    i = pl.program_id(0)
    sh = mod_ref[0, 0:1, :]
    sc = mod_ref[0, 1:2, :]
    g = g_ref[...]
    x_all = jnp.concatenate([xp_ref[...], x_ref[...], xn_ref[...]], axis=0)
    h_all = _norm_mod(x_all, g, sc, sh)
    hb_all = h_all.astype(BF16)
    main_all = _dot(hb_all, wcat_ref[...])
    h, hb, main = h_all[8:8 + TM], hb_all[8:8 + TM], main_all[8:8 + TM]
    prev_ok, next_ok = _seq_edges(i, nct, ntiles)
    x_prev = main_all[7:8, 0:2048] * prev_ok
    x_next = main_all[8 + TM:9 + TM, 0:2048] * next_ok
    act = _conv_silu(main[:, 0:2048], x_prev, x_next, cw_ref[...], cb_ref[...])
    q_ref[...] = act[:, 0:512].astype(BF16)
    kt_ref[...] = jnp.transpose(act[:, 512:1024] * 0.125).astype(BF16)
    v_ref[...] = act[:, 1024:2048].astype(BF16)
    o_ref[...] = main[:, 2048:3072].astype(BF16)
    h_lo = (h - hb.astype(F32)).astype(BF16)
    gates = _hi_lo_cols(_dot(hb, wg_ref[...]), _dot(h_lo, wgh_ref[...]), 32) + gb_ref[...]
    lane = lax.broadcasted_iota(I32, gates.shape, 1)
    gates = jnp.where(lane < 16, gates, _log_sigmoid(gates))
    gc_ref[...] = gates[:, 0:32]
    gr_ref[...] = jnp.transpose(gates)[0:32, :]


def _in1(x1, mod, p, nct):
    nt = x1.shape[0]
    ntiles = nt // TM
    tile = lambda w: pl.BlockSpec((TM, w), lambda i: (i, 0))
    return pl.pallas_call(
        functools.partial(_in1_kernel, nct, ntiles),
        grid=(ntiles,),
        in_specs=_halo_specs(nt, lambda i: i) + [
            _mod_spec(nct), _full((1, D)), _full((D, 3072)), _full((D, 128)), _full((D, 128)),
            _full((3, 2048)), _full((1, 2048)), _full((1, 128))],
        out_specs=[tile(512), pl.BlockSpec((512, TM), lambda i: (0, i)), tile(1024), tile(1024),
                   tile(32), pl.BlockSpec((32, TM), lambda i: (0, i))],
        out_shape=[jax.ShapeDtypeStruct((nt, 512), BF16), jax.ShapeDtypeStruct((512, nt), BF16),
                   jax.ShapeDtypeStruct((nt, 1024), BF16), jax.ShapeDtypeStruct((nt, 1024), BF16),
                   jax.ShapeDtypeStruct((nt, 32), F32), jax.ShapeDtypeStruct((32, nt), F32)],
        compiler_params=_cparams("arbitrary"),
        name="in_proj_odd",
    )(x1, x1, x1, mod, p["g_mix"], p["wcat"], p["wg"], p["wg_hi"], p["conv_w"], p["conv_b"], p["gate_bias"])


def _mlstm_kernel(rev, *refs):
    if rev:
        (q_ref, kt_ref, v_ref, gc_ref, gr_ref, hf_ref, og_ref, hn_ref, o_ref,
         c_ref, mc_ref, mr_ref) = refs
    else:
        q_ref, kt_ref, v_ref, gc_ref, gr_ref, o_ref, c_ref, mc_ref, mr_ref = refs
    j = pl.program_id(0)

    @pl.when(j == 0)
    def _():
        c_ref[...] = jnp.zeros_like(c_ref)
        mc_ref[...] = jnp.zeros_like(mc_ref)
        mr_ref[...] = jnp.zeros_like(mr_ref)

    d = 8 if rev else 0
    end = 0 if rev else T - 1
    order = tuple(reversed(range(SCAN_CHUNKS))) if rev else tuple(range(SCAN_CHUNKS))
    ones_b = jnp.ones((T, 128), BF16)
    sub = lax.broadcasted_iota(I32, (128, T), 0)
    zero_k = jnp.zeros((128, T), BF16)

    def prologue(c):
        rows = slice(c * T, (c + 1) * T)
        ig_col = gc_ref[rows, d:d + 8]
        lf_col = gc_ref[rows, 16 + d:24 + d]
        ig_row = gr_ref[d:d + 8, rows]
        lf_row = gr_ref[16 + d:24 + d, rows]
        tri, b_col, b_row = _cumsums(rev, lf_col, lf_row)
        blast_row = b_col[end:end + 1, :]
        blast_col = b_row[:, end:end + 1]
        wend_row = blast_col - b_row + ig_row
        ac_col = jnp.max(wend_row, axis=1, keepdims=True)
        return dict(tri=tri, b_row=b_row, ig_row=ig_row, blast_row=blast_row, blast_col=blast_col,
                    ac_col=ac_col, eend_row=jnp.exp(wend_row - ac_col),
                    ac_row=jnp.max(blast_row - b_col + ig_col, axis=0, keepdims=True),
                    b_bc=_lane_bcast(b_col), q=q_ref[rows, :])

    pro = {c: prologue(c) for c in order}
    m_col = mc_ref[:, 0:1]
    m_row = mr_ref[0:1, 0:8]
    for c in order:
        p = pro[c]
        mnew_col = jnp.maximum(p["blast_col"] + m_col, p["ac_col"])
        p["sp_col"] = jnp.exp(p["blast_col"] + m_col - mnew_col)
        p["sc_col"] = jnp.exp(p["ac_col"] - mnew_col)
        p["m_row"] = m_row
        m_col, m_row = mnew_col, jnp.maximum(p["blast_row"] + m_row, p["ac_row"])

    def head_matmuls(c, hd):
        p = pro[c]
        rows = slice(c * T, (c + 1) * T)
        pair, hi = hd // 2, hd % 2
        qp = p["q"][:, 128 * pair:128 * (pair + 1)]
        ktp = kt_ref[128 * pair:128 * (pair + 1), rows]
        kth = jnp.where((sub >= 64) if hi else (sub < 64), ktp, zero_k)
        vaug = jnp.concatenate([v_ref[rows, 128 * hd:128 * (hd + 1)], ones_b], axis=1)
        cst = c_ref[hd]
        sqk = _dot(qp, kth)
        inter = _dot(qp, cst.astype(BF16))
        kte = (kth.astype(F32) * p["eend_row"][hd:hd + 1, :]).astype(BF16)
        c_ref[hd] = p["sp_col"][hd:hd + 1, :] * cst + p["sc_col"][hd:hd + 1, :] * _dot(kte, vaug)
        return sqk, inter, vaug

    items = [(c, hd) for c in order for hd in range(8)]
    nxt = head_matmuls(*items[0])
    for n, (c, hd) in enumerate(items):
        p = pro[c]
        rows = slice(c * T, (c + 1) * T)
        cols = slice(128 * hd, 128 * (hd + 1))
        sqk, inter, vaug = nxt
        if n + 1 < len(items):
            nxt = head_matmuls(*items[n + 1])
        bh = p["b_bc"][:, cols]
        dlog = jnp.where(p["tri"], bh - p["b_row"][hd:hd + 1, :] + p["ig_row"][hd:hd + 1, :], NEG_INF)
        gh = bh + p["m_row"][:, hd:hd + 1]
        mstar = jnp.maximum(gh, jnp.max(dlog, axis=-1, keepdims=True))
        w = (jnp.exp(dlog - mstar) * sqk).astype(BF16)
        intra = _dot(w, vaug)
        e_int = jnp.exp(gh - mstar)
        den = jnp.maximum(jnp.abs(intra[:, 128:256] + e_int * inter[:, 128:256]), jnp.exp(-mstar))
        hh = (intra[:, 0:128] + e_int * inter[:, 0:128]) / den
        if rev:
            hh = hh + hf_ref[rows, cols]
            ms = jnp.mean(hh * hh, axis=-1, keepdims=True)
            hh = hh * lax.rsqrt(ms + EPS) * hn_ref[:, cols]
            og = og_ref[rows, cols].astype(F32)
            o_ref[rows, cols] = (hh * _sigmoid(og)).astype(BF16)
        else:
            o_ref[rows, cols] = hh
    mc_ref[...] = jnp.broadcast_to(m_col, mc_ref.shape)
    mr_ref[...] = jnp.broadcast_to(jnp.concatenate([m_row, jnp.zeros((1, 120), F32)], axis=1), mr_ref.shape)


def _mlstm(rev, q, kt, v, gc, gr, p, ncc, nlc, hf=None, og=None):
    nt = q.shape[0]
    assert ncc % SCAN_CHUNKS == 0 and nlc % SCAN_CHUNKS == 0
    ncc, nlc = ncc // SCAN_CHUNKS, nlc // SCAN_CHUNKS
    rows = SCAN_CHUNKS * T
    cmap = _scan_chunk_map(rev, ncc, nlc)
    blk = lambda w: pl.BlockSpec((rows, w), lambda j: (cmap(j), 0))
    blk_t = lambda h: pl.BlockSpec((h, rows), lambda j: (0, cmap(j)))
    in_specs = [blk(512), blk_t(512), blk(1024), blk(32), blk_t(32)]
    args = [q, kt, v, gc, gr]
    if rev:
        in_specs += [blk(1024), blk(1024), _full((1, 1024))]
        args += [hf, og, p["head_norm"]]
    return pl.pallas_call(
        functools.partial(_mlstm_kernel, rev),
        grid=(ncc + nlc,),
        in_specs=in_specs,
        out_specs=blk(1024),
        out_shape=jax.ShapeDtypeStruct((nt, 1024), BF16 if rev else F32),
        scratch_shapes=[pltpu.VMEM((8, 128, 256), F32), pltpu.VMEM((8, 128), F32), pltpu.VMEM((8, 128), F32)],
        compiler_params=_cparams("arbitrary"),
        name="mlstm_bwd" if rev else "mlstm_fwd",
    )(*args)


def _route(logits_t, rb_col):
    scores = _sigmoid(logits_t)
    biased = scores + rb_col
    row = lambda a, e: a[e:e + 1, :]
    gscore = []
    for g in range(4):
        b0, b1, b2, b3 = (row(biased, 4 * g + e) for e in range(4))
        h1, l1 = jnp.maximum(b0, b1), jnp.minimum(b0, b1)
        h2, l2 = jnp.maximum(b2, b3), jnp.minimum(b2, b3)
        gscore.append(jnp.maximum(h1, h2) + jnp.maximum(jnp.minimum(h1, h2), jnp.maximum(l1, l2)))
    gidx = jnp.zeros_like(gscore[0], dtype=I32)
    best = gscore[0]
    for g in range(1, 4):
        better = gscore[g] > best
        gidx = jnp.where(better, g, gidx)
        best = jnp.where(better, gscore[g], best)

    def pick(a, e):
        out = row(a, e)
        for g in range(1, 4):
            out = jnp.where(gidx == g, row(a, 4 * g + e), out)
        return out

    sb = [pick(biased, e) for e in range(4)]
    i1 = jnp.zeros_like(gidx)
    v1 = sb[0]
    for e in range(1, 4):
        better = sb[e] > v1
        i1 = jnp.where(better, e, i1)
        v1 = jnp.where(better, sb[e], v1)
    i2 = jnp.zeros_like(gidx)
    v2 = jnp.full_like(v1, NEG_INF)
    for e in range(4):
        better = jnp.logical_and(i1 != e, sb[e] > v2)
        i2 = jnp.where(better, e, i2)
        v2 = jnp.where(better, sb[e], v2)
    a = jnp.minimum(i1, i2)
    b = jnp.maximum(i1, i2)
    pair = jnp.where(a == 0, jnp.where(b == 1, 0, jnp.where(b == 2, 2, 3)), jnp.where(a == 1, jnp.where(b == 2, 1, 4), 5))
    return 6 * gidx + pair


def _out_kernel(nmix, nct, two_src, sub, t0, *refs):
    per = nmix + (2 if two_src else 1)
    tile_refs = [refs[per * u:per * (u + 1)] for u in range(sub)]
    (w_ref, mod_ref, g_ref, rw_ref, rwh_ref, rb_ref,
     xmid_ref, hrow_ref, bucket_ref, rank_ref, cnt_ref, cnt_scr) = refs[per * sub:]
    i = pl.program_id(0)

    @pl.when(i == 0)
    def _():
        cnt_scr[...] = jnp.zeros_like(cnt_scr)

    brow = lax.broadcasted_iota(I32, (NB_PAD, TM), 0)
    r = lax.broadcasted_iota(I32, (TM, TM), 0)
    c = lax.broadcasted_iota(I32, (TM, TM), 1)
    before = (r < c).astype(F32).astype(BF16)
    onehots = []
    for u in range(sub):
        mix_refs = tile_refs[u][:nmix]
        is_ctx = sub * i + u + t0 < nct
        mod = jnp.where(is_ctx, mod_ref[1], mod_ref[0])
        mix = mix_refs[0][...] if nmix == 1 else jnp.concatenate([mr[...] for mr in mix_refs], axis=1)
        if two_src:
            x = jnp.where(is_ctx, tile_refs[u][nmix][...], tile_refs[u][nmix + 1][...])
        else:
            x = tile_refs[u][nmix][...]
        x_mid = x + mod[2:3, :] * _dot(mix, w_ref[...])
        xmid_ref[TM * u:TM * (u + 1), :] = x_mid
        h = _norm_mod(x_mid, g_ref[...], mod[4:5, :], mod[3:4, :])
        hb = h.astype(BF16)
        h_lo = (h - hb.astype(F32)).astype(BF16)
        logits = _hi_lo_cols(_dot(hb, rw_ref[...]), _dot(h_lo, rwh_ref[...]), 16)
        logits_t = jnp.transpose(logits)[0:16, :]
        bucket = _route(logits_t, rb_ref[...])
        hrow_u = hrow_ref.at[pl.ds(TM * TOK * u, TM * TOK), :]
        for cblk in range(TOK):
            _tok_store(hrow_u, cblk, h[:, 128 * cblk:128 * (cblk + 1)])
        bucket_ref[u] = bucket
        onehots.append((brow == bucket).astype(F32))
    cnt = cnt_scr[...]
    for u in range(sub):
        onehot = onehots[u]
        cum = _dot(onehot.astype(BF16), before)
        rank_ref[u] = jnp.sum(onehot * (cum + cnt[:, 0:1]), axis=0, keepdims=True).astype(I32)
        cnt = cnt + jnp.sum(onehot, axis=1, keepdims=True)
    cnt_scr[...] = cnt
    cnt_ref[...] = cnt.astype(I32)


def _out_proj(mixes, w_out, xs, mod, g_ffn, rw, rw_hi, rb_col, t0, ntiles, nct):
    nmix = len(mixes)
    two_src = len(xs) == 2
    n = ntiles * TM
    sub = max(s for s in (5, 4, 3, 2, 1) if ntiles % s == 0)
    in_specs, args = [], []
    for u in range(sub):
        tile_of = lambda i, u=u: sub * i + u + t0
        for mx in mixes:
            in_specs.append(pl.BlockSpec((TM, mx.shape[1]), lambda i, f=tile_of: (f(i), 0)))
            args.append(mx)
        if two_src:
            in_specs += [pl.BlockSpec((TM, D), lambda i, f=tile_of: (jnp.minimum(f(i), nct - 1), 0)),
                         pl.BlockSpec((TM, D), lambda i, f=tile_of: (jnp.maximum(f(i) - nct, 0), 0))]
        else:
            in_specs.append(pl.BlockSpec((TM, D), lambda i, f=tile_of: (f(i), 0)))
        args += list(xs)
    in_specs += [_full((D, D)), _full((2, 6, D)), _full((1, D)), _full((D, 128)), _full((D, 128)), _full((16, 1))]
    rows_out = pl.BlockSpec((sub, 1, TM), lambda i: (i, 0, 0))
    return pl.pallas_call(
        functools.partial(_out_kernel, nmix, nct, two_src, sub, t0),
        grid=(ntiles // sub,),
        in_specs=in_specs,
        out_specs=[pl.BlockSpec((sub * TM, D), lambda i: (i, 0)),
                   pl.BlockSpec((sub * TM * TOK, 128), lambda i: (i, 0)), rows_out, rows_out,
                   _full((NB_PAD, 128))],
        out_shape=[jax.ShapeDtypeStruct((n, D), F32), jax.ShapeDtypeStruct((n * TOK, 128), F32),
                   jax.ShapeDtypeStruct((ntiles, 1, TM), I32), jax.ShapeDtypeStruct((ntiles, 1, TM), I32),
                   jax.ShapeDtypeStruct((NB_PAD, 128), I32)],
        scratch_shapes=[pltpu.VMEM((NB_PAD, 128), F32)],
        compiler_params=_cparams("arbitrary"),
        name="out_proj_router",
    )(*args, w_out, mod, g_ffn, rw, rw_hi, rb_col)


def _scatter_kernel(ntiles, pos_ref, flo_ref, fhi_ref, src_ref, dst_ref, hbuf, in_sem, out_sem):
    i = pl.program_id(0)
    slot = i % 3
    h_ref = hbuf.at[slot]
    sem = out_sem.at[slot]
    rows = TM * TOK

    def load(tile, s):
        return pltpu.make_async_copy(src_ref.at[pl.ds(pl.multiple_of(tile * rows, rows), rows), :],
                                     hbuf.at[s], in_sem.at[s])

    def tok(ref, t):
        return ref.at[pl.ds(pl.multiple_of(t * TOK, TOK), TOK), :]

    def copy(r, d_row, src=h_ref, sm=sem):
        return pltpu.make_async_copy(tok(src, r), tok(dst_ref, d_row), sm)

    def wait_rows(lo, hi, unroll, s=slot):
        def body(r, carry):
            copy(0, 0, hbuf.at[s], out_sem.at[s]).wait()
            return carry
        lax.fori_loop(lo, hi, body, 0, unroll=unroll)

    @pl.when(i == 0)
    def _():
        load(0, 0).start()
        if ntiles > 1:
            load(1, 1).start()

    load(i, slot).wait()

    def start(r8, c):
        for k in range(8):
            r = r8 * 8 + k
            copy(r, pos_ref[i * TM + r]).start(priority=k % 2)
        return c
    lax.fori_loop(0, TM // 8, start, 0)

    @pl.when(i > 0)
    def _():
        wait_rows(0, TM, 8, (i + 2) % 3)

    @pl.when(i + 2 < ntiles)
    def _():
        load(i + 2, (i + 2) % 3).start()

    @pl.when(i == ntiles - 1)
    def _():
        wait_rows(0, TM, 8)
        for b in range(N_BUCKETS):
            lo, hi = flo_ref[b], fhi_ref[b]

            def fill(r, c):
                copy(0, r).start()
                return c
            lax.fori_loop(lo, hi, fill, 0)
            wait_rows(lo, hi, 1)

        def tile_copy(j):
            rows = TMM * TOK
            return pltpu.make_async_copy(h_ref, dst_ref.at[pl.ds(pl.multiple_of(j * rows, rows), rows), :], sem)

        def fill_tile(j, c):
            tile_copy(j).start()
            return c

        def wait_tile(j, c):
            tile_copy(j).wait()
            return c
        lax.fori_loop(flo_ref[N_BUCKETS], fhi_ref[N_BUCKETS], fill_tile, 0)
        lax.fori_loop(flo_ref[N_BUCKETS], fhi_ref[N_BUCKETS], wait_tile, 0)


def _scatter_rows(hrow, pos, fill_lo, fill_hi, n, npad):
    assert TM == TMM
    return pl.pallas_call(
        functools.partial(_scatter_kernel, n // TM),
        grid_spec=pltpu.PrefetchScalarGridSpec(
            num_scalar_prefetch=3,
            grid=(n // TM,),
            in_specs=[pl.BlockSpec(memory_space=pl.ANY)],
            out_specs=pl.BlockSpec(memory_space=pl.ANY),
            scratch_shapes=[pltpu.VMEM((3, TM * TOK, 128), hrow.dtype), pltpu.SemaphoreType.DMA((3,)),
                            pltpu.SemaphoreType.DMA((3,))],
        ),
        out_shape=jax.ShapeDtypeStruct((npad * TOK, 128), hrow.dtype),
        compiler_params=_cparams("arbitrary"),
        name="moe_scatter_rows",
    )(pos, fill_lo, fill_hi, hrow)


def _combine_kernel(ntiles, pos_ref, x_ref, mod_ref, ys_ref, o_ref, ybuf, sem):
    i = pl.program_id(0)

    def copy(tile, slot, r):
        src = pl.multiple_of(pos_ref[tile * TM + r] * TOK, TOK)
        dst = pl.multiple_of(r * TOK, TOK)
        return pltpu.make_async_copy(ys_ref.at[pl.ds(src, TOK), :], ybuf.at[slot, pl.ds(dst, TOK), :], sem.at[slot])

    def start_tile(tile, slot):
        def body(r8, carry):
            for k in range(8):
                copy(tile, slot, r8 * 8 + k).start(priority=k % 2)
            return carry
        lax.fori_loop(0, TM // 8, body, 0)

    @pl.when(i == 0)
    def _():
        start_tile(0, 0)

    @pl.when(i + 1 < ntiles)
    def _():
        start_tile(i + 1, (i + 1) % 2)

    slot = i % 2

    def wait_body(r, carry):
        copy(i, slot, 0).wait()
        return carry
    lax.fori_loop(0, TM, wait_body, 0, unroll=8)
    o_ref[...] = x_ref[...] + mod_ref[0, 5:6, :] * _tok_rows(ybuf.at[slot], TM)


def _combine(x_mid, mod, ys, pos, t0, nct):
    n = x_mid.shape[0]
    ntiles = n // TM
    return pl.pallas_call(
        functools.partial(_combine_kernel, ntiles),
        grid_spec=pltpu.PrefetchScalarGridSpec(
            num_scalar_prefetch=1,
            grid=(ntiles,),
            in_specs=[pl.BlockSpec((TM, D), lambda i, *_: (i, 0)),
                      pl.BlockSpec((1, 6, D), lambda i, *_: (jnp.where(i + t0 < nct, 1, 0), 0, 0)),
                      pl.BlockSpec(memory_space=pl.ANY)],
            out_specs=pl.BlockSpec((TM, D), lambda i, *_: (i, 0)),
            scratch_shapes=[pltpu.VMEM((2, TM * TOK, 128), F32), pltpu.SemaphoreType.DMA((2,))],
        ),
        out_shape=jax.ShapeDtypeStruct((n, D), F32),
        compiler_params=_cparams("arbitrary"),
        name="moe_combine",
    )(pos, x_mid, mod, ys)


def _moe_kernel(tea_ref, teb_ref, tblk_ref, tval_ref, x_ref, rw_ref, w1a_ref, w3a_ref, w2a_ref,
                w1b_ref, w3b_ref, w2b_ref, y_ref):
    del tblk_ref
    j = pl.program_id(0)

    @pl.when(tval_ref[j] != 0)
    def _():
        h = _tok_rows(x_ref, TMM)
        hb = h.astype(BF16)
        logits = _hi_lo_cols(_dot(hb, rw_ref[...]), 0.0, 16)
        scores = _sigmoid(logits)
        lane = lax.broadcasted_iota(I32, scores.shape, 1)
        s_a = jnp.sum(jnp.where(lane == tea_ref[j], scores, 0.0), axis=1, keepdims=True)
        s_b = jnp.sum(jnp.where(lane == teb_ref[j], scores, 0.0), axis=1, keepdims=True)
        gates = (s_a / (s_a + s_b), s_b / (s_a + s_b))
        acts = []
        for w1_ref, w3_ref, gate in ((w1a_ref, w3a_ref, gates[0]), (w1b_ref, w3b_ref, gates[1])):
            u = _dot(hb, w1_ref[0, 0].astype(BF16))
            v = _dot(hb, w3_ref[0, 0].astype(BF16))
            acts.append((_silu(u) * v * gate).astype(BF16))
        y = _dot(acts[0], w2a_ref[0, 0].astype(BF16)) + _dot(acts[1], w2b_ref[0, 0].astype(BF16))
        for cblk in range(TOK):
            _tok_store(y_ref, cblk, y[:, 128 * cblk:128 * (cblk + 1)])

    @pl.when(tval_ref[j] == 0)
    def _():
        y_ref[...] = jnp.zeros_like(y_ref)


def _moe(xs_sorted, rw, w1, w3, w2, layer, tile_ea, tile_eb, tile_blk, tile_valid):
    npad = xs_sorted.shape[0] // TOK
    ntile = npad // TMM
    wspec = lambda shape, which: pl.BlockSpec(
        (1, 1) + shape, lambda j, ea, eb, blk, val: (layer, (ea, eb)[which][j], 0, 0))
    up, down = (D, D_EXPERT), (D_EXPERT, D)
    return pl.pallas_call(
        _moe_kernel,
        grid_spec=pltpu.PrefetchScalarGridSpec(
            num_scalar_prefetch=4,
            grid=(ntile,),
            in_specs=[pl.BlockSpec((TMM * TOK, 128), lambda j, ea, eb, blk, val: (blk[j], 0)),
                      pl.BlockSpec((D, 128), lambda j, *_: (0, 0)),
                      wspec(up, 0), wspec(up, 0), wspec(down, 0), wspec(up, 1), wspec(up, 1), wspec(down, 1)],
            out_specs=pl.BlockSpec((TMM * TOK, 128), lambda j, ea, eb, blk, val: (j, 0)),
        ),
        out_shape=jax.ShapeDtypeStruct((npad * TOK, 128), F32),
        compiler_params=_cparams("arbitrary"),
        name="moe_experts",
    )(tile_ea, tile_eb, tile_blk, tile_valid, xs_sorted, rw, w1, w3, w2, w1, w3, w2)


def _moe_block(x_mid, hrow, bucket, rank, counts, mod, rw, rw_hi, w1, w3, w2, layer, t0, nct):
    n = hrow.shape[0] // TOK
    ntile = n // TMM + N_BUCKETS
    npad = ntile * TMM
    cnt = counts[:N_BUCKETS, 0]
    padded = ((cnt + TMM - 1) // TMM) * TMM
    ends = jnp.cumsum(padded)
    starts = ends - padded
    total_tiles = ends[-1] // TMM
    tiles = jnp.arange(ntile, dtype=I32)
    tile_valid = (tiles < total_tiles).astype(I32)
    tile_blk = jnp.minimum(tiles, jnp.maximum(total_tiles - 1, 0))
    tile_bucket = jnp.minimum(jnp.sum((ends[None, :] <= (tile_blk * TMM)[:, None]).astype(I32), axis=1), N_BUCKETS - 1)
    pair = tile_bucket % 6
    grp = tile_bucket // 6
    slot_a = sum(jnp.where(pair == k, e, 0) for k, (e, _) in enumerate(_PAIR_SLOTS))
    slot_b = sum(jnp.where(pair == k, e, 0) for k, (_, e) in enumerate(_PAIR_SLOTS))
    tile_ea = (4 * grp + slot_a).astype(I32)
    tile_eb = (4 * grp + slot_b).astype(I32)
    bucket = bucket.reshape(-1)
    onehot = (bucket[:, None] == jnp.arange(N_BUCKETS, dtype=I32)[None, :]).astype(I32)
    pos = (rank.reshape(-1) + jnp.sum(onehot * starts[None, :], axis=1)).astype(I32)
    pad32 = lambda a, tail: jnp.zeros((NB_PAD,), I32).at[:N_BUCKETS].set(a.astype(I32)).at[N_BUCKETS].set(tail)
    fill_lo, fill_hi = pad32(starts + cnt, total_tiles), pad32(ends, ntile)
    xs_sorted = _scatter_rows(hrow, pos, fill_lo, fill_hi, n, npad)
    ys = _moe(xs_sorted, rw, w1, w3, w2, layer, tile_ea, tile_eb, tile_blk, tile_valid)
    return _combine(x_mid, mod, ys, pos, t0, nct)


def kernel(x, c, ctx, c_ctx, router_w, router_b, norm_mix, norm_ffn, w_mod, b_mod, ev_w_in, ev_conv_w, ev_conv_b, ev_dt_bias, ev_a_log, ev_d_skip, ev_ssd_norm, ev_q_norm, ev_k_norm, ev_sink, ev_w_out, od_w_in, od_conv_w, od_conv_b, od_igate_b, od_fgate_b, od_head_norm, od_w_out, moe_w1, moe_w3, moe_w2):
    s_len = x.shape[1]
    c_len = ctx.shape[1]
    assert x.shape[0] == 1 and s_len % TM == 0 and c_len % TM == 0 and s_len % GRID_W == 0
    nct = c_len // TM
    ncc, nlc = c_len // T, s_len // T
    nt = c_len + s_len
    ntiles = nt // TM

    mod = _modulation(c, c_ctx, w_mod, b_mod)
    rw, rw_hi = _hi_lo_weight(router_w)
    rb_col = router_b.reshape(N_EXPERTS, 1)
    pad128 = lambda v: jnp.zeros((1, 128), F32).at[0, :v.shape[0]].set(v)

    w = ev_w_in[0]
    rope_rows, rope_cols = _rope_tables(s_len)
    wdt, wdt_hi = _hi_lo_weight(w[:, 1280:1296])
    p0 = dict(
        g_mix=norm_mix[0].reshape(1, D),
        w_zx=w[:, 0:1280].astype(BF16), w_qkv=w[:, 1296:2064].astype(BF16), wdt=wdt, wdt_hi=wdt_hi,
        conv_w=ev_conv_w[0], conv_b=ev_conv_b[0].reshape(1, 768),
        dt_bias=pad128(ev_dt_bias[0].reshape(16)),
        q_norm=jnp.tile(ev_q_norm[0], 8).reshape(1, 512), k_norm=jnp.tile(ev_k_norm[0], 2).reshape(1, 128),
        rope_rows=rope_rows, rope_cols=rope_cols,
        alog_row=ev_a_log[0].reshape(1, 16), alog_col=ev_a_log[0].reshape(16, 1),
        d_skip=jnp.repeat(ev_d_skip[0], 64).reshape(1, 512), ssd_norm=ev_ssd_norm[0].reshape(1, 512))
    z, xs, bc, q, kk, vv, dtc, dtr = _in0(ctx[0], x[0], mod[0], p0, nct)
    yf = _ssd(False, xs, bc, dtc, dtr, p0, ncc, nlc)
    ymix = _ssd(True, xs, bc, dtc, dtr, p0, ncc, nlc, yf=yf, z=z)
    att = _attention(q, kk, vv, ev_sink[0].reshape(1, 8), ncc, nlc)
    x_mid0, hrow, bucket, rank, counts = _out_proj(
        [ymix, att], ev_w_out[0].astype(BF16), [ctx[0], x[0]], mod[0], norm_ffn[0].reshape(1, D),
        rw, rw_hi, rb_col, 0, ntiles, nct)
    x1 = _moe_block(x_mid0, hrow, bucket, rank, counts, mod[0], rw, rw_hi, moe_w1, moe_w3, moe_w2, 0, 0, nct)

    w = od_w_in[0]
    wg, wg_hi = _hi_lo_weight(w[:, 3072:3104])
    p1 = dict(
        g_mix=norm_mix[1].reshape(1, D),
        wcat=w[:, 0:3072].astype(BF16), wg=wg, wg_hi=wg_hi,
        conv_w=od_conv_w[0], conv_b=od_conv_b[0].reshape(1, 2048),
        gate_bias=pad128(jnp.concatenate([od_igate_b[0].reshape(16), od_fgate_b[0].reshape(16)])),
        head_norm=od_head_norm[0].reshape(1, 1024))
    q1, kt1, v1, og1, gc1, gr1 = _in1(x1, mod[1], p1, nct)
    hf = _mlstm(False, q1, kt1, v1, gc1, gr1, p1, ncc, nlc)
    hmix = _mlstm(True, q1, kt1, v1, gc1, gr1, p1, ncc, nlc, hf=hf, og=og1)
    x_mid1, hrow, bucket, rank, counts = _out_proj(
        [hmix], od_w_out[0].astype(BF16), [x1], mod[1], norm_ffn[1].reshape(1, D),
        rw, rw_hi, rb_col, nct, ntiles - nct, nct)
    return _moe_block(x_mid1, hrow, bucket, rank, counts, mod[1], rw, rw_hi, moe_w1, moe_w3, moe_w2, 1, nct,
                      nct)[None]
```

```python
import functools
import math

import jax
import jax.numpy as jnp
from jax import lax
from jax.experimental import pallas as pl
from jax.experimental.pallas import tpu as pltpu

F32 = jnp.float32
BF16 = jnp.bfloat16
I32 = jnp.int32

EPS = 1e-6
D = 1024
T = 128
SCAN_CHUNKS = 2
TM = 256
TMM = 256
GRID_W = 64
ROPE_THETA = 10000.0
N_EXPERTS = 16
N_BUCKETS = 24
NB_PAD = 32
D_EXPERT = 512
TOK = 8
NEG_INF = float("-inf")
VMEM_LIMIT = 56 * 1024 * 1024

_NN = (((1,), (0,)), ((), ()))
_NT = (((1,), (1,)), ((), ()))
_TN = (((0,), (0,)), ((), ()))

_PAIR_SLOTS = ((0, 1), (2, 1), (2, 0), (3, 0), (3, 1), (3, 2))


def _cparams(*sem):
    return pltpu.CompilerParams(dimension_semantics=sem, vmem_limit_bytes=VMEM_LIMIT)


def _dot(a, b, dims=_NN):
    return lax.dot_general(a, b, dims, preferred_element_type=F32)


def _split(a, n):
    out = []
    r = a
    for _ in range(n):
        t = r.astype(BF16)
        out.append(t)
        r = r - t.astype(F32)
    return out


def _mdot(as_, bs, dims=_NN, order=None):
    if order is None:
        order = len(as_) + len(bs) - 2
    acc = None
    for i, a in enumerate(as_):
        for j, b in enumerate(bs):
            if i + j <= order:
                p = _dot(a, b, dims)
                acc = p if acc is None else acc + p
    return acc


def _sigmoid(x):
    return 1.0 / (1.0 + jnp.exp(-x))


def _silu(x):
    return x * _sigmoid(x)


def _log1p_exp_neg_abs(x):
    e = jnp.exp(-jnp.abs(x))
    u = 1.0 + e
    um1 = u - 1.0
    return jnp.where(um1 == 0.0, e, jnp.log(u) * (e / jnp.where(um1 == 0.0, 1.0, um1)))


def _softplus(x):
    return jnp.maximum(x, 0.0) + _log1p_exp_neg_abs(x)


def _log_sigmoid(x):
    return jnp.minimum(x, 0.0) - _log1p_exp_neg_abs(x)


def _norm_mod(x, g, sc, sh):
    ms = jnp.mean(x * x, axis=-1, keepdims=True)
    return (x * lax.rsqrt(ms + EPS)) * g * (1.0 + sc) + sh


def _tok_load(ref, chunk, n):
    return ref[pl.ds(chunk, n, stride=TOK), :]


def _tok_store(ref, chunk, val):
    ref[pl.ds(chunk, val.shape[0], stride=TOK), :] = val


def _tok_rows(ref, n):
    return jnp.concatenate([_tok_load(ref, c, n) for c in range(TOK)], axis=1)


def _tri(rev):
    r = lax.broadcasted_iota(I32, (T, T), 0)
    c = lax.broadcasted_iota(I32, (T, T), 1)
    return (c >= r) if rev else (c <= r)


def _cumsums(rev, col, row):
    tri = _tri(rev)
    tri_b = tri.astype(F32).astype(BF16)
    trit_b = _tri(not rev).astype(F32).astype(BF16)
    ccol = _mdot([tri_b], _split(col, 3))
    crow = _mdot(_split(row, 3), [trit_b])
    return tri, ccol, crow


def _lane_bcast(col):
    n = col.shape[1]
    col = jnp.concatenate([col, jnp.zeros((col.shape[0], 128 - n), F32)], axis=1)
    r = lax.broadcasted_iota(I32, (128, n * 128), 0)
    c = lax.broadcasted_iota(I32, (128, n * 128), 1) // 128
    return _mdot(_split(col, 3), [(r == c).astype(F32).astype(BF16)])


def _mod_kernel(c_ref, w_ref, b_ref, o_ref):
    a = _silu(c_ref[...])
    o_ref[0] = _mdot(_split(a, 2), _split(w_ref[0], 2), order=1) + b_ref[0]


def _modulation(c, c_ctx, w_mod, b_mod):
    depth = w_mod.shape[0]
    n = w_mod.shape[2]
    tn = 1536
    cc = jnp.zeros((8, D), F32).at[0].set(c[0]).at[1].set(c_ctx)
    out = pl.pallas_call(
        _mod_kernel,
        grid=(depth, n // tn),
        in_specs=[
            pl.BlockSpec((8, D), lambda l, j: (0, 0)),
            pl.BlockSpec((1, D, tn), lambda l, j: (l, 0, j)),
            pl.BlockSpec((1, 1, tn), lambda l, j: (l, 0, j)),
        ],
        out_specs=pl.BlockSpec((1, 8, tn), lambda l, j: (l, 0, j)),
        out_shape=jax.ShapeDtypeStruct((depth, 8, n), F32),
        compiler_params=_cparams("arbitrary", "arbitrary"),
        name="modulation",
    )(cc, w_mod, b_mod.reshape(depth, 1, n))
    return out[:, :2].reshape(depth, 2, 6, D)


def _halo_specs(nrows, tile_of):
    nb8 = nrows // 8
    return [
        pl.BlockSpec((8, D), lambda i: (jnp.maximum(tile_of(i) * (TM // 8) - 1, 0), 0)),
        pl.BlockSpec((TM, D), lambda i: (tile_of(i), 0)),
        pl.BlockSpec((8, D), lambda i: (jnp.minimum((tile_of(i) + 1) * (TM // 8), nb8 - 1), 0)),
    ]


def _ctx_tile(nct):
    return lambda i: jnp.minimum(i, nct - 1)


def _lat_tile(nct):
    return lambda i: jnp.maximum(i - nct, 0)


def _mod_spec(nct, t0=0):
    return pl.BlockSpec((1, 6, D), lambda i: (jnp.where(i + t0 < nct, 1, 0), 0, 0))


def _full(shape):
    nd = len(shape)
    return pl.BlockSpec(shape, lambda i: (0,) * nd)


def _seq_edges(i, nct, ntiles):
    prev_ok = jnp.logical_and(i != 0, i != nct).astype(F32)
    next_ok = jnp.logical_and(i != nct - 1, i != ntiles - 1).astype(F32)
    return prev_ok, next_ok


def _conv_silu(x, x_first_prev, x_last_next, cw, cb):
    n = x.shape[0]
    rows = lax.broadcasted_iota(I32, x.shape, 0)
    x_prev = jnp.where(rows == 0, x_first_prev, pltpu.roll(x, 1, 0))
    x_next = jnp.where(rows == n - 1, x_last_next, pltpu.roll(x, n - 1, 0))
    return _silu(x_prev * cw[0:1] + x * cw[1:2] + x_next * cw[2:3] + cb)


def _hi_lo_cols(blk, lo_pass, n):
    return blk + pltpu.roll(blk, 128 - n, 1) + lo_pass


def _hi_lo_weight(w):
    n = w.shape[1]
    hi = w.astype(BF16)
    lo = (w - hi.astype(F32)).astype(BF16)
    z = jnp.zeros((w.shape[0], 128 - 2 * n), BF16)
    return jnp.concatenate([hi, lo, z], axis=1), jnp.concatenate([hi, jnp.zeros_like(lo), z], axis=1)


def _head_rms(xf, gamma):
    r = lax.broadcasted_iota(I32, (128, 128), 0) // 64
    c = lax.broadcasted_iota(I32, (128, 128), 1) // 64
    ones_bd = (r == c).astype(F32).astype(BF16)
    outs = []
    for j in range(xf.shape[1] // 128):
        blk = xf[:, 128 * j:128 * (j + 1)]
        ssum = _dot((blk * blk).astype(BF16), ones_bd)
        outs.append(blk * lax.rsqrt(ssum * (1.0 / 64.0) + EPS))
    return jnp.concatenate(outs, axis=1) * gamma


def _rope(xf, cos, sin):
    lane = lax.broadcasted_iota(I32, (xf.shape[0], 128), 1)
    first = (lane % 32) < 16
    outs = []
    for j in range(xf.shape[1] // 128):
        blk = xf[:, 128 * j:128 * (j + 1)]
        partner = jnp.where(first, pltpu.roll(blk, 112, 1), pltpu.roll(blk, 16, 1))
        outs.append(blk * cos + partner * sin)
    return jnp.concatenate(outs, axis=1)


def _rope_tables(s_len):
    lane = jnp.arange(128, dtype=I32)
    inv = ROPE_THETA ** (-(lane % 16).astype(F32) / 16.0)
    sign = jnp.where((lane % 32) < 16, -1.0, 1.0).astype(F32)
    ang_r = jnp.arange(s_len // GRID_W, dtype=F32)[:, None] * inv[None, :]
    ang_c = jnp.tile(jnp.arange(GRID_W, dtype=F32), TM // GRID_W)[:, None] * inv[None, :]
    both = lambda ang: jnp.stack([jnp.cos(ang), jnp.sin(ang) * sign[None, :]])
    return both(ang_r), both(ang_c)


def _in0_kernel(nct, ntiles, cp_ref, c_ref, cn_ref, xp_ref, x_ref, xn_ref, mod_ref, g_ref, wa_ref, wb_ref,
                wdt_ref, wdth_ref, cw_ref, cb_ref, dtb_ref, qn_ref, kn_ref, rowcs_ref, colcs_ref,
                z_ref, xs_ref, bc_ref, q_ref, kk_ref, vv_ref, dtc_ref, dtr_ref):
    i = pl.program_id(0)
    is_ctx = i < nct
    sh = mod_ref[0, 0:1, :]
    sc = mod_ref[0, 1:2, :]
    g = g_ref[...]
    x_all = jnp.concatenate([jnp.where(is_ctx, cp_ref[...], xp_ref[...]),
                             jnp.where(is_ctx, c_ref[...], x_ref[...]),
                             jnp.where(is_ctx, cn_ref[...], xn_ref[...])], axis=0)
    h_all = _norm_mod(x_all, g, sc, sh)
    hb_all = h_all.astype(BF16)
    zx_all = _dot(hb_all, wa_ref[...])
    h, hb, zx = h_all[8:8 + TM], hb_all[8:8 + TM], zx_all[8:8 + TM]
    qkv = _dot(hb, wb_ref[...])
    prev_ok, next_ok = _seq_edges(i, nct, ntiles)
    xb_prev = zx_all[7:8, 512:1280] * prev_ok
    xb_next = zx_all[8 + TM:9 + TM, 512:1280] * next_ok
    act = _conv_silu(zx[:, 512:1280], xb_prev, xb_next, cw_ref[...], cb_ref[...])
    z_ref[...] = zx[:, 0:512].astype(BF16)
    xs_ref[...] = act[:, 0:512].astype(BF16)
    bc_ref[...] = act[:, 512:768].astype(BF16)
    row0 = jnp.maximum(i - nct, 0) * (TM // GRID_W)
    rowcs = [jnp.concatenate([jnp.broadcast_to(rowcs_ref[t, pl.ds(row0 + kq, 1), :], (GRID_W, 128))
                              for kq in range(TM // GRID_W)], axis=0) for t in range(2)]
    row_lanes = (lax.broadcasted_iota(I32, (TM, 128), 1) % 64) < 32
    cos = jnp.where(is_ctx, 1.0, jnp.where(row_lanes, rowcs[0], colcs_ref[0]))
    sin = jnp.where(is_ctx, 0.0, jnp.where(row_lanes, rowcs[1], colcs_ref[1]))
    q = _rope(_head_rms(qkv[:, 0:512], qn_ref[...]), cos, sin) * 0.125
    q_ref[...] = q.astype(BF16)
    k = _rope(_head_rms(qkv[:, 512:640], kn_ref[...]), cos, sin)
    kk_ref[...] = jnp.concatenate([k, pltpu.roll(k, 64, 1)], axis=1).astype(BF16)
    v = qkv[:, 640:768]
    vv_ref[...] = jnp.concatenate([v, pltpu.roll(v, 64, 1)], axis=1).astype(BF16)
    h_lo = (h - hb.astype(F32)).astype(BF16)
    dt = _softplus(_hi_lo_cols(_dot(hb, wdt_ref[...]), _dot(h_lo, wdth_ref[...]), 16) + dtb_ref[...])
    dtc_ref[...] = dt[:, 0:16]
    dtr_ref[...] = jnp.transpose(dt)[0:16, :]


def _in0(ctx2, x2, mod, p, nct):
    c_len, s_len = ctx2.shape[0], x2.shape[0]
    nt = c_len + s_len
    ntiles = nt // TM
    tile = lambda w: pl.BlockSpec((TM, w), lambda i: (i, 0))
    lat = _lat_tile(nct)
    outs = [(512, BF16), (512, BF16), (256, BF16), (512, BF16), (256, BF16), (256, BF16), (16, F32)]
    return pl.pallas_call(
        functools.partial(_in0_kernel, nct, ntiles),
        grid=(ntiles,),
        in_specs=_halo_specs(c_len, _ctx_tile(nct)) + _halo_specs(s_len, lat) + [
            _mod_spec(nct), _full((1, D)), _full((D, 1280)), _full((D, 768)), _full((D, 128)), _full((D, 128)),
            _full((3, 768)), _full((1, 768)), _full((1, 128)),
            _full((1, 512)), _full((1, 128)), _full((2, s_len // GRID_W, 128)), _full((2, TM, 128))],
        out_specs=[tile(w) for w, _ in outs] + [pl.BlockSpec((16, TM), lambda i: (0, i))],
        out_shape=[jax.ShapeDtypeStruct((nt, w), dt) for w, dt in outs]
        + [jax.ShapeDtypeStruct((16, nt), F32)],
        compiler_params=_cparams("arbitrary"),
        name="in_proj_even",
    )(ctx2, ctx2, ctx2, x2, x2, x2, mod, p["g_mix"], p["w_zx"], p["w_qkv"], p["wdt"], p["wdt_hi"],
      p["conv_w"], p["conv_b"], p["dt_bias"], p["q_norm"], p["k_norm"], p["rope_rows"], p["rope_cols"])


def _scan_chunk_map(rev, ncc, nlc):
    if not rev:
        return lambda j: j
    return lambda j: jnp.where(j < ncc, ncc - 1 - j, ncc + nlc - 1 - (j - ncc))


def _ssd_kernel(rev, *refs):
    if rev:
        (xs_ref, bc_ref, dtc_ref, dtr_ref, alr_ref, alc_ref, yf_ref, z_ref, dsk_ref, nrm_ref,
         o_ref, st_ref) = refs
    else:
        xs_ref, bc_ref, dtc_ref, dtr_ref, alr_ref, alc_ref, o_ref, st_ref = refs
    j = pl.program_id(0)

    @pl.when(j == 0)
    def _():
        st_ref[...] = jnp.zeros_like(st_ref)

    d = 8 if rev else 0
    a_coef_row = -jnp.exp(alr_ref[...])[:, d:d + 8]
    a_coef_col = -jnp.exp(alc_ref[...])[d:d + 8, :]
    lane = lax.broadcasted_iota(I32, (T, 128), 1)
    lo = lane < 64
    zero_b = jnp.zeros((T, 128), BF16)
    hi_half = jnp.logical_not(lo)
    sub = lax.broadcasted_iota(I32, (128, T), 0)
    eye = (sub == lax.broadcasted_iota(I32, (128, T), 1)).astype(F32).astype(BF16)
    end = 0 if rev else T - 1
    order = tuple(reversed(range(SCAN_CHUNKS))) if rev else tuple(range(SCAN_CHUNKS))

    def prologue(c):
        rows = slice(c * T, (c + 1) * T)
        dtc = dtc_ref[rows, d:d + 8]
        dtr = dtr_ref[d:d + 8, rows]
        tri, acs_col, acs_row = _cumsums(rev, dtc * a_coef_row, dtr * a_coef_col)
        atot_col = acs_row[:, end:end + 1]
        acs_bc = _lane_bcast(acs_col)
        dec_row = jnp.exp(atot_col - acs_row) * dtr
        xs = xs_ref[rows, :]
        bm = bc_ref[rows, 0:128]
        cm = bc_ref[rows, 128:256]
        cgs = [jnp.where(lo, cm, zero_b), jnp.where(hi_half, cm, zero_b)]
        cbs = [_dot(cgs[g], bm, _NT) for g in range(2)]
        bmt = _dot(eye, bm, _NT)
        bgts = [jnp.where(sub < 64, bmt, 0.0), jnp.where(sub >= 64, bmt, 0.0)]
        xpairs = []
        for pr in range(4):
            xp = xs[:, 128 * pr:128 * (pr + 1)]
            xpairs.append(jnp.concatenate([jnp.where(lo, xp, zero_b), jnp.where(hi_half, xp, zero_b)], axis=0))
        return dict(tri=tri, acs_row=acs_row, acs_bc=acs_bc, atot_col=atot_col, dec_row=dec_row, dtr=dtr,
                    xs=xs, cgs=cgs, cbs=cbs, bgts=bgts, xpairs=xpairs)

    pro = {c: prologue(c) for c in order}
    y_off = {}
    for c in order:
        p = pro[c]
        for pr in range(4):
            g, h0, h1 = pr // 2, 2 * pr, 2 * pr + 1
            st = st_ref[pr]
            eacs = jnp.exp(jnp.where(lo, p["acs_bc"][:, 128 * h0:128 * (h0 + 1)],
                                     p["acs_bc"][:, 128 * h1:128 * (h1 + 1)]))
            y_off[c, pr] = _dot(p["cgs"][g], st.astype(BF16)) * eacs
            bdec = jnp.concatenate([(p["bgts"][g] * p["dec_row"][h0:h0 + 1, :]).astype(BF16),
                                    (p["bgts"][g] * p["dec_row"][h1:h1 + 1, :]).astype(BF16)], axis=1)
            carry = jnp.where(lo[0:1, :], jnp.exp(p["atot_col"][h0:h0 + 1, :]), jnp.exp(p["atot_col"][h1:h1 + 1, :]))
            st_ref[pr] = carry * st + _dot(bdec, p["xpairs"][pr])
    for c in order:
        p = pro[c]
        rows = slice(c * T, (c + 1) * T)
        ys = []
        for pr in range(4):
            ms = []
            for hd in (2 * pr, 2 * pr + 1):
                diff = p["acs_bc"][:, 128 * hd:128 * (hd + 1)] - p["acs_row"][hd:hd + 1, :]
                lmat = jnp.exp(jnp.where(p["tri"], diff, NEG_INF))
                ms.append((p["cbs"][hd // 4] * lmat * p["dtr"][hd:hd + 1, :]).astype(BF16))
            ys.append(_dot(jnp.concatenate(ms, axis=1), p["xpairs"][pr]) + y_off[c, pr])
        y = jnp.concatenate(ys, axis=1)
        if not rev:
            o_ref[rows, :] = y
        else:
            ytot = y + yf_ref[rows, :] + dsk_ref[...] * p["xs"].astype(F32)
            gated = ytot * _silu(z_ref[rows, :].astype(F32))
            ms = jnp.mean(gated * gated, axis=-1, keepdims=True)
            o_ref[rows, :] = (gated * lax.rsqrt(ms + EPS) * nrm_ref[...]).astype(BF16)


def _ssd(rev, xs, bc, dtc, dtr, p, ncc, nlc, yf=None, z=None):
    nt = xs.shape[0]
    assert ncc % SCAN_CHUNKS == 0 and nlc % SCAN_CHUNKS == 0
    ncc, nlc = ncc // SCAN_CHUNKS, nlc // SCAN_CHUNKS
    rows = SCAN_CHUNKS * T
    cmap = _scan_chunk_map(rev, ncc, nlc)
    blk = lambda w: pl.BlockSpec((rows, w), lambda j: (cmap(j), 0))
    in_specs = [blk(512), blk(256), blk(16), pl.BlockSpec((16, rows), lambda j: (0, cmap(j))),
                _full((1, 16)), _full((16, 1))]
    args = [xs, bc, dtc, dtr, p["alog_row"], p["alog_col"]]
    if rev:
        in_specs += [blk(512), blk(512), _full((1, 512)), _full((1, 512))]
        args += [yf, z, p["d_skip"], p["ssd_norm"]]
    return pl.pallas_call(
        functools.partial(_ssd_kernel, rev),
        grid=(ncc + nlc,),
        in_specs=in_specs,
        out_specs=blk(512),
        out_shape=jax.ShapeDtypeStruct((nt, 512), BF16 if rev else F32),
        scratch_shapes=[pltpu.VMEM((4, 128, 128), F32)],
        compiler_params=_cparams("arbitrary"),
        name="ssd_bwd" if rev else "ssd_fwd",
    )(*args)


def _attn_kernel(ncc, nblk, q_ref, kp_ref, kc_ref, kn_ref, vp_ref, vc_ref, vn_ref, kx_ref, vx_ref,
                 sink_ref, o_ref):
    j = pl.program_id(0)
    c_len = kx_ref.shape[0]
    r = lax.broadcasted_iota(I32, (T, T), 0)
    c = lax.broadcasted_iota(I32, (T, T), 1)
    zero = jnp.zeros((T, T), F32)
    ninf = jnp.full((T, T), NEG_INF, F32)
    lo = lax.broadcasted_iota(I32, (T, 128), 1) < 64
    zero_b = jnp.zeros((T, 128), BF16)
    sink = sink_ref[...]
    kblk = [kp_ref[...], kc_ref[0:T, :], kc_ref[T:2 * T, :], kn_ref[...]]
    vblk = [vp_ref[...], vc_ref[0:T, :], vc_ref[T:2 * T, :], vn_ref[...]]
    stacks = [[hd for hd in range(8) if (hd // 4 + hd % 2) % 2 == b] for b in range(2)]
    s_all, v_all = {}, {}
    for qb in range(2):
        jb = 2 * j + qb
        is_lat = jb >= ncc
        prev_ok = jnp.logical_and(is_lat, jb >= ncc + 1)
        next_ok = jnp.logical_and(is_lat, jb <= nblk - 2)
        bias = jnp.concatenate([
            jnp.where(jnp.logical_and(prev_ok, c >= r), zero, ninf),
            jnp.where(is_lat, zero, ninf),
            jnp.where(jnp.logical_and(next_ok, c <= r), zero, ninf),
            jnp.zeros((T, c_len), F32)], axis=1)
        bias4 = jnp.concatenate([bias] * 4, axis=0)
        k_all = jnp.concatenate(kblk[qb:qb + 3] + [kx_ref[...]], axis=0)
        v_all[qb] = jnp.concatenate(vblk[qb:qb + 3] + [vx_ref[...]], axis=0)
        q = q_ref[qb * T:(qb + 1) * T, :]
        for b in range(2):
            qs = []
            for hd in stacks[b]:
                qp = q[:, 128 * (hd // 2):128 * (hd // 2 + 1)]
                qs.append(jnp.where(lo, zero_b, qp) if hd % 2 else jnp.where(lo, qp, zero_b))
            s_all[qb, b] = _dot(jnp.concatenate(qs, axis=0), k_all[:, 128 * b:128 * (b + 1)], _NT) + bias4
    for qb in range(2):
        outs = {}
        for b in range(2):
            s = s_all[qb, b]
            sk = jnp.concatenate([jnp.broadcast_to(sink[:, hd:hd + 1], (T, 1)) for hd in stacks[b]], axis=0)
            m = jnp.maximum(jnp.max(s, axis=-1, keepdims=True), sk)
            pr = jnp.exp(s - m)
            den = jnp.sum(pr, axis=-1, keepdims=True) + jnp.exp(sk - m)
            o = _dot(pr.astype(BF16), v_all[qb][:, 128 * b:128 * (b + 1)]) / den
            for n, hd in enumerate(stacks[b]):
                outs[hd] = o[T * n:T * (n + 1)]
        for pair in range(4):
            o_ref[qb * T:(qb + 1) * T, 128 * pair:128 * (pair + 1)] = jnp.where(
                lo, outs[2 * pair], outs[2 * pair + 1]).astype(BF16)


def _attention(q, kk, vv, sink, ncc, nlc):
    nt = q.shape[0]
    nblk = ncc + nlc
    assert nblk % 2 == 0 and ncc % 2 == 0
    c_len = ncc * T
    prev = lambda w: pl.BlockSpec((T, w), lambda j: (jnp.maximum(2 * j - 1, 0), 0))
    cur = lambda w: pl.BlockSpec((2 * T, w), lambda j: (j, 0))
    nxt = lambda w: pl.BlockSpec((T, w), lambda j: (jnp.minimum(2 * j + 2, nblk - 1), 0))
    ctx = lambda w: pl.BlockSpec((c_len, w), lambda j: (0, 0))
    return pl.pallas_call(
        functools.partial(_attn_kernel, ncc, nblk),
        grid=(nblk // 2,),
        in_specs=[cur(512), prev(256), cur(256), nxt(256), prev(256), cur(256), nxt(256),
                  ctx(256), ctx(256), _full((1, 8))],
        out_specs=cur(512),
        out_shape=jax.ShapeDtypeStruct((nt, 512), BF16),
        compiler_params=_cparams("arbitrary"),
        name="window_attention",
    )(q, kk, kk, kk, vv, vv, vv, kk, vv, sink)


def _in1_kernel(nct, ntiles, xp_ref, x_ref, xn_ref, mod_ref, g_ref,
                wcat_ref, wg_ref, wgh_ref, cw_ref, cb_ref, gb_ref,
                q_ref, kt_ref, v_ref, o_ref, gc_ref, gr_ref):
    i = pl.program_id(0)
    sh = mod_ref[0, 0:1, :]
    sc = mod_ref[0, 1:2, :]
    g = g_ref[...]
    x_all = jnp.concatenate([xp_ref[...], x_ref[...], xn_ref[...]], axis=0)
    h_all = _norm_mod(x_all, g, sc, sh)
    hb_all = h_all.astype(BF16)
    main_all = _dot(hb_all, wcat_ref[...])
    h, hb, main = h_all[8:8 + TM], hb_all[8:8 + TM], main_all[8:8 + TM]
    prev_ok, next_ok = _seq_edges(i, nct, ntiles)
    x_prev = main_all[7:8, 0:2048] * prev_ok
    x_next = main_all[8 + TM:9 + TM, 0:2048] * next_ok
    act = _conv_silu(main[:, 0:2048], x_prev, x_next, cw_ref[...], cb_ref[...])
    q_ref[...] = act[:, 0:512].astype(BF16)
    kt_ref[...] = jnp.transpose(act[:, 512:1024] * 0.125).astype(BF16)
    v_ref[...] = act[:, 1024:2048].astype(BF16)
    o_ref[...] = main[:, 2048:3072].astype(BF16)
    h_lo = (h - hb.astype(F32)).astype(BF16)
    gates = _hi_lo_cols(_dot(hb, wg_ref[...]), _dot(h_lo, wgh_ref[...]), 32) + gb_ref[...]
    lane = lax.broadcasted_iota(I32, gates.shape, 1)
    gates = jnp.where(lane < 16, gates, _log_sigmoid(gates))
    gc_ref[...] = gates[:, 0:32]
    gr_ref[...] = jnp.transpose(gates)[0:32, :]


def _in1(x1, mod, p, nct):
    nt = x1.shape[0]
    ntiles = nt // TM
    tile = lambda w: pl.BlockSpec((TM, w), lambda i: (i, 0))
    return pl.pallas_call(
        functools.partial(_in1_kernel, nct, ntiles),
        grid=(ntiles,),
        in_specs=_halo_specs(nt, lambda i: i) + [
            _mod_spec(nct), _full((1, D)), _full((D, 3072)), _full((D, 128)), _full((D, 128)),
            _full((3, 2048)), _full((1, 2048)), _full((1, 128))],
        out_specs=[tile(512), pl.BlockSpec((512, TM), lambda i: (0, i)), tile(1024), tile(1024),
                   tile(32), pl.BlockSpec((32, TM), lambda i: (0, i))],
        out_shape=[jax.ShapeDtypeStruct((nt, 512), BF16), jax.ShapeDtypeStruct((512, nt), BF16),
                   jax.ShapeDtypeStruct((nt, 1024), BF16), jax.ShapeDtypeStruct((nt, 1024), BF16),
                   jax.ShapeDtypeStruct((nt, 32), F32), jax.ShapeDtypeStruct((32, nt), F32)],
        compiler_params=_cparams("arbitrary"),
        name="in_proj_odd",
    )(x1, x1, x1, mod, p["g_mix"], p["wcat"], p["wg"], p["wg_hi"], p["conv_w"], p["conv_b"], p["gate_bias"])


def _mlstm_kernel(rev, *refs):
    if rev:
        (q_ref, kt_ref, v_ref, gc_ref, gr_ref, hf_ref, og_ref, hn_ref, o_ref,
         c_ref, mc_ref, mr_ref) = refs
    else:
        q_ref, kt_ref, v_ref, gc_ref, gr_ref, o_ref, c_ref, mc_ref, mr_ref = refs
    j = pl.program_id(0)

    @pl.when(j == 0)
    def _():
        c_ref[...] = jnp.zeros_like(c_ref)
        mc_ref[...] = jnp.zeros_like(mc_ref)
        mr_ref[...] = jnp.zeros_like(mr_ref)

    d = 8 if rev else 0
    end = 0 if rev else T - 1
    order = tuple(reversed(range(SCAN_CHUNKS))) if rev else tuple(range(SCAN_CHUNKS))
    ones_b = jnp.ones((T, 128), BF16)
    sub = lax.broadcasted_iota(I32, (128, T), 0)
    zero_k = jnp.zeros((128, T), BF16)

    def prologue(c):
        rows = slice(c * T, (c + 1) * T)
        ig_col = gc_ref[rows, d:d + 8]
        lf_col = gc_ref[rows, 16 + d:24 + d]
        ig_row = gr_ref[d:d + 8, rows]
        lf_row = gr_ref[16 + d:24 + d, rows]
        tri, b_col, b_row = _cumsums(rev, lf_col, lf_row)
        blast_row = b_col[end:end + 1, :]
        blast_col = b_row[:, end:end + 1]
        wend_row = blast_col - b_row + ig_row
        ac_col = jnp.max(wend_row, axis=1, keepdims=True)
        return dict(tri=tri, b_row=b_row, ig_row=ig_row, blast_row=blast_row, blast_col=blast_col,
                    ac_col=ac_col, eend_row=jnp.exp(wend_row - ac_col),
                    ac_row=jnp.max(blast_row - b_col + ig_col, axis=0, keepdims=True),
                    b_bc=_lane_bcast(b_col), q=q_ref[rows, :])

    pro = {c: prologue(c) for c in order}
    m_col = mc_ref[:, 0:1]
    m_row = mr_ref[0:1, 0:8]
    for c in order:
        p = pro[c]
        mnew_col = jnp.maximum(p["blast_col"] + m_col, p["ac_col"])
        p["sp_col"] = jnp.exp(p["blast_col"] + m_col - mnew_col)
        p["sc_col"] = jnp.exp(p["ac_col"] - mnew_col)
        p["m_row"] = m_row
        m_col, m_row = mnew_col, jnp.maximum(p["blast_row"] + m_row, p["ac_row"])

    def head_matmuls(c, hd):
        p = pro[c]
        rows = slice(c * T, (c + 1) * T)
        pair, hi = hd // 2, hd % 2
        qp = p["q"][:, 128 * pair:128 * (pair + 1)]
        ktp = kt_ref[128 * pair:128 * (pair + 1), rows]
        kth = jnp.where((sub >= 64) if hi else (sub < 64), ktp, zero_k)
        vaug = jnp.concatenate([v_ref[rows, 128 * hd:128 * (hd + 1)], ones_b], axis=1)
        cst = c_ref[hd]
        sqk = _dot(qp, kth)
        inter = _dot(qp, cst.astype(BF16))
        kte = (kth.astype(F32) * p["eend_row"][hd:hd + 1, :]).astype(BF16)
        c_ref[hd] = p["sp_col"][hd:hd + 1, :] * cst + p["sc_col"][hd:hd + 1, :] * _dot(kte, vaug)
        return sqk, inter, vaug

    items = [(c, hd) for c in order for hd in range(8)]
    nxt = head_matmuls(*items[0])
    for n, (c, hd) in enumerate(items):
        p = pro[c]
        rows = slice(c * T, (c + 1) * T)
        cols = slice(128 * hd, 128 * (hd + 1))
        sqk, inter, vaug = nxt
        if n + 1 < len(items):
            nxt = head_matmuls(*items[n + 1])
        bh = p["b_bc"][:, cols]
        dlog = jnp.where(p["tri"], bh - p["b_row"][hd:hd + 1, :] + p["ig_row"][hd:hd + 1, :], NEG_INF)
        gh = bh + p["m_row"][:, hd:hd + 1]
        mstar = jnp.maximum(gh, jnp.max(dlog, axis=-1, keepdims=True))
        w = (jnp.exp(dlog - mstar) * sqk).astype(BF16)
        intra = _dot(w, vaug)
        e_int = jnp.exp(gh - mstar)
        den = jnp.maximum(jnp.abs(intra[:, 128:256] + e_int * inter[:, 128:256]), jnp.exp(-mstar))
        hh = (intra[:, 0:128] + e_int * inter[:, 0:128]) / den
        if rev:
            hh = hh + hf_ref[rows, cols]
            ms = jnp.mean(hh * hh, axis=-1, keepdims=True)
            hh = hh * lax.rsqrt(ms + EPS) * hn_ref[:, cols]
            og = og_ref[rows, cols].astype(F32)
            o_ref[rows, cols] = (hh * _sigmoid(og)).astype(BF16)
        else:
            o_ref[rows, cols] = hh
    mc_ref[...] = jnp.broadcast_to(m_col, mc_ref.shape)
    mr_ref[...] = jnp.broadcast_to(jnp.concatenate([m_row, jnp.zeros((1, 120), F32)], axis=1), mr_ref.shape)


def _mlstm(rev, q, kt, v, gc, gr, p, ncc, nlc, hf=None, og=None):
    nt = q.shape[0]
    assert ncc % SCAN_CHUNKS == 0 and nlc % SCAN_CHUNKS == 0
    ncc, nlc = ncc // SCAN_CHUNKS, nlc // SCAN_CHUNKS
    rows = SCAN_CHUNKS * T
    cmap = _scan_chunk_map(rev, ncc, nlc)
    blk = lambda w: pl.BlockSpec((rows, w), lambda j: (cmap(j), 0))
    blk_t = lambda h: pl.BlockSpec((h, rows), lambda j: (0, cmap(j)))
    in_specs = [blk(512), blk_t(512), blk(1024), blk(32), blk_t(32)]
    args = [q, kt, v, gc, gr]
    if rev:
        in_specs += [blk(1024), blk(1024), _full((1, 1024))]
        args += [hf, og, p["head_norm"]]
    return pl.pallas_call(
        functools.partial(_mlstm_kernel, rev),
        grid=(ncc + nlc,),
        in_specs=in_specs,
        out_specs=blk(1024),
        out_shape=jax.ShapeDtypeStruct((nt, 1024), BF16 if rev else F32),
        scratch_shapes=[pltpu.VMEM((8, 128, 256), F32), pltpu.VMEM((8, 128), F32), pltpu.VMEM((8, 128), F32)],
        compiler_params=_cparams("arbitrary"),
        name="mlstm_bwd" if rev else "mlstm_fwd",
    )(*args)


def _route(logits_t, rb_col):
    scores = _sigmoid(logits_t)
    biased = scores + rb_col
    row = lambda a, e: a[e:e + 1, :]
    gscore = []
    for g in range(4):
        b0, b1, b2, b3 = (row(biased, 4 * g + e) for e in range(4))
        h1, l1 = jnp.maximum(b0, b1), jnp.minimum(b0, b1)
        h2, l2 = jnp.maximum(b2, b3), jnp.minimum(b2, b3)
        gscore.append(jnp.maximum(h1, h2) + jnp.maximum(jnp.minimum(h1, h2), jnp.maximum(l1, l2)))
    gidx = jnp.zeros_like(gscore[0], dtype=I32)
    best = gscore[0]
    for g in range(1, 4):
        better = gscore[g] > best
        gidx = jnp.where(better, g, gidx)
        best = jnp.where(better, gscore[g], best)

    def pick(a, e):
        out = row(a, e)
        for g in range(1, 4):
            out = jnp.where(gidx == g, row(a, 4 * g + e), out)
        return out

    sb = [pick(biased, e) for e in range(4)]
    i1 = jnp.zeros_like(gidx)
    v1 = sb[0]
    for e in range(1, 4):
        better = sb[e] > v1
        i1 = jnp.where(better, e, i1)
        v1 = jnp.where(better, sb[e], v1)
    i2 = jnp.zeros_like(gidx)
    v2 = jnp.full_like(v1, NEG_INF)
    for e in range(4):
        better = jnp.logical_and(i1 != e, sb[e] > v2)
        i2 = jnp.where(better, e, i2)
        v2 = jnp.where(better, sb[e], v2)
    a = jnp.minimum(i1, i2)
    b = jnp.maximum(i1, i2)
    pair = jnp.where(a == 0, jnp.where(b == 1, 0, jnp.where(b == 2, 2, 3)), jnp.where(a == 1, jnp.where(b == 2, 1, 4), 5))
    return 6 * gidx + pair


def _out_kernel(nmix, nct, two_src, sub, t0, *refs):
    per = nmix + (2 if two_src else 1)
    tile_refs = [refs[per * u:per * (u + 1)] for u in range(sub)]
    (w_ref, mod_ref, g_ref, rw_ref, rwh_ref, rb_ref,
     xmid_ref, hrow_ref, bucket_ref, rank_ref, cnt_ref, cnt_scr) = refs[per * sub:]
    i = pl.program_id(0)

    @pl.when(i == 0)
    def _():
        cnt_scr[...] = jnp.zeros_like(cnt_scr)

    brow = lax.broadcasted_iota(I32, (NB_PAD, TM), 0)
    r = lax.broadcasted_iota(I32, (TM, TM), 0)
    c = lax.broadcasted_iota(I32, (TM, TM), 1)
    before = (r < c).astype(F32).astype(BF16)
    onehots = []
    for u in range(sub):
        mix_refs = tile_refs[u][:nmix]
        is_ctx = sub * i + u + t0 < nct
        mod = jnp.where(is_ctx, mod_ref[1], mod_ref[0])
        mix = mix_refs[0][...] if nmix == 1 else jnp.concatenate([mr[...] for mr in mix_refs], axis=1)
        if two_src:
            x = jnp.where(is_ctx, tile_refs[u][nmix][...], tile_refs[u][nmix + 1][...])
        else:
            x = tile_refs[u][nmix][...]
        x_mid = x + mod[2:3, :] * _dot(mix, w_ref[...])
        xmid_ref[TM * u:TM * (u + 1), :] = x_mid
        h = _norm_mod(x_mid, g_ref[...], mod[4:5, :], mod[3:4, :])
        hb = h.astype(BF16)
        h_lo = (h - hb.astype(F32)).astype(BF16)
        logits = _hi_lo_cols(_dot(hb, rw_ref[...]), _dot(h_lo, rwh_ref[...]), 16)
        logits_t = jnp.transpose(logits)[0:16, :]
        bucket = _route(logits_t, rb_ref[...])
        hrow_u = hrow_ref.at[pl.ds(TM * TOK * u, TM * TOK), :]
        for cblk in range(TOK):
            _tok_store(hrow_u, cblk, h[:, 128 * cblk:128 * (cblk + 1)])
        bucket_ref[u] = bucket
        onehots.append((brow == bucket).astype(F32))
    cnt = cnt_scr[...]
    for u in range(sub):
        onehot = onehots[u]
        cum = _dot(onehot.astype(BF16), before)
        rank_ref[u] = jnp.sum(onehot * (cum + cnt[:, 0:1]), axis=0, keepdims=True).astype(I32)
        cnt = cnt + jnp.sum(onehot, axis=1, keepdims=True)
    cnt_scr[...] = cnt
    cnt_ref[...] = cnt.astype(I32)


def _out_proj(mixes, w_out, xs, mod, g_ffn, rw, rw_hi, rb_col, t0, ntiles, nct):
    nmix = len(mixes)
    two_src = len(xs) == 2
    n = ntiles * TM
    sub = max(s for s in (5, 4, 3, 2, 1) if ntiles % s == 0)
    in_specs, args = [], []
    for u in range(sub):
        tile_of = lambda i, u=u: sub * i + u + t0
        for mx in mixes:
            in_specs.append(pl.BlockSpec((TM, mx.shape[1]), lambda i, f=tile_of: (f(i), 0)))
            args.append(mx)
        if two_src:
            in_specs += [pl.BlockSpec((TM, D), lambda i, f=tile_of: (jnp.minimum(f(i), nct - 1), 0)),
                         pl.BlockSpec((TM, D), lambda i, f=tile_of: (jnp.maximum(f(i) - nct, 0), 0))]
        else:
            in_specs.append(pl.BlockSpec((TM, D), lambda i, f=tile_of: (f(i), 0)))
        args += list(xs)
    in_specs += [_full((D, D)), _full((2, 6, D)), _full((1, D)), _full((D, 128)), _full((D, 128)), _full((16, 1))]
    rows_out = pl.BlockSpec((sub, 1, TM), lambda i: (i, 0, 0))
    return pl.pallas_call(
        functools.partial(_out_kernel, nmix, nct, two_src, sub, t0),
        grid=(ntiles // sub,),
        in_specs=in_specs,
        out_specs=[pl.BlockSpec((sub * TM, D), lambda i: (i, 0)),
                   pl.BlockSpec((sub * TM * TOK, 128), lambda i: (i, 0)), rows_out, rows_out,
                   _full((NB_PAD, 128))],
        out_shape=[jax.ShapeDtypeStruct((n, D), F32), jax.ShapeDtypeStruct((n * TOK, 128), F32),
                   jax.ShapeDtypeStruct((ntiles, 1, TM), I32), jax.ShapeDtypeStruct((ntiles, 1, TM), I32),
                   jax.ShapeDtypeStruct((NB_PAD, 128), I32)],
        scratch_shapes=[pltpu.VMEM((NB_PAD, 128), F32)],
        compiler_params=_cparams("arbitrary"),
        name="out_proj_router",
    )(*args, w_out, mod, g_ffn, rw, rw_hi, rb_col)


def _combine_kernel(ntiles, pos_ref, x_ref, mod_ref, ys_ref, o_ref, ybuf, sem):
    i = pl.program_id(0)

    def copy(tile, slot, r):
        src = pl.multiple_of(pos_ref[tile * TM + r] * TOK, TOK)
        dst = pl.multiple_of(r * TOK, TOK)
        return pltpu.make_async_copy(ys_ref.at[pl.ds(src, TOK), :], ybuf.at[slot, pl.ds(dst, TOK), :], sem.at[slot])

    def start_tile(tile, slot):
        def body(r8, carry):
            for k in range(8):
                copy(tile, slot, r8 * 8 + k).start(priority=k % 2)
            return carry
        lax.fori_loop(0, TM // 8, body, 0)

    @pl.when(i == 0)
    def _():
        start_tile(0, 0)

    @pl.when(i + 1 < ntiles)
    def _():
        start_tile(i + 1, (i + 1) % 2)

    slot = i % 2

    def wait_body(r, carry):
        copy(i, slot, 0).wait()
        return carry
    lax.fori_loop(0, TM, wait_body, 0, unroll=8)
    o_ref[...] = x_ref[...] + mod_ref[0, 5:6, :] * _tok_rows(ybuf.at[slot], TM)


def _combine(x_mid, mod, ys, pos, t0, nct):
    n = x_mid.shape[0]
    ntiles = n // TM
    return pl.pallas_call(
        functools.partial(_combine_kernel, ntiles),
        grid_spec=pltpu.PrefetchScalarGridSpec(
            num_scalar_prefetch=1,
            grid=(ntiles,),
            in_specs=[pl.BlockSpec((TM, D), lambda i, *_: (i, 0)),
                      pl.BlockSpec((1, 6, D), lambda i, *_: (jnp.where(i + t0 < nct, 1, 0), 0, 0)),
                      pl.BlockSpec(memory_space=pl.ANY)],
            out_specs=pl.BlockSpec((TM, D), lambda i, *_: (i, 0)),
            scratch_shapes=[pltpu.VMEM((2, TM * TOK, 128), F32), pltpu.SemaphoreType.DMA((2,))],
        ),
        out_shape=jax.ShapeDtypeStruct((n, D), F32),
        compiler_params=_cparams("arbitrary"),
        name="moe_combine",
    )(pos, x_mid, mod, ys)


def _inv_kernel(n, npad, pos_ref, inv_ref):
    def clear(r, c):
        inv_ref[r] = 0
        return c
    lax.fori_loop(0, npad, clear, 0, unroll=8)

    def put(t, c):
        inv_ref[pos_ref[t]] = t
        return c
    lax.fori_loop(0, n, put, 0, unroll=8)


def _inverse_positions(pos, n, npad):
    return pl.pallas_call(
        functools.partial(_inv_kernel, n, npad),
        grid_spec=pltpu.PrefetchScalarGridSpec(
            num_scalar_prefetch=1, grid=(1,), in_specs=[],
            out_specs=pl.BlockSpec(memory_space=pltpu.SMEM)),
        out_shape=jax.ShapeDtypeStruct((npad,), I32),
        compiler_params=_cparams("arbitrary"),
        name="moe_inverse_positions",
    )(pos)


def _moe_kernel(ntile, tea_ref, teb_ref, tval_ref, inv_ref, x_ref, rw_ref, w1a_ref, w3a_ref, w2a_ref,
                w1b_ref, w3b_ref, w2b_ref, y_ref, xbuf, sem):
    j = pl.program_id(0)

    def copy(tile, slot, r):
        src = pl.multiple_of(inv_ref[tile * TMM + r] * TOK, TOK)
        dst = pl.multiple_of(r * TOK, TOK)
        return pltpu.make_async_copy(x_ref.at[pl.ds(src, TOK), :], xbuf.at[slot, pl.ds(dst, TOK), :], sem.at[slot])

    def start_tile(tile, slot):
        def body(r8, carry):
            for k in range(8):
                copy(tile, slot, r8 * 8 + k).start(priority=k % 2)
            return carry
        lax.fori_loop(0, TMM // 8, body, 0)

    @pl.when(j == 0)
    def _():
        start_tile(0, 0)

    nxt = jnp.minimum(j + 1, ntile - 1)

    @pl.when(jnp.logical_and(j + 1 < ntile, tval_ref[nxt] != 0))
    def _():
        start_tile(j + 1, (j + 1) % 2)

    @pl.when(jnp.logical_or(tval_ref[j] != 0, j == 0))
    def _():
        def wait_body(r, carry):
            copy(j, j % 2, 0).wait()
            return carry
        lax.fori_loop(0, TMM, wait_body, 0, unroll=8)

    @pl.when(tval_ref[j] != 0)
    def _():
        h = _tok_rows(xbuf.at[j % 2], TMM)
        hb = h.astype(BF16)
        logits = _hi_lo_cols(_dot(hb, rw_ref[...]), 0.0, 16)
        scores = _sigmoid(logits)
        lane = lax.broadcasted_iota(I32, scores.shape, 1)
        s_a = jnp.sum(jnp.where(lane == tea_ref[j], scores, 0.0), axis=1, keepdims=True)
        s_b = jnp.sum(jnp.where(lane == teb_ref[j], scores, 0.0), axis=1, keepdims=True)
        gates = (s_a / (s_a + s_b), s_b / (s_a + s_b))
        acts = []
        for w1_ref, w3_ref, gate in ((w1a_ref, w3a_ref, gates[0]), (w1b_ref, w3b_ref, gates[1])):
            u = _dot(hb, w1_ref[0, 0].astype(BF16))
            v = _dot(hb, w3_ref[0, 0].astype(BF16))
            acts.append((_silu(u) * v * gate).astype(BF16))
        y = _dot(acts[0], w2a_ref[0, 0].astype(BF16)) + _dot(acts[1], w2b_ref[0, 0].astype(BF16))
        for cblk in range(TOK):
            _tok_store(y_ref, cblk, y[:, 128 * cblk:128 * (cblk + 1)])

    @pl.when(tval_ref[j] == 0)
    def _():
        y_ref[...] = jnp.zeros_like(y_ref)


def _moe(hrow, inv, rw, w1, w3, w2, layer, tile_ea, tile_eb, tile_valid, npad):
    ntile = npad // TMM
    wspec = lambda shape, which: pl.BlockSpec(
        (1, 1) + shape, lambda j, ea, eb, val, inv: (layer, (ea, eb)[which][j], 0, 0))
    up, down = (D, D_EXPERT), (D_EXPERT, D)
    return pl.pallas_call(
        functools.partial(_moe_kernel, ntile),
        grid_spec=pltpu.PrefetchScalarGridSpec(
            num_scalar_prefetch=4,
            grid=(ntile,),
            in_specs=[pl.BlockSpec(memory_space=pl.ANY),
                      pl.BlockSpec((D, 128), lambda j, *_: (0, 0)),
                      wspec(up, 0), wspec(up, 0), wspec(down, 0), wspec(up, 1), wspec(up, 1), wspec(down, 1)],
            out_specs=pl.BlockSpec((TMM * TOK, 128), lambda j, *_: (j, 0)),
            scratch_shapes=[pltpu.VMEM((2, TMM * TOK, 128), F32), pltpu.SemaphoreType.DMA((2,))],
        ),
        out_shape=jax.ShapeDtypeStruct((npad * TOK, 128), F32),
        compiler_params=_cparams("arbitrary"),
        name="moe_experts",
    )(tile_ea, tile_eb, tile_valid, inv, hrow, rw, w1, w3, w2, w1, w3, w2)


def _moe_block(x_mid, hrow, bucket, rank, counts, mod, rw, rw_hi, w1, w3, w2, layer, t0, nct):
    n = hrow.shape[0] // TOK
    ntile = n // TMM + N_BUCKETS
    npad = ntile * TMM
    cnt = counts[:N_BUCKETS, 0]
    padded = ((cnt + TMM - 1) // TMM) * TMM
    ends = jnp.cumsum(padded)
    starts = ends - padded
    total_tiles = ends[-1] // TMM
    tiles = jnp.arange(ntile, dtype=I32)
    tile_valid = (tiles < total_tiles).astype(I32)
    tile_blk = jnp.minimum(tiles, jnp.maximum(total_tiles - 1, 0))
    tile_bucket = jnp.minimum(jnp.sum((ends[None, :] <= (tile_blk * TMM)[:, None]).astype(I32), axis=1), N_BUCKETS - 1)
    pair = tile_bucket % 6
    grp = tile_bucket // 6
    slot_a = sum(jnp.where(pair == k, e, 0) for k, (e, _) in enumerate(_PAIR_SLOTS))
    slot_b = sum(jnp.where(pair == k, e, 0) for k, (_, e) in enumerate(_PAIR_SLOTS))
    tile_ea = (4 * grp + slot_a).astype(I32)
    tile_eb = (4 * grp + slot_b).astype(I32)
    bucket = bucket.reshape(-1)
    onehot = (bucket[:, None] == jnp.arange(N_BUCKETS, dtype=I32)[None, :]).astype(I32)
    pos = (rank.reshape(-1) + jnp.sum(onehot * starts[None, :], axis=1)).astype(I32)
    inv = _inverse_positions(pos, n, npad)
    ys = _moe(hrow, inv, rw, w1, w3, w2, layer, tile_ea, tile_eb, tile_valid, npad)
    return _combine(x_mid, mod, ys, pos, t0, nct)


def kernel(x, c, ctx, c_ctx, router_w, router_b, norm_mix, norm_ffn, w_mod, b_mod, ev_w_in, ev_conv_w, ev_conv_b, ev_dt_bias, ev_a_log, ev_d_skip, ev_ssd_norm, ev_q_norm, ev_k_norm, ev_sink, ev_w_out, od_w_in, od_conv_w, od_conv_b, od_igate_b, od_fgate_b, od_head_norm, od_w_out, moe_w1, moe_w3, moe_w2):
    s_len = x.shape[1]
    c_len = ctx.shape[1]
    assert x.shape[0] == 1 and s_len % TM == 0 and c_len % TM == 0 and s_len % GRID_W == 0
    nct = c_len // TM
    ncc, nlc = c_len // T, s_len // T
    nt = c_len + s_len
    ntiles = nt // TM

    mod = _modulation(c, c_ctx, w_mod, b_mod)
    rw, rw_hi = _hi_lo_weight(router_w)
    rb_col = router_b.reshape(N_EXPERTS, 1)
    pad128 = lambda v: jnp.zeros((1, 128), F32).at[0, :v.shape[0]].set(v)

    w = ev_w_in[0]
    rope_rows, rope_cols = _rope_tables(s_len)
    wdt, wdt_hi = _hi_lo_weight(w[:, 1280:1296])
    p0 = dict(
        g_mix=norm_mix[0].reshape(1, D),
        w_zx=w[:, 0:1280].astype(BF16), w_qkv=w[:, 1296:2064].astype(BF16), wdt=wdt, wdt_hi=wdt_hi,
        conv_w=ev_conv_w[0], conv_b=ev_conv_b[0].reshape(1, 768),
        dt_bias=pad128(ev_dt_bias[0].reshape(16)),
        q_norm=jnp.tile(ev_q_norm[0], 8).reshape(1, 512), k_norm=jnp.tile(ev_k_norm[0], 2).reshape(1, 128),
        rope_rows=rope_rows, rope_cols=rope_cols,
        alog_row=ev_a_log[0].reshape(1, 16), alog_col=ev_a_log[0].reshape(16, 1),
        d_skip=jnp.repeat(ev_d_skip[0], 64).reshape(1, 512), ssd_norm=ev_ssd_norm[0].reshape(1, 512))
    z, xs, bc, q, kk, vv, dtc, dtr = _in0(ctx[0], x[0], mod[0], p0, nct)
    yf = _ssd(False, xs, bc, dtc, dtr, p0, ncc, nlc)
    ymix = _ssd(True, xs, bc, dtc, dtr, p0, ncc, nlc, yf=yf, z=z)
    att = _attention(q, kk, vv, ev_sink[0].reshape(1, 8), ncc, nlc)
    x_mid0, hrow, bucket, rank, counts = _out_proj(
        [ymix, att], ev_w_out[0].astype(BF16), [ctx[0], x[0]], mod[0], norm_ffn[0].reshape(1, D),
        rw, rw_hi, rb_col, 0, ntiles, nct)
    x1 = _moe_block(x_mid0, hrow, bucket, rank, counts, mod[0], rw, rw_hi, moe_w1, moe_w3, moe_w2, 0, 0, nct)

    w = od_w_in[0]
    wg, wg_hi = _hi_lo_weight(w[:, 3072:3104])
    p1 = dict(
        g_mix=norm_mix[1].reshape(1, D),
        wcat=w[:, 0:3072].astype(BF16), wg=wg, wg_hi=wg_hi,
        conv_w=od_conv_w[0], conv_b=od_conv_b[0].reshape(1, 2048),
        gate_bias=pad128(jnp.concatenate([od_igate_b[0].reshape(16), od_fgate_b[0].reshape(16)])),
        head_norm=od_head_norm[0].reshape(1, 1024))
    q1, kt1, v1, og1, gc1, gr1 = _in1(x1, mod[1], p1, nct)
    hf = _mlstm(False, q1, kt1, v1, gc1, gr1, p1, ncc, nlc)
    hmix = _mlstm(True, q1, kt1, v1, gc1, gr1, p1, ncc, nlc, hf=hf, og=og1)
    x_mid1, hrow, bucket, rank, counts = _out_proj(
        [hmix], od_w_out[0].astype(BF16), [x1], mod[1], norm_ffn[1].reshape(1, D),
        rw, rw_hi, rb_col, nct, ntiles - nct, nct)
    return _moe_block(x_mid1, hrow, bucket, rank, counts, mod[1], rw, rw_hi, moe_w1, moe_w3, moe_w2, 1, nct,
                      nct)[None]
```

```python
import functools
import math

import jax
import jax.numpy as jnp
from jax import lax
from jax.experimental import pallas as pl
from jax.experimental.pallas import tpu as pltpu

F32 = jnp.float32
BF16 = jnp.bfloat16
I32 = jnp.int32

EPS = 1e-6
D = 1024
T = 128
SCAN_CHUNKS = 2
TM = 256
TMM = 256
GRID_W = 64
ROPE_THETA = 10000.0
N_EXPERTS = 16
N_BUCKETS = 24
NB_PAD = 32
D_EXPERT = 512
TOK = 8
U32 = jnp.uint32
NEG_INF = float("-inf")
VMEM_LIMIT = 56 * 1024 * 1024

_NN = (((1,), (0,)), ((), ()))
_NT = (((1,), (1,)), ((), ()))
_TN = (((0,), (0,)), ((), ()))

_PAIR_SLOTS = ((0, 1), (2, 1), (2, 0), (3, 0), (3, 1), (3, 2))


def _cparams(*sem):
    return pltpu.CompilerParams(dimension_semantics=sem, vmem_limit_bytes=VMEM_LIMIT)


def _dot(a, b, dims=_NN):
    return lax.dot_general(a, b, dims, preferred_element_type=F32)


def _split(a, n):
    out = []
    r = a
    for _ in range(n):
        t = r.astype(BF16)
        out.append(t)
        r = r - t.astype(F32)
    return out


def _mdot(as_, bs, dims=_NN, order=None):
    if order is None:
        order = len(as_) + len(bs) - 2
    acc = None
    for i, a in enumerate(as_):
        for j, b in enumerate(bs):
            if i + j <= order:
                p = _dot(a, b, dims)
                acc = p if acc is None else acc + p
    return acc


def _sigmoid(x):
    return 1.0 / (1.0 + jnp.exp(-x))


def _silu(x):
    return x * _sigmoid(x)


def _log1p_exp_neg_abs(x):
    e = jnp.exp(-jnp.abs(x))
    u = 1.0 + e
    um1 = u - 1.0
    return jnp.where(um1 == 0.0, e, jnp.log(u) * (e / jnp.where(um1 == 0.0, 1.0, um1)))


def _softplus(x):
    return jnp.maximum(x, 0.0) + _log1p_exp_neg_abs(x)


def _log_sigmoid(x):
    return jnp.minimum(x, 0.0) - _log1p_exp_neg_abs(x)


def _norm_mod(x, g, sc, sh):
    ms = jnp.mean(x * x, axis=-1, keepdims=True)
    return (x * lax.rsqrt(ms + EPS)) * g * (1.0 + sc) + sh


def _tok_load(ref, chunk, n):
    return ref[pl.ds(chunk, n, stride=TOK), :]


def _tok_store(ref, chunk, val):
    ref[pl.ds(chunk, val.shape[0], stride=TOK), :] = val


def _tok_rows(ref, n):
    return jnp.concatenate([_tok_load(ref, c, n) for c in range(TOK)], axis=1)


def _tri(rev):
    r = lax.broadcasted_iota(I32, (T, T), 0)
    c = lax.broadcasted_iota(I32, (T, T), 1)
    return (c >= r) if rev else (c <= r)


def _cumsums(rev, col, row):
    tri = _tri(rev)
    tri_b = tri.astype(F32).astype(BF16)
    trit_b = _tri(not rev).astype(F32).astype(BF16)
    ccol = _mdot([tri_b], _split(col, 3))
    crow = _mdot(_split(row, 3), [trit_b])
    return tri, ccol, crow


def _lane_bcast(col):
    n = col.shape[1]
    col = jnp.concatenate([col, jnp.zeros((col.shape[0], 128 - n), F32)], axis=1)
    r = lax.broadcasted_iota(I32, (128, n * 128), 0)
    c = lax.broadcasted_iota(I32, (128, n * 128), 1) // 128
    return _mdot(_split(col, 3), [(r == c).astype(F32).astype(BF16)])


def _mod_kernel(c_ref, w_ref, b_ref, o_ref):
    a = _silu(c_ref[...])
    o_ref[0] = _mdot(_split(a, 2), _split(w_ref[0], 2), order=1) + b_ref[0]


def _modulation(c, c_ctx, w_mod, b_mod):
    depth = w_mod.shape[0]
    n = w_mod.shape[2]
    tn = 1536
    cc = jnp.zeros((8, D), F32).at[0].set(c[0]).at[1].set(c_ctx)
    out = pl.pallas_call(
        _mod_kernel,
        grid=(depth, n // tn),
        in_specs=[
            pl.BlockSpec((8, D), lambda l, j: (0, 0)),
            pl.BlockSpec((1, D, tn), lambda l, j: (l, 0, j)),
            pl.BlockSpec((1, 1, tn), lambda l, j: (l, 0, j)),
        ],
        out_specs=pl.BlockSpec((1, 8, tn), lambda l, j: (l, 0, j)),
        out_shape=jax.ShapeDtypeStruct((depth, 8, n), F32),
        compiler_params=_cparams("arbitrary", "arbitrary"),
        name="modulation",
    )(cc, w_mod, b_mod.reshape(depth, 1, n))
    return out[:, :2].reshape(depth, 2, 6, D)


def _halo_specs(nrows, tile_of):
    nb8 = nrows // 8
    return [
        pl.BlockSpec((8, D), lambda i: (jnp.maximum(tile_of(i) * (TM // 8) - 1, 0), 0)),
        pl.BlockSpec((TM, D), lambda i: (tile_of(i), 0)),
        pl.BlockSpec((8, D), lambda i: (jnp.minimum((tile_of(i) + 1) * (TM // 8), nb8 - 1), 0)),
    ]


def _ctx_tile(nct):
    return lambda i: jnp.minimum(i, nct - 1)


def _lat_tile(nct):
    return lambda i: jnp.maximum(i - nct, 0)


def _mod_spec(nct, t0=0):
    return pl.BlockSpec((1, 6, D), lambda i: (jnp.where(i + t0 < nct, 1, 0), 0, 0))


def _full(shape):
    nd = len(shape)
    return pl.BlockSpec(shape, lambda i: (0,) * nd)


def _seq_edges(i, nct, ntiles):
    prev_ok = jnp.logical_and(i != 0, i != nct).astype(F32)
    next_ok = jnp.logical_and(i != nct - 1, i != ntiles - 1).astype(F32)
    return prev_ok, next_ok


def _conv_silu(x, x_first_prev, x_last_next, cw, cb):
    n = x.shape[0]
    rows = lax.broadcasted_iota(I32, x.shape, 0)
    x_prev = jnp.where(rows == 0, x_first_prev, pltpu.roll(x, 1, 0))
    x_next = jnp.where(rows == n - 1, x_last_next, pltpu.roll(x, n - 1, 0))
    return _silu(x_prev * cw[0:1] + x * cw[1:2] + x_next * cw[2:3] + cb)


def _hi_lo_cols(blk, lo_pass, n):
    return blk + pltpu.roll(blk, 128 - n, 1) + lo_pass


def _hi_lo_weight(w):
    n = w.shape[1]
    hi = w.astype(BF16)
    lo = (w - hi.astype(F32)).astype(BF16)
    z = jnp.zeros((w.shape[0], 128 - 2 * n), BF16)
    return jnp.concatenate([hi, lo, z], axis=1), jnp.concatenate([hi, jnp.zeros_like(lo), z], axis=1)


def _head_rms(xf, gamma):
    r = lax.broadcasted_iota(I32, (128, 128), 0) // 64
    c = lax.broadcasted_iota(I32, (128, 128), 1) // 64
    ones_bd = (r == c).astype(F32).astype(BF16)
    outs = []
    for j in range(xf.shape[1] // 128):
        blk = xf[:, 128 * j:128 * (j + 1)]
        ssum = _dot((blk * blk).astype(BF16), ones_bd)
        outs.append(blk * lax.rsqrt(ssum * (1.0 / 64.0) + EPS))
    return jnp.concatenate(outs, axis=1) * gamma


def _rope(xf, cos, sin):
    lane = lax.broadcasted_iota(I32, (xf.shape[0], 128), 1)
    first = (lane % 32) < 16
    outs = []
    for j in range(xf.shape[1] // 128):
        blk = xf[:, 128 * j:128 * (j + 1)]
        partner = jnp.where(first, pltpu.roll(blk, 112, 1), pltpu.roll(blk, 16, 1))
        outs.append(blk * cos + partner * sin)
    return jnp.concatenate(outs, axis=1)


def _rope_tables(s_len):
    lane = jnp.arange(128, dtype=I32)
    inv = ROPE_THETA ** (-(lane % 16).astype(F32) / 16.0)
    sign = jnp.where((lane % 32) < 16, -1.0, 1.0).astype(F32)
    ang_r = jnp.arange(s_len // GRID_W, dtype=F32)[:, None] * inv[None, :]
    ang_c = jnp.tile(jnp.arange(GRID_W, dtype=F32), TM // GRID_W)[:, None] * inv[None, :]
    both = lambda ang: jnp.stack([jnp.cos(ang), jnp.sin(ang) * sign[None, :]])
    return both(ang_r), both(ang_c)


def _in0_kernel(nct, ntiles, cp_ref, c_ref, cn_ref, xp_ref, x_ref, xn_ref, mod_ref, g_ref, wa_ref, wb_ref,
                wdt_ref, wdth_ref, cw_ref, cb_ref, dtb_ref, qn_ref, kn_ref, rowcs_ref, colcs_ref,
                z_ref, xs_ref, bc_ref, q_ref, kk_ref, vv_ref, dtc_ref, dtr_ref):
    i = pl.program_id(0)
    is_ctx = i < nct
    sh = mod_ref[0, 0:1, :]
    sc = mod_ref[0, 1:2, :]
    g = g_ref[...]
    x_all = jnp.concatenate([jnp.where(is_ctx, cp_ref[...], xp_ref[...]),
                             jnp.where(is_ctx, c_ref[...], x_ref[...]),
                             jnp.where(is_ctx, cn_ref[...], xn_ref[...])], axis=0)
    h_all = _norm_mod(x_all, g, sc, sh)
    hb_all = h_all.astype(BF16)
    zx_all = _dot(hb_all, wa_ref[...])
    h, hb, zx = h_all[8:8 + TM], hb_all[8:8 + TM], zx_all[8:8 + TM]
    qkv = _dot(hb, wb_ref[...])
    prev_ok, next_ok = _seq_edges(i, nct, ntiles)
    xb_prev = zx_all[7:8, 512:1280] * prev_ok
    xb_next = zx_all[8 + TM:9 + TM, 512:1280] * next_ok
    act = _conv_silu(zx[:, 512:1280], xb_prev, xb_next, cw_ref[...], cb_ref[...])
    z_ref[...] = zx[:, 0:512].astype(BF16)
    xs_ref[...] = act[:, 0:512].astype(BF16)
    bc_ref[...] = act[:, 512:768].astype(BF16)
    row0 = jnp.maximum(i - nct, 0) * (TM // GRID_W)
    rowcs = [jnp.concatenate([jnp.broadcast_to(rowcs_ref[t, pl.ds(row0 + kq, 1), :], (GRID_W, 128))
                              for kq in range(TM // GRID_W)], axis=0) for t in range(2)]
    row_lanes = (lax.broadcasted_iota(I32, (TM, 128), 1) % 64) < 32
    cos = jnp.where(is_ctx, 1.0, jnp.where(row_lanes, rowcs[0], colcs_ref[0]))
    sin = jnp.where(is_ctx, 0.0, jnp.where(row_lanes, rowcs[1], colcs_ref[1]))
    q = _rope(_head_rms(qkv[:, 0:512], qn_ref[...]), cos, sin) * 0.125
    q_ref[...] = q.astype(BF16)
    k = _rope(_head_rms(qkv[:, 512:640], kn_ref[...]), cos, sin)
    kk_ref[...] = jnp.concatenate([k, pltpu.roll(k, 64, 1)], axis=1).astype(BF16)
    v = qkv[:, 640:768]
    vv_ref[...] = jnp.concatenate([v, pltpu.roll(v, 64, 1)], axis=1).astype(BF16)
    h_lo = (h - hb.astype(F32)).astype(BF16)
    dt = _softplus(_hi_lo_cols(_dot(hb, wdt_ref[...]), _dot(h_lo, wdth_ref[...]), 16) + dtb_ref[...])
    dtc_ref[...] = dt[:, 0:16]
    dtr_ref[...] = jnp.transpose(dt)[0:16, :]


def _in0(ctx2, x2, mod, p, nct):
    c_len, s_len = ctx2.shape[0], x2.shape[0]
    nt = c_len + s_len
    ntiles = nt // TM
    tile = lambda w: pl.BlockSpec((TM, w), lambda i: (i, 0))
    lat = _lat_tile(nct)
    lat_tile = lambda w: pl.BlockSpec((TM, w), lambda i: (lat(i), 0))
    outs = [(512, BF16), (512, BF16), (256, BF16), (512, BF16), (256, BF16), (256, BF16), (16, F32)]
    return pl.pallas_call(
        functools.partial(_in0_kernel, nct, ntiles),
        grid=(ntiles,),
        in_specs=_halo_specs(c_len, _ctx_tile(nct)) + _halo_specs(s_len, lat) + [
            _mod_spec(nct), _full((1, D)), _full((D, 1280)), _full((D, 768)), _full((D, 128)), _full((D, 128)),
            _full((3, 768)), _full((1, 768)), _full((1, 128)),
            _full((1, 512)), _full((1, 128)), _full((2, s_len // GRID_W, 128)), _full((2, TM, 128))],
        out_specs=[tile(w) for w, _ in outs] + [pl.BlockSpec((16, TM), lambda i: (0, i))],
        out_shape=[jax.ShapeDtypeStruct((nt, w), dt) for w, dt in outs]
        + [jax.ShapeDtypeStruct((16, nt), F32)],
        compiler_params=_cparams("arbitrary"),
        name="in_proj_even",
    )(ctx2, ctx2, ctx2, x2, x2, x2, mod, p["g_mix"], p["w_zx"], p["w_qkv"], p["wdt"], p["wdt_hi"],
      p["conv_w"], p["conv_b"], p["dt_bias"], p["q_norm"], p["k_norm"], p["rope_rows"], p["rope_cols"])


def _scan_chunk_map(rev, ncc, nlc):
    if not rev:
        return lambda j: j
    return lambda j: jnp.where(j < ncc, ncc - 1 - j, ncc + nlc - 1 - (j - ncc))


def _ssd_kernel(rev, *refs):
    if rev:
        (xs_ref, bc_ref, dtc_ref, dtr_ref, alr_ref, alc_ref, yf_ref, z_ref, dsk_ref, nrm_ref,
         o_ref, st_ref) = refs
    else:
        xs_ref, bc_ref, dtc_ref, dtr_ref, alr_ref, alc_ref, o_ref, st_ref = refs
    j = pl.program_id(0)

    @pl.when(j == 0)
    def _():
        st_ref[...] = jnp.zeros_like(st_ref)

    d = 8 if rev else 0
    a_coef_row = -jnp.exp(alr_ref[...])[:, d:d + 8]
    a_coef_col = -jnp.exp(alc_ref[...])[d:d + 8, :]
    lane = lax.broadcasted_iota(I32, (T, 128), 1)
    lo = lane < 64
    zero_b = jnp.zeros((T, 128), BF16)
    hi_half = jnp.logical_not(lo)
    sub = lax.broadcasted_iota(I32, (128, T), 0)
    eye = (sub == lax.broadcasted_iota(I32, (128, T), 1)).astype(F32).astype(BF16)
    end = 0 if rev else T - 1
    order = tuple(reversed(range(SCAN_CHUNKS))) if rev else tuple(range(SCAN_CHUNKS))

    def prologue(c):
        rows = slice(c * T, (c + 1) * T)
        dtc = dtc_ref[rows, d:d + 8]
        dtr = dtr_ref[d:d + 8, rows]
        tri, acs_col, acs_row = _cumsums(rev, dtc * a_coef_row, dtr * a_coef_col)
        atot_col = acs_row[:, end:end + 1]
        acs_bc = _lane_bcast(acs_col)
        dec_row = jnp.exp(atot_col - acs_row) * dtr
        xs = xs_ref[rows, :]
        bm = bc_ref[rows, 0:128]
        cm = bc_ref[rows, 128:256]
        cgs = [jnp.where(lo, cm, zero_b), jnp.where(hi_half, cm, zero_b)]
        cbs = [_dot(cgs[g], bm, _NT) for g in range(2)]
        bmt = _dot(eye, bm, _NT)
        bgts = [jnp.where(sub < 64, bmt, 0.0), jnp.where(sub >= 64, bmt, 0.0)]
        xpairs = []
        for pr in range(4):
            xp = xs[:, 128 * pr:128 * (pr + 1)]
            xpairs.append(jnp.concatenate([jnp.where(lo, xp, zero_b), jnp.where(hi_half, xp, zero_b)], axis=0))
        return dict(tri=tri, acs_row=acs_row, acs_bc=acs_bc, atot_col=atot_col, dec_row=dec_row, dtr=dtr,
                    xs=xs, cgs=cgs, cbs=cbs, bgts=bgts, xpairs=xpairs)

    pro = {c: prologue(c) for c in order}
    y_off = {}
    for c in order:
        p = pro[c]
        for pr in range(4):
            g, h0, h1 = pr // 2, 2 * pr, 2 * pr + 1
            st = st_ref[pr]
            eacs = jnp.exp(jnp.where(lo, p["acs_bc"][:, 128 * h0:128 * (h0 + 1)],
                                     p["acs_bc"][:, 128 * h1:128 * (h1 + 1)]))
            y_off[c, pr] = _dot(p["cgs"][g], st.astype(BF16)) * eacs
            bdec = jnp.concatenate([(p["bgts"][g] * p["dec_row"][h0:h0 + 1, :]).astype(BF16),
                                    (p["bgts"][g] * p["dec_row"][h1:h1 + 1, :]).astype(BF16)], axis=1)
            carry = jnp.where(lo[0:1, :], jnp.exp(p["atot_col"][h0:h0 + 1, :]), jnp.exp(p["atot_col"][h1:h1 + 1, :]))
            st_ref[pr] = carry * st + _dot(bdec, p["xpairs"][pr])
    for c in order:
        p = pro[c]
        rows = slice(c * T, (c + 1) * T)
        ys = []
        for pr in range(4):
            ms = []
            for hd in (2 * pr, 2 * pr + 1):
                diff = p["acs_bc"][:, 128 * hd:128 * (hd + 1)] - p["acs_row"][hd:hd + 1, :]
                lmat = jnp.exp(jnp.where(p["tri"], diff, NEG_INF))
                ms.append((p["cbs"][hd // 4] * lmat * p["dtr"][hd:hd + 1, :]).astype(BF16))
            ys.append(_dot(jnp.concatenate(ms, axis=1), p["xpairs"][pr]) + y_off[c, pr])
        y = jnp.concatenate(ys, axis=1)
        if not rev:
            o_ref[rows, :] = y
        else:
            ytot = y + yf_ref[rows, :] + dsk_ref[...] * p["xs"].astype(F32)
            gated = ytot * _silu(z_ref[rows, :].astype(F32))
            ms = jnp.mean(gated * gated, axis=-1, keepdims=True)
            o_ref[rows, :] = (gated * lax.rsqrt(ms + EPS) * nrm_ref[...]).astype(BF16)


def _ssd(rev, xs, bc, dtc, dtr, p, ncc, nlc, yf=None, z=None):
    nt = xs.shape[0]
    assert ncc % SCAN_CHUNKS == 0 and nlc % SCAN_CHUNKS == 0
    ncc, nlc = ncc // SCAN_CHUNKS, nlc // SCAN_CHUNKS
    rows = SCAN_CHUNKS * T
    cmap = _scan_chunk_map(rev, ncc, nlc)
    blk = lambda w: pl.BlockSpec((rows, w), lambda j: (cmap(j), 0))
    in_specs = [blk(512), blk(256), blk(16), pl.BlockSpec((16, rows), lambda j: (0, cmap(j))),
                _full((1, 16)), _full((16, 1))]
    args = [xs, bc, dtc, dtr, p["alog_row"], p["alog_col"]]
    if rev:
        in_specs += [blk(512), blk(512), _full((1, 512)), _full((1, 512))]
        args += [yf, z, p["d_skip"], p["ssd_norm"]]
    return pl.pallas_call(
        functools.partial(_ssd_kernel, rev),
        grid=(ncc + nlc,),
        in_specs=in_specs,
        out_specs=blk(512),
        out_shape=jax.ShapeDtypeStruct((nt, 512), BF16 if rev else F32),
        scratch_shapes=[pltpu.VMEM((4, 128, 128), F32)],
        compiler_params=_cparams("arbitrary"),
        name="ssd_bwd" if rev else "ssd_fwd",
    )(*args)


def _attn_kernel(ncc, nblk, q_ref, kp_ref, kc_ref, kn_ref, vp_ref, vc_ref, vn_ref, kx_ref, vx_ref,
                 sink_ref, o_ref):
    j = pl.program_id(0)
    c_len = kx_ref.shape[0]
    r = lax.broadcasted_iota(I32, (T, T), 0)
    c = lax.broadcasted_iota(I32, (T, T), 1)
    zero = jnp.zeros((T, T), F32)
    ninf = jnp.full((T, T), NEG_INF, F32)
    lo = lax.broadcasted_iota(I32, (T, 128), 1) < 64
    zero_b = jnp.zeros((T, 128), BF16)
    sink = sink_ref[...]
    kblk = [kp_ref[...], kc_ref[0:T, :], kc_ref[T:2 * T, :], kn_ref[...]]
    vblk = [vp_ref[...], vc_ref[0:T, :], vc_ref[T:2 * T, :], vn_ref[...]]
    stacks = [[hd for hd in range(8) if (hd // 4 + hd % 2) % 2 == b] for b in range(2)]
    s_all, v_all = {}, {}
    for qb in range(2):
        jb = 2 * j + qb
        is_lat = jb >= ncc
        prev_ok = jnp.logical_and(is_lat, jb >= ncc + 1)
        next_ok = jnp.logical_and(is_lat, jb <= nblk - 2)
        bias = jnp.concatenate([
            jnp.where(jnp.logical_and(prev_ok, c >= r), zero, ninf),
            jnp.where(is_lat, zero, ninf),
            jnp.where(jnp.logical_and(next_ok, c <= r), zero, ninf),
            jnp.zeros((T, c_len), F32)], axis=1)
        bias4 = jnp.concatenate([bias] * 4, axis=0)
        k_all = jnp.concatenate(kblk[qb:qb + 3] + [kx_ref[...]], axis=0)
        v_all[qb] = jnp.concatenate(vblk[qb:qb + 3] + [vx_ref[...]], axis=0)
        q = q_ref[qb * T:(qb + 1) * T, :]
        for b in range(2):
            qs = []
            for hd in stacks[b]:
                qp = q[:, 128 * (hd // 2):128 * (hd // 2 + 1)]
                qs.append(jnp.where(lo, zero_b, qp) if hd % 2 else jnp.where(lo, qp, zero_b))
            s_all[qb, b] = _dot(jnp.concatenate(qs, axis=0), k_all[:, 128 * b:128 * (b + 1)], _NT) + bias4
    for qb in range(2):
        outs = {}
        for b in range(2):
            s = s_all[qb, b]
            sk = jnp.concatenate([jnp.broadcast_to(sink[:, hd:hd + 1], (T, 1)) for hd in stacks[b]], axis=0)
            m = jnp.maximum(jnp.max(s, axis=-1, keepdims=True), sk)
            pr = jnp.exp(s - m)
            den = jnp.sum(pr, axis=-1, keepdims=True) + jnp.exp(sk - m)
            o = _dot(pr.astype(BF16), v_all[qb][:, 128 * b:128 * (b + 1)]) / den
            for n, hd in enumerate(stacks[b]):
                outs[hd] = o[T * n:T * (n + 1)]
        for pair in range(4):
            o_ref[qb * T:(qb + 1) * T, 128 * pair:128 * (pair + 1)] = jnp.where(
                lo, outs[2 * pair], outs[2 * pair + 1]).astype(BF16)


def _attention(q, kk, vv, sink, ncc, nlc):
    nt = q.shape[0]
    nblk = ncc + nlc
    assert nblk % 2 == 0 and ncc % 2 == 0
    c_len = ncc * T
    prev = lambda w: pl.BlockSpec((T, w), lambda j: (jnp.maximum(2 * j - 1, 0), 0))
    cur = lambda w: pl.BlockSpec((2 * T, w), lambda j: (j, 0))
    nxt = lambda w: pl.BlockSpec((T, w), lambda j: (jnp.minimum(2 * j + 2, nblk - 1), 0))
    ctx = lambda w: pl.BlockSpec((c_len, w), lambda j: (0, 0))
    return pl.pallas_call(
        functools.partial(_attn_kernel, ncc, nblk),
        grid=(nblk // 2,),
        in_specs=[cur(512), prev(256), cur(256), nxt(256), prev(256), cur(256), nxt(256),
                  ctx(256), ctx(256), _full((1, 8))],
        out_specs=cur(512),
        out_shape=jax.ShapeDtypeStruct((nt, 512), BF16),
        compiler_params=_cparams("arbitrary"),
        name="window_attention",
    )(q, kk, kk, kk, vv, vv, vv, kk, vv, sink)


def _in1_kernel(nct, ntiles, xp_ref, x_ref, xn_ref, mod_ref, g_ref,
                wcat_ref, wg_ref, wgh_ref, cw_ref, cb_ref, gb_ref,
                q_ref, kt_ref, v_ref, o_ref, gc_ref, gr_ref):
    i = pl.program_id(0)
    sh = mod_ref[0, 0:1, :]
    sc = mod_ref[0, 1:2, :]
    g = g_ref[...]
    x_all = jnp.concatenate([xp_ref[...], x_ref[...], xn_ref[...]], axis=0)
    h_all = _norm_mod(x_all, g, sc, sh)
    hb_all = h_all.astype(BF16)
    main_all = _dot(hb_all, wcat_ref[...])
    h, hb, main = h_all[8:8 + TM], hb_all[8:8 + TM], main_all[8:8 + TM]
    prev_ok, next_ok = _seq_edges(i, nct, ntiles)
    x_prev = main_all[7:8, 0:2048] * prev_ok
    x_next = main_all[8 + TM:9 + TM, 0:2048] * next_ok
    act = _conv_silu(main[:, 0:2048], x_prev, x_next, cw_ref[...], cb_ref[...])
    q_ref[...] = act[:, 0:512].astype(BF16)
    kt_ref[...] = jnp.transpose(act[:, 512:1024] * 0.125).astype(BF16)
    v_ref[...] = act[:, 1024:2048].astype(BF16)
    o_ref[...] = main[:, 2048:3072].astype(BF16)
    h_lo = (h - hb.astype(F32)).astype(BF16)
    gates = _hi_lo_cols(_dot(hb, wg_ref[...]), _dot(h_lo, wgh_ref[...]), 32) + gb_ref[...]
    lane = lax.broadcasted_iota(I32, gates.shape, 1)
    gates = jnp.where(lane < 16, gates, _log_sigmoid(gates))
    gc_ref[...] = gates[:, 0:32]
    gr_ref[...] = jnp.transpose(gates)[0:32, :]


def _in1(x1, mod, p, nct):
    nt = x1.shape[0]
    ntiles = nt // TM
    tile = lambda w: pl.BlockSpec((TM, w), lambda i: (i, 0))
    return pl.pallas_call(
        functools.partial(_in1_kernel, nct, ntiles),
        grid=(ntiles,),
        in_specs=_halo_specs(nt, lambda i: i) + [
            _mod_spec(nct), _full((1, D)), _full((D, 3072)), _full((D, 128)), _full((D, 128)),
            _full((3, 2048)), _full((1, 2048)), _full((1, 128))],
        out_specs=[tile(512), pl.BlockSpec((512, TM), lambda i: (0, i)), tile(1024), tile(1024),
                   tile(32), pl.BlockSpec((32, TM), lambda i: (0, i))],
        out_shape=[jax.ShapeDtypeStruct((nt, 512), BF16), jax.ShapeDtypeStruct((512, nt), BF16),
                   jax.ShapeDtypeStruct((nt, 1024), BF16), jax.ShapeDtypeStruct((nt, 1024), BF16),
                   jax.ShapeDtypeStruct((nt, 32), F32), jax.ShapeDtypeStruct((32, nt), F32)],
        compiler_params=_cparams("arbitrary"),
        name="in_proj_odd",
    )(x1, x1, x1, mod, p["g_mix"], p["wcat"], p["wg"], p["wg_hi"], p["conv_w"], p["conv_b"], p["gate_bias"])


def _mlstm_kernel(rev, *refs):
    if rev:
        (q_ref, kt_ref, v_ref, gc_ref, gr_ref, hf_ref, og_ref, hn_ref, o_ref,
         c_ref, mc_ref, mr_ref) = refs
    else:
        q_ref, kt_ref, v_ref, gc_ref, gr_ref, o_ref, c_ref, mc_ref, mr_ref = refs
    j = pl.program_id(0)

    @pl.when(j == 0)
    def _():
        c_ref[...] = jnp.zeros_like(c_ref)
        mc_ref[...] = jnp.zeros_like(mc_ref)
        mr_ref[...] = jnp.zeros_like(mr_ref)

    d = 8 if rev else 0
    end = 0 if rev else T - 1
    order = tuple(reversed(range(SCAN_CHUNKS))) if rev else tuple(range(SCAN_CHUNKS))
    ones_b = jnp.ones((T, 128), BF16)
    sub = lax.broadcasted_iota(I32, (128, T), 0)
    zero_k = jnp.zeros((128, T), BF16)

    def prologue(c):
        rows = slice(c * T, (c + 1) * T)
        ig_col = gc_ref[rows, d:d + 8]
        lf_col = gc_ref[rows, 16 + d:24 + d]
        ig_row = gr_ref[d:d + 8, rows]
        lf_row = gr_ref[16 + d:24 + d, rows]
        tri, b_col, b_row = _cumsums(rev, lf_col, lf_row)
        blast_row = b_col[end:end + 1, :]
        blast_col = b_row[:, end:end + 1]
        wend_row = blast_col - b_row + ig_row
        ac_col = jnp.max(wend_row, axis=1, keepdims=True)
        return dict(tri=tri, b_row=b_row, ig_row=ig_row, blast_row=blast_row, blast_col=blast_col,
                    ac_col=ac_col, eend_row=jnp.exp(wend_row - ac_col),
                    ac_row=jnp.max(blast_row - b_col + ig_col, axis=0, keepdims=True),
                    b_bc=_lane_bcast(b_col), q=q_ref[rows, :])

    pro = {c: prologue(c) for c in order}
    m_col = mc_ref[:, 0:1]
    m_row = mr_ref[0:1, 0:8]
    for c in order:
        p = pro[c]
        mnew_col = jnp.maximum(p["blast_col"] + m_col, p["ac_col"])
        p["sp_col"] = jnp.exp(p["blast_col"] + m_col - mnew_col)
        p["sc_col"] = jnp.exp(p["ac_col"] - mnew_col)
        p["m_row"] = m_row
        m_col, m_row = mnew_col, jnp.maximum(p["blast_row"] + m_row, p["ac_row"])

    def head_matmuls(c, hd):
        p = pro[c]
        rows = slice(c * T, (c + 1) * T)
        pair, hi = hd // 2, hd % 2
        qp = p["q"][:, 128 * pair:128 * (pair + 1)]
        ktp = kt_ref[128 * pair:128 * (pair + 1), rows]
        kth = jnp.where((sub >= 64) if hi else (sub < 64), ktp, zero_k)
        vaug = jnp.concatenate([v_ref[rows, 128 * hd:128 * (hd + 1)], ones_b], axis=1)
        cst = c_ref[hd]
        sqk = _dot(qp, kth)
        inter = _dot(qp, cst.astype(BF16))
        kte = (kth.astype(F32) * p["eend_row"][hd:hd + 1, :]).astype(BF16)
        c_ref[hd] = p["sp_col"][hd:hd + 1, :] * cst + p["sc_col"][hd:hd + 1, :] * _dot(kte, vaug)
        return sqk, inter, vaug

    items = [(c, hd) for c in order for hd in range(8)]
    nxt = head_matmuls(*items[0])
    for n, (c, hd) in enumerate(items):
        p = pro[c]
        rows = slice(c * T, (c + 1) * T)
        cols = slice(128 * hd, 128 * (hd + 1))
        sqk, inter, vaug = nxt
        if n + 1 < len(items):
            nxt = head_matmuls(*items[n + 1])
        bh = p["b_bc"][:, cols]
        dlog = jnp.where(p["tri"], bh - p["b_row"][hd:hd + 1, :] + p["ig_row"][hd:hd + 1, :], NEG_INF)
        gh = bh + p["m_row"][:, hd:hd + 1]
        mstar = jnp.maximum(gh, jnp.max(dlog, axis=-1, keepdims=True))
        w = (jnp.exp(dlog - mstar) * sqk).astype(BF16)
        intra = _dot(w, vaug)
        e_int = jnp.exp(gh - mstar)
        den = jnp.maximum(jnp.abs(intra[:, 128:256] + e_int * inter[:, 128:256]), jnp.exp(-mstar))
        hh = (intra[:, 0:128] + e_int * inter[:, 0:128]) / den
        if rev:
            hh = hh + hf_ref[rows, cols]
            ms = jnp.mean(hh * hh, axis=-1, keepdims=True)
            hh = hh * lax.rsqrt(ms + EPS) * hn_ref[:, cols]
            og = og_ref[rows, cols].astype(F32)
            o_ref[rows, cols] = (hh * _sigmoid(og)).astype(BF16)
        else:
            o_ref[rows, cols] = hh
    mc_ref[...] = jnp.broadcast_to(m_col, mc_ref.shape)
    mr_ref[...] = jnp.broadcast_to(jnp.concatenate([m_row, jnp.zeros((1, 120), F32)], axis=1), mr_ref.shape)


def _mlstm(rev, q, kt, v, gc, gr, p, ncc, nlc, hf=None, og=None):
    nt = q.shape[0]
    assert ncc % SCAN_CHUNKS == 0 and nlc % SCAN_CHUNKS == 0
    ncc, nlc = ncc // SCAN_CHUNKS, nlc // SCAN_CHUNKS
    rows = SCAN_CHUNKS * T
    cmap = _scan_chunk_map(rev, ncc, nlc)
    blk = lambda w: pl.BlockSpec((rows, w), lambda j: (cmap(j), 0))
    blk_t = lambda h: pl.BlockSpec((h, rows), lambda j: (0, cmap(j)))
    in_specs = [blk(512), blk_t(512), blk(1024), blk(32), blk_t(32)]
    args = [q, kt, v, gc, gr]
    if rev:
        in_specs += [blk(1024), blk(1024), _full((1, 1024))]
        args += [hf, og, p["head_norm"]]
    return pl.pallas_call(
        functools.partial(_mlstm_kernel, rev),
        grid=(ncc + nlc,),
        in_specs=in_specs,
        out_specs=blk(1024),
        out_shape=jax.ShapeDtypeStruct((nt, 1024), BF16 if rev else F32),
        scratch_shapes=[pltpu.VMEM((8, 128, 256), F32), pltpu.VMEM((8, 128), F32), pltpu.VMEM((8, 128), F32)],
        compiler_params=_cparams("arbitrary"),
        name="mlstm_bwd" if rev else "mlstm_fwd",
    )(*args)


def _route(logits_t, rb_col):
    scores = _sigmoid(logits_t)
    biased = scores + rb_col
    row = lambda a, e: a[e:e + 1, :]
    gscore = []
    for g in range(4):
        b0, b1, b2, b3 = (row(biased, 4 * g + e) for e in range(4))
        h1, l1 = jnp.maximum(b0, b1), jnp.minimum(b0, b1)
        h2, l2 = jnp.maximum(b2, b3), jnp.minimum(b2, b3)
        gscore.append(jnp.maximum(h1, h2) + jnp.maximum(jnp.minimum(h1, h2), jnp.maximum(l1, l2)))
    gidx = jnp.zeros_like(gscore[0], dtype=I32)
    best = gscore[0]
    for g in range(1, 4):
        better = gscore[g] > best
        gidx = jnp.where(better, g, gidx)
        best = jnp.where(better, gscore[g], best)

    def pick(a, e):
        out = row(a, e)
        for g in range(1, 4):
            out = jnp.where(gidx == g, row(a, 4 * g + e), out)
        return out

    sb = [pick(biased, e) for e in range(4)]
    i1 = jnp.zeros_like(gidx)
    v1 = sb[0]
    for e in range(1, 4):
        better = sb[e] > v1
        i1 = jnp.where(better, e, i1)
        v1 = jnp.where(better, sb[e], v1)
    i2 = jnp.zeros_like(gidx)
    v2 = jnp.full_like(v1, NEG_INF)
    for e in range(4):
        better = jnp.logical_and(i1 != e, sb[e] > v2)
        i2 = jnp.where(better, e, i2)
        v2 = jnp.where(better, sb[e], v2)
    a = jnp.minimum(i1, i2)
    b = jnp.maximum(i1, i2)
    pair = jnp.where(a == 0, jnp.where(b == 1, 0, jnp.where(b == 2, 2, 3)), jnp.where(a == 1, jnp.where(b == 2, 1, 4), 5))
    return 6 * gidx + pair


def _out_kernel(nmix, nct, two_src, sub, t0, *refs):
    per = nmix + (2 if two_src else 1)
    tile_refs = [refs[per * u:per * (u + 1)] for u in range(sub)]
    (w_ref, mod_ref, g_ref, rw_ref, rwh_ref, rb_ref,
     xmid_ref, hrow_ref, bucket_ref, rank_ref, cnt_ref, cnt_scr) = refs[per * sub:]
    i = pl.program_id(0)

    @pl.when(i == 0)
    def _():
        cnt_scr[...] = jnp.zeros_like(cnt_scr)

    brow = lax.broadcasted_iota(I32, (NB_PAD, TM), 0)
    r = lax.broadcasted_iota(I32, (TM, TM), 0)
    c = lax.broadcasted_iota(I32, (TM, TM), 1)
    before = (r < c).astype(F32).astype(BF16)
    onehots = []
    for u in range(sub):
        mix_refs = tile_refs[u][:nmix]
        is_ctx = sub * i + u + t0 < nct
        mod = jnp.where(is_ctx, mod_ref[1], mod_ref[0])
        mix = mix_refs[0][...] if nmix == 1 else jnp.concatenate([mr[...] for mr in mix_refs], axis=1)
        if two_src:
            x = jnp.where(is_ctx, tile_refs[u][nmix][...], tile_refs[u][nmix + 1][...])
        else:
            x = tile_refs[u][nmix][...]
        x_mid = x + mod[2:3, :] * _dot(mix, w_ref[...])
        xmid_ref[TM * u:TM * (u + 1), :] = x_mid
        h = _norm_mod(x_mid, g_ref[...], mod[4:5, :], mod[3:4, :])
        hb = h.astype(BF16)
        h_lo = (h - hb.astype(F32)).astype(BF16)
        logits = _hi_lo_cols(_dot(hb, rw_ref[...]), _dot(h_lo, rwh_ref[...]), 16)
        logits_t = jnp.transpose(logits)[0:16, :]
        bucket = _route(logits_t, rb_ref[...])
        hrow_u = hrow_ref.at[pl.ds(TM * TOK * u, TM * TOK), :]
        for cblk in range(TOK):
            _tok_store(hrow_u, cblk, h[:, 128 * cblk:128 * (cblk + 1)])
        bucket_ref[u] = bucket
        onehots.append((brow == bucket).astype(F32))
    cnt = cnt_scr[...]
    for u in range(sub):
        onehot = onehots[u]
        cum = _dot(onehot.astype(BF16), before)
        rank_ref[u] = jnp.sum(onehot * (cum + cnt[:, 0:1]), axis=0, keepdims=True).astype(I32)
        cnt = cnt + jnp.sum(onehot, axis=1, keepdims=True)
    cnt_scr[...] = cnt
    cnt_ref[...] = cnt.astype(I32)


def _out_proj(mixes, w_out, xs, mod, g_ffn, rw, rw_hi, rb_col, t0, ntiles, nct):
    nmix = len(mixes)
    two_src = len(xs) == 2
    n = ntiles * TM
    sub = max(s for s in (5, 4, 3, 2, 1) if ntiles % s == 0)
    in_specs, args = [], []
    for u in range(sub):
        tile_of = lambda i, u=u: sub * i + u + t0
        for mx in mixes:
            in_specs.append(pl.BlockSpec((TM, mx.shape[1]), lambda i, f=tile_of: (f(i), 0)))
            args.append(mx)
        if two_src:
            in_specs += [pl.BlockSpec((TM, D), lambda i, f=tile_of: (jnp.minimum(f(i), nct - 1), 0)),
                         pl.BlockSpec((TM, D), lambda i, f=tile_of: (jnp.maximum(f(i) - nct, 0), 0))]
        else:
            in_specs.append(pl.BlockSpec((TM, D), lambda i, f=tile_of: (f(i), 0)))
        args += list(xs)
    in_specs += [_full((D, D)), _full((2, 6, D)), _full((1, D)), _full((D, 128)), _full((D, 128)), _full((16, 1))]
    rows_out = pl.BlockSpec((sub, 1, TM), lambda i: (i, 0, 0))
    return pl.pallas_call(
        functools.partial(_out_kernel, nmix, nct, two_src, sub, t0),
        grid=(ntiles // sub,),
        in_specs=in_specs,
        out_specs=[pl.BlockSpec((sub * TM, D), lambda i: (i, 0)),
                   pl.BlockSpec((sub * TM * TOK, 128), lambda i: (i, 0)), rows_out, rows_out,
                   _full((NB_PAD, 128))],
        out_shape=[jax.ShapeDtypeStruct((n, D), F32), jax.ShapeDtypeStruct((n * TOK, 128), F32),
                   jax.ShapeDtypeStruct((ntiles, 1, TM), I32), jax.ShapeDtypeStruct((ntiles, 1, TM), I32),
                   jax.ShapeDtypeStruct((NB_PAD, 128), I32)],
        scratch_shapes=[pltpu.VMEM((NB_PAD, 128), F32)],
        compiler_params=_cparams("arbitrary"),
        name="out_proj_router",
    )(*args, w_out, mod, g_ffn, rw, rw_hi, rb_col)


def _scatter_kernel(ntiles, pos_ref, flo_ref, fhi_ref, src_ref, dst_ref, hbuf, in_sem, out_sem):
    i = pl.program_id(0)
    slot = i % 3
    h_ref = hbuf.at[slot]
    sem = out_sem.at[slot]
    rows = TM * TOK

    def load(tile, s):
        return pltpu.make_async_copy(src_ref.at[pl.ds(pl.multiple_of(tile * rows, rows), rows), :],
                                     hbuf.at[s], in_sem.at[s])

    def tok(ref, t):
        return ref.at[pl.ds(pl.multiple_of(t * TOK, TOK), TOK), :]

    def copy(r, d_row, src=h_ref, sm=sem):
        return pltpu.make_async_copy(tok(src, r), tok(dst_ref, d_row), sm)

    def wait_rows(lo, hi, unroll, s=slot):
        def body(r, carry):
            copy(0, 0, hbuf.at[s], out_sem.at[s]).wait()
            return carry
        lax.fori_loop(lo, hi, body, 0, unroll=unroll)

    @pl.when(i == 0)
    def _():
        load(0, 0).start()
        if ntiles > 1:
            load(1, 1).start()

    load(i, slot).wait()

    def start(r8, c):
        for k in range(8):
            r = r8 * 8 + k
            copy(r, pos_ref[i * TM + r]).start(priority=k % 2)
        return c
    lax.fori_loop(0, TM // 8, start, 0)

    @pl.when(i > 0)
    def _():
        wait_rows(0, TM, 8, (i + 2) % 3)

    @pl.when(i + 2 < ntiles)
    def _():
        load(i + 2, (i + 2) % 3).start()

    @pl.when(i == ntiles - 1)
    def _():
        wait_rows(0, TM, 8)

        def pad_copies(b, act):
            off, left = flo_ref[b], fhi_ref[b] - flo_ref[b]
            for bit in reversed(range(TMM.bit_length() - 1)):
                k = 1 << bit
                take = (left & k) != 0

                @pl.when(take)
                def _(off=off, k=k):
                    act(pltpu.make_async_copy(
                        h_ref.at[pl.ds(0, k * TOK), :],
                        dst_ref.at[pl.ds(pl.multiple_of(off * TOK, TOK), k * TOK), :], sem))
                off = off + jnp.where(take, k, 0)

        def fill(b, c):
            pad_copies(b, lambda cp: cp.start())
            return c

        def drain(b, c):
            pad_copies(b, lambda cp: cp.wait())
            return c
        lax.fori_loop(0, N_BUCKETS, fill, 0)
        lax.fori_loop(0, N_BUCKETS, drain, 0)

        def tile_copy(j):
            rows = TMM * TOK
            return pltpu.make_async_copy(h_ref, dst_ref.at[pl.ds(pl.multiple_of(j * rows, rows), rows), :], sem)

        def fill_tile(j, c):
            tile_copy(j).start()
            return c

        def wait_tile(j, c):
            tile_copy(j).wait()
            return c
        lax.fori_loop(flo_ref[N_BUCKETS], fhi_ref[N_BUCKETS], fill_tile, 0)
        lax.fori_loop(flo_ref[N_BUCKETS], fhi_ref[N_BUCKETS], wait_tile, 0)


def _scatter_rows(hrow, pos, fill_lo, fill_hi, n, npad):
    assert TM == TMM
    return pl.pallas_call(
        functools.partial(_scatter_kernel, n // TM),
        grid_spec=pltpu.PrefetchScalarGridSpec(
            num_scalar_prefetch=3,
            grid=(n // TM,),
            in_specs=[pl.BlockSpec(memory_space=pl.ANY)],
            out_specs=pl.BlockSpec(memory_space=pl.ANY),
            scratch_shapes=[pltpu.VMEM((3, TM * TOK, 128), hrow.dtype), pltpu.SemaphoreType.DMA((3,)),
                            pltpu.SemaphoreType.DMA((3,))],
        ),
        out_shape=jax.ShapeDtypeStruct((npad * TOK, 128), hrow.dtype),
        compiler_params=_cparams("arbitrary"),
        name="moe_scatter_rows",
    )(pos, fill_lo, fill_hi, hrow)


def _combine_kernel(ntiles, pos_ref, x_ref, mod_ref, ys_ref, o_ref, ybuf, sem):
    i = pl.program_id(0)

    def copy(tile, slot, r):
        src = pl.multiple_of(pos_ref[tile * TM + r] * TOK, TOK)
        dst = pl.multiple_of(r * TOK, TOK)
        return pltpu.make_async_copy(ys_ref.at[pl.ds(src, TOK), :], ybuf.at[slot, pl.ds(dst, TOK), :], sem.at[slot])

    def start_tile(tile, slot):
        def body(r8, carry):
            for k in range(8):
                copy(tile, slot, r8 * 8 + k).start(priority=k % 2)
            return carry
        lax.fori_loop(0, TM // 8, body, 0)

    @pl.when(i == 0)
    def _():
        start_tile(0, 0)

    @pl.when(i + 1 < ntiles)
    def _():
        start_tile(i + 1, (i + 1) % 2)

    slot = i % 2

    def wait_body(r, carry):
        copy(i, slot, 0).wait()
        return carry
    lax.fori_loop(0, TM, wait_body, 0, unroll=8)
    o_ref[...] = x_ref[...] + mod_ref[0, 5:6, :] * _tok_rows(ybuf.at[slot], TM)


def _combine(x_mid, mod, ys, pos, t0, nct):
    n = x_mid.shape[0]
    ntiles = n // TM
    return pl.pallas_call(
        functools.partial(_combine_kernel, ntiles),
        grid_spec=pltpu.PrefetchScalarGridSpec(
            num_scalar_prefetch=1,
            grid=(ntiles,),
            in_specs=[pl.BlockSpec((TM, D), lambda i, *_: (i, 0)),
                      pl.BlockSpec((1, 6, D), lambda i, *_: (jnp.where(i + t0 < nct, 1, 0), 0, 0)),
                      pl.BlockSpec(memory_space=pl.ANY)],
            out_specs=pl.BlockSpec((TM, D), lambda i, *_: (i, 0)),
            scratch_shapes=[pltpu.VMEM((2, TM * TOK, 128), F32), pltpu.SemaphoreType.DMA((2,))],
        ),
        out_shape=jax.ShapeDtypeStruct((n, D), F32),
        compiler_params=_cparams("arbitrary"),
        name="moe_combine",
    )(pos, x_mid, mod, ys)


def _moe_kernel(tea_ref, teb_ref, tblk_ref, tval_ref, x_ref, rw_ref, w1a_ref, w3a_ref, w2a_ref,
                w1b_ref, w3b_ref, w2b_ref, y_ref):
    del tblk_ref
    j = pl.program_id(0)

    @pl.when(tval_ref[j] != 0)
    def _():
        h = _tok_rows(x_ref, TMM)
        hb = h.astype(BF16)
        logits = _hi_lo_cols(_dot(hb, rw_ref[...]), 0.0, 16)
        scores = _sigmoid(logits)
        lane = lax.broadcasted_iota(I32, scores.shape, 1)
        s_a = jnp.sum(jnp.where(lane == tea_ref[j], scores, 0.0), axis=1, keepdims=True)
        s_b = jnp.sum(jnp.where(lane == teb_ref[j], scores, 0.0), axis=1, keepdims=True)
        gates = (s_a / (s_a + s_b), s_b / (s_a + s_b))
        acts = []
        for w1_ref, w3_ref, gate in ((w1a_ref, w3a_ref, gates[0]), (w1b_ref, w3b_ref, gates[1])):
            u = _dot(hb, w1_ref[0, 0].astype(BF16))
            v = _dot(hb, w3_ref[0, 0].astype(BF16))
            acts.append((_silu(u) * v * gate).astype(BF16))
        y = _dot(acts[0], w2a_ref[0, 0].astype(BF16)) + _dot(acts[1], w2b_ref[0, 0].astype(BF16))
        for cblk in range(TOK):
            _tok_store(y_ref, cblk, y[:, 128 * cblk:128 * (cblk + 1)])

    @pl.when(tval_ref[j] == 0)
    def _():
        y_ref[...] = jnp.zeros_like(y_ref)


def _moe(xs_sorted, rw, w1, w3, w2, layer, tile_ea, tile_eb, tile_blk, tile_valid):
    npad = xs_sorted.shape[0] // TOK
    ntile = npad // TMM
    wspec = lambda shape, which: pl.BlockSpec(
        (1, 1) + shape, lambda j, ea, eb, blk, val: (layer, (ea, eb)[which][j], 0, 0))
    up, down = (D, D_EXPERT), (D_EXPERT, D)
    return pl.pallas_call(
        _moe_kernel,
        grid_spec=pltpu.PrefetchScalarGridSpec(
            num_scalar_prefetch=4,
            grid=(ntile,),
            in_specs=[pl.BlockSpec((TMM * TOK, 128), lambda j, ea, eb, blk, val: (blk[j], 0)),
                      pl.BlockSpec((D, 128), lambda j, *_: (0, 0)),
                      wspec(up, 0), wspec(up, 0), wspec(down, 0), wspec(up, 1), wspec(up, 1), wspec(down, 1)],
            out_specs=pl.BlockSpec((TMM * TOK, 128), lambda j, ea, eb, blk, val: (j, 0)),
        ),
        out_shape=jax.ShapeDtypeStruct((npad * TOK, 128), F32),
        compiler_params=_cparams("arbitrary"),
        name="moe_experts",
    )(tile_ea, tile_eb, tile_blk, tile_valid, xs_sorted, rw, w1, w3, w2, w1, w3, w2)


def _moe_block(x_mid, hrow, bucket, rank, counts, mod, rw, rw_hi, w1, w3, w2, layer, t0, nct):
    n = hrow.shape[0] // TOK
    ntile = n // TMM + N_BUCKETS
    npad = ntile * TMM
    cnt = counts[:N_BUCKETS, 0]
    padded = ((cnt + TMM - 1) // TMM) * TMM
    ends = jnp.cumsum(padded)
    starts = ends - padded
    total_tiles = ends[-1] // TMM
    tiles = jnp.arange(ntile, dtype=I32)
    tile_valid = (tiles < total_tiles).astype(I32)
    tile_blk = jnp.minimum(tiles, jnp.maximum(total_tiles - 1, 0))
    tile_bucket = jnp.minimum(jnp.sum((ends[None, :] <= (tile_blk * TMM)[:, None]).astype(I32), axis=1), N_BUCKETS - 1)
    pair = tile_bucket % 6
    grp = tile_bucket // 6
    slot_a = sum(jnp.where(pair == k, e, 0) for k, (e, _) in enumerate(_PAIR_SLOTS))
    slot_b = sum(jnp.where(pair == k, e, 0) for k, (_, e) in enumerate(_PAIR_SLOTS))
    tile_ea = (4 * grp + slot_a).astype(I32)
    tile_eb = (4 * grp + slot_b).astype(I32)
    bucket = bucket.reshape(-1)
    onehot = (bucket[:, None] == jnp.arange(N_BUCKETS, dtype=I32)[None, :]).astype(I32)
    pos = (rank.reshape(-1) + jnp.sum(onehot * starts[None, :], axis=1)).astype(I32)
    pad32 = lambda a, tail: jnp.zeros((NB_PAD,), I32).at[:N_BUCKETS].set(a.astype(I32)).at[N_BUCKETS].set(tail)
    fill_lo, fill_hi = pad32(starts + cnt, total_tiles), pad32(ends, ntile)
    xs_sorted = _scatter_rows(hrow, pos, fill_lo, fill_hi, n, npad)
    ys = _moe(xs_sorted, rw, w1, w3, w2, layer, tile_ea, tile_eb, tile_blk, tile_valid)
    return _combine(x_mid, mod, ys, pos, t0, nct)


def kernel(x, c, ctx, c_ctx, router_w, router_b, norm_mix, norm_ffn, w_mod, b_mod, ev_w_in, ev_conv_w, ev_conv_b, ev_dt_bias, ev_a_log, ev_d_skip, ev_ssd_norm, ev_q_norm, ev_k_norm, ev_sink, ev_w_out, od_w_in, od_conv_w, od_conv_b, od_igate_b, od_fgate_b, od_head_norm, od_w_out, moe_w1, moe_w3, moe_w2):
    s_len = x.shape[1]
    c_len = ctx.shape[1]
    assert x.shape[0] == 1 and s_len % TM == 0 and c_len % TM == 0 and s_len % GRID_W == 0
    nct = c_len // TM
    ncc, nlc = c_len // T, s_len // T
    nt = c_len + s_len
    ntiles = nt // TM

    mod = _modulation(c, c_ctx, w_mod, b_mod)
    rw, rw_hi = _hi_lo_weight(router_w)
    rb_col = router_b.reshape(N_EXPERTS, 1)
    pad128 = lambda v: jnp.zeros((1, 128), F32).at[0, :v.shape[0]].set(v)

    w = ev_w_in[0]
    rope_rows, rope_cols = _rope_tables(s_len)
    wdt, wdt_hi = _hi_lo_weight(w[:, 1280:1296])
    p0 = dict(
        g_mix=norm_mix[0].reshape(1, D),
        w_zx=w[:, 0:1280].astype(BF16), w_qkv=w[:, 1296:2064].astype(BF16), wdt=wdt, wdt_hi=wdt_hi,
        conv_w=ev_conv_w[0], conv_b=ev_conv_b[0].reshape(1, 768),
        dt_bias=pad128(ev_dt_bias[0].reshape(16)),
        q_norm=jnp.tile(ev_q_norm[0], 8).reshape(1, 512), k_norm=jnp.tile(ev_k_norm[0], 2).reshape(1, 128),
        rope_rows=rope_rows, rope_cols=rope_cols,
        alog_row=ev_a_log[0].reshape(1, 16), alog_col=ev_a_log[0].reshape(16, 1),
        d_skip=jnp.repeat(ev_d_skip[0], 64).reshape(1, 512), ssd_norm=ev_ssd_norm[0].reshape(1, 512))
    z, xs, bc, q, kk, vv, dtc, dtr = _in0(ctx[0], x[0], mod[0], p0, nct)
    yf = _ssd(False, xs, bc, dtc, dtr, p0, ncc, nlc)
    ymix = _ssd(True, xs, bc, dtc, dtr, p0, ncc, nlc, yf=yf, z=z)
    att = _attention(q, kk, vv, ev_sink[0].reshape(1, 8), ncc, nlc)
    x_mid0, hrow, bucket, rank, counts = _out_proj(
        [ymix, att], ev_w_out[0].astype(BF16), [ctx[0], x[0]], mod[0], norm_ffn[0].reshape(1, D),
        rw, rw_hi, rb_col, 0, ntiles, nct)
    x1 = _moe_block(x_mid0, hrow, bucket, rank, counts, mod[0], rw, rw_hi, moe_w1, moe_w3, moe_w2, 0, 0, nct)

    w = od_w_in[0]
    wg, wg_hi = _hi_lo_weight(w[:, 3072:3104])
    p1 = dict(
        g_mix=norm_mix[1].reshape(1, D),
        wcat=w[:, 0:3072].astype(BF16), wg=wg, wg_hi=wg_hi,
        conv_w=od_conv_w[0], conv_b=od_conv_b[0].reshape(1, 2048),
        gate_bias=pad128(jnp.concatenate([od_igate_b[0].reshape(16), od_fgate_b[0].reshape(16)])),
        head_norm=od_head_norm[0].reshape(1, 1024))
    q1, kt1, v1, og1, gc1, gr1 = _in1(x1, mod[1], p1, nct)
    hf = _mlstm(False, q1, kt1, v1, gc1, gr1, p1, ncc, nlc)
    hmix = _mlstm(True, q1, kt1, v1, gc1, gr1, p1, ncc, nlc, hf=hf, og=og1)
    x_mid1, hrow, bucket, rank, counts = _out_proj(
        [hmix], od_w_out[0].astype(BF16), [x1], mod[1], norm_ffn[1].reshape(1, D),
        rw, rw_hi, rb_col, nct, ntiles - nct, nct)
    return _moe_block(x_mid1, hrow, bucket, rank, counts, mod[1], rw, rw_hi, moe_w1, moe_w3, moe_w2, 1, nct,
                      nct)[None]
```

```python
import functools
import math

import jax
import jax.numpy as jnp
from jax import lax
from jax.experimental import pallas as pl
from jax.experimental.pallas import tpu as pltpu

F32 = jnp.float32
BF16 = jnp.bfloat16
I32 = jnp.int32

EPS = 1e-6
D = 1024
T = 128
SCAN_CHUNKS = 2
TM = 256
TMM = 256
GRID_W = 64
ROPE_THETA = 10000.0
N_EXPERTS = 16
N_BUCKETS = 24
NB_PAD = 32
D_EXPERT = 512
TOK = 8
U32 = jnp.uint32
NEG_INF = float("-inf")
VMEM_LIMIT = 56 * 1024 * 1024

_NN = (((1,), (0,)), ((), ()))
_NT = (((1,), (1,)), ((), ()))
_TN = (((0,), (0,)), ((), ()))

_PAIR_SLOTS = ((0, 1), (2, 1), (2, 0), (3, 0), (3, 1), (3, 2))


def _cparams(*sem):
    return pltpu.CompilerParams(dimension_semantics=sem, vmem_limit_bytes=VMEM_LIMIT)


def _dot(a, b, dims=_NN):
    return lax.dot_general(a, b, dims, preferred_element_type=F32)


def _split(a, n):
    out = []
    r = a
    for _ in range(n):
        t = r.astype(BF16)
        out.append(t)
        r = r - t.astype(F32)
    return out


def _mdot(as_, bs, dims=_NN, order=None):
    if order is None:
        order = len(as_) + len(bs) - 2
    acc = None
    for i, a in enumerate(as_):
        for j, b in enumerate(bs):
            if i + j <= order:
                p = _dot(a, b, dims)
                acc = p if acc is None else acc + p
    return acc


def _sigmoid(x):
    return 1.0 / (1.0 + jnp.exp(-x))


def _silu(x):
    return x * _sigmoid(x)


def _log1p_exp_neg_abs(x):
    e = jnp.exp(-jnp.abs(x))
    u = 1.0 + e
    um1 = u - 1.0
    return jnp.where(um1 == 0.0, e, jnp.log(u) * (e / jnp.where(um1 == 0.0, 1.0, um1)))


def _softplus(x):
    return jnp.maximum(x, 0.0) + _log1p_exp_neg_abs(x)


def _log_sigmoid(x):
    return jnp.minimum(x, 0.0) - _log1p_exp_neg_abs(x)


def _norm_mod(x, g, sc, sh):
    ms = jnp.mean(x * x, axis=-1, keepdims=True)
    return (x * lax.rsqrt(ms + EPS)) * g * (1.0 + sc) + sh


def _tok_load(ref, chunk, n):
    return ref[pl.ds(chunk, n, stride=TOK), :]


def _tok_store(ref, chunk, val):
    ref[pl.ds(chunk, val.shape[0], stride=TOK), :] = val


def _tok_rows(ref, n):
    return jnp.concatenate([_tok_load(ref, c, n) for c in range(TOK)], axis=1)


def _tri(rev):
    r = lax.broadcasted_iota(I32, (T, T), 0)
    c = lax.broadcasted_iota(I32, (T, T), 1)
    return (c >= r) if rev else (c <= r)


def _cumsums(rev, col, row):
    tri = _tri(rev)
    tri_b = tri.astype(F32).astype(BF16)
    trit_b = _tri(not rev).astype(F32).astype(BF16)
    ccol = _mdot([tri_b], _split(col, 3))
    crow = _mdot(_split(row, 3), [trit_b])
    return tri, ccol, crow


def _lane_bcast(col):
    return [jnp.broadcast_to(col[:, h:h + 1], (col.shape[0], 128)) for h in range(col.shape[1])]


def _mod_kernel(c_ref, w_ref, b_ref, o_ref):
    a = _silu(c_ref[...])
    o_ref[0] = _mdot(_split(a, 2), _split(w_ref[0], 2), order=1) + b_ref[0]


def _modulation(c, c_ctx, w_mod, b_mod):
    depth = w_mod.shape[0]
    n = w_mod.shape[2]
    tn = 1536
    cc = jnp.zeros((8, D), F32).at[0].set(c[0]).at[1].set(c_ctx)
    out = pl.pallas_call(
        _mod_kernel,
        grid=(depth, n // tn),
        in_specs=[
            pl.BlockSpec((8, D), lambda l, j: (0, 0)),
            pl.BlockSpec((1, D, tn), lambda l, j: (l, 0, j)),
            pl.BlockSpec((1, 1, tn), lambda l, j: (l, 0, j)),
        ],
        out_specs=pl.BlockSpec((1, 8, tn), lambda l, j: (l, 0, j)),
        out_shape=jax.ShapeDtypeStruct((depth, 8, n), F32),
        compiler_params=_cparams("arbitrary", "arbitrary"),
        name="modulation",
    )(cc, w_mod, b_mod.reshape(depth, 1, n))
    return out[:, :2].reshape(depth, 2, 6, D)


def _halo_specs(nrows, tile_of):
    nb8 = nrows // 8
    return [
        pl.BlockSpec((8, D), lambda i: (jnp.maximum(tile_of(i) * (TM // 8) - 1, 0), 0)),
        pl.BlockSpec((TM, D), lambda i: (tile_of(i), 0)),
        pl.BlockSpec((8, D), lambda i: (jnp.minimum((tile_of(i) + 1) * (TM // 8), nb8 - 1), 0)),
    ]


def _ctx_tile(nct):
    return lambda i: jnp.minimum(i, nct - 1)


def _lat_tile(nct):
    return lambda i: jnp.maximum(i - nct, 0)


def _mod_spec(nct, t0=0):
    return pl.BlockSpec((1, 6, D), lambda i: (jnp.where(i + t0 < nct, 1, 0), 0, 0))


def _full(shape):
    nd = len(shape)
    return pl.BlockSpec(shape, lambda i: (0,) * nd)


def _seq_edges(i, nct, ntiles):
    prev_ok = jnp.logical_and(i != 0, i != nct).astype(F32)
    next_ok = jnp.logical_and(i != nct - 1, i != ntiles - 1).astype(F32)
    return prev_ok, next_ok


def _conv_silu(x, x_first_prev, x_last_next, cw, cb):
    n = x.shape[0]
    rows = lax.broadcasted_iota(I32, x.shape, 0)
    x_prev = jnp.where(rows == 0, x_first_prev, pltpu.roll(x, 1, 0))
    x_next = jnp.where(rows == n - 1, x_last_next, pltpu.roll(x, n - 1, 0))
    return _silu(x_prev * cw[0:1] + x * cw[1:2] + x_next * cw[2:3] + cb)


def _hi_lo_cols(blk, lo_pass, n):
    return blk + pltpu.roll(blk, 128 - n, 1) + lo_pass


def _hi_lo_weight(w):
    n = w.shape[1]
    hi = w.astype(BF16)
    lo = (w - hi.astype(F32)).astype(BF16)
    z = jnp.zeros((w.shape[0], 128 - 2 * n), BF16)
    return jnp.concatenate([hi, lo, z], axis=1), jnp.concatenate([hi, jnp.zeros_like(lo), z], axis=1)


def _head_rms(xf, gamma):
    r = lax.broadcasted_iota(I32, (128, 128), 0) // 64
    c = lax.broadcasted_iota(I32, (128, 128), 1) // 64
    ones_bd = (r == c).astype(F32).astype(BF16)
    outs = []
    for j in range(xf.shape[1] // 128):
        blk = xf[:, 128 * j:128 * (j + 1)]
        ssum = _dot((blk * blk).astype(BF16), ones_bd)
        outs.append(blk * lax.rsqrt(ssum * (1.0 / 64.0) + EPS))
    return jnp.concatenate(outs, axis=1) * gamma


def _rope(xf, cos, sin):
    lane = lax.broadcasted_iota(I32, (xf.shape[0], 128), 1)
    first = (lane % 32) < 16
    outs = []
    for j in range(xf.shape[1] // 128):
        blk = xf[:, 128 * j:128 * (j + 1)]
        partner = jnp.where(first, pltpu.roll(blk, 112, 1), pltpu.roll(blk, 16, 1))
        outs.append(blk * cos + partner * sin)
    return jnp.concatenate(outs, axis=1)


def _rope_tables(s_len):
    lane = jnp.arange(128, dtype=I32)
    inv = ROPE_THETA ** (-(lane % 16).astype(F32) / 16.0)
    sign = jnp.where((lane % 32) < 16, -1.0, 1.0).astype(F32)
    ang_r = jnp.arange(s_len // GRID_W, dtype=F32)[:, None] * inv[None, :]
    ang_c = jnp.tile(jnp.arange(GRID_W, dtype=F32), TM // GRID_W)[:, None] * inv[None, :]
    both = lambda ang: jnp.stack([jnp.cos(ang), jnp.sin(ang) * sign[None, :]])
    return both(ang_r), both(ang_c)


def _in0_kernel(nct, ntiles, cp_ref, c_ref, cn_ref, xp_ref, x_ref, xn_ref, mod_ref, g_ref, wa_ref, wb_ref,
                wdt_ref, wdth_ref, cw_ref, cb_ref, dtb_ref, qn_ref, kn_ref, rowcs_ref, colcs_ref,
                z_ref, xs_ref, bc_ref, q_ref, kk_ref, vv_ref, dtc_ref, dtr_ref):
    i = pl.program_id(0)
    is_ctx = i < nct
    sh = mod_ref[0, 0:1, :]
    sc = mod_ref[0, 1:2, :]
    g = g_ref[...]
    x_all = jnp.concatenate([jnp.where(is_ctx, cp_ref[...], xp_ref[...]),
                             jnp.where(is_ctx, c_ref[...], x_ref[...]),
                             jnp.where(is_ctx, cn_ref[...], xn_ref[...])], axis=0)
    h_all = _norm_mod(x_all, g, sc, sh)
    hb_all = h_all.astype(BF16)
    zx_all = _dot(hb_all, wa_ref[...])
    h, hb, zx = h_all[8:8 + TM], hb_all[8:8 + TM], zx_all[8:8 + TM]
    qkv = _dot(hb, wb_ref[...])
    prev_ok, next_ok = _seq_edges(i, nct, ntiles)
    xb_prev = zx_all[7:8, 512:1280] * prev_ok
    xb_next = zx_all[8 + TM:9 + TM, 512:1280] * next_ok
    act = _conv_silu(zx[:, 512:1280], xb_prev, xb_next, cw_ref[...], cb_ref[...])
    z_ref[...] = zx[:, 0:512].astype(BF16)
    xs_ref[...] = act[:, 0:512].astype(BF16)
    bc_ref[...] = act[:, 512:768].astype(BF16)
    row0 = jnp.maximum(i - nct, 0) * (TM // GRID_W)
    rowcs = [jnp.concatenate([jnp.broadcast_to(rowcs_ref[t, pl.ds(row0 + kq, 1), :], (GRID_W, 128))
                              for kq in range(TM // GRID_W)], axis=0) for t in range(2)]
    row_lanes = (lax.broadcasted_iota(I32, (TM, 128), 1) % 64) < 32
    cos = jnp.where(is_ctx, 1.0, jnp.where(row_lanes, rowcs[0], colcs_ref[0]))
    sin = jnp.where(is_ctx, 0.0, jnp.where(row_lanes, rowcs[1], colcs_ref[1]))
    q = _rope(_head_rms(qkv[:, 0:512], qn_ref[...]), cos, sin) * 0.125
    q_ref[...] = q.astype(BF16)
    k = _rope(_head_rms(qkv[:, 512:640], kn_ref[...]), cos, sin)
    kk_ref[...] = jnp.concatenate([k, pltpu.roll(k, 64, 1)], axis=1).astype(BF16)
    v = qkv[:, 640:768]
    vv_ref[...] = jnp.concatenate([v, pltpu.roll(v, 64, 1)], axis=1).astype(BF16)
    h_lo = (h - hb.astype(F32)).astype(BF16)
    dt = _softplus(_hi_lo_cols(_dot(hb, wdt_ref[...]), _dot(h_lo, wdth_ref[...]), 16) + dtb_ref[...])
    dtc_ref[...] = dt[:, 0:16]
    dtr_ref[...] = jnp.transpose(dt)[0:16, :]


def _in0(ctx2, x2, mod, p, nct):
    c_len, s_len = ctx2.shape[0], x2.shape[0]
    nt = c_len + s_len
    ntiles = nt // TM
    tile = lambda w: pl.BlockSpec((TM, w), lambda i: (i, 0))
    lat = _lat_tile(nct)
    lat_tile = lambda w: pl.BlockSpec((TM, w), lambda i: (lat(i), 0))
    outs = [(512, BF16), (512, BF16), (256, BF16), (512, BF16), (256, BF16), (256, BF16), (16, F32)]
    return pl.pallas_call(
        functools.partial(_in0_kernel, nct, ntiles),
        grid=(ntiles,),
        in_specs=_halo_specs(c_len, _ctx_tile(nct)) + _halo_specs(s_len, lat) + [
            _mod_spec(nct), _full((1, D)), _full((D, 1280)), _full((D, 768)), _full((D, 128)), _full((D, 128)),
            _full((3, 768)), _full((1, 768)), _full((1, 128)),
            _full((1, 512)), _full((1, 128)), _full((2, s_len // GRID_W, 128)), _full((2, TM, 128))],
        out_specs=[tile(w) for w, _ in outs] + [pl.BlockSpec((16, TM), lambda i: (0, i))],
        out_shape=[jax.ShapeDtypeStruct((nt, w), dt) for w, dt in outs]
        + [jax.ShapeDtypeStruct((16, nt), F32)],
        compiler_params=_cparams("arbitrary"),
        name="in_proj_even",
    )(ctx2, ctx2, ctx2, x2, x2, x2, mod, p["g_mix"], p["w_zx"], p["w_qkv"], p["wdt"], p["wdt_hi"],
      p["conv_w"], p["conv_b"], p["dt_bias"], p["q_norm"], p["k_norm"], p["rope_rows"], p["rope_cols"])


def _scan_chunk_map(rev, ncc, nlc):
    if not rev:
        return lambda j: j
    return lambda j: jnp.where(j < ncc, ncc - 1 - j, ncc + nlc - 1 - (j - ncc))


def _ssd_kernel(rev, *refs):
    if rev:
        (xs_ref, bc_ref, dtc_ref, dtr_ref, alr_ref, alc_ref, yf_ref, z_ref, dsk_ref, nrm_ref,
         o_ref, st_ref) = refs
    else:
        xs_ref, bc_ref, dtc_ref, dtr_ref, alr_ref, alc_ref, o_ref, st_ref = refs
    j = pl.program_id(0)

    @pl.when(j == 0)
    def _():
        st_ref[...] = jnp.zeros_like(st_ref)

    d = 8 if rev else 0
    a_coef_row = -jnp.exp(alr_ref[...])[:, d:d + 8]
    a_coef_col = -jnp.exp(alc_ref[...])[d:d + 8, :]
    lane = lax.broadcasted_iota(I32, (T, 128), 1)
    lo = lane < 64
    zero_b = jnp.zeros((T, 128), BF16)
    hi_half = jnp.logical_not(lo)
    sub = lax.broadcasted_iota(I32, (128, T), 0)
    eye = (sub == lax.broadcasted_iota(I32, (128, T), 1)).astype(F32).astype(BF16)
    end = 0 if rev else T - 1
    order = tuple(reversed(range(SCAN_CHUNKS))) if rev else tuple(range(SCAN_CHUNKS))

    def prologue(c):
        rows = slice(c * T, (c + 1) * T)
        dtc = dtc_ref[rows, d:d + 8]
        dtr = dtr_ref[d:d + 8, rows]
        tri, acs_col, acs_row = _cumsums(rev, dtc * a_coef_row, dtr * a_coef_col)
        atot_col = acs_row[:, end:end + 1]
        acs_bc = _lane_bcast(acs_col)
        dec_row = jnp.exp(atot_col - acs_row) * dtr
        xs = xs_ref[rows, :]
        bm = bc_ref[rows, 0:128]
        cm = bc_ref[rows, 128:256]
        cgs = [jnp.where(lo, cm, zero_b), jnp.where(hi_half, cm, zero_b)]
        cbs = [_dot(cgs[g], bm, _NT) for g in range(2)]
        bmt = _dot(eye, bm, _NT)
        bgts = [jnp.where(sub < 64, bmt, 0.0), jnp.where(sub >= 64, bmt, 0.0)]
        xpairs = []
        for pr in range(4):
            xp = xs[:, 128 * pr:128 * (pr + 1)]
            xpairs.append(jnp.concatenate([jnp.where(lo, xp, zero_b), jnp.where(hi_half, xp, zero_b)], axis=0))
        return dict(tri=tri, acs_row=acs_row, acs_bc=acs_bc, atot_col=atot_col, dec_row=dec_row, dtr=dtr,
                    xs=xs, cgs=cgs, cbs=cbs, bgts=bgts, xpairs=xpairs)

    pro = {c: prologue(c) for c in order}
    y_off = {}
    for c in order:
        p = pro[c]
        for pr in range(4):
            g, h0, h1 = pr // 2, 2 * pr, 2 * pr + 1
            st = st_ref[pr]
            eacs = jnp.exp(jnp.where(lo, p["acs_bc"][h0], p["acs_bc"][h1]))
            y_off[c, pr] = _dot(p["cgs"][g], st.astype(BF16)) * eacs
            bdec = jnp.concatenate([(p["bgts"][g] * p["dec_row"][h0:h0 + 1, :]).astype(BF16),
                                    (p["bgts"][g] * p["dec_row"][h1:h1 + 1, :]).astype(BF16)], axis=1)
            carry = jnp.where(lo[0:1, :], jnp.exp(p["atot_col"][h0:h0 + 1, :]), jnp.exp(p["atot_col"][h1:h1 + 1, :]))
            st_ref[pr] = carry * st + _dot(bdec, p["xpairs"][pr])
    for c in order:
        p = pro[c]
        rows = slice(c * T, (c + 1) * T)
        ys = []
        for pr in range(4):
            ms = []
            for hd in (2 * pr, 2 * pr + 1):
                diff = p["acs_bc"][hd] - p["acs_row"][hd:hd + 1, :]
                lmat = jnp.exp(jnp.where(p["tri"], diff, NEG_INF))
                ms.append((p["cbs"][hd // 4] * lmat * p["dtr"][hd:hd + 1, :]).astype(BF16))
            ys.append(_dot(jnp.concatenate(ms, axis=1), p["xpairs"][pr]) + y_off[c, pr])
        y = jnp.concatenate(ys, axis=1)
        if not rev:
            o_ref[rows, :] = y
        else:
            ytot = y + yf_ref[rows, :] + dsk_ref[...] * p["xs"].astype(F32)
            gated = ytot * _silu(z_ref[rows, :].astype(F32))
            ms = jnp.mean(gated * gated, axis=-1, keepdims=True)
            o_ref[rows, :] = (gated * lax.rsqrt(ms + EPS) * nrm_ref[...]).astype(BF16)


def _ssd(rev, xs, bc, dtc, dtr, p, ncc, nlc, yf=None, z=None):
    nt = xs.shape[0]
    assert ncc % SCAN_CHUNKS == 0 and nlc % SCAN_CHUNKS == 0
    ncc, nlc = ncc // SCAN_CHUNKS, nlc // SCAN_CHUNKS
    rows = SCAN_CHUNKS * T
    cmap = _scan_chunk_map(rev, ncc, nlc)
    blk = lambda w: pl.BlockSpec((rows, w), lambda j: (cmap(j), 0))
    in_specs = [blk(512), blk(256), blk(16), pl.BlockSpec((16, rows), lambda j: (0, cmap(j))),
                _full((1, 16)), _full((16, 1))]
    args = [xs, bc, dtc, dtr, p["alog_row"], p["alog_col"]]
    if rev:
        in_specs += [blk(512), blk(512), _full((1, 512)), _full((1, 512))]
        args += [yf, z, p["d_skip"], p["ssd_norm"]]
    return pl.pallas_call(
        functools.partial(_ssd_kernel, rev),
        grid=(ncc + nlc,),
        in_specs=in_specs,
        out_specs=blk(512),
        out_shape=jax.ShapeDtypeStruct((nt, 512), BF16 if rev else F32),
        scratch_shapes=[pltpu.VMEM((4, 128, 128), F32)],
        compiler_params=_cparams("arbitrary"),
        name="ssd_bwd" if rev else "ssd_fwd",
    )(*args)


def _attn_kernel(ncc, nblk, q_ref, kp_ref, kc_ref, kn_ref, vp_ref, vc_ref, vn_ref, kx_ref, vx_ref,
                 sink_ref, o_ref):
    j = pl.program_id(0)
    c_len = kx_ref.shape[0]
    r = lax.broadcasted_iota(I32, (T, T), 0)
    c = lax.broadcasted_iota(I32, (T, T), 1)
    zero = jnp.zeros((T, T), F32)
    ninf = jnp.full((T, T), NEG_INF, F32)
    lo = lax.broadcasted_iota(I32, (T, 128), 1) < 64
    zero_b = jnp.zeros((T, 128), BF16)
    sink = sink_ref[...]
    kblk = [kp_ref[...], kc_ref[0:T, :], kc_ref[T:2 * T, :], kn_ref[...]]
    vblk = [vp_ref[...], vc_ref[0:T, :], vc_ref[T:2 * T, :], vn_ref[...]]
    stacks = [[hd for hd in range(8) if (hd // 4 + hd % 2) % 2 == b] for b in range(2)]
    s_all, v_all = {}, {}
    for qb in range(2):
        jb = 2 * j + qb
        is_lat = jb >= ncc
        prev_ok = jnp.logical_and(is_lat, jb >= ncc + 1)
        next_ok = jnp.logical_and(is_lat, jb <= nblk - 2)
        bias = jnp.concatenate([
            jnp.where(jnp.logical_and(prev_ok, c >= r), zero, ninf),
            jnp.where(is_lat, zero, ninf),
            jnp.where(jnp.logical_and(next_ok, c <= r), zero, ninf),
            jnp.zeros((T, c_len), F32)], axis=1)
        bias4 = jnp.concatenate([bias] * 4, axis=0)
        k_all = jnp.concatenate(kblk[qb:qb + 3] + [kx_ref[...]], axis=0)
        v_all[qb] = jnp.concatenate(vblk[qb:qb + 3] + [vx_ref[...]], axis=0)
        q = q_ref[qb * T:(qb + 1) * T, :]
        for b in range(2):
            qs = []
            for hd in stacks[b]:
                qp = q[:, 128 * (hd // 2):128 * (hd // 2 + 1)]
                qs.append(jnp.where(lo, zero_b, qp) if hd % 2 else jnp.where(lo, qp, zero_b))
            s_all[qb, b] = _dot(jnp.concatenate(qs, axis=0), k_all[:, 128 * b:128 * (b + 1)], _NT) + bias4
    for qb in range(2):
        outs = {}
        for b in range(2):
            s = s_all[qb, b]
            sk = jnp.concatenate([jnp.broadcast_to(sink[:, hd:hd + 1], (T, 1)) for hd in stacks[b]], axis=0)
            m = jnp.maximum(jnp.max(s, axis=-1, keepdims=True), sk)
            pr = jnp.exp(s - m)
            den = jnp.sum(pr, axis=-1, keepdims=True) + jnp.exp(sk - m)
            o = _dot(pr.astype(BF16), v_all[qb][:, 128 * b:128 * (b + 1)]) / den
            for n, hd in enumerate(stacks[b]):
                outs[hd] = o[T * n:T * (n + 1)]
        for pair in range(4):
            o_ref[qb * T:(qb + 1) * T, 128 * pair:128 * (pair + 1)] = jnp.where(
                lo, outs[2 * pair], outs[2 * pair + 1]).astype(BF16)


def _attention(q, kk, vv, sink, ncc, nlc):
    nt = q.shape[0]
    nblk = ncc + nlc
    assert nblk % 2 == 0 and ncc % 2 == 0
    c_len = ncc * T
    prev = lambda w: pl.BlockSpec((T, w), lambda j: (jnp.maximum(2 * j - 1, 0), 0))
    cur = lambda w: pl.BlockSpec((2 * T, w), lambda j: (j, 0))
    nxt = lambda w: pl.BlockSpec((T, w), lambda j: (jnp.minimum(2 * j + 2, nblk - 1), 0))
    ctx = lambda w: pl.BlockSpec((c_len, w), lambda j: (0, 0))
    return pl.pallas_call(
        functools.partial(_attn_kernel, ncc, nblk),
        grid=(nblk // 2,),
        in_specs=[cur(512), prev(256), cur(256), nxt(256), prev(256), cur(256), nxt(256),
                  ctx(256), ctx(256), _full((1, 8))],
        out_specs=cur(512),
        out_shape=jax.ShapeDtypeStruct((nt, 512), BF16),
        compiler_params=_cparams("arbitrary"),
        name="window_attention",
    )(q, kk, kk, kk, vv, vv, vv, kk, vv, sink)


def _in1_kernel(nct, ntiles, xp_ref, x_ref, xn_ref, mod_ref, g_ref,
                wcat_ref, wg_ref, wgh_ref, cw_ref, cb_ref, gb_ref,
                q_ref, kt_ref, v_ref, o_ref, gc_ref, gr_ref):
    i = pl.program_id(0)
    sh = mod_ref[0, 0:1, :]
    sc = mod_ref[0, 1:2, :]
    g = g_ref[...]
    x_all = jnp.concatenate([xp_ref[...], x_ref[...], xn_ref[...]], axis=0)
    h_all = _norm_mod(x_all, g, sc, sh)
    hb_all = h_all.astype(BF16)
    main_all = _dot(hb_all, wcat_ref[...])
    h, hb, main = h_all[8:8 + TM], hb_all[8:8 + TM], main_all[8:8 + TM]
    prev_ok, next_ok = _seq_edges(i, nct, ntiles)
    x_prev = main_all[7:8, 0:2048] * prev_ok
    x_next = main_all[8 + TM:9 + TM, 0:2048] * next_ok
    act = _conv_silu(main[:, 0:2048], x_prev, x_next, cw_ref[...], cb_ref[...])
    q_ref[...] = act[:, 0:512].astype(BF16)
    kt_ref[...] = jnp.transpose(act[:, 512:1024] * 0.125).astype(BF16)
    v_ref[...] = act[:, 1024:2048].astype(BF16)
    o_ref[...] = main[:, 2048:3072].astype(BF16)
    h_lo = (h - hb.astype(F32)).astype(BF16)
    gates = _hi_lo_cols(_dot(hb, wg_ref[...]), _dot(h_lo, wgh_ref[...]), 32) + gb_ref[...]
    lane = lax.broadcasted_iota(I32, gates.shape, 1)
    gates = jnp.where(lane < 16, gates, _log_sigmoid(gates))
    gc_ref[...] = gates[:, 0:32]
    gr_ref[...] = jnp.transpose(gates)[0:32, :]


def _in1(x1, mod, p, nct):
    nt = x1.shape[0]
    ntiles = nt // TM
    tile = lambda w: pl.BlockSpec((TM, w), lambda i: (i, 0))
    return pl.pallas_call(
        functools.partial(_in1_kernel, nct, ntiles),
        grid=(ntiles,),
        in_specs=_halo_specs(nt, lambda i: i) + [
            _mod_spec(nct), _full((1, D)), _full((D, 3072)), _full((D, 128)), _full((D, 128)),
            _full((3, 2048)), _full((1, 2048)), _full((1, 128))],
        out_specs=[tile(512), pl.BlockSpec((512, TM), lambda i: (0, i)), tile(1024), tile(1024),
                   tile(32), pl.BlockSpec((32, TM), lambda i: (0, i))],
        out_shape=[jax.ShapeDtypeStruct((nt, 512), BF16), jax.ShapeDtypeStruct((512, nt), BF16),
                   jax.ShapeDtypeStruct((nt, 1024), BF16), jax.ShapeDtypeStruct((nt, 1024), BF16),
                   jax.ShapeDtypeStruct((nt, 32), F32), jax.ShapeDtypeStruct((32, nt), F32)],
        compiler_params=_cparams("arbitrary"),
        name="in_proj_odd",
    )(x1, x1, x1, mod, p["g_mix"], p["wcat"], p["wg"], p["wg_hi"], p["conv_w"], p["conv_b"], p["gate_bias"])


def _mlstm_kernel(rev, *refs):
    if rev:
        (q_ref, kt_ref, v_ref, gc_ref, gr_ref, hf_ref, og_ref, hn_ref, o_ref,
         c_ref, mc_ref, mr_ref) = refs
    else:
        q_ref, kt_ref, v_ref, gc_ref, gr_ref, o_ref, c_ref, mc_ref, mr_ref = refs
    j = pl.program_id(0)

    @pl.when(j == 0)
    def _():
        c_ref[...] = jnp.zeros_like(c_ref)
        mc_ref[...] = jnp.zeros_like(mc_ref)
        mr_ref[...] = jnp.zeros_like(mr_ref)

    d = 8 if rev else 0
    end = 0 if rev else T - 1
    order = tuple(reversed(range(SCAN_CHUNKS))) if rev else tuple(range(SCAN_CHUNKS))
    ones_b = jnp.ones((T, 128), BF16)
    sub = lax.broadcasted_iota(I32, (128, T), 0)
    zero_k = jnp.zeros((128, T), BF16)

    def prologue(c):
        rows = slice(c * T, (c + 1) * T)
        ig_col = gc_ref[rows, d:d + 8]
        lf_col = gc_ref[rows, 16 + d:24 + d]
        ig_row = gr_ref[d:d + 8, rows]
        lf_row = gr_ref[16 + d:24 + d, rows]
        tri, b_col, b_row = _cumsums(rev, lf_col, lf_row)
        blast_row = b_col[end:end + 1, :]
        blast_col = b_row[:, end:end + 1]
        wend_row = blast_col - b_row + ig_row
        ac_col = jnp.max(wend_row, axis=1, keepdims=True)
        return dict(tri=tri, b_row=b_row, ig_row=ig_row, blast_row=blast_row, blast_col=blast_col,
                    ac_col=ac_col, eend_row=jnp.exp(wend_row - ac_col),
                    ac_row=jnp.max(blast_row - b_col + ig_col, axis=0, keepdims=True),
                    b_bc=_lane_bcast(b_col), q=q_ref[rows, :])

    pro = {c: prologue(c) for c in order}
    m_col = mc_ref[:, 0:1]
    m_row = mr_ref[0:1, 0:8]
    for c in order:
        p = pro[c]
        mnew_col = jnp.maximum(p["blast_col"] + m_col, p["ac_col"])
        p["sp_col"] = jnp.exp(p["blast_col"] + m_col - mnew_col)
        p["sc_col"] = jnp.exp(p["ac_col"] - mnew_col)
        p["m_row"] = m_row
        m_col, m_row = mnew_col, jnp.maximum(p["blast_row"] + m_row, p["ac_row"])

    def head_matmuls(c, hd):
        p = pro[c]
        rows = slice(c * T, (c + 1) * T)
        pair, hi = hd // 2, hd % 2
        qp = p["q"][:, 128 * pair:128 * (pair + 1)]
        ktp = kt_ref[128 * pair:128 * (pair + 1), rows]
        kth = jnp.where((sub >= 64) if hi else (sub < 64), ktp, zero_k)
        vaug = jnp.concatenate([v_ref[rows, 128 * hd:128 * (hd + 1)], ones_b], axis=1)
        cst = c_ref[hd]
        sqk = _dot(qp, kth)
        inter = _dot(qp, cst.astype(BF16))
        kte = (kth.astype(F32) * p["eend_row"][hd:hd + 1, :]).astype(BF16)
        c_ref[hd] = p["sp_col"][hd:hd + 1, :] * cst + p["sc_col"][hd:hd + 1, :] * _dot(kte, vaug)
        return sqk, inter, vaug

    items = [(c, hd) for c in order for hd in range(8)]
    nxt = head_matmuls(*items[0])
    for n, (c, hd) in enumerate(items):
        p = pro[c]
        rows = slice(c * T, (c + 1) * T)
        cols = slice(128 * hd, 128 * (hd + 1))
        sqk, inter, vaug = nxt
        if n + 1 < len(items):
            nxt = head_matmuls(*items[n + 1])
        bh = p["b_bc"][hd]
        dlog = jnp.where(p["tri"], bh - p["b_row"][hd:hd + 1, :] + p["ig_row"][hd:hd + 1, :], NEG_INF)
        gh = bh + p["m_row"][:, hd:hd + 1]
        mstar = jnp.maximum(gh, jnp.max(dlog, axis=-1, keepdims=True))
        w = (jnp.exp(dlog - mstar) * sqk).astype(BF16)
        intra = _dot(w, vaug)
        e_int = jnp.exp(gh - mstar)
        den = jnp.maximum(jnp.abs(intra[:, 128:256] + e_int * inter[:, 128:256]), jnp.exp(-mstar))
        hh = (intra[:, 0:128] + e_int * inter[:, 0:128]) / den
        if rev:
            hh = hh + hf_ref[rows, cols]
            ms = jnp.mean(hh * hh, axis=-1, keepdims=True)
            hh = hh * lax.rsqrt(ms + EPS) * hn_ref[:, cols]
            og = og_ref[rows, cols].astype(F32)
            o_ref[rows, cols] = (hh * _sigmoid(og)).astype(BF16)
        else:
            o_ref[rows, cols] = hh
    mc_ref[...] = jnp.broadcast_to(m_col, mc_ref.shape)
    mr_ref[...] = jnp.broadcast_to(jnp.concatenate([m_row, jnp.zeros((1, 120), F32)], axis=1), mr_ref.shape)


def _mlstm(rev, q, kt, v, gc, gr, p, ncc, nlc, hf=None, og=None):
    nt = q.shape[0]
    assert ncc % SCAN_CHUNKS == 0 and nlc % SCAN_CHUNKS == 0
    ncc, nlc = ncc // SCAN_CHUNKS, nlc // SCAN_CHUNKS
    rows = SCAN_CHUNKS * T
    cmap = _scan_chunk_map(rev, ncc, nlc)
    blk = lambda w: pl.BlockSpec((rows, w), lambda j: (cmap(j), 0))
    blk_t = lambda h: pl.BlockSpec((h, rows), lambda j: (0, cmap(j)))
    in_specs = [blk(512), blk_t(512), blk(1024), blk(32), blk_t(32)]
    args = [q, kt, v, gc, gr]
    if rev:
        in_specs += [blk(1024), blk(1024), _full((1, 1024))]
        args += [hf, og, p["head_norm"]]
    return pl.pallas_call(
        functools.partial(_mlstm_kernel, rev),
        grid=(ncc + nlc,),
        in_specs=in_specs,
        out_specs=blk(1024),
        out_shape=jax.ShapeDtypeStruct((nt, 1024), BF16 if rev else F32),
        scratch_shapes=[pltpu.VMEM((8, 128, 256), F32), pltpu.VMEM((8, 128), F32), pltpu.VMEM((8, 128), F32)],
        compiler_params=_cparams("arbitrary"),
        name="mlstm_bwd" if rev else "mlstm_fwd",
    )(*args)


def _route(logits_t, rb_col):
    scores = _sigmoid(logits_t)
    biased = scores + rb_col
    row = lambda a, e: a[e:e + 1, :]
    gscore = []
    for g in range(4):
        b0, b1, b2, b3 = (row(biased, 4 * g + e) for e in range(4))
        h1, l1 = jnp.maximum(b0, b1), jnp.minimum(b0, b1)
        h2, l2 = jnp.maximum(b2, b3), jnp.minimum(b2, b3)
        gscore.append(jnp.maximum(h1, h2) + jnp.maximum(jnp.minimum(h1, h2), jnp.maximum(l1, l2)))
    gidx = jnp.zeros_like(gscore[0], dtype=I32)
    best = gscore[0]
    for g in range(1, 4):
        better = gscore[g] > best
        gidx = jnp.where(better, g, gidx)
        best = jnp.where(better, gscore[g], best)

    def pick(a, e):
        out = row(a, e)
        for g in range(1, 4):
            out = jnp.where(gidx == g, row(a, 4 * g + e), out)
        return out

    sb = [pick(biased, e) for e in range(4)]
    i1 = jnp.zeros_like(gidx)
    v1 = sb[0]
    for e in range(1, 4):
        better = sb[e] > v1
        i1 = jnp.where(better, e, i1)
        v1 = jnp.where(better, sb[e], v1)
    i2 = jnp.zeros_like(gidx)
    v2 = jnp.full_like(v1, NEG_INF)
    for e in range(4):
        better = jnp.logical_and(i1 != e, sb[e] > v2)
        i2 = jnp.where(better, e, i2)
        v2 = jnp.where(better, sb[e], v2)
    a = jnp.minimum(i1, i2)
    b = jnp.maximum(i1, i2)
    pair = jnp.where(a == 0, jnp.where(b == 1, 0, jnp.where(b == 2, 2, 3)), jnp.where(a == 1, jnp.where(b == 2, 1, 4), 5))
    return 6 * gidx + pair


def _out_kernel(nmix, nct, two_src, sub, t0, *refs):
    per = nmix + (2 if two_src else 1)
    tile_refs = [refs[per * u:per * (u + 1)] for u in range(sub)]
    (w_ref, mod_ref, g_ref, rw_ref, rwh_ref, rb_ref,
     xmid_ref, hrow_ref, bucket_ref, rank_ref, cnt_ref, cnt_scr) = refs[per * sub:]
    i = pl.program_id(0)

    @pl.when(i == 0)
    def _():
        cnt_scr[...] = jnp.zeros_like(cnt_scr)

    brow = lax.broadcasted_iota(I32, (NB_PAD, TM), 0)
    r = lax.broadcasted_iota(I32, (TM, TM), 0)
    c = lax.broadcasted_iota(I32, (TM, TM), 1)
    before = (r < c).astype(F32).astype(BF16)
    onehots = []
    for u in range(sub):
        mix_refs = tile_refs[u][:nmix]
        is_ctx = sub * i + u + t0 < nct
        mod = jnp.where(is_ctx, mod_ref[1], mod_ref[0])
        mix = mix_refs[0][...] if nmix == 1 else jnp.concatenate([mr[...] for mr in mix_refs], axis=1)
        if two_src:
            x = jnp.where(is_ctx, tile_refs[u][nmix][...], tile_refs[u][nmix + 1][...])
        else:
            x = tile_refs[u][nmix][...]
        x_mid = x + mod[2:3, :] * _dot(mix, w_ref[...])
        xmid_ref[TM * u:TM * (u + 1), :] = x_mid
        h = _norm_mod(x_mid, g_ref[...], mod[4:5, :], mod[3:4, :])
        hb = h.astype(BF16)
        h_lo = (h - hb.astype(F32)).astype(BF16)
        logits = _hi_lo_cols(_dot(hb, rw_ref[...]), _dot(h_lo, rwh_ref[...]), 16)
        logits_t = jnp.transpose(logits)[0:16, :]
        bucket = _route(logits_t, rb_ref[...])
        hrow_u = hrow_ref.at[pl.ds(TM * TOK * u, TM * TOK), :]
        for cblk in range(TOK):
            _tok_store(hrow_u, cblk, h[:, 128 * cblk:128 * (cblk + 1)])
        bucket_ref[u] = bucket
        onehots.append((brow == bucket).astype(F32))
    cnt = cnt_scr[...]
    for u in range(sub):
        onehot = onehots[u]
        cum = _dot(onehot.astype(BF16), before)
        rank_ref[u] = jnp.sum(onehot * (cum + cnt[:, 0:1]), axis=0, keepdims=True).astype(I32)
        cnt = cnt + jnp.sum(onehot, axis=1, keepdims=True)
    cnt_scr[...] = cnt
    cnt_ref[...] = cnt.astype(I32)


def _out_proj(mixes, w_out, xs, mod, g_ffn, rw, rw_hi, rb_col, t0, ntiles, nct):
    nmix = len(mixes)
    two_src = len(xs) == 2
    n = ntiles * TM
    sub = max(s for s in (5, 4, 3, 2, 1) if ntiles % s == 0)
    in_specs, args = [], []
    for u in range(sub):
        tile_of = lambda i, u=u: sub * i + u + t0
        for mx in mixes:
            in_specs.append(pl.BlockSpec((TM, mx.shape[1]), lambda i, f=tile_of: (f(i), 0)))
            args.append(mx)
        if two_src:
            in_specs += [pl.BlockSpec((TM, D), lambda i, f=tile_of: (jnp.minimum(f(i), nct - 1), 0)),
                         pl.BlockSpec((TM, D), lambda i, f=tile_of: (jnp.maximum(f(i) - nct, 0), 0))]
        else:
            in_specs.append(pl.BlockSpec((TM, D), lambda i, f=tile_of: (f(i), 0)))
        args += list(xs)
    in_specs += [_full((D, D)), _full((2, 6, D)), _full((1, D)), _full((D, 128)), _full((D, 128)), _full((16, 1))]
    rows_out = pl.BlockSpec((sub, 1, TM), lambda i: (i, 0, 0))
    return pl.pallas_call(
        functools.partial(_out_kernel, nmix, nct, two_src, sub, t0),
        grid=(ntiles // sub,),
        in_specs=in_specs,
        out_specs=[pl.BlockSpec((sub * TM, D), lambda i: (i, 0)),
                   pl.BlockSpec((sub * TM * TOK, 128), lambda i: (i, 0)), rows_out, rows_out,
                   _full((NB_PAD, 128))],
        out_shape=[jax.ShapeDtypeStruct((n, D), F32), jax.ShapeDtypeStruct((n * TOK, 128), F32),
                   jax.ShapeDtypeStruct((ntiles, 1, TM), I32), jax.ShapeDtypeStruct((ntiles, 1, TM), I32),
                   jax.ShapeDtypeStruct((NB_PAD, 128), I32)],
        scratch_shapes=[pltpu.VMEM((NB_PAD, 128), F32)],
        compiler_params=_cparams("arbitrary"),
        name="out_proj_router",
    )(*args, w_out, mod, g_ffn, rw, rw_hi, rb_col)


def _scatter_kernel(ntiles, pos_ref, flo_ref, fhi_ref, src_ref, dst_ref, hbuf, in_sem, out_sem):
    i = pl.program_id(0)
    slot = i % 3
    h_ref = hbuf.at[slot]
    sem = out_sem.at[slot]
    rows = TM * TOK

    def load(tile, s):
        return pltpu.make_async_copy(src_ref.at[pl.ds(pl.multiple_of(tile * rows, rows), rows), :],
                                     hbuf.at[s], in_sem.at[s])

    def tok(ref, t):
        return ref.at[pl.ds(pl.multiple_of(t * TOK, TOK), TOK), :]

    def copy(r, d_row, src=h_ref, sm=sem):
        return pltpu.make_async_copy(tok(src, r), tok(dst_ref, d_row), sm)

    def wait_rows(lo, hi, unroll, s=slot):
        def body(r, carry):
            copy(0, 0, hbuf.at[s], out_sem.at[s]).wait()
            return carry
        lax.fori_loop(lo, hi, body, 0, unroll=unroll)

    @pl.when(i == 0)
    def _():
        load(0, 0).start()
        if ntiles > 1:
            load(1, 1).start()

    load(i, slot).wait()

    def start(r8, c):
        for k in range(8):
            r = r8 * 8 + k
            copy(r, pos_ref[i * TM + r]).start(priority=k % 2)
        return c
    lax.fori_loop(0, TM // 8, start, 0)

    @pl.when(i > 0)
    def _():
        wait_rows(0, TM, 8, (i + 2) % 3)

    @pl.when(i + 2 < ntiles)
    def _():
        load(i + 2, (i + 2) % 3).start()

    @pl.when(i == ntiles - 1)
    def _():
        wait_rows(0, TM, 8)

        def pad_copies(b, act):
            off, left = flo_ref[b], fhi_ref[b] - flo_ref[b]
            for bit in reversed(range(TMM.bit_length() - 1)):
                k = 1 << bit
                take = (left & k) != 0

                @pl.when(take)
                def _(off=off, k=k):
                    act(pltpu.make_async_copy(
                        h_ref.at[pl.ds(0, k * TOK), :],
                        dst_ref.at[pl.ds(pl.multiple_of(off * TOK, TOK), k * TOK), :], sem))
                off = off + jnp.where(take, k, 0)

        def fill(b, c):
            pad_copies(b, lambda cp: cp.start())
            return c

        def drain(b, c):
            pad_copies(b, lambda cp: cp.wait())
            return c
        lax.fori_loop(0, N_BUCKETS, fill, 0)
        lax.fori_loop(0, N_BUCKETS, drain, 0)

        def tile_copy(j):
            rows = TMM * TOK
            return pltpu.make_async_copy(h_ref, dst_ref.at[pl.ds(pl.multiple_of(j * rows, rows), rows), :], sem)

        def fill_tile(j, c):
            tile_copy(j).start()
            return c

        def wait_tile(j, c):
            tile_copy(j).wait()
            return c
        lax.fori_loop(flo_ref[N_BUCKETS], fhi_ref[N_BUCKETS], fill_tile, 0)
        lax.fori_loop(flo_ref[N_BUCKETS], fhi_ref[N_BUCKETS], wait_tile, 0)


def _scatter_rows(hrow, pos, fill_lo, fill_hi, n, npad):
    assert TM == TMM
    return pl.pallas_call(
        functools.partial(_scatter_kernel, n // TM),
        grid_spec=pltpu.PrefetchScalarGridSpec(
            num_scalar_prefetch=3,
            grid=(n // TM,),
            in_specs=[pl.BlockSpec(memory_space=pl.ANY)],
            out_specs=pl.BlockSpec(memory_space=pl.ANY),
            scratch_shapes=[pltpu.VMEM((3, TM * TOK, 128), hrow.dtype), pltpu.SemaphoreType.DMA((3,)),
                            pltpu.SemaphoreType.DMA((3,))],
        ),
        out_shape=jax.ShapeDtypeStruct((npad * TOK, 128), hrow.dtype),
        compiler_params=_cparams("arbitrary"),
        name="moe_scatter_rows",
    )(pos, fill_lo, fill_hi, hrow)


def _combine_kernel(ntiles, pos_ref, x_ref, mod_ref, ys_ref, o_ref, ybuf, sem):
    i = pl.program_id(0)

    def copy(tile, slot, r):
        src = pl.multiple_of(pos_ref[tile * TM + r] * TOK, TOK)
        dst = pl.multiple_of(r * TOK, TOK)
        return pltpu.make_async_copy(ys_ref.at[pl.ds(src, TOK), :], ybuf.at[slot, pl.ds(dst, TOK), :], sem.at[slot])

    def start_tile(tile, slot):
        def body(r8, carry):
            for k in range(8):
                copy(tile, slot, r8 * 8 + k).start(priority=k % 2)
            return carry
        lax.fori_loop(0, TM // 8, body, 0)

    @pl.when(i == 0)
    def _():
        start_tile(0, 0)

    @pl.when(i + 1 < ntiles)
    def _():
        start_tile(i + 1, (i + 1) % 2)

    slot = i % 2

    def wait_body(r, carry):
        copy(i, slot, 0).wait()
        return carry
    lax.fori_loop(0, TM, wait_body, 0, unroll=8)
    o_ref[...] = x_ref[...] + mod_ref[0, 5:6, :] * _tok_rows(ybuf.at[slot], TM)


def _combine(x_mid, mod, ys, pos, t0, nct):
    n = x_mid.shape[0]
    ntiles = n // TM
    return pl.pallas_call(
        functools.partial(_combine_kernel, ntiles),
        grid_spec=pltpu.PrefetchScalarGridSpec(
            num_scalar_prefetch=1,
            grid=(ntiles,),
            in_specs=[pl.BlockSpec((TM, D), lambda i, *_: (i, 0)),
                      pl.BlockSpec((1, 6, D), lambda i, *_: (jnp.where(i + t0 < nct, 1, 0), 0, 0)),
                      pl.BlockSpec(memory_space=pl.ANY)],
            out_specs=pl.BlockSpec((TM, D), lambda i, *_: (i, 0)),
            scratch_shapes=[pltpu.VMEM((2, TM * TOK, 128), F32), pltpu.SemaphoreType.DMA((2,))],
        ),
        out_shape=jax.ShapeDtypeStruct((n, D), F32),
        compiler_params=_cparams("arbitrary"),
        name="moe_combine",
    )(pos, x_mid, mod, ys)


def _moe_kernel(tea_ref, teb_ref, tblk_ref, tval_ref, x_ref, rw_ref, w1a_ref, w3a_ref, w2a_ref,
                w1b_ref, w3b_ref, w2b_ref, y_ref):
    del tblk_ref
    j = pl.program_id(0)

    @pl.when(tval_ref[j] != 0)
    def _():
        h = _tok_rows(x_ref, TMM)
        hb = h.astype(BF16)
        logits = _hi_lo_cols(_dot(hb, rw_ref[...]), 0.0, 16)
        scores = _sigmoid(logits)
        lane = lax.broadcasted_iota(I32, scores.shape, 1)
        s_a = jnp.sum(jnp.where(lane == tea_ref[j], scores, 0.0), axis=1, keepdims=True)
        s_b = jnp.sum(jnp.where(lane == teb_ref[j], scores, 0.0), axis=1, keepdims=True)
        gates = (s_a / (s_a + s_b), s_b / (s_a + s_b))
        acts = []
        for w1_ref, w3_ref, gate in ((w1a_ref, w3a_ref, gates[0]), (w1b_ref, w3b_ref, gates[1])):
            u = _dot(hb, w1_ref[0, 0].astype(BF16))
            v = _dot(hb, w3_ref[0, 0].astype(BF16))
            acts.append((_silu(u) * v * gate).astype(BF16))
        y = _dot(acts[0], w2a_ref[0, 0].astype(BF16)) + _dot(acts[1], w2b_ref[0, 0].astype(BF16))
        for cblk in range(TOK):
            _tok_store(y_ref, cblk, y[:, 128 * cblk:128 * (cblk + 1)])

    @pl.when(tval_ref[j] == 0)
    def _():
        y_ref[...] = jnp.zeros_like(y_ref)


def _moe(xs_sorted, rw, w1, w3, w2, layer, tile_ea, tile_eb, tile_blk, tile_valid):
    npad = xs_sorted.shape[0] // TOK
    ntile = npad // TMM
    wspec = lambda shape, which: pl.BlockSpec(
        (1, 1) + shape, lambda j, ea, eb, blk, val: (layer, (ea, eb)[which][j], 0, 0))
    up, down = (D, D_EXPERT), (D_EXPERT, D)
    return pl.pallas_call(
        _moe_kernel,
        grid_spec=pltpu.PrefetchScalarGridSpec(
            num_scalar_prefetch=4,
            grid=(ntile,),
            in_specs=[pl.BlockSpec((TMM * TOK, 128), lambda j, ea, eb, blk, val: (blk[j], 0)),
                      pl.BlockSpec((D, 128), lambda j, *_: (0, 0)),
                      wspec(up, 0), wspec(up, 0), wspec(down, 0), wspec(up, 1), wspec(up, 1), wspec(down, 1)],
            out_specs=pl.BlockSpec((TMM * TOK, 128), lambda j, ea, eb, blk, val: (j, 0)),
        ),
        out_shape=jax.ShapeDtypeStruct((npad * TOK, 128), F32),
        compiler_params=_cparams("arbitrary"),
        name="moe_experts",
    )(tile_ea, tile_eb, tile_blk, tile_valid, xs_sorted, rw, w1, w3, w2, w1, w3, w2)


def _moe_block(x_mid, hrow, bucket, rank, counts, mod, rw, rw_hi, w1, w3, w2, layer, t0, nct):
    n = hrow.shape[0] // TOK
    ntile = n // TMM + N_BUCKETS
    npad = ntile * TMM
    cnt = counts[:N_BUCKETS, 0]
    padded = ((cnt + TMM - 1) // TMM) * TMM
    ends = jnp.cumsum(padded)
    starts = ends - padded
    total_tiles = ends[-1] // TMM
    tiles = jnp.arange(ntile, dtype=I32)
    tile_valid = (tiles < total_tiles).astype(I32)
    tile_blk = jnp.minimum(tiles, jnp.maximum(total_tiles - 1, 0))
    tile_bucket = jnp.minimum(jnp.sum((ends[None, :] <= (tile_blk * TMM)[:, None]).astype(I32), axis=1), N_BUCKETS - 1)
    pair = tile_bucket % 6
    grp = tile_bucket // 6
    slot_a = sum(jnp.where(pair == k, e, 0) for k, (e, _) in enumerate(_PAIR_SLOTS))
    slot_b = sum(jnp.where(pair == k, e, 0) for k, (_, e) in enumerate(_PAIR_SLOTS))
    tile_ea = (4 * grp + slot_a).astype(I32)
    tile_eb = (4 * grp + slot_b).astype(I32)
    bucket = bucket.reshape(-1)
    onehot = (bucket[:, None] == jnp.arange(N_BUCKETS, dtype=I32)[None, :]).astype(I32)
    pos = (rank.reshape(-1) + jnp.sum(onehot * starts[None, :], axis=1)).astype(I32)
    pad32 = lambda a, tail: jnp.zeros((NB_PAD,), I32).at[:N_BUCKETS].set(a.astype(I32)).at[N_BUCKETS].set(tail)
    fill_lo, fill_hi = pad32(starts + cnt, total_tiles), pad32(ends, ntile)
    xs_sorted = _scatter_rows(hrow, pos, fill_lo, fill_hi, n, npad)
    ys = _moe(xs_sorted, rw, w1, w3, w2, layer, tile_ea, tile_eb, tile_blk, tile_valid)
    return _combine(x_mid, mod, ys, pos, t0, nct)


def kernel(x, c, ctx, c_ctx, router_w, router_b, norm_mix, norm_ffn, w_mod, b_mod, ev_w_in, ev_conv_w, ev_conv_b, ev_dt_bias, ev_a_log, ev_d_skip, ev_ssd_norm, ev_q_norm, ev_k_norm, ev_sink, ev_w_out, od_w_in, od_conv_w, od_conv_b, od_igate_b, od_fgate_b, od_head_norm, od_w_out, moe_w1, moe_w3, moe_w2):
    s_len = x.shape[1]
    c_len = ctx.shape[1]
    assert x.shape[0] == 1 and s_len % TM == 0 and c_len % TM == 0 and s_len % GRID_W == 0
    nct = c_len // TM
    ncc, nlc = c_len // T, s_len // T
    nt = c_len + s_len
    ntiles = nt // TM

    mod = _modulation(c, c_ctx, w_mod, b_mod)
    rw, rw_hi = _hi_lo_weight(router_w)
    rb_col = router_b.reshape(N_EXPERTS, 1)
    pad128 = lambda v: jnp.zeros((1, 128), F32).at[0, :v.shape[0]].set(v)

    w = ev_w_in[0]
    rope_rows, rope_cols = _rope_tables(s_len)
    wdt, wdt_hi = _hi_lo_weight(w[:, 1280:1296])
    p0 = dict(
        g_mix=norm_mix[0].reshape(1, D),
        w_zx=w[:, 0:1280].astype(BF16), w_qkv=w[:, 1296:2064].astype(BF16), wdt=wdt, wdt_hi=wdt_hi,
        conv_w=ev_conv_w[0], conv_b=ev_conv_b[0].reshape(1, 768),
        dt_bias=pad128(ev_dt_bias[0].reshape(16)),
        q_norm=jnp.tile(ev_q_norm[0], 8).reshape(1, 512), k_norm=jnp.tile(ev_k_norm[0], 2).reshape(1, 128),
        rope_rows=rope_rows, rope_cols=rope_cols,
        alog_row=ev_a_log[0].reshape(1, 16), alog_col=ev_a_log[0].reshape(16, 1),
        d_skip=jnp.repeat(ev_d_skip[0], 64).reshape(1, 512), ssd_norm=ev_ssd_norm[0].reshape(1, 512))
    z, xs, bc, q, kk, vv, dtc, dtr = _in0(ctx[0], x[0], mod[0], p0, nct)
    yf = _ssd(False, xs, bc, dtc, dtr, p0, ncc, nlc)
    ymix = _ssd(True, xs, bc, dtc, dtr, p0, ncc, nlc, yf=yf, z=z)
    att = _attention(q, kk, vv, ev_sink[0].reshape(1, 8), ncc, nlc)
    x_mid0, hrow, bucket, rank, counts = _out_proj(
        [ymix, att], ev_w_out[0].astype(BF16), [ctx[0], x[0]], mod[0], norm_ffn[0].reshape(1, D),
        rw, rw_hi, rb_col, 0, ntiles, nct)
    x1 = _moe_block(x_mid0, hrow, bucket, rank, counts, mod[0], rw, rw_hi, moe_w1, moe_w3, moe_w2, 0, 0, nct)

    w = od_w_in[0]
    wg, wg_hi = _hi_lo_weight(w[:, 3072:3104])
    p1 = dict(
        g_mix=norm_mix[1].reshape(1, D),
        wcat=w[:, 0:3072].astype(BF16), wg=wg, wg_hi=wg_hi,
        conv_w=od_conv_w[0], conv_b=od_conv_b[0].reshape(1, 2048),
        gate_bias=pad128(jnp.concatenate([od_igate_b[0].reshape(16), od_fgate_b[0].reshape(16)])),
        head_norm=od_head_norm[0].reshape(1, 1024))
    q1, kt1, v1, og1, gc1, gr1 = _in1(x1, mod[1], p1, nct)
    hf = _mlstm(False, q1, kt1, v1, gc1, gr1, p1, ncc, nlc)
    hmix = _mlstm(True, q1, kt1, v1, gc1, gr1, p1, ncc, nlc, hf=hf, og=og1)
    x_mid1, hrow, bucket, rank, counts = _out_proj(
        [hmix], od_w_out[0].astype(BF16), [x1], mod[1], norm_ffn[1].reshape(1, D),
        rw, rw_hi, rb_col, nct, ntiles - nct, nct)
    return _moe_block(x_mid1, hrow, bucket, rank, counts, mod[1], rw, rw_hi, moe_w1, moe_w3, moe_w2, 1, nct,
                      nct)[None]
```

```python
import functools
import math

import jax
import jax.numpy as jnp
from jax import lax
from jax.experimental import pallas as pl
from jax.experimental.pallas import tpu as pltpu

F32 = jnp.float32
BF16 = jnp.bfloat16
I32 = jnp.int32

EPS = 1e-6
D = 1024
T = 128
SCAN_CHUNKS = 2
TM = 256
TMM = 256
GRID_W = 64
ROPE_THETA = 10000.0
N_EXPERTS = 16
N_BUCKETS = 24
NB_PAD = 32
D_EXPERT = 512
TOK = 8
U32 = jnp.uint32
NEG_INF = float("-inf")
VMEM_LIMIT = 56 * 1024 * 1024

_NN = (((1,), (0,)), ((), ()))
_NT = (((1,), (1,)), ((), ()))
_TN = (((0,), (0,)), ((), ()))

_PAIR_SLOTS = ((0, 1), (2, 1), (2, 0), (3, 0), (3, 1), (3, 2))


def _cparams(*sem):
    return pltpu.CompilerParams(dimension_semantics=sem, vmem_limit_bytes=VMEM_LIMIT)


def _dot(a, b, dims=_NN):
    return lax.dot_general(a, b, dims, preferred_element_type=F32)


def _split(a, n):
    out = []
    r = a
    for _ in range(n):
        t = r.astype(BF16)
        out.append(t)
        r = r - t.astype(F32)
    return out


def _mdot(as_, bs, dims=_NN, order=None):
    if order is None:
        order = len(as_) + len(bs) - 2
    acc = None
    for i, a in enumerate(as_):
        for j, b in enumerate(bs):
            if i + j <= order:
                p = _dot(a, b, dims)
                acc = p if acc is None else acc + p
    return acc


def _sigmoid(x):
    return 1.0 / (1.0 + jnp.exp(-x))


def _silu(x):
    hx = 0.5 * x
    return hx * jnp.tanh(hx) + hx


def _log1p_exp_neg_abs(x):
    e = jnp.exp(-jnp.abs(x))
    u = 1.0 + e
    um1 = u - 1.0
    return jnp.where(um1 == 0.0, e, jnp.log(u) * (e / jnp.where(um1 == 0.0, 1.0, um1)))


def _softplus(x):
    return jnp.maximum(x, 0.0) + _log1p_exp_neg_abs(x)


def _log_sigmoid(x):
    return jnp.minimum(x, 0.0) - _log1p_exp_neg_abs(x)


def _norm_mod(x, g, sc, sh):
    ms = jnp.mean(x * x, axis=-1, keepdims=True)
    return (x * lax.rsqrt(ms + EPS)) * g * (1.0 + sc) + sh


def _tok_load(ref, chunk, n):
    return ref[pl.ds(chunk, n, stride=TOK), :]


def _tok_store(ref, chunk, val):
    ref[pl.ds(chunk, val.shape[0], stride=TOK), :] = val


def _tok_rows(ref, n):
    return jnp.concatenate([_tok_load(ref, c, n) for c in range(TOK)], axis=1)


def _tri(rev):
    r = lax.broadcasted_iota(I32, (T, T), 0)
    c = lax.broadcasted_iota(I32, (T, T), 1)
    return (c >= r) if rev else (c <= r)


def _cumsums(rev, col, row):
    tri = _tri(rev)
    tri_b = tri.astype(F32).astype(BF16)
    trit_b = _tri(not rev).astype(F32).astype(BF16)
    ccol = _mdot([tri_b], _split(col, 3))
    crow = _mdot(_split(row, 3), [trit_b])
    return tri, ccol, crow


def _lane_bcast(col):
    return [jnp.broadcast_to(col[:, h:h + 1], (col.shape[0], 128)) for h in range(col.shape[1])]


def _mod_kernel(c_ref, w_ref, b_ref, o_ref):
    a = _silu(c_ref[...])
    o_ref[0] = _mdot(_split(a, 2), _split(w_ref[0], 2), order=1) + b_ref[0]


def _modulation(c, c_ctx, w_mod, b_mod):
    depth = w_mod.shape[0]
    n = w_mod.shape[2]
    tn = 1536
    cc = jnp.zeros((8, D), F32).at[0].set(c[0]).at[1].set(c_ctx)
    out = pl.pallas_call(
        _mod_kernel,
        grid=(depth, n // tn),
        in_specs=[
            pl.BlockSpec((8, D), lambda l, j: (0, 0)),
            pl.BlockSpec((1, D, tn), lambda l, j: (l, 0, j)),
            pl.BlockSpec((1, 1, tn), lambda l, j: (l, 0, j)),
        ],
        out_specs=pl.BlockSpec((1, 8, tn), lambda l, j: (l, 0, j)),
        out_shape=jax.ShapeDtypeStruct((depth, 8, n), F32),
        compiler_params=_cparams("arbitrary", "arbitrary"),
        name="modulation",
    )(cc, w_mod, b_mod.reshape(depth, 1, n))
    return out[:, :2].reshape(depth, 2, 6, D)


def _halo_specs(nrows, tile_of):
    nb8 = nrows // 8
    return [
        pl.BlockSpec((8, D), lambda i: (jnp.maximum(tile_of(i) * (TM // 8) - 1, 0), 0)),
        pl.BlockSpec((TM, D), lambda i: (tile_of(i), 0)),
        pl.BlockSpec((8, D), lambda i: (jnp.minimum((tile_of(i) + 1) * (TM // 8), nb8 - 1), 0)),
    ]


def _ctx_tile(nct):
    return lambda i: jnp.minimum(i, nct - 1)


def _lat_tile(nct):
    return lambda i: jnp.maximum(i - nct, 0)


def _mod_spec(nct, t0=0):
    return pl.BlockSpec((1, 6, D), lambda i: (jnp.where(i + t0 < nct, 1, 0), 0, 0))


def _full(shape):
    nd = len(shape)
    return pl.BlockSpec(shape, lambda i: (0,) * nd)


def _seq_edges(i, nct, ntiles):
    prev_ok = jnp.logical_and(i != 0, i != nct).astype(F32)
    next_ok = jnp.logical_and(i != nct - 1, i != ntiles - 1).astype(F32)
    return prev_ok, next_ok


def _conv_silu(x, x_first_prev, x_last_next, cw, cb):
    n = x.shape[0]
    rows = lax.broadcasted_iota(I32, x.shape, 0)
    x_prev = jnp.where(rows == 0, x_first_prev, pltpu.roll(x, 1, 0))
    x_next = jnp.where(rows == n - 1, x_last_next, pltpu.roll(x, n - 1, 0))
    return _silu(x_prev * cw[0:1] + x * cw[1:2] + x_next * cw[2:3] + cb)


def _hi_lo_cols(blk, lo_pass, n):
    return blk + pltpu.roll(blk, 128 - n, 1) + lo_pass


def _hi_lo_weight(w):
    n = w.shape[1]
    hi = w.astype(BF16)
    lo = (w - hi.astype(F32)).astype(BF16)
    z = jnp.zeros((w.shape[0], 128 - 2 * n), BF16)
    return jnp.concatenate([hi, lo, z], axis=1), jnp.concatenate([hi, jnp.zeros_like(lo), z], axis=1)


def _head_rms(xf, gamma):
    r = lax.broadcasted_iota(I32, (128, 128), 0) // 64
    c = lax.broadcasted_iota(I32, (128, 128), 1) // 64
    ones_bd = (r == c).astype(F32).astype(BF16)
    outs = []
    for j in range(xf.shape[1] // 128):
        blk = xf[:, 128 * j:128 * (j + 1)]
        ssum = _dot((blk * blk).astype(BF16), ones_bd)
        outs.append(blk * lax.rsqrt(ssum * (1.0 / 64.0) + EPS))
    return jnp.concatenate(outs, axis=1) * gamma


def _rope(xf, cos, sin):
    lane = lax.broadcasted_iota(I32, (xf.shape[0], 128), 1)
    first = (lane % 32) < 16
    outs = []
    for j in range(xf.shape[1] // 128):
        blk = xf[:, 128 * j:128 * (j + 1)]
        partner = jnp.where(first, pltpu.roll(blk, 112, 1), pltpu.roll(blk, 16, 1))
        outs.append(blk * cos + partner * sin)
    return jnp.concatenate(outs, axis=1)


def _rope_tables(s_len):
    lane = jnp.arange(128, dtype=I32)
    inv = ROPE_THETA ** (-(lane % 16).astype(F32) / 16.0)
    sign = jnp.where((lane % 32) < 16, -1.0, 1.0).astype(F32)
    ang_r = jnp.arange(s_len // GRID_W, dtype=F32)[:, None] * inv[None, :]
    ang_c = jnp.tile(jnp.arange(GRID_W, dtype=F32), TM // GRID_W)[:, None] * inv[None, :]
    both = lambda ang: jnp.stack([jnp.cos(ang), jnp.sin(ang) * sign[None, :]])
    return both(ang_r), both(ang_c)


def _in0_kernel(nct, ntiles, cp_ref, c_ref, cn_ref, xp_ref, x_ref, xn_ref, mod_ref, g_ref, wa_ref, wb_ref,
                wdt_ref, wdth_ref, cw_ref, cb_ref, dtb_ref, qn_ref, kn_ref, rowcs_ref, colcs_ref,
                z_ref, xs_ref, bc_ref, q_ref, kk_ref, vv_ref, dtc_ref, dtr_ref):
    i = pl.program_id(0)
    is_ctx = i < nct
    sh = mod_ref[0, 0:1, :]
    sc = mod_ref[0, 1:2, :]
    g = g_ref[...]
    x_all = jnp.concatenate([jnp.where(is_ctx, cp_ref[...], xp_ref[...]),
                             jnp.where(is_ctx, c_ref[...], x_ref[...]),
                             jnp.where(is_ctx, cn_ref[...], xn_ref[...])], axis=0)
    h_all = _norm_mod(x_all, g, sc, sh)
    hb_all = h_all.astype(BF16)
    zx_all = _dot(hb_all, wa_ref[...])
    h, hb, zx = h_all[8:8 + TM], hb_all[8:8 + TM], zx_all[8:8 + TM]
    qkv = _dot(hb, wb_ref[...])
    prev_ok, next_ok = _seq_edges(i, nct, ntiles)
    xb_prev = zx_all[7:8, 512:1280] * prev_ok
    xb_next = zx_all[8 + TM:9 + TM, 512:1280] * next_ok
    act = _conv_silu(zx[:, 512:1280], xb_prev, xb_next, cw_ref[...], cb_ref[...])
    z_ref[...] = zx[:, 0:512].astype(BF16)
    xs_ref[...] = act[:, 0:512].astype(BF16)
    bc_ref[...] = act[:, 512:768].astype(BF16)
    row0 = jnp.maximum(i - nct, 0) * (TM // GRID_W)
    rowcs = [jnp.concatenate([jnp.broadcast_to(rowcs_ref[t, pl.ds(row0 + kq, 1), :], (GRID_W, 128))
                              for kq in range(TM // GRID_W)], axis=0) for t in range(2)]
    row_lanes = (lax.broadcasted_iota(I32, (TM, 128), 1) % 64) < 32
    cos = jnp.where(is_ctx, 1.0, jnp.where(row_lanes, rowcs[0], colcs_ref[0]))
    sin = jnp.where(is_ctx, 0.0, jnp.where(row_lanes, rowcs[1], colcs_ref[1]))
    q = _rope(_head_rms(qkv[:, 0:512], qn_ref[...]), cos, sin) * 0.125
    q_ref[...] = q.astype(BF16)
    k = _rope(_head_rms(qkv[:, 512:640], kn_ref[...]), cos, sin)
    kk_ref[...] = jnp.concatenate([k, pltpu.roll(k, 64, 1)], axis=1).astype(BF16)
    v = qkv[:, 640:768]
    vv_ref[...] = jnp.concatenate([v, pltpu.roll(v, 64, 1)], axis=1).astype(BF16)
    h_lo = (h - hb.astype(F32)).astype(BF16)
    dt = _softplus(_hi_lo_cols(_dot(hb, wdt_ref[...]), _dot(h_lo, wdth_ref[...]), 16) + dtb_ref[...])
    dtc_ref[...] = dt[:, 0:16]
    dtr_ref[...] = jnp.transpose(dt)[0:16, :]


def _in0(ctx2, x2, mod, p, nct):
    c_len, s_len = ctx2.shape[0], x2.shape[0]
    nt = c_len + s_len
    ntiles = nt // TM
    tile = lambda w: pl.BlockSpec((TM, w), lambda i: (i, 0))
    lat = _lat_tile(nct)
    lat_tile = lambda w: pl.BlockSpec((TM, w), lambda i: (lat(i), 0))
    outs = [(512, BF16), (512, BF16), (256, BF16), (512, BF16), (256, BF16), (256, BF16), (16, F32)]
    return pl.pallas_call(
        functools.partial(_in0_kernel, nct, ntiles),
        grid=(ntiles,),
        in_specs=_halo_specs(c_len, _ctx_tile(nct)) + _halo_specs(s_len, lat) + [
            _mod_spec(nct), _full((1, D)), _full((D, 1280)), _full((D, 768)), _full((D, 128)), _full((D, 128)),
            _full((3, 768)), _full((1, 768)), _full((1, 128)),
            _full((1, 512)), _full((1, 128)), _full((2, s_len // GRID_W, 128)), _full((2, TM, 128))],
        out_specs=[tile(w) for w, _ in outs] + [pl.BlockSpec((16, TM), lambda i: (0, i))],
        out_shape=[jax.ShapeDtypeStruct((nt, w), dt) for w, dt in outs]
        + [jax.ShapeDtypeStruct((16, nt), F32)],
        compiler_params=_cparams("arbitrary"),
        name="in_proj_even",
    )(ctx2, ctx2, ctx2, x2, x2, x2, mod, p["g_mix"], p["w_zx"], p["w_qkv"], p["wdt"], p["wdt_hi"],
      p["conv_w"], p["conv_b"], p["dt_bias"], p["q_norm"], p["k_norm"], p["rope_rows"], p["rope_cols"])


def _scan_chunk_map(rev, ncc, nlc):
    if not rev:
        return lambda j: j
    return lambda j: jnp.where(j < ncc, ncc - 1 - j, ncc + nlc - 1 - (j - ncc))


def _ssd_kernel(rev, *refs):
    if rev:
        (xs_ref, bc_ref, dtc_ref, dtr_ref, alr_ref, alc_ref, yf_ref, z_ref, dsk_ref, nrm_ref,
         o_ref, st_ref) = refs
    else:
        xs_ref, bc_ref, dtc_ref, dtr_ref, alr_ref, alc_ref, o_ref, st_ref = refs
    j = pl.program_id(0)

    @pl.when(j == 0)
    def _():
        st_ref[...] = jnp.zeros_like(st_ref)

    d = 8 if rev else 0
    a_coef_row = -jnp.exp(alr_ref[...])[:, d:d + 8]
    a_coef_col = -jnp.exp(alc_ref[...])[d:d + 8, :]
    lane = lax.broadcasted_iota(I32, (T, 128), 1)
    lo = lane < 64
    zero_b = jnp.zeros((T, 128), BF16)
    hi_half = jnp.logical_not(lo)
    sub = lax.broadcasted_iota(I32, (128, T), 0)
    eye = (sub == lax.broadcasted_iota(I32, (128, T), 1)).astype(F32).astype(BF16)
    end = 0 if rev else T - 1
    order = tuple(reversed(range(SCAN_CHUNKS))) if rev else tuple(range(SCAN_CHUNKS))

    def prologue(c):
        rows = slice(c * T, (c + 1) * T)
        dtc = dtc_ref[rows, d:d + 8]
        dtr = dtr_ref[d:d + 8, rows]
        tri, acs_col, acs_row = _cumsums(rev, dtc * a_coef_row, dtr * a_coef_col)
        atot_col = acs_row[:, end:end + 1]
        acs_bc = _lane_bcast(acs_col)
        dec_row = jnp.exp(atot_col - acs_row) * dtr
        xs = xs_ref[rows, :]
        bm = bc_ref[rows, 0:128]
        cm = bc_ref[rows, 128:256]
        cgs = [jnp.where(lo, cm, zero_b), jnp.where(hi_half, cm, zero_b)]
        cbs = [_dot(cgs[g], bm, _NT) for g in range(2)]
        bmt = _dot(eye, bm, _NT)
        bgts = [jnp.where(sub < 64, bmt, 0.0), jnp.where(sub >= 64, bmt, 0.0)]
        xpairs = []
        for pr in range(4):
            xp = xs[:, 128 * pr:128 * (pr + 1)]
            xpairs.append(jnp.concatenate([jnp.where(lo, xp, zero_b), jnp.where(hi_half, xp, zero_b)], axis=0))
        return dict(tri=tri, acs_row=acs_row, acs_bc=acs_bc, atot_col=atot_col, dec_row=dec_row, dtr=dtr,
                    xs=xs, cgs=cgs, cbs=cbs, bgts=bgts, xpairs=xpairs)

    pro = {c: prologue(c) for c in order}
    y_off = {}
    for c in order:
        p = pro[c]
        for pr in range(4):
            g, h0, h1 = pr // 2, 2 * pr, 2 * pr + 1
            st = st_ref[pr]
            eacs = jnp.exp(jnp.where(lo, p["acs_bc"][h0], p["acs_bc"][h1]))
            y_off[c, pr] = _dot(p["cgs"][g], st.astype(BF16)) * eacs
            bdec = jnp.concatenate([(p["bgts"][g] * p["dec_row"][h0:h0 + 1, :]).astype(BF16),
                                    (p["bgts"][g] * p["dec_row"][h1:h1 + 1, :]).astype(BF16)], axis=1)
            carry = jnp.where(lo[0:1, :], jnp.exp(p["atot_col"][h0:h0 + 1, :]), jnp.exp(p["atot_col"][h1:h1 + 1, :]))
            st_ref[pr] = carry * st + _dot(bdec, p["xpairs"][pr])
    for c in order:
        p = pro[c]
        rows = slice(c * T, (c + 1) * T)
        ys = []
        for pr in range(4):
            ms = []
            for hd in (2 * pr, 2 * pr + 1):
                diff = p["acs_bc"][hd] - p["acs_row"][hd:hd + 1, :]
                lmat = jnp.exp(jnp.where(p["tri"], diff, NEG_INF))
                ms.append((p["cbs"][hd // 4] * lmat * p["dtr"][hd:hd + 1, :]).astype(BF16))
            ys.append(_dot(jnp.concatenate(ms, axis=1), p["xpairs"][pr]) + y_off[c, pr])
        y = jnp.concatenate(ys, axis=1)
        if not rev:
            o_ref[rows, :] = y
        else:
            ytot = y + yf_ref[rows, :] + dsk_ref[...] * p["xs"].astype(F32)
            gated = ytot * _silu(z_ref[rows, :].astype(F32))
            ms = jnp.mean(gated * gated, axis=-1, keepdims=True)
            o_ref[rows, :] = (gated * lax.rsqrt(ms + EPS) * nrm_ref[...]).astype(BF16)


def _ssd(rev, xs, bc, dtc, dtr, p, ncc, nlc, yf=None, z=None):
    nt = xs.shape[0]
    assert ncc % SCAN_CHUNKS == 0 and nlc % SCAN_CHUNKS == 0
    ncc, nlc = ncc // SCAN_CHUNKS, nlc // SCAN_CHUNKS
    rows = SCAN_CHUNKS * T
    cmap = _scan_chunk_map(rev, ncc, nlc)
    blk = lambda w: pl.BlockSpec((rows, w), lambda j: (cmap(j), 0))
    in_specs = [blk(512), blk(256), blk(16), pl.BlockSpec((16, rows), lambda j: (0, cmap(j))),
                _full((1, 16)), _full((16, 1))]
    args = [xs, bc, dtc, dtr, p["alog_row"], p["alog_col"]]
    if rev:
        in_specs += [blk(512), blk(512), _full((1, 512)), _full((1, 512))]
        args += [yf, z, p["d_skip"], p["ssd_norm"]]
    return pl.pallas_call(
        functools.partial(_ssd_kernel, rev),
        grid=(ncc + nlc,),
        in_specs=in_specs,
        out_specs=blk(512),
        out_shape=jax.ShapeDtypeStruct((nt, 512), BF16 if rev else F32),
        scratch_shapes=[pltpu.VMEM((4, 128, 128), F32)],
        compiler_params=_cparams("arbitrary"),
        name="ssd_bwd" if rev else "ssd_fwd",
    )(*args)


def _attn_kernel(ncc, nblk, q_ref, kp_ref, kc_ref, kn_ref, vp_ref, vc_ref, vn_ref, kx_ref, vx_ref,
                 sink_ref, o_ref):
    j = pl.program_id(0)
    c_len = kx_ref.shape[0]
    r = lax.broadcasted_iota(I32, (T, T), 0)
    c = lax.broadcasted_iota(I32, (T, T), 1)
    zero = jnp.zeros((T, T), F32)
    ninf = jnp.full((T, T), NEG_INF, F32)
    lo = lax.broadcasted_iota(I32, (T, 128), 1) < 64
    zero_b = jnp.zeros((T, 128), BF16)
    sink = sink_ref[...]
    kblk = [kp_ref[...], kc_ref[0:T, :], kc_ref[T:2 * T, :], kn_ref[...]]
    vblk = [vp_ref[...], vc_ref[0:T, :], vc_ref[T:2 * T, :], vn_ref[...]]
    stacks = [[hd for hd in range(8) if (hd // 4 + hd % 2) % 2 == b] for b in range(2)]
    s_all, v_all = {}, {}
    for qb in range(2):
        jb = 2 * j + qb
        is_lat = jb >= ncc
        prev_ok = jnp.logical_and(is_lat, jb >= ncc + 1)
        next_ok = jnp.logical_and(is_lat, jb <= nblk - 2)
        bias = jnp.concatenate([
            jnp.where(jnp.logical_and(prev_ok, c >= r), zero, ninf),
            jnp.where(is_lat, zero, ninf),
            jnp.where(jnp.logical_and(next_ok, c <= r), zero, ninf),
            jnp.zeros((T, c_len), F32)], axis=1)
        bias4 = jnp.concatenate([bias] * 4, axis=0)
        k_all = jnp.concatenate(kblk[qb:qb + 3] + [kx_ref[...]], axis=0)
        v_all[qb] = jnp.concatenate(vblk[qb:qb + 3] + [vx_ref[...]], axis=0)
        q = q_ref[qb * T:(qb + 1) * T, :]
        for b in range(2):
            qs = []
            for hd in stacks[b]:
                qp = q[:, 128 * (hd // 2):128 * (hd // 2 + 1)]
                qs.append(jnp.where(lo, zero_b, qp) if hd % 2 else jnp.where(lo, qp, zero_b))
            s_all[qb, b] = _dot(jnp.concatenate(qs, axis=0), k_all[:, 128 * b:128 * (b + 1)], _NT) + bias4
    for qb in range(2):
        outs = {}
        for b in range(2):
            s = s_all[qb, b]
            sk = jnp.concatenate([jnp.broadcast_to(sink[:, hd:hd + 1], (T, 1)) for hd in stacks[b]], axis=0)
            m = jnp.maximum(jnp.max(s, axis=-1, keepdims=True), sk)
            pr = jnp.exp(s - m)
            den = jnp.sum(pr, axis=-1, keepdims=True) + jnp.exp(sk - m)
            o = _dot(pr.astype(BF16), v_all[qb][:, 128 * b:128 * (b + 1)]) / den
            for n, hd in enumerate(stacks[b]):
                outs[hd] = o[T * n:T * (n + 1)]
        for pair in range(4):
            o_ref[qb * T:(qb + 1) * T, 128 * pair:128 * (pair + 1)] = jnp.where(
                lo, outs[2 * pair], outs[2 * pair + 1]).astype(BF16)


def _attention(q, kk, vv, sink, ncc, nlc):
    nt = q.shape[0]
    nblk = ncc + nlc
    assert nblk % 2 == 0 and ncc % 2 == 0
    c_len = ncc * T
    prev = lambda w: pl.BlockSpec((T, w), lambda j: (jnp.maximum(2 * j - 1, 0), 0))
    cur = lambda w: pl.BlockSpec((2 * T, w), lambda j: (j, 0))
    nxt = lambda w: pl.BlockSpec((T, w), lambda j: (jnp.minimum(2 * j + 2, nblk - 1), 0))
    ctx = lambda w: pl.BlockSpec((c_len, w), lambda j: (0, 0))
    return pl.pallas_call(
        functools.partial(_attn_kernel, ncc, nblk),
        grid=(nblk // 2,),
        in_specs=[cur(512), prev(256), cur(256), nxt(256), prev(256), cur(256), nxt(256),
                  ctx(256), ctx(256), _full((1, 8))],
        out_specs=cur(512),
        out_shape=jax.ShapeDtypeStruct((nt, 512), BF16),
        compiler_params=_cparams("arbitrary"),
        name="window_attention",
    )(q, kk, kk, kk, vv, vv, vv, kk, vv, sink)


def _in1_kernel(nct, ntiles, xp_ref, x_ref, xn_ref, mod_ref, g_ref,
                wcat_ref, wg_ref, wgh_ref, cw_ref, cb_ref, gb_ref,
                q_ref, kt_ref, v_ref, o_ref, gc_ref, gr_ref):
    i = pl.program_id(0)
    sh = mod_ref[0, 0:1, :]
    sc = mod_ref[0, 1:2, :]
    g = g_ref[...]
    x_all = jnp.concatenate([xp_ref[...], x_ref[...], xn_ref[...]], axis=0)
    h_all = _norm_mod(x_all, g, sc, sh)
    hb_all = h_all.astype(BF16)
    main_all = _dot(hb_all, wcat_ref[...])
    h, hb, main = h_all[8:8 + TM], hb_all[8:8 + TM], main_all[8:8 + TM]
    prev_ok, next_ok = _seq_edges(i, nct, ntiles)
    x_prev = main_all[7:8, 0:2048] * prev_ok
    x_next = main_all[8 + TM:9 + TM, 0:2048] * next_ok
    act = _conv_silu(main[:, 0:2048], x_prev, x_next, cw_ref[...], cb_ref[...])
    q_ref[...] = act[:, 0:512].astype(BF16)
    kt_ref[...] = jnp.transpose(act[:, 512:1024] * 0.125).astype(BF16)
    v_ref[...] = act[:, 1024:2048].astype(BF16)
    o_ref[...] = main[:, 2048:3072].astype(BF16)
    h_lo = (h - hb.astype(F32)).astype(BF16)
    gates = _hi_lo_cols(_dot(hb, wg_ref[...]), _dot(h_lo, wgh_ref[...]), 32) + gb_ref[...]
    lane = lax.broadcasted_iota(I32, gates.shape, 1)
    gates = jnp.where(lane < 16, gates, _log_sigmoid(gates))
    gc_ref[...] = gates[:, 0:32]
    gr_ref[...] = jnp.transpose(gates)[0:32, :]


def _in1(x1, mod, p, nct):
    nt = x1.shape[0]
    ntiles = nt // TM
    tile = lambda w: pl.BlockSpec((TM, w), lambda i: (i, 0))
    return pl.pallas_call(
        functools.partial(_in1_kernel, nct, ntiles),
        grid=(ntiles,),
        in_specs=_halo_specs(nt, lambda i: i) + [
            _mod_spec(nct), _full((1, D)), _full((D, 3072)), _full((D, 128)), _full((D, 128)),
            _full((3, 2048)), _full((1, 2048)), _full((1, 128))],
        out_specs=[tile(512), pl.BlockSpec((512, TM), lambda i: (0, i)), tile(1024), tile(1024),
                   tile(32), pl.BlockSpec((32, TM), lambda i: (0, i))],
        out_shape=[jax.ShapeDtypeStruct((nt, 512), BF16), jax.ShapeDtypeStruct((512, nt), BF16),
                   jax.ShapeDtypeStruct((nt, 1024), BF16), jax.ShapeDtypeStruct((nt, 1024), BF16),
                   jax.ShapeDtypeStruct((nt, 32), F32), jax.ShapeDtypeStruct((32, nt), F32)],
        compiler_params=_cparams("arbitrary"),
        name="in_proj_odd",
    )(x1, x1, x1, mod, p["g_mix"], p["wcat"], p["wg"], p["wg_hi"], p["conv_w"], p["conv_b"], p["gate_bias"])


def _mlstm_kernel(rev, *refs):
    if rev:
        (q_ref, kt_ref, v_ref, gc_ref, gr_ref, hf_ref, og_ref, hn_ref, o_ref,
         c_ref, mc_ref, mr_ref) = refs
    else:
        q_ref, kt_ref, v_ref, gc_ref, gr_ref, o_ref, c_ref, mc_ref, mr_ref = refs
    j = pl.program_id(0)

    @pl.when(j == 0)
    def _():
        c_ref[...] = jnp.zeros_like(c_ref)
        mc_ref[...] = jnp.zeros_like(mc_ref)
        mr_ref[...] = jnp.zeros_like(mr_ref)

    d = 8 if rev else 0
    end = 0 if rev else T - 1
    order = tuple(reversed(range(SCAN_CHUNKS))) if rev else tuple(range(SCAN_CHUNKS))
    ones_b = jnp.ones((T, 128), BF16)
    sub = lax.broadcasted_iota(I32, (128, T), 0)
    zero_k = jnp.zeros((128, T), BF16)

    def prologue(c):
        rows = slice(c * T, (c + 1) * T)
        ig_col = gc_ref[rows, d:d + 8]
        lf_col = gc_ref[rows, 16 + d:24 + d]
        ig_row = gr_ref[d:d + 8, rows]
        lf_row = gr_ref[16 + d:24 + d, rows]
        tri, b_col, b_row = _cumsums(rev, lf_col, lf_row)
        blast_row = b_col[end:end + 1, :]
        blast_col = b_row[:, end:end + 1]
        wend_row = blast_col - b_row + ig_row
        ac_col = jnp.max(wend_row, axis=1, keepdims=True)
        return dict(tri=tri, b_row=b_row, ig_row=ig_row, blast_row=blast_row, blast_col=blast_col,
                    ac_col=ac_col, eend_row=jnp.exp(wend_row - ac_col),
                    ac_row=jnp.max(blast_row - b_col + ig_col, axis=0, keepdims=True),
                    b_bc=_lane_bcast(b_col), q=q_ref[rows, :])

    pro = {c: prologue(c) for c in order}
    m_col = mc_ref[:, 0:1]
    m_row = mr_ref[0:1, 0:8]
    for c in order:
        p = pro[c]
        mnew_col = jnp.maximum(p["blast_col"] + m_col, p["ac_col"])
        p["sp_col"] = jnp.exp(p["blast_col"] + m_col - mnew_col)
        p["sc_col"] = jnp.exp(p["ac_col"] - mnew_col)
        p["m_row"] = m_row
        m_col, m_row = mnew_col, jnp.maximum(p["blast_row"] + m_row, p["ac_row"])

    def head_matmuls(c, hd):
        p = pro[c]
        rows = slice(c * T, (c + 1) * T)
        pair, hi = hd // 2, hd % 2
        qp = p["q"][:, 128 * pair:128 * (pair + 1)]
        ktp = kt_ref[128 * pair:128 * (pair + 1), rows]
        kth = jnp.where((sub >= 64) if hi else (sub < 64), ktp, zero_k)
        vaug = jnp.concatenate([v_ref[rows, 128 * hd:128 * (hd + 1)], ones_b], axis=1)
        cst = c_ref[hd]
        sqk = _dot(qp, kth)
        inter = _dot(qp, cst.astype(BF16))
        kte = (kth.astype(F32) * p["eend_row"][hd:hd + 1, :]).astype(BF16)
        c_ref[hd] = p["sp_col"][hd:hd + 1, :] * cst + p["sc_col"][hd:hd + 1, :] * _dot(kte, vaug)
        return sqk, inter, vaug

    items = [(c, hd) for c in order for hd in range(8)]
    nxt = head_matmuls(*items[0])
    for n, (c, hd) in enumerate(items):
        p = pro[c]
        rows = slice(c * T, (c + 1) * T)
        cols = slice(128 * hd, 128 * (hd + 1))
        sqk, inter, vaug = nxt
        if n + 1 < len(items):
            nxt = head_matmuls(*items[n + 1])
        bh = p["b_bc"][hd]
        dlog = jnp.where(p["tri"], bh - p["b_row"][hd:hd + 1, :] + p["ig_row"][hd:hd + 1, :], NEG_INF)
        gh = bh + p["m_row"][:, hd:hd + 1]
        mstar = jnp.maximum(gh, jnp.max(dlog, axis=-1, keepdims=True))
        w = (jnp.exp(dlog - mstar) * sqk).astype(BF16)
        intra = _dot(w, vaug)
        e_int = jnp.exp(gh - mstar)
        den = jnp.maximum(jnp.abs(intra[:, 128:256] + e_int * inter[:, 128:256]), jnp.exp(-mstar))
        hh = (intra[:, 0:128] + e_int * inter[:, 0:128]) / den
        if rev:
            hh = hh + hf_ref[rows, cols]
            ms = jnp.mean(hh * hh, axis=-1, keepdims=True)
            hh = hh * lax.rsqrt(ms + EPS) * hn_ref[:, cols]
            og = og_ref[rows, cols].astype(F32)
            o_ref[rows, cols] = (hh * (0.5 * jnp.tanh(0.5 * og) + 0.5)).astype(BF16)
        else:
            o_ref[rows, cols] = hh
    mc_ref[...] = jnp.broadcast_to(m_col, mc_ref.shape)
    mr_ref[...] = jnp.broadcast_to(jnp.concatenate([m_row, jnp.zeros((1, 120), F32)], axis=1), mr_ref.shape)


def _mlstm(rev, q, kt, v, gc, gr, p, ncc, nlc, hf=None, og=None):
    nt = q.shape[0]
    assert ncc % SCAN_CHUNKS == 0 and nlc % SCAN_CHUNKS == 0
    ncc, nlc = ncc // SCAN_CHUNKS, nlc // SCAN_CHUNKS
    rows = SCAN_CHUNKS * T
    cmap = _scan_chunk_map(rev, ncc, nlc)
    blk = lambda w: pl.BlockSpec((rows, w), lambda j: (cmap(j), 0))
    blk_t = lambda h: pl.BlockSpec((h, rows), lambda j: (0, cmap(j)))
    in_specs = [blk(512), blk_t(512), blk(1024), blk(32), blk_t(32)]
    args = [q, kt, v, gc, gr]
    if rev:
        in_specs += [blk(1024), blk(1024), _full((1, 1024))]
        args += [hf, og, p["head_norm"]]
    return pl.pallas_call(
        functools.partial(_mlstm_kernel, rev),
        grid=(ncc + nlc,),
        in_specs=in_specs,
        out_specs=blk(1024),
        out_shape=jax.ShapeDtypeStruct((nt, 1024), BF16 if rev else F32),
        scratch_shapes=[pltpu.VMEM((8, 128, 256), F32), pltpu.VMEM((8, 128), F32), pltpu.VMEM((8, 128), F32)],
        compiler_params=_cparams("arbitrary"),
        name="mlstm_bwd" if rev else "mlstm_fwd",
    )(*args)


def _route(logits_t, rb_col):
    scores = _sigmoid(logits_t)
    biased = scores + rb_col
    row = lambda a, e: a[e:e + 1, :]
    gscore = []
    for g in range(4):
        b0, b1, b2, b3 = (row(biased, 4 * g + e) for e in range(4))
        h1, l1 = jnp.maximum(b0, b1), jnp.minimum(b0, b1)
        h2, l2 = jnp.maximum(b2, b3), jnp.minimum(b2, b3)
        gscore.append(jnp.maximum(h1, h2) + jnp.maximum(jnp.minimum(h1, h2), jnp.maximum(l1, l2)))
    gidx = jnp.zeros_like(gscore[0], dtype=I32)
    best = gscore[0]
    for g in range(1, 4):
        better = gscore[g] > best
        gidx = jnp.where(better, g, gidx)
        best = jnp.where(better, gscore[g], best)

    def pick(a, e):
        out = row(a, e)
        for g in range(1, 4):
            out = jnp.where(gidx == g, row(a, 4 * g + e), out)
        return out

    sb = [pick(biased, e) for e in range(4)]
    i1 = jnp.zeros_like(gidx)
    v1 = sb[0]
    for e in range(1, 4):
        better = sb[e] > v1
        i1 = jnp.where(better, e, i1)
        v1 = jnp.where(better, sb[e], v1)
    i2 = jnp.zeros_like(gidx)
    v2 = jnp.full_like(v1, NEG_INF)
    for e in range(4):
        better = jnp.logical_and(i1 != e, sb[e] > v2)
        i2 = jnp.where(better, e, i2)
        v2 = jnp.where(better, sb[e], v2)
    a = jnp.minimum(i1, i2)
    b = jnp.maximum(i1, i2)
    pair = jnp.where(a == 0, jnp.where(b == 1, 0, jnp.where(b == 2, 2, 3)), jnp.where(a == 1, jnp.where(b == 2, 1, 4), 5))
    return 6 * gidx + pair


def _out_kernel(nmix, nct, two_src, sub, t0, *refs):
    per = nmix + (2 if two_src else 1)
    tile_refs = [refs[per * u:per * (u + 1)] for u in range(sub)]
    (w_ref, mod_ref, g_ref, rw_ref, rwh_ref, rb_ref,
     xmid_ref, hrow_ref, bucket_ref, rank_ref, cnt_ref, cnt_scr) = refs[per * sub:]
    i = pl.program_id(0)

    @pl.when(i == 0)
    def _():
        cnt_scr[...] = jnp.zeros_like(cnt_scr)

    brow = lax.broadcasted_iota(I32, (NB_PAD, TM), 0)
    r = lax.broadcasted_iota(I32, (TM, TM), 0)
    c = lax.broadcasted_iota(I32, (TM, TM), 1)
    before = (r < c).astype(F32).astype(BF16)
    onehots = []
    for u in range(sub):
        mix_refs = tile_refs[u][:nmix]
        is_ctx = sub * i + u + t0 < nct
        mod = jnp.where(is_ctx, mod_ref[1], mod_ref[0])
        mix = mix_refs[0][...] if nmix == 1 else jnp.concatenate([mr[...] for mr in mix_refs], axis=1)
        if two_src:
            x = jnp.where(is_ctx, tile_refs[u][nmix][...], tile_refs[u][nmix + 1][...])
        else:
            x = tile_refs[u][nmix][...]
        x_mid = x + mod[2:3, :] * _dot(mix, w_ref[...])
        xmid_ref[TM * u:TM * (u + 1), :] = x_mid
        h = _norm_mod(x_mid, g_ref[...], mod[4:5, :], mod[3:4, :])
        hb = h.astype(BF16)
        h_lo = (h - hb.astype(F32)).astype(BF16)
        logits = _hi_lo_cols(_dot(hb, rw_ref[...]), _dot(h_lo, rwh_ref[...]), 16)
        logits_t = jnp.transpose(logits)[0:16, :]
        bucket = _route(logits_t, rb_ref[...])
        hrow_u = hrow_ref.at[pl.ds(TM * TOK * u, TM * TOK), :]
        for cblk in range(TOK):
            _tok_store(hrow_u, cblk, h[:, 128 * cblk:128 * (cblk + 1)])
        bucket_ref[u] = bucket
        onehots.append((brow == bucket).astype(F32))
    cnt = cnt_scr[...]
    for u in range(sub):
        onehot = onehots[u]
        cum = _dot(onehot.astype(BF16), before)
        rank_ref[u] = jnp.sum(onehot * (cum + cnt[:, 0:1]), axis=0, keepdims=True).astype(I32)
        cnt = cnt + jnp.sum(onehot, axis=1, keepdims=True)
    cnt_scr[...] = cnt
    cnt_ref[...] = cnt.astype(I32)


def _out_proj(mixes, w_out, xs, mod, g_ffn, rw, rw_hi, rb_col, t0, ntiles, nct):
    nmix = len(mixes)
    two_src = len(xs) == 2
    n = ntiles * TM
    sub = max(s for s in (5, 4, 3, 2, 1) if ntiles % s == 0)
    in_specs, args = [], []
    for u in range(sub):
        tile_of = lambda i, u=u: sub * i + u + t0
        for mx in mixes:
            in_specs.append(pl.BlockSpec((TM, mx.shape[1]), lambda i, f=tile_of: (f(i), 0)))
            args.append(mx)
        if two_src:
            in_specs += [pl.BlockSpec((TM, D), lambda i, f=tile_of: (jnp.minimum(f(i), nct - 1), 0)),
                         pl.BlockSpec((TM, D), lambda i, f=tile_of: (jnp.maximum(f(i) - nct, 0), 0))]
        else:
            in_specs.append(pl.BlockSpec((TM, D), lambda i, f=tile_of: (f(i), 0)))
        args += list(xs)
    in_specs += [_full((D, D)), _full((2, 6, D)), _full((1, D)), _full((D, 128)), _full((D, 128)), _full((16, 1))]
    rows_out = pl.BlockSpec((sub, 1, TM), lambda i: (i, 0, 0))
    return pl.pallas_call(
        functools.partial(_out_kernel, nmix, nct, two_src, sub, t0),
        grid=(ntiles // sub,),
        in_specs=in_specs,
        out_specs=[pl.BlockSpec((sub * TM, D), lambda i: (i, 0)),
                   pl.BlockSpec((sub * TM * TOK, 128), lambda i: (i, 0)), rows_out, rows_out,
                   _full((NB_PAD, 128))],
        out_shape=[jax.ShapeDtypeStruct((n, D), F32), jax.ShapeDtypeStruct((n * TOK, 128), F32),
                   jax.ShapeDtypeStruct((ntiles, 1, TM), I32), jax.ShapeDtypeStruct((ntiles, 1, TM), I32),
                   jax.ShapeDtypeStruct((NB_PAD, 128), I32)],
        scratch_shapes=[pltpu.VMEM((NB_PAD, 128), F32)],
        compiler_params=_cparams("arbitrary"),
        name="out_proj_router",
    )(*args, w_out, mod, g_ffn, rw, rw_hi, rb_col)


def _scatter_kernel(ntiles, pos_ref, flo_ref, fhi_ref, src_ref, dst_ref, hbuf, in_sem, out_sem):
    i = pl.program_id(0)
    slot = i % 3
    h_ref = hbuf.at[slot]
    sem = out_sem.at[slot]
    rows = TM * TOK

    def load(tile, s):
        return pltpu.make_async_copy(src_ref.at[pl.ds(pl.multiple_of(tile * rows, rows), rows), :],
                                     hbuf.at[s], in_sem.at[s])

    def tok(ref, t):
        return ref.at[pl.ds(pl.multiple_of(t * TOK, TOK), TOK), :]

    def copy(r, d_row, src=h_ref, sm=sem):
        return pltpu.make_async_copy(tok(src, r), tok(dst_ref, d_row), sm)

    def wait_rows(lo, hi, unroll, s=slot):
        def body(r, carry):
            copy(0, 0, hbuf.at[s], out_sem.at[s]).wait()
            return carry
        lax.fori_loop(lo, hi, body, 0, unroll=unroll)

    @pl.when(i == 0)
    def _():
        load(0, 0).start()
        if ntiles > 1:
            load(1, 1).start()

    load(i, slot).wait()

    def start(r8, c):
        for k in range(8):
            r = r8 * 8 + k
            copy(r, pos_ref[i * TM + r]).start(priority=k % 2)
        return c
    lax.fori_loop(0, TM // 8, start, 0)

    @pl.when(i > 0)
    def _():
        wait_rows(0, TM, 8, (i + 2) % 3)

    @pl.when(i + 2 < ntiles)
    def _():
        load(i + 2, (i + 2) % 3).start()

    @pl.when(i == ntiles - 1)
    def _():
        wait_rows(0, TM, 8)

        def pad_copies(b, act):
            off, left = flo_ref[b], fhi_ref[b] - flo_ref[b]
            for bit in reversed(range(TMM.bit_length() - 1)):
                k = 1 << bit
                take = (left & k) != 0

                @pl.when(take)
                def _(off=off, k=k):
                    act(pltpu.make_async_copy(
                        h_ref.at[pl.ds(0, k * TOK), :],
                        dst_ref.at[pl.ds(pl.multiple_of(off * TOK, TOK), k * TOK), :], sem))
                off = off + jnp.where(take, k, 0)

        def fill(b, c):
            pad_copies(b, lambda cp: cp.start())
            return c

        def drain(b, c):
            pad_copies(b, lambda cp: cp.wait())
            return c
        lax.fori_loop(0, N_BUCKETS, fill, 0)
        lax.fori_loop(0, N_BUCKETS, drain, 0)

        def tile_copy(j):
            rows = TMM * TOK
            return pltpu.make_async_copy(h_ref, dst_ref.at[pl.ds(pl.multiple_of(j * rows, rows), rows), :], sem)

        def fill_tile(j, c):
            tile_copy(j).start()
            return c

        def wait_tile(j, c):
            tile_copy(j).wait()
            return c
        lax.fori_loop(flo_ref[N_BUCKETS], fhi_ref[N_BUCKETS], fill_tile, 0)
        lax.fori_loop(flo_ref[N_BUCKETS], fhi_ref[N_BUCKETS], wait_tile, 0)


def _scatter_rows(hrow, pos, fill_lo, fill_hi, n, npad):
    assert TM == TMM
    return pl.pallas_call(
        functools.partial(_scatter_kernel, n // TM),
        grid_spec=pltpu.PrefetchScalarGridSpec(
            num_scalar_prefetch=3,
            grid=(n // TM,),
            in_specs=[pl.BlockSpec(memory_space=pl.ANY)],
            out_specs=pl.BlockSpec(memory_space=pl.ANY),
            scratch_shapes=[pltpu.VMEM((3, TM * TOK, 128), hrow.dtype), pltpu.SemaphoreType.DMA((3,)),
                            pltpu.SemaphoreType.DMA((3,))],
        ),
        out_shape=jax.ShapeDtypeStruct((npad * TOK, 128), hrow.dtype),
        compiler_params=_cparams("arbitrary"),
        name="moe_scatter_rows",
    )(pos, fill_lo, fill_hi, hrow)


def _combine_kernel(ntiles, pos_ref, x_ref, mod_ref, ys_ref, o_ref, ybuf, sem):
    i = pl.program_id(0)

    def copy(tile, slot, r):
        src = pl.multiple_of(pos_ref[tile * TM + r] * TOK, TOK)
        dst = pl.multiple_of(r * TOK, TOK)
        return pltpu.make_async_copy(ys_ref.at[pl.ds(src, TOK), :], ybuf.at[slot, pl.ds(dst, TOK), :], sem.at[slot])

    def start_tile(tile, slot):
        def body(r8, carry):
            for k in range(8):
                copy(tile, slot, r8 * 8 + k).start(priority=k % 2)
            return carry
        lax.fori_loop(0, TM // 8, body, 0)

    @pl.when(i == 0)
    def _():
        start_tile(0, 0)

    @pl.when(i + 1 < ntiles)
    def _():
        start_tile(i + 1, (i + 1) % 2)

    slot = i % 2

    def wait_body(r, carry):
        copy(i, slot, 0).wait()
        return carry
    lax.fori_loop(0, TM, wait_body, 0, unroll=8)
    o_ref[...] = x_ref[...] + mod_ref[0, 5:6, :] * _tok_rows(ybuf.at[slot], TM)


def _combine(x_mid, mod, ys, pos, t0, nct):
    n = x_mid.shape[0]
    ntiles = n // TM
    return pl.pallas_call(
        functools.partial(_combine_kernel, ntiles),
        grid_spec=pltpu.PrefetchScalarGridSpec(
            num_scalar_prefetch=1,
            grid=(ntiles,),
            in_specs=[pl.BlockSpec((TM, D), lambda i, *_: (i, 0)),
                      pl.BlockSpec((1, 6, D), lambda i, *_: (jnp.where(i + t0 < nct, 1, 0), 0, 0)),
                      pl.BlockSpec(memory_space=pl.ANY)],
            out_specs=pl.BlockSpec((TM, D), lambda i, *_: (i, 0)),
            scratch_shapes=[pltpu.VMEM((2, TM * TOK, 128), F32), pltpu.SemaphoreType.DMA((2,))],
        ),
        out_shape=jax.ShapeDtypeStruct((n, D), F32),
        compiler_params=_cparams("arbitrary"),
        name="moe_combine",
    )(pos, x_mid, mod, ys)


def _moe_kernel(tea_ref, teb_ref, tblk_ref, tval_ref, x_ref, rw_ref, w1a_ref, w3a_ref, w2a_ref,
                w1b_ref, w3b_ref, w2b_ref, y_ref):
    del tblk_ref
    j = pl.program_id(0)

    @pl.when(tval_ref[j] != 0)
    def _():
        h = _tok_rows(x_ref, TMM)
        hb = h.astype(BF16)
        logits = _hi_lo_cols(_dot(hb, rw_ref[...]), 0.0, 16)
        scores = _sigmoid(logits)
        lane = lax.broadcasted_iota(I32, scores.shape, 1)
        s_a = jnp.sum(jnp.where(lane == tea_ref[j], scores, 0.0), axis=1, keepdims=True)
        s_b = jnp.sum(jnp.where(lane == teb_ref[j], scores, 0.0), axis=1, keepdims=True)
        gates = (s_a / (s_a + s_b), s_b / (s_a + s_b))
        acts = []
        for w1_ref, w3_ref, gate in ((w1a_ref, w3a_ref, gates[0]), (w1b_ref, w3b_ref, gates[1])):
            u = _dot(hb, w1_ref[0, 0].astype(BF16))
            v = _dot(hb, w3_ref[0, 0].astype(BF16))
            acts.append((_silu(u) * v * gate).astype(BF16))
        y = _dot(acts[0], w2a_ref[0, 0].astype(BF16)) + _dot(acts[1], w2b_ref[0, 0].astype(BF16))
        for cblk in range(TOK):
            _tok_store(y_ref, cblk, y[:, 128 * cblk:128 * (cblk + 1)])

    @pl.when(tval_ref[j] == 0)
    def _():
        y_ref[...] = jnp.zeros_like(y_ref)


def _moe(xs_sorted, rw, w1, w3, w2, layer, tile_ea, tile_eb, tile_blk, tile_valid):
    npad = xs_sorted.shape[0] // TOK
    ntile = npad // TMM
    wspec = lambda shape, which: pl.BlockSpec(
        (1, 1) + shape, lambda j, ea, eb, blk, val: (layer, (ea, eb)[which][j], 0, 0))
    up, down = (D, D_EXPERT), (D_EXPERT, D)
    return pl.pallas_call(
        _moe_kernel,
        grid_spec=pltpu.PrefetchScalarGridSpec(
            num_scalar_prefetch=4,
            grid=(ntile,),
            in_specs=[pl.BlockSpec((TMM * TOK, 128), lambda j, ea, eb, blk, val: (blk[j], 0)),
                      pl.BlockSpec((D, 128), lambda j, *_: (0, 0)),
                      wspec(up, 0), wspec(up, 0), wspec(down, 0), wspec(up, 1), wspec(up, 1), wspec(down, 1)],
            out_specs=pl.BlockSpec((TMM * TOK, 128), lambda j, ea, eb, blk, val: (j, 0)),
        ),
        out_shape=jax.ShapeDtypeStruct((npad * TOK, 128), F32),
        compiler_params=_cparams("arbitrary"),
        name="moe_experts",
    )(tile_ea, tile_eb, tile_blk, tile_valid, xs_sorted, rw, w1, w3, w2, w1, w3, w2)


def _moe_block(x_mid, hrow, bucket, rank, counts, mod, rw, rw_hi, w1, w3, w2, layer, t0, nct):
    n = hrow.shape[0] // TOK
    ntile = n // TMM + N_BUCKETS
    npad = ntile * TMM
    cnt = counts[:N_BUCKETS, 0]
    padded = ((cnt + TMM - 1) // TMM) * TMM
    ends = jnp.cumsum(padded)
    starts = ends - padded
    total_tiles = ends[-1] // TMM
    tiles = jnp.arange(ntile, dtype=I32)
    tile_valid = (tiles < total_tiles).astype(I32)
    tile_blk = jnp.minimum(tiles, jnp.maximum(total_tiles - 1, 0))
    tile_bucket = jnp.minimum(jnp.sum((ends[None, :] <= (tile_blk * TMM)[:, None]).astype(I32), axis=1), N_BUCKETS - 1)
    pair = tile_bucket % 6
    grp = tile_bucket // 6
    slot_a = sum(jnp.where(pair == k, e, 0) for k, (e, _) in enumerate(_PAIR_SLOTS))
    slot_b = sum(jnp.where(pair == k, e, 0) for k, (_, e) in enumerate(_PAIR_SLOTS))
    tile_ea = (4 * grp + slot_a).astype(I32)
    tile_eb = (4 * grp + slot_b).astype(I32)
    bucket = bucket.reshape(-1)
    onehot = (bucket[:, None] == jnp.arange(N_BUCKETS, dtype=I32)[None, :]).astype(I32)
    pos = (rank.reshape(-1) + jnp.sum(onehot * starts[None, :], axis=1)).astype(I32)
    pad32 = lambda a, tail: jnp.zeros((NB_PAD,), I32).at[:N_BUCKETS].set(a.astype(I32)).at[N_BUCKETS].set(tail)
    fill_lo, fill_hi = pad32(starts + cnt, total_tiles), pad32(ends, ntile)
    xs_sorted = _scatter_rows(hrow, pos, fill_lo, fill_hi, n, npad)
    ys = _moe(xs_sorted, rw, w1, w3, w2, layer, tile_ea, tile_eb, tile_blk, tile_valid)
    return _combine(x_mid, mod, ys, pos, t0, nct)


def kernel(x, c, ctx, c_ctx, router_w, router_b, norm_mix, norm_ffn, w_mod, b_mod, ev_w_in, ev_conv_w, ev_conv_b, ev_dt_bias, ev_a_log, ev_d_skip, ev_ssd_norm, ev_q_norm, ev_k_norm, ev_sink, ev_w_out, od_w_in, od_conv_w, od_conv_b, od_igate_b, od_fgate_b, od_head_norm, od_w_out, moe_w1, moe_w3, moe_w2):
    s_len = x.shape[1]
    c_len = ctx.shape[1]
    assert x.shape[0] == 1 and s_len % TM == 0 and c_len % TM == 0 and s_len % GRID_W == 0
    nct = c_len // TM
    ncc, nlc = c_len // T, s_len // T
    nt = c_len + s_len
    ntiles = nt // TM

    mod = _modulation(c, c_ctx, w_mod, b_mod)
    rw, rw_hi = _hi_lo_weight(router_w)
    rb_col = router_b.reshape(N_EXPERTS, 1)
    pad128 = lambda v: jnp.zeros((1, 128), F32).at[0, :v.shape[0]].set(v)

    w = ev_w_in[0]
    rope_rows, rope_cols = _rope_tables(s_len)
    wdt, wdt_hi = _hi_lo_weight(w[:, 1280:1296])
    p0 = dict(
        g_mix=norm_mix[0].reshape(1, D),
        w_zx=w[:, 0:1280].astype(BF16), w_qkv=w[:, 1296:2064].astype(BF16), wdt=wdt, wdt_hi=wdt_hi,
        conv_w=ev_conv_w[0], conv_b=ev_conv_b[0].reshape(1, 768),
        dt_bias=pad128(ev_dt_bias[0].reshape(16)),
        q_norm=jnp.tile(ev_q_norm[0], 8).reshape(1, 512), k_norm=jnp.tile(ev_k_norm[0], 2).reshape(1, 128),
        rope_rows=rope_rows, rope_cols=rope_cols,
        alog_row=ev_a_log[0].reshape(1, 16), alog_col=ev_a_log[0].reshape(16, 1),
        d_skip=jnp.repeat(ev_d_skip[0], 64).reshape(1, 512), ssd_norm=ev_ssd_norm[0].reshape(1, 512))
    z, xs, bc, q, kk, vv, dtc, dtr = _in0(ctx[0], x[0], mod[0], p0, nct)
    yf = _ssd(False, xs, bc, dtc, dtr, p0, ncc, nlc)
    ymix = _ssd(True, xs, bc, dtc, dtr, p0, ncc, nlc, yf=yf, z=z)
    att = _attention(q, kk, vv, ev_sink[0].reshape(1, 8), ncc, nlc)
    x_mid0, hrow, bucket, rank, counts = _out_proj(
        [ymix, att], ev_w_out[0].astype(BF16), [ctx[0], x[0]], mod[0], norm_ffn[0].reshape(1, D),
        rw, rw_hi, rb_col, 0, ntiles, nct)
    x1 = _moe_block(x_mid0, hrow, bucket, rank, counts, mod[0], rw, rw_hi, moe_w1, moe_w3, moe_w2, 0, 0, nct)

    w = od_w_in[0]
    wg, wg_hi = _hi_lo_weight(w[:, 3072:3104])
    p1 = dict(
        g_mix=norm_mix[1].reshape(1, D),
        wcat=w[:, 0:3072].astype(BF16), wg=wg, wg_hi=wg_hi,
        conv_w=od_conv_w[0], conv_b=od_conv_b[0].reshape(1, 2048),
        gate_bias=pad128(jnp.concatenate([od_igate_b[0].reshape(16), od_fgate_b[0].reshape(16)])),
        head_norm=od_head_norm[0].reshape(1, 1024))
    q1, kt1, v1, og1, gc1, gr1 = _in1(x1, mod[1], p1, nct)
    hf = _mlstm(False, q1, kt1, v1, gc1, gr1, p1, ncc, nlc)
    hmix = _mlstm(True, q1, kt1, v1, gc1, gr1, p1, ncc, nlc, hf=hf, og=og1)
    x_mid1, hrow, bucket, rank, counts = _out_proj(
        [hmix], od_w_out[0].astype(BF16), [x1], mod[1], norm_ffn[1].reshape(1, D),
        rw, rw_hi, rb_col, nct, ntiles - nct, nct)
    return _moe_block(x_mid1, hrow, bucket, rank, counts, mod[1], rw, rw_hi, moe_w1, moe_w3, moe_w2, 1, nct,
                      nct)[None]
```

```python
import functools
import math

import jax
import jax.numpy as jnp
from jax import lax
from jax.experimental import pallas as pl
from jax.experimental.pallas import tpu as pltpu

F32 = jnp.float32
BF16 = jnp.bfloat16
I32 = jnp.int32

EPS = 1e-6
D = 1024
T = 128
SCAN_CHUNKS = 2
TM = 256
TMM = 256
GRID_W = 64
ROPE_THETA = 10000.0
N_EXPERTS = 16
N_BUCKETS = 24
NB_PAD = 32
D_EXPERT = 512
TOK = 8
U32 = jnp.uint32
NEG_INF = float("-inf")
VMEM_LIMIT = 56 * 1024 * 1024

_NN = (((1,), (0,)), ((), ()))
_NT = (((1,), (1,)), ((), ()))
_TN = (((0,), (0,)), ((), ()))

_PAIR_SLOTS = ((0, 1), (2, 1), (2, 0), (3, 0), (3, 1), (3, 2))


def _cparams(*sem):
    return pltpu.CompilerParams(dimension_semantics=sem, vmem_limit_bytes=VMEM_LIMIT)


def _dot(a, b, dims=_NN):
    return lax.dot_general(a, b, dims, preferred_element_type=F32)


def _split(a, n):
    out = []
    r = a
    for _ in range(n):
        t = r.astype(BF16)
        out.append(t)
        r = r - t.astype(F32)
    return out


def _mdot(as_, bs, dims=_NN, order=None):
    if order is None:
        order = len(as_) + len(bs) - 2
    acc = None
    for i, a in enumerate(as_):
        for j, b in enumerate(bs):
            if i + j <= order:
                p = _dot(a, b, dims)
                acc = p if acc is None else acc + p
    return acc


def _sigmoid(x):
    return 1.0 / (1.0 + jnp.exp(-x))


def _silu(x):
    hx = 0.5 * x
    return hx * jnp.tanh(hx) + hx


def _log1p_exp_neg_abs(x):
    e = jnp.exp(-jnp.abs(x))
    u = 1.0 + e
    um1 = u - 1.0
    return jnp.where(um1 == 0.0, e, jnp.log(u) * (e / jnp.where(um1 == 0.0, 1.0, um1)))


def _softplus(x):
    return jnp.maximum(x, 0.0) + _log1p_exp_neg_abs(x)


def _log_sigmoid(x):
    return jnp.minimum(x, 0.0) - _log1p_exp_neg_abs(x)


def _norm_mod(x, g, sc, sh):
    ms = jnp.mean(x * x, axis=-1, keepdims=True)
    return (x * lax.rsqrt(ms + EPS)) * g * (1.0 + sc) + sh


def _tok_load(ref, chunk, n):
    return ref[pl.ds(chunk, n, stride=TOK), :]


def _tok_store(ref, chunk, val):
    ref[pl.ds(chunk, val.shape[0], stride=TOK), :] = val


def _tok_rows(ref, n):
    return jnp.concatenate([_tok_load(ref, c, n) for c in range(TOK)], axis=1)


def _tri(rev):
    r = lax.broadcasted_iota(I32, (T, T), 0)
    c = lax.broadcasted_iota(I32, (T, T), 1)
    return (c >= r) if rev else (c <= r)


def _cumsums(rev, col, row):
    tri = _tri(rev)
    tri_b = tri.astype(F32).astype(BF16)
    trit_b = _tri(not rev).astype(F32).astype(BF16)
    ccol = _mdot([tri_b], _split(col, 3))
    crow = _mdot(_split(row, 3), [trit_b])
    return tri, ccol, crow


def _lane_bcast(col):
    return [jnp.broadcast_to(col[:, h:h + 1], (col.shape[0], 128)) for h in range(col.shape[1])]


def _mod_kernel(c_ref, w_ref, b_ref, o_ref):
    a = _silu(c_ref[...])
    o_ref[0] = _mdot(_split(a, 2), _split(w_ref[0], 2), order=1) + b_ref[0]


def _modulation(c, c_ctx, w_mod, b_mod):
    depth = w_mod.shape[0]
    n = w_mod.shape[2]
    tn = 1536
    cc = jnp.zeros((8, D), F32).at[0].set(c[0]).at[1].set(c_ctx)
    out = pl.pallas_call(
        _mod_kernel,
        grid=(depth, n // tn),
        in_specs=[
            pl.BlockSpec((8, D), lambda l, j: (0, 0)),
            pl.BlockSpec((1, D, tn), lambda l, j: (l, 0, j)),
            pl.BlockSpec((1, 1, tn), lambda l, j: (l, 0, j)),
        ],
        out_specs=pl.BlockSpec((1, 8, tn), lambda l, j: (l, 0, j)),
        out_shape=jax.ShapeDtypeStruct((depth, 8, n), F32),
        compiler_params=_cparams("arbitrary", "arbitrary"),
        name="modulation",
    )(cc, w_mod, b_mod.reshape(depth, 1, n))
    return out[:, :2].reshape(depth, 2, 6, D)


def _halo_specs(nrows, tile_of):
    nb8 = nrows // 8
    return [
        pl.BlockSpec((8, D), lambda i: (jnp.maximum(tile_of(i) * (TM // 8) - 1, 0), 0)),
        pl.BlockSpec((TM, D), lambda i: (tile_of(i), 0)),
        pl.BlockSpec((8, D), lambda i: (jnp.minimum((tile_of(i) + 1) * (TM // 8), nb8 - 1), 0)),
    ]


def _ctx_tile(nct):
    return lambda i: jnp.minimum(i, nct - 1)


def _lat_tile(nct):
    return lambda i: jnp.maximum(i - nct, 0)


def _mod_spec(nct, t0=0):
    return pl.BlockSpec((1, 6, D), lambda i: (jnp.where(i + t0 < nct, 1, 0), 0, 0))


def _full(shape):
    nd = len(shape)
    return pl.BlockSpec(shape, lambda i: (0,) * nd)


def _seq_edges(i, nct, ntiles):
    prev_ok = jnp.logical_and(i != 0, i != nct).astype(F32)
    next_ok = jnp.logical_and(i != nct - 1, i != ntiles - 1).astype(F32)
    return prev_ok, next_ok


def _conv_silu(x, x_first_prev, x_last_next, cw, cb):
    n = x.shape[0]
    rows = lax.broadcasted_iota(I32, x.shape, 0)
    x_prev = jnp.where(rows == 0, x_first_prev, pltpu.roll(x, 1, 0))
    x_next = jnp.where(rows == n - 1, x_last_next, pltpu.roll(x, n - 1, 0))
    return _silu(x_prev * cw[0:1] + x * cw[1:2] + x_next * cw[2:3] + cb)


def _hi_lo_cols(blk, lo_pass, n):
    return blk + pltpu.roll(blk, 128 - n, 1) + lo_pass


def _hi_lo_weight(w):
    n = w.shape[1]
    hi = w.astype(BF16)
    lo = (w - hi.astype(F32)).astype(BF16)
    z = jnp.zeros((w.shape[0], 128 - 2 * n), BF16)
    return jnp.concatenate([hi, lo, z], axis=1), jnp.concatenate([hi, jnp.zeros_like(lo), z], axis=1)


def _head_rms(xf, gamma):
    r = lax.broadcasted_iota(I32, (128, 128), 0) // 64
    c = lax.broadcasted_iota(I32, (128, 128), 1) // 64
    ones_bd = (r == c).astype(F32).astype(BF16)
    outs = []
    for j in range(xf.shape[1] // 128):
        blk = xf[:, 128 * j:128 * (j + 1)]
        ssum = _dot((blk * blk).astype(BF16), ones_bd)
        outs.append(blk * lax.rsqrt(ssum * (1.0 / 64.0) + EPS))
    return jnp.concatenate(outs, axis=1) * gamma


def _rope(xf, cos, sin):
    lane = lax.broadcasted_iota(I32, (xf.shape[0], 128), 1)
    first = (lane % 32) < 16
    outs = []
    for j in range(xf.shape[1] // 128):
        blk = xf[:, 128 * j:128 * (j + 1)]
        partner = jnp.where(first, pltpu.roll(blk, 112, 1), pltpu.roll(blk, 16, 1))
        outs.append(blk * cos + partner * sin)
    return jnp.concatenate(outs, axis=1)


def _rope_tables(s_len):
    lane = jnp.arange(128, dtype=I32)
    inv = ROPE_THETA ** (-(lane % 16).astype(F32) / 16.0)
    sign = jnp.where((lane % 32) < 16, -1.0, 1.0).astype(F32)
    ang_r = jnp.arange(s_len // GRID_W, dtype=F32)[:, None] * inv[None, :]
    ang_c = jnp.tile(jnp.arange(GRID_W, dtype=F32), TM // GRID_W)[:, None] * inv[None, :]
    both = lambda ang: jnp.stack([jnp.cos(ang), jnp.sin(ang) * sign[None, :]])
    return both(ang_r), both(ang_c)


def _in0_kernel(nct, ntiles, cp_ref, c_ref, cn_ref, xp_ref, x_ref, xn_ref, mod_ref, g_ref, wa_ref, wb_ref,
                wdt_ref, wdth_ref, cw_ref, cb_ref, dtb_ref, qn_ref, kn_ref, rowcs_ref, colcs_ref,
                z_ref, xs_ref, bc_ref, q_ref, kk_ref, vv_ref, dtc_ref, dtr_ref):
    i = pl.program_id(0)
    is_ctx = i < nct
    sh = mod_ref[0, 0:1, :]
    sc = mod_ref[0, 1:2, :]
    g = g_ref[...]
    x_all = jnp.concatenate([jnp.where(is_ctx, cp_ref[...], xp_ref[...]),
                             jnp.where(is_ctx, c_ref[...], x_ref[...]),
                             jnp.where(is_ctx, cn_ref[...], xn_ref[...])], axis=0)
    h_all = _norm_mod(x_all, g, sc, sh)
    hb_all = h_all.astype(BF16)
    zx_all = _dot(hb_all, wa_ref[...])
    h, hb, zx = h_all[8:8 + TM], hb_all[8:8 + TM], zx_all[8:8 + TM]
    qkv = _dot(hb, wb_ref[...])
    prev_ok, next_ok = _seq_edges(i, nct, ntiles)
    xb_prev = zx_all[7:8, 512:1280] * prev_ok
    xb_next = zx_all[8 + TM:9 + TM, 512:1280] * next_ok
    act = _conv_silu(zx[:, 512:1280], xb_prev, xb_next, cw_ref[...], cb_ref[...])
    z_ref[...] = zx[:, 0:512].astype(BF16)
    xs_ref[...] = act[:, 0:512].astype(BF16)
    bc_ref[...] = act[:, 512:768].astype(BF16)
    row0 = jnp.maximum(i - nct, 0) * (TM // GRID_W)
    rowcs = [jnp.concatenate([jnp.broadcast_to(rowcs_ref[t, pl.ds(row0 + kq, 1), :], (GRID_W, 128))
                              for kq in range(TM // GRID_W)], axis=0) for t in range(2)]
    row_lanes = (lax.broadcasted_iota(I32, (TM, 128), 1) % 64) < 32
    cos = jnp.where(is_ctx, 1.0, jnp.where(row_lanes, rowcs[0], colcs_ref[0]))
    sin = jnp.where(is_ctx, 0.0, jnp.where(row_lanes, rowcs[1], colcs_ref[1]))
    q = _rope(_head_rms(qkv[:, 0:512], qn_ref[...]), cos, sin) * 0.125
    q_ref[...] = q.astype(BF16)
    k = _rope(_head_rms(qkv[:, 512:640], kn_ref[...]), cos, sin)
    kk_ref[...] = jnp.concatenate([k, pltpu.roll(k, 64, 1)], axis=1).astype(BF16)
    v = qkv[:, 640:768]
    vv_ref[...] = jnp.concatenate([v, pltpu.roll(v, 64, 1)], axis=1).astype(BF16)
    h_lo = (h - hb.astype(F32)).astype(BF16)
    dt = _softplus(_hi_lo_cols(_dot(hb, wdt_ref[...]), _dot(h_lo, wdth_ref[...]), 16) + dtb_ref[...])
    dtc_ref[...] = dt[:, 0:16]
    dtr_ref[...] = jnp.transpose(dt)[0:16, :]


def _in0(ctx2, x2, mod, p, nct):
    c_len, s_len = ctx2.shape[0], x2.shape[0]
    nt = c_len + s_len
    ntiles = nt // TM
    tile = lambda w: pl.BlockSpec((TM, w), lambda i: (i, 0))
    lat = _lat_tile(nct)
    lat_tile = lambda w: pl.BlockSpec((TM, w), lambda i: (lat(i), 0))
    outs = [(512, BF16), (512, BF16), (256, BF16), (512, BF16), (256, BF16), (256, BF16), (16, F32)]
    return pl.pallas_call(
        functools.partial(_in0_kernel, nct, ntiles),
        grid=(ntiles,),
        in_specs=_halo_specs(c_len, _ctx_tile(nct)) + _halo_specs(s_len, lat) + [
            _mod_spec(nct), _full((1, D)), _full((D, 1280)), _full((D, 768)), _full((D, 128)), _full((D, 128)),
            _full((3, 768)), _full((1, 768)), _full((1, 128)),
            _full((1, 512)), _full((1, 128)), _full((2, s_len // GRID_W, 128)), _full((2, TM, 128))],
        out_specs=[tile(w) for w, _ in outs] + [pl.BlockSpec((16, TM), lambda i: (0, i))],
        out_shape=[jax.ShapeDtypeStruct((nt, w), dt) for w, dt in outs]
        + [jax.ShapeDtypeStruct((16, nt), F32)],
        compiler_params=_cparams("arbitrary"),
        name="in_proj_even",
    )(ctx2, ctx2, ctx2, x2, x2, x2, mod, p["g_mix"], p["w_zx"], p["w_qkv"], p["wdt"], p["wdt_hi"],
      p["conv_w"], p["conv_b"], p["dt_bias"], p["q_norm"], p["k_norm"], p["rope_rows"], p["rope_cols"])


def _scan_chunk_map(rev, ncc, nlc):
    if not rev:
        return lambda j: j
    return lambda j: jnp.where(j < ncc, ncc - 1 - j, ncc + nlc - 1 - (j - ncc))


def _ssd_kernel(rev, *refs):
    if rev:
        (xs_ref, bc_ref, dtc_ref, dtr_ref, alr_ref, alc_ref, yf_ref, z_ref, dsk_ref, nrm_ref,
         o_ref, st_ref) = refs
    else:
        xs_ref, bc_ref, dtc_ref, dtr_ref, alr_ref, alc_ref, o_ref, st_ref = refs
    j = pl.program_id(0)

    @pl.when(j == 0)
    def _():
        st_ref[...] = jnp.zeros_like(st_ref)

    d = 8 if rev else 0
    a_coef_row = -jnp.exp(alr_ref[...])[:, d:d + 8]
    a_coef_col = -jnp.exp(alc_ref[...])[d:d + 8, :]
    lane = lax.broadcasted_iota(I32, (T, 128), 1)
    lo = lane < 64
    zero_b = jnp.zeros((T, 128), BF16)
    hi_half = jnp.logical_not(lo)
    sub = lax.broadcasted_iota(I32, (128, T), 0)
    eye = (sub == lax.broadcasted_iota(I32, (128, T), 1)).astype(F32).astype(BF16)
    end = 0 if rev else T - 1
    order = tuple(reversed(range(SCAN_CHUNKS))) if rev else tuple(range(SCAN_CHUNKS))

    def prologue(c):
        rows = slice(c * T, (c + 1) * T)
        dtc = dtc_ref[rows, d:d + 8]
        dtr = dtr_ref[d:d + 8, rows]
        tri, acs_col, acs_row = _cumsums(rev, dtc * a_coef_row, dtr * a_coef_col)
        atot_col = acs_row[:, end:end + 1]
        acs_bc = _lane_bcast(acs_col)
        dec_row = jnp.exp(atot_col - acs_row) * dtr
        xs = xs_ref[rows, :]
        bm = bc_ref[rows, 0:128]
        cm = bc_ref[rows, 128:256]
        cgs = [jnp.where(lo, cm, zero_b), jnp.where(hi_half, cm, zero_b)]
        cbs = [_dot(cgs[g], bm, _NT) for g in range(2)]
        bmt = _dot(eye, bm, _NT)
        bgts = [jnp.where(sub < 64, bmt, 0.0), jnp.where(sub >= 64, bmt, 0.0)]
        xpairs = []
        for pr in range(4):
            xp = xs[:, 128 * pr:128 * (pr + 1)]
            xpairs.append(jnp.concatenate([jnp.where(lo, xp, zero_b), jnp.where(hi_half, xp, zero_b)], axis=0))
        return dict(tri=tri, acs_row=acs_row, acs_bc=acs_bc, atot_col=atot_col, dec_row=dec_row, dtr=dtr,
                    xs=xs, cgs=cgs, cbs=cbs, bgts=bgts, xpairs=xpairs)

    pro = {c: prologue(c) for c in order}
    y_off = {}
    for c in order:
        p = pro[c]
        for pr in range(4):
            g, h0, h1 = pr // 2, 2 * pr, 2 * pr + 1
            st = st_ref[pr]
            eacs = jnp.exp(jnp.where(lo, p["acs_bc"][h0], p["acs_bc"][h1]))
            y_off[c, pr] = _dot(p["cgs"][g], st.astype(BF16)) * eacs
            bdec = jnp.concatenate([(p["bgts"][g] * p["dec_row"][h0:h0 + 1, :]).astype(BF16),
                                    (p["bgts"][g] * p["dec_row"][h1:h1 + 1, :]).astype(BF16)], axis=1)
            carry = jnp.where(lo[0:1, :], jnp.exp(p["atot_col"][h0:h0 + 1, :]), jnp.exp(p["atot_col"][h1:h1 + 1, :]))
            st_ref[pr] = carry * st + _dot(bdec, p["xpairs"][pr])
    for c in order:
        p = pro[c]
        rows = slice(c * T, (c + 1) * T)
        ys = []
        for pr in range(4):
            ms = []
            for hd in (2 * pr, 2 * pr + 1):
                diff = p["acs_bc"][hd] - p["acs_row"][hd:hd + 1, :]
                lmat = jnp.exp(jnp.where(p["tri"], diff, NEG_INF))
                ms.append((p["cbs"][hd // 4] * lmat * p["dtr"][hd:hd + 1, :]).astype(BF16))
            ys.append(_dot(jnp.concatenate(ms, axis=1), p["xpairs"][pr]) + y_off[c, pr])
        y = jnp.concatenate(ys, axis=1)
        if not rev:
            o_ref[rows, :] = y
        else:
            ytot = y + yf_ref[rows, :] + dsk_ref[...] * p["xs"].astype(F32)
            gated = ytot * _silu(z_ref[rows, :].astype(F32))
            ms = jnp.mean(gated * gated, axis=-1, keepdims=True)
            o_ref[rows, :] = (gated * lax.rsqrt(ms + EPS) * nrm_ref[...]).astype(BF16)


def _ssd(rev, xs, bc, dtc, dtr, p, ncc, nlc, yf=None, z=None):
    nt = xs.shape[0]
    assert ncc % SCAN_CHUNKS == 0 and nlc % SCAN_CHUNKS == 0
    ncc, nlc = ncc // SCAN_CHUNKS, nlc // SCAN_CHUNKS
    rows = SCAN_CHUNKS * T
    cmap = _scan_chunk_map(rev, ncc, nlc)
    blk = lambda w: pl.BlockSpec((rows, w), lambda j: (cmap(j), 0))
    in_specs = [blk(512), blk(256), blk(16), pl.BlockSpec((16, rows), lambda j: (0, cmap(j))),
                _full((1, 16)), _full((16, 1))]
    args = [xs, bc, dtc, dtr, p["alog_row"], p["alog_col"]]
    if rev:
        in_specs += [blk(512), blk(512), _full((1, 512)), _full((1, 512))]
        args += [yf, z, p["d_skip"], p["ssd_norm"]]
    return pl.pallas_call(
        functools.partial(_ssd_kernel, rev),
        grid=(ncc + nlc,),
        in_specs=in_specs,
        out_specs=blk(512),
        out_shape=jax.ShapeDtypeStruct((nt, 512), BF16 if rev else F32),
        scratch_shapes=[pltpu.VMEM((4, 128, 128), F32)],
        compiler_params=_cparams("arbitrary"),
        name="ssd_bwd" if rev else "ssd_fwd",
    )(*args)


def _attn_kernel(ncc, nblk, q_ref, kp_ref, kc_ref, kn_ref, vp_ref, vc_ref, vn_ref, kx_ref, vx_ref,
                 sink_ref, o_ref):
    j = pl.program_id(0)
    c_len = kx_ref.shape[0]
    r = lax.broadcasted_iota(I32, (T, T), 0)
    c = lax.broadcasted_iota(I32, (T, T), 1)
    zero = jnp.zeros((T, T), F32)
    ninf = jnp.full((T, T), NEG_INF, F32)
    lo = lax.broadcasted_iota(I32, (T, 128), 1) < 64
    zero_b = jnp.zeros((T, 128), BF16)
    sink = sink_ref[...]
    kblk = [kp_ref[...], kc_ref[0:T, :], kc_ref[T:2 * T, :], kn_ref[...]]
    vblk = [vp_ref[...], vc_ref[0:T, :], vc_ref[T:2 * T, :], vn_ref[...]]
    stacks = [[hd for hd in range(8) if (hd // 4 + hd % 2) % 2 == b] for b in range(2)]
    s_all, v_all = {}, {}
    for qb in range(2):
        jb = 2 * j + qb
        is_lat = jb >= ncc
        prev_ok = jnp.logical_and(is_lat, jb >= ncc + 1)
        next_ok = jnp.logical_and(is_lat, jb <= nblk - 2)
        bias = jnp.concatenate([
            jnp.where(jnp.logical_and(prev_ok, c >= r), zero, ninf),
            jnp.where(is_lat, zero, ninf),
            jnp.where(jnp.logical_and(next_ok, c <= r), zero, ninf),
            jnp.zeros((T, c_len), F32)], axis=1)
        bias4 = jnp.concatenate([bias] * 4, axis=0)
        k_all = jnp.concatenate(kblk[qb:qb + 3] + [kx_ref[...]], axis=0)
        v_all[qb] = jnp.concatenate(vblk[qb:qb + 3] + [vx_ref[...]], axis=0)
        q = q_ref[qb * T:(qb + 1) * T, :]
        for b in range(2):
            qs = []
            for hd in stacks[b]:
                qp = q[:, 128 * (hd // 2):128 * (hd // 2 + 1)]
                qs.append(jnp.where(lo, zero_b, qp) if hd % 2 else jnp.where(lo, qp, zero_b))
            s_all[qb, b] = _dot(jnp.concatenate(qs, axis=0), k_all[:, 128 * b:128 * (b + 1)], _NT) + bias4
    for qb in range(2):
        outs = {}
        for b in range(2):
            s = s_all[qb, b]
            sk = jnp.concatenate([jnp.broadcast_to(sink[:, hd:hd + 1], (T, 1)) for hd in stacks[b]], axis=0)
            m = jnp.maximum(jnp.max(s, axis=-1, keepdims=True), sk)
            pr = jnp.exp(s - m)
            den = jnp.sum(pr, axis=-1, keepdims=True) + jnp.exp(sk - m)
            o = _dot(pr.astype(BF16), v_all[qb][:, 128 * b:128 * (b + 1)]) / den
            for n, hd in enumerate(stacks[b]):
                outs[hd] = o[T * n:T * (n + 1)]
        for pair in range(4):
            o_ref[qb * T:(qb + 1) * T, 128 * pair:128 * (pair + 1)] = jnp.where(
                lo, outs[2 * pair], outs[2 * pair + 1]).astype(BF16)


def _attention(q, kk, vv, sink, ncc, nlc):
    nt = q.shape[0]
    nblk = ncc + nlc
    assert nblk % 2 == 0 and ncc % 2 == 0
    c_len = ncc * T
    prev = lambda w: pl.BlockSpec((T, w), lambda j: (jnp.maximum(2 * j - 1, 0), 0))
    cur = lambda w: pl.BlockSpec((2 * T, w), lambda j: (j, 0))
    nxt = lambda w: pl.BlockSpec((T, w), lambda j: (jnp.minimum(2 * j + 2, nblk - 1), 0))
    ctx = lambda w: pl.BlockSpec((c_len, w), lambda j: (0, 0))
    return pl.pallas_call(
        functools.partial(_attn_kernel, ncc, nblk),
        grid=(nblk // 2,),
        in_specs=[cur(512), prev(256), cur(256), nxt(256), prev(256), cur(256), nxt(256),
                  ctx(256), ctx(256), _full((1, 8))],
        out_specs=cur(512),
        out_shape=jax.ShapeDtypeStruct((nt, 512), BF16),
        compiler_params=_cparams("arbitrary"),
        name="window_attention",
    )(q, kk, kk, kk, vv, vv, vv, kk, vv, sink)


def _in1_kernel(nct, ntiles, xp_ref, x_ref, xn_ref, mod_ref, g_ref,
                wcat_ref, wg_ref, wgh_ref, cw_ref, cb_ref, gb_ref,
                q_ref, kt_ref, v_ref, o_ref, gc_ref, gr_ref):
    i = pl.program_id(0)
    sh = mod_ref[0, 0:1, :]
    sc = mod_ref[0, 1:2, :]
    g = g_ref[...]
    x_all = jnp.concatenate([xp_ref[...], x_ref[...], xn_ref[...]], axis=0)
    h_all = _norm_mod(x_all, g, sc, sh)
    hb_all = h_all.astype(BF16)
    main_all = _dot(hb_all, wcat_ref[...])
    h, hb, main = h_all[8:8 + TM], hb_all[8:8 + TM], main_all[8:8 + TM]
    prev_ok, next_ok = _seq_edges(i, nct, ntiles)
    x_prev = main_all[7:8, 0:2048] * prev_ok
    x_next = main_all[8 + TM:9 + TM, 0:2048] * next_ok
    act = _conv_silu(main[:, 0:2048], x_prev, x_next, cw_ref[...], cb_ref[...])
    q_ref[...] = act[:, 0:512].astype(BF16)
    kt_ref[...] = jnp.transpose(act[:, 512:1024] * 0.125).astype(BF16)
    v_ref[...] = act[:, 1024:2048].astype(BF16)
    o_ref[...] = main[:, 2048:3072].astype(BF16)
    h_lo = (h - hb.astype(F32)).astype(BF16)
    gates = _hi_lo_cols(_dot(hb, wg_ref[...]), _dot(h_lo, wgh_ref[...]), 32) + gb_ref[...]
    lane = lax.broadcasted_iota(I32, gates.shape, 1)
    gates = jnp.where(lane < 16, gates, _log_sigmoid(gates))
    gc_ref[...] = gates[:, 0:32]
    gr_ref[...] = jnp.transpose(gates)[0:32, :]


def _in1(x1, mod, p, nct):
    nt = x1.shape[0]
    ntiles = nt // TM
    tile = lambda w: pl.BlockSpec((TM, w), lambda i: (i, 0))
    return pl.pallas_call(
        functools.partial(_in1_kernel, nct, ntiles),
        grid=(ntiles,),
        in_specs=_halo_specs(nt, lambda i: i) + [
            _mod_spec(nct), _full((1, D)), _full((D, 3072)), _full((D, 128)), _full((D, 128)),
            _full((3, 2048)), _full((1, 2048)), _full((1, 128))],
        out_specs=[tile(512), pl.BlockSpec((512, TM), lambda i: (0, i)), tile(1024), tile(1024),
                   tile(32), pl.BlockSpec((32, TM), lambda i: (0, i))],
        out_shape=[jax.ShapeDtypeStruct((nt, 512), BF16), jax.ShapeDtypeStruct((512, nt), BF16),
                   jax.ShapeDtypeStruct((nt, 1024), BF16), jax.ShapeDtypeStruct((nt, 1024), BF16),
                   jax.ShapeDtypeStruct((nt, 32), F32), jax.ShapeDtypeStruct((32, nt), F32)],
        compiler_params=_cparams("arbitrary"),
        name="in_proj_odd",
    )(x1, x1, x1, mod, p["g_mix"], p["wcat"], p["wg"], p["wg_hi"], p["conv_w"], p["conv_b"], p["gate_bias"])


def _mlstm_kernel(rev, *refs):
    if rev:
        (q_ref, kt_ref, v_ref, gc_ref, gr_ref, hf_ref, og_ref, hn_ref, o_ref,
         c_ref, mc_ref, mr_ref) = refs
    else:
        q_ref, kt_ref, v_ref, gc_ref, gr_ref, o_ref, c_ref, mc_ref, mr_ref = refs
    j = pl.program_id(0)

    @pl.when(j == 0)
    def _():
        c_ref[...] = jnp.zeros_like(c_ref)
        mc_ref[...] = jnp.zeros_like(mc_ref)
        mr_ref[...] = jnp.zeros_like(mr_ref)

    d = 8 if rev else 0
    end = 0 if rev else T - 1
    order = tuple(reversed(range(SCAN_CHUNKS))) if rev else tuple(range(SCAN_CHUNKS))
    ones_b = jnp.ones((T, 128), BF16)
    sub = lax.broadcasted_iota(I32, (128, T), 0)
    zero_k = jnp.zeros((128, T), BF16)

    def prologue(c):
        rows = slice(c * T, (c + 1) * T)
        ig_col = gc_ref[rows, d:d + 8]
        lf_col = gc_ref[rows, 16 + d:24 + d]
        ig_row = gr_ref[d:d + 8, rows]
        lf_row = gr_ref[16 + d:24 + d, rows]
        tri, b_col, b_row = _cumsums(rev, lf_col, lf_row)
        blast_row = b_col[end:end + 1, :]
        blast_col = b_row[:, end:end + 1]
        wend_row = blast_col - b_row + ig_row
        ac_col = jnp.max(wend_row, axis=1, keepdims=True)
        return dict(tri=tri, b_row=b_row, ig_row=ig_row, blast_row=blast_row, blast_col=blast_col,
                    ac_col=ac_col, eend_row=jnp.exp(wend_row - ac_col),
                    ac_row=jnp.max(blast_row - b_col + ig_col, axis=0, keepdims=True),
                    b_bc=_lane_bcast(b_col), q=q_ref[rows, :])

    pro = {c: prologue(c) for c in order}
    m_col = mc_ref[:, 0:1]
    m_row = mr_ref[0:1, 0:8]
    for c in order:
        p = pro[c]
        mnew_col = jnp.maximum(p["blast_col"] + m_col, p["ac_col"])
        p["sp_col"] = jnp.exp(p["blast_col"] + m_col - mnew_col)
        p["sc_col"] = jnp.exp(p["ac_col"] - mnew_col)
        p["m_row"] = m_row
        m_col, m_row = mnew_col, jnp.maximum(p["blast_row"] + m_row, p["ac_row"])

    def head_matmuls(c, hd):
        p = pro[c]
        rows = slice(c * T, (c + 1) * T)
        pair, hi = hd // 2, hd % 2
        qp = p["q"][:, 128 * pair:128 * (pair + 1)]
        ktp = kt_ref[128 * pair:128 * (pair + 1), rows]
        kth = jnp.where((sub >= 64) if hi else (sub < 64), ktp, zero_k)
        vaug = jnp.concatenate([v_ref[rows, 128 * hd:128 * (hd + 1)], ones_b], axis=1)
        cst = c_ref[hd]
        sqk = _dot(qp, kth)
        inter = _dot(qp, cst.astype(BF16))
        kte = (kth.astype(F32) * p["eend_row"][hd:hd + 1, :]).astype(BF16)
        c_ref[hd] = p["sp_col"][hd:hd + 1, :] * cst + p["sc_col"][hd:hd + 1, :] * _dot(kte, vaug)
        return sqk, inter, vaug

    items = [(c, hd) for c in order for hd in range(8)]
    nxt = head_matmuls(*items[0])
    for n, (c, hd) in enumerate(items):
        p = pro[c]
        rows = slice(c * T, (c + 1) * T)
        cols = slice(128 * hd, 128 * (hd + 1))
        sqk, inter, vaug = nxt
        if n + 1 < len(items):
            nxt = head_matmuls(*items[n + 1])
        bh = p["b_bc"][hd]
        dlog = jnp.where(p["tri"], bh - p["b_row"][hd:hd + 1, :] + p["ig_row"][hd:hd + 1, :], NEG_INF)
        gh = bh + p["m_row"][:, hd:hd + 1]
        mstar = jnp.maximum(gh, jnp.max(dlog, axis=-1, keepdims=True))
        w = (jnp.exp(dlog - mstar) * sqk).astype(BF16)
        intra = _dot(w, vaug)
        e_int = jnp.exp(gh - mstar)
        den = jnp.maximum(jnp.abs(intra[:, 128:256] + e_int * inter[:, 128:256]), jnp.exp(-mstar))
        hh = (intra[:, 0:128] + e_int * inter[:, 0:128]) / den
        if rev:
            hh = hh + hf_ref[rows, cols]
            ms = jnp.mean(hh * hh, axis=-1, keepdims=True)
            hh = hh * lax.rsqrt(ms + EPS) * hn_ref[:, cols]
            og = og_ref[rows, cols].astype(F32)
            o_ref[rows, cols] = (hh * (0.5 * jnp.tanh(0.5 * og) + 0.5)).astype(BF16)
        else:
            o_ref[rows, cols] = hh
    mc_ref[...] = jnp.broadcast_to(m_col, mc_ref.shape)
    mr_ref[...] = jnp.broadcast_to(jnp.concatenate([m_row, jnp.zeros((1, 120), F32)], axis=1), mr_ref.shape)


def _mlstm(rev, q, kt, v, gc, gr, p, ncc, nlc, hf=None, og=None):
    nt = q.shape[0]
    assert ncc % SCAN_CHUNKS == 0 and nlc % SCAN_CHUNKS == 0
    ncc, nlc = ncc // SCAN_CHUNKS, nlc // SCAN_CHUNKS
    rows = SCAN_CHUNKS * T
    cmap = _scan_chunk_map(rev, ncc, nlc)
    blk = lambda w: pl.BlockSpec((rows, w), lambda j: (cmap(j), 0))
    blk_t = lambda h: pl.BlockSpec((h, rows), lambda j: (0, cmap(j)))
    in_specs = [blk(512), blk_t(512), blk(1024), blk(32), blk_t(32)]
    args = [q, kt, v, gc, gr]
    if rev:
        in_specs += [blk(1024), blk(1024), _full((1, 1024))]
        args += [hf, og, p["head_norm"]]
    return pl.pallas_call(
        functools.partial(_mlstm_kernel, rev),
        grid=(ncc + nlc,),
        in_specs=in_specs,
        out_specs=blk(1024),
        out_shape=jax.ShapeDtypeStruct((nt, 1024), BF16 if rev else F32),
        scratch_shapes=[pltpu.VMEM((8, 128, 256), F32), pltpu.VMEM((8, 128), F32), pltpu.VMEM((8, 128), F32)],
        compiler_params=_cparams("arbitrary"),
        name="mlstm_bwd" if rev else "mlstm_fwd",
    )(*args)


def _route(logits_t, rb_col):
    scores = _sigmoid(logits_t)
    biased = scores + rb_col
    row = lambda a, e: a[e:e + 1, :]
    gscore = []
    for g in range(4):
        b0, b1, b2, b3 = (row(biased, 4 * g + e) for e in range(4))
        h1, l1 = jnp.maximum(b0, b1), jnp.minimum(b0, b1)
        h2, l2 = jnp.maximum(b2, b3), jnp.minimum(b2, b3)
        gscore.append(jnp.maximum(h1, h2) + jnp.maximum(jnp.minimum(h1, h2), jnp.maximum(l1, l2)))
    gidx = jnp.zeros_like(gscore[0], dtype=I32)
    best = gscore[0]
    for g in range(1, 4):
        better = gscore[g] > best
        gidx = jnp.where(better, g, gidx)
        best = jnp.where(better, gscore[g], best)

    def pick(a, e):
        out = row(a, e)
        for g in range(1, 4):
            out = jnp.where(gidx == g, row(a, 4 * g + e), out)
        return out

    sb = [pick(biased, e) for e in range(4)]
    i1 = jnp.zeros_like(gidx)
    v1 = sb[0]
    for e in range(1, 4):
        better = sb[e] > v1
        i1 = jnp.where(better, e, i1)
        v1 = jnp.where(better, sb[e], v1)
    i2 = jnp.zeros_like(gidx)
    v2 = jnp.full_like(v1, NEG_INF)
    for e in range(4):
        better = jnp.logical_and(i1 != e, sb[e] > v2)
        i2 = jnp.where(better, e, i2)
        v2 = jnp.where(better, sb[e], v2)
    a = jnp.minimum(i1, i2)
    b = jnp.maximum(i1, i2)
    pair = jnp.where(a == 0, jnp.where(b == 1, 0, jnp.where(b == 2, 2, 3)), jnp.where(a == 1, jnp.where(b == 2, 1, 4), 5))
    return 6 * gidx + pair


def _out_kernel(nmix, nct, two_src, sub, t0, *refs):
    per = nmix + (2 if two_src else 1)
    tile_refs = [refs[per * u:per * (u + 1)] for u in range(sub)]
    (w_ref, mod_ref, g_ref, rw_ref, rwh_ref, rb_ref,
     xmid_ref, hrow_ref, bucket_ref, rank_ref, cnt_ref, cnt_scr) = refs[per * sub:]
    i = pl.program_id(0)

    @pl.when(i == 0)
    def _():
        cnt_scr[...] = jnp.zeros_like(cnt_scr)

    brow = lax.broadcasted_iota(I32, (NB_PAD, TM), 0)
    r = lax.broadcasted_iota(I32, (TM, TM), 0)
    c = lax.broadcasted_iota(I32, (TM, TM), 1)
    before = (r < c).astype(F32).astype(BF16)
    onehots = []
    for u in range(sub):
        mix_refs = tile_refs[u][:nmix]
        is_ctx = sub * i + u + t0 < nct
        mod = jnp.where(is_ctx, mod_ref[1], mod_ref[0])
        mix = mix_refs[0][...] if nmix == 1 else jnp.concatenate([mr[...] for mr in mix_refs], axis=1)
        if two_src:
            x = jnp.where(is_ctx, tile_refs[u][nmix][...], tile_refs[u][nmix + 1][...])
        else:
            x = tile_refs[u][nmix][...]
        x_mid = x + mod[2:3, :] * _dot(mix, w_ref[...])
        xmid_ref[TM * u:TM * (u + 1), :] = x_mid
        h = _norm_mod(x_mid, g_ref[...], mod[4:5, :], mod[3:4, :])
        hb = h.astype(BF16)
        h_lo = (h - hb.astype(F32)).astype(BF16)
        logits = _hi_lo_cols(_dot(hb, rw_ref[...]), _dot(h_lo, rwh_ref[...]), 16)
        logits_t = jnp.transpose(logits)[0:16, :]
        bucket = _route(logits_t, rb_ref[...])
        hrow_u = hrow_ref.at[pl.ds(TM * TOK * u, TM * TOK), :]
        for cblk in range(TOK):
            _tok_store(hrow_u, cblk, h[:, 128 * cblk:128 * (cblk + 1)])
        bucket_ref[u] = bucket
        onehots.append((brow == bucket).astype(F32))
    cnt = cnt_scr[...]
    for u in range(sub):
        onehot = onehots[u]
        cum = _dot(onehot.astype(BF16), before)
        rank_ref[u] = jnp.sum(onehot * (cum + cnt[:, 0:1]), axis=0, keepdims=True).astype(I32)
        cnt = cnt + jnp.sum(onehot, axis=1, keepdims=True)
    cnt_scr[...] = cnt
    cnt_ref[...] = cnt.astype(I32)


def _out_proj(mixes, w_out, xs, mod, g_ffn, rw, rw_hi, rb_col, t0, ntiles, nct):
    nmix = len(mixes)
    two_src = len(xs) == 2
    n = ntiles * TM
    sub = max(s for s in (5, 4, 3, 2, 1) if ntiles % s == 0)
    in_specs, args = [], []
    for u in range(sub):
        tile_of = lambda i, u=u: sub * i + u + t0
        for mx in mixes:
            in_specs.append(pl.BlockSpec((TM, mx.shape[1]), lambda i, f=tile_of: (f(i), 0)))
            args.append(mx)
        if two_src:
            in_specs += [pl.BlockSpec((TM, D), lambda i, f=tile_of: (jnp.minimum(f(i), nct - 1), 0)),
                         pl.BlockSpec((TM, D), lambda i, f=tile_of: (jnp.maximum(f(i) - nct, 0), 0))]
        else:
            in_specs.append(pl.BlockSpec((TM, D), lambda i, f=tile_of: (f(i), 0)))
        args += list(xs)
    in_specs += [_full((D, D)), _full((2, 6, D)), _full((1, D)), _full((D, 128)), _full((D, 128)), _full((16, 1))]
    rows_out = pl.BlockSpec((sub, 1, TM), lambda i: (i, 0, 0))
    return pl.pallas_call(
        functools.partial(_out_kernel, nmix, nct, two_src, sub, t0),
        grid=(ntiles // sub,),
        in_specs=in_specs,
        out_specs=[pl.BlockSpec((sub * TM, D), lambda i: (i, 0)),
                   pl.BlockSpec((sub * TM * TOK, 128), lambda i: (i, 0)), rows_out, rows_out,
                   _full((NB_PAD, 128))],
        out_shape=[jax.ShapeDtypeStruct((n, D), F32), jax.ShapeDtypeStruct((n * TOK, 128), F32),
                   jax.ShapeDtypeStruct((ntiles, 1, TM), I32), jax.ShapeDtypeStruct((ntiles, 1, TM), I32),
                   jax.ShapeDtypeStruct((NB_PAD, 128), I32)],
        scratch_shapes=[pltpu.VMEM((NB_PAD, 128), F32)],
        compiler_params=_cparams("arbitrary"),
        name="out_proj_router",
    )(*args, w_out, mod, g_ffn, rw, rw_hi, rb_col)


def _scatter_kernel(ntiles, pos_ref, flo_ref, fhi_ref, src_ref, dst_ref, hbuf, in_sem, out_sem):
    i = pl.program_id(0)
    slot = i % 3
    h_ref = hbuf.at[slot]
    sem = out_sem.at[slot]
    rows = TM * TOK

    def load(tile, s):
        return pltpu.make_async_copy(src_ref.at[pl.ds(pl.multiple_of(tile * rows, rows), rows), :],
                                     hbuf.at[s], in_sem.at[s])

    def tok(ref, t):
        return ref.at[pl.ds(pl.multiple_of(t * TOK, TOK), TOK), :]

    def copy(r, d_row, src=h_ref, sm=sem):
        return pltpu.make_async_copy(tok(src, r), tok(dst_ref, d_row), sm)

    def wait_rows(lo, hi, unroll, s=slot):
        def body(r, carry):
            copy(0, 0, hbuf.at[s], out_sem.at[s]).wait()
            return carry
        lax.fori_loop(lo, hi, body, 0, unroll=unroll)

    @pl.when(i == 0)
    def _():
        load(0, 0).start()
        if ntiles > 1:
            load(1, 1).start()

    load(i, slot).wait()

    def start(r8, c):
        for k in range(8):
            r = r8 * 8 + k
            copy(r, pos_ref[i * TM + r]).start(priority=k % 2)
        return c
    lax.fori_loop(0, TM // 8, start, 0)

    @pl.when(i > 0)
    def _():
        wait_rows(0, TM, 8, (i + 2) % 3)

    @pl.when(i + 2 < ntiles)
    def _():
        load(i + 2, (i + 2) % 3).start()

    @pl.when(i == ntiles - 1)
    def _():
        wait_rows(0, TM, 8)

        def pad_copies(b, act):
            off, left = flo_ref[b], fhi_ref[b] - flo_ref[b]
            for bit in reversed(range(TMM.bit_length() - 1)):
                k = 1 << bit
                take = (left & k) != 0

                @pl.when(take)
                def _(off=off, k=k):
                    act(pltpu.make_async_copy(
                        h_ref.at[pl.ds(0, k * TOK), :],
                        dst_ref.at[pl.ds(pl.multiple_of(off * TOK, TOK), k * TOK), :], sem))
                off = off + jnp.where(take, k, 0)

        def fill(b, c):
            pad_copies(b, lambda cp: cp.start())
            return c

        def drain(b, c):
            pad_copies(b, lambda cp: cp.wait())
            return c
        lax.fori_loop(0, N_BUCKETS, fill, 0)
        lax.fori_loop(0, N_BUCKETS, drain, 0)

        def tile_copy(j):
            rows = TMM * TOK
            return pltpu.make_async_copy(h_ref, dst_ref.at[pl.ds(pl.multiple_of(j * rows, rows), rows), :], sem)

        def fill_tile(j, c):
            tile_copy(j).start()
            return c

        def wait_tile(j, c):
            tile_copy(j).wait()
            return c
        lax.fori_loop(flo_ref[N_BUCKETS], fhi_ref[N_BUCKETS], fill_tile, 0)
        lax.fori_loop(flo_ref[N_BUCKETS], fhi_ref[N_BUCKETS], wait_tile, 0)


def _scatter_rows(hrow, pos, fill_lo, fill_hi, n, npad):
    assert TM == TMM
    return pl.pallas_call(
        functools.partial(_scatter_kernel, n // TM),
        grid_spec=pltpu.PrefetchScalarGridSpec(
            num_scalar_prefetch=3,
            grid=(n // TM,),
            in_specs=[pl.BlockSpec(memory_space=pl.ANY)],
            out_specs=pl.BlockSpec(memory_space=pl.ANY),
            scratch_shapes=[pltpu.VMEM((3, TM * TOK, 128), hrow.dtype), pltpu.SemaphoreType.DMA((3,)),
                            pltpu.SemaphoreType.DMA((3,))],
        ),
        out_shape=jax.ShapeDtypeStruct((npad * TOK, 128), hrow.dtype),
        compiler_params=_cparams("arbitrary"),
        name="moe_scatter_rows",
    )(pos, fill_lo, fill_hi, hrow)


def _combine_kernel(ntiles, pos_ref, x_ref, mod_ref, ys_ref, o_ref, ybuf, sem):
    i = pl.program_id(0)

    def copy(tile, slot, r):
        src = pl.multiple_of(pos_ref[tile * TM + r] * TOK, TOK)
        dst = pl.multiple_of(r * TOK, TOK)
        return pltpu.make_async_copy(ys_ref.at[pl.ds(src, TOK), :], ybuf.at[slot, pl.ds(dst, TOK), :], sem.at[slot])

    def start_tile(tile, slot):
        def body(r8, carry):
            for k in range(8):
                copy(tile, slot, r8 * 8 + k).start(priority=k % 2)
            return carry
        lax.fori_loop(0, TM // 8, body, 0)

    @pl.when(i == 0)
    def _():
        start_tile(0, 0)

    @pl.when(i + 1 < ntiles)
    def _():
        start_tile(i + 1, (i + 1) % 2)

    slot = i % 2

    def wait_body(r, carry):
        copy(i, slot, 0).wait()
        return carry
    lax.fori_loop(0, TM, wait_body, 0, unroll=8)
    o_ref[...] = x_ref[...] + mod_ref[0, 5:6, :] * _tok_rows(ybuf.at[slot], TM)


def _combine(x_mid, mod, ys, pos, t0, nct):
    n = x_mid.shape[0]
    ntiles = n // TM
    return pl.pallas_call(
        functools.partial(_combine_kernel, ntiles),
        grid_spec=pltpu.PrefetchScalarGridSpec(
            num_scalar_prefetch=1,
            grid=(ntiles,),
            in_specs=[pl.BlockSpec((TM, D), lambda i, *_: (i, 0)),
                      pl.BlockSpec((1, 6, D), lambda i, *_: (jnp.where(i + t0 < nct, 1, 0), 0, 0)),
                      pl.BlockSpec(memory_space=pl.ANY)],
            out_specs=pl.BlockSpec((TM, D), lambda i, *_: (i, 0)),
            scratch_shapes=[pltpu.VMEM((2, TM * TOK, 128), F32), pltpu.SemaphoreType.DMA((2,))],
        ),
        out_shape=jax.ShapeDtypeStruct((n, D), F32),
        compiler_params=_cparams("arbitrary"),
        name="moe_combine",
    )(pos, x_mid, mod, ys)


def _moe_kernel(tea_ref, teb_ref, tblk_ref, tval_ref, x_ref, rw_ref, w1a_ref, w3a_ref, w2a_ref,
                w1b_ref, w3b_ref, w2b_ref, y_ref):
    del tblk_ref
    j = pl.program_id(0)

    def experts(nrows):
        h = _tok_rows(x_ref, nrows)
        hb = h.astype(BF16)
        logits = _hi_lo_cols(_dot(hb, rw_ref[...]), 0.0, 16)
        scores = _sigmoid(logits)
        lane = lax.broadcasted_iota(I32, scores.shape, 1)
        s_a = jnp.sum(jnp.where(lane == tea_ref[j], scores, 0.0), axis=1, keepdims=True)
        s_b = jnp.sum(jnp.where(lane == teb_ref[j], scores, 0.0), axis=1, keepdims=True)
        gates = (s_a / (s_a + s_b), s_b / (s_a + s_b))
        acts = []
        for w1_ref, w3_ref, gate in ((w1a_ref, w3a_ref, gates[0]), (w1b_ref, w3b_ref, gates[1])):
            u = _dot(hb, w1_ref[0, 0].astype(BF16))
            v = _dot(hb, w3_ref[0, 0].astype(BF16))
            acts.append((_silu(u) * v * gate).astype(BF16))
        y = _dot(acts[0], w2a_ref[0, 0].astype(BF16)) + _dot(acts[1], w2b_ref[0, 0].astype(BF16))
        for cblk in range(TOK):
            _tok_store(y_ref, cblk, y[:, 128 * cblk:128 * (cblk + 1)])
        if nrows < TMM:
            y_ref[nrows * TOK:TMM * TOK, :] = jnp.zeros(((TMM - nrows) * TOK, 128), F32)

    @pl.when(tval_ref[j] == 2)
    def _():
        experts(TMM)

    @pl.when(tval_ref[j] == 1)
    def _():
        experts(TMM // 2)

    @pl.when(tval_ref[j] == 0)
    def _():
        y_ref[...] = jnp.zeros_like(y_ref)


def _moe(xs_sorted, rw, w1, w3, w2, layer, tile_ea, tile_eb, tile_blk, tile_valid):
    npad = xs_sorted.shape[0] // TOK
    ntile = npad // TMM
    wspec = lambda shape, which: pl.BlockSpec(
        (1, 1) + shape, lambda j, ea, eb, blk, val: (layer, (ea, eb)[which][j], 0, 0))
    up, down = (D, D_EXPERT), (D_EXPERT, D)
    return pl.pallas_call(
        _moe_kernel,
        grid_spec=pltpu.PrefetchScalarGridSpec(
            num_scalar_prefetch=4,
            grid=(ntile,),
            in_specs=[pl.BlockSpec((TMM * TOK, 128), lambda j, ea, eb, blk, val: (blk[j], 0)),
                      pl.BlockSpec((D, 128), lambda j, *_: (0, 0)),
                      wspec(up, 0), wspec(up, 0), wspec(down, 0), wspec(up, 1), wspec(up, 1), wspec(down, 1)],
            out_specs=pl.BlockSpec((TMM * TOK, 128), lambda j, ea, eb, blk, val: (j, 0)),
        ),
        out_shape=jax.ShapeDtypeStruct((npad * TOK, 128), F32),
        compiler_params=_cparams("arbitrary"),
        name="moe_experts",
    )(tile_ea, tile_eb, tile_blk, tile_valid, xs_sorted, rw, w1, w3, w2, w1, w3, w2)


def _moe_block(x_mid, hrow, bucket, rank, counts, mod, rw, rw_hi, w1, w3, w2, layer, t0, nct):
    n = hrow.shape[0] // TOK
    ntile = n // TMM + N_BUCKETS
    npad = ntile * TMM
    cnt = counts[:N_BUCKETS, 0]
    padded = ((cnt + TMM - 1) // TMM) * TMM
    ends = jnp.cumsum(padded)
    starts = ends - padded
    total_tiles = ends[-1] // TMM
    tiles = jnp.arange(ntile, dtype=I32)
    tile_blk = jnp.minimum(tiles, jnp.maximum(total_tiles - 1, 0))
    tile_bucket = jnp.minimum(jnp.sum((ends[None, :] <= (tile_blk * TMM)[:, None]).astype(I32), axis=1), N_BUCKETS - 1)
    in_bucket = (tile_bucket[:, None] == jnp.arange(N_BUCKETS, dtype=I32)[None, :]).astype(I32)
    real = jnp.sum(in_bucket * (starts + cnt)[None, :], axis=1) - tiles * TMM
    tile_valid = jnp.where(tiles < total_tiles, jnp.where(real <= TMM // 2, 1, 2), 0).astype(I32)
    pair = tile_bucket % 6
    grp = tile_bucket // 6
    slot_a = sum(jnp.where(pair == k, e, 0) for k, (e, _) in enumerate(_PAIR_SLOTS))
    slot_b = sum(jnp.where(pair == k, e, 0) for k, (_, e) in enumerate(_PAIR_SLOTS))
    tile_ea = (4 * grp + slot_a).astype(I32)
    tile_eb = (4 * grp + slot_b).astype(I32)
    bucket = bucket.reshape(-1)
    onehot = (bucket[:, None] == jnp.arange(N_BUCKETS, dtype=I32)[None, :]).astype(I32)
    pos = (rank.reshape(-1) + jnp.sum(onehot * starts[None, :], axis=1)).astype(I32)
    pad32 = lambda a, tail: jnp.zeros((NB_PAD,), I32).at[:N_BUCKETS].set(a.astype(I32)).at[N_BUCKETS].set(tail)
    fill_lo, fill_hi = pad32(starts + cnt, total_tiles), pad32(ends, ntile)
    xs_sorted = _scatter_rows(hrow, pos, fill_lo, fill_hi, n, npad)
    ys = _moe(xs_sorted, rw, w1, w3, w2, layer, tile_ea, tile_eb, tile_blk, tile_valid)
    return _combine(x_mid, mod, ys, pos, t0, nct)


def kernel(x, c, ctx, c_ctx, router_w, router_b, norm_mix, norm_ffn, w_mod, b_mod, ev_w_in, ev_conv_w, ev_conv_b, ev_dt_bias, ev_a_log, ev_d_skip, ev_ssd_norm, ev_q_norm, ev_k_norm, ev_sink, ev_w_out, od_w_in, od_conv_w, od_conv_b, od_igate_b, od_fgate_b, od_head_norm, od_w_out, moe_w1, moe_w3, moe_w2):
    s_len = x.shape[1]
    c_len = ctx.shape[1]
    assert x.shape[0] == 1 and s_len % TM == 0 and c_len % TM == 0 and s_len % GRID_W == 0
    nct = c_len // TM
    ncc, nlc = c_len // T, s_len // T
    nt = c_len + s_len
    ntiles = nt // TM

    mod = _modulation(c, c_ctx, w_mod, b_mod)
    rw, rw_hi = _hi_lo_weight(router_w)
    rb_col = router_b.reshape(N_EXPERTS, 1)
    pad128 = lambda v: jnp.zeros((1, 128), F32).at[0, :v.shape[0]].set(v)

    w = ev_w_in[0]
    rope_rows, rope_cols = _rope_tables(s_len)
    wdt, wdt_hi = _hi_lo_weight(w[:, 1280:1296])
    p0 = dict(
        g_mix=norm_mix[0].reshape(1, D),
        w_zx=w[:, 0:1280].astype(BF16), w_qkv=w[:, 1296:2064].astype(BF16), wdt=wdt, wdt_hi=wdt_hi,
        conv_w=ev_conv_w[0], conv_b=ev_conv_b[0].reshape(1, 768),
        dt_bias=pad128(ev_dt_bias[0].reshape(16)),
        q_norm=jnp.tile(ev_q_norm[0], 8).reshape(1, 512), k_norm=jnp.tile(ev_k_norm[0], 2).reshape(1, 128),
        rope_rows=rope_rows, rope_cols=rope_cols,
        alog_row=ev_a_log[0].reshape(1, 16), alog_col=ev_a_log[0].reshape(16, 1),
        d_skip=jnp.repeat(ev_d_skip[0], 64).reshape(1, 512), ssd_norm=ev_ssd_norm[0].reshape(1, 512))
    z, xs, bc, q, kk, vv, dtc, dtr = _in0(ctx[0], x[0], mod[0], p0, nct)
    yf = _ssd(False, xs, bc, dtc, dtr, p0, ncc, nlc)
    ymix = _ssd(True, xs, bc, dtc, dtr, p0, ncc, nlc, yf=yf, z=z)
    att = _attention(q, kk, vv, ev_sink[0].reshape(1, 8), ncc, nlc)
    x_mid0, hrow, bucket, rank, counts = _out_proj(
        [ymix, att], ev_w_out[0].astype(BF16), [ctx[0], x[0]], mod[0], norm_ffn[0].reshape(1, D),
        rw, rw_hi, rb_col, 0, ntiles, nct)
    x1 = _moe_block(x_mid0, hrow, bucket, rank, counts, mod[0], rw, rw_hi, moe_w1, moe_w3, moe_w2, 0, 0, nct)

    w = od_w_in[0]
    wg, wg_hi = _hi_lo_weight(w[:, 3072:3104])
    p1 = dict(
        g_mix=norm_mix[1].reshape(1, D),
        wcat=w[:, 0:3072].astype(BF16), wg=wg, wg_hi=wg_hi,
        conv_w=od_conv_w[0], conv_b=od_conv_b[0].reshape(1, 2048),
        gate_bias=pad128(jnp.concatenate([od_igate_b[0].reshape(16), od_fgate_b[0].reshape(16)])),
        head_norm=od_head_norm[0].reshape(1, 1024))
    q1, kt1, v1, og1, gc1, gr1 = _in1(x1, mod[1], p1, nct)
    hf = _mlstm(False, q1, kt1, v1, gc1, gr1, p1, ncc, nlc)
    hmix = _mlstm(True, q1, kt1, v1, gc1, gr1, p1, ncc, nlc, hf=hf, og=og1)
    x_mid1, hrow, bucket, rank, counts = _out_proj(
        [hmix], od_w_out[0].astype(BF16), [x1], mod[1], norm_ffn[1].reshape(1, D),
        rw, rw_hi, rb_col, nct, ntiles - nct, nct)
    return _moe_block(x_mid1, hrow, bucket, rank, counts, mod[1], rw, rw_hi, moe_w1, moe_w3, moe_w2, 1, nct,
                      nct)[None]
```

```python
import functools

import jax
import jax.numpy as jnp
from jax import lax
from jax.experimental import pallas as pl
from jax.experimental.pallas import tpu as pltpu

F32 = jnp.float32
BF16 = jnp.bfloat16
I32 = jnp.int32

EPS = 1e-6
D = 1024
T = 128
SCAN_CHUNKS = 2
TM = 256
TMM = 256
GRID_W = 64
ROPE_THETA = 10000.0
N_EXPERTS = 16
N_BUCKETS = 24
NB_PAD = 32
D_EXPERT = 512
TOK = 8
NEG_INF = float("-inf")
VMEM_LIMIT = 56 * 1024 * 1024

_NN = (((1,), (0,)), ((), ()))
_NT = (((1,), (1,)), ((), ()))

_PAIR_SLOTS = ((0, 1), (2, 1), (2, 0), (3, 0), (3, 1), (3, 2))


def _cparams(*sem):
    return pltpu.CompilerParams(dimension_semantics=sem, vmem_limit_bytes=VMEM_LIMIT)


def _dot(a, b, dims=_NN):
    return lax.dot_general(a, b, dims, preferred_element_type=F32)


def _split(a, n):
    out = []
    r = a
    for _ in range(n):
        t = r.astype(BF16)
        out.append(t)
        r = r - t.astype(F32)
    return out


def _mdot(as_, bs, dims=_NN, order=None):
    if order is None:
        order = len(as_) + len(bs) - 2
    acc = None
    for i, a in enumerate(as_):
        for j, b in enumerate(bs):
            if i + j <= order:
                p = _dot(a, b, dims)
                acc = p if acc is None else acc + p
    return acc


def _sigmoid(x):
    return 1.0 / (1.0 + jnp.exp(-x))


def _silu(x):
    hx = 0.5 * x
    return hx * jnp.tanh(hx) + hx


def _log1p_exp_neg_abs(x):
    e = jnp.exp(-jnp.abs(x))
    u = 1.0 + e
    um1 = u - 1.0
    return jnp.where(um1 == 0.0, e, jnp.log(u) * (e / jnp.where(um1 == 0.0, 1.0, um1)))


def _softplus(x):
    return jnp.maximum(x, 0.0) + _log1p_exp_neg_abs(x)


def _log_sigmoid(x):
    return jnp.minimum(x, 0.0) - _log1p_exp_neg_abs(x)


def _norm_mod(x, g, sc, sh):
    ms = jnp.mean(x * x, axis=-1, keepdims=True)
    return (x * lax.rsqrt(ms + EPS)) * (g * (1.0 + sc)) + sh


def _tok_load(ref, chunk, n):
    return ref[pl.ds(chunk, n, stride=TOK), :]


def _tok_store(ref, chunk, val):
    ref[pl.ds(chunk, val.shape[0], stride=TOK), :] = val


def _tok_rows(ref, n):
    return jnp.concatenate([_tok_load(ref, c, n) for c in range(TOK)], axis=1)


def _tri(rev):
    r = lax.broadcasted_iota(I32, (T, T), 0)
    c = lax.broadcasted_iota(I32, (T, T), 1)
    return (c >= r) if rev else (c <= r)


def _cumsums(rev, col, row):
    tri = _tri(rev)
    tri_b = tri.astype(F32).astype(BF16)
    trit_b = _tri(not rev).astype(F32).astype(BF16)
    ccol = _mdot([tri_b], _split(col, 3))
    crow = _mdot(_split(row, 3), [trit_b])
    return tri, ccol, crow


def _lane_bcast(col):
    return [jnp.broadcast_to(col[:, h:h + 1], (col.shape[0], 128)) for h in range(col.shape[1])]


def _mod_kernel(c_ref, w_ref, b_ref, o_ref):
    a = _silu(c_ref[...])
    o_ref[0] = _mdot(_split(a, 2), _split(w_ref[0], 2), order=1) + b_ref[0]


def _modulation(c, c_ctx, w_mod, b_mod):
    depth = w_mod.shape[0]
    n = w_mod.shape[2]
    tn = 1536
    cc = jnp.zeros((8, D), F32).at[0].set(c[0]).at[1].set(c_ctx)
    out = pl.pallas_call(
        _mod_kernel,
        grid=(depth, n // tn),
        in_specs=[
            pl.BlockSpec((8, D), lambda l, j: (0, 0)),
            pl.BlockSpec((1, D, tn), lambda l, j: (l, 0, j)),
            pl.BlockSpec((1, 1, tn), lambda l, j: (l, 0, j)),
        ],
        out_specs=pl.BlockSpec((1, 8, tn), lambda l, j: (l, 0, j)),
        out_shape=jax.ShapeDtypeStruct((depth, 8, n), F32),
        compiler_params=_cparams("arbitrary", "arbitrary"),
        name="modulation",
    )(cc, w_mod, b_mod.reshape(depth, 1, n))
    return out[:, :2].reshape(depth, 2, 6, D)


def _halo_specs(nrows, tile_of):
    nb8 = nrows // 8
    return [
        pl.BlockSpec((8, D), lambda i: (jnp.maximum(tile_of(i) * (TM // 8) - 1, 0), 0)),
        pl.BlockSpec((TM, D), lambda i: (tile_of(i), 0)),
        pl.BlockSpec((8, D), lambda i: (jnp.minimum((tile_of(i) + 1) * (TM // 8), nb8 - 1), 0)),
    ]


def _ctx_tile(nct):
    return lambda i: jnp.minimum(i, nct - 1)


def _lat_tile(nct):
    return lambda i: jnp.maximum(i - nct, 0)


def _mod_spec(nct, t0=0):
    return pl.BlockSpec((1, 6, D), lambda i: (jnp.where(i + t0 < nct, 1, 0), 0, 0))


def _full(shape):
    nd = len(shape)
    return pl.BlockSpec(shape, lambda i: (0,) * nd)


def _seq_edges(i, nct, ntiles):
    prev_ok = jnp.logical_and(i != 0, i != nct).astype(F32)
    next_ok = jnp.logical_and(i != nct - 1, i != ntiles - 1).astype(F32)
    return prev_ok, next_ok


def _conv_silu(x, x_first_prev, x_last_next, cw, cb):
    n = x.shape[0]
    rows = lax.broadcasted_iota(I32, x.shape, 0)
    x_prev = jnp.where(rows == 0, x_first_prev, pltpu.roll(x, 1, 0))
    x_next = jnp.where(rows == n - 1, x_last_next, pltpu.roll(x, n - 1, 0))
    return _silu(x_prev * cw[0:1] + x * cw[1:2] + x_next * cw[2:3] + cb)


def _hi_lo_cols(blk, lo_pass, n):
    return blk + pltpu.roll(blk, 128 - n, 1) + lo_pass


def _hi_lo_weight(w):
    n = w.shape[1]
    hi = w.astype(BF16)
    lo = (w - hi.astype(F32)).astype(BF16)
    z = jnp.zeros((w.shape[0], 128 - 2 * n), BF16)
    return jnp.concatenate([hi, lo, z], axis=1), jnp.concatenate([hi, jnp.zeros_like(lo), z], axis=1)


def _head_rms(xf, gamma):
    r = lax.broadcasted_iota(I32, (128, 128), 0) // 64
    c = lax.broadcasted_iota(I32, (128, 128), 1) // 64
    ones_bd = (r == c).astype(F32).astype(BF16)
    outs = []
    for j in range(xf.shape[1] // 128):
        blk = xf[:, 128 * j:128 * (j + 1)]
        ssum = _dot((blk * blk).astype(BF16), ones_bd)
        outs.append(blk * lax.rsqrt(ssum * (1.0 / 64.0) + EPS))
    return jnp.concatenate(outs, axis=1) * gamma


def _rope(xf, cos, sin):
    lane = lax.broadcasted_iota(I32, (xf.shape[0], 128), 1)
    first = (lane % 32) < 16
    outs = []
    for j in range(xf.shape[1] // 128):
        blk = xf[:, 128 * j:128 * (j + 1)]
        partner = jnp.where(first, pltpu.roll(blk, 112, 1), pltpu.roll(blk, 16, 1))
        outs.append(blk * cos + partner * sin)
    return jnp.concatenate(outs, axis=1)


def _rope_tables(s_len):
    lane = jnp.arange(128, dtype=I32)
    inv = ROPE_THETA ** (-(lane % 16).astype(F32) / 16.0)
    sign = jnp.where((lane % 32) < 16, -1.0, 1.0).astype(F32)
    ang_r = jnp.arange(s_len // GRID_W, dtype=F32)[:, None] * inv[None, :]
    ang_c = jnp.tile(jnp.arange(GRID_W, dtype=F32), TM // GRID_W)[:, None] * inv[None, :]
    both = lambda ang: jnp.stack([jnp.cos(ang), jnp.sin(ang) * sign[None, :]])
    return both(ang_r), both(ang_c)


def _in0_kernel(nct, ntiles, cp_ref, c_ref, cn_ref, xp_ref, x_ref, xn_ref, mod_ref, g_ref, wa_ref, wb_ref,
                wdt_ref, wdth_ref, cw_ref, cb_ref, dtb_ref, qn_ref, kn_ref, rowcs_ref, colcs_ref,
                z_ref, xs_ref, bc_ref, q_ref, kk_ref, vv_ref, dtc_ref, dtr_ref):
    i = pl.program_id(0)
    is_ctx = i < nct
    sh = mod_ref[0, 0:1, :]
    sc = mod_ref[0, 1:2, :]
    g = g_ref[...]
    x_all = jnp.concatenate([jnp.where(is_ctx, cp_ref[...], xp_ref[...]),
                             jnp.where(is_ctx, c_ref[...], x_ref[...]),
                             jnp.where(is_ctx, cn_ref[...], xn_ref[...])], axis=0)
    h_all = _norm_mod(x_all, g, sc, sh)
    hb_all = h_all.astype(BF16)
    zx_all = _dot(hb_all, wa_ref[...])
    h, hb, zx = h_all[8:8 + TM], hb_all[8:8 + TM], zx_all[8:8 + TM]
    qkv = _dot(hb, wb_ref[...])
    prev_ok, next_ok = _seq_edges(i, nct, ntiles)
    xb_prev = zx_all[7:8, 512:1280] * prev_ok
    xb_next = zx_all[8 + TM:9 + TM, 512:1280] * next_ok
    act = _conv_silu(zx[:, 512:1280], xb_prev, xb_next, cw_ref[...], cb_ref[...])
    z_ref[...] = zx[:, 0:512].astype(BF16)
    xs_ref[...] = act[:, 0:512].astype(BF16)
    bc_ref[...] = act[:, 512:768].astype(BF16)
    row0 = jnp.maximum(i - nct, 0) * (TM // GRID_W)
    rowcs = [jnp.concatenate([jnp.broadcast_to(rowcs_ref[t, pl.ds(row0 + kq, 1), :], (GRID_W, 128))
                              for kq in range(TM // GRID_W)], axis=0) for t in range(2)]
    row_lanes = (lax.broadcasted_iota(I32, (TM, 128), 1) % 64) < 32
    cos = jnp.where(is_ctx, 1.0, jnp.where(row_lanes, rowcs[0], colcs_ref[0]))
    sin = jnp.where(is_ctx, 0.0, jnp.where(row_lanes, rowcs[1], colcs_ref[1]))
    q = _rope(_head_rms(qkv[:, 0:512], qn_ref[...]), cos, sin) * 0.125
    q_ref[...] = q.astype(BF16)
    k = _rope(_head_rms(qkv[:, 512:640], kn_ref[...]), cos, sin)
    kk_ref[...] = jnp.concatenate([k, pltpu.roll(k, 64, 1)], axis=1).astype(BF16)
    v = qkv[:, 640:768]
    vv_ref[...] = jnp.concatenate([v, pltpu.roll(v, 64, 1)], axis=1).astype(BF16)
    h_lo = (h - hb.astype(F32)).astype(BF16)
    dt = _softplus(_hi_lo_cols(_dot(hb, wdt_ref[...]), _dot(h_lo, wdth_ref[...]), 16) + dtb_ref[...])
    dtc_ref[...] = dt[:, 0:16]
    dtr_ref[...] = jnp.transpose(dt)[0:16, :]


def _in0(ctx2, x2, mod, p, nct):
    c_len, s_len = ctx2.shape[0], x2.shape[0]
    nt = c_len + s_len
    ntiles = nt // TM
    tile = lambda w: pl.BlockSpec((TM, w), lambda i: (i, 0))
    lat = _lat_tile(nct)
    outs = [(512, BF16), (512, BF16), (256, BF16), (512, BF16), (256, BF16), (256, BF16), (16, F32)]
    return pl.pallas_call(
        functools.partial(_in0_kernel, nct, ntiles),
        grid=(ntiles,),
        in_specs=_halo_specs(c_len, _ctx_tile(nct)) + _halo_specs(s_len, lat) + [
            _mod_spec(nct), _full((1, D)), _full((D, 1280)), _full((D, 768)), _full((D, 128)), _full((D, 128)),
            _full((3, 768)), _full((1, 768)), _full((1, 128)),
            _full((1, 512)), _full((1, 128)), _full((2, s_len // GRID_W, 128)), _full((2, TM, 128))],
        out_specs=[tile(w) for w, _ in outs] + [pl.BlockSpec((16, TM), lambda i: (0, i))],
        out_shape=[jax.ShapeDtypeStruct((nt, w), dt) for w, dt in outs]
        + [jax.ShapeDtypeStruct((16, nt), F32)],
        compiler_params=_cparams("arbitrary"),
        name="in_proj_even",
    )(ctx2, ctx2, ctx2, x2, x2, x2, mod, p["g_mix"], p["w_zx"], p["w_qkv"], p["wdt"], p["wdt_hi"],
      p["conv_w"], p["conv_b"], p["dt_bias"], p["q_norm"], p["k_norm"], p["rope_rows"], p["rope_cols"])


def _scan_chunk_map(rev, ncc, nlc):
    if not rev:
        return lambda j: j
    return lambda j: jnp.where(j < ncc, ncc - 1 - j, ncc + nlc - 1 - (j - ncc))


def _ssd_kernel(rev, *refs):
    if rev:
        (xs_ref, bc_ref, dtc_ref, dtr_ref, alr_ref, alc_ref, yf_ref, z_ref, dsk_ref, nrm_ref,
         o_ref, st_ref) = refs
    else:
        xs_ref, bc_ref, dtc_ref, dtr_ref, alr_ref, alc_ref, o_ref, st_ref = refs
    j = pl.program_id(0)

    @pl.when(j == 0)
    def _():
        st_ref[...] = jnp.zeros_like(st_ref)

    d = 8 if rev else 0
    a_coef_row = -jnp.exp(alr_ref[...])[:, d:d + 8]
    a_coef_col = -jnp.exp(alc_ref[...])[d:d + 8, :]
    lane = lax.broadcasted_iota(I32, (T, 128), 1)
    lo = lane < 64
    zero_b = jnp.zeros((T, 128), BF16)
    hi_half = jnp.logical_not(lo)
    sub = lax.broadcasted_iota(I32, (128, T), 0)
    eye = (sub == lax.broadcasted_iota(I32, (128, T), 1)).astype(F32).astype(BF16)
    end = 0 if rev else T - 1
    order = tuple(reversed(range(SCAN_CHUNKS))) if rev else tuple(range(SCAN_CHUNKS))

    def prologue(c):
        rows = slice(c * T, (c + 1) * T)
        dtc = dtc_ref[rows, d:d + 8]
        dtr = dtr_ref[d:d + 8, rows]
        tri, acs_col, acs_row = _cumsums(rev, dtc * a_coef_row, dtr * a_coef_col)
        atot_col = acs_row[:, end:end + 1]
        acs_bc = _lane_bcast(acs_col)
        dec_row = jnp.exp(atot_col - acs_row) * dtr
        xs = xs_ref[rows, :]
        bm = bc_ref[rows, 0:128]
        cm = bc_ref[rows, 128:256]
        cgs = [jnp.where(lo, cm, zero_b), jnp.where(hi_half, cm, zero_b)]
        cbs = [_dot(cgs[g], bm, _NT) for g in range(2)]
        bmt = _dot(eye, bm, _NT)
        bgts = [jnp.where(sub < 64, bmt, 0.0), jnp.where(sub >= 64, bmt, 0.0)]
        xpairs = []
        for pr in range(4):
            xp = xs[:, 128 * pr:128 * (pr + 1)]
            xpairs.append(jnp.concatenate([jnp.where(lo, xp, zero_b), jnp.where(hi_half, xp, zero_b)], axis=0))
        return dict(tri=tri, acs_row=acs_row, acs_bc=acs_bc, atot_col=atot_col, dec_row=dec_row, dtr=dtr,
                    xs=xs, cgs=cgs, cbs=cbs, bgts=bgts, xpairs=xpairs)

    pro = {c: prologue(c) for c in order}
    y_off = {}
    for c in order:
        p = pro[c]
        for pr in range(4):
            g, h0, h1 = pr // 2, 2 * pr, 2 * pr + 1
            st = st_ref[pr]
            eacs = jnp.exp(jnp.where(lo, p["acs_bc"][h0], p["acs_bc"][h1]))
            y_off[c, pr] = _dot(p["cgs"][g], st.astype(BF16)) * eacs
            bdec = jnp.concatenate([(p["bgts"][g] * p["dec_row"][h0:h0 + 1, :]).astype(BF16),
                                    (p["bgts"][g] * p["dec_row"][h1:h1 + 1, :]).astype(BF16)], axis=1)
            carry = jnp.where(lo[0:1, :], jnp.exp(p["atot_col"][h0:h0 + 1, :]), jnp.exp(p["atot_col"][h1:h1 + 1, :]))
            st_ref[pr] = carry * st + _dot(bdec, p["xpairs"][pr])
    for c in order:
        p = pro[c]
        rows = slice(c * T, (c + 1) * T)
        ys = []
        for pr in range(4):
            ms = []
            for hd in (2 * pr, 2 * pr + 1):
                diff = p["acs_bc"][hd] - p["acs_row"][hd:hd + 1, :]
                lmat = jnp.exp(jnp.where(p["tri"], diff, NEG_INF))
                ms.append((p["cbs"][hd // 4] * lmat * p["dtr"][hd:hd + 1, :]).astype(BF16))
            ys.append(_dot(jnp.concatenate(ms, axis=1), p["xpairs"][pr]) + y_off[c, pr])
        y = jnp.concatenate(ys, axis=1)
        if not rev:
            o_ref[rows, :] = y
        else:
            ytot = y + yf_ref[rows, :] + dsk_ref[...] * p["xs"].astype(F32)
            gated = ytot * _silu(z_ref[rows, :].astype(F32))
            ms = jnp.mean(gated * gated, axis=-1, keepdims=True)
            o_ref[rows, :] = (gated * lax.rsqrt(ms + EPS) * nrm_ref[...]).astype(BF16)


def _ssd(rev, xs, bc, dtc, dtr, p, ncc, nlc, yf=None, z=None):
    nt = xs.shape[0]
    assert ncc % SCAN_CHUNKS == 0 and nlc % SCAN_CHUNKS == 0
    ncc, nlc = ncc // SCAN_CHUNKS, nlc // SCAN_CHUNKS
    rows = SCAN_CHUNKS * T
    cmap = _scan_chunk_map(rev, ncc, nlc)
    blk = lambda w: pl.BlockSpec((rows, w), lambda j: (cmap(j), 0))
    in_specs = [blk(512), blk(256), blk(16), pl.BlockSpec((16, rows), lambda j: (0, cmap(j))),
                _full((1, 16)), _full((16, 1))]
    args = [xs, bc, dtc, dtr, p["alog_row"], p["alog_col"]]
    if rev:
        in_specs += [blk(512), blk(512), _full((1, 512)), _full((1, 512))]
        args += [yf, z, p["d_skip"], p["ssd_norm"]]
    return pl.pallas_call(
        functools.partial(_ssd_kernel, rev),
        grid=(ncc + nlc,),
        in_specs=in_specs,
        out_specs=blk(512),
        out_shape=jax.ShapeDtypeStruct((nt, 512), BF16 if rev else F32),
        scratch_shapes=[pltpu.VMEM((4, 128, 128), F32)],
        compiler_params=_cparams("arbitrary"),
        name="ssd_bwd" if rev else "ssd_fwd",
    )(*args)


def _attn_kernel(ncc, nblk, q_ref, kp_ref, kc_ref, kn_ref, vp_ref, vc_ref, vn_ref, kx_ref, vx_ref,
                 sink_ref, o_ref):
    j = pl.program_id(0)
    c_len = kx_ref.shape[0]
    r = lax.broadcasted_iota(I32, (T, T), 0)
    c = lax.broadcasted_iota(I32, (T, T), 1)
    zero = jnp.zeros((T, T), F32)
    ninf = jnp.full((T, T), NEG_INF, F32)
    lo = lax.broadcasted_iota(I32, (T, 128), 1) < 64
    zero_b = jnp.zeros((T, 128), BF16)
    sink = sink_ref[...]
    kblk = [kp_ref[...], kc_ref[0:T, :], kc_ref[T:2 * T, :], kn_ref[...]]
    vblk = [vp_ref[...], vc_ref[0:T, :], vc_ref[T:2 * T, :], vn_ref[...]]
    stacks = [[hd for hd in range(8) if (hd // 4 + hd % 2) % 2 == b] for b in range(2)]
    s_all, v_all = {}, {}
    for qb in range(2):
        jb = 2 * j + qb
        is_lat = jb >= ncc
        prev_ok = jnp.logical_and(is_lat, jb >= ncc + 1)
        next_ok = jnp.logical_and(is_lat, jb <= nblk - 2)
        bias = jnp.concatenate([
            jnp.where(jnp.logical_and(prev_ok, c >= r), zero, ninf),
            jnp.where(is_lat, zero, ninf),
            jnp.where(jnp.logical_and(next_ok, c <= r), zero, ninf),
            jnp.zeros((T, c_len), F32)], axis=1)
        bias4 = jnp.concatenate([bias] * 4, axis=0)
        k_all = jnp.concatenate(kblk[qb:qb + 3] + [kx_ref[...]], axis=0)
        v_all[qb] = jnp.concatenate(vblk[qb:qb + 3] + [vx_ref[...]], axis=0)
        q = q_ref[qb * T:(qb + 1) * T, :]
        for b in range(2):
            qs = []
            for hd in stacks[b]:
                qp = q[:, 128 * (hd // 2):128 * (hd // 2 + 1)]
                qs.append(jnp.where(lo, zero_b, qp) if hd % 2 else jnp.where(lo, qp, zero_b))
            s_all[qb, b] = _dot(jnp.concatenate(qs, axis=0), k_all[:, 128 * b:128 * (b + 1)], _NT) + bias4
    for qb in range(2):
        outs = {}
        for b in range(2):
            s = s_all[qb, b]
            sk = jnp.concatenate([jnp.broadcast_to(sink[:, hd:hd + 1], (T, 1)) for hd in stacks[b]], axis=0)
            m = jnp.maximum(jnp.max(s, axis=-1, keepdims=True), sk)
            pr = jnp.exp(s - m)
            den = jnp.sum(pr, axis=-1, keepdims=True) + jnp.exp(sk - m)
            o = _dot(pr.astype(BF16), v_all[qb][:, 128 * b:128 * (b + 1)]) / den
            for n, hd in enumerate(stacks[b]):
                outs[hd] = o[T * n:T * (n + 1)]
        for pair in range(4):
            o_ref[qb * T:(qb + 1) * T, 128 * pair:128 * (pair + 1)] = jnp.where(
                lo, outs[2 * pair], outs[2 * pair + 1]).astype(BF16)


def _attention(q, kk, vv, sink, ncc, nlc):
    nt = q.shape[0]
    nblk = ncc + nlc
    assert nblk % 2 == 0 and ncc % 2 == 0
    c_len = ncc * T
    prev = lambda w: pl.BlockSpec((T, w), lambda j: (jnp.maximum(2 * j - 1, 0), 0))
    cur = lambda w: pl.BlockSpec((2 * T, w), lambda j: (j, 0))
    nxt = lambda w: pl.BlockSpec((T, w), lambda j: (jnp.minimum(2 * j + 2, nblk - 1), 0))
    ctx = lambda w: pl.BlockSpec((c_len, w), lambda j: (0, 0))
    return pl.pallas_call(
        functools.partial(_attn_kernel, ncc, nblk),
        grid=(nblk // 2,),
        in_specs=[cur(512), prev(256), cur(256), nxt(256), prev(256), cur(256), nxt(256),
                  ctx(256), ctx(256), _full((1, 8))],
        out_specs=cur(512),
        out_shape=jax.ShapeDtypeStruct((nt, 512), BF16),
        compiler_params=_cparams("arbitrary"),
        name="window_attention",
    )(q, kk, kk, kk, vv, vv, vv, kk, vv, sink)


def _in1_kernel(nct, ntiles, xp_ref, x_ref, xn_ref, mod_ref, g_ref,
                wcat_ref, wg_ref, wgh_ref, cw_ref, cb_ref, gb_ref,
                q_ref, kt_ref, v_ref, o_ref, gc_ref, gr_ref):
    i = pl.program_id(0)
    sh = mod_ref[0, 0:1, :]
    sc = mod_ref[0, 1:2, :]
    g = g_ref[...]
    x_all = jnp.concatenate([xp_ref[...], x_ref[...], xn_ref[...]], axis=0)
    h_all = _norm_mod(x_all, g, sc, sh)
    hb_all = h_all.astype(BF16)
    main_all = _dot(hb_all, wcat_ref[...])
    h, hb, main = h_all[8:8 + TM], hb_all[8:8 + TM], main_all[8:8 + TM]
    prev_ok, next_ok = _seq_edges(i, nct, ntiles)
    x_prev = main_all[7:8, 0:2048] * prev_ok
    x_next = main_all[8 + TM:9 + TM, 0:2048] * next_ok
    act = _conv_silu(main[:, 0:2048], x_prev, x_next, cw_ref[...], cb_ref[...])
    q_ref[...] = act[:, 0:512].astype(BF16)
    kt_ref[...] = jnp.transpose(act[:, 512:1024] * 0.125).astype(BF16)
    v_ref[...] = act[:, 1024:2048].astype(BF16)
    o_ref[...] = main[:, 2048:3072].astype(BF16)
    h_lo = (h - hb.astype(F32)).astype(BF16)
    gates = _hi_lo_cols(_dot(hb, wg_ref[...]), _dot(h_lo, wgh_ref[...]), 32) + gb_ref[...]
    lane = lax.broadcasted_iota(I32, gates.shape, 1)
    gates = jnp.where(lane < 16, gates, _log_sigmoid(gates))
    gc_ref[...] = gates[:, 0:32]
    gr_ref[...] = jnp.transpose(gates)[0:32, :]


def _in1(x1, mod, p, nct):
    nt = x1.shape[0]
    ntiles = nt // TM
    tile = lambda w: pl.BlockSpec((TM, w), lambda i: (i, 0))
    return pl.pallas_call(
        functools.partial(_in1_kernel, nct, ntiles),
        grid=(ntiles,),
        in_specs=_halo_specs(nt, lambda i: i) + [
            _mod_spec(nct), _full((1, D)), _full((D, 3072)), _full((D, 128)), _full((D, 128)),
            _full((3, 2048)), _full((1, 2048)), _full((1, 128))],
        out_specs=[tile(512), pl.BlockSpec((512, TM), lambda i: (0, i)), tile(1024), tile(1024),
                   tile(32), pl.BlockSpec((32, TM), lambda i: (0, i))],
        out_shape=[jax.ShapeDtypeStruct((nt, 512), BF16), jax.ShapeDtypeStruct((512, nt), BF16),
                   jax.ShapeDtypeStruct((nt, 1024), BF16), jax.ShapeDtypeStruct((nt, 1024), BF16),
                   jax.ShapeDtypeStruct((nt, 32), F32), jax.ShapeDtypeStruct((32, nt), F32)],
        compiler_params=_cparams("arbitrary"),
        name="in_proj_odd",
    )(x1, x1, x1, mod, p["g_mix"], p["wcat"], p["wg"], p["wg_hi"], p["conv_w"], p["conv_b"], p["gate_bias"])


def _mlstm_kernel(rev, *refs):
    if rev:
        (q_ref, kt_ref, v_ref, gc_ref, gr_ref, hf_ref, og_ref, hn_ref, o_ref,
         c_ref, mc_ref, mr_ref) = refs
    else:
        q_ref, kt_ref, v_ref, gc_ref, gr_ref, o_ref, c_ref, mc_ref, mr_ref = refs
    j = pl.program_id(0)

    @pl.when(j == 0)
    def _():
        c_ref[...] = jnp.zeros_like(c_ref)
        mc_ref[...] = jnp.zeros_like(mc_ref)
        mr_ref[...] = jnp.zeros_like(mr_ref)

    d = 8 if rev else 0
    end = 0 if rev else T - 1
    order = tuple(reversed(range(SCAN_CHUNKS))) if rev else tuple(range(SCAN_CHUNKS))
    ones_b = jnp.ones((T, 128), BF16)
    sub = lax.broadcasted_iota(I32, (128, T), 0)
    zero_k = jnp.zeros((128, T), BF16)

    def prologue(c):
        rows = slice(c * T, (c + 1) * T)
        ig_col = gc_ref[rows, d:d + 8]
        lf_col = gc_ref[rows, 16 + d:24 + d]
        ig_row = gr_ref[d:d + 8, rows]
        lf_row = gr_ref[16 + d:24 + d, rows]
        tri, b_col, b_row = _cumsums(rev, lf_col, lf_row)
        blast_row = b_col[end:end + 1, :]
        blast_col = b_row[:, end:end + 1]
        wend_row = blast_col - b_row + ig_row
        ac_col = jnp.max(wend_row, axis=1, keepdims=True)
        return dict(tri=tri, b_row=b_row, ig_row=ig_row, blast_row=blast_row, blast_col=blast_col,
                    ac_col=ac_col, eend_row=jnp.exp(wend_row - ac_col),
                    ac_row=jnp.max(blast_row - b_col + ig_col, axis=0, keepdims=True),
                    b_bc=_lane_bcast(b_col), q=q_ref[rows, :])

    pro = {c: prologue(c) for c in order}
    m_col = mc_ref[:, 0:1]
    m_row = mr_ref[0:1, 0:8]
    for c in order:
        p = pro[c]
        mnew_col = jnp.maximum(p["blast_col"] + m_col, p["ac_col"])
        p["sp_col"] = jnp.exp(p["blast_col"] + m_col - mnew_col)
        p["sc_col"] = jnp.exp(p["ac_col"] - mnew_col)
        p["m_row"] = m_row
        m_col, m_row = mnew_col, jnp.maximum(p["blast_row"] + m_row, p["ac_row"])

    def head_matmuls(c, hd):
        p = pro[c]
        rows = slice(c * T, (c + 1) * T)
        pair, hi = hd // 2, hd % 2
        qp = p["q"][:, 128 * pair:128 * (pair + 1)]
        ktp = kt_ref[128 * pair:128 * (pair + 1), rows]
        kth = jnp.where((sub >= 64) if hi else (sub < 64), ktp, zero_k)
        vaug = jnp.concatenate([v_ref[rows, 128 * hd:128 * (hd + 1)], ones_b], axis=1)
        cst = c_ref[hd]
        sqk = _dot(qp, kth)
        inter = _dot(qp, cst.astype(BF16))
        kte = (kth.astype(F32) * p["eend_row"][hd:hd + 1, :]).astype(BF16)
        c_ref[hd] = p["sp_col"][hd:hd + 1, :] * cst + p["sc_col"][hd:hd + 1, :] * _dot(kte, vaug)
        return sqk, inter, vaug

    items = [(c, hd) for c in order for hd in range(8)]
    nxt = head_matmuls(*items[0])
    for n, (c, hd) in enumerate(items):
        p = pro[c]
        rows = slice(c * T, (c + 1) * T)
        cols = slice(128 * hd, 128 * (hd + 1))
        sqk, inter, vaug = nxt
        if n + 1 < len(items):
            nxt = head_matmuls(*items[n + 1])
        bh = p["b_bc"][hd]
        dlog = jnp.where(p["tri"], bh - p["b_row"][hd:hd + 1, :] + p["ig_row"][hd:hd + 1, :], NEG_INF)
        gh = bh + p["m_row"][:, hd:hd + 1]
        mstar = jnp.maximum(gh, jnp.max(dlog, axis=-1, keepdims=True))
        w = (jnp.exp(dlog - mstar) * sqk).astype(BF16)
        intra = _dot(w, vaug)
        e_int = jnp.exp(gh - mstar)
        den = jnp.maximum(jnp.abs(intra[:, 128:256] + e_int * inter[:, 128:256]), jnp.exp(-mstar))
        hh = (intra[:, 0:128] + e_int * inter[:, 0:128]) / den
        if rev:
            hh = hh + hf_ref[rows, cols]
            ms = jnp.mean(hh * hh, axis=-1, keepdims=True)
            hh = hh * lax.rsqrt(ms + EPS) * hn_ref[:, cols]
            og = og_ref[rows, cols].astype(F32)
            o_ref[rows, cols] = (hh * (0.5 * jnp.tanh(0.5 * og) + 0.5)).astype(BF16)
        else:
            o_ref[rows, cols] = hh
    mc_ref[...] = jnp.broadcast_to(m_col, mc_ref.shape)
    mr_ref[...] = jnp.broadcast_to(jnp.concatenate([m_row, jnp.zeros((1, 120), F32)], axis=1), mr_ref.shape)


def _mlstm(rev, q, kt, v, gc, gr, p, ncc, nlc, hf=None, og=None):
    nt = q.shape[0]
    assert ncc % SCAN_CHUNKS == 0 and nlc % SCAN_CHUNKS == 0
    ncc, nlc = ncc // SCAN_CHUNKS, nlc // SCAN_CHUNKS
    rows = SCAN_CHUNKS * T
    cmap = _scan_chunk_map(rev, ncc, nlc)
    blk = lambda w: pl.BlockSpec((rows, w), lambda j: (cmap(j), 0))
    blk_t = lambda h: pl.BlockSpec((h, rows), lambda j: (0, cmap(j)))
    in_specs = [blk(512), blk_t(512), blk(1024), blk(32), blk_t(32)]
    args = [q, kt, v, gc, gr]
    if rev:
        in_specs += [blk(1024), blk(1024), _full((1, 1024))]
        args += [hf, og, p["head_norm"]]
    return pl.pallas_call(
        functools.partial(_mlstm_kernel, rev),
        grid=(ncc + nlc,),
        in_specs=in_specs,
        out_specs=blk(1024),
        out_shape=jax.ShapeDtypeStruct((nt, 1024), BF16 if rev else F32),
        scratch_shapes=[pltpu.VMEM((8, 128, 256), F32), pltpu.VMEM((8, 128), F32), pltpu.VMEM((8, 128), F32)],
        compiler_params=_cparams("arbitrary"),
        name="mlstm_bwd" if rev else "mlstm_fwd",
    )(*args)


def _route(logits_t, rb_col):
    scores = _sigmoid(logits_t)
    biased = scores + rb_col
    row = lambda a, e: a[e:e + 1, :]
    gscore = []
    for g in range(4):
        b0, b1, b2, b3 = (row(biased, 4 * g + e) for e in range(4))
        h1, l1 = jnp.maximum(b0, b1), jnp.minimum(b0, b1)
        h2, l2 = jnp.maximum(b2, b3), jnp.minimum(b2, b3)
        gscore.append(jnp.maximum(h1, h2) + jnp.maximum(jnp.minimum(h1, h2), jnp.maximum(l1, l2)))
    gidx = jnp.zeros_like(gscore[0], dtype=I32)
    best = gscore[0]
    for g in range(1, 4):
        better = gscore[g] > best
        gidx = jnp.where(better, g, gidx)
        best = jnp.where(better, gscore[g], best)

    def pick(a, e):
        out = row(a, e)
        for g in range(1, 4):
            out = jnp.where(gidx == g, row(a, 4 * g + e), out)
        return out

    sb = [pick(biased, e) for e in range(4)]
    i1 = jnp.zeros_like(gidx)
    v1 = sb[0]
    for e in range(1, 4):
        better = sb[e] > v1
        i1 = jnp.where(better, e, i1)
        v1 = jnp.where(better, sb[e], v1)
    i2 = jnp.zeros_like(gidx)
    v2 = jnp.full_like(v1, NEG_INF)
    for e in range(4):
        better = jnp.logical_and(i1 != e, sb[e] > v2)
        i2 = jnp.where(better, e, i2)
        v2 = jnp.where(better, sb[e], v2)
    a = jnp.minimum(i1, i2)
    b = jnp.maximum(i1, i2)
    pair = jnp.where(a == 0, jnp.where(b == 1, 0, jnp.where(b == 2, 2, 3)), jnp.where(a == 1, jnp.where(b == 2, 1, 4), 5))
    return 6 * gidx + pair


def _out_kernel(nmix, nct, two_src, sub, t0, *refs):
    per = nmix + (2 if two_src else 1)
    tile_refs = [refs[per * u:per * (u + 1)] for u in range(sub)]
    (w_ref, mod_ref, g_ref, rw_ref, rwh_ref, rb_ref,
     xmid_ref, hrow_ref, bucket_ref, rank_ref, cnt_ref, cnt_scr) = refs[per * sub:]
    i = pl.program_id(0)

    @pl.when(i == 0)
    def _():
        cnt_scr[...] = jnp.zeros_like(cnt_scr)

    brow = lax.broadcasted_iota(I32, (NB_PAD, TM), 0)
    r = lax.broadcasted_iota(I32, (TM, TM), 0)
    c = lax.broadcasted_iota(I32, (TM, TM), 1)
    before = (r < c).astype(F32).astype(BF16)
    onehots = []
    for u in range(sub):
        mix_refs = tile_refs[u][:nmix]
        is_ctx = sub * i + u + t0 < nct
        mod = jnp.where(is_ctx, mod_ref[1], mod_ref[0])
        mix = mix_refs[0][...] if nmix == 1 else jnp.concatenate([mr[...] for mr in mix_refs], axis=1)
        if two_src:
            x = jnp.where(is_ctx, tile_refs[u][nmix][...], tile_refs[u][nmix + 1][...])
        else:
            x = tile_refs[u][nmix][...]
        x_mid = x + mod[2:3, :] * _dot(mix, w_ref[...])
        xmid_ref[TM * u:TM * (u + 1), :] = x_mid
        h = _norm_mod(x_mid, g_ref[...], mod[4:5, :], mod[3:4, :])
        hb = h.astype(BF16)
        h_lo = (h - hb.astype(F32)).astype(BF16)
        logits = _hi_lo_cols(_dot(hb, rw_ref[...]), _dot(h_lo, rwh_ref[...]), 16)
        logits_t = jnp.transpose(logits)[0:16, :]
        bucket = _route(logits_t, rb_ref[...])
        hrow_u = hrow_ref.at[pl.ds(TM * TOK * u, TM * TOK), :]
        for cblk in range(TOK):
            _tok_store(hrow_u, cblk, h[:, 128 * cblk:128 * (cblk + 1)])
        bucket_ref[u] = bucket
        onehots.append((brow == bucket).astype(F32))
    cnt = cnt_scr[...]
    for u in range(sub):
        onehot = onehots[u]
        cum = _dot(onehot.astype(BF16), before)
        rank_ref[u] = jnp.sum(onehot * (cum + cnt[:, 0:1]), axis=0, keepdims=True).astype(I32)
        cnt = cnt + jnp.sum(onehot, axis=1, keepdims=True)
    cnt_scr[...] = cnt
    cnt_ref[...] = cnt.astype(I32)


def _out_proj(mixes, w_out, xs, mod, g_ffn, rw, rw_hi, rb_col, t0, ntiles, nct):
    nmix = len(mixes)
    two_src = len(xs) == 2
    n = ntiles * TM
    sub = max(s for s in (5, 4, 3, 2, 1) if ntiles % s == 0)
    in_specs, args = [], []
    for u in range(sub):
        tile_of = lambda i, u=u: sub * i + u + t0
        for mx in mixes:
            in_specs.append(pl.BlockSpec((TM, mx.shape[1]), lambda i, f=tile_of: (f(i), 0)))
            args.append(mx)
        if two_src:
            in_specs += [pl.BlockSpec((TM, D), lambda i, f=tile_of: (jnp.minimum(f(i), nct - 1), 0)),
                         pl.BlockSpec((TM, D), lambda i, f=tile_of: (jnp.maximum(f(i) - nct, 0), 0))]
        else:
            in_specs.append(pl.BlockSpec((TM, D), lambda i, f=tile_of: (f(i), 0)))
        args += list(xs)
    in_specs += [_full((D, D)), _full((2, 6, D)), _full((1, D)), _full((D, 128)), _full((D, 128)), _full((16, 1))]
    rows_out = pl.BlockSpec((sub, 1, TM), lambda i: (i, 0, 0))
    return pl.pallas_call(
        functools.partial(_out_kernel, nmix, nct, two_src, sub, t0),
        grid=(ntiles // sub,),
        in_specs=in_specs,
        out_specs=[pl.BlockSpec((sub * TM, D), lambda i: (i, 0)),
                   pl.BlockSpec((sub * TM * TOK, 128), lambda i: (i, 0)), rows_out, rows_out,
                   _full((NB_PAD, 128))],
        out_shape=[jax.ShapeDtypeStruct((n, D), F32), jax.ShapeDtypeStruct((n * TOK, 128), F32),
                   jax.ShapeDtypeStruct((ntiles, 1, TM), I32), jax.ShapeDtypeStruct((ntiles, 1, TM), I32),
                   jax.ShapeDtypeStruct((NB_PAD, 128), I32)],
        scratch_shapes=[pltpu.VMEM((NB_PAD, 128), F32)],
        compiler_params=_cparams("arbitrary"),
        name="out_proj_router",
    )(*args, w_out, mod, g_ffn, rw, rw_hi, rb_col)


def _scatter_kernel(ntiles, pos_ref, flo_ref, fhi_ref, src_ref, dst_ref, hbuf, in_sem, out_sem):
    i = pl.program_id(0)
    slot = i % 3
    h_ref = hbuf.at[slot]
    sem = out_sem.at[slot]
    rows = TM * TOK

    def load(tile, s):
        return pltpu.make_async_copy(src_ref.at[pl.ds(pl.multiple_of(tile * rows, rows), rows), :],
                                     hbuf.at[s], in_sem.at[s])

    def tok(ref, t):
        return ref.at[pl.ds(pl.multiple_of(t * TOK, TOK), TOK), :]

    def copy(r, d_row, src=h_ref, sm=sem):
        return pltpu.make_async_copy(tok(src, r), tok(dst_ref, d_row), sm)

    def wait_rows(lo, hi, unroll, s=slot):
        def body(r, carry):
            copy(0, 0, hbuf.at[s], out_sem.at[s]).wait()
            return carry
        lax.fori_loop(lo, hi, body, 0, unroll=unroll)

    @pl.when(i == 0)
    def _():
        load(0, 0).start()
        if ntiles > 1:
            load(1, 1).start()

    load(i, slot).wait()

    def start(r8, c):
        for k in range(8):
            r = r8 * 8 + k
            copy(r, pos_ref[i * TM + r]).start(priority=k % 2)
        return c
    lax.fori_loop(0, TM // 8, start, 0)

    @pl.when(i > 0)
    def _():
        wait_rows(0, TM, 8, (i + 2) % 3)

    @pl.when(i + 2 < ntiles)
    def _():
        load(i + 2, (i + 2) % 3).start()

    @pl.when(i == ntiles - 1)
    def _():
        wait_rows(0, TM, 8)

        def pad_copies(b, act):
            off, left = flo_ref[b], fhi_ref[b] - flo_ref[b]
            for bit in reversed(range(TMM.bit_length() - 1)):
                k = 1 << bit
                take = (left & k) != 0

                @pl.when(take)
                def _(off=off, k=k):
                    act(pltpu.make_async_copy(
                        h_ref.at[pl.ds(0, k * TOK), :],
                        dst_ref.at[pl.ds(pl.multiple_of(off * TOK, TOK), k * TOK), :], sem))
                off = off + jnp.where(take, k, 0)

        def fill(b, c):
            pad_copies(b, lambda cp: cp.start())
            return c

        def drain(b, c):
            pad_copies(b, lambda cp: cp.wait())
            return c
        lax.fori_loop(0, N_BUCKETS, fill, 0)
        lax.fori_loop(0, N_BUCKETS, drain, 0)

        def tile_copy(j):
            rows = TMM * TOK
            return pltpu.make_async_copy(h_ref, dst_ref.at[pl.ds(pl.multiple_of(j * rows, rows), rows), :], sem)

        def fill_tile(j, c):
            tile_copy(j).start()
            return c

        def wait_tile(j, c):
            tile_copy(j).wait()
            return c
        lax.fori_loop(flo_ref[N_BUCKETS], fhi_ref[N_BUCKETS], fill_tile, 0)
        lax.fori_loop(flo_ref[N_BUCKETS], fhi_ref[N_BUCKETS], wait_tile, 0)


def _scatter_rows(hrow, pos, fill_lo, fill_hi, n, npad):
    assert TM == TMM
    return pl.pallas_call(
        functools.partial(_scatter_kernel, n // TM),
        grid_spec=pltpu.PrefetchScalarGridSpec(
            num_scalar_prefetch=3,
            grid=(n // TM,),
            in_specs=[pl.BlockSpec(memory_space=pl.ANY)],
            out_specs=pl.BlockSpec(memory_space=pl.ANY),
            scratch_shapes=[pltpu.VMEM((3, TM * TOK, 128), hrow.dtype), pltpu.SemaphoreType.DMA((3,)),
                            pltpu.SemaphoreType.DMA((3,))],
        ),
        out_shape=jax.ShapeDtypeStruct((npad * TOK, 128), hrow.dtype),
        compiler_params=_cparams("arbitrary"),
        name="moe_scatter_rows",
    )(pos, fill_lo, fill_hi, hrow)


def _combine_kernel(ntiles, pos_ref, x_ref, mod_ref, ys_ref, o_ref, ybuf, sem):
    i = pl.program_id(0)

    def copy(tile, slot, r):
        src = pl.multiple_of(pos_ref[tile * TM + r] * TOK, TOK)
        dst = pl.multiple_of(r * TOK, TOK)
        return pltpu.make_async_copy(ys_ref.at[pl.ds(src, TOK), :], ybuf.at[slot, pl.ds(dst, TOK), :], sem.at[slot])

    def start_tile(tile, slot):
        def body(r8, carry):
            for k in range(8):
                copy(tile, slot, r8 * 8 + k).start(priority=k % 2)
            return carry
        lax.fori_loop(0, TM // 8, body, 0)

    @pl.when(i == 0)
    def _():
        start_tile(0, 0)

    @pl.when(i + 1 < ntiles)
    def _():
        start_tile(i + 1, (i + 1) % 2)

    slot = i % 2

    def wait_body(r, carry):
        copy(i, slot, 0).wait()
        return carry
    lax.fori_loop(0, TM, wait_body, 0, unroll=8)
    o_ref[...] = x_ref[...] + mod_ref[0, 5:6, :] * _tok_rows(ybuf.at[slot], TM)


def _combine(x_mid, mod, ys, pos, t0, nct):
    n = x_mid.shape[0]
    ntiles = n // TM
    return pl.pallas_call(
        functools.partial(_combine_kernel, ntiles),
        grid_spec=pltpu.PrefetchScalarGridSpec(
            num_scalar_prefetch=1,
            grid=(ntiles,),
            in_specs=[pl.BlockSpec((TM, D), lambda i, *_: (i, 0)),
                      pl.BlockSpec((1, 6, D), lambda i, *_: (jnp.where(i + t0 < nct, 1, 0), 0, 0)),
                      pl.BlockSpec(memory_space=pl.ANY)],
            out_specs=pl.BlockSpec((TM, D), lambda i, *_: (i, 0)),
            scratch_shapes=[pltpu.VMEM((2, TM * TOK, 128), F32), pltpu.SemaphoreType.DMA((2,))],
        ),
        out_shape=jax.ShapeDtypeStruct((n, D), F32),
        compiler_params=_cparams("arbitrary"),
        name="moe_combine",
    )(pos, x_mid, mod, ys)


def _moe_kernel(tea_ref, teb_ref, tblk_ref, tval_ref, x_ref, rw_ref, w1a_ref, w3a_ref, w2a_ref,
                w1b_ref, w3b_ref, w2b_ref, y_ref):
    del tblk_ref
    j = pl.program_id(0)

    @pl.when(tval_ref[j] != 0)
    def _():
        h = _tok_rows(x_ref, TMM)
        hb = h.astype(BF16)
        logits = _hi_lo_cols(_dot(hb, rw_ref[...]), 0.0, 16)
        scores = _sigmoid(logits)
        lane = lax.broadcasted_iota(I32, scores.shape, 1)
        s_a = jnp.sum(jnp.where(lane == tea_ref[j], scores, 0.0), axis=1, keepdims=True)
        s_b = jnp.sum(jnp.where(lane == teb_ref[j], scores, 0.0), axis=1, keepdims=True)
        gates = (s_a / (s_a + s_b), s_b / (s_a + s_b))
        acts = []
        for w1_ref, w3_ref, gate in ((w1a_ref, w3a_ref, gates[0]), (w1b_ref, w3b_ref, gates[1])):
            u = _dot(hb, w1_ref[0, 0].astype(BF16))
            v = _dot(hb, w3_ref[0, 0].astype(BF16))
            acts.append((_silu(u) * v * gate).astype(BF16))
        y = _dot(acts[0], w2a_ref[0, 0].astype(BF16)) + _dot(acts[1], w2b_ref[0, 0].astype(BF16))
        for cblk in range(TOK):
            _tok_store(y_ref, cblk, y[:, 128 * cblk:128 * (cblk + 1)])

    @pl.when(tval_ref[j] == 0)
    def _():
        y_ref[...] = jnp.zeros_like(y_ref)


def _moe(xs_sorted, rw, w1, w3, w2, layer, tile_ea, tile_eb, tile_blk, tile_valid):
    npad = xs_sorted.shape[0] // TOK
    ntile = npad // TMM
    wspec = lambda shape, which: pl.BlockSpec(
        (1, 1) + shape, lambda j, ea, eb, blk, val: (layer, (ea, eb)[which][j], 0, 0))
    up, down = (D, D_EXPERT), (D_EXPERT, D)
    return pl.pallas_call(
        _moe_kernel,
        grid_spec=pltpu.PrefetchScalarGridSpec(
            num_scalar_prefetch=4,
            grid=(ntile,),
            in_specs=[pl.BlockSpec((TMM * TOK, 128), lambda j, ea, eb, blk, val: (blk[j], 0)),
                      pl.BlockSpec((D, 128), lambda j, *_: (0, 0)),
                      wspec(up, 0), wspec(up, 0), wspec(down, 0), wspec(up, 1), wspec(up, 1), wspec(down, 1)],
            out_specs=pl.BlockSpec((TMM * TOK, 128), lambda j, ea, eb, blk, val: (j, 0)),
        ),
        out_shape=jax.ShapeDtypeStruct((npad * TOK, 128), F32),
        compiler_params=_cparams("arbitrary"),
        name="moe_experts",
    )(tile_ea, tile_eb, tile_blk, tile_valid, xs_sorted, rw, w1, w3, w2, w1, w3, w2)


def _moe_block(x_mid, hrow, bucket, rank, counts, mod, rw, w1, w3, w2, layer, t0, nct):
    n = hrow.shape[0] // TOK
    ntile = n // TMM + N_BUCKETS
    npad = ntile * TMM
    cnt = counts[:N_BUCKETS, 0]
    padded = ((cnt + TMM - 1) // TMM) * TMM
    ends = jnp.cumsum(padded)
    starts = ends - padded
    total_tiles = ends[-1] // TMM
    tiles = jnp.arange(ntile, dtype=I32)
    tile_valid = (tiles < total_tiles).astype(I32)
    tile_blk = jnp.minimum(tiles, jnp.maximum(total_tiles - 1, 0))
    tile_bucket = jnp.minimum(jnp.sum((ends[None, :] <= (tile_blk * TMM)[:, None]).astype(I32), axis=1), N_BUCKETS - 1)
    pair = tile_bucket % 6
    grp = tile_bucket // 6
    slot_a = sum(jnp.where(pair == k, e, 0) for k, (e, _) in enumerate(_PAIR_SLOTS))
    slot_b = sum(jnp.where(pair == k, e, 0) for k, (_, e) in enumerate(_PAIR_SLOTS))
    tile_ea = (4 * grp + slot_a).astype(I32)
    tile_eb = (4 * grp + slot_b).astype(I32)
    bucket = bucket.reshape(-1)
    onehot = (bucket[:, None] == jnp.arange(N_BUCKETS, dtype=I32)[None, :]).astype(I32)
    pos = (rank.reshape(-1) + jnp.sum(onehot * starts[None, :], axis=1)).astype(I32)
    pad32 = lambda a, tail: jnp.zeros((NB_PAD,), I32).at[:N_BUCKETS].set(a.astype(I32)).at[N_BUCKETS].set(tail)
    fill_lo, fill_hi = pad32(starts + cnt, total_tiles), pad32(ends, ntile)
    xs_sorted = _scatter_rows(hrow, pos, fill_lo, fill_hi, n, npad)
    ys = _moe(xs_sorted, rw, w1, w3, w2, layer, tile_ea, tile_eb, tile_blk, tile_valid)
    return _combine(x_mid, mod, ys, pos, t0, nct)


def kernel(x, c, ctx, c_ctx, router_w, router_b, norm_mix, norm_ffn, w_mod, b_mod, ev_w_in, ev_conv_w, ev_conv_b, ev_dt_bias, ev_a_log, ev_d_skip, ev_ssd_norm, ev_q_norm, ev_k_norm, ev_sink, ev_w_out, od_w_in, od_conv_w, od_conv_b, od_igate_b, od_fgate_b, od_head_norm, od_w_out, moe_w1, moe_w3, moe_w2):
    s_len = x.shape[1]
    c_len = ctx.shape[1]
    assert x.shape[0] == 1 and s_len % TM == 0 and c_len % TM == 0 and s_len % GRID_W == 0
    nct = c_len // TM
    ncc, nlc = c_len // T, s_len // T
    nt = c_len + s_len
    ntiles = nt // TM

    mod = _modulation(c, c_ctx, w_mod, b_mod)
    rw, rw_hi = _hi_lo_weight(router_w)
    rb_col = router_b.reshape(N_EXPERTS, 1)
    pad128 = lambda v: jnp.zeros((1, 128), F32).at[0, :v.shape[0]].set(v)

    w = ev_w_in[0]
    rope_rows, rope_cols = _rope_tables(s_len)
    wdt, wdt_hi = _hi_lo_weight(w[:, 1280:1296])
    p0 = dict(
        g_mix=norm_mix[0].reshape(1, D),
        w_zx=w[:, 0:1280].astype(BF16), w_qkv=w[:, 1296:2064].astype(BF16), wdt=wdt, wdt_hi=wdt_hi,
        conv_w=ev_conv_w[0], conv_b=ev_conv_b[0].reshape(1, 768),
        dt_bias=pad128(ev_dt_bias[0].reshape(16)),
        q_norm=jnp.tile(ev_q_norm[0], 8).reshape(1, 512), k_norm=jnp.tile(ev_k_norm[0], 2).reshape(1, 128),
        rope_rows=rope_rows, rope_cols=rope_cols,
        alog_row=ev_a_log[0].reshape(1, 16), alog_col=ev_a_log[0].reshape(16, 1),
        d_skip=jnp.repeat(ev_d_skip[0], 64).reshape(1, 512), ssd_norm=ev_ssd_norm[0].reshape(1, 512))
    z, xs, bc, q, kk, vv, dtc, dtr = _in0(ctx[0], x[0], mod[0], p0, nct)
    yf = _ssd(False, xs, bc, dtc, dtr, p0, ncc, nlc)
    ymix = _ssd(True, xs, bc, dtc, dtr, p0, ncc, nlc, yf=yf, z=z)
    att = _attention(q, kk, vv, ev_sink[0].reshape(1, 8), ncc, nlc)
    x_mid0, hrow, bucket, rank, counts = _out_proj(
        [ymix, att], ev_w_out[0].astype(BF16), [ctx[0], x[0]], mod[0], norm_ffn[0].reshape(1, D),
        rw, rw_hi, rb_col, 0, ntiles, nct)
    x1 = _moe_block(x_mid0, hrow, bucket, rank, counts, mod[0], rw, moe_w1, moe_w3, moe_w2, 0, 0, nct)

    w = od_w_in[0]
    wg, wg_hi = _hi_lo_weight(w[:, 3072:3104])
    p1 = dict(
        g_mix=norm_mix[1].reshape(1, D),
        wcat=w[:, 0:3072].astype(BF16), wg=wg, wg_hi=wg_hi,
        conv_w=od_conv_w[0], conv_b=od_conv_b[0].reshape(1, 2048),
        gate_bias=pad128(jnp.concatenate([od_igate_b[0].reshape(16), od_fgate_b[0].reshape(16)])),
        head_norm=od_head_norm[0].reshape(1, 1024))
    q1, kt1, v1, og1, gc1, gr1 = _in1(x1, mod[1], p1, nct)
    hf = _mlstm(False, q1, kt1, v1, gc1, gr1, p1, ncc, nlc)
    hmix = _mlstm(True, q1, kt1, v1, gc1, gr1, p1, ncc, nlc, hf=hf, og=og1)
    x_mid1, hrow, bucket, rank, counts = _out_proj(
        [hmix], od_w_out[0].astype(BF16), [x1], mod[1], norm_ffn[1].reshape(1, D),
        rw, rw_hi, rb_col, nct, ntiles - nct, nct)
    return _moe_block(x_mid1, hrow, bucket, rank, counts, mod[1], rw, moe_w1, moe_w3, moe_w2, 1, nct, nct)[None]
```
